```python
import math
import jax, jax.numpy as jnp
from jax import lax
import numpy as np

D_MODEL = 1024
BATCH = 8
SEQ = 4096
DEPTH = 1

SSM_EXPAND = 2
SSM_D_INNER = SSM_EXPAND * D_MODEL
SSM_HEAD_DIM = 64
SSM_HEADS = SSM_D_INNER // SSM_HEAD_DIM
SSM_GROUPS = 4
SSM_HEADS_PER_GROUP = SSM_HEADS // SSM_GROUPS
SSM_D_STATE = 128
SSM_CONV = 4
SSM_CHUNK = 128
SSM_BC_DIM = SSM_GROUPS * SSM_D_STATE
SSM_CONV_DIM = SSM_D_INNER + 2 * SSM_BC_DIM
SSM_NORM_GROUP = SSM_D_INNER // SSM_GROUPS
SSM_DT_MIN = 0.001
SSM_DT_MAX = 0.1

LRU_WIDTH = 1280
LRU_BLOCKS = 10
LRU_BLOCK = LRU_WIDTH // LRU_BLOCKS
LRU_CONV = 4
LRU_C = 8.0

FFN_HIDDEN = -(-8 * D_MODEL // (3 * 256)) * 256

RMS_EPS = 1e-6

N_GATES = 2 * D_MODEL
IN_PROJ_DIM = N_GATES + SSM_D_INNER + SSM_CONV_DIM + SSM_HEADS + 2 * LRU_WIDTH
IN_SPLITS = (
    N_GATES,
    N_GATES + SSM_D_INNER,
    N_GATES + SSM_D_INNER + SSM_CONV_DIM,
    N_GATES + SSM_D_INNER + SSM_CONV_DIM + SSM_HEADS,
    N_GATES + SSM_D_INNER + SSM_CONV_DIM + SSM_HEADS + LRU_WIDTH,
)

kernel_name = "hybrid_ssd_rglru_gated_block"


def rmsnorm(x, w, eps=RMS_EPS):
    xf = x.astype(jnp.float32)
    y = xf * lax.rsqrt(jnp.mean(xf * xf, axis=-1, keepdims=True) + eps)
    return (y * w.astype(jnp.float32)).astype(x.dtype)


def causal_dwconv(u, w, b):
    k_width = w.shape[0]
    length = u.shape[1]
    up = jnp.pad(u, ((0, 0), (k_width - 1, 0), (0, 0)))
    out = b + up[:, 0:length] * w[0]
    for k in range(1, k_width):
        out = out + up[:, k:k + length] * w[k]
    return out


def segsum_exp(cs):
    t = cs.shape[-1]
    mask = jnp.tril(jnp.ones((t, t), dtype=bool))
    diff = cs[..., :, None] - cs[..., None, :]
    return jnp.exp(jnp.where(mask, diff, -jnp.inf))


def ssd_chunked(xdt, dA, bm, cm):
    b, length, g, r, p = xdt.shape
    n = bm.shape[-1]
    l = SSM_CHUNK
    c = length // l
    xdt = xdt.reshape(b, c, l, g, r, p)
    dA = dA.reshape(b, c, l, g, r)
    bm = bm.reshape(b, c, l, g, n)
    cm = cm.reshape(b, c, l, g, n)

    cs = jnp.cumsum(dA, axis=2)
    lmat = segsum_exp(jnp.moveaxis(cs, 2, -1))
    cb = jnp.einsum('bclgn,bcsgn->bcgls', cm, bm)
    y_diag = jnp.einsum('bcgls,bcgrls,bcsgrp->bclgrp', cb, lmat, xdt)
    decay_states = jnp.exp(cs[:, :, -1:] - cs)
    states = jnp.einsum('bclgn,bclgr,bclgrp->bcgrpn', bm, decay_states, xdt)
    chunk_tot = jnp.moveaxis(cs[:, :, -1], 1, -1)
    chunk_cs = jnp.cumsum(jnp.pad(chunk_tot, ((0, 0), (0, 0), (0, 0), (1, 0))), axis=-1)
    decay_chunk = segsum_exp(chunk_cs)
    states = jnp.concatenate([jnp.zeros_like(states[:, :1]), states], axis=1)
    states_in = jnp.einsum('bgrzc,bcgrpn->bzgrpn', decay_chunk[..., :-1, :], states)
    y_off = jnp.einsum('bclgn,bcgrpn,bclgr->bclgrp', cm, states_in, jnp.exp(cs))
    return (y_diag + y_off).reshape(b, length, g, r, p)


def mamba2_mixer(z, xbc, dt_raw, conv_w, conv_b, dt_bias, a_log, d_skip, norm_w):
    b, length, _ = z.shape
    f32 = jnp.float32
    xbc = jax.nn.silu(causal_dwconv(xbc, conv_w, conv_b))
    xs, bm, cm = jnp.split(xbc, [SSM_D_INNER, SSM_D_INNER + SSM_BC_DIM], axis=-1)
    dt = jax.nn.softplus(dt_raw.astype(f32) + dt_bias.astype(f32))
    a = -jnp.exp(a_log.astype(f32))
    g, r, p, n = SSM_GROUPS, SSM_HEADS_PER_GROUP, SSM_HEAD_DIM, SSM_D_STATE
    xs_h = xs.astype(f32).reshape(b, length, g, r, p)
    dt_h = dt.reshape(b, length, g, r)
    y = ssd_chunked(xs_h * dt_h[..., None], dt_h * a.reshape(g, r),
                    bm.astype(f32).reshape(b, length, g, n),
                    cm.astype(f32).reshape(b, length, g, n))
    y = y + xs_h * d_skip.astype(f32).reshape(g, r, 1)
    y = y.reshape(b, length, SSM_D_INNER) * jax.nn.silu(z.astype(f32))
    yg = y.reshape(b, length, SSM_GROUPS, SSM_NORM_GROUP)
    yg = yg * lax.rsqrt(jnp.mean(yg * yg, axis=-1, keepdims=True) + RMS_EPS)
    y = yg.reshape(b, length, SSM_D_INNER) * norm_w.astype(f32)
    return y.astype(z.dtype)


def rglru_mixer(xl, yl, conv_w, conv_b, w_r, b_r, w_i, b_i, lam):
    b, length, _ = xl.shape
    f32 = jnp.float32
    u = causal_dwconv(xl, conv_w, conv_b)
    ub = u.reshape(b, length, LRU_BLOCKS, LRU_BLOCK)
    r_gate = jax.nn.sigmoid((jnp.einsum('blhi,hij->blhj', ub, w_r).reshape(b, length, LRU_WIDTH) + b_r).astype(f32))
    i_gate = jax.nn.sigmoid((jnp.einsum('blhi,hij->blhj', ub, w_i).reshape(b, length, LRU_WIDTH) + b_i).astype(f32))
    log_a = -LRU_C * r_gate * jax.nn.softplus(-lam.astype(f32))
    a_t = jnp.exp(log_a)
    b_t = jnp.sqrt(-jnp.expm1(2.0 * log_a)) * (i_gate * u.astype(f32))

    def combine(left, right):
        a1, b1 = left
        a2, b2 = right
        return a1 * a2, a2 * b1 + b2

    _, h = lax.associative_scan(combine, (a_t, b_t), axis=1)
    out = h * jax.nn.gelu(yl.astype(f32), approximate=True)
    return out.astype(xl.dtype)


def _fwd_setup_inputs(seed: int = 0) -> dict:
    key = jax.random.key(seed)
    ks = jax.random.split(key, 32)
    f32 = jnp.float32
    nrm = lambda k, shape, scale: jax.random.normal(k, shape, f32) * scale
    L = DEPTH

    x = jax.random.normal(ks[0], (BATCH, SEQ, D_MODEL), f32)
    norm1_w = 1.0 + nrm(ks[1], (L, D_MODEL), 0.05)
    w_in = nrm(ks[2], (L, D_MODEL, IN_PROJ_DIM), D_MODEL ** -0.5)
    b_branch_gate = nrm(ks[3], (L, N_GATES), 0.1)

    ssm_conv_w = nrm(ks[4], (L, SSM_CONV, SSM_CONV_DIM), SSM_CONV ** -0.5)
    ssm_conv_b = nrm(ks[5], (L, SSM_CONV_DIM), 0.02)
    u = jax.random.uniform(ks[6], (L, SSM_HEADS), f32)
    dt0 = jnp.exp(u * (math.log(SSM_DT_MAX) - math.log(SSM_DT_MIN)) + math.log(SSM_DT_MIN))
    dt0 = jnp.maximum(dt0, 1e-4)
    ssm_dt_bias = dt0 + jnp.log(-jnp.expm1(-dt0))
    ssm_a_log = jnp.log(jax.random.uniform(ks[7], (L, SSM_HEADS), f32, 1.0, 16.0))
    ssm_d = 1.0 + nrm(ks[8], (L, SSM_HEADS), 0.1)
    ssm_norm_w = 1.0 + nrm(ks[9], (L, SSM_D_INNER), 0.05)
    w_out_ssm = nrm(ks[10], (L, SSM_D_INNER, D_MODEL), SSM_D_INNER ** -0.5)

    lru_conv_w = nrm(ks[11], (L, LRU_CONV, LRU_WIDTH), LRU_CONV ** -0.5)
    lru_conv_b = nrm(ks[12], (L, LRU_WIDTH), 0.02)
    lru_w_r = nrm(ks[13], (L, LRU_BLOCKS, LRU_BLOCK, LRU_BLOCK), LRU_BLOCK ** -0.5)
    lru_b_r = nrm(ks[14], (L, LRU_WIDTH), 0.02)
    lru_w_i = nrm(ks[15], (L, LRU_BLOCKS, LRU_BLOCK, LRU_BLOCK), LRU_BLOCK ** -0.5)
    lru_b_i = nrm(ks[16], (L, LRU_WIDTH), 0.02)
    a0 = jax.random.uniform(ks[17], (L, LRU_WIDTH), f32, 0.9, 0.999)
    s0 = a0 ** (1.0 / LRU_C)
    lru_lambda = jnp.log(s0) - jnp.log1p(-s0)
    w_out_lru = nrm(ks[18], (L, LRU_WIDTH, D_MODEL), LRU_WIDTH ** -0.5)

    w_out = nrm(ks[19], (L, D_MODEL, D_MODEL), D_MODEL ** -0.5)
    norm2_w = 1.0 + nrm(ks[20], (L, D_MODEL), 0.05)
    w_ffn_in = nrm(ks[21], (L, D_MODEL, 2 * FFN_HIDDEN), D_MODEL ** -0.5)
    w_ffn_out = nrm(ks[22], (L, FFN_HIDDEN, D_MODEL), FFN_HIDDEN ** -0.5)
    norm_f_w = 1.0 + nrm(ks[23], (D_MODEL,), 0.05)

    return {
        "x": x, "norm1_w": norm1_w, "w_in": w_in, "b_branch_gate": b_branch_gate,
        "ssm_conv_w": ssm_conv_w, "ssm_conv_b": ssm_conv_b, "ssm_dt_bias": ssm_dt_bias,
        "ssm_a_log": ssm_a_log, "ssm_d": ssm_d, "ssm_norm_w": ssm_norm_w, "w_out_ssm": w_out_ssm,
        "lru_conv_w": lru_conv_w, "lru_conv_b": lru_conv_b, "lru_w_r": lru_w_r, "lru_b_r": lru_b_r,
        "lru_w_i": lru_w_i, "lru_b_i": lru_b_i, "lru_lambda": lru_lambda, "w_out_lru": w_out_lru,
        "w_out": w_out, "norm2_w": norm2_w, "w_ffn_in": w_ffn_in, "w_ffn_out": w_ffn_out,
        "norm_f_w": norm_f_w,
    }


def _fwd_reference(x, norm1_w, w_in, b_branch_gate, ssm_conv_w, ssm_conv_b, ssm_dt_bias, ssm_a_log,
              ssm_d, ssm_norm_w, w_out_ssm, lru_conv_w, lru_conv_b, lru_w_r, lru_b_r, lru_w_i,
              lru_b_i, lru_lambda, w_out_lru, w_out, norm2_w, w_ffn_in, w_ffn_out, norm_f_w):
    h = x
    for l in range(DEPTH):
        hn = rmsnorm(h, norm1_w[l])
        proj = hn @ w_in[l]
        gates, z, xbc, dt_raw, lru_x, lru_y = jnp.split(proj, IN_SPLITS, axis=-1)
        gates = jax.nn.sigmoid(gates + b_branch_gate[l])
        g_ssm, g_lru = jnp.split(gates, 2, axis=-1)
        y_ssm = mamba2_mixer(z, xbc, dt_raw, ssm_conv_w[l], ssm_conv_b[l], ssm_dt_bias[l],
                             ssm_a_log[l], ssm_d[l], ssm_norm_w[l]) @ w_out_ssm[l]
        y_lru = rglru_mixer(lru_x, lru_y, lru_conv_w[l], lru_conv_b[l], lru_w_r[l], lru_b_r[l],
                            lru_w_i[l], lru_b_i[l], lru_lambda[l]) @ w_out_lru[l]
        h = h + (g_ssm * y_ssm + g_lru * y_lru) @ w_out[l]
        hn = rmsnorm(h, norm2_w[l])
        gate, up = jnp.split(hn @ w_ffn_in[l], 2, axis=-1)
        h = h + (jax.nn.silu(gate) * up) @ w_ffn_out[l]
    return rmsnorm(h, norm_f_w)


import jax as _jax
import jax.numpy as _jnp

TWIN_FORMAT = 'train_step'
FWD_PARAMS = ['x', 'norm1_w', 'w_in', 'b_branch_gate', 'ssm_conv_w', 'ssm_conv_b', 'ssm_dt_bias', 'ssm_a_log', 'ssm_d', 'ssm_norm_w', 'w_out_ssm', 'lru_conv_w', 'lru_conv_b', 'lru_w_r', 'lru_b_r', 'lru_w_i', 'lru_b_i', 'lru_lambda', 'w_out_lru', 'w_out', 'norm2_w', 'w_ffn_in', 'w_ffn_out', 'norm_f_w']
TWIN_WEIGHTS = ['norm1_w', 'w_in', 'b_branch_gate', 'ssm_conv_w', 'ssm_conv_b', 'ssm_dt_bias', 'ssm_a_log', 'ssm_d', 'ssm_norm_w', 'w_out_ssm', 'lru_conv_w', 'lru_conv_b', 'lru_w_r', 'lru_b_r', 'lru_w_i', 'lru_b_i', 'lru_lambda', 'w_out_lru', 'w_out', 'norm2_w', 'w_ffn_in', 'w_ffn_out', 'norm_f_w']
TWIN_DIFF_INPUT = 'x'
TWIN_INPUTS = ['x', 'norm1_w', 'w_in', 'b_branch_gate', 'ssm_conv_w', 'ssm_conv_b', 'ssm_dt_bias', 'ssm_a_log', 'ssm_d', 'ssm_norm_w', 'w_out_ssm', 'lru_conv_w', 'lru_conv_b', 'lru_w_r', 'lru_b_r', 'lru_w_i', 'lru_b_i', 'lru_lambda', 'w_out_lru', 'w_out', 'norm2_w', 'w_ffn_in', 'w_ffn_out', 'norm_f_w', 'loss_target', 'm_norm1_w', 'm_w_in', 'm_b_branch_gate', 'm_ssm_conv_w', 'm_ssm_conv_b', 'm_ssm_dt_bias', 'm_ssm_a_log', 'm_ssm_d', 'm_ssm_norm_w', 'm_w_out_ssm', 'm_lru_conv_w', 'm_lru_conv_b', 'm_lru_w_r', 'm_lru_b_r', 'm_lru_w_i', 'm_lru_b_i', 'm_lru_lambda', 'm_w_out_lru', 'm_w_out', 'm_norm2_w', 'm_w_ffn_in', 'm_w_ffn_out', 'm_norm_f_w', 'v_norm1_w', 'v_w_in', 'v_b_branch_gate', 'v_ssm_conv_w', 'v_ssm_conv_b', 'v_ssm_dt_bias', 'v_ssm_a_log', 'v_ssm_d', 'v_ssm_norm_w', 'v_w_out_ssm', 'v_lru_conv_w', 'v_lru_conv_b', 'v_lru_w_r', 'v_lru_b_r', 'v_lru_w_i', 'v_lru_b_i', 'v_lru_lambda', 'v_w_out_lru', 'v_w_out', 'v_norm2_w', 'v_w_ffn_in', 'v_w_ffn_out', 'v_norm_f_w']
TWIN_OUTPUTS = ['loss', 'grad_x', 'grad_norm1_w', 'grad_w_in', 'grad_b_branch_gate', 'grad_ssm_conv_w', 'grad_ssm_conv_b', 'grad_ssm_dt_bias', 'grad_ssm_a_log', 'grad_ssm_d', 'grad_ssm_norm_w', 'grad_w_out_ssm', 'grad_lru_conv_w', 'grad_lru_conv_b', 'grad_lru_w_r', 'grad_lru_b_r', 'grad_lru_w_i', 'grad_lru_b_i', 'grad_lru_lambda', 'grad_w_out_lru', 'grad_w_out', 'grad_norm2_w', 'grad_w_ffn_in', 'grad_w_ffn_out', 'grad_norm_f_w', 'delta_norm1_w', 'delta_w_in', 'delta_b_branch_gate', 'delta_ssm_conv_w', 'delta_ssm_conv_b', 'delta_ssm_dt_bias', 'delta_ssm_a_log', 'delta_ssm_d', 'delta_ssm_norm_w', 'delta_w_out_ssm', 'delta_lru_conv_w', 'delta_lru_conv_b', 'delta_lru_w_r', 'delta_lru_b_r', 'delta_lru_w_i', 'delta_lru_b_i', 'delta_lru_lambda', 'delta_w_out_lru', 'delta_w_out', 'delta_norm2_w', 'delta_w_ffn_in', 'delta_w_ffn_out', 'delta_norm_f_w', 'new_m_norm1_w', 'new_m_w_in', 'new_m_b_branch_gate', 'new_m_ssm_conv_w', 'new_m_ssm_conv_b', 'new_m_ssm_dt_bias', 'new_m_ssm_a_log', 'new_m_ssm_d', 'new_m_ssm_norm_w', 'new_m_w_out_ssm', 'new_m_lru_conv_w', 'new_m_lru_conv_b', 'new_m_lru_w_r', 'new_m_lru_b_r', 'new_m_lru_w_i', 'new_m_lru_b_i', 'new_m_lru_lambda', 'new_m_w_out_lru', 'new_m_w_out', 'new_m_norm2_w', 'new_m_w_ffn_in', 'new_m_w_ffn_out', 'new_m_norm_f_w', 'new_v_norm1_w', 'new_v_w_in', 'new_v_b_branch_gate', 'new_v_ssm_conv_w', 'new_v_ssm_conv_b', 'new_v_ssm_dt_bias', 'new_v_ssm_a_log', 'new_v_ssm_d', 'new_v_ssm_norm_w', 'new_v_w_out_ssm', 'new_v_lru_conv_w', 'new_v_lru_conv_b', 'new_v_lru_w_r', 'new_v_lru_b_r', 'new_v_lru_w_i', 'new_v_lru_b_i', 'new_v_lru_lambda', 'new_v_w_out_lru', 'new_v_w_out', 'new_v_norm2_w', 'new_v_w_ffn_in', 'new_v_w_ffn_out', 'new_v_norm_f_w']
TWIN_LEAF_KINDS = {'loss': 'loss', 'grad_x': 'grad_x', 'grad_norm1_w': 'grad_w', 'grad_w_in': 'grad_w', 'grad_b_branch_gate': 'grad_w', 'grad_ssm_conv_w': 'grad_w', 'grad_ssm_conv_b': 'grad_w', 'grad_ssm_dt_bias': 'grad_w', 'grad_ssm_a_log': 'grad_w', 'grad_ssm_d': 'grad_w', 'grad_ssm_norm_w': 'grad_w', 'grad_w_out_ssm': 'grad_w', 'grad_lru_conv_w': 'grad_w', 'grad_lru_conv_b': 'grad_w', 'grad_lru_w_r': 'grad_w', 'grad_lru_b_r': 'grad_w', 'grad_lru_w_i': 'grad_w', 'grad_lru_b_i': 'grad_w', 'grad_lru_lambda': 'grad_w', 'grad_w_out_lru': 'grad_w', 'grad_w_out': 'grad_w', 'grad_norm2_w': 'grad_w', 'grad_w_ffn_in': 'grad_w', 'grad_w_ffn_out': 'grad_w', 'grad_norm_f_w': 'grad_w', 'delta_norm1_w': 'delta_w', 'delta_w_in': 'delta_w', 'delta_b_branch_gate': 'delta_w', 'delta_ssm_conv_w': 'delta_w', 'delta_ssm_conv_b': 'delta_w', 'delta_ssm_dt_bias': 'delta_w', 'delta_ssm_a_log': 'delta_w', 'delta_ssm_d': 'delta_w', 'delta_ssm_norm_w': 'delta_w', 'delta_w_out_ssm': 'delta_w', 'delta_lru_conv_w': 'delta_w', 'delta_lru_conv_b': 'delta_w', 'delta_lru_w_r': 'delta_w', 'delta_lru_b_r': 'delta_w', 'delta_lru_w_i': 'delta_w', 'delta_lru_b_i': 'delta_w', 'delta_lru_lambda': 'delta_w', 'delta_w_out_lru': 'delta_w', 'delta_w_out': 'delta_w', 'delta_norm2_w': 'delta_w', 'delta_w_ffn_in': 'delta_w', 'delta_w_ffn_out': 'delta_w', 'delta_norm_f_w': 'delta_w', 'new_m_norm1_w': 'new_m', 'new_m_w_in': 'new_m', 'new_m_b_branch_gate': 'new_m', 'new_m_ssm_conv_w': 'new_m', 'new_m_ssm_conv_b': 'new_m', 'new_m_ssm_dt_bias': 'new_m', 'new_m_ssm_a_log': 'new_m', 'new_m_ssm_d': 'new_m', 'new_m_ssm_norm_w': 'new_m', 'new_m_w_out_ssm': 'new_m', 'new_m_lru_conv_w': 'new_m', 'new_m_lru_conv_b': 'new_m', 'new_m_lru_w_r': 'new_m', 'new_m_lru_b_r': 'new_m', 'new_m_lru_w_i': 'new_m', 'new_m_lru_b_i': 'new_m', 'new_m_lru_lambda': 'new_m', 'new_m_w_out_lru': 'new_m', 'new_m_w_out': 'new_m', 'new_m_norm2_w': 'new_m', 'new_m_w_ffn_in': 'new_m', 'new_m_w_ffn_out': 'new_m', 'new_m_norm_f_w': 'new_m', 'new_v_norm1_w': 'new_v', 'new_v_w_in': 'new_v', 'new_v_b_branch_gate': 'new_v', 'new_v_ssm_conv_w': 'new_v', 'new_v_ssm_conv_b': 'new_v', 'new_v_ssm_dt_bias': 'new_v', 'new_v_ssm_a_log': 'new_v', 'new_v_ssm_d': 'new_v', 'new_v_ssm_norm_w': 'new_v', 'new_v_w_out_ssm': 'new_v', 'new_v_lru_conv_w': 'new_v', 'new_v_lru_conv_b': 'new_v', 'new_v_lru_w_r': 'new_v', 'new_v_lru_b_r': 'new_v', 'new_v_lru_w_i': 'new_v', 'new_v_lru_b_i': 'new_v', 'new_v_lru_lambda': 'new_v', 'new_v_w_out_lru': 'new_v', 'new_v_w_out': 'new_v', 'new_v_norm2_w': 'new_v', 'new_v_w_ffn_in': 'new_v', 'new_v_w_ffn_out': 'new_v', 'new_v_norm_f_w': 'new_v'}


def _forward(args):
    return _fwd_reference(*[args[k] for k in FWD_PARAMS])


def _output_shape():
    def fwd():
        inp = _fwd_setup_inputs(0)
        return _fwd_reference(*[inp[k] for k in FWD_PARAMS])
    out = _jax.eval_shape(fwd)
    return out.shape, out.dtype

N_MICROBATCH = 1
ADAM_LR = 0.001
ADAM_B1 = 0.9
ADAM_B2 = 0.999
ADAM_EPS = 1e-08
ADAM_WD = 0.01
ADAM_STEP = 10
PER_EXAMPLE_BATCH_AXIS = {'x': 0, 'loss_target': 0}
SHARED_INPUTS = []
_WEIGHT_DTYPES = {'norm1_w': _jnp.float32, 'w_in': _jnp.float32, 'b_branch_gate': _jnp.float32, 'ssm_conv_w': _jnp.float32, 'ssm_conv_b': _jnp.float32, 'ssm_dt_bias': _jnp.float32, 'ssm_a_log': _jnp.float32, 'ssm_d': _jnp.float32, 'ssm_norm_w': _jnp.float32, 'w_out_ssm': _jnp.float32, 'lru_conv_w': _jnp.float32, 'lru_conv_b': _jnp.float32, 'lru_w_r': _jnp.float32, 'lru_b_r': _jnp.float32, 'lru_w_i': _jnp.float32, 'lru_b_i': _jnp.float32, 'lru_lambda': _jnp.float32, 'w_out_lru': _jnp.float32, 'w_out': _jnp.float32, 'norm2_w': _jnp.float32, 'w_ffn_in': _jnp.float32, 'w_ffn_out': _jnp.float32, 'norm_f_w': _jnp.float32}
MOMENT_SCALE = {'norm1_w': 1.584702e-01, 'w_in': 4.955116e-02, 'b_branch_gate': 2.944939e-02, 'ssm_conv_w': 5.568700e-02, 'ssm_conv_b': 8.314750e-02, 'ssm_dt_bias': 1.476595e-01, 'ssm_a_log': 2.236784e-01, 'ssm_d': 3.789820e-01, 'ssm_norm_w': 6.681557e-02, 'w_out_ssm': 9.377433e-02, 'lru_conv_w': 3.798657e-02, 'lru_conv_b': 4.117562e-01, 'lru_w_r': 1.102450e-02, 'lru_b_r': 1.017064e-02, 'lru_w_i': 1.972095e-02, 'lru_b_i': 1.495121e-02, 'lru_lambda': 1.795846e-02, 'w_out_lru': 4.053073e-02, 'w_out': 1.026939e-01, 'norm2_w': 1.146136e-01, 'w_ffn_in': 5.037045e-02, 'w_ffn_out': 8.302584e-02, 'norm_f_w': 3.199003e+01}


def _to_microbatches(a, axis):
    t = _jnp.moveaxis(a, axis, 0)
    t = t.reshape((N_MICROBATCH, t.shape[0] // N_MICROBATCH) + t.shape[1:])
    return _jnp.moveaxis(t, 1, axis + 1)


def setup_inputs(seed: int = 0) -> dict:
    inp = _fwd_setup_inputs(seed)
    key = _jax.random.fold_in(_jax.random.key(seed), 7919)
    shape, _ = _output_shape()
    out = dict(inp)
    out["loss_target"] = _jax.random.normal(_jax.random.fold_in(key, 0), shape, _jnp.float32)
    for i, name in enumerate(TWIN_WEIGHTS):
        w = inp[name].astype(_jnp.float32)
        if MOMENT_SCALE is None:
            s = _jnp.sqrt(_jnp.mean(_jnp.square(w)) + 1e-30)
        else:
            s = MOMENT_SCALE[name]
        km, kv = _jax.random.split(_jax.random.fold_in(key, i + 1))
        out[name] = w
        out["m_" + name] = s * _jax.random.normal(km, w.shape, _jnp.float32)
        out["v_" + name] = (s * s) * _jax.random.uniform(kv, w.shape, _jnp.float32, 0.5, 1.5)
    if N_MICROBATCH > 1:
        for name, axis in PER_EXAMPLE_BATCH_AXIS.items():
            out[name] = _to_microbatches(out[name], axis)
    return {'x': out['x'], 'norm1_w': out['norm1_w'], 'w_in': out['w_in'], 'b_branch_gate': out['b_branch_gate'], 'ssm_conv_w': out['ssm_conv_w'], 'ssm_conv_b': out['ssm_conv_b'], 'ssm_dt_bias': out['ssm_dt_bias'], 'ssm_a_log': out['ssm_a_log'], 'ssm_d': out['ssm_d'], 'ssm_norm_w': out['ssm_norm_w'], 'w_out_ssm': out['w_out_ssm'], 'lru_conv_w': out['lru_conv_w'], 'lru_conv_b': out['lru_conv_b'], 'lru_w_r': out['lru_w_r'], 'lru_b_r': out['lru_b_r'], 'lru_w_i': out['lru_w_i'], 'lru_b_i': out['lru_b_i'], 'lru_lambda': out['lru_lambda'], 'w_out_lru': out['w_out_lru'], 'w_out': out['w_out'], 'norm2_w': out['norm2_w'], 'w_ffn_in': out['w_ffn_in'], 'w_ffn_out': out['w_ffn_out'], 'norm_f_w': out['norm_f_w'], 'loss_target': out['loss_target'], 'm_norm1_w': out['m_norm1_w'], 'm_w_in': out['m_w_in'], 'm_b_branch_gate': out['m_b_branch_gate'], 'm_ssm_conv_w': out['m_ssm_conv_w'], 'm_ssm_conv_b': out['m_ssm_conv_b'], 'm_ssm_dt_bias': out['m_ssm_dt_bias'], 'm_ssm_a_log': out['m_ssm_a_log'], 'm_ssm_d': out['m_ssm_d'], 'm_ssm_norm_w': out['m_ssm_norm_w'], 'm_w_out_ssm': out['m_w_out_ssm'], 'm_lru_conv_w': out['m_lru_conv_w'], 'm_lru_conv_b': out['m_lru_conv_b'], 'm_lru_w_r': out['m_lru_w_r'], 'm_lru_b_r': out['m_lru_b_r'], 'm_lru_w_i': out['m_lru_w_i'], 'm_lru_b_i': out['m_lru_b_i'], 'm_lru_lambda': out['m_lru_lambda'], 'm_w_out_lru': out['m_w_out_lru'], 'm_w_out': out['m_w_out'], 'm_norm2_w': out['m_norm2_w'], 'm_w_ffn_in': out['m_w_ffn_in'], 'm_w_ffn_out': out['m_w_ffn_out'], 'm_norm_f_w': out['m_norm_f_w'], 'v_norm1_w': out['v_norm1_w'], 'v_w_in': out['v_w_in'], 'v_b_branch_gate': out['v_b_branch_gate'], 'v_ssm_conv_w': out['v_ssm_conv_w'], 'v_ssm_conv_b': out['v_ssm_conv_b'], 'v_ssm_dt_bias': out['v_ssm_dt_bias'], 'v_ssm_a_log': out['v_ssm_a_log'], 'v_ssm_d': out['v_ssm_d'], 'v_ssm_norm_w': out['v_ssm_norm_w'], 'v_w_out_ssm': out['v_w_out_ssm'], 'v_lru_conv_w': out['v_lru_conv_w'], 'v_lru_conv_b': out['v_lru_conv_b'], 'v_lru_w_r': out['v_lru_w_r'], 'v_lru_b_r': out['v_lru_b_r'], 'v_lru_w_i': out['v_lru_w_i'], 'v_lru_b_i': out['v_lru_b_i'], 'v_lru_lambda': out['v_lru_lambda'], 'v_w_out_lru': out['v_w_out_lru'], 'v_w_out': out['v_w_out'], 'v_norm2_w': out['v_norm2_w'], 'v_w_ffn_in': out['v_w_ffn_in'], 'v_w_ffn_out': out['v_w_ffn_out'], 'v_norm_f_w': out['v_norm_f_w']}


def _loss(weights, diff, rest, loss_target):
    with _jax.named_scope("forward"):
        args = {**rest, TWIN_DIFF_INPUT: diff, **{k: w.astype(_WEIGHT_DTYPES[k]) for k, w in weights.items()}}
        y = _forward(args)
    with _jax.named_scope("loss_head"):
        err = _jnp.square(y.astype(_jnp.float32) - loss_target)
        return 0.5 * _jnp.sum(_jnp.mean(err, axis=-1)) if err.ndim else 0.5 * err


def _adamw(w, g, m, v):
    m = ADAM_B1 * m + (1.0 - ADAM_B1) * g
    v = ADAM_B2 * v + (1.0 - ADAM_B2) * _jnp.square(g)
    m_hat = m / (1.0 - ADAM_B1 ** ADAM_STEP)
    v_hat = v / (1.0 - ADAM_B2 ** ADAM_STEP)
    delta = -ADAM_LR * (m_hat / (_jnp.sqrt(v_hat) + ADAM_EPS) + ADAM_WD * w)
    return delta, m, v


def reference(x, norm1_w, w_in, b_branch_gate, ssm_conv_w, ssm_conv_b, ssm_dt_bias, ssm_a_log, ssm_d, ssm_norm_w, w_out_ssm, lru_conv_w, lru_conv_b, lru_w_r, lru_b_r, lru_w_i, lru_b_i, lru_lambda, w_out_lru, w_out, norm2_w, w_ffn_in, w_ffn_out, norm_f_w, loss_target, m_norm1_w, m_w_in, m_b_branch_gate, m_ssm_conv_w, m_ssm_conv_b, m_ssm_dt_bias, m_ssm_a_log, m_ssm_d, m_ssm_norm_w, m_w_out_ssm, m_lru_conv_w, m_lru_conv_b, m_lru_w_r, m_lru_b_r, m_lru_w_i, m_lru_b_i, m_lru_lambda, m_w_out_lru, m_w_out, m_norm2_w, m_w_ffn_in, m_w_ffn_out, m_norm_f_w, v_norm1_w, v_w_in, v_b_branch_gate, v_ssm_conv_w, v_ssm_conv_b, v_ssm_dt_bias, v_ssm_a_log, v_ssm_d, v_ssm_norm_w, v_w_out_ssm, v_lru_conv_w, v_lru_conv_b, v_lru_w_r, v_lru_b_r, v_lru_w_i, v_lru_b_i, v_lru_lambda, v_w_out_lru, v_w_out, v_norm2_w, v_w_ffn_in, v_w_ffn_out, v_norm_f_w):
    given = dict(x=x, norm1_w=norm1_w, w_in=w_in, b_branch_gate=b_branch_gate, ssm_conv_w=ssm_conv_w, ssm_conv_b=ssm_conv_b, ssm_dt_bias=ssm_dt_bias, ssm_a_log=ssm_a_log, ssm_d=ssm_d, ssm_norm_w=ssm_norm_w, w_out_ssm=w_out_ssm, lru_conv_w=lru_conv_w, lru_conv_b=lru_conv_b, lru_w_r=lru_w_r, lru_b_r=lru_b_r, lru_w_i=lru_w_i, lru_b_i=lru_b_i, lru_lambda=lru_lambda, w_out_lru=w_out_lru, w_out=w_out, norm2_w=norm2_w, w_ffn_in=w_ffn_in, w_ffn_out=w_ffn_out, norm_f_w=norm_f_w, loss_target=loss_target, m_norm1_w=m_norm1_w, m_w_in=m_w_in, m_b_branch_gate=m_b_branch_gate, m_ssm_conv_w=m_ssm_conv_w, m_ssm_conv_b=m_ssm_conv_b, m_ssm_dt_bias=m_ssm_dt_bias, m_ssm_a_log=m_ssm_a_log, m_ssm_d=m_ssm_d, m_ssm_norm_w=m_ssm_norm_w, m_w_out_ssm=m_w_out_ssm, m_lru_conv_w=m_lru_conv_w, m_lru_conv_b=m_lru_conv_b, m_lru_w_r=m_lru_w_r, m_lru_b_r=m_lru_b_r, m_lru_w_i=m_lru_w_i, m_lru_b_i=m_lru_b_i, m_lru_lambda=m_lru_lambda, m_w_out_lru=m_w_out_lru, m_w_out=m_w_out, m_norm2_w=m_norm2_w, m_w_ffn_in=m_w_ffn_in, m_w_ffn_out=m_w_ffn_out, m_norm_f_w=m_norm_f_w, v_norm1_w=v_norm1_w, v_w_in=v_w_in, v_b_branch_gate=v_b_branch_gate, v_ssm_conv_w=v_ssm_conv_w, v_ssm_conv_b=v_ssm_conv_b, v_ssm_dt_bias=v_ssm_dt_bias, v_ssm_a_log=v_ssm_a_log, v_ssm_d=v_ssm_d, v_ssm_norm_w=v_ssm_norm_w, v_w_out_ssm=v_w_out_ssm, v_lru_conv_w=v_lru_conv_w, v_lru_conv_b=v_lru_conv_b, v_lru_w_r=v_lru_w_r, v_lru_b_r=v_lru_b_r, v_lru_w_i=v_lru_w_i, v_lru_b_i=v_lru_b_i, v_lru_lambda=v_lru_lambda, v_w_out_lru=v_w_out_lru, v_w_out=v_w_out, v_norm2_w=v_norm2_w, v_w_ffn_in=v_w_ffn_in, v_w_ffn_out=v_w_ffn_out, v_norm_f_w=v_norm_f_w)
    weights = {n: given[n] for n in TWIN_WEIGHTS}
    shared = {n: given[n] for n in SHARED_INPUTS}
    per_example = {n: given[n] for n in ['x']}
    grad_fn = _jax.value_and_grad(_loss, argnums=(0, 1))

    def one_microbatch(ex, loss_target):
        ex = dict(ex)
        diff = ex.pop(TWIN_DIFF_INPUT)
        return grad_fn(weights, diff, {**shared, **ex}, loss_target)

    if N_MICROBATCH == 1:
        loss, (grad_w, grad_x) = one_microbatch(per_example, given["loss_target"])
    else:
        def body(carry, xs):
            loss_sum, grad_sum = carry
            l_k, (gw_k, gx_k) = one_microbatch(xs[0], xs[1])
            with _jax.named_scope("update"):
                return (loss_sum + l_k, _jax.tree.map(_jnp.add, grad_sum, gw_k)), gx_k

        init = (_jnp.zeros((), _jnp.float32), _jax.tree.map(_jnp.zeros_like, weights))
        (loss, grad_w), grad_x = _jax.lax.scan(body, init, (per_example, given["loss_target"]))
    with _jax.named_scope("update"):
        delta_w, new_m, new_v = {}, {}, {}
        for n in TWIN_WEIGHTS:
            delta_w[n], new_m[n], new_v[n] = _adamw(weights[n], grad_w[n], given["m_" + n], given["v_" + n])
    return (loss, grad_x, *[grad_w[n] for n in TWIN_WEIGHTS], *[delta_w[n] for n in TWIN_WEIGHTS],
            *[new_m[n] for n in TWIN_WEIGHTS], *[new_v[n] for n in TWIN_WEIGHTS])
```

```python
import math

import jax
import jax.numpy as jnp
from jax import lax
from jax.experimental import pallas as pl
from jax.experimental.pallas import tpu as pltpu

F32 = jnp.float32
BF16 = jnp.bfloat16
HIGHEST = lax.Precision.HIGHEST
MESH = pl.DeviceIdType.MESH
AXES = ("x", "y", "c")
N_DEV = 8

D_MODEL = 1024
SSM_INNER = 2048
SSM_HEADS = 32
SSM_HEAD_DIM = 64
SSM_GROUPS = 4
SSM_STATE = 128
SSM_BC = SSM_GROUPS * SSM_STATE
SSM_CONV_DIM = SSM_INNER + 2 * SSM_BC
SSM_CHUNK = 128
SSM_PAIRS = SSM_HEADS // 2
CONV_K = 4
LRU_WIDTH = 1280
LRU_BLOCKS = 10
LRU_C = 8.0
FFN_HIDDEN = 2816
RMS_EPS = 1e-6
IN_PROJ = 9760

COL_GATES = 0
COL_Z = 2048
COL_XBC = 4096
COL_LRU_X = 7168
COL_LRU_Y = 8448
COL_DT = 9728
PROJ_W = 9856
ORIG_DT = 7168
ORIG_LRU = 7200

ADAM_LR = 0.001
ADAM_B1 = 0.9
ADAM_B2 = 0.999
ADAM_EPS = 1e-08
ADAM_WD = 0.01
ADAM_STEP = 10

LANES = 128
SUBLANES = 8
V7X_VMEM_BYTES = 64 * 1024 * 1024
VMEM_LIMIT = V7X_VMEM_BYTES * 3 // 4
VMEM_LIMIT_BIG = V7X_VMEM_BYTES * 15 // 16

NT = (((1,), (1,)), ((), ()))
TN = (((0,), (0,)), ((), ()))

FLAT_W = 1024
BIG_ROWS = 2880
BIG_TILE = 288
SMALL_ROWS = 336


def _params(sem=None, big=False):
    return pltpu.CompilerParams(dimension_semantics=sem,
                                vmem_limit_bytes=VMEM_LIMIT_BIG if big else VMEM_LIMIT)


def _blk(dim, cap):
    if dim <= cap:
        return dim
    for m in range(cap // LANES, 0, -1):
        if dim % (m * LANES) == 0:
            return m * LANES
    raise ValueError(f"no block for {dim}")


def _rows(t):
    return min(t, 256)


def _sigmoid(v):
    return 1.0 / (1.0 + jnp.exp(-v))


def _softplus(v):
    e = jnp.exp(-jnp.abs(v))
    u = 1.0 + e
    log1p = jnp.where(u == 1.0, e, jnp.log(u) * e / jnp.where(u == 1.0, 1.0, u - 1.0))
    return jnp.maximum(v, 0.0) + log1p


def _iota(shape, dim):
    return lax.broadcasted_iota(jnp.int32, shape, dim)


def _shift_down(v, s):
    if s == 0:
        return v
    return jnp.where(_iota(v.shape, 0) >= s, pltpu.roll(v, s, 0), 0.0)


def _shift_up(v, s):
    if s == 0:
        return v
    n = v.shape[0]
    return jnp.where(_iota(v.shape, 0) < n - s, pltpu.roll(v, n - s, 0), 0.0)


def _bdot(a, b, dn=None):
    a = a.astype(BF16)
    b = b.astype(BF16)
    if dn is None:
        return jnp.dot(a, b, preferred_element_type=F32)
    return lax.dot_general(a, b, dn, preferred_element_type=F32)


def _fdot(a, b, dn=None):
    if dn is None:
        return jnp.dot(a, b, precision=HIGHEST, preferred_element_type=F32)
    return lax.dot_general(a, b, dn, precision=HIGHEST, preferred_element_type=F32)


def _mm(a, b, *, ta=False, tb=False, add=None, name):
    if ta:
        kdim, m = a.shape
    else:
        m, kdim = a.shape
    if tb:
        n, k2 = b.shape
    else:
        k2, n = b.shape
    assert kdim == k2, (a.shape, b.shape, ta, tb)
    bm, bn, bk = _blk(m, 512), _blk(n, 1408), _blk(kdim, 1408)
    nk = kdim // bk
    dn = (((0 if ta else 1,), (1 if tb else 0,)), ((), ()))

    def body(*refs):
        if add is None:
            a_ref, b_ref, o_ref, acc = refs
        else:
            a_ref, b_ref, r_ref, o_ref, acc = refs
        k = pl.program_id(2)

        @pl.when(k == 0)
        def _():
            acc[...] = jnp.zeros_like(acc)

        acc[...] += lax.dot_general(a_ref[...].astype(BF16), b_ref[...].astype(BF16), dn,
                                    preferred_element_type=F32)

        @pl.when(k == nk - 1)
        def _():
            r = acc[...]
            if add is not None:
                r = r + r_ref[...]
            o_ref[...] = r

    a_spec = pl.BlockSpec((bk, bm), lambda i, j, k: (k, i)) if ta else pl.BlockSpec((bm, bk), lambda i, j, k: (i, k))
    b_spec = pl.BlockSpec((bn, bk), lambda i, j, k: (j, k)) if tb else pl.BlockSpec((bk, bn), lambda i, j, k: (k, j))
    o_spec = pl.BlockSpec((bm, bn), lambda i, j, k: (i, j))
    in_specs = [a_spec, b_spec] + ([o_spec] if add is not None else [])
    args = (a, b) + ((add,) if add is not None else ())
    return pl.pallas_call(
        body, name=name, grid=(m // bm, n // bn, nk),
        in_specs=in_specs, out_specs=o_spec,
        out_shape=jax.ShapeDtypeStruct((m, n), F32),
        scratch_shapes=[pltpu.VMEM((bm, bn), F32)],
        compiler_params=_params(("parallel", "parallel", "arbitrary")),
    )(*args)


def _rmsnorm_fwd(x, w, name):
    t, d = x.shape
    tr = _rows(t)

    def body(x_ref, w_ref, o_ref):
        xv = x_ref[...]
        rstd = lax.rsqrt(jnp.mean(xv * xv, axis=-1, keepdims=True) + RMS_EPS)
        o_ref[...] = xv * rstd * w_ref[...]

    return pl.pallas_call(
        body, name=name, grid=(t // tr,),
        in_specs=[pl.BlockSpec((tr, d), lambda i: (i, 0)), pl.BlockSpec((1, d), lambda i: (0, 0))],
        out_specs=pl.BlockSpec((tr, d), lambda i: (i, 0)),
        out_shape=jax.ShapeDtypeStruct((t, d), F32),
        compiler_params=_params(("parallel",)),
    )(x, w)


def _rmsnorm_bwd(x, w, dy, dres, name):
    t, d = x.shape
    tr = _rows(t)

    def body(x_ref, w_ref, dy_ref, dres_ref, dx_ref, dw_ref):
        i = pl.program_id(0)
        xv = x_ref[...]
        rstd = lax.rsqrt(jnp.mean(xv * xv, axis=-1, keepdims=True) + RMS_EPS)
        xhat = xv * rstd
        dyv = dy_ref[...]
        dxhat = dyv * w_ref[...]
        m = jnp.mean(dxhat * xhat, axis=-1, keepdims=True)
        dx_ref[...] = rstd * (dxhat - xhat * m) + dres_ref[...]
        part = jnp.sum(dyv * xhat, axis=0, keepdims=True)

        @pl.when(i == 0)
        def _():
            dw_ref[...] = part

        @pl.when(i > 0)
        def _():
            dw_ref[...] += part

    row = pl.BlockSpec((tr, d), lambda i: (i, 0))
    vec = pl.BlockSpec((1, d), lambda i: (0, 0))
    return pl.pallas_call(
        body, name=name, grid=(t // tr,),
        in_specs=[row, vec, row, row], out_specs=[row, vec],
        out_shape=[jax.ShapeDtypeStruct((t, d), F32), jax.ShapeDtypeStruct((1, d), F32)],
        compiler_params=_params(("arbitrary",)),
    )(x, w, dy, dres)


def _loss_head(h2, w, tgt):
    t, d = h2.shape
    tr = _rows(t)

    def body(x_ref, w_ref, t_ref, dx_ref, dw_ref, ls_ref):
        i = pl.program_id(0)
        xv = x_ref[...]
        wv = w_ref[...]
        rstd = lax.rsqrt(jnp.mean(xv * xv, axis=-1, keepdims=True) + RMS_EPS)
        xhat = xv * rstd
        err = xhat * wv - t_ref[...]
        dyv = err * (1.0 / d)
        dxhat = dyv * wv
        m = jnp.mean(dxhat * xhat, axis=-1, keepdims=True)
        dx_ref[...] = rstd * (dxhat - xhat * m)
        dw_part = jnp.sum(dyv * xhat, axis=0, keepdims=True)
        ls_part = jnp.sum(err * err, axis=0, keepdims=True)

        @pl.when(i == 0)
        def _():
            dw_ref[...] = dw_part
            ls_ref[...] = ls_part

        @pl.when(i > 0)
        def _():
            dw_ref[...] += dw_part
            ls_ref[...] += ls_part

    row = pl.BlockSpec((tr, d), lambda i: (i, 0))
    vec = pl.BlockSpec((1, d), lambda i: (0, 0))
    return pl.pallas_call(
        body, name="loss_head", grid=(t // tr,),
        in_specs=[row, vec, row], out_specs=[row, vec, vec],
        out_shape=[jax.ShapeDtypeStruct((t, d), F32), jax.ShapeDtypeStruct((1, d), F32),
                   jax.ShapeDtypeStruct((1, d), F32)],
        compiler_params=_params(("arbitrary",)),
    )(h2, w, tgt)


def _merge_fwd(proj, bg, ys, yl):
    t = proj.shape[0]
    d = D_MODEL
    tr = _rows(t)

    def body(ps_ref, pl_ref, bg_ref, ys_ref, yl_ref, o_ref):
        gs = _sigmoid(ps_ref[...] + bg_ref[:, 0:d])
        gl = _sigmoid(pl_ref[...] + bg_ref[:, d:2 * d])
        o_ref[...] = gs * ys_ref[...] + gl * yl_ref[...]

    row = pl.BlockSpec((tr, d), lambda i: (i, 0))
    return pl.pallas_call(
        body, name="merge_fwd", grid=(t // tr,),
        in_specs=[row, pl.BlockSpec((tr, d), lambda i: (i, 1)), pl.BlockSpec((1, 2 * d), lambda i: (0, 0)), row, row],
        out_specs=row, out_shape=jax.ShapeDtypeStruct((t, d), F32),
        compiler_params=_params(("parallel",)),
    )(proj, proj, bg, ys, yl)


def _merge_bwd(proj, bg, ys, yl, dm):
    t = proj.shape[0]
    d = D_MODEL
    tr = _rows(t)

    def body(ps_ref, pl_ref, bg_ref, ys_ref, yl_ref, dm_ref, dys_ref, dyl_ref, dg_ref, dbg_ref):
        i = pl.program_id(0)
        gs = _sigmoid(ps_ref[...] + bg_ref[:, 0:d])
        gl = _sigmoid(pl_ref[...] + bg_ref[:, d:2 * d])
        dmv = dm_ref[...]
        dys_ref[...] = dmv * gs
        dyl_ref[...] = dmv * gl
        dgs = dmv * ys_ref[...] * gs * (1.0 - gs)
        dgl = dmv * yl_ref[...] * gl * (1.0 - gl)
        dg_ref[:, 0:d] = dgs
        dg_ref[:, d:2 * d] = dgl

        @pl.when(i == 0)
        def _():
            dbg_ref[...] = jnp.zeros_like(dbg_ref)

        dbg_ref[:, 0:d] += jnp.sum(dgs, axis=0, keepdims=True)
        dbg_ref[:, d:2 * d] += jnp.sum(dgl, axis=0, keepdims=True)

    row = pl.BlockSpec((tr, d), lambda i: (i, 0))
    wide = pl.BlockSpec((tr, 2 * d), lambda i: (i, 0))
    vec = pl.BlockSpec((1, 2 * d), lambda i: (0, 0))
    return pl.pallas_call(
        body, name="merge_bwd", grid=(t // tr,),
        in_specs=[row, pl.BlockSpec((tr, d), lambda i: (i, 1)), vec, row, row, row],
        out_specs=[row, row, wide, vec],
        out_shape=[jax.ShapeDtypeStruct((t, d), F32), jax.ShapeDtypeStruct((t, d), F32),
                   jax.ShapeDtypeStruct((t, 2 * d), F32), jax.ShapeDtypeStruct((1, 2 * d), F32)],
        compiler_params=_params(("arbitrary",)),
    )(proj, proj, bg, ys, yl, dm)


def _swiglu_fwd(gu):
    t = gu.shape[0]
    f = FFN_HIDDEN
    tr = _rows(t)

    def body(g_ref, u_ref, o_ref):
        g = g_ref[...]
        o_ref[...] = g * _sigmoid(g) * u_ref[...]

    return pl.pallas_call(
        body, name="swiglu_fwd", grid=(t // tr,),
        in_specs=[pl.BlockSpec((tr, f), lambda i: (i, 0)), pl.BlockSpec((tr, f), lambda i: (i, 1))],
        out_specs=pl.BlockSpec((tr, f), lambda i: (i, 0)),
        out_shape=jax.ShapeDtypeStruct((t, f), F32),
        compiler_params=_params(("parallel",)),
    )(gu, gu)


def _swiglu_bwd(gu, dact):
    t = gu.shape[0]
    f = FFN_HIDDEN
    tr = _rows(t)

    def body(g_ref, u_ref, da_ref, o_ref):
        g = g_ref[...]
        sg = _sigmoid(g)
        da = da_ref[...]
        o_ref[:, 0:f] = da * u_ref[...] * (sg * (1.0 + g * (1.0 - sg)))
        o_ref[:, f:2 * f] = da * g * sg

    return pl.pallas_call(
        body, name="swiglu_bwd", grid=(t // tr,),
        in_specs=[pl.BlockSpec((tr, f), lambda i: (i, 0)), pl.BlockSpec((tr, f), lambda i: (i, 1)),
                  pl.BlockSpec((tr, f), lambda i: (i, 0))],
        out_specs=pl.BlockSpec((tr, 2 * f), lambda i: (i, 0)),
        out_shape=jax.ShapeDtypeStruct((t, 2 * f), F32),
        compiler_params=_params(("parallel",)),
    )(gu, gu, dact)


def _conv_pre(xv, wv, bv):
    pre = bv + wv[CONV_K - 1:CONV_K, :] * xv
    for k in range(CONV_K - 1):
        pre = pre + wv[k:k + 1, :] * _shift_down(xv, CONV_K - 1 - k)
    return pre


def _ssm_conv_fwd(proj, w, b):
    t = proj.shape[0]
    nb = SSM_CONV_DIM // LANES
    c0 = COL_XBC // LANES

    def body(x_ref, w_ref, b_ref, o_ref):
        pre = _conv_pre(x_ref[...], w_ref[...], b_ref[...])
        o_ref[...] = pre * _sigmoid(pre)

    return pl.pallas_call(
        body, name="ssm_conv_fwd", grid=(nb,),
        in_specs=[pl.BlockSpec((t, LANES), lambda j: (0, c0 + j)), pl.BlockSpec((CONV_K, LANES), lambda j: (0, j)),
                  pl.BlockSpec((1, LANES), lambda j: (0, j))],
        out_specs=pl.BlockSpec((t, LANES), lambda j: (0, j)),
        out_shape=jax.ShapeDtypeStruct((t, SSM_CONV_DIM), F32),
        compiler_params=_params(("parallel",)),
    )(proj, w, b)


def _ssm_conv_bwd(proj, w, b, dact):
    t = proj.shape[0]
    nb = SSM_CONV_DIM // LANES
    c0 = COL_XBC // LANES

    def body(x_ref, w_ref, b_ref, da_ref, dx_ref, dw_ref, db_ref):
        xv = x_ref[...]
        wv = w_ref[...]
        pre = _conv_pre(xv, wv, b_ref[...])
        sg = _sigmoid(pre)
        dpre = da_ref[...] * (sg * (1.0 + pre * (1.0 - sg)))
        dx = wv[CONV_K - 1:CONV_K, :] * dpre
        for k in range(CONV_K - 1):
            dx = dx + wv[k:k + 1, :] * _shift_up(dpre, CONV_K - 1 - k)
        dx_ref[...] = dx
        for k in range(CONV_K):
            dw_ref[k:k + 1, :] = jnp.sum(dpre * _shift_down(xv, CONV_K - 1 - k), axis=0, keepdims=True)
        db_ref[...] = jnp.sum(dpre, axis=0, keepdims=True)

    col = pl.BlockSpec((t, LANES), lambda j: (0, j))
    wsp = pl.BlockSpec((CONV_K, LANES), lambda j: (0, j))
    bsp = pl.BlockSpec((1, LANES), lambda j: (0, j))
    return pl.pallas_call(
        body, name="ssm_conv_bwd", grid=(nb,),
        in_specs=[pl.BlockSpec((t, LANES), lambda j: (0, c0 + j)), wsp, bsp, col],
        out_specs=[col, wsp, bsp],
        out_shape=[jax.ShapeDtypeStruct((t, SSM_CONV_DIM), F32), jax.ShapeDtypeStruct((CONV_K, SSM_CONV_DIM), F32),
                   jax.ShapeDtypeStruct((1, SSM_CONV_DIM), F32)],
        compiler_params=_params(("parallel",)),
    )(proj, w, b, dact)


def _ssd_chunk_terms(dtr, bias, alog):
    a = -jnp.exp(alog)
    dt = _softplus(dtr + bias)
    row = _iota((SSM_CHUNK, SSM_CHUNK), 0)
    col = _iota((SSM_CHUNK, SSM_CHUNK), 1)
    tri = (row >= col).astype(F32)
    cs = _fdot(tri, dt * a)
    dec = jnp.exp(cs[SSM_CHUNK - 1:SSM_CHUNK, :] - cs)
    ecs = jnp.exp(cs)
    off = _iota((LANES, SSM_INNER), 1) - SSM_HEAD_DIM * _iota((LANES, SSM_INNER), 0)
    expand = jnp.logical_and(off >= 0, off < SSM_HEAD_DIM).astype(F32)
    return a, dt, cs, dec, ecs, expand, row, col


def _ssd_specs(t):
    nc = t // SSM_CHUNK
    xs = pl.BlockSpec((SSM_CHUNK, SSM_INNER), lambda c: (c, 0))
    bm = pl.BlockSpec((SSM_CHUNK, SSM_BC), lambda c: (c, SSM_INNER // SSM_BC))
    cm = pl.BlockSpec((SSM_CHUNK, SSM_BC), lambda c: (c, SSM_INNER // SSM_BC + 1))
    dtr = pl.BlockSpec((SSM_CHUNK, LANES), lambda c: (c, COL_DT // LANES))
    vec = pl.BlockSpec((1, LANES), lambda c: (0, 0))
    wide = pl.BlockSpec((1, SSM_INNER), lambda c: (0, 0))
    return nc, xs, bm, cm, dtr, vec, wide


def _ssd_fwd(xbc_act, proj, bias, alog, dexp):
    t = proj.shape[0]
    nc, xs_s, bm_s, cm_s, dtr_s, vec, wide = _ssd_specs(t)

    def body(xs_ref, b_ref, c_ref, dtr_ref, bias_ref, alog_ref, dexp_ref, y_ref, sin_ref, state):
        @pl.when(pl.program_id(0) == 0)
        def _():
            state[...] = jnp.zeros_like(state)

        a, dt, cs, dec, ecs, expand, row, col = _ssd_chunk_terms(dtr_ref[...], bias_ref[...], alog_ref[...])
        cst = cs.T
        dt_x = _fdot(dt, expand)
        dec_x = _fdot(dec, expand)
        ecs_x = _fdot(ecs, expand)
        xs = xs_ref[...]
        xdt = xs * dt_x
        xdec = xdt * dec_x
        lane_lo = col < SSM_HEAD_DIM
        causal = row >= col
        sin_ref[0] = state[...]
        for g in range(SSM_GROUPS):
            bg = b_ref[:, g * SSM_STATE:(g + 1) * SSM_STATE].astype(BF16)
            cg = c_ref[:, g * SSM_STATE:(g + 1) * SSM_STATE].astype(BF16)
            cb = _bdot(cg, bg, NT)
            for q in range(SSM_PAIRS // SSM_GROUPS):
                pq = g * (SSM_PAIRS // SSM_GROUPS) + q
                sl = slice(pq * LANES, (pq + 1) * LANES)
                xp = xdt[:, sl].astype(BF16)
                yd = []
                for hh in range(2):
                    h = 2 * pq + hh
                    lmat = jnp.exp(jnp.where(causal, cs[:, h:h + 1] - cst[h:h + 1, :], -jnp.inf))
                    yd.append(_bdot(cb * lmat, xp))
                s_in = state[pq]
                y_off = _bdot(cg, s_in) * ecs_x[:, sl]
                y_ref[:, sl] = jnp.where(lane_lo, yd[0], yd[1]) + y_off + xs[:, sl] * dexp_ref[:, sl]
                state[pq] = s_in * ecs_x[SSM_CHUNK - 1:SSM_CHUNK, sl] + _bdot(bg, xdec[:, sl], TN)

    return pl.pallas_call(
        body, name="ssd_fwd", grid=(nc,),
        in_specs=[xs_s, bm_s, cm_s, dtr_s, vec, vec, wide],
        out_specs=[pl.BlockSpec((SSM_CHUNK, SSM_INNER), lambda c: (c, 0)),
                   pl.BlockSpec((1, SSM_PAIRS, SSM_STATE, LANES), lambda c: (c, 0, 0, 0))],
        out_shape=[jax.ShapeDtypeStruct((t, SSM_INNER), F32),
                   jax.ShapeDtypeStruct((nc, SSM_PAIRS, SSM_STATE, LANES), F32)],
        scratch_shapes=[pltpu.VMEM((SSM_PAIRS, SSM_STATE, LANES), F32)],
        compiler_params=_params(("arbitrary",)),
    )(xbc_act, xbc_act, xbc_act, proj, bias, alog, dexp)


def _ssd_bwd(xbc_act, proj, s_in_all, dy, bias, alog, dexp):
    t = proj.shape[0]
    nc = t // SSM_CHUNK
    last = nc - 1
    xs_s = pl.BlockSpec((SSM_CHUNK, SSM_INNER), lambda c: (last - c, 0))
    bm_s = pl.BlockSpec((SSM_CHUNK, SSM_BC), lambda c: (last - c, SSM_INNER // SSM_BC))
    cm_s = pl.BlockSpec((SSM_CHUNK, SSM_BC), lambda c: (last - c, SSM_INNER // SSM_BC + 1))
    dtr_s = pl.BlockSpec((SSM_CHUNK, LANES), lambda c: (last - c, COL_DT // LANES))
    sin_s = pl.BlockSpec((1, SSM_PAIRS, SSM_STATE, LANES), lambda c: (last - c, 0, 0, 0))
    vec = pl.BlockSpec((1, LANES), lambda c: (0, 0))
    wide = pl.BlockSpec((1, SSM_INNER), lambda c: (0, 0))

    def body(xs_ref, b_ref, c_ref, dtr_ref, sin_ref, dy_ref, bias_ref, alog_ref, dexp_ref,
             dxbc_ref, ddtr_ref, dbias_ref, dalog_ref, ddcol_ref, dstate, dxdt_s, yoff_s, rx_s, trow_s):
        @pl.when(pl.program_id(0) == 0)
        def _():
            dstate[...] = jnp.zeros_like(dstate)
            trow_s[...] = jnp.zeros_like(trow_s)
            dbias_ref[...] = jnp.zeros_like(dbias_ref)
            dalog_ref[...] = jnp.zeros_like(dalog_ref)
            ddcol_ref[...] = jnp.zeros_like(ddcol_ref)

        dtr = dtr_ref[...]
        a, dt, cs, dec, ecs, expand, row, col = _ssd_chunk_terms(dtr, bias_ref[...], alog_ref[...])
        cst = cs.T
        dt_x = _fdot(dt, expand)
        dec_x = _fdot(dec, expand)
        ecs_x = _fdot(ecs, expand)
        xs = xs_ref[...]
        dyv = dy_ref[...]
        xdt = xs * dt_x
        lane_lo = col < SSM_HEAD_DIM
        causal = row >= col
        ddcol_ref[...] += jnp.sum(dyv * xs, axis=0, keepdims=True)
        dcs_col = jnp.zeros((SSM_CHUNK, LANES), F32)
        dcs_row = jnp.zeros((LANES, SSM_CHUNK), F32)
        for g in range(SSM_GROUPS):
            bg = b_ref[:, g * SSM_STATE:(g + 1) * SSM_STATE].astype(BF16)
            cg = c_ref[:, g * SSM_STATE:(g + 1) * SSM_STATE].astype(BF16)
            cb = _bdot(cg, bg, NT)
            dgm = jnp.zeros((SSM_CHUNK, SSM_CHUNK), F32)
            dbg = jnp.zeros((SSM_CHUNK, SSM_STATE), F32)
            dcg = jnp.zeros((SSM_CHUNK, SSM_STATE), F32)
            for q in range(SSM_PAIRS // SSM_GROUPS):
                pq = g * (SSM_PAIRS // SSM_GROUPS) + q
                sl = slice(pq * LANES, (pq + 1) * LANES)
                dyp = dyv[:, sl]
                xp = xdt[:, sl]
                dxh = []
                for hh in range(2):
                    h = 2 * pq + hh
                    lmat = jnp.exp(jnp.where(causal, cs[:, h:h + 1] - cst[h:h + 1, :], -jnp.inf))
                    mmat = cb * lmat
                    dyh = jnp.where(lane_lo if hh == 0 else jnp.logical_not(lane_lo), dyp, 0.0)
                    dmm = _bdot(dyh, xp, NT)
                    pm = dmm * mmat
                    dcs_col = jnp.where(col == h, jnp.sum(pm, axis=1, keepdims=True), dcs_col)
                    dcs_row = jnp.where(row == h, jnp.sum(pm, axis=0, keepdims=True), dcs_row)
                    dgm = dgm + dmm * lmat
                    dxh.append(_bdot(mmat, dyp, TN))
                s_in = sin_ref[0, pq]
                ecs_p = ecs_x[:, sl]
                dec_p = dec_x[:, sl]
                etot_p = ecs_x[SSM_CHUNK - 1:SSM_CHUNK, sl]
                yoff_s[:, sl] = dyp * (_bdot(cg, s_in) * ecs_p)
                dq = dyp * ecs_p
                dcg = dcg + _bdot(dq, s_in, NT)
                ds = dstate[pq]
                r = _bdot(bg, ds)
                rx_s[:, sl] = r * xp
                dxdt_s[:, sl] = jnp.where(lane_lo, dxh[0], dxh[1]) + dec_p * r
                dbg = dbg + _bdot(xp * dec_p, ds, NT)
                trow_s[0:1, sl] = jnp.sum(ds * s_in, axis=0, keepdims=True) * etot_p
                dstate[pq] = etot_p * ds + _bdot(cg, dq, TN)
            dcg = dcg + _bdot(dgm, bg)
            dbg = dbg + _bdot(dgm, cg, TN)
            dxbc_ref[:, SSM_INNER + g * SSM_STATE:SSM_INNER + (g + 1) * SSM_STATE] = dbg
            dxbc_ref[:, SSM_INNER + SSM_BC + g * SSM_STATE:SSM_INNER + SSM_BC + (g + 1) * SSM_STATE] = dcg
        ddec = _fdot(rx_s[...], expand, NT) * dec
        dtot = _fdot(trow_s[...], expand, NT)[0:1, :]
        dcs = dcs_col - dcs_row.T + _fdot(yoff_s[...], expand, NT) - ddec
        dcs = dcs + jnp.where(row == SSM_CHUNK - 1, jnp.sum(ddec, axis=0, keepdims=True) + dtot, 0.0)
        da = _fdot((row <= col).astype(F32), dcs)
        dxdt = dxdt_s[...]
        ddt = da * a + _fdot(dxdt * xs, expand, NT)
        dalog_ref[...] += jnp.sum(da * dt, axis=0, keepdims=True) * a
        ddtr = ddt * _sigmoid(dtr + bias_ref[...])
        ddtr_ref[...] = ddtr
        dbias_ref[...] += jnp.sum(ddtr, axis=0, keepdims=True)
        dxbc_ref[:, 0:SSM_INNER] = dxdt * dt_x + dyv * dexp_ref[...]

    return pl.pallas_call(
        body, name="ssd_bwd", grid=(nc,),
        in_specs=[xs_s, bm_s, cm_s, dtr_s, sin_s, pl.BlockSpec((SSM_CHUNK, SSM_INNER), lambda c: (last - c, 0)),
                  vec, vec, wide],
        out_specs=[pl.BlockSpec((SSM_CHUNK, SSM_CONV_DIM), lambda c: (last - c, 0)),
                   pl.BlockSpec((SSM_CHUNK, LANES), lambda c: (last - c, 0)), vec, vec, wide],
        out_shape=[jax.ShapeDtypeStruct((t, SSM_CONV_DIM), F32), jax.ShapeDtypeStruct((t, LANES), F32),
                   jax.ShapeDtypeStruct((1, LANES), F32), jax.ShapeDtypeStruct((1, LANES), F32),
                   jax.ShapeDtypeStruct((1, SSM_INNER), F32)],
        scratch_shapes=[pltpu.VMEM((SSM_PAIRS, SSM_STATE, LANES), F32),
                        pltpu.VMEM((SSM_CHUNK, SSM_INNER), F32), pltpu.VMEM((SSM_CHUNK, SSM_INNER), F32),
                        pltpu.VMEM((SSM_CHUNK, SSM_INNER), F32), pltpu.VMEM((SUBLANES, SSM_INNER), F32)],
        compiler_params=_params(("arbitrary",)),
    )(xbc_act, xbc_act, xbc_act, proj, s_in_all, dy, bias, alog, dexp)


def _group_rstd(y):
    n = SSM_INNER // SSM_GROUPS
    parts = []
    for g in range(SSM_GROUPS):
        yg = y[:, g * n:(g + 1) * n]
        r = lax.rsqrt(jnp.mean(yg * yg, axis=-1, keepdims=True) + RMS_EPS)
        parts.append(jnp.broadcast_to(r, yg.shape))
    return jnp.concatenate(parts, axis=1)


def _group_mean(v):
    n = SSM_INNER // SSM_GROUPS
    parts = []
    for g in range(SSM_GROUPS):
        vg = v[:, g * n:(g + 1) * n]
        parts.append(jnp.broadcast_to(jnp.mean(vg, axis=-1, keepdims=True), vg.shape))
    return jnp.concatenate(parts, axis=1)


def _ssm_post_fwd(y_ssd, proj, nw):
    t = proj.shape[0]
    n = SSM_INNER
    tr = _rows(t)

    def body(y_ref, z_ref, nw_ref, o_ref):
        z = z_ref[...]
        y = y_ref[...] * (z * _sigmoid(z))
        o_ref[...] = y * _group_rstd(y) * nw_ref[...]

    row = pl.BlockSpec((tr, n), lambda i: (i, 0))
    return pl.pallas_call(
        body, name="ssm_post_fwd", grid=(t // tr,),
        in_specs=[row, pl.BlockSpec((tr, n), lambda i: (i, COL_Z // n)), pl.BlockSpec((1, n), lambda i: (0, 0))],
        out_specs=row, out_shape=jax.ShapeDtypeStruct((t, n), F32),
        compiler_params=_params(("parallel",)),
    )(y_ssd, proj, nw)


def _ssm_post_bwd(y_ssd, proj, nw, dout):
    t = proj.shape[0]
    n = SSM_INNER
    tr = _rows(t)

    def body(y_ref, z_ref, nw_ref, do_ref, dy_ref, dz_ref, dnw_ref):
        i = pl.program_id(0)
        z = z_ref[...]
        sg = _sigmoid(z)
        sz = z * sg
        ys = y_ref[...]
        y = ys * sz
        rstd = _group_rstd(y)
        yn = y * rstd
        dov = do_ref[...]
        dyn = dov * nw_ref[...]
        dyg = rstd * (dyn - yn * _group_mean(dyn * yn))
        dy_ref[...] = dyg * sz
        dz_ref[...] = dyg * ys * (sg * (1.0 + z * (1.0 - sg)))
        part = jnp.sum(dov * yn, axis=0, keepdims=True)

        @pl.when(i == 0)
        def _():
            dnw_ref[...] = part

        @pl.when(i > 0)
        def _():
            dnw_ref[...] += part

    row = pl.BlockSpec((tr, n), lambda i: (i, 0))
    vec = pl.BlockSpec((1, n), lambda i: (0, 0))
    return pl.pallas_call(
        body, name="ssm_post_bwd", grid=(t // tr,),
        in_specs=[row, pl.BlockSpec((tr, n), lambda i: (i, COL_Z // n)), vec, row],
        out_specs=[row, row, vec],
        out_shape=[jax.ShapeDtypeStruct((t, n), F32), jax.ShapeDtypeStruct((t, n), F32),
                   jax.ShapeDtypeStruct((1, n), F32)],
        compiler_params=_params(("arbitrary",)),
    )(y_ssd, proj, nw, dout)


GELU_C = math.sqrt(2.0 / math.pi)
GELU_K = 0.044715


def _gelu_parts(y):
    th = jnp.tanh(GELU_C * (y + GELU_K * y * y * y))
    val = 0.5 * y * (1.0 + th)
    grad = 0.5 * (1.0 + th) + 0.5 * y * (1.0 - th * th) * GELU_C * (1.0 + 3.0 * GELU_K * y * y)
    return val, grad


def _scan_tiles(a_ref, b_ref, h_ref, n_rows, reverse):
    n_tiles = n_rows // SUBLANES
    shape = (SUBLANES, a_ref.shape[1])
    row = _iota(shape, 0)

    def step(k, carry):
        tile = (n_tiles - 1 - k) if reverse else k
        at = pl.ds(pl.multiple_of(tile * SUBLANES, SUBLANES), SUBLANES)
        av = a_ref[at, :]
        bv = b_ref[at, :]
        for s in (1, 2, 4):
            if reverse:
                keep = row < SUBLANES - s
                a_sh = jnp.where(keep, pltpu.roll(av, SUBLANES - s, 0), 1.0)
                b_sh = jnp.where(keep, pltpu.roll(bv, SUBLANES - s, 0), 0.0)
            else:
                keep = row >= s
                a_sh = jnp.where(keep, pltpu.roll(av, s, 0), 1.0)
                b_sh = jnp.where(keep, pltpu.roll(bv, s, 0), 0.0)
            bv = av * b_sh + bv
            av = av * a_sh
        hv = bv + av * carry
        h_ref[at, :] = hv
        return hv[0:1, :] if reverse else hv[SUBLANES - 1:SUBLANES, :]

    lax.fori_loop(0, n_tiles, step, jnp.zeros((1, a_ref.shape[1]), F32))


def _lru_gates(xl, cw, cb, wr, br, wi, bi, lam):
    u = cb + cw[CONV_K - 1:CONV_K, :] * xl
    for k in range(CONV_K - 1):
        u = u + cw[k:k + 1, :] * _shift_down(xl, CONV_K - 1 - k)
    r = _sigmoid(_bdot(u, wr) + br)
    i = _sigmoid(_bdot(u, wi) + bi)
    sp = _softplus(-lam)
    la = -LRU_C * r * sp
    a = jnp.exp(la)
    mult = jnp.sqrt(-jnp.tanh(la) * (a * a + 1.0))
    return u, r, i, sp, a, mult


def _lru_specs(t):
    c_x = COL_LRU_X // LANES
    c_y = COL_LRU_Y // LANES
    xl = pl.BlockSpec((t, LANES), lambda j: (0, c_x + j))
    yl = pl.BlockSpec((t, LANES), lambda j: (0, c_y + j))
    col = pl.BlockSpec((t, LANES), lambda j: (0, j))
    cw = pl.BlockSpec((CONV_K, LANES), lambda j: (0, j))
    vec = pl.BlockSpec((1, LANES), lambda j: (0, j))
    wblk = pl.BlockSpec((1, LANES, LANES), lambda j: (j, 0, 0))
    return xl, yl, col, cw, vec, wblk


def _lru_fwd(proj, cw, cb, wr, br, wi, bi, lam):
    t = proj.shape[0]
    xl_s, yl_s, col, cw_s, vec, wblk = _lru_specs(t)

    def body(xl_ref, yl_ref, cw_ref, cb_ref, wr_ref, br_ref, wi_ref, bi_ref, lam_ref, o_ref, a_s, b_s, h_s):
        u, r, i, sp, a, mult = _lru_gates(xl_ref[...], cw_ref[...], cb_ref[...], wr_ref[0], br_ref[...],
                                          wi_ref[0], bi_ref[...], lam_ref[...])
        a_s[...] = a
        b_s[...] = mult * (i * u)
        _scan_tiles(a_s, b_s, h_s, t, reverse=False)
        o_ref[...] = h_s[...] * _gelu_parts(yl_ref[...])[0]

    return pl.pallas_call(
        body, name="lru_fwd", grid=(LRU_BLOCKS,),
        in_specs=[xl_s, yl_s, cw_s, vec, wblk, vec, wblk, vec, vec],
        out_specs=col, out_shape=jax.ShapeDtypeStruct((t, LRU_WIDTH), F32),
        scratch_shapes=[pltpu.VMEM((t, LANES), F32)] * 3,
        compiler_params=_params(("arbitrary",), big=True),
    )(proj, proj, cw, cb, wr, br, wi, bi, lam)


def _lru_bwd(proj, cw, cb, wr, br, wi, bi, lam, dout):
    t = proj.shape[0]
    xl_s, yl_s, col, cw_s, vec, wblk = _lru_specs(t)

    def body(xl_ref, yl_ref, cw_ref, cb_ref, wr_ref, br_ref, wi_ref, bi_ref, lam_ref, do_ref,
             dxl_ref, dyl_ref, dcw_ref, dcb_ref, dwr_ref, dbr_ref, dwi_ref, dbi_ref, dlam_ref,
             a_s, b_s, h_s, g_s):
        xl = xl_ref[...]
        cwv = cw_ref[...]
        lam = lam_ref[...]
        u, r, i, sp, a, mult = _lru_gates(xl, cwv, cb_ref[...], wr_ref[0], br_ref[...], wi_ref[0], bi_ref[...], lam)
        v = i * u
        a_s[...] = a
        b_s[...] = mult * v
        _scan_tiles(a_s, b_s, h_s, t, reverse=False)
        gl, dgl = _gelu_parts(yl_ref[...])
        dov = do_ref[...]
        h = h_s[...]
        dyl_ref[...] = dov * h * dgl
        b_s[...] = dov * gl
        a_s[...] = _shift_up(a, 1)
        _scan_tiles(a_s, b_s, g_s, t, reverse=True)
        g = g_s[...]
        da = g * _shift_down(h, 1)
        dmult = g * v
        dv = g * mult
        dla = da * a - dmult * (a * a) / mult
        dr = dla * (-LRU_C * sp)
        dsp = jnp.sum(dla * (-LRU_C * r), axis=0, keepdims=True)
        dlam_ref[...] = -dsp * _sigmoid(-lam)
        dpr = dr * r * (1.0 - r)
        dpi = dv * u * i * (1.0 - i)
        dbr_ref[...] = jnp.sum(dpr, axis=0, keepdims=True)
        dbi_ref[...] = jnp.sum(dpi, axis=0, keepdims=True)
        dwr_ref[0] = _bdot(u, dpr, TN)
        dwi_ref[0] = _bdot(u, dpi, TN)
        du = dv * i + _bdot(dpr, wr_ref[0], NT) + _bdot(dpi, wi_ref[0], NT)
        dxl = cwv[CONV_K - 1:CONV_K, :] * du
        for k in range(CONV_K - 1):
            dxl = dxl + cwv[k:k + 1, :] * _shift_up(du, CONV_K - 1 - k)
        dxl_ref[...] = dxl
        for k in range(CONV_K):
            dcw_ref[k:k + 1, :] = jnp.sum(du * _shift_down(xl, CONV_K - 1 - k), axis=0, keepdims=True)
        dcb_ref[...] = jnp.sum(du, axis=0, keepdims=True)

    return pl.pallas_call(
        body, name="lru_bwd", grid=(LRU_BLOCKS,),
        in_specs=[xl_s, yl_s, cw_s, vec, wblk, vec, wblk, vec, vec, col],
        out_specs=[col, col, cw_s, vec, wblk, vec, wblk, vec, vec],
        out_shape=[jax.ShapeDtypeStruct((t, LRU_WIDTH), F32), jax.ShapeDtypeStruct((t, LRU_WIDTH), F32),
                   jax.ShapeDtypeStruct((CONV_K, LRU_WIDTH), F32), jax.ShapeDtypeStruct((1, LRU_WIDTH), F32),
                   jax.ShapeDtypeStruct((LRU_BLOCKS, LANES, LANES), F32), jax.ShapeDtypeStruct((1, LRU_WIDTH), F32),
                   jax.ShapeDtypeStruct((LRU_BLOCKS, LANES, LANES), F32), jax.ShapeDtypeStruct((1, LRU_WIDTH), F32),
                   jax.ShapeDtypeStruct((1, LRU_WIDTH), F32)],
        scratch_shapes=[pltpu.VMEM((t, LANES), F32)] * 4,
        compiler_params=_params(("arbitrary",), big=True),
    )(proj, proj, cw, cb, wr, br, wi, bi, lam, dout)


def _mesh_pos():
    return lax.axis_index("x"), lax.axis_index("y"), lax.axis_index("c")


def _all_gather(v, name):
    def body(v_ref, out_ref, send_sems, recv_sems, local_sem):
        x, y, c = _mesh_pos()
        me, sibling = (x, y, c), (x, y, 1 - c)
        chips = [(1 - x, y), (x, 1 - y), (1 - x, 1 - y)]

        def block(px, py, pc):
            return out_ref.at[4 * px + 2 * py + pc]

        def copy(k, blk, to, src=None):
            return pltpu.make_async_remote_copy(
                src_ref=block(*blk) if src is None else src, dst_ref=block(*blk),
                send_sem=send_sems.at[k], recv_sem=recv_sems.at[k], device_id=to, device_id_type=MESH)

        mine = pltpu.make_async_copy(v_ref, block(*me), local_sem)
        mine.start()
        first = [copy(0, me, sibling, src=v_ref)]
        first += [copy(1 + j, me, (*chip, c), src=v_ref) for j, chip in enumerate(chips)]
        for cp in first:
            cp.start()
        passed = [copy(4 + j, (*chip, c), sibling) for j, chip in enumerate(chips)]
        for j, chip in enumerate(chips):
            copy(1 + j, (*chip, c), me).wait_recv()
            passed[j].start()
        copy(0, sibling, me).wait_recv()
        for j, chip in enumerate(chips):
            copy(4 + j, (*chip, 1 - c), me).wait_recv()
        for cp in first + passed:
            cp.wait_send()
        mine.wait()

    return pl.pallas_call(
        body, name=name,
        out_shape=jax.ShapeDtypeStruct((N_DEV,) + v.shape, v.dtype),
        in_specs=[pl.BlockSpec(memory_space=pl.ANY)], out_specs=pl.BlockSpec(memory_space=pl.ANY),
        scratch_shapes=[pltpu.SemaphoreType.DMA((7,)), pltpu.SemaphoreType.DMA((7,)), pltpu.SemaphoreType.DMA],
    )(v)


def _exchange(parts, name):
    def body(p_ref, out_ref, send_sems, recv_sems, local_sem):
        x, y, c = _mesh_pos()
        me = 4 * x + 2 * y + c
        mine = pltpu.make_async_copy(p_ref.at[me], out_ref.at[me], local_sem)
        mine.start()
        copies = []
        for k in range(1, N_DEV):
            px = (1 - x) if k & 4 else x
            py = (1 - y) if k & 2 else y
            pc = (1 - c) if k & 1 else c
            cp = pltpu.make_async_remote_copy(
                src_ref=p_ref.at[4 * px + 2 * py + pc], dst_ref=out_ref.at[me],
                send_sem=send_sems.at[k - 1], recv_sem=recv_sems.at[k - 1],
                device_id=(px, py, pc), device_id_type=MESH)
            cp.start()
            copies.append(cp)
        for cp in copies:
            cp.wait()
        mine.wait()

    return pl.pallas_call(
        body, name=name,
        out_shape=jax.ShapeDtypeStruct(parts.shape, parts.dtype),
        in_specs=[pl.BlockSpec(memory_space=pl.ANY)], out_specs=pl.BlockSpec(memory_space=pl.ANY),
        scratch_shapes=[pltpu.SemaphoreType.DMA((7,)), pltpu.SemaphoreType.DMA((7,)), pltpu.SemaphoreType.DMA],
    )(parts)


def _sum_sources(recv, tile, name):
    n, rows, width = recv.shape

    def body(r_ref, o_ref):
        acc = r_ref[0].astype(F32)
        for s in range(1, n):
            acc = acc + r_ref[s].astype(F32)
        o_ref[...] = acc

    return pl.pallas_call(
        body, name=name, grid=(rows // tile,),
        in_specs=[pl.BlockSpec((n, tile, width), lambda i: (0, i, 0))],
        out_specs=pl.BlockSpec((tile, width), lambda i: (i, 0)),
        out_shape=jax.ShapeDtypeStruct((rows, width), F32),
        compiler_params=_params(("parallel",)),
    )(recv)


def _adamw(w, g, m, v, tile, name):
    rows, width = w.shape
    c1 = 1.0 - ADAM_B1 ** ADAM_STEP
    c2 = 1.0 - ADAM_B2 ** ADAM_STEP

    def body(w_ref, g_ref, m_ref, v_ref, d_ref, nm_ref, nv_ref):
        gv = g_ref[...]
        nm = ADAM_B1 * m_ref[...] + (1.0 - ADAM_B1) * gv
        nv = ADAM_B2 * v_ref[...] + (1.0 - ADAM_B2) * (gv * gv)
        nm_ref[...] = nm
        nv_ref[...] = nv
        d_ref[...] = -ADAM_LR * ((nm / c1) / (jnp.sqrt(nv / c2) + ADAM_EPS) + ADAM_WD * w_ref[...])

    spec = pl.BlockSpec((tile, width), lambda i: (i, 0))
    shape = jax.ShapeDtypeStruct((rows, width), F32)
    return pl.pallas_call(
        body, name=name, grid=(rows // tile,),
        in_specs=[spec] * 4, out_specs=[spec] * 3, out_shape=[shape] * 3,
        compiler_params=_params(("parallel",)),
    )(w, g, m, v)


BIG_NAMES = ("w_in", "w_out_ssm", "w_out_lru", "w_out", "w_ffn_in", "w_ffn_out", "ssm_conv_w", "lru_conv_w")
SMALL_NAMES = ("norm1_w", "b_branch_gate", "ssm_conv_b", "ssm_dt_bias", "ssm_a_log", "ssm_d", "ssm_norm_w",
               "lru_conv_b", "lru_w_r", "lru_b_r", "lru_w_i", "lru_b_i", "lru_lambda", "norm2_w", "norm_f_w")


def _pack(arrays, total):
    flat = []
    used = 0
    for a in arrays:
        a = a.reshape(-1)
        pad = (-a.shape[0]) % LANES
        flat.append(a)
        if pad:
            flat.append(jnp.zeros((pad,), a.dtype))
        used += a.shape[0] + pad
    assert used <= total, (used, total)
    if total > used:
        flat.append(jnp.zeros((total - used,), arrays[0].dtype))
    return jnp.concatenate(flat)


def _unpack(flat, shapes):
    out = []
    off = 0
    for shp in shapes:
        n = math.prod(shp)
        out.append(flat[off:off + n].reshape(shp))
        off += n + (-n) % LANES
    return out


def _col_shards(full):
    rows, cols = full.shape
    return full.reshape(rows, N_DEV, cols // N_DEV).transpose(1, 0, 2)


def _row_shards(full):
    rows, cols = full.shape
    return full.reshape(N_DEV, rows // N_DEV, cols)


def _from_col_shards(g):
    n, rows, w = g.shape
    return g.transpose(1, 0, 2).reshape(rows, n * w)


def kernel(x, norm1_w, w_in, b_branch_gate, ssm_conv_w, ssm_conv_b, ssm_dt_bias, ssm_a_log, ssm_d, ssm_norm_w, w_out_ssm, lru_conv_w, lru_conv_b, lru_w_r, lru_b_r, lru_w_i, lru_b_i, lru_lambda, w_out_lru, w_out, norm2_w, w_ffn_in, w_ffn_out, norm_f_w, loss_target, m_norm1_w, m_w_in, m_b_branch_gate, m_ssm_conv_w, m_ssm_conv_b, m_ssm_dt_bias, m_ssm_a_log, m_ssm_d, m_ssm_norm_w, m_w_out_ssm, m_lru_conv_w, m_lru_conv_b, m_lru_w_r, m_lru_b_r, m_lru_w_i, m_lru_b_i, m_lru_lambda, m_w_out_lru, m_w_out, m_norm2_w, m_w_ffn_in, m_w_ffn_out, m_norm_f_w, v_norm1_w, v_w_in, v_b_branch_gate, v_ssm_conv_w, v_ssm_conv_b, v_ssm_dt_bias, v_ssm_a_log, v_ssm_d, v_ssm_norm_w, v_w_out_ssm, v_lru_conv_w, v_lru_conv_b, v_lru_w_r, v_lru_b_r, v_lru_w_i, v_lru_b_i, v_lru_lambda, v_w_out_lru, v_w_out, v_norm2_w, v_w_ffn_in, v_w_ffn_out, v_norm_f_w):
    given = dict(locals())
    weights = {n: given[n] for n in BIG_NAMES + SMALL_NAMES}
    t = x.shape[1]
    xt = x[0]
    tgt = loss_target[0]

    mat_names = BIG_NAMES[:6]
    mat_rows = [math.prod(weights[n].shape) // FLAT_W for n in mat_names]
    mats = jnp.concatenate([weights[n].astype(BF16).reshape(-1, FLAT_W) for n in mat_names], axis=0)
    mats = _all_gather(mats, "gather_matmul_weights")
    offs = [0]
    for r in mat_rows:
        offs.append(offs[-1] + r)
    full = {}
    for i, n in enumerate(mat_names):
        blk = mats[:, offs[i]:offs[i + 1], :].reshape((N_DEV,) + weights[n].shape[1:])
        full[n] = _from_col_shards(blk) if n in ("w_in", "w_ffn_in") else blk.reshape(-1, blk.shape[-1])
    convs = jnp.concatenate([ssm_conv_w.reshape(-1), lru_conv_w.reshape(-1)])
    n_ssm_cw = ssm_conv_w.size
    convs = _all_gather(_pack([convs], 24 * LANES).reshape(24, LANES), "gather_conv_weights").reshape(N_DEV, -1)
    ssm_cw = _from_col_shards(convs[:, :n_ssm_cw].reshape(N_DEV, CONV_K, -1))
    lru_cw = _from_col_shards(convs[:, n_ssm_cw:n_ssm_cw + lru_conv_w.size].reshape(N_DEV, CONV_K, -1))
    wi_full = full["w_in"]
    w_p = jnp.concatenate([wi_full[:, :ORIG_DT], wi_full[:, ORIG_LRU:], wi_full[:, ORIG_DT:ORIG_LRU],
                           jnp.zeros((D_MODEL, PROJ_W - IN_PROJ), BF16)], axis=1)

    def pad_heads(a):
        return jnp.pad(a.reshape(1, SSM_HEADS), ((0, 0), (0, LANES - SSM_HEADS)))

    dt_bias_p = pad_heads(ssm_dt_bias)
    a_log_p = pad_heads(ssm_a_log)
    d_exp = jnp.repeat(ssm_d.reshape(SSM_HEADS), SSM_HEAD_DIM).reshape(1, SSM_INNER)
    lru_wr, lru_wi = lru_w_r[0], lru_w_i[0]

    hn1 = _rmsnorm_fwd(xt, norm1_w, "norm1_fwd")
    proj = _mm(hn1, w_p, name="in_proj")
    xbc_act = _ssm_conv_fwd(proj, ssm_cw, ssm_conv_b)
    y_ssd, s_in_all = _ssd_fwd(xbc_act, proj, dt_bias_p, a_log_p, d_exp)
    y_pre = _ssm_post_fwd(y_ssd, proj, ssm_norm_w)
    y_ssm = _mm(y_pre, full["w_out_ssm"], name="out_ssm")
    l_out = _lru_fwd(proj, lru_cw, lru_conv_b, lru_wr, lru_b_r, lru_wi, lru_b_i, lru_lambda)
    y_lru = _mm(l_out, full["w_out_lru"], name="out_lru")
    merged = _merge_fwd(proj, b_branch_gate, y_ssm, y_lru)
    h1 = _mm(merged, full["w_out"], add=xt, name="out_proj")
    hn2 = _rmsnorm_fwd(h1, norm2_w, "norm2_fwd")
    gu = _mm(hn2, full["w_ffn_in"], name="ffn_in")
    act = _swiglu_fwd(gu)
    h2 = _mm(act, full["w_ffn_out"], add=h1, name="ffn_out")

    grads = {}
    dh2, grads["norm_f_w"], loss_cols = _loss_head(h2, norm_f_w.reshape(1, D_MODEL), tgt)
    loss = lax.psum(0.5 * jnp.sum(loss_cols) / D_MODEL, AXES)
    dact = _mm(dh2, full["w_ffn_out"], tb=True, name="d_act")
    grads["w_ffn_out"] = _mm(act, dh2, ta=True, name="dw_ffn_out")
    dgu = _swiglu_bwd(gu, dact)
    dhn2 = _mm(dgu, full["w_ffn_in"], tb=True, name="d_hn2")
    grads["w_ffn_in"] = _mm(hn2, dgu, ta=True, name="dw_ffn_in")
    dh1, grads["norm2_w"] = _rmsnorm_bwd(h1, norm2_w, dhn2, dh2, "norm2_bwd")
    dmerged = _mm(dh1, full["w_out"], tb=True, name="d_merged")
    grads["w_out"] = _mm(merged, dh1, ta=True, name="dw_out")
    dy_ssm, dy_lru, dgates, grads["b_branch_gate"] = _merge_bwd(proj, b_branch_gate, y_ssm, y_lru, dmerged)
    dy_pre = _mm(dy_ssm, full["w_out_ssm"], tb=True, name="d_y_pre")
    grads["w_out_ssm"] = _mm(y_pre, dy_ssm, ta=True, name="dw_out_ssm")
    dl_out = _mm(dy_lru, full["w_out_lru"], tb=True, name="d_l_out")
    grads["w_out_lru"] = _mm(l_out, dy_lru, ta=True, name="dw_out_lru")
    dy_ssd, dz, grads["ssm_norm_w"] = _ssm_post_bwd(y_ssd, proj, ssm_norm_w, dy_pre)
    dxbc_act, ddt, dbias, dalog, ddcol = _ssd_bwd(xbc_act, proj, s_in_all, dy_ssd, dt_bias_p, a_log_p, d_exp)
    grads["ssm_dt_bias"] = dbias[:, :SSM_HEADS]
    grads["ssm_a_log"] = dalog[:, :SSM_HEADS]
    grads["ssm_d"] = ddcol.reshape(SSM_HEADS, SSM_HEAD_DIM).sum(axis=1).reshape(1, SSM_HEADS)
    dxbc, grads["ssm_conv_w"], grads["ssm_conv_b"] = _ssm_conv_bwd(proj, ssm_cw, ssm_conv_b, dxbc_act)
    (dxl, dyl, grads["lru_conv_w"], grads["lru_conv_b"], dwr, grads["lru_b_r"], dwi, grads["lru_b_i"],
     grads["lru_lambda"]) = _lru_bwd(proj, lru_cw, lru_conv_b, lru_wr, lru_b_r, lru_wi, lru_b_i, lru_lambda, dl_out)
    grads["lru_w_r"], grads["lru_w_i"] = dwr[None], dwi[None]
    dproj = jnp.concatenate([dgates, dz, dxbc, dxl, dyl, ddt], axis=1)
    dhn1 = _mm(dproj, w_p, tb=True, name="d_hn1")
    dwp = _mm(hn1, dproj, ta=True, name="dw_in")
    grads["w_in"] = jnp.concatenate([dwp[:, :ORIG_DT], dwp[:, COL_DT:COL_DT + SSM_HEADS], dwp[:, COL_LRU_X:COL_DT]],
                                    axis=1)
    grad_x, grads["norm1_w"] = _rmsnorm_bwd(xt, norm1_w, dhn1, dh1, "norm1_bwd")

    col_sharded = ("w_in", "w_ffn_in", "ssm_conv_w", "lru_conv_w")
    slices = [(_col_shards if n in col_sharded else _row_shards)(grads[n]).astype(BF16).reshape(N_DEV, -1)
              for n in BIG_NAMES]
    used = sum(s.shape[1] for s in slices)
    slices.append(jnp.zeros((N_DEV, BIG_ROWS * FLAT_W - used), BF16))
    big_parts = jnp.concatenate(slices, axis=1).reshape(N_DEV, BIG_ROWS, FLAT_W)
    big_g = _sum_sources(_exchange(big_parts, "exchange_sharded_grads"), BIG_TILE, "sum_sharded_grads")
    small_total = N_DEV * SMALL_ROWS * LANES
    small_parts = _pack([grads[n] for n in SMALL_NAMES], small_total).reshape(N_DEV, SMALL_ROWS, LANES)
    small_g = _sum_sources(_exchange(small_parts, "exchange_small_grads"), SMALL_ROWS, "sum_small_grads")
    small_g = _all_gather(small_g, "gather_small_grads").reshape(small_total // FLAT_W, FLAT_W)

    def packed_state(prefix):
        big = _pack([given[prefix + n] for n in BIG_NAMES], BIG_ROWS * FLAT_W).reshape(BIG_ROWS, FLAT_W)
        small = _pack([given[prefix + n] for n in SMALL_NAMES], small_total).reshape(-1, FLAT_W)
        return big, small

    (w_big, w_small), (m_big, m_small), (v_big, v_small) = packed_state(""), packed_state("m_"), packed_state("v_")
    big_out = _adamw(w_big, big_g, m_big, v_big, BIG_TILE, "adamw_sharded")
    small_out = _adamw(w_small, small_g, m_small, v_small, small_total // FLAT_W, "adamw_replicated")

    big_shapes = [weights[n].shape for n in BIG_NAMES]
    small_shapes = [weights[n].shape for n in SMALL_NAMES]
    order = list(given)[1:24]
    results = []
    for big_flat, small_flat in [(big_g, small_g)] + list(zip(big_out, small_out)):
        vals = dict(zip(BIG_NAMES, _unpack(big_flat.reshape(-1), big_shapes)))
        vals.update(zip(SMALL_NAMES, _unpack(small_flat.reshape(-1), small_shapes)))
        results.extend(vals[n] for n in order)
    return (loss, grad_x[None], *results)
```

```python
import math

import jax
import jax.numpy as jnp
from jax import lax
from jax.experimental import pallas as pl
from jax.experimental.pallas import tpu as pltpu

F32 = jnp.float32
BF16 = jnp.bfloat16
HIGHEST = lax.Precision.HIGHEST
MESH = pl.DeviceIdType.MESH
AXES = ("x", "y", "c")
N_DEV = 8

D_MODEL = 1024
SSM_INNER = 2048
SSM_HEADS = 32
SSM_HEAD_DIM = 64
SSM_GROUPS = 4
SSM_STATE = 128
SSM_BC = SSM_GROUPS * SSM_STATE
SSM_CONV_DIM = SSM_INNER + 2 * SSM_BC
SSM_CHUNK = 128
SSM_PAIRS = SSM_HEADS // 2
CONV_K = 4
LRU_WIDTH = 1280
LRU_BLOCKS = 10
LRU_C = 8.0
FFN_HIDDEN = 2816
RMS_EPS = 1e-6
IN_PROJ = 9760

COL_GATES = 0
COL_Z = 2048
COL_XBC = 4096
COL_LRU_X = 7168
COL_LRU_Y = 8448
COL_DT = 9728
PROJ_W = 9856
ORIG_DT = 7168
ORIG_LRU = 7200

ADAM_LR = 0.001
ADAM_B1 = 0.9
ADAM_B2 = 0.999
ADAM_EPS = 1e-08
ADAM_WD = 0.01
ADAM_STEP = 10

LANES = 128
SUBLANES = 8
V7X_VMEM_BYTES = 64 * 1024 * 1024
VMEM_LIMIT = V7X_VMEM_BYTES * 3 // 4
VMEM_LIMIT_BIG = V7X_VMEM_BYTES * 15 // 16

NT = (((1,), (1,)), ((), ()))
TN = (((0,), (0,)), ((), ()))

FLAT_W = 1024
SMALL_ROWS = 336


def _params(sem=None, big=False):
    return pltpu.CompilerParams(dimension_semantics=sem,
                                vmem_limit_bytes=VMEM_LIMIT_BIG if big else VMEM_LIMIT)


def _blk(dim, cap):
    if dim <= cap:
        return dim
    for m in range(cap // LANES, 0, -1):
        if dim % (m * LANES) == 0:
            return m * LANES
    raise ValueError(f"no block for {dim}")


def _rows(t):
    return min(t, 256)


def _sigmoid(v):
    return 1.0 / (1.0 + jnp.exp(-v))


def _softplus(v):
    e = jnp.exp(-jnp.abs(v))
    u = 1.0 + e
    log1p = jnp.where(u == 1.0, e, jnp.log(u) * e / jnp.where(u == 1.0, 1.0, u - 1.0))
    return jnp.maximum(v, 0.0) + log1p


def _iota(shape, dim):
    return lax.broadcasted_iota(jnp.int32, shape, dim)


def _shift_down(v, s):
    if s == 0:
        return v
    return jnp.where(_iota(v.shape, 0) >= s, pltpu.roll(v, s, 0), 0.0)


def _shift_up(v, s):
    if s == 0:
        return v
    n = v.shape[0]
    return jnp.where(_iota(v.shape, 0) < n - s, pltpu.roll(v, n - s, 0), 0.0)


def _bdot(a, b, dn=None):
    a = a.astype(BF16)
    b = b.astype(BF16)
    if dn is None:
        return jnp.dot(a, b, preferred_element_type=F32)
    return lax.dot_general(a, b, dn, preferred_element_type=F32)


def _fdot(a, b, dn=None):
    if dn is None:
        return jnp.dot(a, b, precision=HIGHEST, preferred_element_type=F32)
    return lax.dot_general(a, b, dn, precision=HIGHEST, preferred_element_type=F32)


def _mm(a, b, *, ta=False, tb=False, add=None, name):
    if ta:
        kdim, m = a.shape
    else:
        m, kdim = a.shape
    if tb:
        n, k2 = b.shape
    else:
        k2, n = b.shape
    assert kdim == k2, (a.shape, b.shape, ta, tb)
    bm, bn, bk = _blk(m, 512), _blk(n, 1408), _blk(kdim, 1408)
    nk = kdim // bk
    dn = (((0 if ta else 1,), (1 if tb else 0,)), ((), ()))

    def body(*refs):
        if add is None:
            a_ref, b_ref, o_ref, acc = refs
        else:
            a_ref, b_ref, r_ref, o_ref, acc = refs
        k = pl.program_id(2)

        @pl.when(k == 0)
        def _():
            acc[...] = jnp.zeros_like(acc)

        acc[...] += lax.dot_general(a_ref[...].astype(BF16), b_ref[...].astype(BF16), dn,
                                    preferred_element_type=F32)

        @pl.when(k == nk - 1)
        def _():
            r = acc[...]
            if add is not None:
                r = r + r_ref[...]
            o_ref[...] = r

    a_spec = pl.BlockSpec((bk, bm), lambda i, j, k: (k, i)) if ta else pl.BlockSpec((bm, bk), lambda i, j, k: (i, k))
    b_spec = pl.BlockSpec((bn, bk), lambda i, j, k: (j, k)) if tb else pl.BlockSpec((bk, bn), lambda i, j, k: (k, j))
    o_spec = pl.BlockSpec((bm, bn), lambda i, j, k: (i, j))
    in_specs = [a_spec, b_spec] + ([o_spec] if add is not None else [])
    args = (a, b) + ((add,) if add is not None else ())
    return pl.pallas_call(
        body, name=name, grid=(m // bm, n // bn, nk),
        in_specs=in_specs, out_specs=o_spec,
        out_shape=jax.ShapeDtypeStruct((m, n), F32),
        scratch_shapes=[pltpu.VMEM((bm, bn), F32)],
        compiler_params=_params(("parallel", "parallel", "arbitrary")),
    )(*args)


def _rmsnorm_fwd(x, w, name):
    t, d = x.shape
    tr = _rows(t)

    def body(x_ref, w_ref, o_ref):
        xv = x_ref[...]
        rstd = lax.rsqrt(jnp.mean(xv * xv, axis=-1, keepdims=True) + RMS_EPS)
        o_ref[...] = xv * rstd * w_ref[...]

    return pl.pallas_call(
        body, name=name, grid=(t // tr,),
        in_specs=[pl.BlockSpec((tr, d), lambda i: (i, 0)), pl.BlockSpec((1, d), lambda i: (0, 0))],
        out_specs=pl.BlockSpec((tr, d), lambda i: (i, 0)),
        out_shape=jax.ShapeDtypeStruct((t, d), F32),
        compiler_params=_params(("parallel",)),
    )(x, w)


def _rmsnorm_bwd(x, w, dy, dres, name):
    t, d = x.shape
    tr = _rows(t)

    def body(x_ref, w_ref, dy_ref, dres_ref, dx_ref, dw_ref):
        i = pl.program_id(0)
        xv = x_ref[...]
        rstd = lax.rsqrt(jnp.mean(xv * xv, axis=-1, keepdims=True) + RMS_EPS)
        xhat = xv * rstd
        dyv = dy_ref[...]
        dxhat = dyv * w_ref[...]
        m = jnp.mean(dxhat * xhat, axis=-1, keepdims=True)
        dx_ref[...] = rstd * (dxhat - xhat * m) + dres_ref[...]
        part = jnp.sum(dyv * xhat, axis=0, keepdims=True)

        @pl.when(i == 0)
        def _():
            dw_ref[...] = part

        @pl.when(i > 0)
        def _():
            dw_ref[...] += part

    row = pl.BlockSpec((tr, d), lambda i: (i, 0))
    vec = pl.BlockSpec((1, d), lambda i: (0, 0))
    return pl.pallas_call(
        body, name=name, grid=(t // tr,),
        in_specs=[row, vec, row, row], out_specs=[row, vec],
        out_shape=[jax.ShapeDtypeStruct((t, d), F32), jax.ShapeDtypeStruct((1, d), F32)],
        compiler_params=_params(("arbitrary",)),
    )(x, w, dy, dres)


def _loss_head(h2, w, tgt):
    t, d = h2.shape
    tr = _rows(t)

    def body(x_ref, w_ref, t_ref, dx_ref, dw_ref, ls_ref):
        i = pl.program_id(0)
        xv = x_ref[...]
        wv = w_ref[...]
        rstd = lax.rsqrt(jnp.mean(xv * xv, axis=-1, keepdims=True) + RMS_EPS)
        xhat = xv * rstd
        err = xhat * wv - t_ref[...]
        dyv = err * (1.0 / d)
        dxhat = dyv * wv
        m = jnp.mean(dxhat * xhat, axis=-1, keepdims=True)
        dx_ref[...] = rstd * (dxhat - xhat * m)
        dw_part = jnp.sum(dyv * xhat, axis=0, keepdims=True)
        ls_part = jnp.sum(err * err, axis=0, keepdims=True)

        @pl.when(i == 0)
        def _():
            dw_ref[...] = dw_part
            ls_ref[...] = ls_part

        @pl.when(i > 0)
        def _():
            dw_ref[...] += dw_part
            ls_ref[...] += ls_part

    row = pl.BlockSpec((tr, d), lambda i: (i, 0))
    vec = pl.BlockSpec((1, d), lambda i: (0, 0))
    return pl.pallas_call(
        body, name="loss_head", grid=(t // tr,),
        in_specs=[row, vec, row], out_specs=[row, vec, vec],
        out_shape=[jax.ShapeDtypeStruct((t, d), F32), jax.ShapeDtypeStruct((1, d), F32),
                   jax.ShapeDtypeStruct((1, d), F32)],
        compiler_params=_params(("arbitrary",)),
    )(h2, w, tgt)


def _merge_fwd(proj, bg, ys, yl):
    t = proj.shape[0]
    d = D_MODEL
    tr = _rows(t)

    def body(ps_ref, pl_ref, bg_ref, ys_ref, yl_ref, o_ref):
        gs = _sigmoid(ps_ref[...] + bg_ref[:, 0:d])
        gl = _sigmoid(pl_ref[...] + bg_ref[:, d:2 * d])
        o_ref[...] = gs * ys_ref[...] + gl * yl_ref[...]

    row = pl.BlockSpec((tr, d), lambda i: (i, 0))
    return pl.pallas_call(
        body, name="merge_fwd", grid=(t // tr,),
        in_specs=[row, pl.BlockSpec((tr, d), lambda i: (i, 1)), pl.BlockSpec((1, 2 * d), lambda i: (0, 0)), row, row],
        out_specs=row, out_shape=jax.ShapeDtypeStruct((t, d), F32),
        compiler_params=_params(("parallel",)),
    )(proj, proj, bg, ys, yl)


def _merge_bwd(proj, bg, ys, yl, dm):
    t = proj.shape[0]
    d = D_MODEL
    tr = _rows(t)

    def body(ps_ref, pl_ref, bg_ref, ys_ref, yl_ref, dm_ref, dys_ref, dyl_ref, dg_ref, dbg_ref):
        i = pl.program_id(0)
        gs = _sigmoid(ps_ref[...] + bg_ref[:, 0:d])
        gl = _sigmoid(pl_ref[...] + bg_ref[:, d:2 * d])
        dmv = dm_ref[...]
        dys_ref[...] = dmv * gs
        dyl_ref[...] = dmv * gl
        dgs = dmv * ys_ref[...] * gs * (1.0 - gs)
        dgl = dmv * yl_ref[...] * gl * (1.0 - gl)
        dg_ref[:, 0:d] = dgs
        dg_ref[:, d:2 * d] = dgl

        @pl.when(i == 0)
        def _():
            dbg_ref[...] = jnp.zeros_like(dbg_ref)

        dbg_ref[:, 0:d] += jnp.sum(dgs, axis=0, keepdims=True)
        dbg_ref[:, d:2 * d] += jnp.sum(dgl, axis=0, keepdims=True)

    row = pl.BlockSpec((tr, d), lambda i: (i, 0))
    wide = pl.BlockSpec((tr, 2 * d), lambda i: (i, 0))
    vec = pl.BlockSpec((1, 2 * d), lambda i: (0, 0))
    return pl.pallas_call(
        body, name="merge_bwd", grid=(t // tr,),
        in_specs=[row, pl.BlockSpec((tr, d), lambda i: (i, 1)), vec, row, row, row],
        out_specs=[row, row, wide, vec],
        out_shape=[jax.ShapeDtypeStruct((t, d), F32), jax.ShapeDtypeStruct((t, d), F32),
                   jax.ShapeDtypeStruct((t, 2 * d), F32), jax.ShapeDtypeStruct((1, 2 * d), F32)],
        compiler_params=_params(("arbitrary",)),
    )(proj, proj, bg, ys, yl, dm)


def _swiglu_fwd(gu):
    t = gu.shape[0]
    f = FFN_HIDDEN
    tr = _rows(t)

    def body(g_ref, u_ref, o_ref):
        g = g_ref[...]
        o_ref[...] = g * _sigmoid(g) * u_ref[...]

    return pl.pallas_call(
        body, name="swiglu_fwd", grid=(t // tr,),
        in_specs=[pl.BlockSpec((tr, f), lambda i: (i, 0)), pl.BlockSpec((tr, f), lambda i: (i, 1))],
        out_specs=pl.BlockSpec((tr, f), lambda i: (i, 0)),
        out_shape=jax.ShapeDtypeStruct((t, f), F32),
        compiler_params=_params(("parallel",)),
    )(gu, gu)


def _swiglu_bwd(gu, dact):
    t = gu.shape[0]
    f = FFN_HIDDEN
    tr = _rows(t)

    def body(g_ref, u_ref, da_ref, o_ref):
        g = g_ref[...]
        sg = _sigmoid(g)
        da = da_ref[...]
        o_ref[:, 0:f] = da * u_ref[...] * (sg * (1.0 + g * (1.0 - sg)))
        o_ref[:, f:2 * f] = da * g * sg

    return pl.pallas_call(
        body, name="swiglu_bwd", grid=(t // tr,),
        in_specs=[pl.BlockSpec((tr, f), lambda i: (i, 0)), pl.BlockSpec((tr, f), lambda i: (i, 1)),
                  pl.BlockSpec((tr, f), lambda i: (i, 0))],
        out_specs=pl.BlockSpec((tr, 2 * f), lambda i: (i, 0)),
        out_shape=jax.ShapeDtypeStruct((t, 2 * f), F32),
        compiler_params=_params(("parallel",)),
    )(gu, gu, dact)


def _conv_pre(xv, wv, bv):
    pre = bv + wv[CONV_K - 1:CONV_K, :] * xv
    for k in range(CONV_K - 1):
        pre = pre + wv[k:k + 1, :] * _shift_down(xv, CONV_K - 1 - k)
    return pre


def _ssm_conv_fwd(proj, w, b):
    t = proj.shape[0]
    nb = SSM_CONV_DIM // LANES
    c0 = COL_XBC // LANES

    def body(x_ref, w_ref, b_ref, o_ref):
        pre = _conv_pre(x_ref[...], w_ref[...], b_ref[...])
        o_ref[...] = pre * _sigmoid(pre)

    return pl.pallas_call(
        body, name="ssm_conv_fwd", grid=(nb,),
        in_specs=[pl.BlockSpec((t, LANES), lambda j: (0, c0 + j)), pl.BlockSpec((CONV_K, LANES), lambda j: (0, j)),
                  pl.BlockSpec((1, LANES), lambda j: (0, j))],
        out_specs=pl.BlockSpec((t, LANES), lambda j: (0, j)),
        out_shape=jax.ShapeDtypeStruct((t, SSM_CONV_DIM), F32),
        compiler_params=_params(("parallel",)),
    )(proj, w, b)


def _ssm_conv_bwd(proj, w, b, dact):
    t = proj.shape[0]
    nb = SSM_CONV_DIM // LANES
    c0 = COL_XBC // LANES

    def body(x_ref, w_ref, b_ref, da_ref, dx_ref, dw_ref, db_ref):
        xv = x_ref[...]
        wv = w_ref[...]
        pre = _conv_pre(xv, wv, b_ref[...])
        sg = _sigmoid(pre)
        dpre = da_ref[...] * (sg * (1.0 + pre * (1.0 - sg)))
        dx = wv[CONV_K - 1:CONV_K, :] * dpre
        for k in range(CONV_K - 1):
            dx = dx + wv[k:k + 1, :] * _shift_up(dpre, CONV_K - 1 - k)
        dx_ref[...] = dx
        for k in range(CONV_K):
            dw_ref[k:k + 1, :] = jnp.sum(dpre * _shift_down(xv, CONV_K - 1 - k), axis=0, keepdims=True)
        db_ref[...] = jnp.sum(dpre, axis=0, keepdims=True)

    col = pl.BlockSpec((t, LANES), lambda j: (0, j))
    wsp = pl.BlockSpec((CONV_K, LANES), lambda j: (0, j))
    bsp = pl.BlockSpec((1, LANES), lambda j: (0, j))
    return pl.pallas_call(
        body, name="ssm_conv_bwd", grid=(nb,),
        in_specs=[pl.BlockSpec((t, LANES), lambda j: (0, c0 + j)), wsp, bsp, col],
        out_specs=[col, wsp, bsp],
        out_shape=[jax.ShapeDtypeStruct((t, SSM_CONV_DIM), F32), jax.ShapeDtypeStruct((CONV_K, SSM_CONV_DIM), F32),
                   jax.ShapeDtypeStruct((1, SSM_CONV_DIM), F32)],
        compiler_params=_params(("parallel",)),
    )(proj, w, b, dact)


def _ssd_chunk_terms(dtr, bias, alog):
    a = -jnp.exp(alog)
    dt = _softplus(dtr + bias)
    row = _iota((SSM_CHUNK, SSM_CHUNK), 0)
    col = _iota((SSM_CHUNK, SSM_CHUNK), 1)
    tri = (row >= col).astype(F32)
    cs = _fdot(tri, dt * a)
    dec = jnp.exp(cs[SSM_CHUNK - 1:SSM_CHUNK, :] - cs)
    ecs = jnp.exp(cs)
    off = _iota((LANES, SSM_INNER), 1) - SSM_HEAD_DIM * _iota((LANES, SSM_INNER), 0)
    expand = jnp.logical_and(off >= 0, off < SSM_HEAD_DIM).astype(F32)
    return a, dt, cs, dec, ecs, expand, row, col


def _ssd_specs(t):
    nc = t // SSM_CHUNK
    xs = pl.BlockSpec((SSM_CHUNK, SSM_INNER), lambda c: (c, 0))
    bm = pl.BlockSpec((SSM_CHUNK, SSM_BC), lambda c: (c, SSM_INNER // SSM_BC))
    cm = pl.BlockSpec((SSM_CHUNK, SSM_BC), lambda c: (c, SSM_INNER // SSM_BC + 1))
    dtr = pl.BlockSpec((SSM_CHUNK, LANES), lambda c: (c, COL_DT // LANES))
    vec = pl.BlockSpec((1, LANES), lambda c: (0, 0))
    wide = pl.BlockSpec((1, SSM_INNER), lambda c: (0, 0))
    return nc, xs, bm, cm, dtr, vec, wide


def _ssd_fwd(xbc_act, proj, bias, alog, dexp):
    t = proj.shape[0]
    nc, xs_s, bm_s, cm_s, dtr_s, vec, wide = _ssd_specs(t)

    def body(xs_ref, b_ref, c_ref, dtr_ref, bias_ref, alog_ref, dexp_ref, y_ref, sin_ref, state):
        @pl.when(pl.program_id(0) == 0)
        def _():
            state[...] = jnp.zeros_like(state)

        a, dt, cs, dec, ecs, expand, row, col = _ssd_chunk_terms(dtr_ref[...], bias_ref[...], alog_ref[...])
        cst = cs.T
        dt_x = _fdot(dt, expand)
        dec_x = _fdot(dec, expand)
        ecs_x = _fdot(ecs, expand)
        xs = xs_ref[...]
        xdt = xs * dt_x
        xdec = xdt * dec_x
        lane_lo = col < SSM_HEAD_DIM
        causal = row >= col
        sin_ref[0] = state[...]
        for g in range(SSM_GROUPS):
            bg = b_ref[:, g * SSM_STATE:(g + 1) * SSM_STATE].astype(BF16)
            cg = c_ref[:, g * SSM_STATE:(g + 1) * SSM_STATE].astype(BF16)
            cb = _bdot(cg, bg, NT)
            for q in range(SSM_PAIRS // SSM_GROUPS):
                pq = g * (SSM_PAIRS // SSM_GROUPS) + q
                sl = slice(pq * LANES, (pq + 1) * LANES)
                xp = xdt[:, sl].astype(BF16)
                yd = []
                for hh in range(2):
                    h = 2 * pq + hh
                    lmat = jnp.exp(jnp.where(causal, cs[:, h:h + 1] - cst[h:h + 1, :], -jnp.inf))
                    yd.append(_bdot(cb * lmat, xp))
                s_in = state[pq]
                y_off = _bdot(cg, s_in) * ecs_x[:, sl]
                y_ref[:, sl] = jnp.where(lane_lo, yd[0], yd[1]) + y_off + xs[:, sl] * dexp_ref[:, sl]
                state[pq] = s_in * ecs_x[SSM_CHUNK - 1:SSM_CHUNK, sl] + _bdot(bg, xdec[:, sl], TN)

    return pl.pallas_call(
        body, name="ssd_fwd", grid=(nc,),
        in_specs=[xs_s, bm_s, cm_s, dtr_s, vec, vec, wide],
        out_specs=[pl.BlockSpec((SSM_CHUNK, SSM_INNER), lambda c: (c, 0)),
                   pl.BlockSpec((1, SSM_PAIRS, SSM_STATE, LANES), lambda c: (c, 0, 0, 0))],
        out_shape=[jax.ShapeDtypeStruct((t, SSM_INNER), F32),
                   jax.ShapeDtypeStruct((nc, SSM_PAIRS, SSM_STATE, LANES), F32)],
        scratch_shapes=[pltpu.VMEM((SSM_PAIRS, SSM_STATE, LANES), F32)],
        compiler_params=_params(("arbitrary",)),
    )(xbc_act, xbc_act, xbc_act, proj, bias, alog, dexp)


def _ssd_bwd(xbc_act, proj, s_in_all, dy, bias, alog, dexp):
    t = proj.shape[0]
    nc = t // SSM_CHUNK
    last = nc - 1
    xs_s = pl.BlockSpec((SSM_CHUNK, SSM_INNER), lambda c: (last - c, 0))
    bm_s = pl.BlockSpec((SSM_CHUNK, SSM_BC), lambda c: (last - c, SSM_INNER // SSM_BC))
    cm_s = pl.BlockSpec((SSM_CHUNK, SSM_BC), lambda c: (last - c, SSM_INNER // SSM_BC + 1))
    dtr_s = pl.BlockSpec((SSM_CHUNK, LANES), lambda c: (last - c, COL_DT // LANES))
    sin_s = pl.BlockSpec((1, SSM_PAIRS, SSM_STATE, LANES), lambda c: (last - c, 0, 0, 0))
    vec = pl.BlockSpec((1, LANES), lambda c: (0, 0))
    wide = pl.BlockSpec((1, SSM_INNER), lambda c: (0, 0))

    def body(xs_ref, b_ref, c_ref, dtr_ref, sin_ref, dy_ref, bias_ref, alog_ref, dexp_ref,
             dxbc_ref, ddtr_ref, dbias_ref, dalog_ref, ddcol_ref, dstate, dxdt_s, yoff_s, rx_s, trow_s):
        @pl.when(pl.program_id(0) == 0)
        def _():
            dstate[...] = jnp.zeros_like(dstate)
            trow_s[...] = jnp.zeros_like(trow_s)
            dbias_ref[...] = jnp.zeros_like(dbias_ref)
            dalog_ref[...] = jnp.zeros_like(dalog_ref)
            ddcol_ref[...] = jnp.zeros_like(ddcol_ref)

        dtr = dtr_ref[...]
        a, dt, cs, dec, ecs, expand, row, col = _ssd_chunk_terms(dtr, bias_ref[...], alog_ref[...])
        cst = cs.T
        dt_x = _fdot(dt, expand)
        dec_x = _fdot(dec, expand)
        ecs_x = _fdot(ecs, expand)
        xs = xs_ref[...]
        dyv = dy_ref[...]
        xdt = xs * dt_x
        lane_lo = col < SSM_HEAD_DIM
        causal = row >= col
        ddcol_ref[...] += jnp.sum(dyv * xs, axis=0, keepdims=True)
        dcs_col = jnp.zeros((SSM_CHUNK, LANES), F32)
        dcs_row = jnp.zeros((LANES, SSM_CHUNK), F32)
        for g in range(SSM_GROUPS):
            bg = b_ref[:, g * SSM_STATE:(g + 1) * SSM_STATE].astype(BF16)
            cg = c_ref[:, g * SSM_STATE:(g + 1) * SSM_STATE].astype(BF16)
            cb = _bdot(cg, bg, NT)
            dgm = jnp.zeros((SSM_CHUNK, SSM_CHUNK), F32)
            dbg = jnp.zeros((SSM_CHUNK, SSM_STATE), F32)
            dcg = jnp.zeros((SSM_CHUNK, SSM_STATE), F32)
            for q in range(SSM_PAIRS // SSM_GROUPS):
                pq = g * (SSM_PAIRS // SSM_GROUPS) + q
                sl = slice(pq * LANES, (pq + 1) * LANES)
                dyp = dyv[:, sl]
                xp = xdt[:, sl]
                dxh = []
                for hh in range(2):
                    h = 2 * pq + hh
                    lmat = jnp.exp(jnp.where(causal, cs[:, h:h + 1] - cst[h:h + 1, :], -jnp.inf))
                    mmat = cb * lmat
                    dyh = jnp.where(lane_lo if hh == 0 else jnp.logical_not(lane_lo), dyp, 0.0)
                    dmm = _bdot(dyh, xp, NT)
                    pm = dmm * mmat
                    dcs_col = jnp.where(col == h, jnp.sum(pm, axis=1, keepdims=True), dcs_col)
                    dcs_row = jnp.where(row == h, jnp.sum(pm, axis=0, keepdims=True), dcs_row)
                    dgm = dgm + dmm * lmat
                    dxh.append(_bdot(mmat, dyp, TN))
                s_in = sin_ref[0, pq]
                ecs_p = ecs_x[:, sl]
                dec_p = dec_x[:, sl]
                etot_p = ecs_x[SSM_CHUNK - 1:SSM_CHUNK, sl]
                yoff_s[:, sl] = dyp * (_bdot(cg, s_in) * ecs_p)
                dq = dyp * ecs_p
                dcg = dcg + _bdot(dq, s_in, NT)
                ds = dstate[pq]
                r = _bdot(bg, ds)
                rx_s[:, sl] = r * xp
                dxdt_s[:, sl] = jnp.where(lane_lo, dxh[0], dxh[1]) + dec_p * r
                dbg = dbg + _bdot(xp * dec_p, ds, NT)
                trow_s[0:1, sl] = jnp.sum(ds * s_in, axis=0, keepdims=True) * etot_p
                dstate[pq] = etot_p * ds + _bdot(cg, dq, TN)
            dcg = dcg + _bdot(dgm, bg)
            dbg = dbg + _bdot(dgm, cg, TN)
            dxbc_ref[:, SSM_INNER + g * SSM_STATE:SSM_INNER + (g + 1) * SSM_STATE] = dbg
            dxbc_ref[:, SSM_INNER + SSM_BC + g * SSM_STATE:SSM_INNER + SSM_BC + (g + 1) * SSM_STATE] = dcg
        ddec = _fdot(rx_s[...], expand, NT) * dec
        dtot = _fdot(trow_s[...], expand, NT)[0:1, :]
        dcs = dcs_col - dcs_row.T + _fdot(yoff_s[...], expand, NT) - ddec
        dcs = dcs + jnp.where(row == SSM_CHUNK - 1, jnp.sum(ddec, axis=0, keepdims=True) + dtot, 0.0)
        da = _fdot((row <= col).astype(F32), dcs)
        dxdt = dxdt_s[...]
        ddt = da * a + _fdot(dxdt * xs, expand, NT)
        dalog_ref[...] += jnp.sum(da * dt, axis=0, keepdims=True) * a
        ddtr = ddt * _sigmoid(dtr + bias_ref[...])
        ddtr_ref[...] = ddtr
        dbias_ref[...] += jnp.sum(ddtr, axis=0, keepdims=True)
        dxbc_ref[:, 0:SSM_INNER] = dxdt * dt_x + dyv * dexp_ref[...]

    return pl.pallas_call(
        body, name="ssd_bwd", grid=(nc,),
        in_specs=[xs_s, bm_s, cm_s, dtr_s, sin_s, pl.BlockSpec((SSM_CHUNK, SSM_INNER), lambda c: (last - c, 0)),
                  vec, vec, wide],
        out_specs=[pl.BlockSpec((SSM_CHUNK, SSM_CONV_DIM), lambda c: (last - c, 0)),
                   pl.BlockSpec((SSM_CHUNK, LANES), lambda c: (last - c, 0)), vec, vec, wide],
        out_shape=[jax.ShapeDtypeStruct((t, SSM_CONV_DIM), F32), jax.ShapeDtypeStruct((t, LANES), F32),
                   jax.ShapeDtypeStruct((1, LANES), F32), jax.ShapeDtypeStruct((1, LANES), F32),
                   jax.ShapeDtypeStruct((1, SSM_INNER), F32)],
        scratch_shapes=[pltpu.VMEM((SSM_PAIRS, SSM_STATE, LANES), F32),
                        pltpu.VMEM((SSM_CHUNK, SSM_INNER), F32), pltpu.VMEM((SSM_CHUNK, SSM_INNER), F32),
                        pltpu.VMEM((SSM_CHUNK, SSM_INNER), F32), pltpu.VMEM((SUBLANES, SSM_INNER), F32)],
        compiler_params=_params(("arbitrary",)),
    )(xbc_act, xbc_act, xbc_act, proj, s_in_all, dy, bias, alog, dexp)


def _group_rstd(y):
    n = SSM_INNER // SSM_GROUPS
    parts = []
    for g in range(SSM_GROUPS):
        yg = y[:, g * n:(g + 1) * n]
        r = lax.rsqrt(jnp.mean(yg * yg, axis=-1, keepdims=True) + RMS_EPS)
        parts.append(jnp.broadcast_to(r, yg.shape))
    return jnp.concatenate(parts, axis=1)


def _group_mean(v):
    n = SSM_INNER // SSM_GROUPS
    parts = []
    for g in range(SSM_GROUPS):
        vg = v[:, g * n:(g + 1) * n]
        parts.append(jnp.broadcast_to(jnp.mean(vg, axis=-1, keepdims=True), vg.shape))
    return jnp.concatenate(parts, axis=1)


def _ssm_post_fwd(y_ssd, proj, nw):
    t = proj.shape[0]
    n = SSM_INNER
    tr = _rows(t)

    def body(y_ref, z_ref, nw_ref, o_ref):
        z = z_ref[...]
        y = y_ref[...] * (z * _sigmoid(z))
        o_ref[...] = y * _group_rstd(y) * nw_ref[...]

    row = pl.BlockSpec((tr, n), lambda i: (i, 0))
    return pl.pallas_call(
        body, name="ssm_post_fwd", grid=(t // tr,),
        in_specs=[row, pl.BlockSpec((tr, n), lambda i: (i, COL_Z // n)), pl.BlockSpec((1, n), lambda i: (0, 0))],
        out_specs=row, out_shape=jax.ShapeDtypeStruct((t, n), F32),
        compiler_params=_params(("parallel",)),
    )(y_ssd, proj, nw)


def _ssm_post_bwd(y_ssd, proj, nw, dout):
    t = proj.shape[0]
    n = SSM_INNER
    tr = _rows(t)

    def body(y_ref, z_ref, nw_ref, do_ref, dy_ref, dz_ref, dnw_ref):
        i = pl.program_id(0)
        z = z_ref[...]
        sg = _sigmoid(z)
        sz = z * sg
        ys = y_ref[...]
        y = ys * sz
        rstd = _group_rstd(y)
        yn = y * rstd
        dov = do_ref[...]
        dyn = dov * nw_ref[...]
        dyg = rstd * (dyn - yn * _group_mean(dyn * yn))
        dy_ref[...] = dyg * sz
        dz_ref[...] = dyg * ys * (sg * (1.0 + z * (1.0 - sg)))
        part = jnp.sum(dov * yn, axis=0, keepdims=True)

        @pl.when(i == 0)
        def _():
            dnw_ref[...] = part

        @pl.when(i > 0)
        def _():
            dnw_ref[...] += part

    row = pl.BlockSpec((tr, n), lambda i: (i, 0))
    vec = pl.BlockSpec((1, n), lambda i: (0, 0))
    return pl.pallas_call(
        body, name="ssm_post_bwd", grid=(t // tr,),
        in_specs=[row, pl.BlockSpec((tr, n), lambda i: (i, COL_Z // n)), vec, row],
        out_specs=[row, row, vec],
        out_shape=[jax.ShapeDtypeStruct((t, n), F32), jax.ShapeDtypeStruct((t, n), F32),
                   jax.ShapeDtypeStruct((1, n), F32)],
        compiler_params=_params(("arbitrary",)),
    )(y_ssd, proj, nw, dout)


GELU_C = math.sqrt(2.0 / math.pi)
GELU_K = 0.044715


def _gelu_parts(y):
    th = jnp.tanh(GELU_C * (y + GELU_K * y * y * y))
    val = 0.5 * y * (1.0 + th)
    grad = 0.5 * (1.0 + th) + 0.5 * y * (1.0 - th * th) * GELU_C * (1.0 + 3.0 * GELU_K * y * y)
    return val, grad


def _scan_tiles(a_ref, b_ref, h_ref, n_rows, reverse):
    n_tiles = n_rows // SUBLANES
    shape = (SUBLANES, a_ref.shape[1])
    row = _iota(shape, 0)

    def step(k, carry):
        tile = (n_tiles - 1 - k) if reverse else k
        at = pl.ds(pl.multiple_of(tile * SUBLANES, SUBLANES), SUBLANES)
        av = a_ref[at, :]
        bv = b_ref[at, :]
        for s in (1, 2, 4):
            if reverse:
                keep = row < SUBLANES - s
                a_sh = jnp.where(keep, pltpu.roll(av, SUBLANES - s, 0), 1.0)
                b_sh = jnp.where(keep, pltpu.roll(bv, SUBLANES - s, 0), 0.0)
            else:
                keep = row >= s
                a_sh = jnp.where(keep, pltpu.roll(av, s, 0), 1.0)
                b_sh = jnp.where(keep, pltpu.roll(bv, s, 0), 0.0)
            bv = av * b_sh + bv
            av = av * a_sh
        hv = bv + av * carry
        h_ref[at, :] = hv
        return hv[0:1, :] if reverse else hv[SUBLANES - 1:SUBLANES, :]

    lax.fori_loop(0, n_tiles, step, jnp.zeros((1, a_ref.shape[1]), F32))


def _lru_gates(xl, cw, cb, wr, br, wi, bi, lam):
    u = cb + cw[CONV_K - 1:CONV_K, :] * xl
    for k in range(CONV_K - 1):
        u = u + cw[k:k + 1, :] * _shift_down(xl, CONV_K - 1 - k)
    r = _sigmoid(_bdot(u, wr) + br)
    i = _sigmoid(_bdot(u, wi) + bi)
    sp = _softplus(-lam)
    la = -LRU_C * r * sp
    a = jnp.exp(la)
    mult = jnp.sqrt(-jnp.tanh(la) * (a * a + 1.0))
    return u, r, i, sp, a, mult


def _lru_specs(t):
    c_x = COL_LRU_X // LANES
    c_y = COL_LRU_Y // LANES
    xl = pl.BlockSpec((t, LANES), lambda j: (0, c_x + j))
    yl = pl.BlockSpec((t, LANES), lambda j: (0, c_y + j))
    col = pl.BlockSpec((t, LANES), lambda j: (0, j))
    cw = pl.BlockSpec((CONV_K, LANES), lambda j: (0, j))
    vec = pl.BlockSpec((1, LANES), lambda j: (0, j))
    wblk = pl.BlockSpec((1, LANES, LANES), lambda j: (j, 0, 0))
    return xl, yl, col, cw, vec, wblk


def _lru_fwd(proj, cw, cb, wr, br, wi, bi, lam):
    t = proj.shape[0]
    xl_s, yl_s, col, cw_s, vec, wblk = _lru_specs(t)

    def body(xl_ref, yl_ref, cw_ref, cb_ref, wr_ref, br_ref, wi_ref, bi_ref, lam_ref, o_ref, a_s, b_s, h_s):
        u, r, i, sp, a, mult = _lru_gates(xl_ref[...], cw_ref[...], cb_ref[...], wr_ref[0], br_ref[...],
                                          wi_ref[0], bi_ref[...], lam_ref[...])
        a_s[...] = a
        b_s[...] = mult * (i * u)
        _scan_tiles(a_s, b_s, h_s, t, reverse=False)
        o_ref[...] = h_s[...] * _gelu_parts(yl_ref[...])[0]

    return pl.pallas_call(
        body, name="lru_fwd", grid=(LRU_BLOCKS,),
        in_specs=[xl_s, yl_s, cw_s, vec, wblk, vec, wblk, vec, vec],
        out_specs=col, out_shape=jax.ShapeDtypeStruct((t, LRU_WIDTH), F32),
        scratch_shapes=[pltpu.VMEM((t, LANES), F32)] * 3,
        compiler_params=_params(("arbitrary",), big=True),
    )(proj, proj, cw, cb, wr, br, wi, bi, lam)


def _lru_bwd(proj, cw, cb, wr, br, wi, bi, lam, dout):
    t = proj.shape[0]
    xl_s, yl_s, col, cw_s, vec, wblk = _lru_specs(t)

    def body(xl_ref, yl_ref, cw_ref, cb_ref, wr_ref, br_ref, wi_ref, bi_ref, lam_ref, do_ref,
             dxl_ref, dyl_ref, dcw_ref, dcb_ref, dwr_ref, dbr_ref, dwi_ref, dbi_ref, dlam_ref,
             a_s, b_s, h_s, g_s):
        xl = xl_ref[...]
        cwv = cw_ref[...]
        lam = lam_ref[...]
        u, r, i, sp, a, mult = _lru_gates(xl, cwv, cb_ref[...], wr_ref[0], br_ref[...], wi_ref[0], bi_ref[...], lam)
        v = i * u
        a_s[...] = a
        b_s[...] = mult * v
        _scan_tiles(a_s, b_s, h_s, t, reverse=False)
        gl, dgl = _gelu_parts(yl_ref[...])
        dov = do_ref[...]
        h = h_s[...]
        dyl_ref[...] = dov * h * dgl
        b_s[...] = dov * gl
        a_s[...] = _shift_up(a, 1)
        _scan_tiles(a_s, b_s, g_s, t, reverse=True)
        g = g_s[...]
        da = g * _shift_down(h, 1)
        dmult = g * v
        dv = g * mult
        dla = da * a - dmult * (a * a) / mult
        dr = dla * (-LRU_C * sp)
        dsp = jnp.sum(dla * (-LRU_C * r), axis=0, keepdims=True)
        dlam_ref[...] = -dsp * _sigmoid(-lam)
        dpr = dr * r * (1.0 - r)
        dpi = dv * u * i * (1.0 - i)
        dbr_ref[...] = jnp.sum(dpr, axis=0, keepdims=True)
        dbi_ref[...] = jnp.sum(dpi, axis=0, keepdims=True)
        dwr_ref[0] = _bdot(u, dpr, TN)
        dwi_ref[0] = _bdot(u, dpi, TN)
        du = dv * i + _bdot(dpr, wr_ref[0], NT) + _bdot(dpi, wi_ref[0], NT)
        dxl = cwv[CONV_K - 1:CONV_K, :] * du
        for k in range(CONV_K - 1):
            dxl = dxl + cwv[k:k + 1, :] * _shift_up(du, CONV_K - 1 - k)
        dxl_ref[...] = dxl
        for k in range(CONV_K):
            dcw_ref[k:k + 1, :] = jnp.sum(du * _shift_down(xl, CONV_K - 1 - k), axis=0, keepdims=True)
        dcb_ref[...] = jnp.sum(du, axis=0, keepdims=True)

    return pl.pallas_call(
        body, name="lru_bwd", grid=(LRU_BLOCKS,),
        in_specs=[xl_s, yl_s, cw_s, vec, wblk, vec, wblk, vec, vec, col],
        out_specs=[col, col, cw_s, vec, wblk, vec, wblk, vec, vec],
        out_shape=[jax.ShapeDtypeStruct((t, LRU_WIDTH), F32), jax.ShapeDtypeStruct((t, LRU_WIDTH), F32),
                   jax.ShapeDtypeStruct((CONV_K, LRU_WIDTH), F32), jax.ShapeDtypeStruct((1, LRU_WIDTH), F32),
                   jax.ShapeDtypeStruct((LRU_BLOCKS, LANES, LANES), F32), jax.ShapeDtypeStruct((1, LRU_WIDTH), F32),
                   jax.ShapeDtypeStruct((LRU_BLOCKS, LANES, LANES), F32), jax.ShapeDtypeStruct((1, LRU_WIDTH), F32),
                   jax.ShapeDtypeStruct((1, LRU_WIDTH), F32)],
        scratch_shapes=[pltpu.VMEM((t, LANES), F32)] * 4,
        compiler_params=_params(("arbitrary",), big=True),
    )(proj, proj, cw, cb, wr, br, wi, bi, lam, dout)


def _mesh_pos():
    return lax.axis_index("x"), lax.axis_index("y"), lax.axis_index("c")


def _all_gather(vs, name):
    n = len(vs)

    def body(*refs):
        v_refs, out_refs = refs[:n], refs[n:2 * n]
        send_sems, recv_sems, local_sems = refs[2 * n:]
        x, y, c = _mesh_pos()
        me, sibling = (x, y, c), (x, y, 1 - c)
        chips = [(1 - x, y), (x, 1 - y), (1 - x, 1 - y)]

        def block(a, px, py, pc):
            return out_refs[a].at[4 * px + 2 * py + pc]

        def copy(a, k, blk, to, src=None):
            return pltpu.make_async_remote_copy(
                src_ref=block(a, *blk) if src is None else src, dst_ref=block(a, *blk),
                send_sem=send_sems.at[a, k], recv_sem=recv_sems.at[a, k], device_id=to, device_id_type=MESH)

        started = []
        for a in range(n):
            mine = pltpu.make_async_copy(v_refs[a], block(a, *me), local_sems.at[a])
            mine.start()
            started.append(mine)
        sends = []
        for a in range(n):
            first = [copy(a, 0, me, sibling, src=v_refs[a])]
            first += [copy(a, 1 + j, me, (*chip, c), src=v_refs[a]) for j, chip in enumerate(chips)]
            for cp in first:
                cp.start()
            sends += first
        for j, chip in enumerate(chips):
            for a in range(n):
                copy(a, 1 + j, (*chip, c), me).wait_recv()
                fwd = copy(a, 4 + j, (*chip, c), sibling)
                fwd.start()
                sends.append(fwd)
        for a in range(n):
            copy(a, 0, sibling, me).wait_recv()
            for j, chip in enumerate(chips):
                copy(a, 4 + j, (*chip, 1 - c), me).wait_recv()
        for cp in sends:
            cp.wait_send()
        for mine in started:
            mine.wait()

    hbm = pl.BlockSpec(memory_space=pl.ANY)
    return pl.pallas_call(
        body, name=name,
        out_shape=[jax.ShapeDtypeStruct((N_DEV,) + v.shape, v.dtype) for v in vs],
        in_specs=[hbm] * n, out_specs=[hbm] * n,
        scratch_shapes=[pltpu.SemaphoreType.DMA((n, 7)), pltpu.SemaphoreType.DMA((n, 7)),
                        pltpu.SemaphoreType.DMA((n,))],
    )(*vs)


def _exchange(parts, name):
    n = len(parts)

    def body(*refs):
        p_refs, out_refs = refs[:n], refs[n:2 * n]
        send_sems, recv_sems, local_sems = refs[2 * n:]
        x, y, c = _mesh_pos()
        me = 4 * x + 2 * y + c
        local = []
        for a in range(n):
            mine = pltpu.make_async_copy(p_refs[a].at[me], out_refs[a].at[me], local_sems.at[a])
            mine.start()
            local.append(mine)
        copies = []
        for k in range(1, N_DEV):
            px = (1 - x) if k & 4 else x
            py = (1 - y) if k & 2 else y
            pc = (1 - c) if k & 1 else c
            for a in range(n):
                cp = pltpu.make_async_remote_copy(
                    src_ref=p_refs[a].at[4 * px + 2 * py + pc], dst_ref=out_refs[a].at[me],
                    send_sem=send_sems.at[a, k - 1], recv_sem=recv_sems.at[a, k - 1],
                    device_id=(px, py, pc), device_id_type=MESH)
                cp.start()
                copies.append(cp)
        for cp in copies:
            cp.wait()
        for mine in local:
            mine.wait()

    hbm = pl.BlockSpec(memory_space=pl.ANY)
    return pl.pallas_call(
        body, name=name,
        out_shape=[jax.ShapeDtypeStruct(p.shape, p.dtype) for p in parts],
        in_specs=[hbm] * n, out_specs=[hbm] * n,
        scratch_shapes=[pltpu.SemaphoreType.DMA((n, 7)), pltpu.SemaphoreType.DMA((n, 7)),
                        pltpu.SemaphoreType.DMA((n,))],
    )(*parts)


def _sum_sources(recv, tile, name):
    n, rows, width = recv.shape

    def body(r_ref, o_ref):
        acc = r_ref[0].astype(F32)
        for s in range(1, n):
            acc = acc + r_ref[s].astype(F32)
        o_ref[...] = acc

    return pl.pallas_call(
        body, name=name, grid=(rows // tile,),
        in_specs=[pl.BlockSpec((n, tile, width), lambda i: (0, i, 0))],
        out_specs=pl.BlockSpec((tile, width), lambda i: (i, 0)),
        out_shape=jax.ShapeDtypeStruct((rows, width), F32),
        compiler_params=_params(("parallel",)),
    )(recv)


def _row_tile(rows):
    for tile in range(128, 15, -16):
        if rows % tile == 0:
            return tile
    return rows


def _adamw(w, recv, m, v, name):
    rows, width = w.shape
    n = recv.shape[0]
    tile = _row_tile(rows)
    c1 = 1.0 - ADAM_B1 ** ADAM_STEP
    c2 = 1.0 - ADAM_B2 ** ADAM_STEP

    def body(w_ref, r_ref, m_ref, v_ref, g_ref, d_ref, nm_ref, nv_ref):
        gv = r_ref[0].astype(F32)
        for s in range(1, n):
            gv = gv + r_ref[s].astype(F32)
        nm = ADAM_B1 * m_ref[...] + (1.0 - ADAM_B1) * gv
        nv = ADAM_B2 * v_ref[...] + (1.0 - ADAM_B2) * (gv * gv)
        g_ref[...] = gv
        nm_ref[...] = nm
        nv_ref[...] = nv
        d_ref[...] = -ADAM_LR * ((nm / c1) / (jnp.sqrt(nv / c2) + ADAM_EPS) + ADAM_WD * w_ref[...])

    spec = pl.BlockSpec((tile, width), lambda i: (i, 0))
    shape = jax.ShapeDtypeStruct((rows, width), F32)
    return pl.pallas_call(
        body, name=name, grid=(rows // tile,),
        in_specs=[spec, pl.BlockSpec((n, tile, width), lambda i: (0, i, 0)), spec, spec],
        out_specs=[spec] * 4, out_shape=[shape] * 4,
        compiler_params=_params(("parallel",)),
    )(w, recv, m, v)


BIG_NAMES = ("w_in", "w_out_ssm", "w_out_lru", "w_out", "w_ffn_in", "w_ffn_out", "ssm_conv_w", "lru_conv_w")
COL_SHARDED = ("w_in", "w_ffn_in", "ssm_conv_w", "lru_conv_w")
SMALL_NAMES = ("norm1_w", "b_branch_gate", "ssm_conv_b", "ssm_dt_bias", "ssm_a_log", "ssm_d", "ssm_norm_w",
               "lru_conv_b", "lru_w_r", "lru_b_r", "lru_w_i", "lru_b_i", "lru_lambda", "norm2_w", "norm_f_w")


def _pack(arrays, total):
    flat = []
    used = 0
    for a in arrays:
        a = a.reshape(-1)
        pad = (-a.shape[0]) % LANES
        flat.append(a)
        if pad:
            flat.append(jnp.zeros((pad,), a.dtype))
        used += a.shape[0] + pad
    assert used <= total, (used, total)
    if total > used:
        flat.append(jnp.zeros((total - used,), arrays[0].dtype))
    return jnp.concatenate(flat)


def _unpack(flat, shapes):
    out = []
    off = 0
    for shp in shapes:
        n = math.prod(shp)
        out.append(flat[off:off + n].reshape(shp))
        off += n + (-n) % LANES
    return out


def _col_shards(full):
    rows, cols = full.shape
    return full.reshape(rows, N_DEV, cols // N_DEV).transpose(1, 0, 2)


def _row_shards(full):
    rows, cols = full.shape
    return full.reshape(N_DEV, rows // N_DEV, cols)


def _from_col_shards(g):
    n, rows, w = g.shape
    return g.transpose(1, 0, 2).reshape(rows, n * w)


def kernel(x, norm1_w, w_in, b_branch_gate, ssm_conv_w, ssm_conv_b, ssm_dt_bias, ssm_a_log, ssm_d, ssm_norm_w, w_out_ssm, lru_conv_w, lru_conv_b, lru_w_r, lru_b_r, lru_w_i, lru_b_i, lru_lambda, w_out_lru, w_out, norm2_w, w_ffn_in, w_ffn_out, norm_f_w, loss_target, m_norm1_w, m_w_in, m_b_branch_gate, m_ssm_conv_w, m_ssm_conv_b, m_ssm_dt_bias, m_ssm_a_log, m_ssm_d, m_ssm_norm_w, m_w_out_ssm, m_lru_conv_w, m_lru_conv_b, m_lru_w_r, m_lru_b_r, m_lru_w_i, m_lru_b_i, m_lru_lambda, m_w_out_lru, m_w_out, m_norm2_w, m_w_ffn_in, m_w_ffn_out, m_norm_f_w, v_norm1_w, v_w_in, v_b_branch_gate, v_ssm_conv_w, v_ssm_conv_b, v_ssm_dt_bias, v_ssm_a_log, v_ssm_d, v_ssm_norm_w, v_w_out_ssm, v_lru_conv_w, v_lru_conv_b, v_lru_w_r, v_lru_b_r, v_lru_w_i, v_lru_b_i, v_lru_lambda, v_w_out_lru, v_w_out, v_norm2_w, v_w_ffn_in, v_w_ffn_out, v_norm_f_w):
    given = dict(locals())
    weights = {n: given[n] for n in BIG_NAMES + SMALL_NAMES}
    t = x.shape[1]
    xt = x[0]
    tgt = loss_target[0]

    shards = [weights[n][0].astype(BF16) for n in BIG_NAMES[:6]] + [weights[n][0] for n in BIG_NAMES[6:]]
    full = {}
    for n, g in zip(BIG_NAMES, _all_gather(shards, "gather_weights")):
        full[n] = _from_col_shards(g) if n in COL_SHARDED else g.reshape(-1, g.shape[-1])
    ssm_cw, lru_cw = full["ssm_conv_w"], full["lru_conv_w"]
    wi_full = full["w_in"]
    w_p = jnp.concatenate([wi_full[:, :ORIG_DT], wi_full[:, ORIG_LRU:], wi_full[:, ORIG_DT:ORIG_LRU],
                           jnp.zeros((D_MODEL, PROJ_W - IN_PROJ), BF16)], axis=1)

    def pad_heads(a):
        return jnp.pad(a.reshape(1, SSM_HEADS), ((0, 0), (0, LANES - SSM_HEADS)))

    dt_bias_p = pad_heads(ssm_dt_bias)
    a_log_p = pad_heads(ssm_a_log)
    d_exp = jnp.repeat(ssm_d.reshape(SSM_HEADS), SSM_HEAD_DIM).reshape(1, SSM_INNER)
    lru_wr, lru_wi = lru_w_r[0], lru_w_i[0]

    hn1 = _rmsnorm_fwd(xt, norm1_w, "norm1_fwd")
    proj = _mm(hn1, w_p, name="in_proj")
    xbc_act = _ssm_conv_fwd(proj, ssm_cw, ssm_conv_b)
    y_ssd, s_in_all = _ssd_fwd(xbc_act, proj, dt_bias_p, a_log_p, d_exp)
    y_pre = _ssm_post_fwd(y_ssd, proj, ssm_norm_w)
    y_ssm = _mm(y_pre, full["w_out_ssm"], name="out_ssm")
    l_out = _lru_fwd(proj, lru_cw, lru_conv_b, lru_wr, lru_b_r, lru_wi, lru_b_i, lru_lambda)
    y_lru = _mm(l_out, full["w_out_lru"], name="out_lru")
    merged = _merge_fwd(proj, b_branch_gate, y_ssm, y_lru)
    h1 = _mm(merged, full["w_out"], add=xt, name="out_proj")
    hn2 = _rmsnorm_fwd(h1, norm2_w, "norm2_fwd")
    gu = _mm(hn2, full["w_ffn_in"], name="ffn_in")
    act = _swiglu_fwd(gu)
    h2 = _mm(act, full["w_ffn_out"], add=h1, name="ffn_out")

    grads = {}
    dh2, grads["norm_f_w"], loss_cols = _loss_head(h2, norm_f_w.reshape(1, D_MODEL), tgt)
    loss = lax.psum(0.5 * jnp.sum(loss_cols) / D_MODEL, AXES)
    dact = _mm(dh2, full["w_ffn_out"], tb=True, name="d_act")
    grads["w_ffn_out"] = _mm(act, dh2, ta=True, name="dw_ffn_out")
    dgu = _swiglu_bwd(gu, dact)
    dhn2 = _mm(dgu, full["w_ffn_in"], tb=True, name="d_hn2")
    grads["w_ffn_in"] = _mm(hn2, dgu, ta=True, name="dw_ffn_in")
    dh1, grads["norm2_w"] = _rmsnorm_bwd(h1, norm2_w, dhn2, dh2, "norm2_bwd")
    dmerged = _mm(dh1, full["w_out"], tb=True, name="d_merged")
    grads["w_out"] = _mm(merged, dh1, ta=True, name="dw_out")
    dy_ssm, dy_lru, dgates, grads["b_branch_gate"] = _merge_bwd(proj, b_branch_gate, y_ssm, y_lru, dmerged)
    dy_pre = _mm(dy_ssm, full["w_out_ssm"], tb=True, name="d_y_pre")
    grads["w_out_ssm"] = _mm(y_pre, dy_ssm, ta=True, name="dw_out_ssm")
    dl_out = _mm(dy_lru, full["w_out_lru"], tb=True, name="d_l_out")
    grads["w_out_lru"] = _mm(l_out, dy_lru, ta=True, name="dw_out_lru")
    dy_ssd, dz, grads["ssm_norm_w"] = _ssm_post_bwd(y_ssd, proj, ssm_norm_w, dy_pre)
    dxbc_act, ddt, dbias, dalog, ddcol = _ssd_bwd(xbc_act, proj, s_in_all, dy_ssd, dt_bias_p, a_log_p, d_exp)
    grads["ssm_dt_bias"] = dbias[:, :SSM_HEADS]
    grads["ssm_a_log"] = dalog[:, :SSM_HEADS]
    grads["ssm_d"] = ddcol.reshape(SSM_HEADS, SSM_HEAD_DIM).sum(axis=1).reshape(1, SSM_HEADS)
    dxbc, grads["ssm_conv_w"], grads["ssm_conv_b"] = _ssm_conv_bwd(proj, ssm_cw, ssm_conv_b, dxbc_act)
    (dxl, dyl, grads["lru_conv_w"], grads["lru_conv_b"], dwr, grads["lru_b_r"], dwi, grads["lru_b_i"],
     grads["lru_lambda"]) = _lru_bwd(proj, lru_cw, lru_conv_b, lru_wr, lru_b_r, lru_wi, lru_b_i, lru_lambda, dl_out)
    grads["lru_w_r"], grads["lru_w_i"] = dwr[None], dwi[None]
    dproj = jnp.concatenate([dgates, dz, dxbc, dxl, dyl, ddt], axis=1)
    dhn1 = _mm(dproj, w_p, tb=True, name="d_hn1")
    dwp = _mm(hn1, dproj, ta=True, name="dw_in")
    grads["w_in"] = jnp.concatenate([dwp[:, :ORIG_DT], dwp[:, COL_DT:COL_DT + SSM_HEADS], dwp[:, COL_LRU_X:COL_DT]],
                                    axis=1)
    grad_x, grads["norm1_w"] = _rmsnorm_bwd(xt, norm1_w, dhn1, dh1, "norm1_bwd")

    parts = [(_col_shards if n in COL_SHARDED else _row_shards)(grads[n]).astype(BF16) for n in BIG_NAMES]
    small_total = N_DEV * SMALL_ROWS * LANES
    parts.append(_pack([grads[n] for n in SMALL_NAMES], small_total).reshape(N_DEV, SMALL_ROWS, LANES))
    recv = _exchange(parts, "exchange_grads")
    small_g = _sum_sources(recv[-1], SMALL_ROWS, "sum_small_grads")
    small_g = _all_gather([small_g], "gather_small_grads")[0].reshape(1, small_total // FLAT_W, FLAT_W)

    big_out = {n: _adamw(weights[n][0], r, given["m_" + n][0], given["v_" + n][0], "adamw_" + n)
               for n, r in zip(BIG_NAMES, recv)}
    small_state = [_pack([given[p + n] for n in SMALL_NAMES], small_total).reshape(-1, FLAT_W) for p in ("", "m_", "v_")]
    small_out = _adamw(small_state[0], small_g, small_state[1], small_state[2], "adamw_replicated")

    small_shapes = [weights[n].shape for n in SMALL_NAMES]
    order = list(given)[1:24]
    results = []
    for q in range(4):
        vals = {n: big_out[n][q][None] for n in BIG_NAMES}
        vals.update(zip(SMALL_NAMES, _unpack(small_out[q].reshape(-1), small_shapes)))
        results.extend(vals[n] for n in order)
    return (loss, grad_x[None], *results)
```

```python
import math

import jax
import jax.numpy as jnp
from jax import lax
from jax.experimental import pallas as pl
from jax.experimental.pallas import tpu as pltpu

F32 = jnp.float32
BF16 = jnp.bfloat16
HIGHEST = lax.Precision.HIGHEST
MESH = pl.DeviceIdType.MESH
AXES = ("x", "y", "c")
N_DEV = 8

D_MODEL = 1024
SSM_INNER = 2048
SSM_HEADS = 32
SSM_HEAD_DIM = 64
SSM_GROUPS = 4
SSM_STATE = 128
SSM_BC = SSM_GROUPS * SSM_STATE
SSM_CONV_DIM = SSM_INNER + 2 * SSM_BC
SSM_CHUNK = 128
SSM_PAIRS = SSM_HEADS // 2
CONV_K = 4
LRU_WIDTH = 1280
LRU_BLOCKS = 10
LRU_C = 8.0
FFN_HIDDEN = 2816
RMS_EPS = 1e-6
IN_PROJ = 9760

COL_GATES = 0
COL_Z = 2048
COL_XBC = 4096
COL_LRU_X = 7168
COL_LRU_Y = 8448
COL_DT = 9728
PROJ_W = 9856
ORIG_DT = 7168
ORIG_LRU = 7200

ADAM_LR = 0.001
ADAM_B1 = 0.9
ADAM_B2 = 0.999
ADAM_EPS = 1e-08
ADAM_WD = 0.01
ADAM_STEP = 10

LANES = 128
SUBLANES = 8
V7X_VMEM_BYTES = 64 * 1024 * 1024
VMEM_LIMIT = V7X_VMEM_BYTES * 3 // 4
VMEM_LIMIT_BIG = V7X_VMEM_BYTES * 15 // 16

NT = (((1,), (1,)), ((), ()))
TN = (((0,), (0,)), ((), ()))

FLAT_W = 1024
SMALL_ROWS = 336


def _params(sem=None, big=False):
    return pltpu.CompilerParams(dimension_semantics=sem,
                                vmem_limit_bytes=VMEM_LIMIT_BIG if big else VMEM_LIMIT)


def _blk(dim, cap):
    if dim <= cap:
        return dim
    for m in range(cap // LANES, 0, -1):
        if dim % (m * LANES) == 0:
            return m * LANES
    raise ValueError(f"no block for {dim}")


def _rows(t):
    return min(t, 256)


def _sigmoid(v):
    return 1.0 / (1.0 + jnp.exp(-v))


def _softplus(v):
    e = jnp.exp(-jnp.abs(v))
    u = 1.0 + e
    log1p = jnp.where(u == 1.0, e, jnp.log(u) * e / jnp.where(u == 1.0, 1.0, u - 1.0))
    return jnp.maximum(v, 0.0) + log1p


def _iota(shape, dim):
    return lax.broadcasted_iota(jnp.int32, shape, dim)


def _shift_down(v, s):
    if s == 0:
        return v
    return jnp.where(_iota(v.shape, 0) >= s, pltpu.roll(v, s, 0), 0.0)


def _shift_up(v, s):
    if s == 0:
        return v
    n = v.shape[0]
    return jnp.where(_iota(v.shape, 0) < n - s, pltpu.roll(v, n - s, 0), 0.0)


def _bdot(a, b, dn=None):
    a = a.astype(BF16)
    b = b.astype(BF16)
    if dn is None:
        return jnp.dot(a, b, preferred_element_type=F32)
    return lax.dot_general(a, b, dn, preferred_element_type=F32)


def _fdot(a, b, dn=None):
    if dn is None:
        return jnp.dot(a, b, precision=HIGHEST, preferred_element_type=F32)
    return lax.dot_general(a, b, dn, precision=HIGHEST, preferred_element_type=F32)


def _mm(a, b, *, ta=False, tb=False, add=None, name):
    if ta:
        kdim, m = a.shape
    else:
        m, kdim = a.shape
    if tb:
        n, k2 = b.shape
    else:
        k2, n = b.shape
    assert kdim == k2, (a.shape, b.shape, ta, tb)
    bm, bn, bk = _blk(m, 1024), _blk(n, 1408), _blk(kdim, 1408)
    nk = kdim // bk
    dn = (((0 if ta else 1,), (1 if tb else 0,)), ((), ()))

    def body(*refs):
        if add is None:
            a_ref, b_ref, o_ref, acc = refs
        else:
            a_ref, b_ref, r_ref, o_ref, acc = refs
        k = pl.program_id(2)

        @pl.when(k == 0)
        def _():
            acc[...] = jnp.zeros_like(acc)

        acc[...] += lax.dot_general(a_ref[...].astype(BF16), b_ref[...].astype(BF16), dn,
                                    preferred_element_type=F32)

        @pl.when(k == nk - 1)
        def _():
            r = acc[...]
            if add is not None:
                r = r + r_ref[...]
            o_ref[...] = r

    a_spec = pl.BlockSpec((bk, bm), lambda i, j, k: (k, i)) if ta else pl.BlockSpec((bm, bk), lambda i, j, k: (i, k))
    b_spec = pl.BlockSpec((bn, bk), lambda i, j, k: (j, k)) if tb else pl.BlockSpec((bk, bn), lambda i, j, k: (k, j))
    o_spec = pl.BlockSpec((bm, bn), lambda i, j, k: (i, j))
    in_specs = [a_spec, b_spec] + ([o_spec] if add is not None else [])
    args = (a, b) + ((add,) if add is not None else ())
    return pl.pallas_call(
        body, name=name, grid=(m // bm, n // bn, nk),
        in_specs=in_specs, out_specs=o_spec,
        out_shape=jax.ShapeDtypeStruct((m, n), F32),
        scratch_shapes=[pltpu.VMEM((bm, bn), F32)],
        compiler_params=_params(("parallel", "parallel", "arbitrary")),
    )(*args)


def _rmsnorm_fwd(x, w, name):
    t, d = x.shape
    tr = _rows(t)

    def body(x_ref, w_ref, o_ref):
        xv = x_ref[...]
        rstd = lax.rsqrt(jnp.mean(xv * xv, axis=-1, keepdims=True) + RMS_EPS)
        o_ref[...] = (xv * rstd * w_ref[...]).astype(BF16)

    return pl.pallas_call(
        body, name=name, grid=(t // tr,),
        in_specs=[pl.BlockSpec((tr, d), lambda i: (i, 0)), pl.BlockSpec((1, d), lambda i: (0, 0))],
        out_specs=pl.BlockSpec((tr, d), lambda i: (i, 0)),
        out_shape=jax.ShapeDtypeStruct((t, d), BF16),
        compiler_params=_params(("parallel",)),
    )(x, w)


def _rmsnorm_bwd(x, w, dy, dres, name):
    t, d = x.shape
    tr = _rows(t)

    def body(x_ref, w_ref, dy_ref, dres_ref, dx_ref, dw_ref):
        i = pl.program_id(0)
        xv = x_ref[...]
        rstd = lax.rsqrt(jnp.mean(xv * xv, axis=-1, keepdims=True) + RMS_EPS)
        xhat = xv * rstd
        dyv = dy_ref[...]
        dxhat = dyv * w_ref[...]
        m = jnp.mean(dxhat * xhat, axis=-1, keepdims=True)
        dx_ref[...] = rstd * (dxhat - xhat * m) + dres_ref[...]
        part = jnp.sum(dyv * xhat, axis=0, keepdims=True)

        @pl.when(i == 0)
        def _():
            dw_ref[...] = part

        @pl.when(i > 0)
        def _():
            dw_ref[...] += part

    row = pl.BlockSpec((tr, d), lambda i: (i, 0))
    vec = pl.BlockSpec((1, d), lambda i: (0, 0))
    return pl.pallas_call(
        body, name=name, grid=(t // tr,),
        in_specs=[row, vec, row, row], out_specs=[row, vec],
        out_shape=[jax.ShapeDtypeStruct((t, d), F32), jax.ShapeDtypeStruct((1, d), F32)],
        compiler_params=_params(("arbitrary",)),
    )(x, w, dy, dres)


def _loss_head(h2, w, tgt):
    t, d = h2.shape
    tr = _rows(t)

    def body(x_ref, w_ref, t_ref, dx_ref, dw_ref, ls_ref):
        i = pl.program_id(0)
        xv = x_ref[...]
        wv = w_ref[...]
        rstd = lax.rsqrt(jnp.mean(xv * xv, axis=-1, keepdims=True) + RMS_EPS)
        xhat = xv * rstd
        err = xhat * wv - t_ref[...]
        dyv = err * (1.0 / d)
        dxhat = dyv * wv
        m = jnp.mean(dxhat * xhat, axis=-1, keepdims=True)
        dx_ref[...] = rstd * (dxhat - xhat * m)
        dw_part = jnp.sum(dyv * xhat, axis=0, keepdims=True)
        ls_part = jnp.sum(err * err, axis=0, keepdims=True)

        @pl.when(i == 0)
        def _():
            dw_ref[...] = dw_part
            ls_ref[...] = ls_part

        @pl.when(i > 0)
        def _():
            dw_ref[...] += dw_part
            ls_ref[...] += ls_part

    row = pl.BlockSpec((tr, d), lambda i: (i, 0))
    vec = pl.BlockSpec((1, d), lambda i: (0, 0))
    return pl.pallas_call(
        body, name="loss_head", grid=(t // tr,),
        in_specs=[row, vec, row], out_specs=[row, vec, vec],
        out_shape=[jax.ShapeDtypeStruct((t, d), F32), jax.ShapeDtypeStruct((1, d), F32),
                   jax.ShapeDtypeStruct((1, d), F32)],
        compiler_params=_params(("arbitrary",)),
    )(h2, w, tgt)


def _merge_fwd(proj, bg, ys, yl):
    t = proj.shape[0]
    d = D_MODEL
    tr = _rows(t)

    def body(ps_ref, pl_ref, bg_ref, ys_ref, yl_ref, o_ref):
        gs = _sigmoid(ps_ref[...] + bg_ref[:, 0:d])
        gl = _sigmoid(pl_ref[...] + bg_ref[:, d:2 * d])
        o_ref[...] = (gs * ys_ref[...] + gl * yl_ref[...]).astype(BF16)

    row = pl.BlockSpec((tr, d), lambda i: (i, 0))
    return pl.pallas_call(
        body, name="merge_fwd", grid=(t // tr,),
        in_specs=[row, pl.BlockSpec((tr, d), lambda i: (i, 1)), pl.BlockSpec((1, 2 * d), lambda i: (0, 0)), row, row],
        out_specs=row, out_shape=jax.ShapeDtypeStruct((t, d), BF16),
        compiler_params=_params(("parallel",)),
    )(proj, proj, bg, ys, yl)


def _merge_bwd(proj, bg, ys, yl, dm):
    t = proj.shape[0]
    d = D_MODEL
    tr = _rows(t)

    def body(ps_ref, pl_ref, bg_ref, ys_ref, yl_ref, dm_ref, dys_ref, dyl_ref, dg_ref, dbg_ref):
        i = pl.program_id(0)
        gs = _sigmoid(ps_ref[...] + bg_ref[:, 0:d])
        gl = _sigmoid(pl_ref[...] + bg_ref[:, d:2 * d])
        dmv = dm_ref[...]
        dys_ref[...] = (dmv * gs).astype(BF16)
        dyl_ref[...] = (dmv * gl).astype(BF16)
        dgs = dmv * ys_ref[...] * gs * (1.0 - gs)
        dgl = dmv * yl_ref[...] * gl * (1.0 - gl)
        dg_ref[:, 0:d] = dgs.astype(BF16)
        dg_ref[:, d:2 * d] = dgl.astype(BF16)

        @pl.when(i == 0)
        def _():
            dbg_ref[...] = jnp.zeros_like(dbg_ref)

        dbg_ref[:, 0:d] += jnp.sum(dgs, axis=0, keepdims=True)
        dbg_ref[:, d:2 * d] += jnp.sum(dgl, axis=0, keepdims=True)

    row = pl.BlockSpec((tr, d), lambda i: (i, 0))
    wide = pl.BlockSpec((tr, 2 * d), lambda i: (i, 0))
    vec = pl.BlockSpec((1, 2 * d), lambda i: (0, 0))
    return pl.pallas_call(
        body, name="merge_bwd", grid=(t // tr,),
        in_specs=[row, pl.BlockSpec((tr, d), lambda i: (i, 1)), vec, row, row, row],
        out_specs=[row, row, wide, vec],
        out_shape=[jax.ShapeDtypeStruct((t, d), BF16), jax.ShapeDtypeStruct((t, d), BF16),
                   jax.ShapeDtypeStruct((t, 2 * d), BF16), jax.ShapeDtypeStruct((1, 2 * d), F32)],
        compiler_params=_params(("arbitrary",)),
    )(proj, proj, bg, ys, yl, dm)


def _swiglu_fwd(gu):
    t = gu.shape[0]
    f = FFN_HIDDEN
    tr = _rows(t)

    def body(g_ref, u_ref, o_ref):
        g = g_ref[...]
        o_ref[...] = (g * _sigmoid(g) * u_ref[...]).astype(BF16)

    return pl.pallas_call(
        body, name="swiglu_fwd", grid=(t // tr,),
        in_specs=[pl.BlockSpec((tr, f), lambda i: (i, 0)), pl.BlockSpec((tr, f), lambda i: (i, 1))],
        out_specs=pl.BlockSpec((tr, f), lambda i: (i, 0)),
        out_shape=jax.ShapeDtypeStruct((t, f), BF16),
        compiler_params=_params(("parallel",)),
    )(gu, gu)


def _swiglu_bwd(gu, dact):
    t = gu.shape[0]
    f = FFN_HIDDEN
    tr = _rows(t)

    def body(g_ref, u_ref, da_ref, o_ref):
        g = g_ref[...]
        sg = _sigmoid(g)
        da = da_ref[...]
        o_ref[:, 0:f] = (da * u_ref[...] * (sg * (1.0 + g * (1.0 - sg)))).astype(BF16)
        o_ref[:, f:2 * f] = (da * g * sg).astype(BF16)

    return pl.pallas_call(
        body, name="swiglu_bwd", grid=(t // tr,),
        in_specs=[pl.BlockSpec((tr, f), lambda i: (i, 0)), pl.BlockSpec((tr, f), lambda i: (i, 1)),
                  pl.BlockSpec((tr, f), lambda i: (i, 0))],
        out_specs=pl.BlockSpec((tr, 2 * f), lambda i: (i, 0)),
        out_shape=jax.ShapeDtypeStruct((t, 2 * f), BF16),
        compiler_params=_params(("parallel",)),
    )(gu, gu, dact)


def _conv_pre(xv, wv, bv):
    pre = bv + wv[CONV_K - 1:CONV_K, :] * xv
    for k in range(CONV_K - 1):
        pre = pre + wv[k:k + 1, :] * _shift_down(xv, CONV_K - 1 - k)
    return pre


def _ssm_conv_fwd(proj, w, b):
    t = proj.shape[0]
    nb = SSM_CONV_DIM // LANES
    c0 = COL_XBC // LANES

    def body(x_ref, w_ref, b_ref, o_ref):
        pre = _conv_pre(x_ref[...], w_ref[...], b_ref[...])
        o_ref[...] = pre * _sigmoid(pre)

    return pl.pallas_call(
        body, name="ssm_conv_fwd", grid=(nb,),
        in_specs=[pl.BlockSpec((t, LANES), lambda j: (0, c0 + j)), pl.BlockSpec((CONV_K, LANES), lambda j: (0, j)),
                  pl.BlockSpec((1, LANES), lambda j: (0, j))],
        out_specs=pl.BlockSpec((t, LANES), lambda j: (0, j)),
        out_shape=jax.ShapeDtypeStruct((t, SSM_CONV_DIM), F32),
        compiler_params=_params(("parallel",)),
    )(proj, w, b)


def _ssm_conv_bwd(proj, w, b, dact):
    t = proj.shape[0]
    nb = SSM_CONV_DIM // LANES
    c0 = COL_XBC // LANES

    def body(x_ref, w_ref, b_ref, da_ref, dx_ref, dw_ref, db_ref):
        xv = x_ref[...]
        wv = w_ref[...]
        pre = _conv_pre(xv, wv, b_ref[...])
        sg = _sigmoid(pre)
        dpre = da_ref[...] * (sg * (1.0 + pre * (1.0 - sg)))
        dx = wv[CONV_K - 1:CONV_K, :] * dpre
        for k in range(CONV_K - 1):
            dx = dx + wv[k:k + 1, :] * _shift_up(dpre, CONV_K - 1 - k)
        dx_ref[...] = dx.astype(BF16)
        for k in range(CONV_K):
            dw_ref[k:k + 1, :] = jnp.sum(dpre * _shift_down(xv, CONV_K - 1 - k), axis=0, keepdims=True)
        db_ref[...] = jnp.sum(dpre, axis=0, keepdims=True)

    col = pl.BlockSpec((t, LANES), lambda j: (0, j))
    wsp = pl.BlockSpec((CONV_K, LANES), lambda j: (0, j))
    bsp = pl.BlockSpec((1, LANES), lambda j: (0, j))
    return pl.pallas_call(
        body, name="ssm_conv_bwd", grid=(nb,),
        in_specs=[pl.BlockSpec((t, LANES), lambda j: (0, c0 + j)), wsp, bsp, col],
        out_specs=[col, wsp, bsp],
        out_shape=[jax.ShapeDtypeStruct((t, SSM_CONV_DIM), BF16), jax.ShapeDtypeStruct((CONV_K, SSM_CONV_DIM), F32),
                   jax.ShapeDtypeStruct((1, SSM_CONV_DIM), F32)],
        compiler_params=_params(("parallel",)),
    )(proj, w, b, dact)


def _ssd_chunk_terms(dtr, bias, alog):
    a = -jnp.exp(alog)
    dt = _softplus(dtr + bias)
    row = _iota((SSM_CHUNK, SSM_CHUNK), 0)
    col = _iota((SSM_CHUNK, SSM_CHUNK), 1)
    tri = (row >= col).astype(F32)
    cs = _fdot(tri, dt * a)
    dec = jnp.exp(cs[SSM_CHUNK - 1:SSM_CHUNK, :] - cs)
    ecs = jnp.exp(cs)
    off = _iota((LANES, SSM_INNER), 1) - SSM_HEAD_DIM * _iota((LANES, SSM_INNER), 0)
    expand = jnp.logical_and(off >= 0, off < SSM_HEAD_DIM).astype(F32)
    return a, dt, cs, dec, ecs, expand, row, col


def _ssd_specs(t):
    nc = t // SSM_CHUNK
    xs = pl.BlockSpec((SSM_CHUNK, SSM_INNER), lambda c: (c, 0))
    bm = pl.BlockSpec((SSM_CHUNK, SSM_BC), lambda c: (c, SSM_INNER // SSM_BC))
    cm = pl.BlockSpec((SSM_CHUNK, SSM_BC), lambda c: (c, SSM_INNER // SSM_BC + 1))
    dtr = pl.BlockSpec((SSM_CHUNK, LANES), lambda c: (c, COL_DT // LANES))
    vec = pl.BlockSpec((1, LANES), lambda c: (0, 0))
    wide = pl.BlockSpec((1, SSM_INNER), lambda c: (0, 0))
    return nc, xs, bm, cm, dtr, vec, wide


def _ssd_fwd(xbc_act, proj, bias, alog, dexp):
    t = proj.shape[0]
    nc, xs_s, bm_s, cm_s, dtr_s, vec, wide = _ssd_specs(t)

    def body(xs_ref, b_ref, c_ref, dtr_ref, bias_ref, alog_ref, dexp_ref, y_ref, sin_ref, state):
        @pl.when(pl.program_id(0) == 0)
        def _():
            state[...] = jnp.zeros_like(state)

        a, dt, cs, dec, ecs, expand, row, col = _ssd_chunk_terms(dtr_ref[...], bias_ref[...], alog_ref[...])
        cst = cs.T
        dt_x = _fdot(dt, expand)
        dec_x = _fdot(dec, expand)
        ecs_x = _fdot(ecs, expand)
        xs = xs_ref[...]
        xdt = xs * dt_x
        xdec = xdt * dec_x
        lane_lo = col < SSM_HEAD_DIM
        causal = row >= col
        sin_ref[0] = state[...]
        for g in range(SSM_GROUPS):
            bg = b_ref[:, g * SSM_STATE:(g + 1) * SSM_STATE].astype(BF16)
            cg = c_ref[:, g * SSM_STATE:(g + 1) * SSM_STATE].astype(BF16)
            cb = _bdot(cg, bg, NT)
            for q in range(SSM_PAIRS // SSM_GROUPS):
                pq = g * (SSM_PAIRS // SSM_GROUPS) + q
                sl = slice(pq * LANES, (pq + 1) * LANES)
                xp = xdt[:, sl].astype(BF16)
                yd = []
                for hh in range(2):
                    h = 2 * pq + hh
                    lmat = jnp.exp(jnp.where(causal, cs[:, h:h + 1] - cst[h:h + 1, :], -jnp.inf))
                    yd.append(_bdot(cb * lmat, xp))
                s_in = state[pq]
                y_off = _bdot(cg, s_in) * ecs_x[:, sl]
                y_ref[:, sl] = jnp.where(lane_lo, yd[0], yd[1]) + y_off + xs[:, sl] * dexp_ref[:, sl]
                state[pq] = s_in * ecs_x[SSM_CHUNK - 1:SSM_CHUNK, sl] + _bdot(bg, xdec[:, sl], TN)

    return pl.pallas_call(
        body, name="ssd_fwd", grid=(nc,),
        in_specs=[xs_s, bm_s, cm_s, dtr_s, vec, vec, wide],
        out_specs=[pl.BlockSpec((SSM_CHUNK, SSM_INNER), lambda c: (c, 0)),
                   pl.BlockSpec((1, SSM_PAIRS, SSM_STATE, LANES), lambda c: (c, 0, 0, 0))],
        out_shape=[jax.ShapeDtypeStruct((t, SSM_INNER), F32),
                   jax.ShapeDtypeStruct((nc, SSM_PAIRS, SSM_STATE, LANES), F32)],
        scratch_shapes=[pltpu.VMEM((SSM_PAIRS, SSM_STATE, LANES), F32)],
        compiler_params=_params(("arbitrary",)),
    )(xbc_act, xbc_act, xbc_act, proj, bias, alog, dexp)


def _ssd_bwd(xbc_act, proj, s_in_all, dy, bias, alog, dexp):
    t = proj.shape[0]
    nc = t // SSM_CHUNK
    last = nc - 1
    xs_s = pl.BlockSpec((SSM_CHUNK, SSM_INNER), lambda c: (last - c, 0))
    bm_s = pl.BlockSpec((SSM_CHUNK, SSM_BC), lambda c: (last - c, SSM_INNER // SSM_BC))
    cm_s = pl.BlockSpec((SSM_CHUNK, SSM_BC), lambda c: (last - c, SSM_INNER // SSM_BC + 1))
    dtr_s = pl.BlockSpec((SSM_CHUNK, LANES), lambda c: (last - c, COL_DT // LANES))
    sin_s = pl.BlockSpec((1, SSM_PAIRS, SSM_STATE, LANES), lambda c: (last - c, 0, 0, 0))
    vec = pl.BlockSpec((1, LANES), lambda c: (0, 0))
    wide = pl.BlockSpec((1, SSM_INNER), lambda c: (0, 0))

    def body(xs_ref, b_ref, c_ref, dtr_ref, sin_ref, dy_ref, bias_ref, alog_ref, dexp_ref,
             dxbc_ref, ddtr_ref, dbias_ref, dalog_ref, ddcol_ref, dstate, dxdt_s, yoff_s, rx_s, trow_s):
        @pl.when(pl.program_id(0) == 0)
        def _():
            dstate[...] = jnp.zeros_like(dstate)
            trow_s[...] = jnp.zeros_like(trow_s)
            dbias_ref[...] = jnp.zeros_like(dbias_ref)
            dalog_ref[...] = jnp.zeros_like(dalog_ref)
            ddcol_ref[...] = jnp.zeros_like(ddcol_ref)

        dtr = dtr_ref[...]
        a, dt, cs, dec, ecs, expand, row, col = _ssd_chunk_terms(dtr, bias_ref[...], alog_ref[...])
        cst = cs.T
        dt_x = _fdot(dt, expand)
        dec_x = _fdot(dec, expand)
        ecs_x = _fdot(ecs, expand)
        xs = xs_ref[...]
        dyv = dy_ref[...]
        xdt = xs * dt_x
        lane_lo = col < SSM_HEAD_DIM
        causal = row >= col
        ddcol_ref[...] += jnp.sum(dyv * xs, axis=0, keepdims=True)
        dcs_col = jnp.zeros((SSM_CHUNK, LANES), F32)
        dcs_row = jnp.zeros((LANES, SSM_CHUNK), F32)
        for g in range(SSM_GROUPS):
            bg = b_ref[:, g * SSM_STATE:(g + 1) * SSM_STATE].astype(BF16)
            cg = c_ref[:, g * SSM_STATE:(g + 1) * SSM_STATE].astype(BF16)
            cb = _bdot(cg, bg, NT)
            dgm = jnp.zeros((SSM_CHUNK, SSM_CHUNK), F32)
            dbg = jnp.zeros((SSM_CHUNK, SSM_STATE), F32)
            dcg = jnp.zeros((SSM_CHUNK, SSM_STATE), F32)
            for q in range(SSM_PAIRS // SSM_GROUPS):
                pq = g * (SSM_PAIRS // SSM_GROUPS) + q
                sl = slice(pq * LANES, (pq + 1) * LANES)
                dyp = dyv[:, sl]
                xp = xdt[:, sl]
                dxh = []
                for hh in range(2):
                    h = 2 * pq + hh
                    lmat = jnp.exp(jnp.where(causal, cs[:, h:h + 1] - cst[h:h + 1, :], -jnp.inf))
                    mmat = cb * lmat
                    dyh = jnp.where(lane_lo if hh == 0 else jnp.logical_not(lane_lo), dyp, 0.0)
                    dmm = _bdot(dyh, xp, NT)
                    pm = dmm * mmat
                    dcs_col = jnp.where(col == h, jnp.sum(pm, axis=1, keepdims=True), dcs_col)
                    dcs_row = jnp.where(row == h, jnp.sum(pm, axis=0, keepdims=True), dcs_row)
                    dgm = dgm + dmm * lmat
                    dxh.append(_bdot(mmat, dyp, TN))
                s_in = sin_ref[0, pq]
                ecs_p = ecs_x[:, sl]
                dec_p = dec_x[:, sl]
                etot_p = ecs_x[SSM_CHUNK - 1:SSM_CHUNK, sl]
                yoff_s[:, sl] = dyp * (_bdot(cg, s_in) * ecs_p)
                dq = dyp * ecs_p
                dcg = dcg + _bdot(dq, s_in, NT)
                ds = dstate[pq]
                r = _bdot(bg, ds)
                rx_s[:, sl] = r * xp
                dxdt_s[:, sl] = jnp.where(lane_lo, dxh[0], dxh[1]) + dec_p * r
                dbg = dbg + _bdot(xp * dec_p, ds, NT)
                trow_s[0:1, sl] = jnp.sum(ds * s_in, axis=0, keepdims=True) * etot_p
                dstate[pq] = etot_p * ds + _bdot(cg, dq, TN)
            dcg = dcg + _bdot(dgm, bg)
            dbg = dbg + _bdot(dgm, cg, TN)
            dxbc_ref[:, SSM_INNER + g * SSM_STATE:SSM_INNER + (g + 1) * SSM_STATE] = dbg
            dxbc_ref[:, SSM_INNER + SSM_BC + g * SSM_STATE:SSM_INNER + SSM_BC + (g + 1) * SSM_STATE] = dcg
        ddec = _fdot(rx_s[...], expand, NT) * dec
        dtot = _fdot(trow_s[...], expand, NT)[0:1, :]
        dcs = dcs_col - dcs_row.T + _fdot(yoff_s[...], expand, NT) - ddec
        dcs = dcs + jnp.where(row == SSM_CHUNK - 1, jnp.sum(ddec, axis=0, keepdims=True) + dtot, 0.0)
        da = _fdot((row <= col).astype(F32), dcs)
        dxdt = dxdt_s[...]
        ddt = da * a + _fdot(dxdt * xs, expand, NT)
        dalog_ref[...] += jnp.sum(da * dt, axis=0, keepdims=True) * a
        ddtr = ddt * _sigmoid(dtr + bias_ref[...])
        ddtr_ref[...] = ddtr.astype(BF16)
        dbias_ref[...] += jnp.sum(ddtr, axis=0, keepdims=True)
        dxbc_ref[:, 0:SSM_INNER] = dxdt * dt_x + dyv * dexp_ref[...]

    return pl.pallas_call(
        body, name="ssd_bwd", grid=(nc,),
        in_specs=[xs_s, bm_s, cm_s, dtr_s, sin_s, pl.BlockSpec((SSM_CHUNK, SSM_INNER), lambda c: (last - c, 0)),
                  vec, vec, wide],
        out_specs=[pl.BlockSpec((SSM_CHUNK, SSM_CONV_DIM), lambda c: (last - c, 0)),
                   pl.BlockSpec((SSM_CHUNK, LANES), lambda c: (last - c, 0)), vec, vec, wide],
        out_shape=[jax.ShapeDtypeStruct((t, SSM_CONV_DIM), F32), jax.ShapeDtypeStruct((t, LANES), BF16),
                   jax.ShapeDtypeStruct((1, LANES), F32), jax.ShapeDtypeStruct((1, LANES), F32),
                   jax.ShapeDtypeStruct((1, SSM_INNER), F32)],
        scratch_shapes=[pltpu.VMEM((SSM_PAIRS, SSM_STATE, LANES), F32),
                        pltpu.VMEM((SSM_CHUNK, SSM_INNER), F32), pltpu.VMEM((SSM_CHUNK, SSM_INNER), F32),
                        pltpu.VMEM((SSM_CHUNK, SSM_INNER), F32), pltpu.VMEM((SUBLANES, SSM_INNER), F32)],
        compiler_params=_params(("arbitrary",)),
    )(xbc_act, xbc_act, xbc_act, proj, s_in_all, dy, bias, alog, dexp)


def _group_rstd(y):
    n = SSM_INNER // SSM_GROUPS
    parts = []
    for g in range(SSM_GROUPS):
        yg = y[:, g * n:(g + 1) * n]
        r = lax.rsqrt(jnp.mean(yg * yg, axis=-1, keepdims=True) + RMS_EPS)
        parts.append(jnp.broadcast_to(r, yg.shape))
    return jnp.concatenate(parts, axis=1)


def _group_mean(v):
    n = SSM_INNER // SSM_GROUPS
    parts = []
    for g in range(SSM_GROUPS):
        vg = v[:, g * n:(g + 1) * n]
        parts.append(jnp.broadcast_to(jnp.mean(vg, axis=-1, keepdims=True), vg.shape))
    return jnp.concatenate(parts, axis=1)


def _ssm_post_fwd(y_ssd, proj, nw):
    t = proj.shape[0]
    n = SSM_INNER
    tr = _rows(t)

    def body(y_ref, z_ref, nw_ref, o_ref):
        z = z_ref[...]
        y = y_ref[...] * (z * _sigmoid(z))
        o_ref[...] = (y * _group_rstd(y) * nw_ref[...]).astype(BF16)

    row = pl.BlockSpec((tr, n), lambda i: (i, 0))
    return pl.pallas_call(
        body, name="ssm_post_fwd", grid=(t // tr,),
        in_specs=[row, pl.BlockSpec((tr, n), lambda i: (i, COL_Z // n)), pl.BlockSpec((1, n), lambda i: (0, 0))],
        out_specs=row, out_shape=jax.ShapeDtypeStruct((t, n), BF16),
        compiler_params=_params(("parallel",)),
    )(y_ssd, proj, nw)


def _ssm_post_bwd(y_ssd, proj, nw, dout):
    t = proj.shape[0]
    n = SSM_INNER
    tr = _rows(t)

    def body(y_ref, z_ref, nw_ref, do_ref, dy_ref, dz_ref, dnw_ref):
        i = pl.program_id(0)
        z = z_ref[...]
        sg = _sigmoid(z)
        sz = z * sg
        ys = y_ref[...]
        y = ys * sz
        rstd = _group_rstd(y)
        yn = y * rstd
        dov = do_ref[...]
        dyn = dov * nw_ref[...]
        dyg = rstd * (dyn - yn * _group_mean(dyn * yn))
        dy_ref[...] = dyg * sz
        dz_ref[...] = (dyg * ys * (sg * (1.0 + z * (1.0 - sg)))).astype(BF16)
        part = jnp.sum(dov * yn, axis=0, keepdims=True)

        @pl.when(i == 0)
        def _():
            dnw_ref[...] = part

        @pl.when(i > 0)
        def _():
            dnw_ref[...] += part

    row = pl.BlockSpec((tr, n), lambda i: (i, 0))
    vec = pl.BlockSpec((1, n), lambda i: (0, 0))
    return pl.pallas_call(
        body, name="ssm_post_bwd", grid=(t // tr,),
        in_specs=[row, pl.BlockSpec((tr, n), lambda i: (i, COL_Z // n)), vec, row],
        out_specs=[row, row, vec],
        out_shape=[jax.ShapeDtypeStruct((t, n), F32), jax.ShapeDtypeStruct((t, n), BF16),
                   jax.ShapeDtypeStruct((1, n), F32)],
        compiler_params=_params(("arbitrary",)),
    )(y_ssd, proj, nw, dout)


GELU_C = math.sqrt(2.0 / math.pi)
GELU_K = 0.044715


def _gelu_parts(y):
    th = jnp.tanh(GELU_C * (y + GELU_K * y * y * y))
    val = 0.5 * y * (1.0 + th)
    grad = 0.5 * (1.0 + th) + 0.5 * y * (1.0 - th * th) * GELU_C * (1.0 + 3.0 * GELU_K * y * y)
    return val, grad


def _scan_tiles(a_ref, b_ref, h_ref, n_rows, reverse):
    n_tiles = n_rows // SUBLANES
    shape = (SUBLANES, a_ref.shape[1])
    row = _iota(shape, 0)

    def step(k, carry):
        tile = (n_tiles - 1 - k) if reverse else k
        at = pl.ds(pl.multiple_of(tile * SUBLANES, SUBLANES), SUBLANES)
        av = a_ref[at, :]
        bv = b_ref[at, :]
        for s in (1, 2, 4):
            if reverse:
                keep = row < SUBLANES - s
                a_sh = jnp.where(keep, pltpu.roll(av, SUBLANES - s, 0), 1.0)
                b_sh = jnp.where(keep, pltpu.roll(bv, SUBLANES - s, 0), 0.0)
            else:
                keep = row >= s
                a_sh = jnp.where(keep, pltpu.roll(av, s, 0), 1.0)
                b_sh = jnp.where(keep, pltpu.roll(bv, s, 0), 0.0)
            bv = av * b_sh + bv
            av = av * a_sh
        hv = bv + av * carry
        h_ref[at, :] = hv
        return hv[0:1, :] if reverse else hv[SUBLANES - 1:SUBLANES, :]

    lax.fori_loop(0, n_tiles, step, jnp.zeros((1, a_ref.shape[1]), F32))


def _lru_gates(xl, cw, cb, wr, br, wi, bi, lam):
    u = cb + cw[CONV_K - 1:CONV_K, :] * xl
    for k in range(CONV_K - 1):
        u = u + cw[k:k + 1, :] * _shift_down(xl, CONV_K - 1 - k)
    r = _sigmoid(_bdot(u, wr) + br)
    i = _sigmoid(_bdot(u, wi) + bi)
    sp = _softplus(-lam)
    la = -LRU_C * r * sp
    a = jnp.exp(la)
    mult = jnp.sqrt(-jnp.tanh(la) * (a * a + 1.0))
    return u, r, i, sp, a, mult


def _lru_specs(t):
    c_x = COL_LRU_X // LANES
    c_y = COL_LRU_Y // LANES
    xl = pl.BlockSpec((t, LANES), lambda j: (0, c_x + j))
    yl = pl.BlockSpec((t, LANES), lambda j: (0, c_y + j))
    col = pl.BlockSpec((t, LANES), lambda j: (0, j))
    cw = pl.BlockSpec((CONV_K, LANES), lambda j: (0, j))
    vec = pl.BlockSpec((1, LANES), lambda j: (0, j))
    wblk = pl.BlockSpec((1, LANES, LANES), lambda j: (j, 0, 0))
    return xl, yl, col, cw, vec, wblk


def _lru_fwd(proj, cw, cb, wr, br, wi, bi, lam):
    t = proj.shape[0]
    xl_s, yl_s, col, cw_s, vec, wblk = _lru_specs(t)

    def body(xl_ref, yl_ref, cw_ref, cb_ref, wr_ref, br_ref, wi_ref, bi_ref, lam_ref, o_ref, a_s, b_s, h_s):
        u, r, i, sp, a, mult = _lru_gates(xl_ref[...], cw_ref[...], cb_ref[...], wr_ref[0], br_ref[...],
                                          wi_ref[0], bi_ref[...], lam_ref[...])
        a_s[...] = a
        b_s[...] = mult * (i * u)
        _scan_tiles(a_s, b_s, h_s, t, reverse=False)
        o_ref[...] = (h_s[...] * _gelu_parts(yl_ref[...])[0]).astype(BF16)

    return pl.pallas_call(
        body, name="lru_fwd", grid=(LRU_BLOCKS,),
        in_specs=[xl_s, yl_s, cw_s, vec, wblk, vec, wblk, vec, vec],
        out_specs=col, out_shape=jax.ShapeDtypeStruct((t, LRU_WIDTH), BF16),
        scratch_shapes=[pltpu.VMEM((t, LANES), F32)] * 3,
        compiler_params=_params(("arbitrary",), big=True),
    )(proj, proj, cw, cb, wr, br, wi, bi, lam)


def _lru_bwd(proj, cw, cb, wr, br, wi, bi, lam, dout):
    t = proj.shape[0]
    xl_s, yl_s, col, cw_s, vec, wblk = _lru_specs(t)

    def body(xl_ref, yl_ref, cw_ref, cb_ref, wr_ref, br_ref, wi_ref, bi_ref, lam_ref, do_ref,
             dxl_ref, dyl_ref, dcw_ref, dcb_ref, dwr_ref, dbr_ref, dwi_ref, dbi_ref, dlam_ref,
             a_s, b_s, h_s, g_s):
        xl = xl_ref[...]
        cwv = cw_ref[...]
        lam = lam_ref[...]
        u, r, i, sp, a, mult = _lru_gates(xl, cwv, cb_ref[...], wr_ref[0], br_ref[...], wi_ref[0], bi_ref[...], lam)
        v = i * u
        a_s[...] = a
        b_s[...] = mult * v
        _scan_tiles(a_s, b_s, h_s, t, reverse=False)
        gl, dgl = _gelu_parts(yl_ref[...])
        dov = do_ref[...]
        h = h_s[...]
        dyl_ref[...] = (dov * h * dgl).astype(BF16)
        b_s[...] = dov * gl
        a_s[...] = _shift_up(a, 1)
        _scan_tiles(a_s, b_s, g_s, t, reverse=True)
        g = g_s[...]
        da = g * _shift_down(h, 1)
        dmult = g * v
        dv = g * mult
        dla = da * a - dmult * (a * a) / mult
        dr = dla * (-LRU_C * sp)
        dsp = jnp.sum(dla * (-LRU_C * r), axis=0, keepdims=True)
        dlam_ref[...] = -dsp * _sigmoid(-lam)
        dpr = dr * r * (1.0 - r)
        dpi = dv * u * i * (1.0 - i)
        dbr_ref[...] = jnp.sum(dpr, axis=0, keepdims=True)
        dbi_ref[...] = jnp.sum(dpi, axis=0, keepdims=True)
        dwr_ref[0] = _bdot(u, dpr, TN)
        dwi_ref[0] = _bdot(u, dpi, TN)
        du = dv * i + _bdot(dpr, wr_ref[0], NT) + _bdot(dpi, wi_ref[0], NT)
        dxl = cwv[CONV_K - 1:CONV_K, :] * du
        for k in range(CONV_K - 1):
            dxl = dxl + cwv[k:k + 1, :] * _shift_up(du, CONV_K - 1 - k)
        dxl_ref[...] = dxl.astype(BF16)
        for k in range(CONV_K):
            dcw_ref[k:k + 1, :] = jnp.sum(du * _shift_down(xl, CONV_K - 1 - k), axis=0, keepdims=True)
        dcb_ref[...] = jnp.sum(du, axis=0, keepdims=True)

    return pl.pallas_call(
        body, name="lru_bwd", grid=(LRU_BLOCKS,),
        in_specs=[xl_s, yl_s, cw_s, vec, wblk, vec, wblk, vec, vec, col],
        out_specs=[col, col, cw_s, vec, wblk, vec, wblk, vec, vec],
        out_shape=[jax.ShapeDtypeStruct((t, LRU_WIDTH), BF16), jax.ShapeDtypeStruct((t, LRU_WIDTH), BF16),
                   jax.ShapeDtypeStruct((CONV_K, LRU_WIDTH), F32), jax.ShapeDtypeStruct((1, LRU_WIDTH), F32),
                   jax.ShapeDtypeStruct((LRU_BLOCKS, LANES, LANES), F32), jax.ShapeDtypeStruct((1, LRU_WIDTH), F32),
                   jax.ShapeDtypeStruct((LRU_BLOCKS, LANES, LANES), F32), jax.ShapeDtypeStruct((1, LRU_WIDTH), F32),
                   jax.ShapeDtypeStruct((1, LRU_WIDTH), F32)],
        scratch_shapes=[pltpu.VMEM((t, LANES), F32)] * 4,
        compiler_params=_params(("arbitrary",), big=True),
    )(proj, proj, cw, cb, wr, br, wi, bi, lam, dout)


def _mesh_pos():
    return lax.axis_index("x"), lax.axis_index("y"), lax.axis_index("c")


def _all_gather(vs, name):
    n = len(vs)

    def body(*refs):
        v_refs, out_refs = refs[:n], refs[n:2 * n]
        send_sems, recv_sems, local_sems = refs[2 * n:]
        x, y, c = _mesh_pos()
        me, sibling = (x, y, c), (x, y, 1 - c)
        chips = [(1 - x, y), (x, 1 - y), (1 - x, 1 - y)]

        def block(a, px, py, pc):
            return out_refs[a].at[4 * px + 2 * py + pc]

        def copy(a, k, blk, to, src=None):
            return pltpu.make_async_remote_copy(
                src_ref=block(a, *blk) if src is None else src, dst_ref=block(a, *blk),
                send_sem=send_sems.at[a, k], recv_sem=recv_sems.at[a, k], device_id=to, device_id_type=MESH)

        started = []
        for a in range(n):
            mine = pltpu.make_async_copy(v_refs[a], block(a, *me), local_sems.at[a])
            mine.start()
            started.append(mine)
        sends = []
        for a in range(n):
            first = [copy(a, 0, me, sibling, src=v_refs[a])]
            first += [copy(a, 1 + j, me, (*chip, c), src=v_refs[a]) for j, chip in enumerate(chips)]
            for cp in first:
                cp.start()
            sends += first
        for j, chip in enumerate(chips):
            for a in range(n):
                copy(a, 1 + j, (*chip, c), me).wait_recv()
                fwd = copy(a, 4 + j, (*chip, c), sibling)
                fwd.start()
                sends.append(fwd)
        for a in range(n):
            copy(a, 0, sibling, me).wait_recv()
            for j, chip in enumerate(chips):
                copy(a, 4 + j, (*chip, 1 - c), me).wait_recv()
        for cp in sends:
            cp.wait_send()
        for mine in started:
            mine.wait()

    hbm = pl.BlockSpec(memory_space=pl.ANY)
    return pl.pallas_call(
        body, name=name,
        out_shape=[jax.ShapeDtypeStruct((N_DEV,) + v.shape, v.dtype) for v in vs],
        in_specs=[hbm] * n, out_specs=[hbm] * n,
        scratch_shapes=[pltpu.SemaphoreType.DMA((n, 7)), pltpu.SemaphoreType.DMA((n, 7)),
                        pltpu.SemaphoreType.DMA((n,))],
    )(*vs)


def _exchange(parts, name):
    n = len(parts)

    def body(*refs):
        p_refs, out_refs = refs[:n], refs[n:2 * n]
        send_sems, recv_sems, local_sems = refs[2 * n:]
        x, y, c = _mesh_pos()
        me = 4 * x + 2 * y + c
        local = []
        for a in range(n):
            mine = pltpu.make_async_copy(p_refs[a].at[me], out_refs[a].at[me], local_sems.at[a])
            mine.start()
            local.append(mine)
        copies = []
        for k in range(1, N_DEV):
            px = (1 - x) if k & 4 else x
            py = (1 - y) if k & 2 else y
            pc = (1 - c) if k & 1 else c
            for a in range(n):
                cp = pltpu.make_async_remote_copy(
                    src_ref=p_refs[a].at[4 * px + 2 * py + pc], dst_ref=out_refs[a].at[me],
                    send_sem=send_sems.at[a, k - 1], recv_sem=recv_sems.at[a, k - 1],
                    device_id=(px, py, pc), device_id_type=MESH)
                cp.start()
                copies.append(cp)
        for cp in copies:
            cp.wait()
        for mine in local:
            mine.wait()

    hbm = pl.BlockSpec(memory_space=pl.ANY)
    return pl.pallas_call(
        body, name=name,
        out_shape=[jax.ShapeDtypeStruct(p.shape, p.dtype) for p in parts],
        in_specs=[hbm] * n, out_specs=[hbm] * n,
        scratch_shapes=[pltpu.SemaphoreType.DMA((n, 7)), pltpu.SemaphoreType.DMA((n, 7)),
                        pltpu.SemaphoreType.DMA((n,))],
    )(*parts)


def _sum_sources(recv, tile, name):
    n, rows, width = recv.shape

    def body(r_ref, o_ref):
        acc = r_ref[0].astype(F32)
        for s in range(1, n):
            acc = acc + r_ref[s].astype(F32)
        o_ref[...] = acc

    return pl.pallas_call(
        body, name=name, grid=(rows // tile,),
        in_specs=[pl.BlockSpec((n, tile, width), lambda i: (0, i, 0))],
        out_specs=pl.BlockSpec((tile, width), lambda i: (i, 0)),
        out_shape=jax.ShapeDtypeStruct((rows, width), F32),
        compiler_params=_params(("parallel",)),
    )(recv)


def _row_tile(rows):
    for tile in range(128, 15, -16):
        if rows % tile == 0:
            return tile
    return rows


def _adamw(w, recv, m, v, name):
    rows, width = w.shape
    n = recv.shape[0]
    tile = _row_tile(rows)
    c1 = 1.0 - ADAM_B1 ** ADAM_STEP
    c2 = 1.0 - ADAM_B2 ** ADAM_STEP

    def body(w_ref, r_ref, m_ref, v_ref, g_ref, d_ref, nm_ref, nv_ref):
        gv = r_ref[0].astype(F32)
        for s in range(1, n):
            gv = gv + r_ref[s].astype(F32)
        nm = ADAM_B1 * m_ref[...] + (1.0 - ADAM_B1) * gv
        nv = ADAM_B2 * v_ref[...] + (1.0 - ADAM_B2) * (gv * gv)
        g_ref[...] = gv
        nm_ref[...] = nm
        nv_ref[...] = nv
        d_ref[...] = -ADAM_LR * ((nm / c1) / (jnp.sqrt(nv / c2) + ADAM_EPS) + ADAM_WD * w_ref[...])

    spec = pl.BlockSpec((tile, width), lambda i: (i, 0))
    shape = jax.ShapeDtypeStruct((rows, width), F32)
    return pl.pallas_call(
        body, name=name, grid=(rows // tile,),
        in_specs=[spec, pl.BlockSpec((n, tile, width), lambda i: (0, i, 0)), spec, spec],
        out_specs=[spec] * 4, out_shape=[shape] * 4,
        compiler_params=_params(("parallel",)),
    )(w, recv, m, v)


BIG_NAMES = ("w_in", "w_out_ssm", "w_out_lru", "w_out", "w_ffn_in", "w_ffn_out", "ssm_conv_w", "lru_conv_w")
COL_SHARDED = ("w_in", "w_ffn_in", "ssm_conv_w", "lru_conv_w")
SMALL_NAMES = ("norm1_w", "b_branch_gate", "ssm_conv_b", "ssm_dt_bias", "ssm_a_log", "ssm_d", "ssm_norm_w",
               "lru_conv_b", "lru_w_r", "lru_b_r", "lru_w_i", "lru_b_i", "lru_lambda", "norm2_w", "norm_f_w")


def _pack(arrays, total):
    flat = []
    used = 0
    for a in arrays:
        a = a.reshape(-1)
        pad = (-a.shape[0]) % LANES
        flat.append(a)
        if pad:
            flat.append(jnp.zeros((pad,), a.dtype))
        used += a.shape[0] + pad
    assert used <= total, (used, total)
    if total > used:
        flat.append(jnp.zeros((total - used,), arrays[0].dtype))
    return jnp.concatenate(flat)


def _unpack(flat, shapes):
    out = []
    off = 0
    for shp in shapes:
        n = math.prod(shp)
        out.append(flat[off:off + n].reshape(shp))
        off += n + (-n) % LANES
    return out


def _col_shards(full):
    rows, cols = full.shape
    return full.reshape(rows, N_DEV, cols // N_DEV).transpose(1, 0, 2)


def _row_shards(full):
    rows, cols = full.shape
    return full.reshape(N_DEV, rows // N_DEV, cols)


def _from_col_shards(g):
    n, rows, w = g.shape
    return g.transpose(1, 0, 2).reshape(rows, n * w)


def kernel(x, norm1_w, w_in, b_branch_gate, ssm_conv_w, ssm_conv_b, ssm_dt_bias, ssm_a_log, ssm_d, ssm_norm_w, w_out_ssm, lru_conv_w, lru_conv_b, lru_w_r, lru_b_r, lru_w_i, lru_b_i, lru_lambda, w_out_lru, w_out, norm2_w, w_ffn_in, w_ffn_out, norm_f_w, loss_target, m_norm1_w, m_w_in, m_b_branch_gate, m_ssm_conv_w, m_ssm_conv_b, m_ssm_dt_bias, m_ssm_a_log, m_ssm_d, m_ssm_norm_w, m_w_out_ssm, m_lru_conv_w, m_lru_conv_b, m_lru_w_r, m_lru_b_r, m_lru_w_i, m_lru_b_i, m_lru_lambda, m_w_out_lru, m_w_out, m_norm2_w, m_w_ffn_in, m_w_ffn_out, m_norm_f_w, v_norm1_w, v_w_in, v_b_branch_gate, v_ssm_conv_w, v_ssm_conv_b, v_ssm_dt_bias, v_ssm_a_log, v_ssm_d, v_ssm_norm_w, v_w_out_ssm, v_lru_conv_w, v_lru_conv_b, v_lru_w_r, v_lru_b_r, v_lru_w_i, v_lru_b_i, v_lru_lambda, v_w_out_lru, v_w_out, v_norm2_w, v_w_ffn_in, v_w_ffn_out, v_norm_f_w):
    given = dict(locals())
    weights = {n: given[n] for n in BIG_NAMES + SMALL_NAMES}
    t = x.shape[1]
    xt = x[0]
    tgt = loss_target[0]

    shards = [weights[n][0].astype(BF16) for n in BIG_NAMES[:6]] + [weights[n][0] for n in BIG_NAMES[6:]]
    full = {}
    for n, g in zip(BIG_NAMES, _all_gather(shards, "gather_weights")):
        full[n] = _from_col_shards(g) if n in COL_SHARDED else g.reshape(-1, g.shape[-1])
    ssm_cw, lru_cw = full["ssm_conv_w"], full["lru_conv_w"]
    wi_full = full["w_in"]
    w_p = jnp.concatenate([wi_full[:, :ORIG_DT], wi_full[:, ORIG_LRU:], wi_full[:, ORIG_DT:ORIG_LRU],
                           jnp.zeros((D_MODEL, PROJ_W - IN_PROJ), BF16)], axis=1)

    def pad_heads(a):
        return jnp.pad(a.reshape(1, SSM_HEADS), ((0, 0), (0, LANES - SSM_HEADS)))

    dt_bias_p = pad_heads(ssm_dt_bias)
    a_log_p = pad_heads(ssm_a_log)
    d_exp = jnp.repeat(ssm_d.reshape(SSM_HEADS), SSM_HEAD_DIM).reshape(1, SSM_INNER)
    lru_wr, lru_wi = lru_w_r[0], lru_w_i[0]

    hn1 = _rmsnorm_fwd(xt, norm1_w, "norm1_fwd")
    proj = _mm(hn1, w_p, name="in_proj")
    xbc_act = _ssm_conv_fwd(proj, ssm_cw, ssm_conv_b)
    y_ssd, s_in_all = _ssd_fwd(xbc_act, proj, dt_bias_p, a_log_p, d_exp)
    y_pre = _ssm_post_fwd(y_ssd, proj, ssm_norm_w)
    y_ssm = _mm(y_pre, full["w_out_ssm"], name="out_ssm")
    l_out = _lru_fwd(proj, lru_cw, lru_conv_b, lru_wr, lru_b_r, lru_wi, lru_b_i, lru_lambda)
    y_lru = _mm(l_out, full["w_out_lru"], name="out_lru")
    merged = _merge_fwd(proj, b_branch_gate, y_ssm, y_lru)
    h1 = _mm(merged, full["w_out"], add=xt, name="out_proj")
    hn2 = _rmsnorm_fwd(h1, norm2_w, "norm2_fwd")
    gu = _mm(hn2, full["w_ffn_in"], name="ffn_in")
    act = _swiglu_fwd(gu)
    h2 = _mm(act, full["w_ffn_out"], add=h1, name="ffn_out")

    grads = {}
    dh2, grads["norm_f_w"], loss_cols = _loss_head(h2, norm_f_w.reshape(1, D_MODEL), tgt)
    loss = lax.psum(0.5 * jnp.sum(loss_cols) / D_MODEL, AXES)
    dact = _mm(dh2, full["w_ffn_out"], tb=True, name="d_act")
    grads["w_ffn_out"] = _mm(act, dh2, ta=True, name="dw_ffn_out")
    dgu = _swiglu_bwd(gu, dact)
    dhn2 = _mm(dgu, full["w_ffn_in"], tb=True, name="d_hn2")
    grads["w_ffn_in"] = _mm(hn2, dgu, ta=True, name="dw_ffn_in")
    dh1, grads["norm2_w"] = _rmsnorm_bwd(h1, norm2_w, dhn2, dh2, "norm2_bwd")
    dmerged = _mm(dh1, full["w_out"], tb=True, name="d_merged")
    grads["w_out"] = _mm(merged, dh1, ta=True, name="dw_out")
    dy_ssm, dy_lru, dgates, grads["b_branch_gate"] = _merge_bwd(proj, b_branch_gate, y_ssm, y_lru, dmerged)
    dy_pre = _mm(dy_ssm, full["w_out_ssm"], tb=True, name="d_y_pre")
    grads["w_out_ssm"] = _mm(y_pre, dy_ssm, ta=True, name="dw_out_ssm")
    dl_out = _mm(dy_lru, full["w_out_lru"], tb=True, name="d_l_out")
    grads["w_out_lru"] = _mm(l_out, dy_lru, ta=True, name="dw_out_lru")
    dy_ssd, dz, grads["ssm_norm_w"] = _ssm_post_bwd(y_ssd, proj, ssm_norm_w, dy_pre)
    dxbc_act, ddt, dbias, dalog, ddcol = _ssd_bwd(xbc_act, proj, s_in_all, dy_ssd, dt_bias_p, a_log_p, d_exp)
    grads["ssm_dt_bias"] = dbias[:, :SSM_HEADS]
    grads["ssm_a_log"] = dalog[:, :SSM_HEADS]
    grads["ssm_d"] = ddcol.reshape(SSM_HEADS, SSM_HEAD_DIM).sum(axis=1).reshape(1, SSM_HEADS)
    dxbc, grads["ssm_conv_w"], grads["ssm_conv_b"] = _ssm_conv_bwd(proj, ssm_cw, ssm_conv_b, dxbc_act)
    (dxl, dyl, grads["lru_conv_w"], grads["lru_conv_b"], dwr, grads["lru_b_r"], dwi, grads["lru_b_i"],
     grads["lru_lambda"]) = _lru_bwd(proj, lru_cw, lru_conv_b, lru_wr, lru_b_r, lru_wi, lru_b_i, lru_lambda, dl_out)
    grads["lru_w_r"], grads["lru_w_i"] = dwr[None], dwi[None]
    dproj = jnp.concatenate([dgates, dz, dxbc, dxl, dyl, ddt], axis=1)
    dhn1 = _mm(dproj, w_p, tb=True, name="d_hn1")
    dwp = _mm(hn1, dproj, ta=True, name="dw_in")
    grads["w_in"] = jnp.concatenate([dwp[:, :ORIG_DT], dwp[:, COL_DT:COL_DT + SSM_HEADS], dwp[:, COL_LRU_X:COL_DT]],
                                    axis=1)
    grad_x, grads["norm1_w"] = _rmsnorm_bwd(xt, norm1_w, dhn1, dh1, "norm1_bwd")

    parts = [(_col_shards if n in COL_SHARDED else _row_shards)(grads[n]).astype(BF16) for n in BIG_NAMES]
    small_total = N_DEV * SMALL_ROWS * LANES
    parts.append(_pack([grads[n] for n in SMALL_NAMES], small_total).reshape(N_DEV, SMALL_ROWS, LANES))
    recv = _exchange(parts, "exchange_grads")
    small_g = _sum_sources(recv[-1], SMALL_ROWS, "sum_small_grads")
    small_g = _all_gather([small_g], "gather_small_grads")[0].reshape(1, small_total // FLAT_W, FLAT_W)

    big_out = {n: _adamw(weights[n][0], r, given["m_" + n][0], given["v_" + n][0], "adamw_" + n)
               for n, r in zip(BIG_NAMES, recv)}
    small_state = [_pack([given[p + n] for n in SMALL_NAMES], small_total).reshape(-1, FLAT_W) for p in ("", "m_", "v_")]
    small_out = _adamw(small_state[0], small_g, small_state[1], small_state[2], "adamw_replicated")

    small_shapes = [weights[n].shape for n in SMALL_NAMES]
    order = list(given)[1:24]
    results = []
    for q in range(4):
        vals = {n: big_out[n][q][None] for n in BIG_NAMES}
        vals.update(zip(SMALL_NAMES, _unpack(small_out[q].reshape(-1), small_shapes)))
        results.extend(vals[n] for n in order)
    return (loss, grad_x[None], *results)
```

```python
import math

import jax
import jax.numpy as jnp
from jax import lax
from jax.experimental import pallas as pl
from jax.experimental.pallas import tpu as pltpu

F32 = jnp.float32
BF16 = jnp.bfloat16
HIGHEST = lax.Precision.HIGHEST
MESH = pl.DeviceIdType.MESH
AXES = ("x", "y", "c")
N_DEV = 8

D_MODEL = 1024
SSM_INNER = 2048
SSM_HEADS = 32
SSM_HEAD_DIM = 64
SSM_GROUPS = 4
SSM_STATE = 128
SSM_BC = SSM_GROUPS * SSM_STATE
SSM_CONV_DIM = SSM_INNER + 2 * SSM_BC
SSM_CHUNK = 128
SSM_PAIRS = SSM_HEADS // 2
CONV_K = 4
LRU_WIDTH = 1280
LRU_BLOCKS = 10
LRU_C = 8.0
FFN_HIDDEN = 2816
RMS_EPS = 1e-6
IN_PROJ = 9760

COL_GATES = 0
COL_Z = 2048
COL_XBC = 4096
COL_LRU_X = 7168
COL_LRU_Y = 8448
COL_DT = 9728
PROJ_W = 9856
ORIG_DT = 7168
ORIG_LRU = 7200

ADAM_LR = 0.001
ADAM_B1 = 0.9
ADAM_B2 = 0.999
ADAM_EPS = 1e-08
ADAM_WD = 0.01
ADAM_STEP = 10

LANES = 128
SUBLANES = 8
V7X_VMEM_BYTES = 64 * 1024 * 1024
VMEM_LIMIT = V7X_VMEM_BYTES * 3 // 4
VMEM_LIMIT_BIG = V7X_VMEM_BYTES * 15 // 16

NT = (((1,), (1,)), ((), ()))
TN = (((0,), (0,)), ((), ()))

FLAT_W = 1024
SMALL_ROWS = 336


def _params(sem=None, big=False):
    return pltpu.CompilerParams(dimension_semantics=sem,
                                vmem_limit_bytes=VMEM_LIMIT_BIG if big else VMEM_LIMIT)


def _blk(dim, cap):
    if dim <= cap:
        return dim
    for m in range(cap // LANES, 0, -1):
        if dim % (m * LANES) == 0:
            return m * LANES
    raise ValueError(f"no block for {dim}")


def _rows(t):
    return min(t, 256)


def _sigmoid(v):
    return 1.0 / (1.0 + jnp.exp(-v))


def _softplus(v):
    e = jnp.exp(-jnp.abs(v))
    u = 1.0 + e
    log1p = jnp.where(u == 1.0, e, jnp.log(u) * e / jnp.where(u == 1.0, 1.0, u - 1.0))
    return jnp.maximum(v, 0.0) + log1p


def _iota(shape, dim):
    return lax.broadcasted_iota(jnp.int32, shape, dim)


def _shift_down(v, s):
    if s == 0:
        return v
    return jnp.where(_iota(v.shape, 0) >= s, pltpu.roll(v, s, 0), 0.0)


def _shift_up(v, s):
    if s == 0:
        return v
    n = v.shape[0]
    return jnp.where(_iota(v.shape, 0) < n - s, pltpu.roll(v, n - s, 0), 0.0)


def _bdot(a, b, dn=None):
    a = a.astype(BF16)
    b = b.astype(BF16)
    if dn is None:
        return jnp.dot(a, b, preferred_element_type=F32)
    return lax.dot_general(a, b, dn, preferred_element_type=F32)


def _fdot(a, b, dn=None):
    if dn is None:
        return jnp.dot(a, b, precision=HIGHEST, preferred_element_type=F32)
    return lax.dot_general(a, b, dn, precision=HIGHEST, preferred_element_type=F32)


def _mm(a, b, *, ta=False, tb=False, add=None, exchange=(), name):
    if ta:
        kdim, m = a.shape
    else:
        m, kdim = a.shape
    if tb:
        n, k2 = b.shape
    else:
        k2, n = b.shape
    assert kdim == k2, (a.shape, b.shape, ta, tb)
    bm, bn, bk = _blk(m, 1024), _blk(n, 1408), _blk(kdim, 1408)
    grid = (m // bm, n // bn, kdim // bk)
    nk = grid[2]
    dn = (((0 if ta else 1,), (1 if tb else 0,)), ((), ()))
    n_in = 2 if add is None else 3
    n_ex = len(exchange)

    def body(*refs):
        a_ref, b_ref = refs[:2]
        r_ref = None if add is None else refs[2]
        o_ref = refs[n_in + n_ex]
        acc = refs[n_in + 2 * n_ex + 1]
        comm = (refs[n_in:n_in + n_ex], refs[n_in + n_ex + 1:n_in + 2 * n_ex + 1]) + tuple(refs[n_in + 2 * n_ex + 2:])
        step = (pl.program_id(0) * grid[1] + pl.program_id(1)) * nk + pl.program_id(2)
        k = pl.program_id(2)
        if n_ex:
            @pl.when(step == 0)
            def _():
                _exchange_phase("start", *comm)

        @pl.when(k == 0)
        def _():
            acc[...] = jnp.zeros_like(acc)

        acc[...] += lax.dot_general(a_ref[...].astype(BF16), b_ref[...].astype(BF16), dn,
                                    preferred_element_type=F32)

        @pl.when(k == nk - 1)
        def _():
            r = acc[...]
            if add is not None:
                r = r + r_ref[...]
            o_ref[...] = r

        if n_ex:
            @pl.when(step == grid[0] * grid[1] * nk - 1)
            def _():
                _exchange_phase("finish", *comm)

    a_spec = pl.BlockSpec((bk, bm), lambda i, j, k: (k, i)) if ta else pl.BlockSpec((bm, bk), lambda i, j, k: (i, k))
    b_spec = pl.BlockSpec((bn, bk), lambda i, j, k: (j, k)) if tb else pl.BlockSpec((bk, bn), lambda i, j, k: (k, j))
    o_spec = pl.BlockSpec((bm, bn), lambda i, j, k: (i, j))
    in_specs = [a_spec, b_spec] + ([o_spec] if add is not None else []) + [HBM] * n_ex
    args = (a, b) + ((add,) if add is not None else ()) + tuple(exchange)
    out = pl.pallas_call(
        body, name=name, grid=grid,
        in_specs=in_specs, out_specs=[o_spec] + [HBM] * n_ex,
        out_shape=[jax.ShapeDtypeStruct((m, n), F32)] + [jax.ShapeDtypeStruct(p.shape, p.dtype) for p in exchange],
        scratch_shapes=[pltpu.VMEM((bm, bn), F32)] + (_comm_scratch(n_ex) if n_ex else []),
        compiler_params=_params(("arbitrary",) * 3 if n_ex else ("parallel", "parallel", "arbitrary")),
    )(*args)
    return (out[0], out[1:]) if n_ex else out[0]


def _rmsnorm_fwd(x, w, name):
    t, d = x.shape
    tr = _rows(t)

    def body(x_ref, w_ref, o_ref):
        xv = x_ref[...]
        rstd = lax.rsqrt(jnp.mean(xv * xv, axis=-1, keepdims=True) + RMS_EPS)
        o_ref[...] = (xv * rstd * w_ref[...]).astype(BF16)

    return pl.pallas_call(
        body, name=name, grid=(t // tr,),
        in_specs=[pl.BlockSpec((tr, d), lambda i: (i, 0)), pl.BlockSpec((1, d), lambda i: (0, 0))],
        out_specs=pl.BlockSpec((tr, d), lambda i: (i, 0)),
        out_shape=jax.ShapeDtypeStruct((t, d), BF16),
        compiler_params=_params(("parallel",)),
    )(x, w)


def _rmsnorm_bwd(x, w, dy, dres, name):
    t, d = x.shape
    tr = _rows(t)

    def body(x_ref, w_ref, dy_ref, dres_ref, dx_ref, dw_ref):
        i = pl.program_id(0)
        xv = x_ref[...]
        rstd = lax.rsqrt(jnp.mean(xv * xv, axis=-1, keepdims=True) + RMS_EPS)
        xhat = xv * rstd
        dyv = dy_ref[...]
        dxhat = dyv * w_ref[...]
        m = jnp.mean(dxhat * xhat, axis=-1, keepdims=True)
        dx_ref[...] = rstd * (dxhat - xhat * m) + dres_ref[...]
        part = jnp.sum(dyv * xhat, axis=0, keepdims=True)

        @pl.when(i == 0)
        def _():
            dw_ref[...] = part

        @pl.when(i > 0)
        def _():
            dw_ref[...] += part

    row = pl.BlockSpec((tr, d), lambda i: (i, 0))
    vec = pl.BlockSpec((1, d), lambda i: (0, 0))
    return pl.pallas_call(
        body, name=name, grid=(t // tr,),
        in_specs=[row, vec, row, row], out_specs=[row, vec],
        out_shape=[jax.ShapeDtypeStruct((t, d), F32), jax.ShapeDtypeStruct((1, d), F32)],
        compiler_params=_params(("arbitrary",)),
    )(x, w, dy, dres)


def _loss_head(h2, w, tgt):
    t, d = h2.shape
    tr = _rows(t)

    def body(x_ref, w_ref, t_ref, dx_ref, dw_ref, ls_ref):
        i = pl.program_id(0)
        xv = x_ref[...]
        wv = w_ref[...]
        rstd = lax.rsqrt(jnp.mean(xv * xv, axis=-1, keepdims=True) + RMS_EPS)
        xhat = xv * rstd
        err = xhat * wv - t_ref[...]
        dyv = err * (1.0 / d)
        dxhat = dyv * wv
        m = jnp.mean(dxhat * xhat, axis=-1, keepdims=True)
        dx_ref[...] = rstd * (dxhat - xhat * m)
        dw_part = jnp.sum(dyv * xhat, axis=0, keepdims=True)
        ls_part = jnp.sum(err * err, axis=0, keepdims=True)

        @pl.when(i == 0)
        def _():
            dw_ref[...] = dw_part
            ls_ref[...] = ls_part

        @pl.when(i > 0)
        def _():
            dw_ref[...] += dw_part
            ls_ref[...] += ls_part

    row = pl.BlockSpec((tr, d), lambda i: (i, 0))
    vec = pl.BlockSpec((1, d), lambda i: (0, 0))
    return pl.pallas_call(
        body, name="loss_head", grid=(t // tr,),
        in_specs=[row, vec, row], out_specs=[row, vec, vec],
        out_shape=[jax.ShapeDtypeStruct((t, d), F32), jax.ShapeDtypeStruct((1, d), F32),
                   jax.ShapeDtypeStruct((1, d), F32)],
        compiler_params=_params(("arbitrary",)),
    )(h2, w, tgt)


def _merge_fwd(proj, bg, ys, yl):
    t = proj.shape[0]
    d = D_MODEL
    tr = _rows(t)

    def body(ps_ref, pl_ref, bg_ref, ys_ref, yl_ref, o_ref):
        gs = _sigmoid(ps_ref[...] + bg_ref[:, 0:d])
        gl = _sigmoid(pl_ref[...] + bg_ref[:, d:2 * d])
        o_ref[...] = (gs * ys_ref[...] + gl * yl_ref[...]).astype(BF16)

    row = pl.BlockSpec((tr, d), lambda i: (i, 0))
    return pl.pallas_call(
        body, name="merge_fwd", grid=(t // tr,),
        in_specs=[row, pl.BlockSpec((tr, d), lambda i: (i, 1)), pl.BlockSpec((1, 2 * d), lambda i: (0, 0)), row, row],
        out_specs=row, out_shape=jax.ShapeDtypeStruct((t, d), BF16),
        compiler_params=_params(("parallel",)),
    )(proj, proj, bg, ys, yl)


def _merge_bwd(proj, bg, ys, yl, dm):
    t = proj.shape[0]
    d = D_MODEL
    tr = _rows(t)

    def body(ps_ref, pl_ref, bg_ref, ys_ref, yl_ref, dm_ref, dys_ref, dyl_ref, dg_ref, dbg_ref):
        i = pl.program_id(0)
        gs = _sigmoid(ps_ref[...] + bg_ref[:, 0:d])
        gl = _sigmoid(pl_ref[...] + bg_ref[:, d:2 * d])
        dmv = dm_ref[...]
        dys_ref[...] = (dmv * gs).astype(BF16)
        dyl_ref[...] = (dmv * gl).astype(BF16)
        dgs = dmv * ys_ref[...] * gs * (1.0 - gs)
        dgl = dmv * yl_ref[...] * gl * (1.0 - gl)
        dg_ref[:, 0:d] = dgs.astype(BF16)
        dg_ref[:, d:2 * d] = dgl.astype(BF16)

        @pl.when(i == 0)
        def _():
            dbg_ref[...] = jnp.zeros_like(dbg_ref)

        dbg_ref[:, 0:d] += jnp.sum(dgs, axis=0, keepdims=True)
        dbg_ref[:, d:2 * d] += jnp.sum(dgl, axis=0, keepdims=True)

    row = pl.BlockSpec((tr, d), lambda i: (i, 0))
    wide = pl.BlockSpec((tr, 2 * d), lambda i: (i, 0))
    vec = pl.BlockSpec((1, 2 * d), lambda i: (0, 0))
    return pl.pallas_call(
        body, name="merge_bwd", grid=(t // tr,),
        in_specs=[row, pl.BlockSpec((tr, d), lambda i: (i, 1)), vec, row, row, row],
        out_specs=[row, row, wide, vec],
        out_shape=[jax.ShapeDtypeStruct((t, d), BF16), jax.ShapeDtypeStruct((t, d), BF16),
                   jax.ShapeDtypeStruct((t, 2 * d), BF16), jax.ShapeDtypeStruct((1, 2 * d), F32)],
        compiler_params=_params(("arbitrary",)),
    )(proj, proj, bg, ys, yl, dm)


def _swiglu_fwd(gu):
    t = gu.shape[0]
    f = FFN_HIDDEN
    tr = _rows(t)

    def body(g_ref, u_ref, o_ref):
        g = g_ref[...]
        o_ref[...] = (g * _sigmoid(g) * u_ref[...]).astype(BF16)

    return pl.pallas_call(
        body, name="swiglu_fwd", grid=(t // tr,),
        in_specs=[pl.BlockSpec((tr, f), lambda i: (i, 0)), pl.BlockSpec((tr, f), lambda i: (i, 1))],
        out_specs=pl.BlockSpec((tr, f), lambda i: (i, 0)),
        out_shape=jax.ShapeDtypeStruct((t, f), BF16),
        compiler_params=_params(("parallel",)),
    )(gu, gu)


def _swiglu_bwd(gu, dact):
    t = gu.shape[0]
    f = FFN_HIDDEN
    tr = _rows(t)

    def body(g_ref, u_ref, da_ref, o_ref):
        g = g_ref[...]
        sg = _sigmoid(g)
        da = da_ref[...]
        o_ref[:, 0:f] = (da * u_ref[...] * (sg * (1.0 + g * (1.0 - sg)))).astype(BF16)
        o_ref[:, f:2 * f] = (da * g * sg).astype(BF16)

    return pl.pallas_call(
        body, name="swiglu_bwd", grid=(t // tr,),
        in_specs=[pl.BlockSpec((tr, f), lambda i: (i, 0)), pl.BlockSpec((tr, f), lambda i: (i, 1)),
                  pl.BlockSpec((tr, f), lambda i: (i, 0))],
        out_specs=pl.BlockSpec((tr, 2 * f), lambda i: (i, 0)),
        out_shape=jax.ShapeDtypeStruct((t, 2 * f), BF16),
        compiler_params=_params(("parallel",)),
    )(gu, gu, dact)


def _conv_pre(xv, wv, bv):
    pre = bv + wv[CONV_K - 1:CONV_K, :] * xv
    for k in range(CONV_K - 1):
        pre = pre + wv[k:k + 1, :] * _shift_down(xv, CONV_K - 1 - k)
    return pre


def _ssm_conv_fwd(proj, w, b):
    t = proj.shape[0]
    nb = SSM_CONV_DIM // LANES
    c0 = COL_XBC // LANES

    def body(x_ref, w_ref, b_ref, o_ref):
        pre = _conv_pre(x_ref[...], w_ref[...], b_ref[...])
        o_ref[...] = pre * _sigmoid(pre)

    return pl.pallas_call(
        body, name="ssm_conv_fwd", grid=(nb,),
        in_specs=[pl.BlockSpec((t, LANES), lambda j: (0, c0 + j)), pl.BlockSpec((CONV_K, LANES), lambda j: (0, j)),
                  pl.BlockSpec((1, LANES), lambda j: (0, j))],
        out_specs=pl.BlockSpec((t, LANES), lambda j: (0, j)),
        out_shape=jax.ShapeDtypeStruct((t, SSM_CONV_DIM), F32),
        compiler_params=_params(("parallel",)),
    )(proj, w, b)


def _ssm_conv_bwd(proj, w, b, dact):
    t = proj.shape[0]
    nb = SSM_CONV_DIM // LANES
    c0 = COL_XBC // LANES

    def body(x_ref, w_ref, b_ref, da_ref, dx_ref, dw_ref, db_ref):
        xv = x_ref[...]
        wv = w_ref[...]
        pre = _conv_pre(xv, wv, b_ref[...])
        sg = _sigmoid(pre)
        dpre = da_ref[...] * (sg * (1.0 + pre * (1.0 - sg)))
        dx = wv[CONV_K - 1:CONV_K, :] * dpre
        for k in range(CONV_K - 1):
            dx = dx + wv[k:k + 1, :] * _shift_up(dpre, CONV_K - 1 - k)
        dx_ref[...] = dx.astype(BF16)
        for k in range(CONV_K):
            dw_ref[k:k + 1, :] = jnp.sum(dpre * _shift_down(xv, CONV_K - 1 - k), axis=0, keepdims=True)
        db_ref[...] = jnp.sum(dpre, axis=0, keepdims=True)

    col = pl.BlockSpec((t, LANES), lambda j: (0, j))
    wsp = pl.BlockSpec((CONV_K, LANES), lambda j: (0, j))
    bsp = pl.BlockSpec((1, LANES), lambda j: (0, j))
    return pl.pallas_call(
        body, name="ssm_conv_bwd", grid=(nb,),
        in_specs=[pl.BlockSpec((t, LANES), lambda j: (0, c0 + j)), wsp, bsp, col],
        out_specs=[col, wsp, bsp],
        out_shape=[jax.ShapeDtypeStruct((t, SSM_CONV_DIM), BF16), jax.ShapeDtypeStruct((CONV_K, SSM_CONV_DIM), F32),
                   jax.ShapeDtypeStruct((1, SSM_CONV_DIM), F32)],
        compiler_params=_params(("parallel",)),
    )(proj, w, b, dact)


def _ssd_chunk_terms(dtr, bias, alog):
    a = -jnp.exp(alog)
    dt = _softplus(dtr + bias)
    row = _iota((SSM_CHUNK, SSM_CHUNK), 0)
    col = _iota((SSM_CHUNK, SSM_CHUNK), 1)
    tri = (row >= col).astype(F32)
    cs = _fdot(tri, dt * a)
    dec = jnp.exp(cs[SSM_CHUNK - 1:SSM_CHUNK, :] - cs)
    ecs = jnp.exp(cs)
    off = _iota((LANES, SSM_INNER), 1) - SSM_HEAD_DIM * _iota((LANES, SSM_INNER), 0)
    expand = jnp.logical_and(off >= 0, off < SSM_HEAD_DIM).astype(F32)
    return a, dt, cs, dec, ecs, expand, row, col


def _ssd_specs(t):
    nc = t // SSM_CHUNK
    xs = pl.BlockSpec((SSM_CHUNK, SSM_INNER), lambda c: (c, 0))
    bm = pl.BlockSpec((SSM_CHUNK, SSM_BC), lambda c: (c, SSM_INNER // SSM_BC))
    cm = pl.BlockSpec((SSM_CHUNK, SSM_BC), lambda c: (c, SSM_INNER // SSM_BC + 1))
    dtr = pl.BlockSpec((SSM_CHUNK, LANES), lambda c: (c, COL_DT // LANES))
    vec = pl.BlockSpec((1, LANES), lambda c: (0, 0))
    wide = pl.BlockSpec((1, SSM_INNER), lambda c: (0, 0))
    return nc, xs, bm, cm, dtr, vec, wide


def _ssd_fwd(xbc_act, proj, bias, alog, dexp):
    t = proj.shape[0]
    nc, xs_s, bm_s, cm_s, dtr_s, vec, wide = _ssd_specs(t)

    def body(xs_ref, b_ref, c_ref, dtr_ref, bias_ref, alog_ref, dexp_ref, y_ref, sin_ref, state):
        @pl.when(pl.program_id(0) == 0)
        def _():
            state[...] = jnp.zeros_like(state)

        a, dt, cs, dec, ecs, expand, row, col = _ssd_chunk_terms(dtr_ref[...], bias_ref[...], alog_ref[...])
        cst = cs.T
        dt_x = _fdot(dt, expand)
        dec_x = _fdot(dec, expand)
        ecs_x = _fdot(ecs, expand)
        xs = xs_ref[...]
        xdt = xs * dt_x
        xdec = xdt * dec_x
        lane_lo = col < SSM_HEAD_DIM
        causal = row >= col
        sin_ref[0] = state[...]
        for g in range(SSM_GROUPS):
            bg = b_ref[:, g * SSM_STATE:(g + 1) * SSM_STATE].astype(BF16)
            cg = c_ref[:, g * SSM_STATE:(g + 1) * SSM_STATE].astype(BF16)
            cb = _bdot(cg, bg, NT)
            for q in range(SSM_PAIRS // SSM_GROUPS):
                pq = g * (SSM_PAIRS // SSM_GROUPS) + q
                sl = slice(pq * LANES, (pq + 1) * LANES)
                xp = xdt[:, sl].astype(BF16)
                yd = []
                for hh in range(2):
                    h = 2 * pq + hh
                    lmat = jnp.exp(jnp.where(causal, cs[:, h:h + 1] - cst[h:h + 1, :], -jnp.inf))
                    yd.append(_bdot(cb * lmat, xp))
                s_in = state[pq]
                y_off = _bdot(cg, s_in) * ecs_x[:, sl]
                y_ref[:, sl] = jnp.where(lane_lo, yd[0], yd[1]) + y_off + xs[:, sl] * dexp_ref[:, sl]
                state[pq] = s_in * ecs_x[SSM_CHUNK - 1:SSM_CHUNK, sl] + _bdot(bg, xdec[:, sl], TN)

    return pl.pallas_call(
        body, name="ssd_fwd", grid=(nc,),
        in_specs=[xs_s, bm_s, cm_s, dtr_s, vec, vec, wide],
        out_specs=[pl.BlockSpec((SSM_CHUNK, SSM_INNER), lambda c: (c, 0)),
                   pl.BlockSpec((1, SSM_PAIRS, SSM_STATE, LANES), lambda c: (c, 0, 0, 0))],
        out_shape=[jax.ShapeDtypeStruct((t, SSM_INNER), F32),
                   jax.ShapeDtypeStruct((nc, SSM_PAIRS, SSM_STATE, LANES), F32)],
        scratch_shapes=[pltpu.VMEM((SSM_PAIRS, SSM_STATE, LANES), F32)],
        compiler_params=_params(("arbitrary",)),
    )(xbc_act, xbc_act, xbc_act, proj, bias, alog, dexp)


def _ssd_bwd(xbc_act, proj, s_in_all, dy, bias, alog, dexp, exchange):
    n_ex = len(exchange)
    t = proj.shape[0]
    nc = t // SSM_CHUNK
    last = nc - 1
    xs_s = pl.BlockSpec((SSM_CHUNK, SSM_INNER), lambda c: (last - c, 0))
    bm_s = pl.BlockSpec((SSM_CHUNK, SSM_BC), lambda c: (last - c, SSM_INNER // SSM_BC))
    cm_s = pl.BlockSpec((SSM_CHUNK, SSM_BC), lambda c: (last - c, SSM_INNER // SSM_BC + 1))
    dtr_s = pl.BlockSpec((SSM_CHUNK, LANES), lambda c: (last - c, COL_DT // LANES))
    sin_s = pl.BlockSpec((1, SSM_PAIRS, SSM_STATE, LANES), lambda c: (last - c, 0, 0, 0))
    vec = pl.BlockSpec((1, LANES), lambda c: (0, 0))
    wide = pl.BlockSpec((1, SSM_INNER), lambda c: (0, 0))

    def body(*refs):
        xs_ref, b_ref, c_ref, dtr_ref, sin_ref, dy_ref, bias_ref, alog_ref, dexp_ref = refs[:9]
        dxbc_ref, ddtr_ref, dbias_ref, dalog_ref, ddcol_ref = refs[9 + n_ex:14 + n_ex]
        dstate, dxdt_s, yoff_s, rx_s, trow_s = refs[14 + 2 * n_ex:19 + 2 * n_ex]
        comm = (refs[9:9 + n_ex], refs[14 + n_ex:14 + 2 * n_ex]) + tuple(refs[19 + 2 * n_ex:])

        @pl.when(pl.program_id(0) == 0)
        def _():
            _exchange_phase("start", *comm)
            dstate[...] = jnp.zeros_like(dstate)
            trow_s[...] = jnp.zeros_like(trow_s)
            dbias_ref[...] = jnp.zeros_like(dbias_ref)
            dalog_ref[...] = jnp.zeros_like(dalog_ref)
            ddcol_ref[...] = jnp.zeros_like(ddcol_ref)

        dtr = dtr_ref[...]
        a, dt, cs, dec, ecs, expand, row, col = _ssd_chunk_terms(dtr, bias_ref[...], alog_ref[...])
        cst = cs.T
        dt_x = _fdot(dt, expand)
        dec_x = _fdot(dec, expand)
        ecs_x = _fdot(ecs, expand)
        xs = xs_ref[...]
        dyv = dy_ref[...]
        xdt = xs * dt_x
        lane_lo = col < SSM_HEAD_DIM
        causal = row >= col
        ddcol_ref[...] += jnp.sum(dyv * xs, axis=0, keepdims=True)
        dcs_col = jnp.zeros((SSM_CHUNK, LANES), F32)
        dcs_row = jnp.zeros((LANES, SSM_CHUNK), F32)
        for g in range(SSM_GROUPS):
            bg = b_ref[:, g * SSM_STATE:(g + 1) * SSM_STATE].astype(BF16)
            cg = c_ref[:, g * SSM_STATE:(g + 1) * SSM_STATE].astype(BF16)
            cb = _bdot(cg, bg, NT)
            dgm = jnp.zeros((SSM_CHUNK, SSM_CHUNK), F32)
            dbg = jnp.zeros((SSM_CHUNK, SSM_STATE), F32)
            dcg = jnp.zeros((SSM_CHUNK, SSM_STATE), F32)
            for q in range(SSM_PAIRS // SSM_GROUPS):
                pq = g * (SSM_PAIRS // SSM_GROUPS) + q
                sl = slice(pq * LANES, (pq + 1) * LANES)
                dyp = dyv[:, sl]
                xp = xdt[:, sl]
                dxh = []
                for hh in range(2):
                    h = 2 * pq + hh
                    lmat = jnp.exp(jnp.where(causal, cs[:, h:h + 1] - cst[h:h + 1, :], -jnp.inf))
                    mmat = cb * lmat
                    dyh = jnp.where(lane_lo if hh == 0 else jnp.logical_not(lane_lo), dyp, 0.0)
                    dmm = _bdot(dyh, xp, NT)
                    pm = dmm * mmat
                    dcs_col = jnp.where(col == h, jnp.sum(pm, axis=1, keepdims=True), dcs_col)
                    dcs_row = jnp.where(row == h, jnp.sum(pm, axis=0, keepdims=True), dcs_row)
                    dgm = dgm + dmm * lmat
                    dxh.append(_bdot(mmat, dyp, TN))
                s_in = sin_ref[0, pq]
                ecs_p = ecs_x[:, sl]
                dec_p = dec_x[:, sl]
                etot_p = ecs_x[SSM_CHUNK - 1:SSM_CHUNK, sl]
                yoff_s[:, sl] = dyp * (_bdot(cg, s_in) * ecs_p)
                dq = dyp * ecs_p
                dcg = dcg + _bdot(dq, s_in, NT)
                ds = dstate[pq]
                r = _bdot(bg, ds)
                rx_s[:, sl] = r * xp
                dxdt_s[:, sl] = jnp.where(lane_lo, dxh[0], dxh[1]) + dec_p * r
                dbg = dbg + _bdot(xp * dec_p, ds, NT)
                trow_s[0:1, sl] = jnp.sum(ds * s_in, axis=0, keepdims=True) * etot_p
                dstate[pq] = etot_p * ds + _bdot(cg, dq, TN)
            dcg = dcg + _bdot(dgm, bg)
            dbg = dbg + _bdot(dgm, cg, TN)
            dxbc_ref[:, SSM_INNER + g * SSM_STATE:SSM_INNER + (g + 1) * SSM_STATE] = dbg
            dxbc_ref[:, SSM_INNER + SSM_BC + g * SSM_STATE:SSM_INNER + SSM_BC + (g + 1) * SSM_STATE] = dcg
        ddec = _fdot(rx_s[...], expand, NT) * dec
        dtot = _fdot(trow_s[...], expand, NT)[0:1, :]
        dcs = dcs_col - dcs_row.T + _fdot(yoff_s[...], expand, NT) - ddec
        dcs = dcs + jnp.where(row == SSM_CHUNK - 1, jnp.sum(ddec, axis=0, keepdims=True) + dtot, 0.0)
        da = _fdot((row <= col).astype(F32), dcs)
        dxdt = dxdt_s[...]
        ddt = da * a + _fdot(dxdt * xs, expand, NT)
        dalog_ref[...] += jnp.sum(da * dt, axis=0, keepdims=True) * a
        ddtr = ddt * _sigmoid(dtr + bias_ref[...])
        ddtr_ref[...] = ddtr.astype(BF16)
        dbias_ref[...] += jnp.sum(ddtr, axis=0, keepdims=True)
        dxbc_ref[:, 0:SSM_INNER] = dxdt * dt_x + dyv * dexp_ref[...]

        @pl.when(pl.program_id(0) == last)
        def _():
            _exchange_phase("finish", *comm)

    out = pl.pallas_call(
        body, name="ssd_bwd", grid=(nc,),
        in_specs=[xs_s, bm_s, cm_s, dtr_s, sin_s, pl.BlockSpec((SSM_CHUNK, SSM_INNER), lambda c: (last - c, 0)),
                  vec, vec, wide] + [HBM] * n_ex,
        out_specs=[pl.BlockSpec((SSM_CHUNK, SSM_CONV_DIM), lambda c: (last - c, 0)),
                   pl.BlockSpec((SSM_CHUNK, LANES), lambda c: (last - c, 0)), vec, vec, wide] + [HBM] * n_ex,
        out_shape=[jax.ShapeDtypeStruct((t, SSM_CONV_DIM), F32), jax.ShapeDtypeStruct((t, LANES), BF16),
                   jax.ShapeDtypeStruct((1, LANES), F32), jax.ShapeDtypeStruct((1, LANES), F32),
                   jax.ShapeDtypeStruct((1, SSM_INNER), F32)]
        + [jax.ShapeDtypeStruct(p.shape, p.dtype) for p in exchange],
        scratch_shapes=[pltpu.VMEM((SSM_PAIRS, SSM_STATE, LANES), F32),
                        pltpu.VMEM((SSM_CHUNK, SSM_INNER), F32), pltpu.VMEM((SSM_CHUNK, SSM_INNER), F32),
                        pltpu.VMEM((SSM_CHUNK, SSM_INNER), F32), pltpu.VMEM((SUBLANES, SSM_INNER), F32)]
        + _comm_scratch(n_ex),
        compiler_params=_params(("arbitrary",)),
    )(xbc_act, xbc_act, xbc_act, proj, s_in_all, dy, bias, alog, dexp, *exchange)
    return out[:5], out[5:]


def _group_rstd(y):
    n = SSM_INNER // SSM_GROUPS
    parts = []
    for g in range(SSM_GROUPS):
        yg = y[:, g * n:(g + 1) * n]
        r = lax.rsqrt(jnp.mean(yg * yg, axis=-1, keepdims=True) + RMS_EPS)
        parts.append(jnp.broadcast_to(r, yg.shape))
    return jnp.concatenate(parts, axis=1)


def _group_mean(v):
    n = SSM_INNER // SSM_GROUPS
    parts = []
    for g in range(SSM_GROUPS):
        vg = v[:, g * n:(g + 1) * n]
        parts.append(jnp.broadcast_to(jnp.mean(vg, axis=-1, keepdims=True), vg.shape))
    return jnp.concatenate(parts, axis=1)


def _ssm_post_fwd(y_ssd, proj, nw):
    t = proj.shape[0]
    n = SSM_INNER
    tr = _rows(t)

    def body(y_ref, z_ref, nw_ref, o_ref):
        z = z_ref[...]
        y = y_ref[...] * (z * _sigmoid(z))
        o_ref[...] = (y * _group_rstd(y) * nw_ref[...]).astype(BF16)

    row = pl.BlockSpec((tr, n), lambda i: (i, 0))
    return pl.pallas_call(
        body, name="ssm_post_fwd", grid=(t // tr,),
        in_specs=[row, pl.BlockSpec((tr, n), lambda i: (i, COL_Z // n)), pl.BlockSpec((1, n), lambda i: (0, 0))],
        out_specs=row, out_shape=jax.ShapeDtypeStruct((t, n), BF16),
        compiler_params=_params(("parallel",)),
    )(y_ssd, proj, nw)


def _ssm_post_bwd(y_ssd, proj, nw, dout):
    t = proj.shape[0]
    n = SSM_INNER
    tr = _rows(t)

    def body(y_ref, z_ref, nw_ref, do_ref, dy_ref, dz_ref, dnw_ref):
        i = pl.program_id(0)
        z = z_ref[...]
        sg = _sigmoid(z)
        sz = z * sg
        ys = y_ref[...]
        y = ys * sz
        rstd = _group_rstd(y)
        yn = y * rstd
        dov = do_ref[...]
        dyn = dov * nw_ref[...]
        dyg = rstd * (dyn - yn * _group_mean(dyn * yn))
        dy_ref[...] = dyg * sz
        dz_ref[...] = (dyg * ys * (sg * (1.0 + z * (1.0 - sg)))).astype(BF16)
        part = jnp.sum(dov * yn, axis=0, keepdims=True)

        @pl.when(i == 0)
        def _():
            dnw_ref[...] = part

        @pl.when(i > 0)
        def _():
            dnw_ref[...] += part

    row = pl.BlockSpec((tr, n), lambda i: (i, 0))
    vec = pl.BlockSpec((1, n), lambda i: (0, 0))
    return pl.pallas_call(
        body, name="ssm_post_bwd", grid=(t // tr,),
        in_specs=[row, pl.BlockSpec((tr, n), lambda i: (i, COL_Z // n)), vec, row],
        out_specs=[row, row, vec],
        out_shape=[jax.ShapeDtypeStruct((t, n), F32), jax.ShapeDtypeStruct((t, n), BF16),
                   jax.ShapeDtypeStruct((1, n), F32)],
        compiler_params=_params(("arbitrary",)),
    )(y_ssd, proj, nw, dout)


GELU_C = math.sqrt(2.0 / math.pi)
GELU_K = 0.044715


def _gelu_parts(y):
    th = jnp.tanh(GELU_C * (y + GELU_K * y * y * y))
    val = 0.5 * y * (1.0 + th)
    grad = 0.5 * (1.0 + th) + 0.5 * y * (1.0 - th * th) * GELU_C * (1.0 + 3.0 * GELU_K * y * y)
    return val, grad


def _scan_tiles(a_ref, b_ref, h_ref, n_rows, reverse):
    n_tiles = n_rows // SUBLANES
    shape = (SUBLANES, a_ref.shape[1])
    row = _iota(shape, 0)

    def step(k, carry):
        tile = (n_tiles - 1 - k) if reverse else k
        at = pl.ds(pl.multiple_of(tile * SUBLANES, SUBLANES), SUBLANES)
        av = a_ref[at, :]
        bv = b_ref[at, :]
        for s in (1, 2, 4):
            if reverse:
                keep = row < SUBLANES - s
                a_sh = jnp.where(keep, pltpu.roll(av, SUBLANES - s, 0), 1.0)
                b_sh = jnp.where(keep, pltpu.roll(bv, SUBLANES - s, 0), 0.0)
            else:
                keep = row >= s
                a_sh = jnp.where(keep, pltpu.roll(av, s, 0), 1.0)
                b_sh = jnp.where(keep, pltpu.roll(bv, s, 0), 0.0)
            bv = av * b_sh + bv
            av = av * a_sh
        hv = bv + av * carry
        h_ref[at, :] = hv
        return hv[0:1, :] if reverse else hv[SUBLANES - 1:SUBLANES, :]

    lax.fori_loop(0, n_tiles, step, jnp.zeros((1, a_ref.shape[1]), F32))


def _lru_gates(xl, cw, cb, wr, br, wi, bi, lam):
    u = cb + cw[CONV_K - 1:CONV_K, :] * xl
    for k in range(CONV_K - 1):
        u = u + cw[k:k + 1, :] * _shift_down(xl, CONV_K - 1 - k)
    r = _sigmoid(_bdot(u, wr) + br)
    i = _sigmoid(_bdot(u, wi) + bi)
    sp = _softplus(-lam)
    la = -LRU_C * r * sp
    a = jnp.exp(la)
    mult = jnp.sqrt(-jnp.tanh(la) * (a * a + 1.0))
    return u, r, i, sp, a, mult


def _lru_specs(t):
    c_x = COL_LRU_X // LANES
    c_y = COL_LRU_Y // LANES
    xl = pl.BlockSpec((t, LANES), lambda j: (0, c_x + j))
    yl = pl.BlockSpec((t, LANES), lambda j: (0, c_y + j))
    col = pl.BlockSpec((t, LANES), lambda j: (0, j))
    cw = pl.BlockSpec((CONV_K, LANES), lambda j: (0, j))
    vec = pl.BlockSpec((1, LANES), lambda j: (0, j))
    wblk = pl.BlockSpec((1, LANES, LANES), lambda j: (j, 0, 0))
    return xl, yl, col, cw, vec, wblk


def _lru_fwd(proj, cw, cb, wr, br, wi, bi, lam, gather):
    t = proj.shape[0]
    xl_s, yl_s, col, cw_s, vec, wblk = _lru_specs(t)
    n = len(gather)

    def body(*refs):
        xl_ref, yl_ref, cw_ref, cb_ref, wr_ref, br_ref, wi_ref, bi_ref, lam_ref = refs[:9]
        o_ref = refs[9 + n]
        a_s, b_s, h_s = refs[10 + 2 * n:13 + 2 * n]
        comm = (refs[9:9 + n], refs[10 + n:10 + 2 * n]) + tuple(refs[13 + 2 * n:])
        j = pl.program_id(0)
        for step, phase in ((0, "start"), (LRU_BLOCKS // 2, "forward")):
            @pl.when(j == step)
            def _():
                _gather_phase(phase, *comm)

        u, r, i, sp, a, mult = _lru_gates(xl_ref[...], cw_ref[...], cb_ref[...], wr_ref[0], br_ref[...],
                                          wi_ref[0], bi_ref[...], lam_ref[...])
        a_s[...] = a
        b_s[...] = mult * (i * u)
        _scan_tiles(a_s, b_s, h_s, t, reverse=False)
        o_ref[...] = (h_s[...] * _gelu_parts(yl_ref[...])[0]).astype(BF16)

        @pl.when(j == LRU_BLOCKS - 1)
        def _():
            _gather_phase("finish", *comm)

    out = pl.pallas_call(
        body, name="lru_fwd", grid=(LRU_BLOCKS,),
        in_specs=[xl_s, yl_s, cw_s, vec, wblk, vec, wblk, vec, vec] + [HBM] * n,
        out_specs=[col] + [HBM] * n,
        out_shape=[jax.ShapeDtypeStruct((t, LRU_WIDTH), BF16)]
        + [jax.ShapeDtypeStruct((N_DEV,) + v.shape, v.dtype) for v in gather],
        scratch_shapes=[pltpu.VMEM((t, LANES), F32)] * 3 + _comm_scratch(n),
        compiler_params=_params(("arbitrary",), big=True),
    )(proj, proj, cw, cb, wr, br, wi, bi, lam, *gather)
    return out[0], out[1:]


def _lru_bwd(proj, cw, cb, wr, br, wi, bi, lam, dout):
    t = proj.shape[0]
    xl_s, yl_s, col, cw_s, vec, wblk = _lru_specs(t)

    def body(xl_ref, yl_ref, cw_ref, cb_ref, wr_ref, br_ref, wi_ref, bi_ref, lam_ref, do_ref,
             dxl_ref, dyl_ref, dcw_ref, dcb_ref, dwr_ref, dbr_ref, dwi_ref, dbi_ref, dlam_ref,
             a_s, b_s, h_s, g_s):
        xl = xl_ref[...]
        cwv = cw_ref[...]
        lam = lam_ref[...]
        u, r, i, sp, a, mult = _lru_gates(xl, cwv, cb_ref[...], wr_ref[0], br_ref[...], wi_ref[0], bi_ref[...], lam)
        v = i * u
        a_s[...] = a
        b_s[...] = mult * v
        _scan_tiles(a_s, b_s, h_s, t, reverse=False)
        gl, dgl = _gelu_parts(yl_ref[...])
        dov = do_ref[...]
        h = h_s[...]
        dyl_ref[...] = (dov * h * dgl).astype(BF16)
        b_s[...] = dov * gl
        a_s[...] = _shift_up(a, 1)
        _scan_tiles(a_s, b_s, g_s, t, reverse=True)
        g = g_s[...]
        da = g * _shift_down(h, 1)
        dmult = g * v
        dv = g * mult
        dla = da * a - dmult * (a * a) / mult
        dr = dla * (-LRU_C * sp)
        dsp = jnp.sum(dla * (-LRU_C * r), axis=0, keepdims=True)
        dlam_ref[...] = -dsp * _sigmoid(-lam)
        dpr = dr * r * (1.0 - r)
        dpi = dv * u * i * (1.0 - i)
        dbr_ref[...] = jnp.sum(dpr, axis=0, keepdims=True)
        dbi_ref[...] = jnp.sum(dpi, axis=0, keepdims=True)
        dwr_ref[0] = _bdot(u, dpr, TN)
        dwi_ref[0] = _bdot(u, dpi, TN)
        du = dv * i + _bdot(dpr, wr_ref[0], NT) + _bdot(dpi, wi_ref[0], NT)
        dxl = cwv[CONV_K - 1:CONV_K, :] * du
        for k in range(CONV_K - 1):
            dxl = dxl + cwv[k:k + 1, :] * _shift_up(du, CONV_K - 1 - k)
        dxl_ref[...] = dxl.astype(BF16)
        for k in range(CONV_K):
            dcw_ref[k:k + 1, :] = jnp.sum(du * _shift_down(xl, CONV_K - 1 - k), axis=0, keepdims=True)
        dcb_ref[...] = jnp.sum(du, axis=0, keepdims=True)

    return pl.pallas_call(
        body, name="lru_bwd", grid=(LRU_BLOCKS,),
        in_specs=[xl_s, yl_s, cw_s, vec, wblk, vec, wblk, vec, vec, col],
        out_specs=[col, col, cw_s, vec, wblk, vec, wblk, vec, vec],
        out_shape=[jax.ShapeDtypeStruct((t, LRU_WIDTH), BF16), jax.ShapeDtypeStruct((t, LRU_WIDTH), BF16),
                   jax.ShapeDtypeStruct((CONV_K, LRU_WIDTH), F32), jax.ShapeDtypeStruct((1, LRU_WIDTH), F32),
                   jax.ShapeDtypeStruct((LRU_BLOCKS, LANES, LANES), F32), jax.ShapeDtypeStruct((1, LRU_WIDTH), F32),
                   jax.ShapeDtypeStruct((LRU_BLOCKS, LANES, LANES), F32), jax.ShapeDtypeStruct((1, LRU_WIDTH), F32),
                   jax.ShapeDtypeStruct((1, LRU_WIDTH), F32)],
        scratch_shapes=[pltpu.VMEM((t, LANES), F32)] * 4,
        compiler_params=_params(("arbitrary",), big=True),
    )(proj, proj, cw, cb, wr, br, wi, bi, lam, dout)


def _mesh_pos():
    return lax.axis_index("x"), lax.axis_index("y"), lax.axis_index("c")


HBM = pl.BlockSpec(memory_space=pl.ANY)


def _comm_scratch(n):
    return [pltpu.SemaphoreType.DMA((n, 7)), pltpu.SemaphoreType.DMA((n, 7)), pltpu.SemaphoreType.DMA((n,))]


def _gather_phase(phase, v_refs, out_refs, send_sems, recv_sems, local_sems):
    n = len(v_refs)
    x, y, c = _mesh_pos()
    me, sibling = (x, y, c), (x, y, 1 - c)
    chips = [(1 - x, y), (x, 1 - y), (1 - x, 1 - y)]

    def block(a, px, py, pc):
        return out_refs[a].at[4 * px + 2 * py + pc]

    def copy(a, k, blk, to, src=None):
        return pltpu.make_async_remote_copy(
            src_ref=block(a, *blk) if src is None else src, dst_ref=block(a, *blk),
            send_sem=send_sems.at[a, k], recv_sem=recv_sems.at[a, k], device_id=to, device_id_type=MESH)

    def own(a):
        return pltpu.make_async_copy(v_refs[a], block(a, *me), local_sems.at[a])

    def first(a):
        return ([copy(a, 0, me, sibling, src=v_refs[a])]
                + [copy(a, 1 + j, me, (*chip, c), src=v_refs[a]) for j, chip in enumerate(chips)])

    def forward(a, j):
        return copy(a, 4 + j, (*chips[j], c), sibling)

    if phase == "start":
        for a in range(n):
            own(a).start()
        for a in range(n):
            for cp in first(a):
                cp.start()
    elif phase == "forward":
        for j in range(3):
            for a in range(n):
                copy(a, 1 + j, (*chips[j], c), me).wait_recv()
                forward(a, j).start()
    else:
        for a in range(n):
            copy(a, 0, sibling, me).wait_recv()
            for j in range(3):
                copy(a, 4 + j, (*chips[j], 1 - c), me).wait_recv()
        for a in range(n):
            for cp in first(a) + [forward(a, j) for j in range(3)]:
                cp.wait_send()
            own(a).wait()


def _all_gather(vs, name):
    n = len(vs)

    def body(*refs):
        comm = (refs[:n], refs[n:2 * n]) + tuple(refs[2 * n:])
        for phase in ("start", "forward", "finish"):
            _gather_phase(phase, *comm)

    return pl.pallas_call(
        body, name=name,
        out_shape=[jax.ShapeDtypeStruct((N_DEV,) + v.shape, v.dtype) for v in vs],
        in_specs=[HBM] * n, out_specs=[HBM] * n, scratch_shapes=_comm_scratch(n),
    )(*vs)


def _exchange(parts, name):
    n = len(parts)

    def body(*refs):
        comm = (refs[:n], refs[n:2 * n]) + tuple(refs[2 * n:])
        _exchange_phase("start", *comm)
        _exchange_phase("finish", *comm)

    return pl.pallas_call(
        body, name=name,
        out_shape=[jax.ShapeDtypeStruct(p.shape, p.dtype) for p in parts],
        in_specs=[HBM] * n, out_specs=[HBM] * n, scratch_shapes=_comm_scratch(n),
    )(*parts)


def _exchange_phase(phase, p_refs, out_refs, send_sems, recv_sems, local_sems):
    n = len(p_refs)
    x, y, c = _mesh_pos()
    me = 4 * x + 2 * y + c
    local = [pltpu.make_async_copy(p_refs[a].at[me], out_refs[a].at[me], local_sems.at[a]) for a in range(n)]
    remote = []
    for k in range(1, N_DEV):
        px = (1 - x) if k & 4 else x
        py = (1 - y) if k & 2 else y
        pc = (1 - c) if k & 1 else c
        for a in range(n):
            remote.append(pltpu.make_async_remote_copy(
                src_ref=p_refs[a].at[4 * px + 2 * py + pc], dst_ref=out_refs[a].at[me],
                send_sem=send_sems.at[a, k - 1], recv_sem=recv_sems.at[a, k - 1],
                device_id=(px, py, pc), device_id_type=MESH))
    if phase == "start":
        for cp in local + remote:
            cp.start()
    else:
        for cp in remote:
            cp.wait()
        for cp in local:
            cp.wait()


def _sum_sources(recv, tile, name):
    n, rows, width = recv.shape

    def body(r_ref, o_ref):
        acc = r_ref[0].astype(F32)
        for s in range(1, n):
            acc = acc + r_ref[s].astype(F32)
        o_ref[...] = acc

    return pl.pallas_call(
        body, name=name, grid=(rows // tile,),
        in_specs=[pl.BlockSpec((n, tile, width), lambda i: (0, i, 0))],
        out_specs=pl.BlockSpec((tile, width), lambda i: (i, 0)),
        out_shape=jax.ShapeDtypeStruct((rows, width), F32),
        compiler_params=_params(("parallel",)),
    )(recv)


def _row_tile(rows):
    for tile in range(128, 15, -16):
        if rows % tile == 0:
            return tile
    return rows


def _adamw(w, recv, m, v, name):
    rows, width = w.shape
    n = recv.shape[0]
    tile = _row_tile(rows)
    c1 = 1.0 - ADAM_B1 ** ADAM_STEP
    c2 = 1.0 - ADAM_B2 ** ADAM_STEP

    def body(w_ref, r_ref, m_ref, v_ref, g_ref, d_ref, nm_ref, nv_ref):
        gv = r_ref[0].astype(F32)
        for s in range(1, n):
            gv = gv + r_ref[s].astype(F32)
        nm = ADAM_B1 * m_ref[...] + (1.0 - ADAM_B1) * gv
        nv = ADAM_B2 * v_ref[...] + (1.0 - ADAM_B2) * (gv * gv)
        g_ref[...] = gv
        nm_ref[...] = nm
        nv_ref[...] = nv
        d_ref[...] = -ADAM_LR * ((nm / c1) / (jnp.sqrt(nv / c2) + ADAM_EPS) + ADAM_WD * w_ref[...])

    spec = pl.BlockSpec((tile, width), lambda i: (i, 0))
    shape = jax.ShapeDtypeStruct((rows, width), F32)
    return pl.pallas_call(
        body, name=name, grid=(rows // tile,),
        in_specs=[spec, pl.BlockSpec((n, tile, width), lambda i: (0, i, 0)), spec, spec],
        out_specs=[spec] * 4, out_shape=[shape] * 4,
        compiler_params=_params(("parallel",)),
    )(w, recv, m, v)


BIG_NAMES = ("w_in", "w_out_ssm", "w_out_lru", "w_out", "w_ffn_in", "w_ffn_out", "ssm_conv_w", "lru_conv_w")
COL_SHARDED = ("w_in", "w_ffn_in", "ssm_conv_w", "lru_conv_w")
MATMUL_NAMES = BIG_NAMES[:6]
NEEDED_FIRST = ("w_in", "ssm_conv_w", "lru_conv_w")
NEEDED_LATER = ("w_out_ssm", "w_out_lru", "w_out", "w_ffn_in", "w_ffn_out")
SMALL_NAMES = ("norm1_w", "b_branch_gate", "ssm_conv_b", "ssm_dt_bias", "ssm_a_log", "ssm_d", "ssm_norm_w",
               "lru_conv_b", "lru_w_r", "lru_b_r", "lru_w_i", "lru_b_i", "lru_lambda", "norm2_w", "norm_f_w")


def _pack(arrays, total):
    flat = []
    used = 0
    for a in arrays:
        a = a.reshape(-1)
        pad = (-a.shape[0]) % LANES
        flat.append(a)
        if pad:
            flat.append(jnp.zeros((pad,), a.dtype))
        used += a.shape[0] + pad
    assert used <= total, (used, total)
    if total > used:
        flat.append(jnp.zeros((total - used,), arrays[0].dtype))
    return jnp.concatenate(flat)


def _unpack(flat, shapes):
    out = []
    off = 0
    for shp in shapes:
        n = math.prod(shp)
        out.append(flat[off:off + n].reshape(shp))
        off += n + (-n) % LANES
    return out


def _col_shards(full):
    rows, cols = full.shape
    return full.reshape(rows, N_DEV, cols // N_DEV).transpose(1, 0, 2)


def _row_shards(full):
    rows, cols = full.shape
    return full.reshape(N_DEV, rows // N_DEV, cols)


def _from_col_shards(g):
    n, rows, w = g.shape
    return g.transpose(1, 0, 2).reshape(rows, n * w)


def kernel(x, norm1_w, w_in, b_branch_gate, ssm_conv_w, ssm_conv_b, ssm_dt_bias, ssm_a_log, ssm_d, ssm_norm_w, w_out_ssm, lru_conv_w, lru_conv_b, lru_w_r, lru_b_r, lru_w_i, lru_b_i, lru_lambda, w_out_lru, w_out, norm2_w, w_ffn_in, w_ffn_out, norm_f_w, loss_target, m_norm1_w, m_w_in, m_b_branch_gate, m_ssm_conv_w, m_ssm_conv_b, m_ssm_dt_bias, m_ssm_a_log, m_ssm_d, m_ssm_norm_w, m_w_out_ssm, m_lru_conv_w, m_lru_conv_b, m_lru_w_r, m_lru_b_r, m_lru_w_i, m_lru_b_i, m_lru_lambda, m_w_out_lru, m_w_out, m_norm2_w, m_w_ffn_in, m_w_ffn_out, m_norm_f_w, v_norm1_w, v_w_in, v_b_branch_gate, v_ssm_conv_w, v_ssm_conv_b, v_ssm_dt_bias, v_ssm_a_log, v_ssm_d, v_ssm_norm_w, v_w_out_ssm, v_lru_conv_w, v_lru_conv_b, v_lru_w_r, v_lru_b_r, v_lru_w_i, v_lru_b_i, v_lru_lambda, v_w_out_lru, v_w_out, v_norm2_w, v_w_ffn_in, v_w_ffn_out, v_norm_f_w):
    given = dict(locals())
    weights = {n: given[n] for n in BIG_NAMES + SMALL_NAMES}
    t = x.shape[1]
    xt = x[0]
    tgt = loss_target[0]

    def shard(n):
        return weights[n][0].astype(BF16) if n in MATMUL_NAMES else weights[n][0]

    def unshard(n, g):
        return _from_col_shards(g) if n in COL_SHARDED else g.reshape(-1, g.shape[-1])

    def grad_slices(n):
        return (_col_shards if n in COL_SHARDED else _row_shards)(grads[n]).astype(BF16)

    gathered = _all_gather([shard(n) for n in NEEDED_FIRST], "gather_in_weights")
    full = {n: unshard(n, g) for n, g in zip(NEEDED_FIRST, gathered)}
    ssm_cw, lru_cw = full["ssm_conv_w"], full["lru_conv_w"]
    wi_full = full["w_in"]
    w_p = jnp.concatenate([wi_full[:, :ORIG_DT], wi_full[:, ORIG_LRU:], wi_full[:, ORIG_DT:ORIG_LRU],
                           jnp.zeros((D_MODEL, PROJ_W - IN_PROJ), BF16)], axis=1)

    def pad_heads(a):
        return jnp.pad(a.reshape(1, SSM_HEADS), ((0, 0), (0, LANES - SSM_HEADS)))

    dt_bias_p = pad_heads(ssm_dt_bias)
    a_log_p = pad_heads(ssm_a_log)
    d_exp = jnp.repeat(ssm_d.reshape(SSM_HEADS), SSM_HEAD_DIM).reshape(1, SSM_INNER)
    lru_wr, lru_wi = lru_w_r[0], lru_w_i[0]

    hn1 = _rmsnorm_fwd(xt, norm1_w, "norm1_fwd")
    proj = _mm(hn1, w_p, name="in_proj")
    xbc_act = _ssm_conv_fwd(proj, ssm_cw, ssm_conv_b)
    y_ssd, s_in_all = _ssd_fwd(xbc_act, proj, dt_bias_p, a_log_p, d_exp)
    l_out, gathered = _lru_fwd(proj, lru_cw, lru_conv_b, lru_wr, lru_b_r, lru_wi, lru_b_i, lru_lambda,
                               gather=[shard(n) for n in NEEDED_LATER])
    full.update({n: unshard(n, g) for n, g in zip(NEEDED_LATER, gathered)})
    y_pre = _ssm_post_fwd(y_ssd, proj, ssm_norm_w)
    y_ssm = _mm(y_pre, full["w_out_ssm"], name="out_ssm")
    y_lru = _mm(l_out, full["w_out_lru"], name="out_lru")
    merged = _merge_fwd(proj, b_branch_gate, y_ssm, y_lru)
    h1 = _mm(merged, full["w_out"], add=xt, name="out_proj")
    hn2 = _rmsnorm_fwd(h1, norm2_w, "norm2_fwd")
    gu = _mm(hn2, full["w_ffn_in"], name="ffn_in")
    act = _swiglu_fwd(gu)
    h2 = _mm(act, full["w_ffn_out"], add=h1, name="ffn_out")

    grads = {}
    dh2, grads["norm_f_w"], loss_cols = _loss_head(h2, norm_f_w.reshape(1, D_MODEL), tgt)
    loss = lax.psum(0.5 * jnp.sum(loss_cols) / D_MODEL, AXES)
    dact = _mm(dh2, full["w_ffn_out"], tb=True, name="d_act")
    grads["w_ffn_out"] = _mm(act, dh2, ta=True, name="dw_ffn_out")
    dgu = _swiglu_bwd(gu, dact)
    dhn2 = _mm(dgu, full["w_ffn_in"], tb=True, name="d_hn2")
    grads["w_ffn_in"] = _mm(hn2, dgu, ta=True, name="dw_ffn_in")
    dh1, grads["norm2_w"] = _rmsnorm_bwd(h1, norm2_w, dhn2, dh2, "norm2_bwd")
    dmerged = _mm(dh1, full["w_out"], tb=True, name="d_merged")
    grads["w_out"] = _mm(merged, dh1, ta=True, name="dw_out")
    dy_ssm, dy_lru, dgates, grads["b_branch_gate"] = _merge_bwd(proj, b_branch_gate, y_ssm, y_lru, dmerged)
    dy_pre = _mm(dy_ssm, full["w_out_ssm"], tb=True, name="d_y_pre")
    grads["w_out_ssm"] = _mm(y_pre, dy_ssm, ta=True, name="dw_out_ssm")
    dl_out = _mm(dy_lru, full["w_out_lru"], tb=True, name="d_l_out")
    grads["w_out_lru"] = _mm(l_out, dy_lru, ta=True, name="dw_out_lru")
    dy_ssd, dz, grads["ssm_norm_w"] = _ssm_post_bwd(y_ssd, proj, ssm_norm_w, dy_pre)
    (dxbc_act, ddt, dbias, dalog, ddcol), recv_later = _ssd_bwd(
        xbc_act, proj, s_in_all, dy_ssd, dt_bias_p, a_log_p, d_exp, exchange=[grad_slices(n) for n in NEEDED_LATER])
    grads["ssm_dt_bias"] = dbias[:, :SSM_HEADS]
    grads["ssm_a_log"] = dalog[:, :SSM_HEADS]
    grads["ssm_d"] = ddcol.reshape(SSM_HEADS, SSM_HEAD_DIM).sum(axis=1).reshape(1, SSM_HEADS)
    dxbc, grads["ssm_conv_w"], grads["ssm_conv_b"] = _ssm_conv_bwd(proj, ssm_cw, ssm_conv_b, dxbc_act)
    (dxl, dyl, grads["lru_conv_w"], grads["lru_conv_b"], dwr, grads["lru_b_r"], dwi, grads["lru_b_i"],
     grads["lru_lambda"]) = _lru_bwd(proj, lru_cw, lru_conv_b, lru_wr, lru_b_r, lru_wi, lru_b_i, lru_lambda, dl_out)
    grads["lru_w_r"], grads["lru_w_i"] = dwr[None], dwi[None]
    dproj = jnp.concatenate([dgates, dz, dxbc, dxl, dyl, ddt], axis=1)
    dwp = _mm(hn1, dproj, ta=True, name="dw_in")
    grads["w_in"] = jnp.concatenate([dwp[:, :ORIG_DT], dwp[:, COL_DT:COL_DT + SSM_HEADS], dwp[:, COL_LRU_X:COL_DT]],
                                    axis=1)
    dhn1, recv_first = _mm(dproj, w_p, tb=True, exchange=[grad_slices(n) for n in NEEDED_FIRST], name="d_hn1")
    grad_x, grads["norm1_w"] = _rmsnorm_bwd(xt, norm1_w, dhn1, dh1, "norm1_bwd")

    small_total = N_DEV * SMALL_ROWS * LANES
    small_parts = _pack([grads[n] for n in SMALL_NAMES], small_total).reshape(N_DEV, SMALL_ROWS, LANES)
    small_g = _sum_sources(_exchange([small_parts], "exchange_small_grads")[0], SMALL_ROWS, "sum_small_grads")
    small_g = _all_gather([small_g], "gather_small_grads")[0].reshape(1, small_total // FLAT_W, FLAT_W)

    recv = dict(zip(NEEDED_LATER + NEEDED_FIRST, list(recv_later) + list(recv_first)))
    big_out = {n: _adamw(weights[n][0], recv[n], given["m_" + n][0], given["v_" + n][0], "adamw_" + n)
               for n in BIG_NAMES}
    small_state = [_pack([given[p + n] for n in SMALL_NAMES], small_total).reshape(-1, FLAT_W) for p in ("", "m_", "v_")]
    small_out = _adamw(small_state[0], small_g, small_state[1], small_state[2], "adamw_replicated")

    small_shapes = [weights[n].shape for n in SMALL_NAMES]
    order = list(given)[1:24]
    results = []
    for q in range(4):
        vals = {n: big_out[n][q][None] for n in BIG_NAMES}
        vals.update(zip(SMALL_NAMES, _unpack(small_out[q].reshape(-1), small_shapes)))
        results.extend(vals[n] for n in order)
    return (loss, grad_x[None], *results)
```

```python
import math

import jax
import jax.numpy as jnp
from jax import lax
from jax.experimental import pallas as pl
from jax.experimental.pallas import tpu as pltpu

F32 = jnp.float32
BF16 = jnp.bfloat16
HIGHEST = lax.Precision.HIGHEST
MESH = pl.DeviceIdType.MESH
AXES = ("x", "y", "c")
N_DEV = 8

D_MODEL = 1024
SSM_INNER = 2048
SSM_HEADS = 32
SSM_HEAD_DIM = 64
SSM_GROUPS = 4
SSM_STATE = 128
SSM_BC = SSM_GROUPS * SSM_STATE
SSM_CONV_DIM = SSM_INNER + 2 * SSM_BC
SSM_CHUNK = 128
SSM_PAIRS = SSM_HEADS // 2
CONV_K = 4
LRU_WIDTH = 1280
LRU_BLOCKS = 10
LRU_C = 8.0
FFN_HIDDEN = 2816
RMS_EPS = 1e-6
IN_PROJ = 9760

COL_GATES = 0
COL_Z = 2048
COL_XBC = 4096
COL_LRU_X = 7168
COL_LRU_Y = 8448
COL_DT = 9728
PROJ_W = 9856
ORIG_DT = 7168
ORIG_LRU = 7200

ADAM_LR = 0.001
ADAM_B1 = 0.9
ADAM_B2 = 0.999
ADAM_EPS = 1e-08
ADAM_WD = 0.01
ADAM_STEP = 10

LANES = 128
SUBLANES = 8
V7X_VMEM_BYTES = 64 * 1024 * 1024
VMEM_LIMIT = V7X_VMEM_BYTES * 3 // 4
VMEM_LIMIT_BIG = V7X_VMEM_BYTES * 15 // 16

NT = (((1,), (1,)), ((), ()))
TN = (((0,), (0,)), ((), ()))

FLAT_W = 1024
SMALL_ROWS = 336


def _params(sem=None, big=False):
    return pltpu.CompilerParams(dimension_semantics=sem,
                                vmem_limit_bytes=VMEM_LIMIT_BIG if big else VMEM_LIMIT)


def _blk(dim, cap):
    if dim <= cap:
        return dim
    for m in range(cap // LANES, 0, -1):
        if dim % (m * LANES) == 0:
            return m * LANES
    raise ValueError(f"no block for {dim}")


def _rows(t):
    return min(t, 256)


def _sigmoid(v):
    return 1.0 / (1.0 + jnp.exp(-v))


def _softplus(v):
    e = jnp.exp(-jnp.abs(v))
    u = 1.0 + e
    log1p = jnp.where(u == 1.0, e, jnp.log(u) * e / jnp.where(u == 1.0, 1.0, u - 1.0))
    return jnp.maximum(v, 0.0) + log1p


def _iota(shape, dim):
    return lax.broadcasted_iota(jnp.int32, shape, dim)


def _shift_down(v, s):
    if s == 0:
        return v
    return jnp.where(_iota(v.shape, 0) >= s, pltpu.roll(v, s, 0), 0.0)


def _shift_up(v, s):
    if s == 0:
        return v
    n = v.shape[0]
    return jnp.where(_iota(v.shape, 0) < n - s, pltpu.roll(v, n - s, 0), 0.0)


def _bdot(a, b, dn=None):
    a = a.astype(BF16)
    b = b.astype(BF16)
    if dn is None:
        return jnp.dot(a, b, preferred_element_type=F32)
    return lax.dot_general(a, b, dn, preferred_element_type=F32)


def _fdot(a, b, dn=None):
    if dn is None:
        return jnp.dot(a, b, precision=HIGHEST, preferred_element_type=F32)
    return lax.dot_general(a, b, dn, precision=HIGHEST, preferred_element_type=F32)


def _mm(a, b, *, ta=False, tb=False, add=None, exchange=(), out_dtype=F32, name):
    if ta:
        kdim, m = a.shape
    else:
        m, kdim = a.shape
    if tb:
        n, k2 = b.shape
    else:
        k2, n = b.shape
    assert kdim == k2, (a.shape, b.shape, ta, tb)
    bm, bn, bk = _blk(m, 1024), _blk(n, 1408), _blk(kdim, 1408)
    grid = (m // bm, n // bn, kdim // bk)
    nk = grid[2]
    dn = (((0 if ta else 1,), (1 if tb else 0,)), ((), ()))
    n_in = 2 if add is None else 3
    n_ex = len(exchange)

    def body(*refs):
        a_ref, b_ref = refs[:2]
        r_ref = None if add is None else refs[2]
        o_ref = refs[n_in + n_ex]
        acc = refs[n_in + 2 * n_ex + 1]
        comm = (refs[n_in:n_in + n_ex], refs[n_in + n_ex + 1:n_in + 2 * n_ex + 1]) + tuple(refs[n_in + 2 * n_ex + 2:])
        step = (pl.program_id(0) * grid[1] + pl.program_id(1)) * nk + pl.program_id(2)
        k = pl.program_id(2)
        if n_ex:
            @pl.when(step == 0)
            def _():
                _exchange_phase("start", *comm)

        @pl.when(k == 0)
        def _():
            acc[...] = jnp.zeros_like(acc)

        acc[...] += lax.dot_general(a_ref[...].astype(BF16), b_ref[...].astype(BF16), dn,
                                    preferred_element_type=F32)

        @pl.when(k == nk - 1)
        def _():
            r = acc[...]
            if add is not None:
                r = r + r_ref[...]
            o_ref[...] = r.astype(out_dtype)

        if n_ex:
            @pl.when(step == grid[0] * grid[1] * nk - 1)
            def _():
                _exchange_phase("finish", *comm)

    a_spec = pl.BlockSpec((bk, bm), lambda i, j, k: (k, i)) if ta else pl.BlockSpec((bm, bk), lambda i, j, k: (i, k))
    b_spec = pl.BlockSpec((bn, bk), lambda i, j, k: (j, k)) if tb else pl.BlockSpec((bk, bn), lambda i, j, k: (k, j))
    o_spec = pl.BlockSpec((bm, bn), lambda i, j, k: (i, j))
    in_specs = [a_spec, b_spec] + ([o_spec] if add is not None else []) + [HBM] * n_ex
    args = (a, b) + ((add,) if add is not None else ()) + tuple(exchange)
    out = pl.pallas_call(
        body, name=name, grid=grid,
        in_specs=in_specs, out_specs=[o_spec] + [HBM] * n_ex,
        out_shape=[jax.ShapeDtypeStruct((m, n), out_dtype)]
        + [jax.ShapeDtypeStruct(p.shape, p.dtype) for p in exchange],
        scratch_shapes=[pltpu.VMEM((bm, bn), F32)] + (_comm_scratch(n_ex) if n_ex else []),
        compiler_params=_params(("arbitrary",) * 3 if n_ex else ("parallel", "parallel", "arbitrary")),
    )(*args)
    return (out[0], out[1:]) if n_ex else out[0]


def _rmsnorm_fwd(x, w, name):
    t, d = x.shape
    tr = _rows(t)

    def body(x_ref, w_ref, o_ref):
        xv = x_ref[...]
        rstd = lax.rsqrt(jnp.mean(xv * xv, axis=-1, keepdims=True) + RMS_EPS)
        o_ref[...] = (xv * rstd * w_ref[...]).astype(BF16)

    return pl.pallas_call(
        body, name=name, grid=(t // tr,),
        in_specs=[pl.BlockSpec((tr, d), lambda i: (i, 0)), pl.BlockSpec((1, d), lambda i: (0, 0))],
        out_specs=pl.BlockSpec((tr, d), lambda i: (i, 0)),
        out_shape=jax.ShapeDtypeStruct((t, d), BF16),
        compiler_params=_params(("parallel",)),
    )(x, w)


def _rmsnorm_bwd(x, w, dy, dres, name):
    t, d = x.shape
    tr = _rows(t)

    def body(x_ref, w_ref, dy_ref, dres_ref, dx_ref, dw_ref):
        i = pl.program_id(0)
        xv = x_ref[...]
        rstd = lax.rsqrt(jnp.mean(xv * xv, axis=-1, keepdims=True) + RMS_EPS)
        xhat = xv * rstd
        dyv = dy_ref[...]
        dxhat = dyv * w_ref[...]
        m = jnp.mean(dxhat * xhat, axis=-1, keepdims=True)
        dx_ref[...] = rstd * (dxhat - xhat * m) + dres_ref[...]
        part = jnp.sum(dyv * xhat, axis=0, keepdims=True)

        @pl.when(i == 0)
        def _():
            dw_ref[...] = part

        @pl.when(i > 0)
        def _():
            dw_ref[...] += part

    row = pl.BlockSpec((tr, d), lambda i: (i, 0))
    vec = pl.BlockSpec((1, d), lambda i: (0, 0))
    return pl.pallas_call(
        body, name=name, grid=(t // tr,),
        in_specs=[row, vec, row, row], out_specs=[row, vec],
        out_shape=[jax.ShapeDtypeStruct((t, d), F32), jax.ShapeDtypeStruct((1, d), F32)],
        compiler_params=_params(("arbitrary",)),
    )(x, w, dy, dres)


def _loss_head(h2, w, tgt):
    t, d = h2.shape
    tr = _rows(t)

    def body(x_ref, w_ref, t_ref, dx_ref, dw_ref, ls_ref):
        i = pl.program_id(0)
        xv = x_ref[...]
        wv = w_ref[...]
        rstd = lax.rsqrt(jnp.mean(xv * xv, axis=-1, keepdims=True) + RMS_EPS)
        xhat = xv * rstd
        err = xhat * wv - t_ref[...]
        dyv = err * (1.0 / d)
        dxhat = dyv * wv
        m = jnp.mean(dxhat * xhat, axis=-1, keepdims=True)
        dx_ref[...] = rstd * (dxhat - xhat * m)
        dw_part = jnp.sum(dyv * xhat, axis=0, keepdims=True)
        ls_part = jnp.sum(err * err, axis=0, keepdims=True)

        @pl.when(i == 0)
        def _():
            dw_ref[...] = dw_part
            ls_ref[...] = ls_part

        @pl.when(i > 0)
        def _():
            dw_ref[...] += dw_part
            ls_ref[...] += ls_part

    row = pl.BlockSpec((tr, d), lambda i: (i, 0))
    vec = pl.BlockSpec((1, d), lambda i: (0, 0))
    return pl.pallas_call(
        body, name="loss_head", grid=(t // tr,),
        in_specs=[row, vec, row], out_specs=[row, vec, vec],
        out_shape=[jax.ShapeDtypeStruct((t, d), F32), jax.ShapeDtypeStruct((1, d), F32),
                   jax.ShapeDtypeStruct((1, d), F32)],
        compiler_params=_params(("arbitrary",)),
    )(h2, w, tgt)


def _merge_fwd(proj, bg, ys, yl):
    t = proj.shape[0]
    d = D_MODEL
    tr = _rows(t)

    def body(ps_ref, pl_ref, bg_ref, ys_ref, yl_ref, o_ref):
        gs = _sigmoid(ps_ref[...] + bg_ref[:, 0:d])
        gl = _sigmoid(pl_ref[...] + bg_ref[:, d:2 * d])
        o_ref[...] = (gs * ys_ref[...] + gl * yl_ref[...]).astype(BF16)

    row = pl.BlockSpec((tr, d), lambda i: (i, 0))
    return pl.pallas_call(
        body, name="merge_fwd", grid=(t // tr,),
        in_specs=[row, pl.BlockSpec((tr, d), lambda i: (i, 1)), pl.BlockSpec((1, 2 * d), lambda i: (0, 0)), row, row],
        out_specs=row, out_shape=jax.ShapeDtypeStruct((t, d), BF16),
        compiler_params=_params(("parallel",)),
    )(proj, proj, bg, ys, yl)


def _merge_bwd(proj, bg, ys, yl, dm):
    t = proj.shape[0]
    d = D_MODEL
    tr = _rows(t)

    def body(ps_ref, pl_ref, bg_ref, ys_ref, yl_ref, dm_ref, dys_ref, dyl_ref, dg_ref, dbg_ref):
        i = pl.program_id(0)
        gs = _sigmoid(ps_ref[...] + bg_ref[:, 0:d])
        gl = _sigmoid(pl_ref[...] + bg_ref[:, d:2 * d])
        dmv = dm_ref[...]
        dys_ref[...] = (dmv * gs).astype(BF16)
        dyl_ref[...] = (dmv * gl).astype(BF16)
        dgs = dmv * ys_ref[...] * gs * (1.0 - gs)
        dgl = dmv * yl_ref[...] * gl * (1.0 - gl)
        dg_ref[:, 0:d] = dgs.astype(BF16)
        dg_ref[:, d:2 * d] = dgl.astype(BF16)

        @pl.when(i == 0)
        def _():
            dbg_ref[...] = jnp.zeros_like(dbg_ref)

        dbg_ref[:, 0:d] += jnp.sum(dgs, axis=0, keepdims=True)
        dbg_ref[:, d:2 * d] += jnp.sum(dgl, axis=0, keepdims=True)

    row = pl.BlockSpec((tr, d), lambda i: (i, 0))
    wide = pl.BlockSpec((tr, 2 * d), lambda i: (i, 0))
    vec = pl.BlockSpec((1, 2 * d), lambda i: (0, 0))
    return pl.pallas_call(
        body, name="merge_bwd", grid=(t // tr,),
        in_specs=[row, pl.BlockSpec((tr, d), lambda i: (i, 1)), vec, row, row, row],
        out_specs=[row, row, wide, vec],
        out_shape=[jax.ShapeDtypeStruct((t, d), BF16), jax.ShapeDtypeStruct((t, d), BF16),
                   jax.ShapeDtypeStruct((t, 2 * d), BF16), jax.ShapeDtypeStruct((1, 2 * d), F32)],
        compiler_params=_params(("arbitrary",)),
    )(proj, proj, bg, ys, yl, dm)


def _swiglu_fwd(gu):
    t = gu.shape[0]
    f = FFN_HIDDEN
    tr = _rows(t)

    def body(g_ref, u_ref, o_ref):
        g = g_ref[...]
        o_ref[...] = (g * _sigmoid(g) * u_ref[...]).astype(BF16)

    return pl.pallas_call(
        body, name="swiglu_fwd", grid=(t // tr,),
        in_specs=[pl.BlockSpec((tr, f), lambda i: (i, 0)), pl.BlockSpec((tr, f), lambda i: (i, 1))],
        out_specs=pl.BlockSpec((tr, f), lambda i: (i, 0)),
        out_shape=jax.ShapeDtypeStruct((t, f), BF16),
        compiler_params=_params(("parallel",)),
    )(gu, gu)


def _swiglu_bwd(gu, dact):
    t = gu.shape[0]
    f = FFN_HIDDEN
    tr = _rows(t)

    def body(g_ref, u_ref, da_ref, o_ref):
        g = g_ref[...]
        sg = _sigmoid(g)
        da = da_ref[...]
        o_ref[:, 0:f] = (da * u_ref[...] * (sg * (1.0 + g * (1.0 - sg)))).astype(BF16)
        o_ref[:, f:2 * f] = (da * g * sg).astype(BF16)

    return pl.pallas_call(
        body, name="swiglu_bwd", grid=(t // tr,),
        in_specs=[pl.BlockSpec((tr, f), lambda i: (i, 0)), pl.BlockSpec((tr, f), lambda i: (i, 1)),
                  pl.BlockSpec((tr, f), lambda i: (i, 0))],
        out_specs=pl.BlockSpec((tr, 2 * f), lambda i: (i, 0)),
        out_shape=jax.ShapeDtypeStruct((t, 2 * f), BF16),
        compiler_params=_params(("parallel",)),
    )(gu, gu, dact)


def _conv_pre(xv, wv, bv):
    pre = bv + wv[CONV_K - 1:CONV_K, :] * xv
    for k in range(CONV_K - 1):
        pre = pre + wv[k:k + 1, :] * _shift_down(xv, CONV_K - 1 - k)
    return pre


def _ssm_conv_fwd(proj, w, b):
    t = proj.shape[0]
    nb = SSM_CONV_DIM // LANES
    c0 = COL_XBC // LANES

    def body(x_ref, w_ref, b_ref, o_ref):
        pre = _conv_pre(x_ref[...], w_ref[...], b_ref[...])
        o_ref[...] = pre * _sigmoid(pre)

    return pl.pallas_call(
        body, name="ssm_conv_fwd", grid=(nb,),
        in_specs=[pl.BlockSpec((t, LANES), lambda j: (0, c0 + j)), pl.BlockSpec((CONV_K, LANES), lambda j: (0, j)),
                  pl.BlockSpec((1, LANES), lambda j: (0, j))],
        out_specs=pl.BlockSpec((t, LANES), lambda j: (0, j)),
        out_shape=jax.ShapeDtypeStruct((t, SSM_CONV_DIM), F32),
        compiler_params=_params(("parallel",)),
    )(proj, w, b)


def _ssm_conv_bwd(proj, w, b, dact):
    t = proj.shape[0]
    nb = SSM_CONV_DIM // LANES
    c0 = COL_XBC // LANES

    def body(x_ref, w_ref, b_ref, da_ref, dx_ref, dw_ref, db_ref):
        xv = x_ref[...]
        wv = w_ref[...]
        pre = _conv_pre(xv, wv, b_ref[...])
        sg = _sigmoid(pre)
        dpre = da_ref[...] * (sg * (1.0 + pre * (1.0 - sg)))
        dx = wv[CONV_K - 1:CONV_K, :] * dpre
        for k in range(CONV_K - 1):
            dx = dx + wv[k:k + 1, :] * _shift_up(dpre, CONV_K - 1 - k)
        dx_ref[...] = dx.astype(BF16)
        for k in range(CONV_K):
            dw_ref[k:k + 1, :] = jnp.sum(dpre * _shift_down(xv, CONV_K - 1 - k), axis=0, keepdims=True)
        db_ref[...] = jnp.sum(dpre, axis=0, keepdims=True)

    col = pl.BlockSpec((t, LANES), lambda j: (0, j))
    wsp = pl.BlockSpec((CONV_K, LANES), lambda j: (0, j))
    bsp = pl.BlockSpec((1, LANES), lambda j: (0, j))
    return pl.pallas_call(
        body, name="ssm_conv_bwd", grid=(nb,),
        in_specs=[pl.BlockSpec((t, LANES), lambda j: (0, c0 + j)), wsp, bsp, col],
        out_specs=[col, wsp, bsp],
        out_shape=[jax.ShapeDtypeStruct((t, SSM_CONV_DIM), BF16), jax.ShapeDtypeStruct((CONV_K, SSM_CONV_DIM), F32),
                   jax.ShapeDtypeStruct((1, SSM_CONV_DIM), F32)],
        compiler_params=_params(("parallel",)),
    )(proj, w, b, dact)


def _ssd_chunk_terms(dtr, bias, alog):
    a = -jnp.exp(alog)
    dt = _softplus(dtr + bias)
    row = _iota((SSM_CHUNK, SSM_CHUNK), 0)
    col = _iota((SSM_CHUNK, SSM_CHUNK), 1)
    tri = (row >= col).astype(F32)
    cs = _fdot(tri, dt * a)
    dec = jnp.exp(cs[SSM_CHUNK - 1:SSM_CHUNK, :] - cs)
    ecs = jnp.exp(cs)
    off = _iota((LANES, SSM_INNER), 1) - SSM_HEAD_DIM * _iota((LANES, SSM_INNER), 0)
    expand = jnp.logical_and(off >= 0, off < SSM_HEAD_DIM).astype(F32)
    return a, dt, cs, dec, ecs, expand, row, col


def _ssd_specs(t):
    nc = t // SSM_CHUNK
    xs = pl.BlockSpec((SSM_CHUNK, SSM_INNER), lambda c: (c, 0))
    bm = pl.BlockSpec((SSM_CHUNK, SSM_BC), lambda c: (c, SSM_INNER // SSM_BC))
    cm = pl.BlockSpec((SSM_CHUNK, SSM_BC), lambda c: (c, SSM_INNER // SSM_BC + 1))
    dtr = pl.BlockSpec((SSM_CHUNK, LANES), lambda c: (c, COL_DT // LANES))
    vec = pl.BlockSpec((1, LANES), lambda c: (0, 0))
    wide = pl.BlockSpec((1, SSM_INNER), lambda c: (0, 0))
    return nc, xs, bm, cm, dtr, vec, wide


def _ssd_fwd(xbc_act, proj, bias, alog, dexp):
    t = proj.shape[0]
    nc, xs_s, bm_s, cm_s, dtr_s, vec, wide = _ssd_specs(t)

    def body(xs_ref, b_ref, c_ref, dtr_ref, bias_ref, alog_ref, dexp_ref, y_ref, sin_ref, state):
        @pl.when(pl.program_id(0) == 0)
        def _():
            state[...] = jnp.zeros_like(state)

        a, dt, cs, dec, ecs, expand, row, col = _ssd_chunk_terms(dtr_ref[...], bias_ref[...], alog_ref[...])
        cst = cs.T
        dt_x = _fdot(dt, expand)
        dec_x = _fdot(dec, expand)
        ecs_x = _fdot(ecs, expand)
        xs = xs_ref[...]
        xdt = xs * dt_x
        xdec = xdt * dec_x
        lane_lo = col < SSM_HEAD_DIM
        causal = row >= col
        sin_ref[0] = state[...]
        for g in range(SSM_GROUPS):
            bg = b_ref[:, g * SSM_STATE:(g + 1) * SSM_STATE].astype(BF16)
            cg = c_ref[:, g * SSM_STATE:(g + 1) * SSM_STATE].astype(BF16)
            cb = _bdot(cg, bg, NT)
            for q in range(SSM_PAIRS // SSM_GROUPS):
                pq = g * (SSM_PAIRS // SSM_GROUPS) + q
                sl = slice(pq * LANES, (pq + 1) * LANES)
                xp = xdt[:, sl].astype(BF16)
                yd = []
                for hh in range(2):
                    h = 2 * pq + hh
                    lmat = jnp.exp(jnp.where(causal, cs[:, h:h + 1] - cst[h:h + 1, :], -jnp.inf))
                    yd.append(_bdot(cb * lmat, xp))
                s_in = state[pq]
                y_off = _bdot(cg, s_in) * ecs_x[:, sl]
                y_ref[:, sl] = jnp.where(lane_lo, yd[0], yd[1]) + y_off + xs[:, sl] * dexp_ref[:, sl]
                state[pq] = s_in * ecs_x[SSM_CHUNK - 1:SSM_CHUNK, sl] + _bdot(bg, xdec[:, sl], TN)

    return pl.pallas_call(
        body, name="ssd_fwd", grid=(nc,),
        in_specs=[xs_s, bm_s, cm_s, dtr_s, vec, vec, wide],
        out_specs=[pl.BlockSpec((SSM_CHUNK, SSM_INNER), lambda c: (c, 0)),
                   pl.BlockSpec((1, SSM_PAIRS, SSM_STATE, LANES), lambda c: (c, 0, 0, 0))],
        out_shape=[jax.ShapeDtypeStruct((t, SSM_INNER), F32),
                   jax.ShapeDtypeStruct((nc, SSM_PAIRS, SSM_STATE, LANES), F32)],
        scratch_shapes=[pltpu.VMEM((SSM_PAIRS, SSM_STATE, LANES), F32)],
        compiler_params=_params(("arbitrary",)),
    )(xbc_act, xbc_act, xbc_act, proj, bias, alog, dexp)


def _ssd_bwd(xbc_act, proj, s_in_all, dy, bias, alog, dexp, exchange):
    n_ex = len(exchange)
    t = proj.shape[0]
    nc = t // SSM_CHUNK
    last = nc - 1
    xs_s = pl.BlockSpec((SSM_CHUNK, SSM_INNER), lambda c: (last - c, 0))
    bm_s = pl.BlockSpec((SSM_CHUNK, SSM_BC), lambda c: (last - c, SSM_INNER // SSM_BC))
    cm_s = pl.BlockSpec((SSM_CHUNK, SSM_BC), lambda c: (last - c, SSM_INNER // SSM_BC + 1))
    dtr_s = pl.BlockSpec((SSM_CHUNK, LANES), lambda c: (last - c, COL_DT // LANES))
    sin_s = pl.BlockSpec((1, SSM_PAIRS, SSM_STATE, LANES), lambda c: (last - c, 0, 0, 0))
    vec = pl.BlockSpec((1, LANES), lambda c: (0, 0))
    wide = pl.BlockSpec((1, SSM_INNER), lambda c: (0, 0))

    def body(*refs):
        xs_ref, b_ref, c_ref, dtr_ref, sin_ref, dy_ref, bias_ref, alog_ref, dexp_ref = refs[:9]
        dxbc_ref, ddtr_ref, dbias_ref, dalog_ref, ddcol_ref = refs[9 + n_ex:14 + n_ex]
        dstate, dxdt_s, yoff_s, rx_s, trow_s = refs[14 + 2 * n_ex:19 + 2 * n_ex]
        comm = (refs[9:9 + n_ex], refs[14 + n_ex:14 + 2 * n_ex]) + tuple(refs[19 + 2 * n_ex:])

        @pl.when(pl.program_id(0) == 0)
        def _():
            _exchange_phase("start", *comm)
            dstate[...] = jnp.zeros_like(dstate)
            trow_s[...] = jnp.zeros_like(trow_s)
            dbias_ref[...] = jnp.zeros_like(dbias_ref)
            dalog_ref[...] = jnp.zeros_like(dalog_ref)
            ddcol_ref[...] = jnp.zeros_like(ddcol_ref)

        dtr = dtr_ref[...]
        a, dt, cs, dec, ecs, expand, row, col = _ssd_chunk_terms(dtr, bias_ref[...], alog_ref[...])
        cst = cs.T
        dt_x = _fdot(dt, expand)
        dec_x = _fdot(dec, expand)
        ecs_x = _fdot(ecs, expand)
        xs = xs_ref[...]
        dyv = dy_ref[...]
        xdt = xs * dt_x
        lane_lo = col < SSM_HEAD_DIM
        causal = row >= col
        ddcol_ref[...] += jnp.sum(dyv * xs, axis=0, keepdims=True)
        dcs_col = jnp.zeros((SSM_CHUNK, LANES), F32)
        dcs_row = jnp.zeros((LANES, SSM_CHUNK), F32)
        for g in range(SSM_GROUPS):
            bg = b_ref[:, g * SSM_STATE:(g + 1) * SSM_STATE].astype(BF16)
            cg = c_ref[:, g * SSM_STATE:(g + 1) * SSM_STATE].astype(BF16)
            cb = _bdot(cg, bg, NT)
            dgm = jnp.zeros((SSM_CHUNK, SSM_CHUNK), F32)
            dbg = jnp.zeros((SSM_CHUNK, SSM_STATE), F32)
            dcg = jnp.zeros((SSM_CHUNK, SSM_STATE), F32)
            for q in range(SSM_PAIRS // SSM_GROUPS):
                pq = g * (SSM_PAIRS // SSM_GROUPS) + q
                sl = slice(pq * LANES, (pq + 1) * LANES)
                dyp = dyv[:, sl]
                xp = xdt[:, sl]
                dxh = []
                for hh in range(2):
                    h = 2 * pq + hh
                    lmat = jnp.exp(jnp.where(causal, cs[:, h:h + 1] - cst[h:h + 1, :], -jnp.inf))
                    mmat = cb * lmat
                    dyh = jnp.where(lane_lo if hh == 0 else jnp.logical_not(lane_lo), dyp, 0.0)
                    dmm = _bdot(dyh, xp, NT)
                    pm = dmm * mmat
                    dcs_col = jnp.where(col == h, jnp.sum(pm, axis=1, keepdims=True), dcs_col)
                    dcs_row = jnp.where(row == h, jnp.sum(pm, axis=0, keepdims=True), dcs_row)
                    dgm = dgm + dmm * lmat
                    dxh.append(_bdot(mmat, dyp, TN))
                s_in = sin_ref[0, pq]
                ecs_p = ecs_x[:, sl]
                dec_p = dec_x[:, sl]
                etot_p = ecs_x[SSM_CHUNK - 1:SSM_CHUNK, sl]
                yoff_s[:, sl] = dyp * (_bdot(cg, s_in) * ecs_p)
                dq = dyp * ecs_p
                dcg = dcg + _bdot(dq, s_in, NT)
                ds = dstate[pq]
                r = _bdot(bg, ds)
                rx_s[:, sl] = r * xp
                dxdt_s[:, sl] = jnp.where(lane_lo, dxh[0], dxh[1]) + dec_p * r
                dbg = dbg + _bdot(xp * dec_p, ds, NT)
                trow_s[0:1, sl] = jnp.sum(ds * s_in, axis=0, keepdims=True) * etot_p
                dstate[pq] = etot_p * ds + _bdot(cg, dq, TN)
            dcg = dcg + _bdot(dgm, bg)
            dbg = dbg + _bdot(dgm, cg, TN)
            dxbc_ref[:, SSM_INNER + g * SSM_STATE:SSM_INNER + (g + 1) * SSM_STATE] = dbg
            dxbc_ref[:, SSM_INNER + SSM_BC + g * SSM_STATE:SSM_INNER + SSM_BC + (g + 1) * SSM_STATE] = dcg
        ddec = _fdot(rx_s[...], expand, NT) * dec
        dtot = _fdot(trow_s[...], expand, NT)[0:1, :]
        dcs = dcs_col - dcs_row.T + _fdot(yoff_s[...], expand, NT) - ddec
        dcs = dcs + jnp.where(row == SSM_CHUNK - 1, jnp.sum(ddec, axis=0, keepdims=True) + dtot, 0.0)
        da = _fdot((row <= col).astype(F32), dcs)
        dxdt = dxdt_s[...]
        ddt = da * a + _fdot(dxdt * xs, expand, NT)
        dalog_ref[...] += jnp.sum(da * dt, axis=0, keepdims=True) * a
        ddtr = ddt * _sigmoid(dtr + bias_ref[...])
        ddtr_ref[...] = ddtr.astype(BF16)
        dbias_ref[...] += jnp.sum(ddtr, axis=0, keepdims=True)
        dxbc_ref[:, 0:SSM_INNER] = dxdt * dt_x + dyv * dexp_ref[...]

        @pl.when(pl.program_id(0) == last)
        def _():
            _exchange_phase("finish", *comm)

    out = pl.pallas_call(
        body, name="ssd_bwd", grid=(nc,),
        in_specs=[xs_s, bm_s, cm_s, dtr_s, sin_s, pl.BlockSpec((SSM_CHUNK, SSM_INNER), lambda c: (last - c, 0)),
                  vec, vec, wide] + [HBM] * n_ex,
        out_specs=[pl.BlockSpec((SSM_CHUNK, SSM_CONV_DIM), lambda c: (last - c, 0)),
                   pl.BlockSpec((SSM_CHUNK, LANES), lambda c: (last - c, 0)), vec, vec, wide] + [HBM] * n_ex,
        out_shape=[jax.ShapeDtypeStruct((t, SSM_CONV_DIM), F32), jax.ShapeDtypeStruct((t, LANES), BF16),
                   jax.ShapeDtypeStruct((1, LANES), F32), jax.ShapeDtypeStruct((1, LANES), F32),
                   jax.ShapeDtypeStruct((1, SSM_INNER), F32)]
        + [jax.ShapeDtypeStruct(p.shape, p.dtype) for p in exchange],
        scratch_shapes=[pltpu.VMEM((SSM_PAIRS, SSM_STATE, LANES), F32),
                        pltpu.VMEM((SSM_CHUNK, SSM_INNER), F32), pltpu.VMEM((SSM_CHUNK, SSM_INNER), F32),
                        pltpu.VMEM((SSM_CHUNK, SSM_INNER), F32), pltpu.VMEM((SUBLANES, SSM_INNER), F32)]
        + _comm_scratch(n_ex),
        compiler_params=_params(("arbitrary",)),
    )(xbc_act, xbc_act, xbc_act, proj, s_in_all, dy, bias, alog, dexp, *exchange)
    return out[:5], out[5:]


def _group_rstd(y):
    n = SSM_INNER // SSM_GROUPS
    parts = []
    for g in range(SSM_GROUPS):
        yg = y[:, g * n:(g + 1) * n]
        r = lax.rsqrt(jnp.mean(yg * yg, axis=-1, keepdims=True) + RMS_EPS)
        parts.append(jnp.broadcast_to(r, yg.shape))
    return jnp.concatenate(parts, axis=1)


def _group_mean(v):
    n = SSM_INNER // SSM_GROUPS
    parts = []
    for g in range(SSM_GROUPS):
        vg = v[:, g * n:(g + 1) * n]
        parts.append(jnp.broadcast_to(jnp.mean(vg, axis=-1, keepdims=True), vg.shape))
    return jnp.concatenate(parts, axis=1)


def _ssm_post_fwd(y_ssd, proj, nw):
    t = proj.shape[0]
    n = SSM_INNER
    tr = _rows(t)

    def body(y_ref, z_ref, nw_ref, o_ref):
        z = z_ref[...]
        y = y_ref[...] * (z * _sigmoid(z))
        o_ref[...] = (y * _group_rstd(y) * nw_ref[...]).astype(BF16)

    row = pl.BlockSpec((tr, n), lambda i: (i, 0))
    return pl.pallas_call(
        body, name="ssm_post_fwd", grid=(t // tr,),
        in_specs=[row, pl.BlockSpec((tr, n), lambda i: (i, COL_Z // n)), pl.BlockSpec((1, n), lambda i: (0, 0))],
        out_specs=row, out_shape=jax.ShapeDtypeStruct((t, n), BF16),
        compiler_params=_params(("parallel",)),
    )(y_ssd, proj, nw)


def _ssm_post_bwd(y_ssd, proj, nw, dout):
    t = proj.shape[0]
    n = SSM_INNER
    tr = _rows(t)

    def body(y_ref, z_ref, nw_ref, do_ref, dy_ref, dz_ref, dnw_ref):
        i = pl.program_id(0)
        z = z_ref[...]
        sg = _sigmoid(z)
        sz = z * sg
        ys = y_ref[...]
        y = ys * sz
        rstd = _group_rstd(y)
        yn = y * rstd
        dov = do_ref[...]
        dyn = dov * nw_ref[...]
        dyg = rstd * (dyn - yn * _group_mean(dyn * yn))
        dy_ref[...] = dyg * sz
        dz_ref[...] = (dyg * ys * (sg * (1.0 + z * (1.0 - sg)))).astype(BF16)
        part = jnp.sum(dov * yn, axis=0, keepdims=True)

        @pl.when(i == 0)
        def _():
            dnw_ref[...] = part

        @pl.when(i > 0)
        def _():
            dnw_ref[...] += part

    row = pl.BlockSpec((tr, n), lambda i: (i, 0))
    vec = pl.BlockSpec((1, n), lambda i: (0, 0))
    return pl.pallas_call(
        body, name="ssm_post_bwd", grid=(t // tr,),
        in_specs=[row, pl.BlockSpec((tr, n), lambda i: (i, COL_Z // n)), vec, row],
        out_specs=[row, row, vec],
        out_shape=[jax.ShapeDtypeStruct((t, n), F32), jax.ShapeDtypeStruct((t, n), BF16),
                   jax.ShapeDtypeStruct((1, n), F32)],
        compiler_params=_params(("arbitrary",)),
    )(y_ssd, proj, nw, dout)


GELU_C = math.sqrt(2.0 / math.pi)
GELU_K = 0.044715


def _gelu_parts(y):
    th = jnp.tanh(GELU_C * (y + GELU_K * y * y * y))
    val = 0.5 * y * (1.0 + th)
    grad = 0.5 * (1.0 + th) + 0.5 * y * (1.0 - th * th) * GELU_C * (1.0 + 3.0 * GELU_K * y * y)
    return val, grad


def _scan_tiles(a_ref, b_ref, h_ref, n_rows, reverse):
    n_tiles = n_rows // SUBLANES
    shape = (SUBLANES, a_ref.shape[1])
    row = _iota(shape, 0)

    def step(k, carry):
        tile = (n_tiles - 1 - k) if reverse else k
        at = pl.ds(pl.multiple_of(tile * SUBLANES, SUBLANES), SUBLANES)
        av = a_ref[at, :]
        bv = b_ref[at, :]
        for s in (1, 2, 4):
            if reverse:
                keep = row < SUBLANES - s
                a_sh = jnp.where(keep, pltpu.roll(av, SUBLANES - s, 0), 1.0)
                b_sh = jnp.where(keep, pltpu.roll(bv, SUBLANES - s, 0), 0.0)
            else:
                keep = row >= s
                a_sh = jnp.where(keep, pltpu.roll(av, s, 0), 1.0)
                b_sh = jnp.where(keep, pltpu.roll(bv, s, 0), 0.0)
            bv = av * b_sh + bv
            av = av * a_sh
        hv = bv + av * carry
        h_ref[at, :] = hv
        return hv[0:1, :] if reverse else hv[SUBLANES - 1:SUBLANES, :]

    lax.fori_loop(0, n_tiles, step, jnp.zeros((1, a_ref.shape[1]), F32))


def _lru_gates(xl, cw, cb, wr, br, wi, bi, lam):
    u = cb + cw[CONV_K - 1:CONV_K, :] * xl
    for k in range(CONV_K - 1):
        u = u + cw[k:k + 1, :] * _shift_down(xl, CONV_K - 1 - k)
    r = _sigmoid(_bdot(u, wr) + br)
    i = _sigmoid(_bdot(u, wi) + bi)
    sp = _softplus(-lam)
    la = -LRU_C * r * sp
    a = jnp.exp(la)
    mult = jnp.sqrt(-jnp.tanh(la) * (a * a + 1.0))
    return u, r, i, sp, a, mult


def _lru_specs(t):
    c_x = COL_LRU_X // LANES
    c_y = COL_LRU_Y // LANES
    xl = pl.BlockSpec((t, LANES), lambda j: (0, c_x + j))
    yl = pl.BlockSpec((t, LANES), lambda j: (0, c_y + j))
    col = pl.BlockSpec((t, LANES), lambda j: (0, j))
    cw = pl.BlockSpec((CONV_K, LANES), lambda j: (0, j))
    vec = pl.BlockSpec((1, LANES), lambda j: (0, j))
    wblk = pl.BlockSpec((1, LANES, LANES), lambda j: (j, 0, 0))
    return xl, yl, col, cw, vec, wblk


def _lru_fwd(proj, cw, cb, wr, br, wi, bi, lam, gather):
    t = proj.shape[0]
    xl_s, yl_s, col, cw_s, vec, wblk = _lru_specs(t)
    n = len(gather)

    def body(*refs):
        xl_ref, yl_ref, cw_ref, cb_ref, wr_ref, br_ref, wi_ref, bi_ref, lam_ref = refs[:9]
        o_ref = refs[9 + n]
        a_s, b_s, h_s = refs[10 + 2 * n:13 + 2 * n]
        comm = (refs[9:9 + n], refs[10 + n:10 + 2 * n]) + tuple(refs[13 + 2 * n:])
        j = pl.program_id(0)
        for step, phase in ((0, "start"), (LRU_BLOCKS // 2, "forward")):
            @pl.when(j == step)
            def _():
                _gather_phase(phase, *comm)

        u, r, i, sp, a, mult = _lru_gates(xl_ref[...], cw_ref[...], cb_ref[...], wr_ref[0], br_ref[...],
                                          wi_ref[0], bi_ref[...], lam_ref[...])
        a_s[...] = a
        b_s[...] = mult * (i * u)
        _scan_tiles(a_s, b_s, h_s, t, reverse=False)
        o_ref[...] = (h_s[...] * _gelu_parts(yl_ref[...])[0]).astype(BF16)

        @pl.when(j == LRU_BLOCKS - 1)
        def _():
            _gather_phase("finish", *comm)

    out = pl.pallas_call(
        body, name="lru_fwd", grid=(LRU_BLOCKS,),
        in_specs=[xl_s, yl_s, cw_s, vec, wblk, vec, wblk, vec, vec] + [HBM] * n,
        out_specs=[col] + [HBM] * n,
        out_shape=[jax.ShapeDtypeStruct((t, LRU_WIDTH), BF16)]
        + [jax.ShapeDtypeStruct((N_DEV,) + v.shape, v.dtype) for v in gather],
        scratch_shapes=[pltpu.VMEM((t, LANES), F32)] * 3 + _comm_scratch(n),
        compiler_params=_params(("arbitrary",), big=True),
    )(proj, proj, cw, cb, wr, br, wi, bi, lam, *gather)
    return out[0], out[1:]


def _lru_bwd(proj, cw, cb, wr, br, wi, bi, lam, dout):
    t = proj.shape[0]
    xl_s, yl_s, col, cw_s, vec, wblk = _lru_specs(t)

    def body(xl_ref, yl_ref, cw_ref, cb_ref, wr_ref, br_ref, wi_ref, bi_ref, lam_ref, do_ref,
             dxl_ref, dyl_ref, dcw_ref, dcb_ref, dwr_ref, dbr_ref, dwi_ref, dbi_ref, dlam_ref,
             a_s, b_s, h_s, g_s):
        xl = xl_ref[...]
        cwv = cw_ref[...]
        lam = lam_ref[...]
        u, r, i, sp, a, mult = _lru_gates(xl, cwv, cb_ref[...], wr_ref[0], br_ref[...], wi_ref[0], bi_ref[...], lam)
        v = i * u
        a_s[...] = a
        b_s[...] = mult * v
        _scan_tiles(a_s, b_s, h_s, t, reverse=False)
        gl, dgl = _gelu_parts(yl_ref[...])
        dov = do_ref[...]
        h = h_s[...]
        dyl_ref[...] = (dov * h * dgl).astype(BF16)
        b_s[...] = dov * gl
        a_s[...] = _shift_up(a, 1)
        _scan_tiles(a_s, b_s, g_s, t, reverse=True)
        g = g_s[...]
        da = g * _shift_down(h, 1)
        dmult = g * v
        dv = g * mult
        dla = da * a - dmult * (a * a) / mult
        dr = dla * (-LRU_C * sp)
        dsp = jnp.sum(dla * (-LRU_C * r), axis=0, keepdims=True)
        dlam_ref[...] = -dsp * _sigmoid(-lam)
        dpr = dr * r * (1.0 - r)
        dpi = dv * u * i * (1.0 - i)
        dbr_ref[...] = jnp.sum(dpr, axis=0, keepdims=True)
        dbi_ref[...] = jnp.sum(dpi, axis=0, keepdims=True)
        dwr_ref[0] = _bdot(u, dpr, TN)
        dwi_ref[0] = _bdot(u, dpi, TN)
        du = dv * i + _bdot(dpr, wr_ref[0], NT) + _bdot(dpi, wi_ref[0], NT)
        dxl = cwv[CONV_K - 1:CONV_K, :] * du
        for k in range(CONV_K - 1):
            dxl = dxl + cwv[k:k + 1, :] * _shift_up(du, CONV_K - 1 - k)
        dxl_ref[...] = dxl.astype(BF16)
        for k in range(CONV_K):
            dcw_ref[k:k + 1, :] = jnp.sum(du * _shift_down(xl, CONV_K - 1 - k), axis=0, keepdims=True)
        dcb_ref[...] = jnp.sum(du, axis=0, keepdims=True)

    return pl.pallas_call(
        body, name="lru_bwd", grid=(LRU_BLOCKS,),
        in_specs=[xl_s, yl_s, cw_s, vec, wblk, vec, wblk, vec, vec, col],
        out_specs=[col, col, cw_s, vec, wblk, vec, wblk, vec, vec],
        out_shape=[jax.ShapeDtypeStruct((t, LRU_WIDTH), BF16), jax.ShapeDtypeStruct((t, LRU_WIDTH), BF16),
                   jax.ShapeDtypeStruct((CONV_K, LRU_WIDTH), F32), jax.ShapeDtypeStruct((1, LRU_WIDTH), F32),
                   jax.ShapeDtypeStruct((LRU_BLOCKS, LANES, LANES), F32), jax.ShapeDtypeStruct((1, LRU_WIDTH), F32),
                   jax.ShapeDtypeStruct((LRU_BLOCKS, LANES, LANES), F32), jax.ShapeDtypeStruct((1, LRU_WIDTH), F32),
                   jax.ShapeDtypeStruct((1, LRU_WIDTH), F32)],
        scratch_shapes=[pltpu.VMEM((t, LANES), F32)] * 4,
        compiler_params=_params(("arbitrary",), big=True),
    )(proj, proj, cw, cb, wr, br, wi, bi, lam, dout)


def _mesh_pos():
    return lax.axis_index("x"), lax.axis_index("y"), lax.axis_index("c")


HBM = pl.BlockSpec(memory_space=pl.ANY)


def _comm_scratch(n):
    return [pltpu.SemaphoreType.DMA((n, 7)), pltpu.SemaphoreType.DMA((n, 7)), pltpu.SemaphoreType.DMA((n,))]


def _gather_phase(phase, v_refs, out_refs, send_sems, recv_sems, local_sems):
    n = len(v_refs)
    x, y, c = _mesh_pos()
    me, sibling = (x, y, c), (x, y, 1 - c)
    chips = [(1 - x, y), (x, 1 - y), (1 - x, 1 - y)]

    def block(a, px, py, pc):
        return out_refs[a].at[4 * px + 2 * py + pc]

    def copy(a, k, blk, to, src=None):
        return pltpu.make_async_remote_copy(
            src_ref=block(a, *blk) if src is None else src, dst_ref=block(a, *blk),
            send_sem=send_sems.at[a, k], recv_sem=recv_sems.at[a, k], device_id=to, device_id_type=MESH)

    def own(a):
        return pltpu.make_async_copy(v_refs[a], block(a, *me), local_sems.at[a])

    def first(a):
        return ([copy(a, 0, me, sibling, src=v_refs[a])]
                + [copy(a, 1 + j, me, (*chip, c), src=v_refs[a]) for j, chip in enumerate(chips)])

    def forward(a, j):
        return copy(a, 4 + j, (*chips[j], c), sibling)

    if phase == "start":
        for a in range(n):
            own(a).start()
        for a in range(n):
            for cp in first(a):
                cp.start()
    elif phase == "forward":
        for j in range(3):
            for a in range(n):
                copy(a, 1 + j, (*chips[j], c), me).wait_recv()
                forward(a, j).start()
    else:
        for a in range(n):
            copy(a, 0, sibling, me).wait_recv()
            for j in range(3):
                copy(a, 4 + j, (*chips[j], 1 - c), me).wait_recv()
        for a in range(n):
            for cp in first(a) + [forward(a, j) for j in range(3)]:
                cp.wait_send()
            own(a).wait()


def _all_gather(vs, name):
    n = len(vs)

    def body(*refs):
        comm = (refs[:n], refs[n:2 * n]) + tuple(refs[2 * n:])
        for phase in ("start", "forward", "finish"):
            _gather_phase(phase, *comm)

    return pl.pallas_call(
        body, name=name,
        out_shape=[jax.ShapeDtypeStruct((N_DEV,) + v.shape, v.dtype) for v in vs],
        in_specs=[HBM] * n, out_specs=[HBM] * n, scratch_shapes=_comm_scratch(n),
    )(*vs)


def _exchange(parts, name):
    n = len(parts)

    def body(*refs):
        comm = (refs[:n], refs[n:2 * n]) + tuple(refs[2 * n:])
        _exchange_phase("start", *comm)
        _exchange_phase("finish", *comm)

    return pl.pallas_call(
        body, name=name,
        out_shape=[jax.ShapeDtypeStruct(p.shape, p.dtype) for p in parts],
        in_specs=[HBM] * n, out_specs=[HBM] * n, scratch_shapes=_comm_scratch(n),
    )(*parts)


def _exchange_phase(phase, p_refs, out_refs, send_sems, recv_sems, local_sems):
    n = len(p_refs)
    x, y, c = _mesh_pos()
    me = 4 * x + 2 * y + c
    local = [pltpu.make_async_copy(p_refs[a].at[me], out_refs[a].at[me], local_sems.at[a]) for a in range(n)]
    remote = []
    for k in range(1, N_DEV):
        px = (1 - x) if k & 4 else x
        py = (1 - y) if k & 2 else y
        pc = (1 - c) if k & 1 else c
        for a in range(n):
            remote.append(pltpu.make_async_remote_copy(
                src_ref=p_refs[a].at[4 * px + 2 * py + pc], dst_ref=out_refs[a].at[me],
                send_sem=send_sems.at[a, k - 1], recv_sem=recv_sems.at[a, k - 1],
                device_id=(px, py, pc), device_id_type=MESH))
    if phase == "start":
        for cp in local + remote:
            cp.start()
    else:
        for cp in remote:
            cp.wait()
        for cp in local:
            cp.wait()


def _sum_sources(recv, tile, name):
    n, rows, width = recv.shape

    def body(r_ref, o_ref):
        acc = r_ref[0].astype(F32)
        for s in range(1, n):
            acc = acc + r_ref[s].astype(F32)
        o_ref[...] = acc

    return pl.pallas_call(
        body, name=name, grid=(rows // tile,),
        in_specs=[pl.BlockSpec((n, tile, width), lambda i: (0, i, 0))],
        out_specs=pl.BlockSpec((tile, width), lambda i: (i, 0)),
        out_shape=jax.ShapeDtypeStruct((rows, width), F32),
        compiler_params=_params(("parallel",)),
    )(recv)


def _row_tile(rows):
    for tile in range(128, 15, -16):
        if rows % tile == 0:
            return tile
    return rows


def _adamw(w, recv, m, v, name):
    rows, width = w.shape
    n = recv.shape[0]
    if rows % 16 == 0 or width % 256:
        tr, tc = _row_tile(rows), width
    else:
        tr, tc = rows, 256
    c1 = 1.0 - ADAM_B1 ** ADAM_STEP
    c2 = 1.0 - ADAM_B2 ** ADAM_STEP

    def body(w_ref, r_ref, m_ref, v_ref, g_ref, d_ref, nm_ref, nv_ref):
        gv = r_ref[0].astype(F32)
        for s in range(1, n):
            gv = gv + r_ref[s].astype(F32)
        nm = ADAM_B1 * m_ref[...] + (1.0 - ADAM_B1) * gv
        nv = ADAM_B2 * v_ref[...] + (1.0 - ADAM_B2) * (gv * gv)
        g_ref[...] = gv
        nm_ref[...] = nm
        nv_ref[...] = nv
        d_ref[...] = -ADAM_LR * ((nm / c1) / (jnp.sqrt(nv / c2) + ADAM_EPS) + ADAM_WD * w_ref[...])

    spec = pl.BlockSpec((tr, tc), lambda i, j: (i, j))
    shape = jax.ShapeDtypeStruct((rows, width), F32)
    return pl.pallas_call(
        body, name=name, grid=(rows // tr, width // tc),
        in_specs=[spec, pl.BlockSpec((n, tr, tc), lambda i, j: (0, i, j)), spec, spec],
        out_specs=[spec] * 4, out_shape=[shape] * 4,
        compiler_params=_params(("parallel", "parallel")),
    )(w, recv, m, v)


BIG_NAMES = ("w_in", "w_out_ssm", "w_out_lru", "w_out", "w_ffn_in", "w_ffn_out", "ssm_conv_w", "lru_conv_w")
TRANSPOSED = ("w_in", "w_ffn_in")
CONV_NAMES = ("ssm_conv_w", "lru_conv_w")
MATMUL_NAMES = BIG_NAMES[:6]
NEEDED_FIRST = ("w_in", "ssm_conv_w", "lru_conv_w")
NEEDED_LATER = ("w_out_ssm", "w_out_lru", "w_out", "w_ffn_in", "w_ffn_out")
SMALL_NAMES = ("norm1_w", "b_branch_gate", "ssm_conv_b", "ssm_dt_bias", "ssm_a_log", "ssm_d", "ssm_norm_w",
               "lru_conv_b", "lru_w_r", "lru_b_r", "lru_w_i", "lru_b_i", "lru_lambda", "norm2_w", "norm_f_w")


def _pack(arrays, total):
    flat = []
    used = 0
    for a in arrays:
        a = a.reshape(-1)
        pad = (-a.shape[0]) % LANES
        flat.append(a)
        if pad:
            flat.append(jnp.zeros((pad,), a.dtype))
        used += a.shape[0] + pad
    assert used <= total, (used, total)
    if total > used:
        flat.append(jnp.zeros((total - used,), arrays[0].dtype))
    return jnp.concatenate(flat)


def _unpack(flat, shapes):
    out = []
    off = 0
    for shp in shapes:
        n = math.prod(shp)
        out.append(flat[off:off + n].reshape(shp))
        off += n + (-n) % LANES
    return out


def _col_shards(full):
    rows, cols = full.shape
    return full.reshape(rows, N_DEV, cols // N_DEV).transpose(1, 0, 2)


def _row_shards(full):
    rows, cols = full.shape
    return full.reshape(N_DEV, rows // N_DEV, cols)


def _from_col_shards(g):
    n, rows, w = g.shape
    return g.transpose(1, 0, 2).reshape(rows, n * w)


def kernel(x, norm1_w, w_in, b_branch_gate, ssm_conv_w, ssm_conv_b, ssm_dt_bias, ssm_a_log, ssm_d, ssm_norm_w, w_out_ssm, lru_conv_w, lru_conv_b, lru_w_r, lru_b_r, lru_w_i, lru_b_i, lru_lambda, w_out_lru, w_out, norm2_w, w_ffn_in, w_ffn_out, norm_f_w, loss_target, m_norm1_w, m_w_in, m_b_branch_gate, m_ssm_conv_w, m_ssm_conv_b, m_ssm_dt_bias, m_ssm_a_log, m_ssm_d, m_ssm_norm_w, m_w_out_ssm, m_lru_conv_w, m_lru_conv_b, m_lru_w_r, m_lru_b_r, m_lru_w_i, m_lru_b_i, m_lru_lambda, m_w_out_lru, m_w_out, m_norm2_w, m_w_ffn_in, m_w_ffn_out, m_norm_f_w, v_norm1_w, v_w_in, v_b_branch_gate, v_ssm_conv_w, v_ssm_conv_b, v_ssm_dt_bias, v_ssm_a_log, v_ssm_d, v_ssm_norm_w, v_w_out_ssm, v_lru_conv_w, v_lru_conv_b, v_lru_w_r, v_lru_b_r, v_lru_w_i, v_lru_b_i, v_lru_lambda, v_w_out_lru, v_w_out, v_norm2_w, v_w_ffn_in, v_w_ffn_out, v_norm_f_w):
    given = dict(locals())
    weights = {n: given[n] for n in BIG_NAMES + SMALL_NAMES}
    t = x.shape[1]
    xt = x[0]
    tgt = loss_target[0]

    def local(n, a):
        return a[0].T if n in TRANSPOSED else a[0]

    def as_output(n, a):
        return a.T[None] if n in TRANSPOSED else a[None]

    def shard(n):
        s = local(n, weights[n])
        return s.astype(BF16) if n in MATMUL_NAMES else s

    def unshard(n, g):
        return _from_col_shards(g) if n in CONV_NAMES else g.reshape(-1, g.shape[-1])

    def grad_slices(n):
        g = grads[n]
        return (_col_shards(g) if n in CONV_NAMES else g.reshape(N_DEV, -1, g.shape[-1])).astype(BF16)

    gathered = _all_gather([shard(n) for n in NEEDED_FIRST], "gather_in_weights")
    full = {n: unshard(n, g) for n, g in zip(NEEDED_FIRST, gathered)}
    ssm_cw, lru_cw = full["ssm_conv_w"], full["lru_conv_w"]
    wi_t = full["w_in"]
    w_pt = jnp.concatenate([wi_t[:ORIG_DT], wi_t[ORIG_LRU:], wi_t[ORIG_DT:ORIG_LRU],
                            jnp.zeros((PROJ_W - IN_PROJ, D_MODEL), BF16)], axis=0)

    def pad_heads(a):
        return jnp.pad(a.reshape(1, SSM_HEADS), ((0, 0), (0, LANES - SSM_HEADS)))

    dt_bias_p = pad_heads(ssm_dt_bias)
    a_log_p = pad_heads(ssm_a_log)
    d_exp = jnp.repeat(ssm_d.reshape(SSM_HEADS), SSM_HEAD_DIM).reshape(1, SSM_INNER)
    lru_wr, lru_wi = lru_w_r[0], lru_w_i[0]

    hn1 = _rmsnorm_fwd(xt, norm1_w, "norm1_fwd")
    proj = _mm(hn1, w_pt, tb=True, name="in_proj")
    xbc_act = _ssm_conv_fwd(proj, ssm_cw, ssm_conv_b)
    y_ssd, s_in_all = _ssd_fwd(xbc_act, proj, dt_bias_p, a_log_p, d_exp)
    l_out, gathered = _lru_fwd(proj, lru_cw, lru_conv_b, lru_wr, lru_b_r, lru_wi, lru_b_i, lru_lambda,
                               gather=[shard(n) for n in NEEDED_LATER])
    full.update({n: unshard(n, g) for n, g in zip(NEEDED_LATER, gathered)})
    y_pre = _ssm_post_fwd(y_ssd, proj, ssm_norm_w)
    y_ssm = _mm(y_pre, full["w_out_ssm"], name="out_ssm")
    y_lru = _mm(l_out, full["w_out_lru"], name="out_lru")
    merged = _merge_fwd(proj, b_branch_gate, y_ssm, y_lru)
    h1 = _mm(merged, full["w_out"], add=xt, name="out_proj")
    hn2 = _rmsnorm_fwd(h1, norm2_w, "norm2_fwd")
    gu = _mm(hn2, full["w_ffn_in"], tb=True, name="ffn_in")
    act = _swiglu_fwd(gu)
    h2 = _mm(act, full["w_ffn_out"], add=h1, name="ffn_out")

    grads = {}
    dh2, grads["norm_f_w"], loss_cols = _loss_head(h2, norm_f_w.reshape(1, D_MODEL), tgt)
    loss = lax.psum(0.5 * jnp.sum(loss_cols) / D_MODEL, AXES)
    dact = _mm(dh2, full["w_ffn_out"], tb=True, name="d_act")
    grads["w_ffn_out"] = _mm(act, dh2, ta=True, out_dtype=BF16, name="dw_ffn_out")
    dgu = _swiglu_bwd(gu, dact)
    dhn2 = _mm(dgu, full["w_ffn_in"], name="d_hn2")
    grads["w_ffn_in"] = _mm(dgu, hn2, ta=True, out_dtype=BF16, name="dw_ffn_in")
    dh1, grads["norm2_w"] = _rmsnorm_bwd(h1, norm2_w, dhn2, dh2, "norm2_bwd")
    dmerged = _mm(dh1, full["w_out"], tb=True, name="d_merged")
    grads["w_out"] = _mm(merged, dh1, ta=True, out_dtype=BF16, name="dw_out")
    dy_ssm, dy_lru, dgates, grads["b_branch_gate"] = _merge_bwd(proj, b_branch_gate, y_ssm, y_lru, dmerged)
    dy_pre = _mm(dy_ssm, full["w_out_ssm"], tb=True, name="d_y_pre")
    grads["w_out_ssm"] = _mm(y_pre, dy_ssm, ta=True, out_dtype=BF16, name="dw_out_ssm")
    dl_out = _mm(dy_lru, full["w_out_lru"], tb=True, name="d_l_out")
    grads["w_out_lru"] = _mm(l_out, dy_lru, ta=True, out_dtype=BF16, name="dw_out_lru")
    dy_ssd, dz, grads["ssm_norm_w"] = _ssm_post_bwd(y_ssd, proj, ssm_norm_w, dy_pre)
    (dxbc_act, ddt, dbias, dalog, ddcol), recv_later = _ssd_bwd(
        xbc_act, proj, s_in_all, dy_ssd, dt_bias_p, a_log_p, d_exp, exchange=[grad_slices(n) for n in NEEDED_LATER])
    grads["ssm_dt_bias"] = dbias[:, :SSM_HEADS]
    grads["ssm_a_log"] = dalog[:, :SSM_HEADS]
    grads["ssm_d"] = ddcol.reshape(SSM_HEADS, SSM_HEAD_DIM).sum(axis=1).reshape(1, SSM_HEADS)
    dxbc, grads["ssm_conv_w"], grads["ssm_conv_b"] = _ssm_conv_bwd(proj, ssm_cw, ssm_conv_b, dxbc_act)
    (dxl, dyl, grads["lru_conv_w"], grads["lru_conv_b"], dwr, grads["lru_b_r"], dwi, grads["lru_b_i"],
     grads["lru_lambda"]) = _lru_bwd(proj, lru_cw, lru_conv_b, lru_wr, lru_b_r, lru_wi, lru_b_i, lru_lambda, dl_out)
    grads["lru_w_r"], grads["lru_w_i"] = dwr[None], dwi[None]
    dproj = jnp.concatenate([dgates, dz, dxbc, dxl, dyl, ddt], axis=1)
    dwpt = _mm(dproj, hn1, ta=True, out_dtype=BF16, name="dw_in")
    grads["w_in"] = jnp.concatenate([dwpt[:ORIG_DT], dwpt[COL_DT:COL_DT + SSM_HEADS], dwpt[COL_LRU_X:COL_DT]], axis=0)
    dhn1, recv_first = _mm(dproj, w_pt, exchange=[grad_slices(n) for n in NEEDED_FIRST], name="d_hn1")
    grad_x, grads["norm1_w"] = _rmsnorm_bwd(xt, norm1_w, dhn1, dh1, "norm1_bwd")

    small_total = N_DEV * SMALL_ROWS * LANES
    small_parts = _pack([grads[n] for n in SMALL_NAMES], small_total).reshape(N_DEV, SMALL_ROWS, LANES)
    small_g = _sum_sources(_exchange([small_parts], "exchange_small_grads")[0], SMALL_ROWS, "sum_small_grads")
    small_g = _all_gather([small_g], "gather_small_grads")[0].reshape(1, small_total // FLAT_W, FLAT_W)

    recv = dict(zip(NEEDED_LATER + NEEDED_FIRST, list(recv_later) + list(recv_first)))
    big_out = {n: _adamw(local(n, weights[n]), recv[n], local(n, given["m_" + n]), local(n, given["v_" + n]),
                         "adamw_" + n) for n in BIG_NAMES}
    small_state = [_pack([given[p + n] for n in SMALL_NAMES], small_total).reshape(-1, FLAT_W) for p in ("", "m_", "v_")]
    small_out = _adamw(small_state[0], small_g, small_state[1], small_state[2], "adamw_replicated")

    small_shapes = [weights[n].shape for n in SMALL_NAMES]
    order = list(given)[1:24]
    results = []
    for q in range(4):
        vals = {n: as_output(n, big_out[n][q]) for n in BIG_NAMES}
        vals.update(zip(SMALL_NAMES, _unpack(small_out[q].reshape(-1), small_shapes)))
        results.extend(vals[n] for n in order)
    return (loss, grad_x[None], *results)
```

```python
import math

import jax
import jax.numpy as jnp
from jax import lax
from jax.experimental import pallas as pl
from jax.experimental.pallas import tpu as pltpu

F32 = jnp.float32
BF16 = jnp.bfloat16
HIGHEST = lax.Precision.HIGHEST
MESH = pl.DeviceIdType.MESH
AXES = ("x", "y", "c")
N_DEV = 8

D_MODEL = 1024
SSM_INNER = 2048
SSM_HEADS = 32
SSM_HEAD_DIM = 64
SSM_GROUPS = 4
SSM_STATE = 128
SSM_BC = SSM_GROUPS * SSM_STATE
SSM_CONV_DIM = SSM_INNER + 2 * SSM_BC
SSM_CHUNK = 128
SSM_PAIRS = SSM_HEADS // 2
CONV_K = 4
LRU_WIDTH = 1280
LRU_BLOCKS = 10
LRU_C = 8.0
FFN_HIDDEN = 2816
RMS_EPS = 1e-6
IN_PROJ = 9760

COL_GATES = 0
COL_Z = 2048
COL_XBC = 4096
COL_LRU_X = 7168
COL_LRU_Y = 8448
COL_DT = 9728
PROJ_W = 9856
ORIG_DT = 7168
ORIG_LRU = 7200

ADAM_LR = 0.001
ADAM_B1 = 0.9
ADAM_B2 = 0.999
ADAM_EPS = 1e-08
ADAM_WD = 0.01
ADAM_STEP = 10

LANES = 128
SUBLANES = 8
V7X_VMEM_BYTES = 64 * 1024 * 1024
VMEM_LIMIT = V7X_VMEM_BYTES * 3 // 4
VMEM_LIMIT_BIG = V7X_VMEM_BYTES * 15 // 16

NT = (((1,), (1,)), ((), ()))
TN = (((0,), (0,)), ((), ()))

FLAT_W = 1024
SMALL_ROWS = 336


def _params(sem=None, big=False):
    return pltpu.CompilerParams(dimension_semantics=sem,
                                vmem_limit_bytes=VMEM_LIMIT_BIG if big else VMEM_LIMIT)


def _blk(dim, cap):
    if dim <= cap:
        return dim
    for m in range(cap // LANES, 0, -1):
        if dim % (m * LANES) == 0:
            return m * LANES
    raise ValueError(f"no block for {dim}")


def _rows(t):
    return min(t, 256)


def _sigmoid(v):
    return 1.0 / (1.0 + jnp.exp(-v))


def _softplus(v):
    e = jnp.exp(-jnp.abs(v))
    u = 1.0 + e
    log1p = jnp.where(u == 1.0, e, jnp.log(u) * e / jnp.where(u == 1.0, 1.0, u - 1.0))
    return jnp.maximum(v, 0.0) + log1p


def _iota(shape, dim):
    return lax.broadcasted_iota(jnp.int32, shape, dim)


def _shift_down(v, s):
    if s == 0:
        return v
    return jnp.where(_iota(v.shape, 0) >= s, pltpu.roll(v, s, 0), 0.0)


def _shift_up(v, s):
    if s == 0:
        return v
    n = v.shape[0]
    return jnp.where(_iota(v.shape, 0) < n - s, pltpu.roll(v, n - s, 0), 0.0)


def _bdot(a, b, dn=None):
    a = a.astype(BF16)
    b = b.astype(BF16)
    if dn is None:
        return jnp.dot(a, b, preferred_element_type=F32)
    return lax.dot_general(a, b, dn, preferred_element_type=F32)


def _split_dot(a, e, dn=None):
    hi = a.astype(BF16)
    lo = (a - hi.astype(F32)).astype(BF16)
    return _bdot(hi, e, dn) + _bdot(lo, e, dn)


def _fdot(a, b, dn=None):
    if dn is None:
        return jnp.dot(a, b, precision=HIGHEST, preferred_element_type=F32)
    return lax.dot_general(a, b, dn, precision=HIGHEST, preferred_element_type=F32)


def _mm(a, b, *, ta=False, tb=False, add=None, exchange=(), out_dtype=F32, name):
    if ta:
        kdim, m = a.shape
    else:
        m, kdim = a.shape
    if tb:
        n, k2 = b.shape
    else:
        k2, n = b.shape
    assert kdim == k2, (a.shape, b.shape, ta, tb)
    bm, bn, bk = _blk(m, 1024), _blk(n, 1408), _blk(kdim, 1408)
    grid = (m // bm, n // bn, kdim // bk)
    nk = grid[2]
    dn = (((0 if ta else 1,), (1 if tb else 0,)), ((), ()))
    n_in = 2 if add is None else 3
    n_ex = len(exchange)

    def body(*refs):
        a_ref, b_ref = refs[:2]
        r_ref = None if add is None else refs[2]
        o_ref = refs[n_in + n_ex]
        acc = refs[n_in + 2 * n_ex + 1]
        comm = (refs[n_in:n_in + n_ex], refs[n_in + n_ex + 1:n_in + 2 * n_ex + 1]) + tuple(refs[n_in + 2 * n_ex + 2:])
        step = (pl.program_id(0) * grid[1] + pl.program_id(1)) * nk + pl.program_id(2)
        k = pl.program_id(2)
        if n_ex:
            @pl.when(step == 0)
            def _():
                _exchange_phase("start", *comm)

        @pl.when(k == 0)
        def _():
            acc[...] = jnp.zeros_like(acc)

        acc[...] += lax.dot_general(a_ref[...].astype(BF16), b_ref[...].astype(BF16), dn,
                                    preferred_element_type=F32)

        @pl.when(k == nk - 1)
        def _():
            r = acc[...]
            if add is not None:
                r = r + r_ref[...]
            o_ref[...] = r.astype(out_dtype)

        if n_ex:
            @pl.when(step == grid[0] * grid[1] * nk - 1)
            def _():
                _exchange_phase("finish", *comm)

    a_spec = pl.BlockSpec((bk, bm), lambda i, j, k: (k, i)) if ta else pl.BlockSpec((bm, bk), lambda i, j, k: (i, k))
    b_spec = pl.BlockSpec((bn, bk), lambda i, j, k: (j, k)) if tb else pl.BlockSpec((bk, bn), lambda i, j, k: (k, j))
    o_spec = pl.BlockSpec((bm, bn), lambda i, j, k: (i, j))
    in_specs = [a_spec, b_spec] + ([o_spec] if add is not None else []) + [HBM] * n_ex
    args = (a, b) + ((add,) if add is not None else ()) + tuple(exchange)
    out = pl.pallas_call(
        body, name=name, grid=grid,
        in_specs=in_specs, out_specs=[o_spec] + [HBM] * n_ex,
        out_shape=[jax.ShapeDtypeStruct((m, n), out_dtype)]
        + [jax.ShapeDtypeStruct(p.shape, p.dtype) for p in exchange],
        scratch_shapes=[pltpu.VMEM((bm, bn), F32)] + (_comm_scratch(n_ex) if n_ex else []),
        compiler_params=_params(("arbitrary",) * 3 if n_ex else ("parallel", "parallel", "arbitrary")),
    )(*args)
    return (out[0], out[1:]) if n_ex else out[0]


def _rmsnorm_fwd(x, w, name):
    t, d = x.shape
    tr = _rows(t)

    def body(x_ref, w_ref, o_ref):
        xv = x_ref[...]
        rstd = lax.rsqrt(jnp.mean(xv * xv, axis=-1, keepdims=True) + RMS_EPS)
        o_ref[...] = (xv * rstd * w_ref[...]).astype(BF16)

    return pl.pallas_call(
        body, name=name, grid=(t // tr,),
        in_specs=[pl.BlockSpec((tr, d), lambda i: (i, 0)), pl.BlockSpec((1, d), lambda i: (0, 0))],
        out_specs=pl.BlockSpec((tr, d), lambda i: (i, 0)),
        out_shape=jax.ShapeDtypeStruct((t, d), BF16),
        compiler_params=_params(("parallel",)),
    )(x, w)


def _rmsnorm_bwd(x, w, dy, dres, name):
    t, d = x.shape
    tr = _rows(t)

    def body(x_ref, w_ref, dy_ref, dres_ref, dx_ref, dw_ref):
        i = pl.program_id(0)
        xv = x_ref[...]
        rstd = lax.rsqrt(jnp.mean(xv * xv, axis=-1, keepdims=True) + RMS_EPS)
        xhat = xv * rstd
        dyv = dy_ref[...]
        dxhat = dyv * w_ref[...]
        m = jnp.mean(dxhat * xhat, axis=-1, keepdims=True)
        dx_ref[...] = rstd * (dxhat - xhat * m) + dres_ref[...]
        part = jnp.sum(dyv * xhat, axis=0, keepdims=True)

        @pl.when(i == 0)
        def _():
            dw_ref[...] = part

        @pl.when(i > 0)
        def _():
            dw_ref[...] += part

    row = pl.BlockSpec((tr, d), lambda i: (i, 0))
    vec = pl.BlockSpec((1, d), lambda i: (0, 0))
    return pl.pallas_call(
        body, name=name, grid=(t // tr,),
        in_specs=[row, vec, row, row], out_specs=[row, vec],
        out_shape=[jax.ShapeDtypeStruct((t, d), F32), jax.ShapeDtypeStruct((1, d), F32)],
        compiler_params=_params(("arbitrary",)),
    )(x, w, dy, dres)


def _loss_head(h2, w, tgt):
    t, d = h2.shape
    tr = _rows(t)

    def body(x_ref, w_ref, t_ref, dx_ref, dw_ref, ls_ref):
        i = pl.program_id(0)
        xv = x_ref[...]
        wv = w_ref[...]
        rstd = lax.rsqrt(jnp.mean(xv * xv, axis=-1, keepdims=True) + RMS_EPS)
        xhat = xv * rstd
        err = xhat * wv - t_ref[...]
        dyv = err * (1.0 / d)
        dxhat = dyv * wv
        m = jnp.mean(dxhat * xhat, axis=-1, keepdims=True)
        dx_ref[...] = rstd * (dxhat - xhat * m)
        dw_part = jnp.sum(dyv * xhat, axis=0, keepdims=True)
        ls_part = jnp.sum(err * err, axis=0, keepdims=True)

        @pl.when(i == 0)
        def _():
            dw_ref[...] = dw_part
            ls_ref[...] = ls_part

        @pl.when(i > 0)
        def _():
            dw_ref[...] += dw_part
            ls_ref[...] += ls_part

    row = pl.BlockSpec((tr, d), lambda i: (i, 0))
    vec = pl.BlockSpec((1, d), lambda i: (0, 0))
    return pl.pallas_call(
        body, name="loss_head", grid=(t // tr,),
        in_specs=[row, vec, row], out_specs=[row, vec, vec],
        out_shape=[jax.ShapeDtypeStruct((t, d), F32), jax.ShapeDtypeStruct((1, d), F32),
                   jax.ShapeDtypeStruct((1, d), F32)],
        compiler_params=_params(("arbitrary",)),
    )(h2, w, tgt)


def _merge_fwd(proj, bg, ys, yl):
    t = proj.shape[0]
    d = D_MODEL
    tr = _rows(t)

    def body(ps_ref, pl_ref, bg_ref, ys_ref, yl_ref, o_ref):
        gs = _sigmoid(ps_ref[...] + bg_ref[:, 0:d])
        gl = _sigmoid(pl_ref[...] + bg_ref[:, d:2 * d])
        o_ref[...] = (gs * ys_ref[...] + gl * yl_ref[...]).astype(BF16)

    row = pl.BlockSpec((tr, d), lambda i: (i, 0))
    return pl.pallas_call(
        body, name="merge_fwd", grid=(t // tr,),
        in_specs=[row, pl.BlockSpec((tr, d), lambda i: (i, 1)), pl.BlockSpec((1, 2 * d), lambda i: (0, 0)), row, row],
        out_specs=row, out_shape=jax.ShapeDtypeStruct((t, d), BF16),
        compiler_params=_params(("parallel",)),
    )(proj, proj, bg, ys, yl)


def _merge_bwd(proj, bg, ys, yl, dm):
    t = proj.shape[0]
    d = D_MODEL
    tr = _rows(t)

    def body(ps_ref, pl_ref, bg_ref, ys_ref, yl_ref, dm_ref, dys_ref, dyl_ref, dg_ref, dbg_ref):
        i = pl.program_id(0)
        gs = _sigmoid(ps_ref[...] + bg_ref[:, 0:d])
        gl = _sigmoid(pl_ref[...] + bg_ref[:, d:2 * d])
        dmv = dm_ref[...]
        dys_ref[...] = (dmv * gs).astype(BF16)
        dyl_ref[...] = (dmv * gl).astype(BF16)
        dgs = dmv * ys_ref[...] * gs * (1.0 - gs)
        dgl = dmv * yl_ref[...] * gl * (1.0 - gl)
        dg_ref[:, 0:d] = dgs.astype(BF16)
        dg_ref[:, d:2 * d] = dgl.astype(BF16)

        @pl.when(i == 0)
        def _():
            dbg_ref[...] = jnp.zeros_like(dbg_ref)

        dbg_ref[:, 0:d] += jnp.sum(dgs, axis=0, keepdims=True)
        dbg_ref[:, d:2 * d] += jnp.sum(dgl, axis=0, keepdims=True)

    row = pl.BlockSpec((tr, d), lambda i: (i, 0))
    wide = pl.BlockSpec((tr, 2 * d), lambda i: (i, 0))
    vec = pl.BlockSpec((1, 2 * d), lambda i: (0, 0))
    return pl.pallas_call(
        body, name="merge_bwd", grid=(t // tr,),
        in_specs=[row, pl.BlockSpec((tr, d), lambda i: (i, 1)), vec, row, row, row],
        out_specs=[row, row, wide, vec],
        out_shape=[jax.ShapeDtypeStruct((t, d), BF16), jax.ShapeDtypeStruct((t, d), BF16),
                   jax.ShapeDtypeStruct((t, 2 * d), BF16), jax.ShapeDtypeStruct((1, 2 * d), F32)],
        compiler_params=_params(("arbitrary",)),
    )(proj, proj, bg, ys, yl, dm)


def _swiglu_fwd(gu):
    t = gu.shape[0]
    f = FFN_HIDDEN
    tr = _rows(t)

    def body(g_ref, u_ref, o_ref):
        g = g_ref[...]
        o_ref[...] = (g * _sigmoid(g) * u_ref[...]).astype(BF16)

    return pl.pallas_call(
        body, name="swiglu_fwd", grid=(t // tr,),
        in_specs=[pl.BlockSpec((tr, f), lambda i: (i, 0)), pl.BlockSpec((tr, f), lambda i: (i, 1))],
        out_specs=pl.BlockSpec((tr, f), lambda i: (i, 0)),
        out_shape=jax.ShapeDtypeStruct((t, f), BF16),
        compiler_params=_params(("parallel",)),
    )(gu, gu)


def _swiglu_bwd(gu, dact):
    t = gu.shape[0]
    f = FFN_HIDDEN
    tr = _rows(t)

    def body(g_ref, u_ref, da_ref, o_ref):
        g = g_ref[...]
        sg = _sigmoid(g)
        da = da_ref[...]
        o_ref[:, 0:f] = (da * u_ref[...] * (sg * (1.0 + g * (1.0 - sg)))).astype(BF16)
        o_ref[:, f:2 * f] = (da * g * sg).astype(BF16)

    return pl.pallas_call(
        body, name="swiglu_bwd", grid=(t // tr,),
        in_specs=[pl.BlockSpec((tr, f), lambda i: (i, 0)), pl.BlockSpec((tr, f), lambda i: (i, 1)),
                  pl.BlockSpec((tr, f), lambda i: (i, 0))],
        out_specs=pl.BlockSpec((tr, 2 * f), lambda i: (i, 0)),
        out_shape=jax.ShapeDtypeStruct((t, 2 * f), BF16),
        compiler_params=_params(("parallel",)),
    )(gu, gu, dact)


def _conv_pre(xv, wv, bv):
    pre = bv + wv[CONV_K - 1:CONV_K, :] * xv
    for k in range(CONV_K - 1):
        pre = pre + wv[k:k + 1, :] * _shift_down(xv, CONV_K - 1 - k)
    return pre


def _ssm_conv_fwd(proj, w, b):
    t = proj.shape[0]
    nb = SSM_CONV_DIM // LANES
    c0 = COL_XBC // LANES

    def body(x_ref, w_ref, b_ref, o_ref):
        pre = _conv_pre(x_ref[...], w_ref[...], b_ref[...])
        o_ref[...] = pre * _sigmoid(pre)

    return pl.pallas_call(
        body, name="ssm_conv_fwd", grid=(nb,),
        in_specs=[pl.BlockSpec((t, LANES), lambda j: (0, c0 + j)), pl.BlockSpec((CONV_K, LANES), lambda j: (0, j)),
                  pl.BlockSpec((1, LANES), lambda j: (0, j))],
        out_specs=pl.BlockSpec((t, LANES), lambda j: (0, j)),
        out_shape=jax.ShapeDtypeStruct((t, SSM_CONV_DIM), F32),
        compiler_params=_params(("parallel",)),
    )(proj, w, b)


def _ssm_conv_bwd(proj, w, b, dact):
    t = proj.shape[0]
    nb = SSM_CONV_DIM // LANES
    c0 = COL_XBC // LANES

    def body(x_ref, w_ref, b_ref, da_ref, dx_ref, dw_ref, db_ref):
        xv = x_ref[...]
        wv = w_ref[...]
        pre = _conv_pre(xv, wv, b_ref[...])
        sg = _sigmoid(pre)
        dpre = da_ref[...] * (sg * (1.0 + pre * (1.0 - sg)))
        dx = wv[CONV_K - 1:CONV_K, :] * dpre
        for k in range(CONV_K - 1):
            dx = dx + wv[k:k + 1, :] * _shift_up(dpre, CONV_K - 1 - k)
        dx_ref[...] = dx.astype(BF16)
        for k in range(CONV_K):
            dw_ref[k:k + 1, :] = jnp.sum(dpre * _shift_down(xv, CONV_K - 1 - k), axis=0, keepdims=True)
        db_ref[...] = jnp.sum(dpre, axis=0, keepdims=True)

    col = pl.BlockSpec((t, LANES), lambda j: (0, j))
    wsp = pl.BlockSpec((CONV_K, LANES), lambda j: (0, j))
    bsp = pl.BlockSpec((1, LANES), lambda j: (0, j))
    return pl.pallas_call(
        body, name="ssm_conv_bwd", grid=(nb,),
        in_specs=[pl.BlockSpec((t, LANES), lambda j: (0, c0 + j)), wsp, bsp, col],
        out_specs=[col, wsp, bsp],
        out_shape=[jax.ShapeDtypeStruct((t, SSM_CONV_DIM), BF16), jax.ShapeDtypeStruct((CONV_K, SSM_CONV_DIM), F32),
                   jax.ShapeDtypeStruct((1, SSM_CONV_DIM), F32)],
        compiler_params=_params(("parallel",)),
    )(proj, w, b, dact)


def _ssd_chunk_terms(dtr, bias, alog):
    a = -jnp.exp(alog)
    dt = _softplus(dtr + bias)
    row = _iota((SSM_CHUNK, SSM_CHUNK), 0)
    col = _iota((SSM_CHUNK, SSM_CHUNK), 1)
    tri = (row >= col).astype(F32)
    cs = _fdot(tri, dt * a)
    dec = jnp.exp(cs[SSM_CHUNK - 1:SSM_CHUNK, :] - cs)
    ecs = jnp.exp(cs)
    off = _iota((LANES, SSM_INNER), 1) - SSM_HEAD_DIM * _iota((LANES, SSM_INNER), 0)
    expand = jnp.where(jnp.logical_and(off >= 0, off < SSM_HEAD_DIM), 1.0, 0.0).astype(BF16)
    return a, dt, cs, dec, ecs, expand, row, col


def _ssd_specs(t):
    nc = t // SSM_CHUNK
    xs = pl.BlockSpec((SSM_CHUNK, SSM_INNER), lambda c: (c, 0))
    bm = pl.BlockSpec((SSM_CHUNK, SSM_BC), lambda c: (c, SSM_INNER // SSM_BC))
    cm = pl.BlockSpec((SSM_CHUNK, SSM_BC), lambda c: (c, SSM_INNER // SSM_BC + 1))
    dtr = pl.BlockSpec((SSM_CHUNK, LANES), lambda c: (c, COL_DT // LANES))
    vec = pl.BlockSpec((1, LANES), lambda c: (0, 0))
    wide = pl.BlockSpec((1, SSM_INNER), lambda c: (0, 0))
    return nc, xs, bm, cm, dtr, vec, wide


def _ssd_fwd(xbc_act, proj, bias, alog, dexp, gather):
    t = proj.shape[0]
    nc, xs_s, bm_s, cm_s, dtr_s, vec, wide = _ssd_specs(t)
    n = len(gather)

    def body(*refs):
        xs_ref, b_ref, c_ref, dtr_ref, bias_ref, alog_ref, dexp_ref = refs[:7]
        y_ref, sin_ref = refs[7 + n:9 + n]
        state = refs[9 + 2 * n]
        comm = (refs[7:7 + n], refs[9 + n:9 + 2 * n]) + tuple(refs[10 + 2 * n:])
        chunk = pl.program_id(0)

        @pl.when(chunk == 0)
        def _():
            _gather_phase("start", *comm)
            state[...] = jnp.zeros_like(state)

        @pl.when(chunk == nc // 2)
        def _():
            _gather_phase("forward", *comm)

        a, dt, cs, dec, ecs, expand, row, col = _ssd_chunk_terms(dtr_ref[...], bias_ref[...], alog_ref[...])
        cst = cs.T
        dt_x = _split_dot(dt, expand)
        dec_x = _split_dot(dec, expand)
        ecs_x = _split_dot(ecs, expand)
        xs = xs_ref[...]
        xdt = xs * dt_x
        xdec = xdt * dec_x
        lane_lo = col < SSM_HEAD_DIM
        causal = row >= col
        sin_ref[0] = state[...]
        for g in range(SSM_GROUPS):
            bg = b_ref[:, g * SSM_STATE:(g + 1) * SSM_STATE].astype(BF16)
            cg = c_ref[:, g * SSM_STATE:(g + 1) * SSM_STATE].astype(BF16)
            cb = _bdot(cg, bg, NT)
            for q in range(SSM_PAIRS // SSM_GROUPS):
                pq = g * (SSM_PAIRS // SSM_GROUPS) + q
                sl = slice(pq * LANES, (pq + 1) * LANES)
                xp = xdt[:, sl].astype(BF16)
                yd = []
                for hh in range(2):
                    h = 2 * pq + hh
                    lmat = jnp.exp(jnp.where(causal, cs[:, h:h + 1] - cst[h:h + 1, :], -jnp.inf))
                    yd.append(_bdot(cb * lmat, xp))
                s_in = state[pq]
                y_off = _bdot(cg, s_in) * ecs_x[:, sl]
                y_ref[:, sl] = jnp.where(lane_lo, yd[0], yd[1]) + y_off + xs[:, sl] * dexp_ref[:, sl]
                state[pq] = s_in * ecs_x[SSM_CHUNK - 1:SSM_CHUNK, sl] + _bdot(bg, xdec[:, sl], TN)

        @pl.when(chunk == nc - 1)
        def _():
            _gather_phase("finish", *comm)

    out = pl.pallas_call(
        body, name="ssd_fwd", grid=(nc,),
        in_specs=[xs_s, bm_s, cm_s, dtr_s, vec, vec, wide] + [HBM] * n,
        out_specs=[pl.BlockSpec((SSM_CHUNK, SSM_INNER), lambda c: (c, 0)),
                   pl.BlockSpec((1, SSM_PAIRS, SSM_STATE, LANES), lambda c: (c, 0, 0, 0))] + [HBM] * n,
        out_shape=[jax.ShapeDtypeStruct((t, SSM_INNER), F32),
                   jax.ShapeDtypeStruct((nc, SSM_PAIRS, SSM_STATE, LANES), F32)]
        + [jax.ShapeDtypeStruct((N_DEV,) + v.shape, v.dtype) for v in gather],
        scratch_shapes=[pltpu.VMEM((SSM_PAIRS, SSM_STATE, LANES), F32)] + _comm_scratch(n),
        compiler_params=_params(("arbitrary",)),
    )(xbc_act, xbc_act, xbc_act, proj, bias, alog, dexp, *gather)
    return out[:2], out[2:]


def _ssd_bwd(xbc_act, proj, s_in_all, dy, bias, alog, dexp, exchange):
    n_ex = len(exchange)
    t = proj.shape[0]
    nc = t // SSM_CHUNK
    last = nc - 1
    xs_s = pl.BlockSpec((SSM_CHUNK, SSM_INNER), lambda c: (last - c, 0))
    bm_s = pl.BlockSpec((SSM_CHUNK, SSM_BC), lambda c: (last - c, SSM_INNER // SSM_BC))
    cm_s = pl.BlockSpec((SSM_CHUNK, SSM_BC), lambda c: (last - c, SSM_INNER // SSM_BC + 1))
    dtr_s = pl.BlockSpec((SSM_CHUNK, LANES), lambda c: (last - c, COL_DT // LANES))
    sin_s = pl.BlockSpec((1, SSM_PAIRS, SSM_STATE, LANES), lambda c: (last - c, 0, 0, 0))
    vec = pl.BlockSpec((1, LANES), lambda c: (0, 0))
    wide = pl.BlockSpec((1, SSM_INNER), lambda c: (0, 0))

    def body(*refs):
        xs_ref, b_ref, c_ref, dtr_ref, sin_ref, dy_ref, bias_ref, alog_ref, dexp_ref = refs[:9]
        dxbc_ref, ddtr_ref, dbias_ref, dalog_ref, ddcol_ref = refs[9 + n_ex:14 + n_ex]
        dstate, dxdt_s, yoff_s, rx_s, trow_s = refs[14 + 2 * n_ex:19 + 2 * n_ex]
        comm = (refs[9:9 + n_ex], refs[14 + n_ex:14 + 2 * n_ex]) + tuple(refs[19 + 2 * n_ex:])

        @pl.when(pl.program_id(0) == 0)
        def _():
            _exchange_phase("start", *comm)
            dstate[...] = jnp.zeros_like(dstate)
            trow_s[...] = jnp.zeros_like(trow_s)
            dbias_ref[...] = jnp.zeros_like(dbias_ref)
            dalog_ref[...] = jnp.zeros_like(dalog_ref)
            ddcol_ref[...] = jnp.zeros_like(ddcol_ref)

        dtr = dtr_ref[...]
        a, dt, cs, dec, ecs, expand, row, col = _ssd_chunk_terms(dtr, bias_ref[...], alog_ref[...])
        cst = cs.T
        dt_x = _split_dot(dt, expand)
        dec_x = _split_dot(dec, expand)
        ecs_x = _split_dot(ecs, expand)
        xs = xs_ref[...]
        dyv = dy_ref[...]
        xdt = xs * dt_x
        lane_lo = col < SSM_HEAD_DIM
        causal = row >= col
        ddcol_ref[...] += jnp.sum(dyv * xs, axis=0, keepdims=True)
        dcs_col = jnp.zeros((SSM_CHUNK, LANES), F32)
        dcs_row = jnp.zeros((LANES, SSM_CHUNK), F32)
        for g in range(SSM_GROUPS):
            bg = b_ref[:, g * SSM_STATE:(g + 1) * SSM_STATE].astype(BF16)
            cg = c_ref[:, g * SSM_STATE:(g + 1) * SSM_STATE].astype(BF16)
            cb = _bdot(cg, bg, NT)
            dgm = jnp.zeros((SSM_CHUNK, SSM_CHUNK), F32)
            dbg = jnp.zeros((SSM_CHUNK, SSM_STATE), F32)
            dcg = jnp.zeros((SSM_CHUNK, SSM_STATE), F32)
            for q in range(SSM_PAIRS // SSM_GROUPS):
                pq = g * (SSM_PAIRS // SSM_GROUPS) + q
                sl = slice(pq * LANES, (pq + 1) * LANES)
                dyp = dyv[:, sl]
                xp = xdt[:, sl]
                dxh = []
                for hh in range(2):
                    h = 2 * pq + hh
                    lmat = jnp.exp(jnp.where(causal, cs[:, h:h + 1] - cst[h:h + 1, :], -jnp.inf))
                    mmat = cb * lmat
                    dyh = jnp.where(lane_lo if hh == 0 else jnp.logical_not(lane_lo), dyp, 0.0)
                    dmm = _bdot(dyh, xp, NT)
                    pm = dmm * mmat
                    dcs_col = jnp.where(col == h, jnp.sum(pm, axis=1, keepdims=True), dcs_col)
                    dcs_row = jnp.where(row == h, jnp.sum(pm, axis=0, keepdims=True), dcs_row)
                    dgm = dgm + dmm * lmat
                    dxh.append(_bdot(mmat, dyp, TN))
                s_in = sin_ref[0, pq]
                ecs_p = ecs_x[:, sl]
                dec_p = dec_x[:, sl]
                etot_p = ecs_x[SSM_CHUNK - 1:SSM_CHUNK, sl]
                yoff_s[:, sl] = dyp * (_bdot(cg, s_in) * ecs_p)
                dq = dyp * ecs_p
                dcg = dcg + _bdot(dq, s_in, NT)
                ds = dstate[pq]
                r = _bdot(bg, ds)
                rx_s[:, sl] = r * xp
                dxdt_s[:, sl] = jnp.where(lane_lo, dxh[0], dxh[1]) + dec_p * r
                dbg = dbg + _bdot(xp * dec_p, ds, NT)
                trow_s[0:1, sl] = jnp.sum(ds * s_in, axis=0, keepdims=True) * etot_p
                dstate[pq] = etot_p * ds + _bdot(cg, dq, TN)
            dcg = dcg + _bdot(dgm, bg)
            dbg = dbg + _bdot(dgm, cg, TN)
            dxbc_ref[:, SSM_INNER + g * SSM_STATE:SSM_INNER + (g + 1) * SSM_STATE] = dbg
            dxbc_ref[:, SSM_INNER + SSM_BC + g * SSM_STATE:SSM_INNER + SSM_BC + (g + 1) * SSM_STATE] = dcg
        ddec = _split_dot(rx_s[...], expand, NT) * dec
        dtot = _split_dot(trow_s[...], expand, NT)[0:1, :]
        dcs = dcs_col - dcs_row.T + _split_dot(yoff_s[...], expand, NT) - ddec
        dcs = dcs + jnp.where(row == SSM_CHUNK - 1, jnp.sum(ddec, axis=0, keepdims=True) + dtot, 0.0)
        da = _fdot((row <= col).astype(F32), dcs)
        dxdt = dxdt_s[...]
        ddt = da * a + _split_dot(dxdt * xs, expand, NT)
        dalog_ref[...] += jnp.sum(da * dt, axis=0, keepdims=True) * a
        ddtr = ddt * _sigmoid(dtr + bias_ref[...])
        ddtr_ref[...] = ddtr.astype(BF16)
        dbias_ref[...] += jnp.sum(ddtr, axis=0, keepdims=True)
        dxbc_ref[:, 0:SSM_INNER] = dxdt * dt_x + dyv * dexp_ref[...]

        @pl.when(pl.program_id(0) == last)
        def _():
            _exchange_phase("finish", *comm)

    out = pl.pallas_call(
        body, name="ssd_bwd", grid=(nc,),
        in_specs=[xs_s, bm_s, cm_s, dtr_s, sin_s, pl.BlockSpec((SSM_CHUNK, SSM_INNER), lambda c: (last - c, 0)),
                  vec, vec, wide] + [HBM] * n_ex,
        out_specs=[pl.BlockSpec((SSM_CHUNK, SSM_CONV_DIM), lambda c: (last - c, 0)),
                   pl.BlockSpec((SSM_CHUNK, LANES), lambda c: (last - c, 0)), vec, vec, wide] + [HBM] * n_ex,
        out_shape=[jax.ShapeDtypeStruct((t, SSM_CONV_DIM), F32), jax.ShapeDtypeStruct((t, LANES), BF16),
                   jax.ShapeDtypeStruct((1, LANES), F32), jax.ShapeDtypeStruct((1, LANES), F32),
                   jax.ShapeDtypeStruct((1, SSM_INNER), F32)]
        + [jax.ShapeDtypeStruct(p.shape, p.dtype) for p in exchange],
        scratch_shapes=[pltpu.VMEM((SSM_PAIRS, SSM_STATE, LANES), F32),
                        pltpu.VMEM((SSM_CHUNK, SSM_INNER), F32), pltpu.VMEM((SSM_CHUNK, SSM_INNER), F32),
                        pltpu.VMEM((SSM_CHUNK, SSM_INNER), F32), pltpu.VMEM((SUBLANES, SSM_INNER), F32)]
        + _comm_scratch(n_ex),
        compiler_params=_params(("arbitrary",)),
    )(xbc_act, xbc_act, xbc_act, proj, s_in_all, dy, bias, alog, dexp, *exchange)
    return out[:5], out[5:]


def _group_rstd(y):
    n = SSM_INNER // SSM_GROUPS
    parts = []
    for g in range(SSM_GROUPS):
        yg = y[:, g * n:(g + 1) * n]
        r = lax.rsqrt(jnp.mean(yg * yg, axis=-1, keepdims=True) + RMS_EPS)
        parts.append(jnp.broadcast_to(r, yg.shape))
    return jnp.concatenate(parts, axis=1)


def _group_mean(v):
    n = SSM_INNER // SSM_GROUPS
    parts = []
    for g in range(SSM_GROUPS):
        vg = v[:, g * n:(g + 1) * n]
        parts.append(jnp.broadcast_to(jnp.mean(vg, axis=-1, keepdims=True), vg.shape))
    return jnp.concatenate(parts, axis=1)


def _ssm_post_fwd(y_ssd, proj, nw):
    t = proj.shape[0]
    n = SSM_INNER
    tr = _rows(t)

    def body(y_ref, z_ref, nw_ref, o_ref):
        z = z_ref[...]
        y = y_ref[...] * (z * _sigmoid(z))
        o_ref[...] = (y * _group_rstd(y) * nw_ref[...]).astype(BF16)

    row = pl.BlockSpec((tr, n), lambda i: (i, 0))
    return pl.pallas_call(
        body, name="ssm_post_fwd", grid=(t // tr,),
        in_specs=[row, pl.BlockSpec((tr, n), lambda i: (i, COL_Z // n)), pl.BlockSpec((1, n), lambda i: (0, 0))],
        out_specs=row, out_shape=jax.ShapeDtypeStruct((t, n), BF16),
        compiler_params=_params(("parallel",)),
    )(y_ssd, proj, nw)


def _ssm_post_bwd(y_ssd, proj, nw, dout):
    t = proj.shape[0]
    n = SSM_INNER
    tr = _rows(t)

    def body(y_ref, z_ref, nw_ref, do_ref, dy_ref, dz_ref, dnw_ref):
        i = pl.program_id(0)
        z = z_ref[...]
        sg = _sigmoid(z)
        sz = z * sg
        ys = y_ref[...]
        y = ys * sz
        rstd = _group_rstd(y)
        yn = y * rstd
        dov = do_ref[...]
        dyn = dov * nw_ref[...]
        dyg = rstd * (dyn - yn * _group_mean(dyn * yn))
        dy_ref[...] = dyg * sz
        dz_ref[...] = (dyg * ys * (sg * (1.0 + z * (1.0 - sg)))).astype(BF16)
        part = jnp.sum(dov * yn, axis=0, keepdims=True)

        @pl.when(i == 0)
        def _():
            dnw_ref[...] = part

        @pl.when(i > 0)
        def _():
            dnw_ref[...] += part

    row = pl.BlockSpec((tr, n), lambda i: (i, 0))
    vec = pl.BlockSpec((1, n), lambda i: (0, 0))
    return pl.pallas_call(
        body, name="ssm_post_bwd", grid=(t // tr,),
        in_specs=[row, pl.BlockSpec((tr, n), lambda i: (i, COL_Z // n)), vec, row],
        out_specs=[row, row, vec],
        out_shape=[jax.ShapeDtypeStruct((t, n), F32), jax.ShapeDtypeStruct((t, n), BF16),
                   jax.ShapeDtypeStruct((1, n), F32)],
        compiler_params=_params(("arbitrary",)),
    )(y_ssd, proj, nw, dout)


SCAN_UNROLL = 8
GELU_C = math.sqrt(2.0 / math.pi)
GELU_K = 0.044715


def _gelu_parts(y):
    th = jnp.tanh(GELU_C * (y + GELU_K * y * y * y))
    val = 0.5 * y * (1.0 + th)
    grad = 0.5 * (1.0 + th) + 0.5 * y * (1.0 - th * th) * GELU_C * (1.0 + 3.0 * GELU_K * y * y)
    return val, grad


def _scan_tiles(a_ref, b_ref, h_ref, n_rows, reverse):
    n_tiles = n_rows // SUBLANES
    shape = (SUBLANES, a_ref.shape[1])
    row = _iota(shape, 0)

    def step(k, carry):
        tile = (n_tiles - 1 - k) if reverse else k
        at = pl.ds(pl.multiple_of(tile * SUBLANES, SUBLANES), SUBLANES)
        av = a_ref[at, :]
        bv = b_ref[at, :]
        for s in (1, 2, 4):
            if reverse:
                keep = row < SUBLANES - s
                a_sh = jnp.where(keep, pltpu.roll(av, SUBLANES - s, 0), 1.0)
                b_sh = jnp.where(keep, pltpu.roll(bv, SUBLANES - s, 0), 0.0)
            else:
                keep = row >= s
                a_sh = jnp.where(keep, pltpu.roll(av, s, 0), 1.0)
                b_sh = jnp.where(keep, pltpu.roll(bv, s, 0), 0.0)
            bv = av * b_sh + bv
            av = av * a_sh
        hv = bv + av * carry
        h_ref[at, :] = hv
        return hv[0:1, :] if reverse else hv[SUBLANES - 1:SUBLANES, :]

    lax.fori_loop(0, n_tiles, step, jnp.zeros((1, a_ref.shape[1]), F32), unroll=SCAN_UNROLL)


def _lru_gates(xl, cw, cb, wr, br, wi, bi, lam):
    u = cb + cw[CONV_K - 1:CONV_K, :] * xl
    for k in range(CONV_K - 1):
        u = u + cw[k:k + 1, :] * _shift_down(xl, CONV_K - 1 - k)
    r = _sigmoid(_bdot(u, wr) + br)
    i = _sigmoid(_bdot(u, wi) + bi)
    sp = _softplus(-lam)
    la = -LRU_C * r * sp
    a = jnp.exp(la)
    mult = jnp.sqrt(-jnp.tanh(la) * (a * a + 1.0))
    return u, r, i, sp, a, mult


def _lru_specs(t):
    c_x = COL_LRU_X // LANES
    c_y = COL_LRU_Y // LANES
    xl = pl.BlockSpec((t, LANES), lambda j: (0, c_x + j))
    yl = pl.BlockSpec((t, LANES), lambda j: (0, c_y + j))
    col = pl.BlockSpec((t, LANES), lambda j: (0, j))
    cw = pl.BlockSpec((CONV_K, LANES), lambda j: (0, j))
    vec = pl.BlockSpec((1, LANES), lambda j: (0, j))
    wblk = pl.BlockSpec((1, LANES, LANES), lambda j: (j, 0, 0))
    return xl, yl, col, cw, vec, wblk


def _lru_fwd(proj, cw, cb, wr, br, wi, bi, lam, gather):
    t = proj.shape[0]
    xl_s, yl_s, col, cw_s, vec, wblk = _lru_specs(t)
    n = len(gather)

    def body(*refs):
        xl_ref, yl_ref, cw_ref, cb_ref, wr_ref, br_ref, wi_ref, bi_ref, lam_ref = refs[:9]
        o_ref, h_ref = refs[9 + n:11 + n]
        a_s, b_s = refs[11 + 2 * n:13 + 2 * n]
        comm = (refs[9:9 + n], refs[11 + n:11 + 2 * n]) + tuple(refs[13 + 2 * n:])
        j = pl.program_id(0)
        for step, phase in ((0, "start"), (LRU_BLOCKS // 2, "forward")):
            @pl.when(j == step)
            def _():
                _gather_phase(phase, *comm)

        u, r, i, sp, a, mult = _lru_gates(xl_ref[...], cw_ref[...], cb_ref[...], wr_ref[0], br_ref[...],
                                          wi_ref[0], bi_ref[...], lam_ref[...])
        a_s[...] = a
        b_s[...] = mult * (i * u)
        _scan_tiles(a_s, b_s, h_ref, t, reverse=False)
        o_ref[...] = (h_ref[...] * _gelu_parts(yl_ref[...])[0]).astype(BF16)

        @pl.when(j == LRU_BLOCKS - 1)
        def _():
            _gather_phase("finish", *comm)

    out = pl.pallas_call(
        body, name="lru_fwd", grid=(LRU_BLOCKS,),
        in_specs=[xl_s, yl_s, cw_s, vec, wblk, vec, wblk, vec, vec] + [HBM] * n,
        out_specs=[col, col] + [HBM] * n,
        out_shape=[jax.ShapeDtypeStruct((t, LRU_WIDTH), BF16), jax.ShapeDtypeStruct((t, LRU_WIDTH), F32)]
        + [jax.ShapeDtypeStruct((N_DEV,) + v.shape, v.dtype) for v in gather],
        scratch_shapes=[pltpu.VMEM((t, LANES), F32)] * 2 + _comm_scratch(n),
        compiler_params=_params(("arbitrary",), big=True),
    )(proj, proj, cw, cb, wr, br, wi, bi, lam, *gather)
    return out[:2], out[2:]


def _lru_bwd(proj, cw, cb, wr, br, wi, bi, lam, h_all, dout):
    t = proj.shape[0]
    xl_s, yl_s, col, cw_s, vec, wblk = _lru_specs(t)

    def body(xl_ref, yl_ref, cw_ref, cb_ref, wr_ref, br_ref, wi_ref, bi_ref, lam_ref, h_ref, do_ref,
             dxl_ref, dyl_ref, dcw_ref, dcb_ref, dwr_ref, dbr_ref, dwi_ref, dbi_ref, dlam_ref,
             a_s, b_s, g_s):
        xl = xl_ref[...]
        cwv = cw_ref[...]
        lam = lam_ref[...]
        u, r, i, sp, a, mult = _lru_gates(xl, cwv, cb_ref[...], wr_ref[0], br_ref[...], wi_ref[0], bi_ref[...], lam)
        v = i * u
        gl, dgl = _gelu_parts(yl_ref[...])
        dov = do_ref[...]
        h = h_ref[...]
        dyl_ref[...] = (dov * h * dgl).astype(BF16)
        b_s[...] = dov * gl
        a_s[...] = _shift_up(a, 1)
        _scan_tiles(a_s, b_s, g_s, t, reverse=True)
        g = g_s[...]
        da = g * _shift_down(h, 1)
        dmult = g * v
        dv = g * mult
        dla = da * a - dmult * (a * a) / mult
        dr = dla * (-LRU_C * sp)
        dsp = jnp.sum(dla * (-LRU_C * r), axis=0, keepdims=True)
        dlam_ref[...] = -dsp * _sigmoid(-lam)
        dpr = dr * r * (1.0 - r)
        dpi = dv * u * i * (1.0 - i)
        dbr_ref[...] = jnp.sum(dpr, axis=0, keepdims=True)
        dbi_ref[...] = jnp.sum(dpi, axis=0, keepdims=True)
        dwr_ref[0] = _bdot(u, dpr, TN)
        dwi_ref[0] = _bdot(u, dpi, TN)
        du = dv * i + _bdot(dpr, wr_ref[0], NT) + _bdot(dpi, wi_ref[0], NT)
        dxl = cwv[CONV_K - 1:CONV_K, :] * du
        for k in range(CONV_K - 1):
            dxl = dxl + cwv[k:k + 1, :] * _shift_up(du, CONV_K - 1 - k)
        dxl_ref[...] = dxl.astype(BF16)
        for k in range(CONV_K):
            dcw_ref[k:k + 1, :] = jnp.sum(du * _shift_down(xl, CONV_K - 1 - k), axis=0, keepdims=True)
        dcb_ref[...] = jnp.sum(du, axis=0, keepdims=True)

    return pl.pallas_call(
        body, name="lru_bwd", grid=(LRU_BLOCKS,),
        in_specs=[xl_s, yl_s, cw_s, vec, wblk, vec, wblk, vec, vec, col, col],
        out_specs=[col, col, cw_s, vec, wblk, vec, wblk, vec, vec],
        out_shape=[jax.ShapeDtypeStruct((t, LRU_WIDTH), BF16), jax.ShapeDtypeStruct((t, LRU_WIDTH), BF16),
                   jax.ShapeDtypeStruct((CONV_K, LRU_WIDTH), F32), jax.ShapeDtypeStruct((1, LRU_WIDTH), F32),
                   jax.ShapeDtypeStruct((LRU_BLOCKS, LANES, LANES), F32), jax.ShapeDtypeStruct((1, LRU_WIDTH), F32),
                   jax.ShapeDtypeStruct((LRU_BLOCKS, LANES, LANES), F32), jax.ShapeDtypeStruct((1, LRU_WIDTH), F32),
                   jax.ShapeDtypeStruct((1, LRU_WIDTH), F32)],
        scratch_shapes=[pltpu.VMEM((t, LANES), F32)] * 3,
        compiler_params=_params(("arbitrary",), big=True),
    )(proj, proj, cw, cb, wr, br, wi, bi, lam, h_all, dout)


def _mesh_pos():
    return lax.axis_index("x"), lax.axis_index("y"), lax.axis_index("c")


HBM = pl.BlockSpec(memory_space=pl.ANY)


def _comm_scratch(n):
    return [pltpu.SemaphoreType.DMA((n, 7)), pltpu.SemaphoreType.DMA((n, 7)), pltpu.SemaphoreType.DMA((n,))]


def _gather_phase(phase, v_refs, out_refs, send_sems, recv_sems, local_sems):
    n = len(v_refs)
    x, y, c = _mesh_pos()
    me, sibling = (x, y, c), (x, y, 1 - c)
    chips = [(1 - x, y), (x, 1 - y), (1 - x, 1 - y)]

    def block(a, px, py, pc):
        return out_refs[a].at[4 * px + 2 * py + pc]

    def copy(a, k, blk, to, src=None):
        return pltpu.make_async_remote_copy(
            src_ref=block(a, *blk) if src is None else src, dst_ref=block(a, *blk),
            send_sem=send_sems.at[a, k], recv_sem=recv_sems.at[a, k], device_id=to, device_id_type=MESH)

    def own(a):
        return pltpu.make_async_copy(v_refs[a], block(a, *me), local_sems.at[a])

    def first(a):
        return ([copy(a, 0, me, sibling, src=v_refs[a])]
                + [copy(a, 1 + j, me, (*chip, c), src=v_refs[a]) for j, chip in enumerate(chips)])

    def forward(a, j):
        return copy(a, 4 + j, (*chips[j], c), sibling)

    if phase == "start":
        for a in range(n):
            own(a).start()
        for a in range(n):
            for cp in first(a):
                cp.start()
    elif phase == "forward":
        for j in range(3):
            for a in range(n):
                copy(a, 1 + j, (*chips[j], c), me).wait_recv()
                forward(a, j).start()
    else:
        for a in range(n):
            copy(a, 0, sibling, me).wait_recv()
            for j in range(3):
                copy(a, 4 + j, (*chips[j], 1 - c), me).wait_recv()
        for a in range(n):
            for cp in first(a) + [forward(a, j) for j in range(3)]:
                cp.wait_send()
            own(a).wait()


def _all_gather(vs, name):
    n = len(vs)

    def body(*refs):
        comm = (refs[:n], refs[n:2 * n]) + tuple(refs[2 * n:])
        for phase in ("start", "forward", "finish"):
            _gather_phase(phase, *comm)

    return pl.pallas_call(
        body, name=name,
        out_shape=[jax.ShapeDtypeStruct((N_DEV,) + v.shape, v.dtype) for v in vs],
        in_specs=[HBM] * n, out_specs=[HBM] * n, scratch_shapes=_comm_scratch(n),
    )(*vs)


def _exchange(parts, name):
    n = len(parts)

    def body(*refs):
        comm = (refs[:n], refs[n:2 * n]) + tuple(refs[2 * n:])
        _exchange_phase("start", *comm)
        _exchange_phase("finish", *comm)

    return pl.pallas_call(
        body, name=name,
        out_shape=[jax.ShapeDtypeStruct(p.shape, p.dtype) for p in parts],
        in_specs=[HBM] * n, out_specs=[HBM] * n, scratch_shapes=_comm_scratch(n),
    )(*parts)


def _exchange_phase(phase, p_refs, out_refs, send_sems, recv_sems, local_sems):
    n = len(p_refs)
    x, y, c = _mesh_pos()
    me = 4 * x + 2 * y + c
    local = [pltpu.make_async_copy(p_refs[a].at[me], out_refs[a].at[me], local_sems.at[a]) for a in range(n)]
    remote = []
    for k in range(1, N_DEV):
        px = (1 - x) if k & 4 else x
        py = (1 - y) if k & 2 else y
        pc = (1 - c) if k & 1 else c
        for a in range(n):
            remote.append(pltpu.make_async_remote_copy(
                src_ref=p_refs[a].at[4 * px + 2 * py + pc], dst_ref=out_refs[a].at[me],
                send_sem=send_sems.at[a, k - 1], recv_sem=recv_sems.at[a, k - 1],
                device_id=(px, py, pc), device_id_type=MESH))
    if phase == "start":
        for cp in local + remote:
            cp.start()
    else:
        for cp in remote:
            cp.wait()
        for cp in local:
            cp.wait()


def _sum_sources(recv, tile, name):
    n, rows, width = recv.shape

    def body(r_ref, o_ref):
        acc = r_ref[0].astype(F32)
        for s in range(1, n):
            acc = acc + r_ref[s].astype(F32)
        o_ref[...] = acc

    return pl.pallas_call(
        body, name=name, grid=(rows // tile,),
        in_specs=[pl.BlockSpec((n, tile, width), lambda i: (0, i, 0))],
        out_specs=pl.BlockSpec((tile, width), lambda i: (i, 0)),
        out_shape=jax.ShapeDtypeStruct((rows, width), F32),
        compiler_params=_params(("parallel",)),
    )(recv)


def _row_tile(rows):
    for tile in range(128, 15, -16):
        if rows % tile == 0:
            return tile
    return rows


def _adamw(w, recv, m, v, name):
    rows, width = w.shape
    n = recv.shape[0]
    if rows % 16 == 0 or width % 256:
        tr, tc = _row_tile(rows), width
    else:
        tr, tc = rows, 256
    c1 = 1.0 - ADAM_B1 ** ADAM_STEP
    c2 = 1.0 - ADAM_B2 ** ADAM_STEP

    def body(w_ref, r_ref, m_ref, v_ref, g_ref, d_ref, nm_ref, nv_ref):
        gv = r_ref[0].astype(F32)
        for s in range(1, n):
            gv = gv + r_ref[s].astype(F32)
        nm = ADAM_B1 * m_ref[...] + (1.0 - ADAM_B1) * gv
        nv = ADAM_B2 * v_ref[...] + (1.0 - ADAM_B2) * (gv * gv)
        g_ref[...] = gv
        nm_ref[...] = nm
        nv_ref[...] = nv
        d_ref[...] = -ADAM_LR * ((nm / c1) / (jnp.sqrt(nv / c2) + ADAM_EPS) + ADAM_WD * w_ref[...])

    spec = pl.BlockSpec((tr, tc), lambda i, j: (i, j))
    shape = jax.ShapeDtypeStruct((rows, width), F32)
    return pl.pallas_call(
        body, name=name, grid=(rows // tr, width // tc),
        in_specs=[spec, pl.BlockSpec((n, tr, tc), lambda i, j: (0, i, j)), spec, spec],
        out_specs=[spec] * 4, out_shape=[shape] * 4,
        compiler_params=_params(("parallel", "parallel")),
    )(w, recv, m, v)


BIG_NAMES = ("w_in", "w_out_ssm", "w_out_lru", "w_out", "w_ffn_in", "w_ffn_out", "ssm_conv_w", "lru_conv_w")
TRANSPOSED = ("w_in", "w_ffn_in")
CONV_NAMES = ("ssm_conv_w", "lru_conv_w")
MATMUL_NAMES = BIG_NAMES[:6]
NEEDED_FIRST = ("w_in", "ssm_conv_w", "lru_conv_w")
GATHERED_IN_SSD = ("w_ffn_in", "w_ffn_out")
GATHERED_IN_LRU = ("w_out_ssm", "w_out_lru", "w_out")
NEEDED_LATER = GATHERED_IN_LRU + GATHERED_IN_SSD
SMALL_NAMES = ("norm1_w", "b_branch_gate", "ssm_conv_b", "ssm_dt_bias", "ssm_a_log", "ssm_d", "ssm_norm_w",
               "lru_conv_b", "lru_w_r", "lru_b_r", "lru_w_i", "lru_b_i", "lru_lambda", "norm2_w", "norm_f_w")


def _pack(arrays, total):
    flat = []
    used = 0
    for a in arrays:
        a = a.reshape(-1)
        pad = (-a.shape[0]) % LANES
        flat.append(a)
        if pad:
            flat.append(jnp.zeros((pad,), a.dtype))
        used += a.shape[0] + pad
    assert used <= total, (used, total)
    if total > used:
        flat.append(jnp.zeros((total - used,), arrays[0].dtype))
    return jnp.concatenate(flat)


def _unpack(flat, shapes):
    out = []
    off = 0
    for shp in shapes:
        n = math.prod(shp)
        out.append(flat[off:off + n].reshape(shp))
        off += n + (-n) % LANES
    return out


def _col_shards(full):
    rows, cols = full.shape
    return full.reshape(rows, N_DEV, cols // N_DEV).transpose(1, 0, 2)


def _row_shards(full):
    rows, cols = full.shape
    return full.reshape(N_DEV, rows // N_DEV, cols)


def _from_col_shards(g):
    n, rows, w = g.shape
    return g.transpose(1, 0, 2).reshape(rows, n * w)


def kernel(x, norm1_w, w_in, b_branch_gate, ssm_conv_w, ssm_conv_b, ssm_dt_bias, ssm_a_log, ssm_d, ssm_norm_w, w_out_ssm, lru_conv_w, lru_conv_b, lru_w_r, lru_b_r, lru_w_i, lru_b_i, lru_lambda, w_out_lru, w_out, norm2_w, w_ffn_in, w_ffn_out, norm_f_w, loss_target, m_norm1_w, m_w_in, m_b_branch_gate, m_ssm_conv_w, m_ssm_conv_b, m_ssm_dt_bias, m_ssm_a_log, m_ssm_d, m_ssm_norm_w, m_w_out_ssm, m_lru_conv_w, m_lru_conv_b, m_lru_w_r, m_lru_b_r, m_lru_w_i, m_lru_b_i, m_lru_lambda, m_w_out_lru, m_w_out, m_norm2_w, m_w_ffn_in, m_w_ffn_out, m_norm_f_w, v_norm1_w, v_w_in, v_b_branch_gate, v_ssm_conv_w, v_ssm_conv_b, v_ssm_dt_bias, v_ssm_a_log, v_ssm_d, v_ssm_norm_w, v_w_out_ssm, v_lru_conv_w, v_lru_conv_b, v_lru_w_r, v_lru_b_r, v_lru_w_i, v_lru_b_i, v_lru_lambda, v_w_out_lru, v_w_out, v_norm2_w, v_w_ffn_in, v_w_ffn_out, v_norm_f_w):
    given = dict(locals())
    weights = {n: given[n] for n in BIG_NAMES + SMALL_NAMES}
    t = x.shape[1]
    xt = x[0]
    tgt = loss_target[0]

    def local(n, a):
        return a[0].T if n in TRANSPOSED else a[0]

    def as_output(n, a):
        return a.T[None] if n in TRANSPOSED else a[None]

    def shard(n):
        s = local(n, weights[n])
        return s.astype(BF16) if n in MATMUL_NAMES else s

    def unshard(n, g):
        return _from_col_shards(g) if n in CONV_NAMES else g.reshape(-1, g.shape[-1])

    def grad_slices(n):
        g = grads[n]
        return (_col_shards(g) if n in CONV_NAMES else g.reshape(N_DEV, -1, g.shape[-1])).astype(BF16)

    gathered = _all_gather([shard(n) for n in NEEDED_FIRST], "gather_in_weights")
    full = {n: unshard(n, g) for n, g in zip(NEEDED_FIRST, gathered)}
    ssm_cw, lru_cw = full["ssm_conv_w"], full["lru_conv_w"]
    wi_t = full["w_in"]
    w_pt = jnp.concatenate([wi_t[:ORIG_DT], wi_t[ORIG_LRU:], wi_t[ORIG_DT:ORIG_LRU],
                            jnp.zeros((PROJ_W - IN_PROJ, D_MODEL), BF16)], axis=0)

    def pad_heads(a):
        return jnp.pad(a.reshape(1, SSM_HEADS), ((0, 0), (0, LANES - SSM_HEADS)))

    dt_bias_p = pad_heads(ssm_dt_bias)
    a_log_p = pad_heads(ssm_a_log)
    d_exp = jnp.repeat(ssm_d.reshape(SSM_HEADS), SSM_HEAD_DIM).reshape(1, SSM_INNER)
    lru_wr, lru_wi = lru_w_r[0], lru_w_i[0]

    hn1 = _rmsnorm_fwd(xt, norm1_w, "norm1_fwd")
    proj = _mm(hn1, w_pt, tb=True, name="in_proj")
    xbc_act = _ssm_conv_fwd(proj, ssm_cw, ssm_conv_b)
    (y_ssd, s_in_all), gathered = _ssd_fwd(xbc_act, proj, dt_bias_p, a_log_p, d_exp,
                                           gather=[shard(n) for n in GATHERED_IN_SSD])
    full.update({n: unshard(n, g) for n, g in zip(GATHERED_IN_SSD, gathered)})
    (l_out, h_lru), gathered = _lru_fwd(proj, lru_cw, lru_conv_b, lru_wr, lru_b_r, lru_wi, lru_b_i, lru_lambda,
                                        gather=[shard(n) for n in GATHERED_IN_LRU])
    full.update({n: unshard(n, g) for n, g in zip(GATHERED_IN_LRU, gathered)})
    y_pre = _ssm_post_fwd(y_ssd, proj, ssm_norm_w)
    y_ssm = _mm(y_pre, full["w_out_ssm"], name="out_ssm")
    y_lru = _mm(l_out, full["w_out_lru"], name="out_lru")
    merged = _merge_fwd(proj, b_branch_gate, y_ssm, y_lru)
    h1 = _mm(merged, full["w_out"], add=xt, name="out_proj")
    hn2 = _rmsnorm_fwd(h1, norm2_w, "norm2_fwd")
    gu = _mm(hn2, full["w_ffn_in"], tb=True, name="ffn_in")
    act = _swiglu_fwd(gu)
    h2 = _mm(act, full["w_ffn_out"], add=h1, name="ffn_out")

    grads = {}
    dh2, grads["norm_f_w"], loss_cols = _loss_head(h2, norm_f_w.reshape(1, D_MODEL), tgt)
    loss = lax.psum(0.5 * jnp.sum(loss_cols) / D_MODEL, AXES)
    dact = _mm(dh2, full["w_ffn_out"], tb=True, name="d_act")
    grads["w_ffn_out"] = _mm(act, dh2, ta=True, out_dtype=BF16, name="dw_ffn_out")
    dgu = _swiglu_bwd(gu, dact)
    dhn2 = _mm(dgu, full["w_ffn_in"], name="d_hn2")
    grads["w_ffn_in"] = _mm(dgu, hn2, ta=True, out_dtype=BF16, name="dw_ffn_in")
    dh1, grads["norm2_w"] = _rmsnorm_bwd(h1, norm2_w, dhn2, dh2, "norm2_bwd")
    dmerged = _mm(dh1, full["w_out"], tb=True, name="d_merged")
    grads["w_out"] = _mm(merged, dh1, ta=True, out_dtype=BF16, name="dw_out")
    dy_ssm, dy_lru, dgates, grads["b_branch_gate"] = _merge_bwd(proj, b_branch_gate, y_ssm, y_lru, dmerged)
    dy_pre = _mm(dy_ssm, full["w_out_ssm"], tb=True, name="d_y_pre")
    grads["w_out_ssm"] = _mm(y_pre, dy_ssm, ta=True, out_dtype=BF16, name="dw_out_ssm")
    dl_out = _mm(dy_lru, full["w_out_lru"], tb=True, name="d_l_out")
    grads["w_out_lru"] = _mm(l_out, dy_lru, ta=True, out_dtype=BF16, name="dw_out_lru")
    dy_ssd, dz, grads["ssm_norm_w"] = _ssm_post_bwd(y_ssd, proj, ssm_norm_w, dy_pre)
    (dxbc_act, ddt, dbias, dalog, ddcol), recv_later = _ssd_bwd(
        xbc_act, proj, s_in_all, dy_ssd, dt_bias_p, a_log_p, d_exp, exchange=[grad_slices(n) for n in NEEDED_LATER])
    grads["ssm_dt_bias"] = dbias[:, :SSM_HEADS]
    grads["ssm_a_log"] = dalog[:, :SSM_HEADS]
    grads["ssm_d"] = ddcol.reshape(SSM_HEADS, SSM_HEAD_DIM).sum(axis=1).reshape(1, SSM_HEADS)
    dxbc, grads["ssm_conv_w"], grads["ssm_conv_b"] = _ssm_conv_bwd(proj, ssm_cw, ssm_conv_b, dxbc_act)
    (dxl, dyl, grads["lru_conv_w"], grads["lru_conv_b"], dwr, grads["lru_b_r"], dwi, grads["lru_b_i"],
     grads["lru_lambda"]) = _lru_bwd(proj, lru_cw, lru_conv_b, lru_wr, lru_b_r, lru_wi, lru_b_i, lru_lambda, h_lru,
                                     dl_out)
    grads["lru_w_r"], grads["lru_w_i"] = dwr[None], dwi[None]
    dproj = jnp.concatenate([dgates, dz, dxbc, dxl, dyl, ddt], axis=1)
    dwpt = _mm(dproj, hn1, ta=True, out_dtype=BF16, name="dw_in")
    grads["w_in"] = jnp.concatenate([dwpt[:ORIG_DT], dwpt[COL_DT:COL_DT + SSM_HEADS], dwpt[COL_LRU_X:COL_DT]], axis=0)
    dhn1, recv_first = _mm(dproj, w_pt, exchange=[grad_slices(n) for n in NEEDED_FIRST], name="d_hn1")
    grad_x, grads["norm1_w"] = _rmsnorm_bwd(xt, norm1_w, dhn1, dh1, "norm1_bwd")

    small_total = N_DEV * SMALL_ROWS * LANES
    small_parts = _pack([grads[n] for n in SMALL_NAMES], small_total).reshape(N_DEV, SMALL_ROWS, LANES)
    small_g = _sum_sources(_exchange([small_parts], "exchange_small_grads")[0], SMALL_ROWS, "sum_small_grads")
    small_g = _all_gather([small_g], "gather_small_grads")[0].reshape(1, small_total // FLAT_W, FLAT_W)

    recv = dict(zip(NEEDED_LATER + NEEDED_FIRST, list(recv_later) + list(recv_first)))
    big_out = {n: _adamw(local(n, weights[n]), recv[n], local(n, given["m_" + n]), local(n, given["v_" + n]),
                         "adamw_" + n) for n in BIG_NAMES}
    small_state = [_pack([given[p + n] for n in SMALL_NAMES], small_total).reshape(-1, FLAT_W) for p in ("", "m_", "v_")]
    small_out = _adamw(small_state[0], small_g, small_state[1], small_state[2], "adamw_replicated")

    small_shapes = [weights[n].shape for n in SMALL_NAMES]
    order = list(given)[1:24]
    results = []
    for q in range(4):
        vals = {n: as_output(n, big_out[n][q]) for n in BIG_NAMES}
        vals.update(zip(SMALL_NAMES, _unpack(small_out[q].reshape(-1), small_shapes)))
        results.extend(vals[n] for n in order)
    return (loss, grad_x[None], *results)
```

```python
import math

import jax
import jax.numpy as jnp
from jax import lax
from jax.experimental import pallas as pl
from jax.experimental.pallas import tpu as pltpu

F32 = jnp.float32
BF16 = jnp.bfloat16
HIGHEST = lax.Precision.HIGHEST
MESH = pl.DeviceIdType.MESH
AXES = ("x", "y", "c")
N_DEV = 8

D_MODEL = 1024
SSM_INNER = 2048
SSM_HEADS = 32
SSM_HEAD_DIM = 64
SSM_GROUPS = 4
SSM_STATE = 128
SSM_BC = SSM_GROUPS * SSM_STATE
SSM_CONV_DIM = SSM_INNER + 2 * SSM_BC
SSM_CHUNK = 128
SSM_PAIRS = SSM_HEADS // 2
CONV_K = 4
LRU_WIDTH = 1280
LRU_BLOCKS = 10
LRU_C = 8.0
FFN_HIDDEN = 2816
RMS_EPS = 1e-6
IN_PROJ = 9760

COL_GATES = 0
COL_Z = 2048
COL_XBC = 4096
COL_LRU_X = 7168
COL_LRU_Y = 8448
COL_DT = 9728
PROJ_W = 9856
ORIG_DT = 7168
ORIG_LRU = 7200

ADAM_LR = 0.001
ADAM_B1 = 0.9
ADAM_B2 = 0.999
ADAM_EPS = 1e-08
ADAM_WD = 0.01
ADAM_STEP = 10

LANES = 128
SUBLANES = 8
V7X_VMEM_BYTES = 64 * 1024 * 1024
VMEM_LIMIT = V7X_VMEM_BYTES * 3 // 4
VMEM_LIMIT_BIG = V7X_VMEM_BYTES * 15 // 16

NT = (((1,), (1,)), ((), ()))
TN = (((0,), (0,)), ((), ()))

FLAT_W = 1024
SMALL_ROWS = 336


def _params(sem=None, big=False):
    return pltpu.CompilerParams(dimension_semantics=sem,
                                vmem_limit_bytes=VMEM_LIMIT_BIG if big else VMEM_LIMIT)


def _blk(dim, cap):
    if dim <= cap:
        return dim
    for m in range(cap // LANES, 0, -1):
        if dim % (m * LANES) == 0:
            return m * LANES
    raise ValueError(f"no block for {dim}")


def _rows(t):
    return min(t, 256)


def _sigmoid(v):
    return 1.0 / (1.0 + jnp.exp(-v))


def _softplus(v):
    e = jnp.exp(-jnp.abs(v))
    u = 1.0 + e
    log1p = jnp.where(u == 1.0, e, jnp.log(u) * e / jnp.where(u == 1.0, 1.0, u - 1.0))
    return jnp.maximum(v, 0.0) + log1p


def _iota(shape, dim):
    return lax.broadcasted_iota(jnp.int32, shape, dim)


def _shift_down(v, s):
    if s == 0:
        return v
    return jnp.where(_iota(v.shape, 0) >= s, pltpu.roll(v, s, 0), 0.0)


def _shift_up(v, s):
    if s == 0:
        return v
    n = v.shape[0]
    return jnp.where(_iota(v.shape, 0) < n - s, pltpu.roll(v, n - s, 0), 0.0)


def _bdot(a, b, dn=None):
    a = a.astype(BF16)
    b = b.astype(BF16)
    if dn is None:
        return jnp.dot(a, b, preferred_element_type=F32)
    return lax.dot_general(a, b, dn, preferred_element_type=F32)


def _split_dot(a, e, dn=None):
    hi = a.astype(BF16)
    lo = (a - hi.astype(F32)).astype(BF16)
    return _bdot(hi, e, dn) + _bdot(lo, e, dn)


def _fdot(a, b, dn=None):
    if dn is None:
        return jnp.dot(a, b, precision=HIGHEST, preferred_element_type=F32)
    return lax.dot_general(a, b, dn, precision=HIGHEST, preferred_element_type=F32)


def _mm(a, b, *, ta=False, tb=False, add=None, exchange=(), out_dtype=F32, name):
    if ta:
        kdim, m = a.shape
    else:
        m, kdim = a.shape
    if tb:
        n, k2 = b.shape
    else:
        k2, n = b.shape
    assert kdim == k2, (a.shape, b.shape, ta, tb)
    bm, bn, bk = _blk(m, 1024), _blk(n, 1408), _blk(kdim, 1408)
    grid = (m // bm, n // bn, kdim // bk)
    nk = grid[2]
    dn = (((0 if ta else 1,), (1 if tb else 0,)), ((), ()))
    n_in = 2 if add is None else 3
    n_ex = len(exchange)

    def body(*refs):
        a_ref, b_ref = refs[:2]
        r_ref = None if add is None else refs[2]
        o_ref = refs[n_in + n_ex]
        acc = refs[n_in + 2 * n_ex + 1]
        comm = (refs[n_in:n_in + n_ex], refs[n_in + n_ex + 1:n_in + 2 * n_ex + 1]) + tuple(refs[n_in + 2 * n_ex + 2:])
        step = (pl.program_id(0) * grid[1] + pl.program_id(1)) * nk + pl.program_id(2)
        k = pl.program_id(2)
        if n_ex:
            @pl.when(step == 0)
            def _():
                _exchange_phase("start", *comm)

        @pl.when(k == 0)
        def _():
            acc[...] = jnp.zeros_like(acc)

        acc[...] += lax.dot_general(a_ref[...].astype(BF16), b_ref[...].astype(BF16), dn,
                                    preferred_element_type=F32)

        @pl.when(k == nk - 1)
        def _():
            r = acc[...]
            if add is not None:
                r = r + r_ref[...]
            o_ref[...] = r.astype(out_dtype)

        if n_ex:
            @pl.when(step == grid[0] * grid[1] * nk - 1)
            def _():
                _exchange_phase("finish", *comm)

    a_spec = pl.BlockSpec((bk, bm), lambda i, j, k: (k, i)) if ta else pl.BlockSpec((bm, bk), lambda i, j, k: (i, k))
    b_spec = pl.BlockSpec((bn, bk), lambda i, j, k: (j, k)) if tb else pl.BlockSpec((bk, bn), lambda i, j, k: (k, j))
    o_spec = pl.BlockSpec((bm, bn), lambda i, j, k: (i, j))
    in_specs = [a_spec, b_spec] + ([o_spec] if add is not None else []) + [HBM] * n_ex
    args = (a, b) + ((add,) if add is not None else ()) + tuple(exchange)
    out = pl.pallas_call(
        body, name=name, grid=grid,
        in_specs=in_specs, out_specs=[o_spec] + [HBM] * n_ex,
        out_shape=[jax.ShapeDtypeStruct((m, n), out_dtype)]
        + [jax.ShapeDtypeStruct(p.shape, p.dtype) for p in exchange],
        scratch_shapes=[pltpu.VMEM((bm, bn), F32)] + (_comm_scratch(n_ex) if n_ex else []),
        compiler_params=_params(("arbitrary",) * 3 if n_ex else ("parallel", "parallel", "arbitrary")),
    )(*args)
    return (out[0], out[1:]) if n_ex else out[0]


def _rmsnorm_fwd(x, w, name):
    t, d = x.shape
    tr = _rows(t)

    def body(x_ref, w_ref, o_ref):
        xv = x_ref[...]
        rstd = lax.rsqrt(jnp.mean(xv * xv, axis=-1, keepdims=True) + RMS_EPS)
        o_ref[...] = (xv * rstd * w_ref[...]).astype(BF16)

    return pl.pallas_call(
        body, name=name, grid=(t // tr,),
        in_specs=[pl.BlockSpec((tr, d), lambda i: (i, 0)), pl.BlockSpec((1, d), lambda i: (0, 0))],
        out_specs=pl.BlockSpec((tr, d), lambda i: (i, 0)),
        out_shape=jax.ShapeDtypeStruct((t, d), BF16),
        compiler_params=_params(("parallel",)),
    )(x, w)


def _rmsnorm_bwd(x, w, dy, dres, name):
    t, d = x.shape
    tr = _rows(t)

    def body(x_ref, w_ref, dy_ref, dres_ref, dx_ref, dw_ref):
        i = pl.program_id(0)
        xv = x_ref[...]
        rstd = lax.rsqrt(jnp.mean(xv * xv, axis=-1, keepdims=True) + RMS_EPS)
        xhat = xv * rstd
        dyv = dy_ref[...]
        dxhat = dyv * w_ref[...]
        m = jnp.mean(dxhat * xhat, axis=-1, keepdims=True)
        dx_ref[...] = rstd * (dxhat - xhat * m) + dres_ref[...]
        part = jnp.sum(dyv * xhat, axis=0, keepdims=True)

        @pl.when(i == 0)
        def _():
            dw_ref[...] = part

        @pl.when(i > 0)
        def _():
            dw_ref[...] += part

    row = pl.BlockSpec((tr, d), lambda i: (i, 0))
    vec = pl.BlockSpec((1, d), lambda i: (0, 0))
    return pl.pallas_call(
        body, name=name, grid=(t // tr,),
        in_specs=[row, vec, row, row], out_specs=[row, vec],
        out_shape=[jax.ShapeDtypeStruct((t, d), F32), jax.ShapeDtypeStruct((1, d), F32)],
        compiler_params=_params(("arbitrary",)),
    )(x, w, dy, dres)


def _loss_head(h2, w, tgt):
    t, d = h2.shape
    tr = _rows(t)

    def body(x_ref, w_ref, t_ref, dx_ref, dw_ref, ls_ref):
        i = pl.program_id(0)
        xv = x_ref[...]
        wv = w_ref[...]
        rstd = lax.rsqrt(jnp.mean(xv * xv, axis=-1, keepdims=True) + RMS_EPS)
        xhat = xv * rstd
        err = xhat * wv - t_ref[...]
        dyv = err * (1.0 / d)
        dxhat = dyv * wv
        m = jnp.mean(dxhat * xhat, axis=-1, keepdims=True)
        dx_ref[...] = rstd * (dxhat - xhat * m)
        dw_part = jnp.sum(dyv * xhat, axis=0, keepdims=True)
        ls_part = jnp.sum(err * err, axis=0, keepdims=True)

        @pl.when(i == 0)
        def _():
            dw_ref[...] = dw_part
            ls_ref[...] = ls_part

        @pl.when(i > 0)
        def _():
            dw_ref[...] += dw_part
            ls_ref[...] += ls_part

    row = pl.BlockSpec((tr, d), lambda i: (i, 0))
    vec = pl.BlockSpec((1, d), lambda i: (0, 0))
    return pl.pallas_call(
        body, name="loss_head", grid=(t // tr,),
        in_specs=[row, vec, row], out_specs=[row, vec, vec],
        out_shape=[jax.ShapeDtypeStruct((t, d), F32), jax.ShapeDtypeStruct((1, d), F32),
                   jax.ShapeDtypeStruct((1, d), F32)],
        compiler_params=_params(("arbitrary",)),
    )(h2, w, tgt)


def _merge_fwd(proj, bg, ys, yl):
    t = proj.shape[0]
    d = D_MODEL
    tr = _rows(t)

    def body(ps_ref, pl_ref, bg_ref, ys_ref, yl_ref, o_ref):
        gs = _sigmoid(ps_ref[...] + bg_ref[:, 0:d])
        gl = _sigmoid(pl_ref[...] + bg_ref[:, d:2 * d])
        o_ref[...] = (gs * ys_ref[...] + gl * yl_ref[...]).astype(BF16)

    row = pl.BlockSpec((tr, d), lambda i: (i, 0))
    return pl.pallas_call(
        body, name="merge_fwd", grid=(t // tr,),
        in_specs=[row, pl.BlockSpec((tr, d), lambda i: (i, 1)), pl.BlockSpec((1, 2 * d), lambda i: (0, 0)), row, row],
        out_specs=row, out_shape=jax.ShapeDtypeStruct((t, d), BF16),
        compiler_params=_params(("parallel",)),
    )(proj, proj, bg, ys, yl)


def _merge_bwd(proj, bg, ys, yl, dm):
    t = proj.shape[0]
    d = D_MODEL
    tr = _rows(t)

    def body(ps_ref, pl_ref, bg_ref, ys_ref, yl_ref, dm_ref, dys_ref, dyl_ref, dg_ref, dbg_ref):
        i = pl.program_id(0)
        gs = _sigmoid(ps_ref[...] + bg_ref[:, 0:d])
        gl = _sigmoid(pl_ref[...] + bg_ref[:, d:2 * d])
        dmv = dm_ref[...]
        dys_ref[...] = (dmv * gs).astype(BF16)
        dyl_ref[...] = (dmv * gl).astype(BF16)
        dgs = dmv * ys_ref[...] * gs * (1.0 - gs)
        dgl = dmv * yl_ref[...] * gl * (1.0 - gl)
        dg_ref[:, 0:d] = dgs.astype(BF16)
        dg_ref[:, d:2 * d] = dgl.astype(BF16)

        @pl.when(i == 0)
        def _():
            dbg_ref[...] = jnp.zeros_like(dbg_ref)

        dbg_ref[:, 0:d] += jnp.sum(dgs, axis=0, keepdims=True)
        dbg_ref[:, d:2 * d] += jnp.sum(dgl, axis=0, keepdims=True)

    row = pl.BlockSpec((tr, d), lambda i: (i, 0))
    wide = pl.BlockSpec((tr, 2 * d), lambda i: (i, 0))
    vec = pl.BlockSpec((1, 2 * d), lambda i: (0, 0))
    return pl.pallas_call(
        body, name="merge_bwd", grid=(t // tr,),
        in_specs=[row, pl.BlockSpec((tr, d), lambda i: (i, 1)), vec, row, row, row],
        out_specs=[row, row, wide, vec],
        out_shape=[jax.ShapeDtypeStruct((t, d), BF16), jax.ShapeDtypeStruct((t, d), BF16),
                   jax.ShapeDtypeStruct((t, 2 * d), BF16), jax.ShapeDtypeStruct((1, 2 * d), F32)],
        compiler_params=_params(("arbitrary",)),
    )(proj, proj, bg, ys, yl, dm)


def _swiglu_fwd(gu):
    t = gu.shape[0]
    f = FFN_HIDDEN
    tr = _rows(t)

    def body(g_ref, u_ref, o_ref):
        g = g_ref[...]
        o_ref[...] = (g * _sigmoid(g) * u_ref[...]).astype(BF16)

    return pl.pallas_call(
        body, name="swiglu_fwd", grid=(t // tr,),
        in_specs=[pl.BlockSpec((tr, f), lambda i: (i, 0)), pl.BlockSpec((tr, f), lambda i: (i, 1))],
        out_specs=pl.BlockSpec((tr, f), lambda i: (i, 0)),
        out_shape=jax.ShapeDtypeStruct((t, f), BF16),
        compiler_params=_params(("parallel",)),
    )(gu, gu)


def _swiglu_bwd(gu, dact):
    t = gu.shape[0]
    f = FFN_HIDDEN
    tr = _rows(t)

    def body(g_ref, u_ref, da_ref, o_ref):
        g = g_ref[...]
        sg = _sigmoid(g)
        da = da_ref[...]
        o_ref[:, 0:f] = (da * u_ref[...] * (sg * (1.0 + g * (1.0 - sg)))).astype(BF16)
        o_ref[:, f:2 * f] = (da * g * sg).astype(BF16)

    return pl.pallas_call(
        body, name="swiglu_bwd", grid=(t // tr,),
        in_specs=[pl.BlockSpec((tr, f), lambda i: (i, 0)), pl.BlockSpec((tr, f), lambda i: (i, 1)),
                  pl.BlockSpec((tr, f), lambda i: (i, 0))],
        out_specs=pl.BlockSpec((tr, 2 * f), lambda i: (i, 0)),
        out_shape=jax.ShapeDtypeStruct((t, 2 * f), BF16),
        compiler_params=_params(("parallel",)),
    )(gu, gu, dact)


def _conv_pre(xv, wv, bv):
    pre = bv + wv[CONV_K - 1:CONV_K, :] * xv
    for k in range(CONV_K - 1):
        pre = pre + wv[k:k + 1, :] * _shift_down(xv, CONV_K - 1 - k)
    return pre


def _ssm_conv_fwd(proj, w, b):
    t = proj.shape[0]
    nb = SSM_CONV_DIM // LANES
    c0 = COL_XBC // LANES

    def body(x_ref, w_ref, b_ref, o_ref):
        pre = _conv_pre(x_ref[...], w_ref[...], b_ref[...])
        o_ref[...] = pre * _sigmoid(pre)

    return pl.pallas_call(
        body, name="ssm_conv_fwd", grid=(nb,),
        in_specs=[pl.BlockSpec((t, LANES), lambda j: (0, c0 + j)), pl.BlockSpec((CONV_K, LANES), lambda j: (0, j)),
                  pl.BlockSpec((1, LANES), lambda j: (0, j))],
        out_specs=pl.BlockSpec((t, LANES), lambda j: (0, j)),
        out_shape=jax.ShapeDtypeStruct((t, SSM_CONV_DIM), F32),
        compiler_params=_params(("parallel",)),
    )(proj, w, b)


def _ssm_conv_bwd(proj, w, b, dact):
    t = proj.shape[0]
    nb = SSM_CONV_DIM // LANES
    c0 = COL_XBC // LANES

    def body(x_ref, w_ref, b_ref, da_ref, dx_ref, dw_ref, db_ref):
        xv = x_ref[...]
        wv = w_ref[...]
        pre = _conv_pre(xv, wv, b_ref[...])
        sg = _sigmoid(pre)
        dpre = da_ref[...] * (sg * (1.0 + pre * (1.0 - sg)))
        dx = wv[CONV_K - 1:CONV_K, :] * dpre
        for k in range(CONV_K - 1):
            dx = dx + wv[k:k + 1, :] * _shift_up(dpre, CONV_K - 1 - k)
        dx_ref[...] = dx.astype(BF16)
        for k in range(CONV_K):
            dw_ref[k:k + 1, :] = jnp.sum(dpre * _shift_down(xv, CONV_K - 1 - k), axis=0, keepdims=True)
        db_ref[...] = jnp.sum(dpre, axis=0, keepdims=True)

    col = pl.BlockSpec((t, LANES), lambda j: (0, j))
    wsp = pl.BlockSpec((CONV_K, LANES), lambda j: (0, j))
    bsp = pl.BlockSpec((1, LANES), lambda j: (0, j))
    return pl.pallas_call(
        body, name="ssm_conv_bwd", grid=(nb,),
        in_specs=[pl.BlockSpec((t, LANES), lambda j: (0, c0 + j)), wsp, bsp, col],
        out_specs=[col, wsp, bsp],
        out_shape=[jax.ShapeDtypeStruct((t, SSM_CONV_DIM), BF16), jax.ShapeDtypeStruct((CONV_K, SSM_CONV_DIM), F32),
                   jax.ShapeDtypeStruct((1, SSM_CONV_DIM), F32)],
        compiler_params=_params(("parallel",)),
    )(proj, w, b, dact)


def _ssd_chunk_terms(dtr, bias, alog):
    a = -jnp.exp(alog)
    dt = _softplus(dtr + bias)
    row = _iota((SSM_CHUNK, SSM_CHUNK), 0)
    col = _iota((SSM_CHUNK, SSM_CHUNK), 1)
    tri = (row >= col).astype(F32)
    cs = _fdot(tri, dt * a)
    dec = jnp.exp(cs[SSM_CHUNK - 1:SSM_CHUNK, :] - cs)
    ecs = jnp.exp(cs)
    off = _iota((LANES, SSM_INNER), 1) - SSM_HEAD_DIM * _iota((LANES, SSM_INNER), 0)
    expand = jnp.where(jnp.logical_and(off >= 0, off < SSM_HEAD_DIM), 1.0, 0.0).astype(BF16)
    return a, dt, cs, dec, ecs, expand, row, col


def _ssd_specs(t):
    nc = t // SSM_CHUNK
    xs = pl.BlockSpec((SSM_CHUNK, SSM_INNER), lambda c: (c, 0))
    bm = pl.BlockSpec((SSM_CHUNK, SSM_BC), lambda c: (c, SSM_INNER // SSM_BC))
    cm = pl.BlockSpec((SSM_CHUNK, SSM_BC), lambda c: (c, SSM_INNER // SSM_BC + 1))
    dtr = pl.BlockSpec((SSM_CHUNK, LANES), lambda c: (c, COL_DT // LANES))
    vec = pl.BlockSpec((1, LANES), lambda c: (0, 0))
    wide = pl.BlockSpec((1, SSM_INNER), lambda c: (0, 0))
    return nc, xs, bm, cm, dtr, vec, wide


def _ssd_fwd(xbc_act, proj, bias, alog, dexp, gather):
    t = proj.shape[0]
    nc, xs_s, bm_s, cm_s, dtr_s, vec, wide = _ssd_specs(t)
    n = len(gather)

    def body(*refs):
        xs_ref, b_ref, c_ref, dtr_ref, bias_ref, alog_ref, dexp_ref = refs[:7]
        y_ref, sin_ref = refs[7 + n:9 + n]
        state = refs[9 + 2 * n]
        comm = (refs[7:7 + n], refs[9 + n:9 + 2 * n]) + tuple(refs[10 + 2 * n:])
        chunk = pl.program_id(0)

        @pl.when(chunk == 0)
        def _():
            _gather_phase("start", *comm)
            state[...] = jnp.zeros_like(state)

        @pl.when(chunk == nc // 2)
        def _():
            _gather_phase("forward", *comm)

        a, dt, cs, dec, ecs, expand, row, col = _ssd_chunk_terms(dtr_ref[...], bias_ref[...], alog_ref[...])
        cst = cs.T
        dt_x = _split_dot(dt, expand)
        dec_x = _split_dot(dec, expand)
        ecs_x = _split_dot(ecs, expand)
        xs = xs_ref[...]
        xdt = xs * dt_x
        xdec = xdt * dec_x
        lane_lo = col < SSM_HEAD_DIM
        causal = row >= col
        sin_ref[0] = state[...]
        for g in range(SSM_GROUPS):
            bg = b_ref[:, g * SSM_STATE:(g + 1) * SSM_STATE].astype(BF16)
            cg = c_ref[:, g * SSM_STATE:(g + 1) * SSM_STATE].astype(BF16)
            cb = _bdot(cg, bg, NT)
            for q in range(SSM_PAIRS // SSM_GROUPS):
                pq = g * (SSM_PAIRS // SSM_GROUPS) + q
                sl = slice(pq * LANES, (pq + 1) * LANES)
                xp = xdt[:, sl].astype(BF16)
                yd = []
                for hh in range(2):
                    h = 2 * pq + hh
                    lmat = jnp.exp(jnp.where(causal, cs[:, h:h + 1] - cst[h:h + 1, :], -jnp.inf))
                    yd.append(_bdot(cb * lmat, xp))
                s_in = state[pq]
                y_off = _bdot(cg, s_in) * ecs_x[:, sl]
                y_ref[:, sl] = jnp.where(lane_lo, yd[0], yd[1]) + y_off + xs[:, sl] * dexp_ref[:, sl]
                state[pq] = s_in * ecs_x[SSM_CHUNK - 1:SSM_CHUNK, sl] + _bdot(bg, xdec[:, sl], TN)

        @pl.when(chunk == nc - 1)
        def _():
            _gather_phase("finish", *comm)

    out = pl.pallas_call(
        body, name="ssd_fwd", grid=(nc,),
        in_specs=[xs_s, bm_s, cm_s, dtr_s, vec, vec, wide] + [HBM] * n,
        out_specs=[pl.BlockSpec((SSM_CHUNK, SSM_INNER), lambda c: (c, 0)),
                   pl.BlockSpec((1, SSM_PAIRS, SSM_STATE, LANES), lambda c: (c, 0, 0, 0))] + [HBM] * n,
        out_shape=[jax.ShapeDtypeStruct((t, SSM_INNER), F32),
                   jax.ShapeDtypeStruct((nc, SSM_PAIRS, SSM_STATE, LANES), F32)]
        + [jax.ShapeDtypeStruct((N_DEV,) + v.shape, v.dtype) for v in gather],
        scratch_shapes=[pltpu.VMEM((SSM_PAIRS, SSM_STATE, LANES), F32)] + _comm_scratch(n),
        compiler_params=_params(("arbitrary",)),
    )(xbc_act, xbc_act, xbc_act, proj, bias, alog, dexp, *gather)
    return out[:2], out[2:]


def _ssd_bwd(xbc_act, proj, s_in_all, dy, bias, alog, dexp, exchange):
    n_ex = len(exchange)
    t = proj.shape[0]
    nc = t // SSM_CHUNK
    last = nc - 1
    xs_s = pl.BlockSpec((SSM_CHUNK, SSM_INNER), lambda c: (last - c, 0))
    bm_s = pl.BlockSpec((SSM_CHUNK, SSM_BC), lambda c: (last - c, SSM_INNER // SSM_BC))
    cm_s = pl.BlockSpec((SSM_CHUNK, SSM_BC), lambda c: (last - c, SSM_INNER // SSM_BC + 1))
    dtr_s = pl.BlockSpec((SSM_CHUNK, LANES), lambda c: (last - c, COL_DT // LANES))
    sin_s = pl.BlockSpec((1, SSM_PAIRS, SSM_STATE, LANES), lambda c: (last - c, 0, 0, 0))
    vec = pl.BlockSpec((1, LANES), lambda c: (0, 0))
    wide = pl.BlockSpec((1, SSM_INNER), lambda c: (0, 0))

    def body(*refs):
        xs_ref, b_ref, c_ref, dtr_ref, sin_ref, dy_ref, bias_ref, alog_ref, dexp_ref = refs[:9]
        dxbc_ref, ddtr_ref, dbias_ref, dalog_ref, ddcol_ref = refs[9 + n_ex:14 + n_ex]
        dstate, dxdt_s, yoff_s, rx_s, trow_s = refs[14 + 2 * n_ex:19 + 2 * n_ex]
        comm = (refs[9:9 + n_ex], refs[14 + n_ex:14 + 2 * n_ex]) + tuple(refs[19 + 2 * n_ex:])

        @pl.when(pl.program_id(0) == 0)
        def _():
            _exchange_phase("start", *comm)
            dstate[...] = jnp.zeros_like(dstate)
            trow_s[...] = jnp.zeros_like(trow_s)
            dbias_ref[...] = jnp.zeros_like(dbias_ref)
            dalog_ref[...] = jnp.zeros_like(dalog_ref)
            ddcol_ref[...] = jnp.zeros_like(ddcol_ref)

        dtr = dtr_ref[...]
        a, dt, cs, dec, ecs, expand, row, col = _ssd_chunk_terms(dtr, bias_ref[...], alog_ref[...])
        cst = cs.T
        dt_x = _split_dot(dt, expand)
        dec_x = _split_dot(dec, expand)
        ecs_x = _split_dot(ecs, expand)
        xs = xs_ref[...]
        dyv = dy_ref[...]
        xdt = xs * dt_x
        lane_lo = col < SSM_HEAD_DIM
        causal = row >= col
        ddcol_ref[...] += jnp.sum(dyv * xs, axis=0, keepdims=True)
        dcs_col = jnp.zeros((SSM_CHUNK, LANES), F32)
        dcs_row = jnp.zeros((LANES, SSM_CHUNK), F32)
        for g in range(SSM_GROUPS):
            bg = b_ref[:, g * SSM_STATE:(g + 1) * SSM_STATE].astype(BF16)
            cg = c_ref[:, g * SSM_STATE:(g + 1) * SSM_STATE].astype(BF16)
            cb = _bdot(cg, bg, NT)
            dgm = jnp.zeros((SSM_CHUNK, SSM_CHUNK), F32)
            dbg = jnp.zeros((SSM_CHUNK, SSM_STATE), F32)
            dcg = jnp.zeros((SSM_CHUNK, SSM_STATE), F32)
            for q in range(SSM_PAIRS // SSM_GROUPS):
                pq = g * (SSM_PAIRS // SSM_GROUPS) + q
                sl = slice(pq * LANES, (pq + 1) * LANES)
                dyp = dyv[:, sl]
                xp = xdt[:, sl]
                dxh = []
                for hh in range(2):
                    h = 2 * pq + hh
                    lmat = jnp.exp(jnp.where(causal, cs[:, h:h + 1] - cst[h:h + 1, :], -jnp.inf))
                    mmat = cb * lmat
                    dyh = jnp.where(lane_lo if hh == 0 else jnp.logical_not(lane_lo), dyp, 0.0)
                    dmm = _bdot(dyh, xp, NT)
                    pm = dmm * mmat
                    dcs_col = jnp.where(col == h, jnp.sum(pm, axis=1, keepdims=True), dcs_col)
                    dcs_row = jnp.where(row == h, jnp.sum(pm, axis=0, keepdims=True), dcs_row)
                    dgm = dgm + dmm * lmat
                    dxh.append(_bdot(mmat, dyp, TN))
                s_in = sin_ref[0, pq]
                ecs_p = ecs_x[:, sl]
                dec_p = dec_x[:, sl]
                etot_p = ecs_x[SSM_CHUNK - 1:SSM_CHUNK, sl]
                yoff_s[:, sl] = dyp * (_bdot(cg, s_in) * ecs_p)
                dq = dyp * ecs_p
                dcg = dcg + _bdot(dq, s_in, NT)
                ds = dstate[pq]
                r = _bdot(bg, ds)
                rx_s[:, sl] = r * xp
                dxdt_s[:, sl] = jnp.where(lane_lo, dxh[0], dxh[1]) + dec_p * r
                dbg = dbg + _bdot(xp * dec_p, ds, NT)
                trow_s[0:1, sl] = jnp.sum(ds * s_in, axis=0, keepdims=True) * etot_p
                dstate[pq] = etot_p * ds + _bdot(cg, dq, TN)
            dcg = dcg + _bdot(dgm, bg)
            dbg = dbg + _bdot(dgm, cg, TN)
            dxbc_ref[:, SSM_INNER + g * SSM_STATE:SSM_INNER + (g + 1) * SSM_STATE] = dbg
            dxbc_ref[:, SSM_INNER + SSM_BC + g * SSM_STATE:SSM_INNER + SSM_BC + (g + 1) * SSM_STATE] = dcg
        ddec = _split_dot(rx_s[...], expand, NT) * dec
        dtot = _split_dot(trow_s[...], expand, NT)[0:1, :]
        dcs = dcs_col - dcs_row.T + _split_dot(yoff_s[...], expand, NT) - ddec
        dcs = dcs + jnp.where(row == SSM_CHUNK - 1, jnp.sum(ddec, axis=0, keepdims=True) + dtot, 0.0)
        da = _fdot((row <= col).astype(F32), dcs)
        dxdt = dxdt_s[...]
        ddt = da * a + _split_dot(dxdt * xs, expand, NT)
        dalog_ref[...] += jnp.sum(da * dt, axis=0, keepdims=True) * a
        ddtr = ddt * _sigmoid(dtr + bias_ref[...])
        ddtr_ref[...] = ddtr.astype(BF16)
        dbias_ref[...] += jnp.sum(ddtr, axis=0, keepdims=True)
        dxbc_ref[:, 0:SSM_INNER] = dxdt * dt_x + dyv * dexp_ref[...]

        @pl.when(pl.program_id(0) == last)
        def _():
            _exchange_phase("finish", *comm)

    out = pl.pallas_call(
        body, name="ssd_bwd", grid=(nc,),
        in_specs=[xs_s, bm_s, cm_s, dtr_s, sin_s, pl.BlockSpec((SSM_CHUNK, SSM_INNER), lambda c: (last - c, 0)),
                  vec, vec, wide] + [HBM] * n_ex,
        out_specs=[pl.BlockSpec((SSM_CHUNK, SSM_CONV_DIM), lambda c: (last - c, 0)),
                   pl.BlockSpec((SSM_CHUNK, LANES), lambda c: (last - c, 0)), vec, vec, wide] + [HBM] * n_ex,
        out_shape=[jax.ShapeDtypeStruct((t, SSM_CONV_DIM), F32), jax.ShapeDtypeStruct((t, LANES), BF16),
                   jax.ShapeDtypeStruct((1, LANES), F32), jax.ShapeDtypeStruct((1, LANES), F32),
                   jax.ShapeDtypeStruct((1, SSM_INNER), F32)]
        + [jax.ShapeDtypeStruct(p.shape, p.dtype) for p in exchange],
        scratch_shapes=[pltpu.VMEM((SSM_PAIRS, SSM_STATE, LANES), F32),
                        pltpu.VMEM((SSM_CHUNK, SSM_INNER), F32), pltpu.VMEM((SSM_CHUNK, SSM_INNER), F32),
                        pltpu.VMEM((SSM_CHUNK, SSM_INNER), F32), pltpu.VMEM((SUBLANES, SSM_INNER), F32)]
        + _comm_scratch(n_ex),
        compiler_params=_params(("arbitrary",)),
    )(xbc_act, xbc_act, xbc_act, proj, s_in_all, dy, bias, alog, dexp, *exchange)
    return out[:5], out[5:]


def _group_rstd(y):
    n = SSM_INNER // SSM_GROUPS
    parts = []
    for g in range(SSM_GROUPS):
        yg = y[:, g * n:(g + 1) * n]
        r = lax.rsqrt(jnp.mean(yg * yg, axis=-1, keepdims=True) + RMS_EPS)
        parts.append(jnp.broadcast_to(r, yg.shape))
    return jnp.concatenate(parts, axis=1)


def _group_mean(v):
    n = SSM_INNER // SSM_GROUPS
    parts = []
    for g in range(SSM_GROUPS):
        vg = v[:, g * n:(g + 1) * n]
        parts.append(jnp.broadcast_to(jnp.mean(vg, axis=-1, keepdims=True), vg.shape))
    return jnp.concatenate(parts, axis=1)


def _ssm_post_fwd(y_ssd, proj, nw):
    t = proj.shape[0]
    n = SSM_INNER
    tr = _rows(t)

    def body(y_ref, z_ref, nw_ref, o_ref):
        z = z_ref[...]
        y = y_ref[...] * (z * _sigmoid(z))
        o_ref[...] = (y * _group_rstd(y) * nw_ref[...]).astype(BF16)

    row = pl.BlockSpec((tr, n), lambda i: (i, 0))
    return pl.pallas_call(
        body, name="ssm_post_fwd", grid=(t // tr,),
        in_specs=[row, pl.BlockSpec((tr, n), lambda i: (i, COL_Z // n)), pl.BlockSpec((1, n), lambda i: (0, 0))],
        out_specs=row, out_shape=jax.ShapeDtypeStruct((t, n), BF16),
        compiler_params=_params(("parallel",)),
    )(y_ssd, proj, nw)


def _ssm_post_bwd(y_ssd, proj, nw, dout):
    t = proj.shape[0]
    n = SSM_INNER
    tr = _rows(t)

    def body(y_ref, z_ref, nw_ref, do_ref, dy_ref, dz_ref, dnw_ref):
        i = pl.program_id(0)
        z = z_ref[...]
        sg = _sigmoid(z)
        sz = z * sg
        ys = y_ref[...]
        y = ys * sz
        rstd = _group_rstd(y)
        yn = y * rstd
        dov = do_ref[...]
        dyn = dov * nw_ref[...]
        dyg = rstd * (dyn - yn * _group_mean(dyn * yn))
        dy_ref[...] = dyg * sz
        dz_ref[...] = (dyg * ys * (sg * (1.0 + z * (1.0 - sg)))).astype(BF16)
        part = jnp.sum(dov * yn, axis=0, keepdims=True)

        @pl.when(i == 0)
        def _():
            dnw_ref[...] = part

        @pl.when(i > 0)
        def _():
            dnw_ref[...] += part

    row = pl.BlockSpec((tr, n), lambda i: (i, 0))
    vec = pl.BlockSpec((1, n), lambda i: (0, 0))
    return pl.pallas_call(
        body, name="ssm_post_bwd", grid=(t // tr,),
        in_specs=[row, pl.BlockSpec((tr, n), lambda i: (i, COL_Z // n)), vec, row],
        out_specs=[row, row, vec],
        out_shape=[jax.ShapeDtypeStruct((t, n), F32), jax.ShapeDtypeStruct((t, n), BF16),
                   jax.ShapeDtypeStruct((1, n), F32)],
        compiler_params=_params(("arbitrary",)),
    )(y_ssd, proj, nw, dout)


SCAN_UNROLL = 8
GELU_C = math.sqrt(2.0 / math.pi)
GELU_K = 0.044715


def _gelu_parts(y):
    th = jnp.tanh(GELU_C * (y + GELU_K * y * y * y))
    val = 0.5 * y * (1.0 + th)
    grad = 0.5 * (1.0 + th) + 0.5 * y * (1.0 - th * th) * GELU_C * (1.0 + 3.0 * GELU_K * y * y)
    return val, grad


def _scan_tiles(a_ref, b_ref, h_ref, n_rows, reverse):
    n_tiles = n_rows // SUBLANES
    shape = (SUBLANES, a_ref.shape[1])
    row = _iota(shape, 0)

    def in_tile(av, bv):
        for s in (1, 2, 4):
            if reverse:
                keep = row < SUBLANES - s
                a_sh = jnp.where(keep, pltpu.roll(av, SUBLANES - s, 0), 1.0)
                b_sh = jnp.where(keep, pltpu.roll(bv, SUBLANES - s, 0), 0.0)
            else:
                keep = row >= s
                a_sh = jnp.where(keep, pltpu.roll(av, s, 0), 1.0)
                b_sh = jnp.where(keep, pltpu.roll(bv, s, 0), 0.0)
            bv = av * b_sh + bv
            av = av * a_sh
        return av, bv

    def step(k, carry):
        first = (n_tiles // SCAN_UNROLL - 1 - k) if reverse else k
        tiles = [first * SCAN_UNROLL + j for j in range(SCAN_UNROLL)]
        if reverse:
            tiles = tiles[::-1]
        ats = [pl.ds(pl.multiple_of(tile * SUBLANES, SUBLANES), SUBLANES) for tile in tiles]
        scanned = [in_tile(a_ref[at, :], b_ref[at, :]) for at in ats]
        for at, (av, bv) in zip(ats, scanned):
            hv = bv + av * carry
            h_ref[at, :] = hv
            carry = hv[0:1, :] if reverse else hv[SUBLANES - 1:SUBLANES, :]
        return carry

    assert n_tiles % SCAN_UNROLL == 0, n_rows
    lax.fori_loop(0, n_tiles // SCAN_UNROLL, step, jnp.zeros((1, a_ref.shape[1]), F32))


def _lru_gates(xl, cw, cb, wr, br, wi, bi, lam):
    u = cb + cw[CONV_K - 1:CONV_K, :] * xl
    for k in range(CONV_K - 1):
        u = u + cw[k:k + 1, :] * _shift_down(xl, CONV_K - 1 - k)
    r = _sigmoid(_bdot(u, wr) + br)
    i = _sigmoid(_bdot(u, wi) + bi)
    sp = _softplus(-lam)
    la = -LRU_C * r * sp
    a = jnp.exp(la)
    mult = jnp.sqrt(-jnp.tanh(la) * (a * a + 1.0))
    return u, r, i, sp, a, mult


def _lru_specs(t):
    c_x = COL_LRU_X // LANES
    c_y = COL_LRU_Y // LANES
    xl = pl.BlockSpec((t, LANES), lambda j: (0, c_x + j))
    yl = pl.BlockSpec((t, LANES), lambda j: (0, c_y + j))
    col = pl.BlockSpec((t, LANES), lambda j: (0, j))
    cw = pl.BlockSpec((CONV_K, LANES), lambda j: (0, j))
    vec = pl.BlockSpec((1, LANES), lambda j: (0, j))
    wblk = pl.BlockSpec((1, LANES, LANES), lambda j: (j, 0, 0))
    return xl, yl, col, cw, vec, wblk


def _lru_fwd(proj, cw, cb, wr, br, wi, bi, lam, gather):
    t = proj.shape[0]
    xl_s, yl_s, col, cw_s, vec, wblk = _lru_specs(t)
    n = len(gather)

    def body(*refs):
        xl_ref, yl_ref, cw_ref, cb_ref, wr_ref, br_ref, wi_ref, bi_ref, lam_ref = refs[:9]
        o_ref, h_ref = refs[9 + n:11 + n]
        a_s, b_s = refs[11 + 2 * n:13 + 2 * n]
        comm = (refs[9:9 + n], refs[11 + n:11 + 2 * n]) + tuple(refs[13 + 2 * n:])
        j = pl.program_id(0)
        for step, phase in ((0, "start"), (LRU_BLOCKS // 2, "forward")):
            @pl.when(j == step)
            def _():
                _gather_phase(phase, *comm)

        u, r, i, sp, a, mult = _lru_gates(xl_ref[...], cw_ref[...], cb_ref[...], wr_ref[0], br_ref[...],
                                          wi_ref[0], bi_ref[...], lam_ref[...])
        a_s[...] = a
        b_s[...] = mult * (i * u)
        _scan_tiles(a_s, b_s, h_ref, t, reverse=False)
        o_ref[...] = (h_ref[...] * _gelu_parts(yl_ref[...])[0]).astype(BF16)

        @pl.when(j == LRU_BLOCKS - 1)
        def _():
            _gather_phase("finish", *comm)

    out = pl.pallas_call(
        body, name="lru_fwd", grid=(LRU_BLOCKS,),
        in_specs=[xl_s, yl_s, cw_s, vec, wblk, vec, wblk, vec, vec] + [HBM] * n,
        out_specs=[col, col] + [HBM] * n,
        out_shape=[jax.ShapeDtypeStruct((t, LRU_WIDTH), BF16), jax.ShapeDtypeStruct((t, LRU_WIDTH), F32)]
        + [jax.ShapeDtypeStruct((N_DEV,) + v.shape, v.dtype) for v in gather],
        scratch_shapes=[pltpu.VMEM((t, LANES), F32)] * 2 + _comm_scratch(n),
        compiler_params=_params(("arbitrary",), big=True),
    )(proj, proj, cw, cb, wr, br, wi, bi, lam, *gather)
    return out[:2], out[2:]


def _lru_bwd(proj, cw, cb, wr, br, wi, bi, lam, h_all, dout, exchange):
    t = proj.shape[0]
    xl_s, yl_s, col, cw_s, vec, wblk = _lru_specs(t)
    n_ex = len(exchange)

    def body(*refs):
        xl_ref, yl_ref, cw_ref, cb_ref, wr_ref, br_ref, wi_ref, bi_ref, lam_ref, h_ref, do_ref = refs[:11]
        (dxl_ref, dyl_ref, dcw_ref, dcb_ref, dwr_ref, dbr_ref, dwi_ref, dbi_ref,
         dlam_ref) = refs[11 + n_ex:20 + n_ex]
        a_s, b_s, g_s = refs[20 + 2 * n_ex:23 + 2 * n_ex]
        comm = (refs[11:11 + n_ex], refs[20 + n_ex:20 + 2 * n_ex]) + tuple(refs[23 + 2 * n_ex:])

        @pl.when(pl.program_id(0) == 0)
        def _():
            _exchange_phase("start", *comm)

        xl = xl_ref[...]
        cwv = cw_ref[...]
        lam = lam_ref[...]
        u, r, i, sp, a, mult = _lru_gates(xl, cwv, cb_ref[...], wr_ref[0], br_ref[...], wi_ref[0], bi_ref[...], lam)
        v = i * u
        gl, dgl = _gelu_parts(yl_ref[...])
        dov = do_ref[...]
        h = h_ref[...]
        dyl_ref[...] = (dov * h * dgl).astype(BF16)
        b_s[...] = dov * gl
        a_s[...] = _shift_up(a, 1)
        _scan_tiles(a_s, b_s, g_s, t, reverse=True)
        g = g_s[...]
        da = g * _shift_down(h, 1)
        dmult = g * v
        dv = g * mult
        dla = da * a - dmult * (a * a) / mult
        dr = dla * (-LRU_C * sp)
        dsp = jnp.sum(dla * (-LRU_C * r), axis=0, keepdims=True)
        dlam_ref[...] = -dsp * _sigmoid(-lam)
        dpr = dr * r * (1.0 - r)
        dpi = dv * u * i * (1.0 - i)
        dbr_ref[...] = jnp.sum(dpr, axis=0, keepdims=True)
        dbi_ref[...] = jnp.sum(dpi, axis=0, keepdims=True)
        dwr_ref[0] = _bdot(u, dpr, TN)
        dwi_ref[0] = _bdot(u, dpi, TN)
        du = dv * i + _bdot(dpr, wr_ref[0], NT) + _bdot(dpi, wi_ref[0], NT)
        dxl = cwv[CONV_K - 1:CONV_K, :] * du
        for k in range(CONV_K - 1):
            dxl = dxl + cwv[k:k + 1, :] * _shift_up(du, CONV_K - 1 - k)
        dxl_ref[...] = dxl.astype(BF16)
        for k in range(CONV_K):
            dcw_ref[k:k + 1, :] = jnp.sum(du * _shift_down(xl, CONV_K - 1 - k), axis=0, keepdims=True)
        dcb_ref[...] = jnp.sum(du, axis=0, keepdims=True)

        @pl.when(pl.program_id(0) == LRU_BLOCKS - 1)
        def _():
            _exchange_phase("finish", *comm)

    out = pl.pallas_call(
        body, name="lru_bwd", grid=(LRU_BLOCKS,),
        in_specs=[xl_s, yl_s, cw_s, vec, wblk, vec, wblk, vec, vec, col, col] + [HBM] * n_ex,
        out_specs=[col, col, cw_s, vec, wblk, vec, wblk, vec, vec] + [HBM] * n_ex,
        out_shape=[jax.ShapeDtypeStruct((t, LRU_WIDTH), BF16), jax.ShapeDtypeStruct((t, LRU_WIDTH), BF16),
                   jax.ShapeDtypeStruct((CONV_K, LRU_WIDTH), F32), jax.ShapeDtypeStruct((1, LRU_WIDTH), F32),
                   jax.ShapeDtypeStruct((LRU_BLOCKS, LANES, LANES), F32), jax.ShapeDtypeStruct((1, LRU_WIDTH), F32),
                   jax.ShapeDtypeStruct((LRU_BLOCKS, LANES, LANES), F32), jax.ShapeDtypeStruct((1, LRU_WIDTH), F32),
                   jax.ShapeDtypeStruct((1, LRU_WIDTH), F32)]
        + [jax.ShapeDtypeStruct(p.shape, p.dtype) for p in exchange],
        scratch_shapes=[pltpu.VMEM((t, LANES), F32)] * 3 + _comm_scratch(n_ex),
        compiler_params=_params(("arbitrary",), big=True),
    )(proj, proj, cw, cb, wr, br, wi, bi, lam, h_all, dout, *exchange)
    return out[:9], out[9:]


def _mesh_pos():
    return lax.axis_index("x"), lax.axis_index("y"), lax.axis_index("c")


HBM = pl.BlockSpec(memory_space=pl.ANY)


def _comm_scratch(n):
    return [pltpu.SemaphoreType.DMA((n, 7)), pltpu.SemaphoreType.DMA((n, 7)), pltpu.SemaphoreType.DMA((n,))]


def _gather_phase(phase, v_refs, out_refs, send_sems, recv_sems, local_sems):
    n = len(v_refs)
    x, y, c = _mesh_pos()
    me, sibling = (x, y, c), (x, y, 1 - c)
    chips = [(1 - x, y), (x, 1 - y), (1 - x, 1 - y)]

    def block(a, px, py, pc):
        return out_refs[a].at[4 * px + 2 * py + pc]

    def copy(a, k, blk, to, src=None):
        return pltpu.make_async_remote_copy(
            src_ref=block(a, *blk) if src is None else src, dst_ref=block(a, *blk),
            send_sem=send_sems.at[a, k], recv_sem=recv_sems.at[a, k], device_id=to, device_id_type=MESH)

    def own(a):
        return pltpu.make_async_copy(v_refs[a], block(a, *me), local_sems.at[a])

    def first(a):
        return ([copy(a, 0, me, sibling, src=v_refs[a])]
                + [copy(a, 1 + j, me, (*chip, c), src=v_refs[a]) for j, chip in enumerate(chips)])

    def forward(a, j):
        return copy(a, 4 + j, (*chips[j], c), sibling)

    if phase == "start":
        for a in range(n):
            own(a).start()
        for a in range(n):
            for cp in first(a):
                cp.start()
    elif phase == "forward":
        for j in range(3):
            for a in range(n):
                copy(a, 1 + j, (*chips[j], c), me).wait_recv()
                forward(a, j).start()
    else:
        for a in range(n):
            copy(a, 0, sibling, me).wait_recv()
            for j in range(3):
                copy(a, 4 + j, (*chips[j], 1 - c), me).wait_recv()
        for a in range(n):
            for cp in first(a) + [forward(a, j) for j in range(3)]:
                cp.wait_send()
            own(a).wait()


def _all_gather(vs, name):
    n = len(vs)

    def body(*refs):
        comm = (refs[:n], refs[n:2 * n]) + tuple(refs[2 * n:])
        for phase in ("start", "forward", "finish"):
            _gather_phase(phase, *comm)

    return pl.pallas_call(
        body, name=name,
        out_shape=[jax.ShapeDtypeStruct((N_DEV,) + v.shape, v.dtype) for v in vs],
        in_specs=[HBM] * n, out_specs=[HBM] * n, scratch_shapes=_comm_scratch(n),
    )(*vs)


def _exchange(parts, name):
    n = len(parts)

    def body(*refs):
        comm = (refs[:n], refs[n:2 * n]) + tuple(refs[2 * n:])
        _exchange_phase("start", *comm)
        _exchange_phase("finish", *comm)

    return pl.pallas_call(
        body, name=name,
        out_shape=[jax.ShapeDtypeStruct(p.shape, p.dtype) for p in parts],
        in_specs=[HBM] * n, out_specs=[HBM] * n, scratch_shapes=_comm_scratch(n),
    )(*parts)


def _exchange_phase(phase, p_refs, out_refs, send_sems, recv_sems, local_sems):
    n = len(p_refs)
    x, y, c = _mesh_pos()
    me = 4 * x + 2 * y + c
    local = [pltpu.make_async_copy(p_refs[a].at[me], out_refs[a].at[me], local_sems.at[a]) for a in range(n)]
    remote = []
    for k in range(1, N_DEV):
        px = (1 - x) if k & 4 else x
        py = (1 - y) if k & 2 else y
        pc = (1 - c) if k & 1 else c
        for a in range(n):
            remote.append(pltpu.make_async_remote_copy(
                src_ref=p_refs[a].at[4 * px + 2 * py + pc], dst_ref=out_refs[a].at[me],
                send_sem=send_sems.at[a, k - 1], recv_sem=recv_sems.at[a, k - 1],
                device_id=(px, py, pc), device_id_type=MESH))
    if phase == "start":
        for cp in local + remote:
            cp.start()
    else:
        for cp in remote:
            cp.wait()
        for cp in local:
            cp.wait()


def _sum_sources(recv, tile, name):
    n, rows, width = recv.shape

    def body(r_ref, o_ref):
        acc = r_ref[0].astype(F32)
        for s in range(1, n):
            acc = acc + r_ref[s].astype(F32)
        o_ref[...] = acc

    return pl.pallas_call(
        body, name=name, grid=(rows // tile,),
        in_specs=[pl.BlockSpec((n, tile, width), lambda i: (0, i, 0))],
        out_specs=pl.BlockSpec((tile, width), lambda i: (i, 0)),
        out_shape=jax.ShapeDtypeStruct((rows, width), F32),
        compiler_params=_params(("parallel",)),
    )(recv)


def _row_tile(rows):
    for tile in range(128, 15, -16):
        if rows % tile == 0:
            return tile
    return rows


def _adamw(w, recv, m, v, name):
    rows, width = w.shape
    n = recv.shape[0]
    if rows % 16 == 0 or width % 256:
        tr, tc = _row_tile(rows), width
    else:
        tr, tc = rows, 256
    c1 = 1.0 - ADAM_B1 ** ADAM_STEP
    c2 = 1.0 - ADAM_B2 ** ADAM_STEP

    def body(w_ref, r_ref, m_ref, v_ref, g_ref, d_ref, nm_ref, nv_ref):
        gv = r_ref[0].astype(F32)
        for s in range(1, n):
            gv = gv + r_ref[s].astype(F32)
        nm = ADAM_B1 * m_ref[...] + (1.0 - ADAM_B1) * gv
        nv = ADAM_B2 * v_ref[...] + (1.0 - ADAM_B2) * (gv * gv)
        g_ref[...] = gv
        nm_ref[...] = nm
        nv_ref[...] = nv
        d_ref[...] = -ADAM_LR * ((nm / c1) / (jnp.sqrt(nv / c2) + ADAM_EPS) + ADAM_WD * w_ref[...])

    spec = pl.BlockSpec((tr, tc), lambda i, j: (i, j))
    shape = jax.ShapeDtypeStruct((rows, width), F32)
    return pl.pallas_call(
        body, name=name, grid=(rows // tr, width // tc),
        in_specs=[spec, pl.BlockSpec((n, tr, tc), lambda i, j: (0, i, j)), spec, spec],
        out_specs=[spec] * 4, out_shape=[shape] * 4,
        compiler_params=_params(("parallel", "parallel")),
    )(w, recv, m, v)


BIG_NAMES = ("w_in", "w_out_ssm", "w_out_lru", "w_out", "w_ffn_in", "w_ffn_out", "ssm_conv_w", "lru_conv_w")
TRANSPOSED = ("w_in", "w_ffn_in")
CONV_NAMES = ("ssm_conv_w", "lru_conv_w")
MATMUL_NAMES = BIG_NAMES[:6]
NEEDED_FIRST = ("w_in", "ssm_conv_w", "lru_conv_w")
GATHERED_IN_SSD = ("w_ffn_in", "w_ffn_out")
GATHERED_IN_LRU = ("w_out_ssm", "w_out_lru", "w_out")
SMALL_NAMES = ("norm1_w", "b_branch_gate", "ssm_conv_b", "ssm_dt_bias", "ssm_a_log", "ssm_d", "ssm_norm_w",
               "lru_conv_b", "lru_w_r", "lru_b_r", "lru_w_i", "lru_b_i", "lru_lambda", "norm2_w", "norm_f_w")


def _pack(arrays, total):
    flat = []
    used = 0
    for a in arrays:
        a = a.reshape(-1)
        pad = (-a.shape[0]) % LANES
        flat.append(a)
        if pad:
            flat.append(jnp.zeros((pad,), a.dtype))
        used += a.shape[0] + pad
    assert used <= total, (used, total)
    if total > used:
        flat.append(jnp.zeros((total - used,), arrays[0].dtype))
    return jnp.concatenate(flat)


def _unpack(flat, shapes):
    out = []
    off = 0
    for shp in shapes:
        n = math.prod(shp)
        out.append(flat[off:off + n].reshape(shp))
        off += n + (-n) % LANES
    return out


def _col_shards(full):
    rows, cols = full.shape
    return full.reshape(rows, N_DEV, cols // N_DEV).transpose(1, 0, 2)


def _from_col_shards(g):
    n, rows, w = g.shape
    return g.transpose(1, 0, 2).reshape(rows, n * w)


def kernel(x, norm1_w, w_in, b_branch_gate, ssm_conv_w, ssm_conv_b, ssm_dt_bias, ssm_a_log, ssm_d, ssm_norm_w, w_out_ssm, lru_conv_w, lru_conv_b, lru_w_r, lru_b_r, lru_w_i, lru_b_i, lru_lambda, w_out_lru, w_out, norm2_w, w_ffn_in, w_ffn_out, norm_f_w, loss_target, m_norm1_w, m_w_in, m_b_branch_gate, m_ssm_conv_w, m_ssm_conv_b, m_ssm_dt_bias, m_ssm_a_log, m_ssm_d, m_ssm_norm_w, m_w_out_ssm, m_lru_conv_w, m_lru_conv_b, m_lru_w_r, m_lru_b_r, m_lru_w_i, m_lru_b_i, m_lru_lambda, m_w_out_lru, m_w_out, m_norm2_w, m_w_ffn_in, m_w_ffn_out, m_norm_f_w, v_norm1_w, v_w_in, v_b_branch_gate, v_ssm_conv_w, v_ssm_conv_b, v_ssm_dt_bias, v_ssm_a_log, v_ssm_d, v_ssm_norm_w, v_w_out_ssm, v_lru_conv_w, v_lru_conv_b, v_lru_w_r, v_lru_b_r, v_lru_w_i, v_lru_b_i, v_lru_lambda, v_w_out_lru, v_w_out, v_norm2_w, v_w_ffn_in, v_w_ffn_out, v_norm_f_w):
    given = dict(locals())
    weights = {n: given[n] for n in BIG_NAMES + SMALL_NAMES}
    t = x.shape[1]
    xt = x[0]
    tgt = loss_target[0]

    def local(n, a):
        return a[0].T if n in TRANSPOSED else a[0]

    def as_output(n, a):
        return a.T[None] if n in TRANSPOSED else a[None]

    def shard(n):
        s = local(n, weights[n])
        return s.astype(BF16) if n in MATMUL_NAMES else s

    def unshard(n, g):
        return _from_col_shards(g) if n in CONV_NAMES else g.reshape(-1, g.shape[-1])

    def grad_slices(n):
        g = grads[n]
        return (_col_shards(g) if n in CONV_NAMES else g.reshape(N_DEV, -1, g.shape[-1])).astype(BF16)

    gathered = _all_gather([shard(n) for n in NEEDED_FIRST], "gather_in_weights")
    full = {n: unshard(n, g) for n, g in zip(NEEDED_FIRST, gathered)}
    ssm_cw, lru_cw = full["ssm_conv_w"], full["lru_conv_w"]
    wi_t = full["w_in"]
    w_pt = jnp.concatenate([wi_t[:ORIG_DT], wi_t[ORIG_LRU:], wi_t[ORIG_DT:ORIG_LRU],
                            jnp.zeros((PROJ_W - IN_PROJ, D_MODEL), BF16)], axis=0)

    def pad_heads(a):
        return jnp.pad(a.reshape(1, SSM_HEADS), ((0, 0), (0, LANES - SSM_HEADS)))

    dt_bias_p = pad_heads(ssm_dt_bias)
    a_log_p = pad_heads(ssm_a_log)
    d_exp = jnp.repeat(ssm_d.reshape(SSM_HEADS), SSM_HEAD_DIM).reshape(1, SSM_INNER)
    lru_wr, lru_wi = lru_w_r[0], lru_w_i[0]

    hn1 = _rmsnorm_fwd(xt, norm1_w, "norm1_fwd")
    proj = _mm(hn1, w_pt, tb=True, name="in_proj")
    xbc_act = _ssm_conv_fwd(proj, ssm_cw, ssm_conv_b)
    (y_ssd, s_in_all), gathered = _ssd_fwd(xbc_act, proj, dt_bias_p, a_log_p, d_exp,
                                           gather=[shard(n) for n in GATHERED_IN_SSD])
    full.update({n: unshard(n, g) for n, g in zip(GATHERED_IN_SSD, gathered)})
    (l_out, h_lru), gathered = _lru_fwd(proj, lru_cw, lru_conv_b, lru_wr, lru_b_r, lru_wi, lru_b_i, lru_lambda,
                                        gather=[shard(n) for n in GATHERED_IN_LRU])
    full.update({n: unshard(n, g) for n, g in zip(GATHERED_IN_LRU, gathered)})
    y_pre = _ssm_post_fwd(y_ssd, proj, ssm_norm_w)
    y_ssm = _mm(y_pre, full["w_out_ssm"], name="out_ssm")
    y_lru = _mm(l_out, full["w_out_lru"], name="out_lru")
    merged = _merge_fwd(proj, b_branch_gate, y_ssm, y_lru)
    h1 = _mm(merged, full["w_out"], add=xt, name="out_proj")
    hn2 = _rmsnorm_fwd(h1, norm2_w, "norm2_fwd")
    gu = _mm(hn2, full["w_ffn_in"], tb=True, name="ffn_in")
    act = _swiglu_fwd(gu)
    h2 = _mm(act, full["w_ffn_out"], add=h1, name="ffn_out")

    grads = {}
    dh2, grads["norm_f_w"], loss_cols = _loss_head(h2, norm_f_w.reshape(1, D_MODEL), tgt)
    loss = lax.psum(0.5 * jnp.sum(loss_cols) / D_MODEL, AXES)
    dact = _mm(dh2, full["w_ffn_out"], tb=True, name="d_act")
    grads["w_ffn_out"] = _mm(act, dh2, ta=True, out_dtype=BF16, name="dw_ffn_out")
    dgu = _swiglu_bwd(gu, dact)
    dhn2 = _mm(dgu, full["w_ffn_in"], name="d_hn2")
    grads["w_ffn_in"] = _mm(dgu, hn2, ta=True, out_dtype=BF16, name="dw_ffn_in")
    dh1, grads["norm2_w"] = _rmsnorm_bwd(h1, norm2_w, dhn2, dh2, "norm2_bwd")
    dmerged = _mm(dh1, full["w_out"], tb=True, name="d_merged")
    grads["w_out"] = _mm(merged, dh1, ta=True, out_dtype=BF16, name="dw_out")
    dy_ssm, dy_lru, dgates, grads["b_branch_gate"] = _merge_bwd(proj, b_branch_gate, y_ssm, y_lru, dmerged)
    dy_pre = _mm(dy_ssm, full["w_out_ssm"], tb=True, name="d_y_pre")
    grads["w_out_ssm"] = _mm(y_pre, dy_ssm, ta=True, out_dtype=BF16, name="dw_out_ssm")
    dl_out = _mm(dy_lru, full["w_out_lru"], tb=True, name="d_l_out")
    grads["w_out_lru"] = _mm(l_out, dy_lru, ta=True, out_dtype=BF16, name="dw_out_lru")
    dy_ssd, dz, grads["ssm_norm_w"] = _ssm_post_bwd(y_ssd, proj, ssm_norm_w, dy_pre)
    (dxbc_act, ddt, dbias, dalog, ddcol), recv_in_ssd = _ssd_bwd(
        xbc_act, proj, s_in_all, dy_ssd, dt_bias_p, a_log_p, d_exp,
        exchange=[grad_slices(n) for n in GATHERED_IN_SSD])
    grads["ssm_dt_bias"] = dbias[:, :SSM_HEADS]
    grads["ssm_a_log"] = dalog[:, :SSM_HEADS]
    grads["ssm_d"] = ddcol.reshape(SSM_HEADS, SSM_HEAD_DIM).sum(axis=1).reshape(1, SSM_HEADS)
    dxbc, grads["ssm_conv_w"], grads["ssm_conv_b"] = _ssm_conv_bwd(proj, ssm_cw, ssm_conv_b, dxbc_act)
    ((dxl, dyl, grads["lru_conv_w"], grads["lru_conv_b"], dwr, grads["lru_b_r"], dwi, grads["lru_b_i"],
      grads["lru_lambda"]), recv_in_lru) = _lru_bwd(
        proj, lru_cw, lru_conv_b, lru_wr, lru_b_r, lru_wi, lru_b_i, lru_lambda, h_lru, dl_out,
        exchange=[grad_slices(n) for n in GATHERED_IN_LRU])
    grads["lru_w_r"], grads["lru_w_i"] = dwr[None], dwi[None]
    dproj = jnp.concatenate([dgates, dz, dxbc, dxl, dyl, ddt], axis=1)
    dwpt = _mm(dproj, hn1, ta=True, out_dtype=BF16, name="dw_in")
    grads["w_in"] = jnp.concatenate([dwpt[:ORIG_DT], dwpt[COL_DT:COL_DT + SSM_HEADS], dwpt[COL_LRU_X:COL_DT]], axis=0)
    dhn1, recv_first = _mm(dproj, w_pt, exchange=[grad_slices(n) for n in NEEDED_FIRST], name="d_hn1")
    grad_x, grads["norm1_w"] = _rmsnorm_bwd(xt, norm1_w, dhn1, dh1, "norm1_bwd")

    small_total = N_DEV * SMALL_ROWS * LANES
    small_parts = _pack([grads[n] for n in SMALL_NAMES], small_total).reshape(N_DEV, SMALL_ROWS, LANES)
    small_g = _sum_sources(_exchange([small_parts], "exchange_small_grads")[0], SMALL_ROWS, "sum_small_grads")
    small_g = _all_gather([small_g], "gather_small_grads")[0].reshape(1, small_total // FLAT_W, FLAT_W)

    recv = dict(zip(GATHERED_IN_SSD + GATHERED_IN_LRU + NEEDED_FIRST,
                    list(recv_in_ssd) + list(recv_in_lru) + list(recv_first)))
    big_out = {n: _adamw(local(n, weights[n]), recv[n], local(n, given["m_" + n]), local(n, given["v_" + n]),
                         "adamw_" + n) for n in BIG_NAMES}
    small_state = [_pack([given[p + n] for n in SMALL_NAMES], small_total).reshape(-1, FLAT_W) for p in ("", "m_", "v_")]
    small_out = _adamw(small_state[0], small_g, small_state[1], small_state[2], "adamw_replicated")

    small_shapes = [weights[n].shape for n in SMALL_NAMES]
    order = list(given)[1:24]
    results = []
    for q in range(4):
        vals = {n: as_output(n, big_out[n][q]) for n in BIG_NAMES}
        vals.update(zip(SMALL_NAMES, _unpack(small_out[q].reshape(-1), small_shapes)))
        results.extend(vals[n] for n in order)
    return (loss, grad_x[None], *results)
```

```python
import math

import jax
import jax.numpy as jnp
from jax import lax
from jax.experimental import pallas as pl
from jax.experimental.pallas import tpu as pltpu

F32 = jnp.float32
BF16 = jnp.bfloat16
HIGHEST = lax.Precision.HIGHEST
MESH = pl.DeviceIdType.MESH
AXES = ("x", "y", "c")
N_DEV = 8

D_MODEL = 1024
SSM_INNER = 2048
SSM_HEADS = 32
SSM_HEAD_DIM = 64
SSM_GROUPS = 4
SSM_STATE = 128
SSM_BC = SSM_GROUPS * SSM_STATE
SSM_CONV_DIM = SSM_INNER + 2 * SSM_BC
SSM_CHUNK = 128
SSM_PAIRS = SSM_HEADS // 2
CONV_K = 4
LRU_WIDTH = 1280
LRU_BLOCKS = 10
LRU_C = 8.0
FFN_HIDDEN = 2816
RMS_EPS = 1e-6
IN_PROJ = 9760

COL_GATES = 0
COL_Z = 2048
COL_XBC = 4096
COL_LRU_X = 7168
COL_LRU_Y = 8448
COL_DT = 9728
PROJ_W = 9856
ORIG_DT = 7168
ORIG_LRU = 7200

ADAM_LR = 0.001
ADAM_B1 = 0.9
ADAM_B2 = 0.999
ADAM_EPS = 1e-08
ADAM_WD = 0.01
ADAM_STEP = 10

LANES = 128
SUBLANES = 8
V7X_VMEM_BYTES = 64 * 1024 * 1024
VMEM_LIMIT = V7X_VMEM_BYTES * 3 // 4
VMEM_LIMIT_BIG = V7X_VMEM_BYTES * 15 // 16

NT = (((1,), (1,)), ((), ()))
TN = (((0,), (0,)), ((), ()))


def _params(sem=None, big=False):
    return pltpu.CompilerParams(dimension_semantics=sem,
                                vmem_limit_bytes=VMEM_LIMIT_BIG if big else VMEM_LIMIT)


def _blk(dim, cap):
    if dim <= cap:
        return dim
    for m in range(cap // LANES, 0, -1):
        if dim % (m * LANES) == 0:
            return m * LANES
    raise ValueError(f"no block for {dim}")


def _rows(t):
    return min(t, 256)


def _sigmoid(v):
    return 1.0 / (1.0 + jnp.exp(-v))


def _softplus(v):
    e = jnp.exp(-jnp.abs(v))
    u = 1.0 + e
    log1p = jnp.where(u == 1.0, e, jnp.log(u) * e / jnp.where(u == 1.0, 1.0, u - 1.0))
    return jnp.maximum(v, 0.0) + log1p


def _iota(shape, dim):
    return lax.broadcasted_iota(jnp.int32, shape, dim)


def _shift_down(v, s):
    if s == 0:
        return v
    return jnp.where(_iota(v.shape, 0) >= s, pltpu.roll(v, s, 0), 0.0)


def _shift_up(v, s):
    if s == 0:
        return v
    n = v.shape[0]
    return jnp.where(_iota(v.shape, 0) < n - s, pltpu.roll(v, n - s, 0), 0.0)


def _bdot(a, b, dn=None):
    a = a.astype(BF16)
    b = b.astype(BF16)
    if dn is None:
        return jnp.dot(a, b, preferred_element_type=F32)
    return lax.dot_general(a, b, dn, preferred_element_type=F32)


def _split_dot(a, e, dn=None):
    hi = a.astype(BF16)
    lo = (a - hi.astype(F32)).astype(BF16)
    return _bdot(hi, e, dn) + _bdot(lo, e, dn)


def _fdot(a, b, dn=None):
    if dn is None:
        return jnp.dot(a, b, precision=HIGHEST, preferred_element_type=F32)
    return lax.dot_general(a, b, dn, precision=HIGHEST, preferred_element_type=F32)


def _mm(a, b, *, ta=False, tb=False, add=None, exchange=(), out_dtype=F32, name):
    if ta:
        kdim, m = a.shape
    else:
        m, kdim = a.shape
    if tb:
        n, k2 = b.shape
    else:
        k2, n = b.shape
    assert kdim == k2, (a.shape, b.shape, ta, tb)
    bm, bn, bk = _blk(m, 1024), _blk(n, 1408), _blk(kdim, 1408)
    grid = (m // bm, n // bn, kdim // bk)
    nk = grid[2]
    dn = (((0 if ta else 1,), (1 if tb else 0,)), ((), ()))
    n_in = 2 if add is None else 3
    n_ex = len(exchange)

    def body(*refs):
        a_ref, b_ref = refs[:2]
        r_ref = None if add is None else refs[2]
        o_ref = refs[n_in + n_ex]
        acc = refs[n_in + 2 * n_ex + 1]
        comm = (refs[n_in:n_in + n_ex], refs[n_in + n_ex + 1:n_in + 2 * n_ex + 1]) + tuple(refs[n_in + 2 * n_ex + 2:])
        step = (pl.program_id(0) * grid[1] + pl.program_id(1)) * nk + pl.program_id(2)
        k = pl.program_id(2)
        if n_ex:
            @pl.when(step == 0)
            def _():
                _exchange_phase("start", *comm)

        @pl.when(k == 0)
        def _():
            acc[...] = jnp.zeros_like(acc)

        acc[...] += lax.dot_general(a_ref[...].astype(BF16), b_ref[...].astype(BF16), dn,
                                    preferred_element_type=F32)

        @pl.when(k == nk - 1)
        def _():
            r = acc[...]
            if add is not None:
                r = r + r_ref[...]
            o_ref[...] = r.astype(out_dtype)

        if n_ex:
            @pl.when(step == grid[0] * grid[1] * nk - 1)
            def _():
                _exchange_phase("finish", *comm)

    a_spec = pl.BlockSpec((bk, bm), lambda i, j, k: (k, i)) if ta else pl.BlockSpec((bm, bk), lambda i, j, k: (i, k))
    b_spec = pl.BlockSpec((bn, bk), lambda i, j, k: (j, k)) if tb else pl.BlockSpec((bk, bn), lambda i, j, k: (k, j))
    o_spec = pl.BlockSpec((bm, bn), lambda i, j, k: (i, j))
    in_specs = [a_spec, b_spec] + ([o_spec] if add is not None else []) + [HBM] * n_ex
    args = (a, b) + ((add,) if add is not None else ()) + tuple(exchange)
    out = pl.pallas_call(
        body, name=name, grid=grid,
        in_specs=in_specs, out_specs=[o_spec] + [HBM] * n_ex,
        out_shape=[jax.ShapeDtypeStruct((m, n), out_dtype)]
        + [jax.ShapeDtypeStruct(p.shape, p.dtype) for p in exchange],
        scratch_shapes=[pltpu.VMEM((bm, bn), F32)] + (_comm_scratch(n_ex) if n_ex else []),
        compiler_params=_params(("arbitrary",) * 3 if n_ex else ("parallel", "parallel", "arbitrary")),
    )(*args)
    return (out[0], out[1:]) if n_ex else out[0]


def _rmsnorm_fwd(x, w, name):
    t, d = x.shape
    tr = _rows(t)

    def body(x_ref, w_ref, o_ref):
        xv = x_ref[...]
        rstd = lax.rsqrt(jnp.mean(xv * xv, axis=-1, keepdims=True) + RMS_EPS)
        o_ref[...] = (xv * rstd * w_ref[...]).astype(BF16)

    return pl.pallas_call(
        body, name=name, grid=(t // tr,),
        in_specs=[pl.BlockSpec((tr, d), lambda i: (i, 0)), pl.BlockSpec((1, d), lambda i: (0, 0))],
        out_specs=pl.BlockSpec((tr, d), lambda i: (i, 0)),
        out_shape=jax.ShapeDtypeStruct((t, d), BF16),
        compiler_params=_params(("parallel",)),
    )(x, w)


def _rmsnorm_bwd(x, w, dy, dres, name):
    t, d = x.shape
    tr = _rows(t)

    def body(x_ref, w_ref, dy_ref, dres_ref, dx_ref, dw_ref):
        i = pl.program_id(0)
        xv = x_ref[...]
        rstd = lax.rsqrt(jnp.mean(xv * xv, axis=-1, keepdims=True) + RMS_EPS)
        xhat = xv * rstd
        dyv = dy_ref[...]
        dxhat = dyv * w_ref[...]
        m = jnp.mean(dxhat * xhat, axis=-1, keepdims=True)
        dx_ref[...] = rstd * (dxhat - xhat * m) + dres_ref[...]
        part = jnp.sum(dyv * xhat, axis=0, keepdims=True)

        @pl.when(i == 0)
        def _():
            dw_ref[...] = part

        @pl.when(i > 0)
        def _():
            dw_ref[...] += part

    row = pl.BlockSpec((tr, d), lambda i: (i, 0))
    vec = pl.BlockSpec((1, d), lambda i: (0, 0))
    return pl.pallas_call(
        body, name=name, grid=(t // tr,),
        in_specs=[row, vec, row, row], out_specs=[row, vec],
        out_shape=[jax.ShapeDtypeStruct((t, d), F32), jax.ShapeDtypeStruct((1, d), F32)],
        compiler_params=_params(("arbitrary",)),
    )(x, w, dy, dres)


def _loss_head(h2, w, tgt):
    t, d = h2.shape
    tr = _rows(t)

    def body(x_ref, w_ref, t_ref, dx_ref, dw_ref, ls_ref):
        i = pl.program_id(0)
        xv = x_ref[...]
        wv = w_ref[...]
        rstd = lax.rsqrt(jnp.mean(xv * xv, axis=-1, keepdims=True) + RMS_EPS)
        xhat = xv * rstd
        err = xhat * wv - t_ref[...]
        dyv = err * (1.0 / d)
        dxhat = dyv * wv
        m = jnp.mean(dxhat * xhat, axis=-1, keepdims=True)
        dx_ref[...] = rstd * (dxhat - xhat * m)
        dw_part = jnp.sum(dyv * xhat, axis=0, keepdims=True)
        ls_part = jnp.sum(err * err, axis=0, keepdims=True)

        @pl.when(i == 0)
        def _():
            dw_ref[...] = dw_part
            ls_ref[...] = ls_part

        @pl.when(i > 0)
        def _():
            dw_ref[...] += dw_part
            ls_ref[...] += ls_part

    row = pl.BlockSpec((tr, d), lambda i: (i, 0))
    vec = pl.BlockSpec((1, d), lambda i: (0, 0))
    return pl.pallas_call(
        body, name="loss_head", grid=(t // tr,),
        in_specs=[row, vec, row], out_specs=[row, vec, vec],
        out_shape=[jax.ShapeDtypeStruct((t, d), F32), jax.ShapeDtypeStruct((1, d), F32),
                   jax.ShapeDtypeStruct((1, d), F32)],
        compiler_params=_params(("arbitrary",)),
    )(h2, w, tgt)


def _merge_fwd(proj, bg, ys, yl):
    t = proj.shape[0]
    d = D_MODEL
    tr = _rows(t)

    def body(ps_ref, pl_ref, bg_ref, ys_ref, yl_ref, o_ref):
        gs = _sigmoid(ps_ref[...] + bg_ref[:, 0:d])
        gl = _sigmoid(pl_ref[...] + bg_ref[:, d:2 * d])
        o_ref[...] = (gs * ys_ref[...] + gl * yl_ref[...]).astype(BF16)

    row = pl.BlockSpec((tr, d), lambda i: (i, 0))
    return pl.pallas_call(
        body, name="merge_fwd", grid=(t // tr,),
        in_specs=[row, pl.BlockSpec((tr, d), lambda i: (i, 1)), pl.BlockSpec((1, 2 * d), lambda i: (0, 0)), row, row],
        out_specs=row, out_shape=jax.ShapeDtypeStruct((t, d), BF16),
        compiler_params=_params(("parallel",)),
    )(proj, proj, bg, ys, yl)


def _merge_bwd(proj, bg, ys, yl, dm):
    t = proj.shape[0]
    d = D_MODEL
    tr = _rows(t)

    def body(ps_ref, pl_ref, bg_ref, ys_ref, yl_ref, dm_ref, dys_ref, dyl_ref, dg_ref, dbg_ref):
        i = pl.program_id(0)
        gs = _sigmoid(ps_ref[...] + bg_ref[:, 0:d])
        gl = _sigmoid(pl_ref[...] + bg_ref[:, d:2 * d])
        dmv = dm_ref[...]
        dys_ref[...] = (dmv * gs).astype(BF16)
        dyl_ref[...] = (dmv * gl).astype(BF16)
        dgs = dmv * ys_ref[...] * gs * (1.0 - gs)
        dgl = dmv * yl_ref[...] * gl * (1.0 - gl)
        dg_ref[:, 0:d] = dgs.astype(BF16)
        dg_ref[:, d:2 * d] = dgl.astype(BF16)

        @pl.when(i == 0)
        def _():
            dbg_ref[...] = jnp.zeros_like(dbg_ref)

        dbg_ref[:, 0:d] += jnp.sum(dgs, axis=0, keepdims=True)
        dbg_ref[:, d:2 * d] += jnp.sum(dgl, axis=0, keepdims=True)

    row = pl.BlockSpec((tr, d), lambda i: (i, 0))
    wide = pl.BlockSpec((tr, 2 * d), lambda i: (i, 0))
    vec = pl.BlockSpec((1, 2 * d), lambda i: (0, 0))
    return pl.pallas_call(
        body, name="merge_bwd", grid=(t // tr,),
        in_specs=[row, pl.BlockSpec((tr, d), lambda i: (i, 1)), vec, row, row, row],
        out_specs=[row, row, wide, vec],
        out_shape=[jax.ShapeDtypeStruct((t, d), BF16), jax.ShapeDtypeStruct((t, d), BF16),
                   jax.ShapeDtypeStruct((t, 2 * d), BF16), jax.ShapeDtypeStruct((1, 2 * d), F32)],
        compiler_params=_params(("arbitrary",)),
    )(proj, proj, bg, ys, yl, dm)


def _swiglu_fwd(gu):
    t = gu.shape[0]
    f = FFN_HIDDEN
    tr = _rows(t)

    def body(g_ref, u_ref, o_ref):
        g = g_ref[...]
        o_ref[...] = (g * _sigmoid(g) * u_ref[...]).astype(BF16)

    return pl.pallas_call(
        body, name="swiglu_fwd", grid=(t // tr,),
        in_specs=[pl.BlockSpec((tr, f), lambda i: (i, 0)), pl.BlockSpec((tr, f), lambda i: (i, 1))],
        out_specs=pl.BlockSpec((tr, f), lambda i: (i, 0)),
        out_shape=jax.ShapeDtypeStruct((t, f), BF16),
        compiler_params=_params(("parallel",)),
    )(gu, gu)


def _swiglu_bwd(gu, dact):
    t = gu.shape[0]
    f = FFN_HIDDEN
    tr = _rows(t)

    def body(g_ref, u_ref, da_ref, o_ref):
        g = g_ref[...]
        sg = _sigmoid(g)
        da = da_ref[...]
        o_ref[:, 0:f] = (da * u_ref[...] * (sg * (1.0 + g * (1.0 - sg)))).astype(BF16)
        o_ref[:, f:2 * f] = (da * g * sg).astype(BF16)

    return pl.pallas_call(
        body, name="swiglu_bwd", grid=(t // tr,),
        in_specs=[pl.BlockSpec((tr, f), lambda i: (i, 0)), pl.BlockSpec((tr, f), lambda i: (i, 1)),
                  pl.BlockSpec((tr, f), lambda i: (i, 0))],
        out_specs=pl.BlockSpec((tr, 2 * f), lambda i: (i, 0)),
        out_shape=jax.ShapeDtypeStruct((t, 2 * f), BF16),
        compiler_params=_params(("parallel",)),
    )(gu, gu, dact)


def _conv_pre(xv, wv, bv):
    pre = bv + wv[CONV_K - 1:CONV_K, :] * xv
    for k in range(CONV_K - 1):
        pre = pre + wv[k:k + 1, :] * _shift_down(xv, CONV_K - 1 - k)
    return pre


def _ssm_conv_fwd(proj, w, b):
    t = proj.shape[0]
    nb = SSM_CONV_DIM // LANES
    c0 = COL_XBC // LANES

    def body(x_ref, w_ref, b_ref, o_ref):
        pre = _conv_pre(x_ref[...], w_ref[...], b_ref[...])
        o_ref[...] = pre * _sigmoid(pre)

    return pl.pallas_call(
        body, name="ssm_conv_fwd", grid=(nb,),
        in_specs=[pl.BlockSpec((t, LANES), lambda j: (0, c0 + j)), pl.BlockSpec((CONV_K, LANES), lambda j: (0, j)),
                  pl.BlockSpec((1, LANES), lambda j: (0, j))],
        out_specs=pl.BlockSpec((t, LANES), lambda j: (0, j)),
        out_shape=jax.ShapeDtypeStruct((t, SSM_CONV_DIM), F32),
        compiler_params=_params(("parallel",)),
    )(proj, w, b)


def _ssm_conv_bwd(proj, w, b, dact):
    t = proj.shape[0]
    nb = SSM_CONV_DIM // LANES
    c0 = COL_XBC // LANES

    def body(x_ref, w_ref, b_ref, da_ref, dx_ref, dw_ref, db_ref):
        xv = x_ref[...]
        wv = w_ref[...]
        pre = _conv_pre(xv, wv, b_ref[...])
        sg = _sigmoid(pre)
        dpre = da_ref[...] * (sg * (1.0 + pre * (1.0 - sg)))
        dx = wv[CONV_K - 1:CONV_K, :] * dpre
        for k in range(CONV_K - 1):
            dx = dx + wv[k:k + 1, :] * _shift_up(dpre, CONV_K - 1 - k)
        dx_ref[...] = dx.astype(BF16)
        for k in range(CONV_K):
            dw_ref[k:k + 1, :] = jnp.sum(dpre * _shift_down(xv, CONV_K - 1 - k), axis=0, keepdims=True)
        db_ref[...] = jnp.sum(dpre, axis=0, keepdims=True)

    col = pl.BlockSpec((t, LANES), lambda j: (0, j))
    wsp = pl.BlockSpec((CONV_K, LANES), lambda j: (0, j))
    bsp = pl.BlockSpec((1, LANES), lambda j: (0, j))
    return pl.pallas_call(
        body, name="ssm_conv_bwd", grid=(nb,),
        in_specs=[pl.BlockSpec((t, LANES), lambda j: (0, c0 + j)), wsp, bsp, col],
        out_specs=[col, wsp, bsp],
        out_shape=[jax.ShapeDtypeStruct((t, SSM_CONV_DIM), BF16), jax.ShapeDtypeStruct((CONV_K, SSM_CONV_DIM), F32),
                   jax.ShapeDtypeStruct((1, SSM_CONV_DIM), F32)],
        compiler_params=_params(("parallel",)),
    )(proj, w, b, dact)


def _ssd_chunk_terms(dtr, bias, alog):
    a = -jnp.exp(alog)
    dt = _softplus(dtr + bias)
    row = _iota((SSM_CHUNK, SSM_CHUNK), 0)
    col = _iota((SSM_CHUNK, SSM_CHUNK), 1)
    tri = (row >= col).astype(F32)
    cs = _fdot(tri, dt * a)
    dec = jnp.exp(cs[SSM_CHUNK - 1:SSM_CHUNK, :] - cs)
    ecs = jnp.exp(cs)
    off = _iota((LANES, SSM_INNER), 1) - SSM_HEAD_DIM * _iota((LANES, SSM_INNER), 0)
    expand = jnp.where(jnp.logical_and(off >= 0, off < SSM_HEAD_DIM), 1.0, 0.0).astype(BF16)
    return a, dt, cs, dec, ecs, expand, row, col


def _ssd_specs(t):
    nc = t // SSM_CHUNK
    xs = pl.BlockSpec((SSM_CHUNK, SSM_INNER), lambda c: (c, 0))
    bm = pl.BlockSpec((SSM_CHUNK, SSM_BC), lambda c: (c, SSM_INNER // SSM_BC))
    cm = pl.BlockSpec((SSM_CHUNK, SSM_BC), lambda c: (c, SSM_INNER // SSM_BC + 1))
    dtr = pl.BlockSpec((SSM_CHUNK, LANES), lambda c: (c, COL_DT // LANES))
    vec = pl.BlockSpec((1, LANES), lambda c: (0, 0))
    wide = pl.BlockSpec((1, SSM_INNER), lambda c: (0, 0))
    return nc, xs, bm, cm, dtr, vec, wide


def _ssd_fwd(xbc_act, proj, bias, alog, dexp, gather):
    t = proj.shape[0]
    nc, xs_s, bm_s, cm_s, dtr_s, vec, wide = _ssd_specs(t)
    n = len(gather)

    def body(*refs):
        xs_ref, b_ref, c_ref, dtr_ref, bias_ref, alog_ref, dexp_ref = refs[:7]
        y_ref, sin_ref = refs[7 + n:9 + n]
        state = refs[9 + 2 * n]
        comm = (refs[7:7 + n], refs[9 + n:9 + 2 * n]) + tuple(refs[10 + 2 * n:])
        chunk = pl.program_id(0)

        @pl.when(chunk == 0)
        def _():
            _gather_phase("start", *comm)
            state[...] = jnp.zeros_like(state)

        @pl.when(chunk == nc // 2)
        def _():
            _gather_phase("forward", *comm)

        a, dt, cs, dec, ecs, expand, row, col = _ssd_chunk_terms(dtr_ref[...], bias_ref[...], alog_ref[...])
        cst = cs.T
        dt_x = _split_dot(dt, expand)
        dec_x = _split_dot(dec, expand)
        ecs_x = _split_dot(ecs, expand)
        xs = xs_ref[...]
        xdt = xs * dt_x
        xdec = xdt * dec_x
        lane_lo = col < SSM_HEAD_DIM
        causal = row >= col
        sin_ref[0] = state[...]
        for g in range(SSM_GROUPS):
            bg = b_ref[:, g * SSM_STATE:(g + 1) * SSM_STATE].astype(BF16)
            cg = c_ref[:, g * SSM_STATE:(g + 1) * SSM_STATE].astype(BF16)
            cb = _bdot(cg, bg, NT)
            for q in range(SSM_PAIRS // SSM_GROUPS):
                pq = g * (SSM_PAIRS // SSM_GROUPS) + q
                sl = slice(pq * LANES, (pq + 1) * LANES)
                xp = xdt[:, sl].astype(BF16)
                yd = []
                for hh in range(2):
                    h = 2 * pq + hh
                    lmat = jnp.exp(jnp.where(causal, cs[:, h:h + 1] - cst[h:h + 1, :], -jnp.inf))
                    yd.append(_bdot(cb * lmat, xp))
                s_in = state[pq]
                y_off = _bdot(cg, s_in) * ecs_x[:, sl]
                y_ref[:, sl] = jnp.where(lane_lo, yd[0], yd[1]) + y_off + xs[:, sl] * dexp_ref[:, sl]
                state[pq] = s_in * ecs_x[SSM_CHUNK - 1:SSM_CHUNK, sl] + _bdot(bg, xdec[:, sl], TN)

        @pl.when(chunk == nc - 1)
        def _():
            _gather_phase("finish", *comm)

    out = pl.pallas_call(
        body, name="ssd_fwd", grid=(nc,),
        in_specs=[xs_s, bm_s, cm_s, dtr_s, vec, vec, wide] + [HBM] * n,
        out_specs=[pl.BlockSpec((SSM_CHUNK, SSM_INNER), lambda c: (c, 0)),
                   pl.BlockSpec((1, SSM_PAIRS, SSM_STATE, LANES), lambda c: (c, 0, 0, 0))] + [HBM] * n,
        out_shape=[jax.ShapeDtypeStruct((t, SSM_INNER), F32),
                   jax.ShapeDtypeStruct((nc, SSM_PAIRS, SSM_STATE, LANES), F32)]
        + [jax.ShapeDtypeStruct((N_DEV,) + v.shape, v.dtype) for v in gather],
        scratch_shapes=[pltpu.VMEM((SSM_PAIRS, SSM_STATE, LANES), F32)] + _comm_scratch(n),
        compiler_params=_params(("arbitrary",)),
    )(xbc_act, xbc_act, xbc_act, proj, bias, alog, dexp, *gather)
    return out[:2], out[2:]


def _ssd_bwd(xbc_act, proj, s_in_all, dy, bias, alog, dexp, exchange):
    n_ex = len(exchange)
    t = proj.shape[0]
    nc = t // SSM_CHUNK
    last = nc - 1
    xs_s = pl.BlockSpec((SSM_CHUNK, SSM_INNER), lambda c: (last - c, 0))
    bm_s = pl.BlockSpec((SSM_CHUNK, SSM_BC), lambda c: (last - c, SSM_INNER // SSM_BC))
    cm_s = pl.BlockSpec((SSM_CHUNK, SSM_BC), lambda c: (last - c, SSM_INNER // SSM_BC + 1))
    dtr_s = pl.BlockSpec((SSM_CHUNK, LANES), lambda c: (last - c, COL_DT // LANES))
    sin_s = pl.BlockSpec((1, SSM_PAIRS, SSM_STATE, LANES), lambda c: (last - c, 0, 0, 0))
    vec = pl.BlockSpec((1, LANES), lambda c: (0, 0))
    wide = pl.BlockSpec((1, SSM_INNER), lambda c: (0, 0))

    def body(*refs):
        xs_ref, b_ref, c_ref, dtr_ref, sin_ref, dy_ref, bias_ref, alog_ref, dexp_ref = refs[:9]
        dxbc_ref, ddtr_ref, dbias_ref, dalog_ref, ddcol_ref = refs[9 + n_ex:14 + n_ex]
        dstate, dxdt_s, yoff_s, rx_s, trow_s = refs[14 + 2 * n_ex:19 + 2 * n_ex]
        comm = (refs[9:9 + n_ex], refs[14 + n_ex:14 + 2 * n_ex]) + tuple(refs[19 + 2 * n_ex:])

        @pl.when(pl.program_id(0) == 0)
        def _():
            _exchange_phase("start", *comm)
            dstate[...] = jnp.zeros_like(dstate)
            trow_s[...] = jnp.zeros_like(trow_s)
            dbias_ref[...] = jnp.zeros_like(dbias_ref)
            dalog_ref[...] = jnp.zeros_like(dalog_ref)
            ddcol_ref[...] = jnp.zeros_like(ddcol_ref)

        dtr = dtr_ref[...]
        a, dt, cs, dec, ecs, expand, row, col = _ssd_chunk_terms(dtr, bias_ref[...], alog_ref[...])
        cst = cs.T
        dt_x = _split_dot(dt, expand)
        dec_x = _split_dot(dec, expand)
        ecs_x = _split_dot(ecs, expand)
        xs = xs_ref[...]
        dyv = dy_ref[...]
        xdt = xs * dt_x
        lane_lo = col < SSM_HEAD_DIM
        causal = row >= col
        ddcol_ref[...] += jnp.sum(dyv * xs, axis=0, keepdims=True)
        dcs_col = jnp.zeros((SSM_CHUNK, LANES), F32)
        dcs_row = jnp.zeros((LANES, SSM_CHUNK), F32)
        for g in range(SSM_GROUPS):
            bg = b_ref[:, g * SSM_STATE:(g + 1) * SSM_STATE].astype(BF16)
            cg = c_ref[:, g * SSM_STATE:(g + 1) * SSM_STATE].astype(BF16)
            cb = _bdot(cg, bg, NT)
            dgm = jnp.zeros((SSM_CHUNK, SSM_CHUNK), F32)
            dbg = jnp.zeros((SSM_CHUNK, SSM_STATE), F32)
            dcg = jnp.zeros((SSM_CHUNK, SSM_STATE), F32)
            for q in range(SSM_PAIRS // SSM_GROUPS):
                pq = g * (SSM_PAIRS // SSM_GROUPS) + q
                sl = slice(pq * LANES, (pq + 1) * LANES)
                dyp = dyv[:, sl]
                xp = xdt[:, sl]
                dxh = []
                for hh in range(2):
                    h = 2 * pq + hh
                    lmat = jnp.exp(jnp.where(causal, cs[:, h:h + 1] - cst[h:h + 1, :], -jnp.inf))
                    mmat = cb * lmat
                    dyh = jnp.where(lane_lo if hh == 0 else jnp.logical_not(lane_lo), dyp, 0.0)
                    dmm = _bdot(dyh, xp, NT)
                    pm = dmm * mmat
                    dcs_col = jnp.where(col == h, jnp.sum(pm, axis=1, keepdims=True), dcs_col)
                    dcs_row = jnp.where(row == h, jnp.sum(pm, axis=0, keepdims=True), dcs_row)
                    dgm = dgm + dmm * lmat
                    dxh.append(_bdot(mmat, dyp, TN))
                s_in = sin_ref[0, pq]
                ecs_p = ecs_x[:, sl]
                dec_p = dec_x[:, sl]
                etot_p = ecs_x[SSM_CHUNK - 1:SSM_CHUNK, sl]
                yoff_s[:, sl] = dyp * (_bdot(cg, s_in) * ecs_p)
                dq = dyp * ecs_p
                dcg = dcg + _bdot(dq, s_in, NT)
                ds = dstate[pq]
                r = _bdot(bg, ds)
                rx_s[:, sl] = r * xp
                dxdt_s[:, sl] = jnp.where(lane_lo, dxh[0], dxh[1]) + dec_p * r
                dbg = dbg + _bdot(xp * dec_p, ds, NT)
                trow_s[0:1, sl] = jnp.sum(ds * s_in, axis=0, keepdims=True) * etot_p
                dstate[pq] = etot_p * ds + _bdot(cg, dq, TN)
            dcg = dcg + _bdot(dgm, bg)
            dbg = dbg + _bdot(dgm, cg, TN)
            dxbc_ref[:, SSM_INNER + g * SSM_STATE:SSM_INNER + (g + 1) * SSM_STATE] = dbg
            dxbc_ref[:, SSM_INNER + SSM_BC + g * SSM_STATE:SSM_INNER + SSM_BC + (g + 1) * SSM_STATE] = dcg
        ddec = _split_dot(rx_s[...], expand, NT) * dec
        dtot = _split_dot(trow_s[...], expand, NT)[0:1, :]
        dcs = dcs_col - dcs_row.T + _split_dot(yoff_s[...], expand, NT) - ddec
        dcs = dcs + jnp.where(row == SSM_CHUNK - 1, jnp.sum(ddec, axis=0, keepdims=True) + dtot, 0.0)
        da = _fdot((row <= col).astype(F32), dcs)
        dxdt = dxdt_s[...]
        ddt = da * a + _split_dot(dxdt * xs, expand, NT)
        dalog_ref[...] += jnp.sum(da * dt, axis=0, keepdims=True) * a
        ddtr = ddt * _sigmoid(dtr + bias_ref[...])
        ddtr_ref[...] = ddtr.astype(BF16)
        dbias_ref[...] += jnp.sum(ddtr, axis=0, keepdims=True)
        dxbc_ref[:, 0:SSM_INNER] = dxdt * dt_x + dyv * dexp_ref[...]

        @pl.when(pl.program_id(0) == last)
        def _():
            _exchange_phase("finish", *comm)

    out = pl.pallas_call(
        body, name="ssd_bwd", grid=(nc,),
        in_specs=[xs_s, bm_s, cm_s, dtr_s, sin_s, pl.BlockSpec((SSM_CHUNK, SSM_INNER), lambda c: (last - c, 0)),
                  vec, vec, wide] + [HBM] * n_ex,
        out_specs=[pl.BlockSpec((SSM_CHUNK, SSM_CONV_DIM), lambda c: (last - c, 0)),
                   pl.BlockSpec((SSM_CHUNK, LANES), lambda c: (last - c, 0)), vec, vec, wide] + [HBM] * n_ex,
        out_shape=[jax.ShapeDtypeStruct((t, SSM_CONV_DIM), F32), jax.ShapeDtypeStruct((t, LANES), BF16),
                   jax.ShapeDtypeStruct((1, LANES), F32), jax.ShapeDtypeStruct((1, LANES), F32),
                   jax.ShapeDtypeStruct((1, SSM_INNER), F32)]
        + [jax.ShapeDtypeStruct(p.shape, p.dtype) for p in exchange],
        scratch_shapes=[pltpu.VMEM((SSM_PAIRS, SSM_STATE, LANES), F32),
                        pltpu.VMEM((SSM_CHUNK, SSM_INNER), F32), pltpu.VMEM((SSM_CHUNK, SSM_INNER), F32),
                        pltpu.VMEM((SSM_CHUNK, SSM_INNER), F32), pltpu.VMEM((SUBLANES, SSM_INNER), F32)]
        + _comm_scratch(n_ex),
        compiler_params=_params(("arbitrary",)),
    )(xbc_act, xbc_act, xbc_act, proj, s_in_all, dy, bias, alog, dexp, *exchange)
    return out[:5], out[5:]


def _group_rstd(y):
    n = SSM_INNER // SSM_GROUPS
    parts = []
    for g in range(SSM_GROUPS):
        yg = y[:, g * n:(g + 1) * n]
        r = lax.rsqrt(jnp.mean(yg * yg, axis=-1, keepdims=True) + RMS_EPS)
        parts.append(jnp.broadcast_to(r, yg.shape))
    return jnp.concatenate(parts, axis=1)


def _group_mean(v):
    n = SSM_INNER // SSM_GROUPS
    parts = []
    for g in range(SSM_GROUPS):
        vg = v[:, g * n:(g + 1) * n]
        parts.append(jnp.broadcast_to(jnp.mean(vg, axis=-1, keepdims=True), vg.shape))
    return jnp.concatenate(parts, axis=1)


def _ssm_post_fwd(y_ssd, proj, nw):
    t = proj.shape[0]
    n = SSM_INNER
    tr = _rows(t)

    def body(y_ref, z_ref, nw_ref, o_ref):
        z = z_ref[...]
        y = y_ref[...] * (z * _sigmoid(z))
        o_ref[...] = (y * _group_rstd(y) * nw_ref[...]).astype(BF16)

    row = pl.BlockSpec((tr, n), lambda i: (i, 0))
    return pl.pallas_call(
        body, name="ssm_post_fwd", grid=(t // tr,),
        in_specs=[row, pl.BlockSpec((tr, n), lambda i: (i, COL_Z // n)), pl.BlockSpec((1, n), lambda i: (0, 0))],
        out_specs=row, out_shape=jax.ShapeDtypeStruct((t, n), BF16),
        compiler_params=_params(("parallel",)),
    )(y_ssd, proj, nw)


def _ssm_post_bwd(y_ssd, proj, nw, dout):
    t = proj.shape[0]
    n = SSM_INNER
    tr = _rows(t)

    def body(y_ref, z_ref, nw_ref, do_ref, dy_ref, dz_ref, dnw_ref):
        i = pl.program_id(0)
        z = z_ref[...]
        sg = _sigmoid(z)
        sz = z * sg
        ys = y_ref[...]
        y = ys * sz
        rstd = _group_rstd(y)
        yn = y * rstd
        dov = do_ref[...]
        dyn = dov * nw_ref[...]
        dyg = rstd * (dyn - yn * _group_mean(dyn * yn))
        dy_ref[...] = dyg * sz
        dz_ref[...] = (dyg * ys * (sg * (1.0 + z * (1.0 - sg)))).astype(BF16)
        part = jnp.sum(dov * yn, axis=0, keepdims=True)

        @pl.when(i == 0)
        def _():
            dnw_ref[...] = part

        @pl.when(i > 0)
        def _():
            dnw_ref[...] += part

    row = pl.BlockSpec((tr, n), lambda i: (i, 0))
    vec = pl.BlockSpec((1, n), lambda i: (0, 0))
    return pl.pallas_call(
        body, name="ssm_post_bwd", grid=(t // tr,),
        in_specs=[row, pl.BlockSpec((tr, n), lambda i: (i, COL_Z // n)), vec, row],
        out_specs=[row, row, vec],
        out_shape=[jax.ShapeDtypeStruct((t, n), F32), jax.ShapeDtypeStruct((t, n), BF16),
                   jax.ShapeDtypeStruct((1, n), F32)],
        compiler_params=_params(("arbitrary",)),
    )(y_ssd, proj, nw, dout)


SCAN_UNROLL = 8
GELU_C = math.sqrt(2.0 / math.pi)
GELU_K = 0.044715


def _gelu_parts(y):
    th = jnp.tanh(GELU_C * (y + GELU_K * y * y * y))
    val = 0.5 * y * (1.0 + th)
    grad = 0.5 * (1.0 + th) + 0.5 * y * (1.0 - th * th) * GELU_C * (1.0 + 3.0 * GELU_K * y * y)
    return val, grad


def _scan_tiles(a_ref, b_ref, h_ref, n_rows, reverse):
    n_tiles = n_rows // SUBLANES
    shape = (SUBLANES, a_ref.shape[1])
    row = _iota(shape, 0)

    def in_tile(av, bv):
        for s in (1, 2, 4):
            if reverse:
                keep = row < SUBLANES - s
                a_sh = jnp.where(keep, pltpu.roll(av, SUBLANES - s, 0), 1.0)
                b_sh = jnp.where(keep, pltpu.roll(bv, SUBLANES - s, 0), 0.0)
            else:
                keep = row >= s
                a_sh = jnp.where(keep, pltpu.roll(av, s, 0), 1.0)
                b_sh = jnp.where(keep, pltpu.roll(bv, s, 0), 0.0)
            bv = av * b_sh + bv
            av = av * a_sh
        return av, bv

    def step(k, carry):
        first = (n_tiles // SCAN_UNROLL - 1 - k) if reverse else k
        tiles = [first * SCAN_UNROLL + j for j in range(SCAN_UNROLL)]
        if reverse:
            tiles = tiles[::-1]
        ats = [pl.ds(pl.multiple_of(tile * SUBLANES, SUBLANES), SUBLANES) for tile in tiles]
        scanned = [in_tile(a_ref[at, :], b_ref[at, :]) for at in ats]
        for at, (av, bv) in zip(ats, scanned):
            hv = bv + av * carry
            h_ref[at, :] = hv
            carry = hv[0:1, :] if reverse else hv[SUBLANES - 1:SUBLANES, :]
        return carry

    assert n_tiles % SCAN_UNROLL == 0, n_rows
    lax.fori_loop(0, n_tiles // SCAN_UNROLL, step, jnp.zeros((1, a_ref.shape[1]), F32))


def _lru_gates(xl, cw, cb, wr, br, wi, bi, lam):
    u = cb + cw[CONV_K - 1:CONV_K, :] * xl
    for k in range(CONV_K - 1):
        u = u + cw[k:k + 1, :] * _shift_down(xl, CONV_K - 1 - k)
    r = _sigmoid(_bdot(u, wr) + br)
    i = _sigmoid(_bdot(u, wi) + bi)
    sp = _softplus(-lam)
    la = -LRU_C * r * sp
    a = jnp.exp(la)
    mult = jnp.sqrt(-jnp.tanh(la) * (a * a + 1.0))
    return u, r, i, sp, a, mult


def _lru_specs(t):
    c_x = COL_LRU_X // LANES
    c_y = COL_LRU_Y // LANES
    xl = pl.BlockSpec((t, LANES), lambda j: (0, c_x + j))
    yl = pl.BlockSpec((t, LANES), lambda j: (0, c_y + j))
    col = pl.BlockSpec((t, LANES), lambda j: (0, j))
    cw = pl.BlockSpec((CONV_K, LANES), lambda j: (0, j))
    vec = pl.BlockSpec((1, LANES), lambda j: (0, j))
    wblk = pl.BlockSpec((1, LANES, LANES), lambda j: (j, 0, 0))
    return xl, yl, col, cw, vec, wblk


def _lru_fwd(proj, cw, cb, wr, br, wi, bi, lam, gather):
    t = proj.shape[0]
    xl_s, yl_s, col, cw_s, vec, wblk = _lru_specs(t)
    n = len(gather)

    def body(*refs):
        xl_ref, yl_ref, cw_ref, cb_ref, wr_ref, br_ref, wi_ref, bi_ref, lam_ref = refs[:9]
        o_ref, h_ref = refs[9 + n:11 + n]
        a_s, b_s = refs[11 + 2 * n:13 + 2 * n]
        comm = (refs[9:9 + n], refs[11 + n:11 + 2 * n]) + tuple(refs[13 + 2 * n:])
        j = pl.program_id(0)
        for step, phase in ((0, "start"), (LRU_BLOCKS // 2, "forward")):
            @pl.when(j == step)
            def _():
                _gather_phase(phase, *comm)

        u, r, i, sp, a, mult = _lru_gates(xl_ref[...], cw_ref[...], cb_ref[...], wr_ref[0], br_ref[...],
                                          wi_ref[0], bi_ref[...], lam_ref[...])
        a_s[...] = a
        b_s[...] = mult * (i * u)
        _scan_tiles(a_s, b_s, h_ref, t, reverse=False)
        o_ref[...] = (h_ref[...] * _gelu_parts(yl_ref[...])[0]).astype(BF16)

        @pl.when(j == LRU_BLOCKS - 1)
        def _():
            _gather_phase("finish", *comm)

    out = pl.pallas_call(
        body, name="lru_fwd", grid=(LRU_BLOCKS,),
        in_specs=[xl_s, yl_s, cw_s, vec, wblk, vec, wblk, vec, vec] + [HBM] * n,
        out_specs=[col, col] + [HBM] * n,
        out_shape=[jax.ShapeDtypeStruct((t, LRU_WIDTH), BF16), jax.ShapeDtypeStruct((t, LRU_WIDTH), F32)]
        + [jax.ShapeDtypeStruct((N_DEV,) + v.shape, v.dtype) for v in gather],
        scratch_shapes=[pltpu.VMEM((t, LANES), F32)] * 2 + _comm_scratch(n),
        compiler_params=_params(("arbitrary",), big=True),
    )(proj, proj, cw, cb, wr, br, wi, bi, lam, *gather)
    return out[:2], out[2:]


def _lru_bwd(proj, cw, cb, wr, br, wi, bi, lam, h_all, dout, exchange):
    t = proj.shape[0]
    xl_s, yl_s, col, cw_s, vec, wblk = _lru_specs(t)
    n_ex = len(exchange)

    def body(*refs):
        xl_ref, yl_ref, cw_ref, cb_ref, wr_ref, br_ref, wi_ref, bi_ref, lam_ref, h_ref, do_ref = refs[:11]
        (dxl_ref, dyl_ref, dcw_ref, dcb_ref, dwr_ref, dbr_ref, dwi_ref, dbi_ref,
         dlam_ref) = refs[11 + n_ex:20 + n_ex]
        a_s, b_s, g_s = refs[20 + 2 * n_ex:23 + 2 * n_ex]
        comm = (refs[11:11 + n_ex], refs[20 + n_ex:20 + 2 * n_ex]) + tuple(refs[23 + 2 * n_ex:])

        @pl.when(pl.program_id(0) == 0)
        def _():
            _exchange_phase("start", *comm)

        xl = xl_ref[...]
        cwv = cw_ref[...]
        lam = lam_ref[...]
        u, r, i, sp, a, mult = _lru_gates(xl, cwv, cb_ref[...], wr_ref[0], br_ref[...], wi_ref[0], bi_ref[...], lam)
        v = i * u
        gl, dgl = _gelu_parts(yl_ref[...])
        dov = do_ref[...]
        h = h_ref[...]
        dyl_ref[...] = (dov * h * dgl).astype(BF16)
        b_s[...] = dov * gl
        a_s[...] = _shift_up(a, 1)
        _scan_tiles(a_s, b_s, g_s, t, reverse=True)
        g = g_s[...]
        da = g * _shift_down(h, 1)
        dmult = g * v
        dv = g * mult
        dla = da * a - dmult * (a * a) / mult
        dr = dla * (-LRU_C * sp)
        dsp = jnp.sum(dla * (-LRU_C * r), axis=0, keepdims=True)
        dlam_ref[...] = -dsp * _sigmoid(-lam)
        dpr = dr * r * (1.0 - r)
        dpi = dv * u * i * (1.0 - i)
        dbr_ref[...] = jnp.sum(dpr, axis=0, keepdims=True)
        dbi_ref[...] = jnp.sum(dpi, axis=0, keepdims=True)
        dwr_ref[0] = _bdot(u, dpr, TN)
        dwi_ref[0] = _bdot(u, dpi, TN)
        du = dv * i + _bdot(dpr, wr_ref[0], NT) + _bdot(dpi, wi_ref[0], NT)
        dxl = cwv[CONV_K - 1:CONV_K, :] * du
        for k in range(CONV_K - 1):
            dxl = dxl + cwv[k:k + 1, :] * _shift_up(du, CONV_K - 1 - k)
        dxl_ref[...] = dxl.astype(BF16)
        for k in range(CONV_K):
            dcw_ref[k:k + 1, :] = jnp.sum(du * _shift_down(xl, CONV_K - 1 - k), axis=0, keepdims=True)
        dcb_ref[...] = jnp.sum(du, axis=0, keepdims=True)

        @pl.when(pl.program_id(0) == LRU_BLOCKS - 1)
        def _():
            _exchange_phase("finish", *comm)

    out = pl.pallas_call(
        body, name="lru_bwd", grid=(LRU_BLOCKS,),
        in_specs=[xl_s, yl_s, cw_s, vec, wblk, vec, wblk, vec, vec, col, col] + [HBM] * n_ex,
        out_specs=[col, col, cw_s, vec, wblk, vec, wblk, vec, vec] + [HBM] * n_ex,
        out_shape=[jax.ShapeDtypeStruct((t, LRU_WIDTH), BF16), jax.ShapeDtypeStruct((t, LRU_WIDTH), BF16),
                   jax.ShapeDtypeStruct((CONV_K, LRU_WIDTH), F32), jax.ShapeDtypeStruct((1, LRU_WIDTH), F32),
                   jax.ShapeDtypeStruct((LRU_BLOCKS, LANES, LANES), F32), jax.ShapeDtypeStruct((1, LRU_WIDTH), F32),
                   jax.ShapeDtypeStruct((LRU_BLOCKS, LANES, LANES), F32), jax.ShapeDtypeStruct((1, LRU_WIDTH), F32),
                   jax.ShapeDtypeStruct((1, LRU_WIDTH), F32)]
        + [jax.ShapeDtypeStruct(p.shape, p.dtype) for p in exchange],
        scratch_shapes=[pltpu.VMEM((t, LANES), F32)] * 3 + _comm_scratch(n_ex),
        compiler_params=_params(("arbitrary",), big=True),
    )(proj, proj, cw, cb, wr, br, wi, bi, lam, h_all, dout, *exchange)
    return out[:9], out[9:]


def _mesh_pos():
    return lax.axis_index("x"), lax.axis_index("y"), lax.axis_index("c")


HBM = pl.BlockSpec(memory_space=pl.ANY)


def _comm_scratch(n):
    return [pltpu.SemaphoreType.DMA((n, 7)), pltpu.SemaphoreType.DMA((n, 7)), pltpu.SemaphoreType.DMA((n,))]


def _gather_phase(phase, v_refs, out_refs, send_sems, recv_sems, local_sems):
    n = len(v_refs)
    x, y, c = _mesh_pos()
    me, sibling = (x, y, c), (x, y, 1 - c)
    chips = [(1 - x, y), (x, 1 - y), (1 - x, 1 - y)]

    def block(a, px, py, pc):
        return out_refs[a].at[4 * px + 2 * py + pc]

    def copy(a, k, blk, to, src=None):
        return pltpu.make_async_remote_copy(
            src_ref=block(a, *blk) if src is None else src, dst_ref=block(a, *blk),
            send_sem=send_sems.at[a, k], recv_sem=recv_sems.at[a, k], device_id=to, device_id_type=MESH)

    def own(a):
        return pltpu.make_async_copy(v_refs[a], block(a, *me), local_sems.at[a])

    def first(a):
        return ([copy(a, 0, me, sibling, src=v_refs[a])]
                + [copy(a, 1 + j, me, (*chip, c), src=v_refs[a]) for j, chip in enumerate(chips)])

    def forward(a, j):
        return copy(a, 4 + j, (*chips[j], c), sibling)

    if phase == "start":
        for a in range(n):
            own(a).start()
        for a in range(n):
            for cp in first(a):
                cp.start()
    elif phase == "forward":
        for j in range(3):
            for a in range(n):
                copy(a, 1 + j, (*chips[j], c), me).wait_recv()
                forward(a, j).start()
    else:
        for a in range(n):
            copy(a, 0, sibling, me).wait_recv()
            for j in range(3):
                copy(a, 4 + j, (*chips[j], 1 - c), me).wait_recv()
        for a in range(n):
            for cp in first(a) + [forward(a, j) for j in range(3)]:
                cp.wait_send()
            own(a).wait()


def _all_gather(vs, name):
    n = len(vs)

    def body(*refs):
        comm = (refs[:n], refs[n:2 * n]) + tuple(refs[2 * n:])
        for phase in ("start", "forward", "finish"):
            _gather_phase(phase, *comm)

    return pl.pallas_call(
        body, name=name,
        out_shape=[jax.ShapeDtypeStruct((N_DEV,) + v.shape, v.dtype) for v in vs],
        in_specs=[HBM] * n, out_specs=[HBM] * n, scratch_shapes=_comm_scratch(n),
    )(*vs)


def _exchange(parts, name):
    n = len(parts)

    def body(*refs):
        comm = (refs[:n], refs[n:2 * n]) + tuple(refs[2 * n:])
        _exchange_phase("start", *comm)
        _exchange_phase("finish", *comm)

    return pl.pallas_call(
        body, name=name,
        out_shape=[jax.ShapeDtypeStruct(p.shape, p.dtype) for p in parts],
        in_specs=[HBM] * n, out_specs=[HBM] * n, scratch_shapes=_comm_scratch(n),
    )(*parts)


def _exchange_phase(phase, p_refs, out_refs, send_sems, recv_sems, local_sems):
    n = len(p_refs)
    x, y, c = _mesh_pos()
    me = 4 * x + 2 * y + c
    local = [pltpu.make_async_copy(p_refs[a].at[me], out_refs[a].at[me], local_sems.at[a]) for a in range(n)]
    remote = []
    for k in range(1, N_DEV):
        px = (1 - x) if k & 4 else x
        py = (1 - y) if k & 2 else y
        pc = (1 - c) if k & 1 else c
        for a in range(n):
            remote.append(pltpu.make_async_remote_copy(
                src_ref=p_refs[a].at[4 * px + 2 * py + pc], dst_ref=out_refs[a].at[me],
                send_sem=send_sems.at[a, k - 1], recv_sem=recv_sems.at[a, k - 1],
                device_id=(px, py, pc), device_id_type=MESH))
    if phase == "start":
        for cp in local + remote:
            cp.start()
    else:
        for cp in remote:
            cp.wait()
        for cp in local:
            cp.wait()


def _sum_sources(recvs, name):
    k = len(recvs)

    def body(*refs):
        for r_ref, o_ref in zip(refs[:k], refs[k:]):
            acc = r_ref[0].astype(F32)
            for s in range(1, r_ref.shape[0]):
                acc = acc + r_ref[s].astype(F32)
            o_ref[...] = acc

    return pl.pallas_call(
        body, name=name, out_shape=[jax.ShapeDtypeStruct(r.shape[1:], F32) for r in recvs],
        compiler_params=_params(),
    )(*recvs)


def _row_tile(rows):
    for tile in range(128, 15, -16):
        if rows % tile == 0:
            return tile
    return rows


def _adam_update(w, g, m, v):
    nm = ADAM_B1 * m + (1.0 - ADAM_B1) * g
    nv = ADAM_B2 * v + (1.0 - ADAM_B2) * (g * g)
    m_hat = nm / (1.0 - ADAM_B1 ** ADAM_STEP)
    v_hat = nv / (1.0 - ADAM_B2 ** ADAM_STEP)
    return -ADAM_LR * (m_hat / (jnp.sqrt(v_hat) + ADAM_EPS) + ADAM_WD * w), nm, nv


def _vector_offsets(widths):
    offsets, end = [], 0
    for c in widths:
        offsets.append(end)
        end += c + (-c) % LANES
    return offsets, end


def _adamw_small(vec_parts, vec_state, mat_grads, mat_state):
    widths = [w.shape[1] for w, _, _ in vec_state]
    offsets, total = _vector_offsets(widths)
    assert vec_parts.shape == (N_DEV, total), (vec_parts.shape, total)
    n_vec, n_mat = len(vec_state), len(mat_state)

    def body(*refs):
        r_ref = refs[0]
        vec_in = refs[1:1 + 3 * n_vec]
        mat_in = refs[1 + 3 * n_vec:1 + 3 * n_vec + 4 * n_mat]
        outs = refs[1 + 3 * n_vec + 4 * n_mat:]
        for i, (off, c) in enumerate(zip(offsets, widths)):
            g = r_ref[0:1, off:off + c]
            for s in range(1, N_DEV):
                g = g + r_ref[s:s + 1, off:off + c]
            w_ref, m_ref, v_ref = vec_in[3 * i:3 * i + 3]
            g_out, d_out, m_out, v_out = outs[4 * i:4 * i + 4]
            g_out[...] = g
            d_out[...], m_out[...], v_out[...] = _adam_update(w_ref[...], g, m_ref[...], v_ref[...])
        for j in range(n_mat):
            g_ref, w_ref, m_ref, v_ref = mat_in[4 * j:4 * j + 4]
            d_out, m_out, v_out = outs[4 * n_vec + 3 * j:4 * n_vec + 3 * j + 3]
            d_out[...], m_out[...], v_out[...] = _adam_update(w_ref[...], g_ref[...], m_ref[...], v_ref[...])

    args = [vec_parts] + [a for state in vec_state for a in state]
    for g, state in zip(mat_grads, mat_state):
        args += [g, *state]
    out_shape = [jax.ShapeDtypeStruct(w.shape, F32) for w, _, _ in vec_state for _ in range(4)]
    out_shape += [jax.ShapeDtypeStruct(w.shape, F32) for w, _, _ in mat_state for _ in range(3)]
    out = pl.pallas_call(body, name="adamw_replicated", out_shape=out_shape, compiler_params=_params())(*args)
    vec_out = [tuple(out[4 * i:4 * i + 4]) for i in range(n_vec)]
    mat_out = [tuple(out[4 * n_vec + 3 * j:4 * n_vec + 3 * j + 3]) for j in range(n_mat)]
    return vec_out, mat_out


def _adamw(w, recv, m, v, name):
    rows, width = w.shape
    n = recv.shape[0]
    if rows % 16 == 0 or width % 256:
        tr, tc = _row_tile(rows), width
    else:
        tr, tc = rows, 256

    def body(w_ref, r_ref, m_ref, v_ref, g_ref, d_ref, nm_ref, nv_ref):
        gv = r_ref[0].astype(F32)
        for s in range(1, n):
            gv = gv + r_ref[s].astype(F32)
        g_ref[...] = gv
        d_ref[...], nm_ref[...], nv_ref[...] = _adam_update(w_ref[...], gv, m_ref[...], v_ref[...])

    spec = pl.BlockSpec((tr, tc), lambda i, j: (i, j))
    shape = jax.ShapeDtypeStruct((rows, width), F32)
    return pl.pallas_call(
        body, name=name, grid=(rows // tr, width // tc),
        in_specs=[spec, pl.BlockSpec((n, tr, tc), lambda i, j: (0, i, j)), spec, spec],
        out_specs=[spec] * 4, out_shape=[shape] * 4,
        compiler_params=_params(("parallel", "parallel")),
    )(w, recv, m, v)


BIG_NAMES = ("w_in", "w_out_ssm", "w_out_lru", "w_out", "w_ffn_in", "w_ffn_out", "ssm_conv_w", "lru_conv_w")
TRANSPOSED = ("w_in", "w_ffn_in")
CONV_NAMES = ("ssm_conv_w", "lru_conv_w")
MATMUL_NAMES = BIG_NAMES[:6]
NEEDED_FIRST = ("w_in", "ssm_conv_w", "lru_conv_w")
GATHERED_IN_SSD = ("w_ffn_in",)
GATHERED_IN_LRU = ("w_ffn_out", "w_out_ssm", "w_out_lru", "w_out")
EXCHANGED_IN_SSD = ("w_ffn_in", "w_ffn_out")
EXCHANGED_IN_LRU = ("w_out_ssm", "w_out_lru", "w_out")
SMALL_VECTORS = ("norm1_w", "b_branch_gate", "ssm_conv_b", "ssm_dt_bias", "ssm_a_log", "ssm_d", "ssm_norm_w",
                 "lru_conv_b", "lru_b_r", "lru_b_i", "lru_lambda", "norm2_w", "norm_f_w")
SMALL_MATRICES = ("lru_w_r", "lru_w_i")


def _col_shards(full):
    rows, cols = full.shape
    return full.reshape(rows, N_DEV, cols // N_DEV).transpose(1, 0, 2)


def _from_col_shards(g):
    n, rows, w = g.shape
    return g.transpose(1, 0, 2).reshape(rows, n * w)


def kernel(x, norm1_w, w_in, b_branch_gate, ssm_conv_w, ssm_conv_b, ssm_dt_bias, ssm_a_log, ssm_d, ssm_norm_w, w_out_ssm, lru_conv_w, lru_conv_b, lru_w_r, lru_b_r, lru_w_i, lru_b_i, lru_lambda, w_out_lru, w_out, norm2_w, w_ffn_in, w_ffn_out, norm_f_w, loss_target, m_norm1_w, m_w_in, m_b_branch_gate, m_ssm_conv_w, m_ssm_conv_b, m_ssm_dt_bias, m_ssm_a_log, m_ssm_d, m_ssm_norm_w, m_w_out_ssm, m_lru_conv_w, m_lru_conv_b, m_lru_w_r, m_lru_b_r, m_lru_w_i, m_lru_b_i, m_lru_lambda, m_w_out_lru, m_w_out, m_norm2_w, m_w_ffn_in, m_w_ffn_out, m_norm_f_w, v_norm1_w, v_w_in, v_b_branch_gate, v_ssm_conv_w, v_ssm_conv_b, v_ssm_dt_bias, v_ssm_a_log, v_ssm_d, v_ssm_norm_w, v_w_out_ssm, v_lru_conv_w, v_lru_conv_b, v_lru_w_r, v_lru_b_r, v_lru_w_i, v_lru_b_i, v_lru_lambda, v_w_out_lru, v_w_out, v_norm2_w, v_w_ffn_in, v_w_ffn_out, v_norm_f_w):
    given = dict(locals())
    weights = {n: given[n] for n in BIG_NAMES + SMALL_VECTORS + SMALL_MATRICES}
    t = x.shape[1]
    xt = x[0]
    tgt = loss_target[0]

    def local(n, a):
        return a[0].T if n in TRANSPOSED else a[0]

    def as_output(n, a):
        return a.T[None] if n in TRANSPOSED else a[None]

    def shard(n):
        s = local(n, weights[n])
        return s.astype(BF16) if n in MATMUL_NAMES else s

    def unshard(n, g):
        return _from_col_shards(g) if n in CONV_NAMES else g.reshape(-1, g.shape[-1])

    def grad_slices(n):
        g = grads[n]
        return (_col_shards(g) if n in CONV_NAMES else g.reshape(N_DEV, -1, g.shape[-1])).astype(BF16)

    gathered = _all_gather([shard(n) for n in NEEDED_FIRST], "gather_in_weights")
    full = {n: unshard(n, g) for n, g in zip(NEEDED_FIRST, gathered)}
    ssm_cw, lru_cw = full["ssm_conv_w"], full["lru_conv_w"]
    wi_t = full["w_in"]
    w_pt = jnp.concatenate([wi_t[:ORIG_DT], wi_t[ORIG_LRU:], wi_t[ORIG_DT:ORIG_LRU],
                            jnp.zeros((PROJ_W - IN_PROJ, D_MODEL), BF16)], axis=0)

    def pad_heads(a):
        return jnp.pad(a.reshape(1, SSM_HEADS), ((0, 0), (0, LANES - SSM_HEADS)))

    dt_bias_p = pad_heads(ssm_dt_bias)
    a_log_p = pad_heads(ssm_a_log)
    d_exp = jnp.repeat(ssm_d.reshape(SSM_HEADS), SSM_HEAD_DIM).reshape(1, SSM_INNER)
    lru_wr, lru_wi = lru_w_r[0], lru_w_i[0]

    hn1 = _rmsnorm_fwd(xt, norm1_w, "norm1_fwd")
    proj = _mm(hn1, w_pt, tb=True, name="in_proj")
    xbc_act = _ssm_conv_fwd(proj, ssm_cw, ssm_conv_b)
    (y_ssd, s_in_all), gathered = _ssd_fwd(xbc_act, proj, dt_bias_p, a_log_p, d_exp,
                                           gather=[shard(n) for n in GATHERED_IN_SSD])
    full.update({n: unshard(n, g) for n, g in zip(GATHERED_IN_SSD, gathered)})
    (l_out, h_lru), gathered = _lru_fwd(proj, lru_cw, lru_conv_b, lru_wr, lru_b_r, lru_wi, lru_b_i, lru_lambda,
                                        gather=[shard(n) for n in GATHERED_IN_LRU])
    full.update({n: unshard(n, g) for n, g in zip(GATHERED_IN_LRU, gathered)})
    y_pre = _ssm_post_fwd(y_ssd, proj, ssm_norm_w)
    y_ssm = _mm(y_pre, full["w_out_ssm"], name="out_ssm")
    y_lru = _mm(l_out, full["w_out_lru"], name="out_lru")
    merged = _merge_fwd(proj, b_branch_gate, y_ssm, y_lru)
    h1 = _mm(merged, full["w_out"], add=xt, name="out_proj")
    hn2 = _rmsnorm_fwd(h1, norm2_w, "norm2_fwd")
    gu = _mm(hn2, full["w_ffn_in"], tb=True, name="ffn_in")
    act = _swiglu_fwd(gu)
    h2 = _mm(act, full["w_ffn_out"], add=h1, name="ffn_out")

    grads = {}
    dh2, grads["norm_f_w"], loss_cols = _loss_head(h2, norm_f_w.reshape(1, D_MODEL), tgt)
    loss = lax.psum(0.5 * jnp.sum(loss_cols) / D_MODEL, AXES)
    dact = _mm(dh2, full["w_ffn_out"], tb=True, name="d_act")
    grads["w_ffn_out"] = _mm(act, dh2, ta=True, out_dtype=BF16, name="dw_ffn_out")
    dgu = _swiglu_bwd(gu, dact)
    dhn2 = _mm(dgu, full["w_ffn_in"], name="d_hn2")
    grads["w_ffn_in"] = _mm(dgu, hn2, ta=True, out_dtype=BF16, name="dw_ffn_in")
    dh1, grads["norm2_w"] = _rmsnorm_bwd(h1, norm2_w, dhn2, dh2, "norm2_bwd")
    dmerged = _mm(dh1, full["w_out"], tb=True, name="d_merged")
    grads["w_out"] = _mm(merged, dh1, ta=True, out_dtype=BF16, name="dw_out")
    dy_ssm, dy_lru, dgates, grads["b_branch_gate"] = _merge_bwd(proj, b_branch_gate, y_ssm, y_lru, dmerged)
    dy_pre = _mm(dy_ssm, full["w_out_ssm"], tb=True, name="d_y_pre")
    grads["w_out_ssm"] = _mm(y_pre, dy_ssm, ta=True, out_dtype=BF16, name="dw_out_ssm")
    dl_out = _mm(dy_lru, full["w_out_lru"], tb=True, name="d_l_out")
    grads["w_out_lru"] = _mm(l_out, dy_lru, ta=True, out_dtype=BF16, name="dw_out_lru")
    dy_ssd, dz, grads["ssm_norm_w"] = _ssm_post_bwd(y_ssd, proj, ssm_norm_w, dy_pre)
    (dxbc_act, ddt, dbias, dalog, ddcol), recv_in_ssd = _ssd_bwd(
        xbc_act, proj, s_in_all, dy_ssd, dt_bias_p, a_log_p, d_exp,
        exchange=[grad_slices(n) for n in EXCHANGED_IN_SSD])
    grads["ssm_dt_bias"] = dbias[:, :SSM_HEADS]
    grads["ssm_a_log"] = dalog[:, :SSM_HEADS]
    grads["ssm_d"] = ddcol.reshape(SSM_HEADS, SSM_HEAD_DIM).sum(axis=1).reshape(1, SSM_HEADS)
    dxbc, grads["ssm_conv_w"], grads["ssm_conv_b"] = _ssm_conv_bwd(proj, ssm_cw, ssm_conv_b, dxbc_act)
    ((dxl, dyl, grads["lru_conv_w"], grads["lru_conv_b"], dwr, grads["lru_b_r"], dwi, grads["lru_b_i"],
      grads["lru_lambda"]), recv_in_lru) = _lru_bwd(
        proj, lru_cw, lru_conv_b, lru_wr, lru_b_r, lru_wi, lru_b_i, lru_lambda, h_lru, dl_out,
        exchange=[grad_slices(n) for n in EXCHANGED_IN_LRU])
    grads["lru_w_r"], grads["lru_w_i"] = dwr[None], dwi[None]
    dproj = jnp.concatenate([dgates, dz, dxbc, dxl, dyl, ddt], axis=1)
    dwpt = _mm(dproj, hn1, ta=True, out_dtype=BF16, name="dw_in")
    grads["w_in"] = jnp.concatenate([dwpt[:ORIG_DT], dwpt[COL_DT:COL_DT + SSM_HEADS], dwpt[COL_LRU_X:COL_DT]], axis=0)
    mat_parts = [grads[n].reshape(N_DEV, -1, LANES) for n in SMALL_MATRICES]
    dhn1, received = _mm(dproj, w_pt, exchange=[grad_slices(n) for n in NEEDED_FIRST] + mat_parts, name="d_hn1")
    recv_first, recv_mats = received[:len(NEEDED_FIRST)], received[len(NEEDED_FIRST):]
    grad_x, grads["norm1_w"] = _rmsnorm_bwd(xt, norm1_w, dhn1, dh1, "norm1_bwd")

    def as_rows(n, a):
        return a.reshape(1, -1) if n in SMALL_VECTORS else a.reshape(-1, LANES)

    vec_g = jnp.concatenate([jnp.pad(as_rows(n, grads[n]), ((0, 0), (0, (-grads[n].size) % LANES)))
                             for n in SMALL_VECTORS], axis=1)
    gathered = _all_gather([vec_g] + list(_sum_sources(recv_mats, "sum_gate_matrix_grads")), "gather_small_grads")
    vec_parts = gathered[0].reshape(N_DEV, -1)
    mat_g = [g.reshape(-1, LANES) for g in gathered[1:]]

    recv = dict(zip(EXCHANGED_IN_SSD + EXCHANGED_IN_LRU + NEEDED_FIRST,
                    list(recv_in_ssd) + list(recv_in_lru) + list(recv_first)))
    big_out = {n: _adamw(local(n, weights[n]), recv[n], local(n, given["m_" + n]), local(n, given["v_" + n]),
                         "adamw_" + n) for n in BIG_NAMES}

    def state(n):
        return tuple(as_rows(n, given[p + n]) for p in ("", "m_", "v_"))

    vec_out, mat_out = _adamw_small(vec_parts, [state(n) for n in SMALL_VECTORS],
                                    mat_g, [state(n) for n in SMALL_MATRICES])
    small_out = dict(zip(SMALL_VECTORS, vec_out))
    small_out.update({n: (g,) + out for n, g, out in zip(SMALL_MATRICES, mat_g, mat_out)})

    order = list(given)[1:24]
    results = []
    for q in range(4):
        vals = {n: as_output(n, big_out[n][q]) for n in BIG_NAMES}
        vals.update({n: out[q].reshape(weights[n].shape) for n, out in small_out.items()})
        results.extend(vals[n] for n in order)
    return (loss, grad_x[None], *results)
```

```python
import math

import jax
import jax.numpy as jnp
from jax import lax
from jax.experimental import pallas as pl
from jax.experimental.pallas import tpu as pltpu

F32 = jnp.float32
BF16 = jnp.bfloat16
HIGHEST = lax.Precision.HIGHEST
MESH = pl.DeviceIdType.MESH
AXES = ("x", "y", "c")
N_DEV = 8

D_MODEL = 1024
SSM_INNER = 2048
SSM_HEADS = 32
SSM_HEAD_DIM = 64
SSM_GROUPS = 4
SSM_STATE = 128
SSM_BC = SSM_GROUPS * SSM_STATE
SSM_CONV_DIM = SSM_INNER + 2 * SSM_BC
SSM_CHUNK = 128
SSM_PAIRS = SSM_HEADS // 2
CONV_K = 4
LRU_WIDTH = 1280
LRU_BLOCKS = 10
LRU_C = 8.0
FFN_HIDDEN = 2816
RMS_EPS = 1e-6
IN_PROJ = 9760

COL_GATES = 0
COL_Z = 2048
COL_XBC = 4096
COL_LRU = 7168
COL_DT = 9728
PROJ_W = 9856
ORIG_DT = 7168
ORIG_LRU_X = 7200
ORIG_LRU_Y = 8480

ADAM_LR = 0.001
ADAM_B1 = 0.9
ADAM_B2 = 0.999
ADAM_EPS = 1e-08
ADAM_WD = 0.01
ADAM_STEP = 10

LANES = 128
SUBLANES = 8
V7X_VMEM_BYTES = 64 * 1024 * 1024
VMEM_LIMIT = V7X_VMEM_BYTES * 3 // 4
VMEM_LIMIT_BIG = V7X_VMEM_BYTES * 15 // 16

NT = (((1,), (1,)), ((), ()))
TN = (((0,), (0,)), ((), ()))


def _params(sem=None, big=False):
    return pltpu.CompilerParams(dimension_semantics=sem,
                                vmem_limit_bytes=VMEM_LIMIT_BIG if big else VMEM_LIMIT)


def _blk(dim, cap):
    if dim <= cap:
        return dim
    for m in range(cap // LANES, 0, -1):
        if dim % (m * LANES) == 0:
            return m * LANES
    raise ValueError(f"no block for {dim}")


def _rows(t):
    return min(t, 256)


def _sigmoid(v):
    return 1.0 / (1.0 + jnp.exp(-v))


def _softplus(v):
    e = jnp.exp(-jnp.abs(v))
    u = 1.0 + e
    log1p = jnp.where(u == 1.0, e, jnp.log(u) * e / jnp.where(u == 1.0, 1.0, u - 1.0))
    return jnp.maximum(v, 0.0) + log1p


def _iota(shape, dim):
    return lax.broadcasted_iota(jnp.int32, shape, dim)


def _shift_down(v, s):
    if s == 0:
        return v
    return jnp.where(_iota(v.shape, 0) >= s, pltpu.roll(v, s, 0), 0.0)


def _shift_up(v, s):
    if s == 0:
        return v
    n = v.shape[0]
    return jnp.where(_iota(v.shape, 0) < n - s, pltpu.roll(v, n - s, 0), 0.0)


def _bdot(a, b, dn=None):
    a = a.astype(BF16)
    b = b.astype(BF16)
    if dn is None:
        return jnp.dot(a, b, preferred_element_type=F32)
    return lax.dot_general(a, b, dn, preferred_element_type=F32)


def _split_dot(a, e, dn=None):
    hi = a.astype(BF16)
    lo = (a - hi.astype(F32)).astype(BF16)
    return _bdot(hi, e, dn) + _bdot(lo, e, dn)


def _fdot(a, b, dn=None):
    if dn is None:
        return jnp.dot(a, b, precision=HIGHEST, preferred_element_type=F32)
    return lax.dot_general(a, b, dn, precision=HIGHEST, preferred_element_type=F32)


def _mm(a, b, *, ta=False, tb=False, add=None, exchange=(), out_dtype=F32, name):
    if ta:
        kdim, m = a.shape
    else:
        m, kdim = a.shape
    if tb:
        n, k2 = b.shape
    else:
        k2, n = b.shape
    assert kdim == k2, (a.shape, b.shape, ta, tb)
    bm, bn, bk = _blk(m, 1408), _blk(n, 1408), _blk(kdim, 1408)
    grid = (m // bm, n // bn, kdim // bk)
    nk = grid[2]
    dn = (((0 if ta else 1,), (1 if tb else 0,)), ((), ()))
    n_in = 2 if add is None else 3
    n_ex = len(exchange)

    def body(*refs):
        a_ref, b_ref = refs[:2]
        r_ref = None if add is None else refs[2]
        o_ref = refs[n_in + n_ex]
        acc = refs[n_in + 2 * n_ex + 1]
        comm = (refs[n_in:n_in + n_ex], refs[n_in + n_ex + 1:n_in + 2 * n_ex + 1]) + tuple(refs[n_in + 2 * n_ex + 2:])
        step = (pl.program_id(0) * grid[1] + pl.program_id(1)) * nk + pl.program_id(2)
        k = pl.program_id(2)
        if n_ex:
            @pl.when(step == 0)
            def _():
                _exchange_phase("start", *comm)

        @pl.when(k == 0)
        def _():
            acc[...] = jnp.zeros_like(acc)

        acc[...] += lax.dot_general(a_ref[...].astype(BF16), b_ref[...].astype(BF16), dn,
                                    preferred_element_type=F32)

        @pl.when(k == nk - 1)
        def _():
            r = acc[...]
            if add is not None:
                r = r + r_ref[...]
            o_ref[...] = r.astype(out_dtype)

        if n_ex:
            @pl.when(step == grid[0] * grid[1] * nk - 1)
            def _():
                _exchange_phase("finish", *comm)

    a_spec = pl.BlockSpec((bk, bm), lambda i, j, k: (k, i)) if ta else pl.BlockSpec((bm, bk), lambda i, j, k: (i, k))
    b_spec = pl.BlockSpec((bn, bk), lambda i, j, k: (j, k)) if tb else pl.BlockSpec((bk, bn), lambda i, j, k: (k, j))
    o_spec = pl.BlockSpec((bm, bn), lambda i, j, k: (i, j))
    in_specs = [a_spec, b_spec] + ([o_spec] if add is not None else []) + [HBM] * n_ex
    args = (a, b) + ((add,) if add is not None else ()) + tuple(exchange)
    out = pl.pallas_call(
        body, name=name, grid=grid,
        in_specs=in_specs, out_specs=[o_spec] + [HBM] * n_ex,
        out_shape=[jax.ShapeDtypeStruct((m, n), out_dtype)]
        + [jax.ShapeDtypeStruct(p.shape, p.dtype) for p in exchange],
        scratch_shapes=[pltpu.VMEM((bm, bn), F32)] + (_comm_scratch(n_ex) if n_ex else []),
        compiler_params=_params(("arbitrary",) * 3 if n_ex else ("parallel", "parallel", "arbitrary")),
    )(*args)
    return (out[0], out[1:]) if n_ex else out[0]


def _rmsnorm_fwd(x, w, name):
    t, d = x.shape
    tr = _rows(t)

    def body(x_ref, w_ref, o_ref):
        xv = x_ref[...]
        rstd = lax.rsqrt(jnp.mean(xv * xv, axis=-1, keepdims=True) + RMS_EPS)
        o_ref[...] = (xv * rstd * w_ref[...]).astype(BF16)

    return pl.pallas_call(
        body, name=name, grid=(t // tr,),
        in_specs=[pl.BlockSpec((tr, d), lambda i: (i, 0)), pl.BlockSpec((1, d), lambda i: (0, 0))],
        out_specs=pl.BlockSpec((tr, d), lambda i: (i, 0)),
        out_shape=jax.ShapeDtypeStruct((t, d), BF16),
        compiler_params=_params(("parallel",)),
    )(x, w)


def _rmsnorm_bwd(x, w, dy, dres, name):
    t, d = x.shape
    tr = _rows(t)

    def body(x_ref, w_ref, dy_ref, dres_ref, dx_ref, dw_ref):
        i = pl.program_id(0)
        xv = x_ref[...]
        rstd = lax.rsqrt(jnp.mean(xv * xv, axis=-1, keepdims=True) + RMS_EPS)
        xhat = xv * rstd
        dyv = dy_ref[...]
        dxhat = dyv * w_ref[...]
        m = jnp.mean(dxhat * xhat, axis=-1, keepdims=True)
        dx_ref[...] = rstd * (dxhat - xhat * m) + dres_ref[...]
        part = jnp.sum(dyv * xhat, axis=0, keepdims=True)

        @pl.when(i == 0)
        def _():
            dw_ref[...] = part

        @pl.when(i > 0)
        def _():
            dw_ref[...] += part

    row = pl.BlockSpec((tr, d), lambda i: (i, 0))
    vec = pl.BlockSpec((1, d), lambda i: (0, 0))
    return pl.pallas_call(
        body, name=name, grid=(t // tr,),
        in_specs=[row, vec, row, row], out_specs=[row, vec],
        out_shape=[jax.ShapeDtypeStruct((t, d), F32), jax.ShapeDtypeStruct((1, d), F32)],
        compiler_params=_params(("arbitrary",)),
    )(x, w, dy, dres)


def _loss_head(h2, w, tgt):
    t, d = h2.shape
    tr = _rows(t)

    def body(x_ref, w_ref, t_ref, dx_ref, dw_ref, ls_ref):
        i = pl.program_id(0)
        xv = x_ref[...]
        wv = w_ref[...]
        rstd = lax.rsqrt(jnp.mean(xv * xv, axis=-1, keepdims=True) + RMS_EPS)
        xhat = xv * rstd
        err = xhat * wv - t_ref[...]
        dyv = err * (1.0 / d)
        dxhat = dyv * wv
        m = jnp.mean(dxhat * xhat, axis=-1, keepdims=True)
        dx_ref[...] = rstd * (dxhat - xhat * m)
        dw_part = jnp.sum(dyv * xhat, axis=0, keepdims=True)
        ls_part = jnp.sum(err * err, axis=0, keepdims=True)

        @pl.when(i == 0)
        def _():
            dw_ref[...] = dw_part
            ls_ref[...] = ls_part

        @pl.when(i > 0)
        def _():
            dw_ref[...] += dw_part
            ls_ref[...] += ls_part

    row = pl.BlockSpec((tr, d), lambda i: (i, 0))
    vec = pl.BlockSpec((1, d), lambda i: (0, 0))
    return pl.pallas_call(
        body, name="loss_head", grid=(t // tr,),
        in_specs=[row, vec, row], out_specs=[row, vec, vec],
        out_shape=[jax.ShapeDtypeStruct((t, d), F32), jax.ShapeDtypeStruct((1, d), F32),
                   jax.ShapeDtypeStruct((1, d), F32)],
        compiler_params=_params(("arbitrary",)),
    )(h2, w, tgt)


def _merge_fwd(proj, bg, ys, yl):
    t = proj.shape[0]
    d = D_MODEL
    tr = _rows(t)

    def body(ps_ref, pl_ref, bg_ref, ys_ref, yl_ref, o_ref):
        gs = _sigmoid(ps_ref[...] + bg_ref[:, 0:d])
        gl = _sigmoid(pl_ref[...] + bg_ref[:, d:2 * d])
        o_ref[...] = (gs * ys_ref[...] + gl * yl_ref[...]).astype(BF16)

    row = pl.BlockSpec((tr, d), lambda i: (i, 0))
    return pl.pallas_call(
        body, name="merge_fwd", grid=(t // tr,),
        in_specs=[row, pl.BlockSpec((tr, d), lambda i: (i, 1)), pl.BlockSpec((1, 2 * d), lambda i: (0, 0)), row, row],
        out_specs=row, out_shape=jax.ShapeDtypeStruct((t, d), BF16),
        compiler_params=_params(("parallel",)),
    )(proj, proj, bg, ys, yl)


def _merge_bwd(proj, bg, ys, yl, dm):
    t = proj.shape[0]
    d = D_MODEL
    tr = _rows(t)

    def body(ps_ref, pl_ref, bg_ref, ys_ref, yl_ref, dm_ref, dys_ref, dyl_ref, dg_ref, dbg_ref):
        i = pl.program_id(0)
        gs = _sigmoid(ps_ref[...] + bg_ref[:, 0:d])
        gl = _sigmoid(pl_ref[...] + bg_ref[:, d:2 * d])
        dmv = dm_ref[...]
        dys_ref[...] = (dmv * gs).astype(BF16)
        dyl_ref[...] = (dmv * gl).astype(BF16)
        dgs = dmv * ys_ref[...] * gs * (1.0 - gs)
        dgl = dmv * yl_ref[...] * gl * (1.0 - gl)
        dg_ref[:, 0:d] = dgs.astype(BF16)
        dg_ref[:, d:2 * d] = dgl.astype(BF16)

        @pl.when(i == 0)
        def _():
            dbg_ref[...] = jnp.zeros_like(dbg_ref)

        dbg_ref[:, 0:d] += jnp.sum(dgs, axis=0, keepdims=True)
        dbg_ref[:, d:2 * d] += jnp.sum(dgl, axis=0, keepdims=True)

    row = pl.BlockSpec((tr, d), lambda i: (i, 0))
    wide = pl.BlockSpec((tr, 2 * d), lambda i: (i, 0))
    vec = pl.BlockSpec((1, 2 * d), lambda i: (0, 0))
    return pl.pallas_call(
        body, name="merge_bwd", grid=(t // tr,),
        in_specs=[row, pl.BlockSpec((tr, d), lambda i: (i, 1)), vec, row, row, row],
        out_specs=[row, row, wide, vec],
        out_shape=[jax.ShapeDtypeStruct((t, d), BF16), jax.ShapeDtypeStruct((t, d), BF16),
                   jax.ShapeDtypeStruct((t, PROJ_W), BF16), jax.ShapeDtypeStruct((1, 2 * d), F32)],
        compiler_params=_params(("arbitrary",)),
    )(proj, proj, bg, ys, yl, dm)


def _swiglu_fwd(gu):
    t = gu.shape[0]
    f = FFN_HIDDEN
    tr = _rows(t)

    def body(g_ref, u_ref, o_ref):
        g = g_ref[...]
        o_ref[...] = (g * _sigmoid(g) * u_ref[...]).astype(BF16)

    return pl.pallas_call(
        body, name="swiglu_fwd", grid=(t // tr,),
        in_specs=[pl.BlockSpec((tr, f), lambda i: (i, 0)), pl.BlockSpec((tr, f), lambda i: (i, 1))],
        out_specs=pl.BlockSpec((tr, f), lambda i: (i, 0)),
        out_shape=jax.ShapeDtypeStruct((t, f), BF16),
        compiler_params=_params(("parallel",)),
    )(gu, gu)


def _swiglu_bwd(gu, dact):
    t = gu.shape[0]
    f = FFN_HIDDEN
    tr = _rows(t)

    def body(g_ref, u_ref, da_ref, o_ref):
        g = g_ref[...]
        sg = _sigmoid(g)
        da = da_ref[...]
        o_ref[:, 0:f] = (da * u_ref[...] * (sg * (1.0 + g * (1.0 - sg)))).astype(BF16)
        o_ref[:, f:2 * f] = (da * g * sg).astype(BF16)

    return pl.pallas_call(
        body, name="swiglu_bwd", grid=(t // tr,),
        in_specs=[pl.BlockSpec((tr, f), lambda i: (i, 0)), pl.BlockSpec((tr, f), lambda i: (i, 1)),
                  pl.BlockSpec((tr, f), lambda i: (i, 0))],
        out_specs=pl.BlockSpec((tr, 2 * f), lambda i: (i, 0)),
        out_shape=jax.ShapeDtypeStruct((t, 2 * f), BF16),
        compiler_params=_params(("parallel",)),
    )(gu, gu, dact)


def _conv_pre(xv, wv, bv):
    pre = bv + wv[CONV_K - 1:CONV_K, :] * xv
    for k in range(CONV_K - 1):
        pre = pre + wv[k:k + 1, :] * _shift_down(xv, CONV_K - 1 - k)
    return pre


def _ssm_conv_fwd(proj, w, b):
    t = proj.shape[0]
    nb = SSM_CONV_DIM // LANES
    c0 = COL_XBC // LANES

    def body(x_ref, w_ref, b_ref, o_ref):
        pre = _conv_pre(x_ref[...], w_ref[...], b_ref[...])
        o_ref[...] = pre * _sigmoid(pre)

    return pl.pallas_call(
        body, name="ssm_conv_fwd", grid=(nb,),
        in_specs=[pl.BlockSpec((t, LANES), lambda j: (0, c0 + j)), pl.BlockSpec((CONV_K, LANES), lambda j: (0, j)),
                  pl.BlockSpec((1, LANES), lambda j: (0, j))],
        out_specs=pl.BlockSpec((t, LANES), lambda j: (0, j)),
        out_shape=jax.ShapeDtypeStruct((t, SSM_CONV_DIM), F32),
        compiler_params=_params(("parallel",)),
    )(proj, w, b)


def _ssm_conv_bwd(proj, w, b, dact, dproj):
    t = proj.shape[0]
    nb = SSM_CONV_DIM // LANES
    c0 = COL_XBC // LANES

    def body(x_ref, w_ref, b_ref, da_ref, dproj_in, dx_ref, dw_ref, db_ref):
        xv = x_ref[...]
        wv = w_ref[...]
        pre = _conv_pre(xv, wv, b_ref[...])
        sg = _sigmoid(pre)
        dpre = da_ref[...] * (sg * (1.0 + pre * (1.0 - sg)))
        dx = wv[CONV_K - 1:CONV_K, :] * dpre
        for k in range(CONV_K - 1):
            dx = dx + wv[k:k + 1, :] * _shift_up(dpre, CONV_K - 1 - k)
        dx_ref[...] = dx.astype(BF16)
        for k in range(CONV_K):
            dw_ref[k:k + 1, :] = jnp.sum(dpre * _shift_down(xv, CONV_K - 1 - k), axis=0, keepdims=True)
        db_ref[...] = jnp.sum(dpre, axis=0, keepdims=True)

    col = pl.BlockSpec((t, LANES), lambda j: (0, j))
    wsp = pl.BlockSpec((CONV_K, LANES), lambda j: (0, j))
    bsp = pl.BlockSpec((1, LANES), lambda j: (0, j))
    return pl.pallas_call(
        body, name="ssm_conv_bwd", grid=(nb,),
        in_specs=[pl.BlockSpec((t, LANES), lambda j: (0, c0 + j)), wsp, bsp, col, HBM],
        out_specs=[pl.BlockSpec((t, LANES), lambda j: (0, c0 + j)), wsp, bsp],
        out_shape=[jax.ShapeDtypeStruct(dproj.shape, dproj.dtype), jax.ShapeDtypeStruct((CONV_K, SSM_CONV_DIM), F32),
                   jax.ShapeDtypeStruct((1, SSM_CONV_DIM), F32)],
        input_output_aliases={4: 0},
        compiler_params=_params(("parallel",)),
    )(proj, w, b, dact, dproj)


def _ssd_chunk_terms(dtr, bias, alog):
    a = -jnp.exp(alog)
    dt = _softplus(dtr + bias)
    row = _iota((SSM_CHUNK, SSM_CHUNK), 0)
    col = _iota((SSM_CHUNK, SSM_CHUNK), 1)
    tri = (row >= col).astype(F32)
    cs = _fdot(tri, dt * a)
    dec = jnp.exp(cs[SSM_CHUNK - 1:SSM_CHUNK, :] - cs)
    ecs = jnp.exp(cs)
    off = _iota((LANES, SSM_INNER), 1) - SSM_HEAD_DIM * _iota((LANES, SSM_INNER), 0)
    expand = jnp.where(jnp.logical_and(off >= 0, off < SSM_HEAD_DIM), 1.0, 0.0).astype(BF16)
    return a, dt, cs, dec, ecs, expand, row, col


def _ssd_specs(t):
    nc = t // SSM_CHUNK
    xs = pl.BlockSpec((SSM_CHUNK, SSM_INNER), lambda c: (c, 0))
    bm = pl.BlockSpec((SSM_CHUNK, SSM_BC), lambda c: (c, SSM_INNER // SSM_BC))
    cm = pl.BlockSpec((SSM_CHUNK, SSM_BC), lambda c: (c, SSM_INNER // SSM_BC + 1))
    dtr = pl.BlockSpec((SSM_CHUNK, LANES), lambda c: (c, COL_DT // LANES))
    vec = pl.BlockSpec((1, LANES), lambda c: (0, 0))
    wide = pl.BlockSpec((1, SSM_INNER), lambda c: (0, 0))
    return nc, xs, bm, cm, dtr, vec, wide


def _ssd_fwd(xbc_act, proj, bias, alog, dexp, gather):
    t = proj.shape[0]
    nc, xs_s, bm_s, cm_s, dtr_s, vec, wide = _ssd_specs(t)
    n = len(gather)

    def body(*refs):
        xs_ref, b_ref, c_ref, dtr_ref, bias_ref, alog_ref, dexp_ref = refs[:7]
        y_ref, sin_ref = refs[7 + n:9 + n]
        state = refs[9 + 2 * n]
        comm = (refs[7:7 + n], refs[9 + n:9 + 2 * n]) + tuple(refs[10 + 2 * n:])
        chunk = pl.program_id(0)

        @pl.when(chunk == 0)
        def _():
            _gather_phase("start", *comm)
            state[...] = jnp.zeros_like(state)

        @pl.when(chunk == nc // 2)
        def _():
            _gather_phase("forward", *comm)

        a, dt, cs, dec, ecs, expand, row, col = _ssd_chunk_terms(dtr_ref[...], bias_ref[...], alog_ref[...])
        cst = cs.T
        dt_x = _split_dot(dt, expand)
        dec_x = _split_dot(dec, expand)
        ecs_x = _split_dot(ecs, expand)
        xs = xs_ref[...]
        xdt = xs * dt_x
        xdec = xdt * dec_x
        lane_lo = col < SSM_HEAD_DIM
        causal = row >= col
        sin_ref[0] = state[...]
        for g in range(SSM_GROUPS):
            bg = b_ref[:, g * SSM_STATE:(g + 1) * SSM_STATE].astype(BF16)
            cg = c_ref[:, g * SSM_STATE:(g + 1) * SSM_STATE].astype(BF16)
            cb = _bdot(cg, bg, NT)
            for q in range(SSM_PAIRS // SSM_GROUPS):
                pq = g * (SSM_PAIRS // SSM_GROUPS) + q
                sl = slice(pq * LANES, (pq + 1) * LANES)
                xp = xdt[:, sl].astype(BF16)
                yd = []
                for hh in range(2):
                    h = 2 * pq + hh
                    lmat = jnp.exp(jnp.where(causal, cs[:, h:h + 1] - cst[h:h + 1, :], -jnp.inf))
                    yd.append(_bdot(cb * lmat, xp))
                s_in = state[pq]
                y_off = _bdot(cg, s_in) * ecs_x[:, sl]
                y_ref[:, sl] = jnp.where(lane_lo, yd[0], yd[1]) + y_off + xs[:, sl] * dexp_ref[:, sl]
                state[pq] = s_in * ecs_x[SSM_CHUNK - 1:SSM_CHUNK, sl] + _bdot(bg, xdec[:, sl], TN)

        @pl.when(chunk == nc - 1)
        def _():
            _gather_phase("finish", *comm)

    out = pl.pallas_call(
        body, name="ssd_fwd", grid=(nc,),
        in_specs=[xs_s, bm_s, cm_s, dtr_s, vec, vec, wide] + [HBM] * n,
        out_specs=[pl.BlockSpec((SSM_CHUNK, SSM_INNER), lambda c: (c, 0)),
                   pl.BlockSpec((1, SSM_PAIRS, SSM_STATE, LANES), lambda c: (c, 0, 0, 0))] + [HBM] * n,
        out_shape=[jax.ShapeDtypeStruct((t, SSM_INNER), F32),
                   jax.ShapeDtypeStruct((nc, SSM_PAIRS, SSM_STATE, LANES), F32)]
        + [jax.ShapeDtypeStruct((N_DEV,) + v.shape, v.dtype) for v in gather],
        scratch_shapes=[pltpu.VMEM((SSM_PAIRS, SSM_STATE, LANES), F32)] + _comm_scratch(n),
        compiler_params=_params(("arbitrary",)),
    )(xbc_act, xbc_act, xbc_act, proj, bias, alog, dexp, *gather)
    return out[:2], out[2:]


def _ssd_bwd(xbc_act, proj, s_in_all, dy, bias, alog, dexp, dproj, exchange):
    n_ex = len(exchange)
    t = proj.shape[0]
    nc = t // SSM_CHUNK
    last = nc - 1
    xs_s = pl.BlockSpec((SSM_CHUNK, SSM_INNER), lambda c: (last - c, 0))
    bm_s = pl.BlockSpec((SSM_CHUNK, SSM_BC), lambda c: (last - c, SSM_INNER // SSM_BC))
    cm_s = pl.BlockSpec((SSM_CHUNK, SSM_BC), lambda c: (last - c, SSM_INNER // SSM_BC + 1))
    dtr_s = pl.BlockSpec((SSM_CHUNK, LANES), lambda c: (last - c, COL_DT // LANES))
    sin_s = pl.BlockSpec((1, SSM_PAIRS, SSM_STATE, LANES), lambda c: (last - c, 0, 0, 0))
    vec = pl.BlockSpec((1, LANES), lambda c: (0, 0))
    wide = pl.BlockSpec((1, SSM_INNER), lambda c: (0, 0))

    def body(*refs):
        xs_ref, b_ref, c_ref, dtr_ref, sin_ref, dy_ref, bias_ref, alog_ref, dexp_ref = refs[:9]
        dxbc_ref, ddtr_ref, dbias_ref, dalog_ref, ddcol_ref = refs[10 + n_ex:15 + n_ex]
        dstate, dxdt_s, yoff_s, rx_s, trow_s = refs[15 + 2 * n_ex:20 + 2 * n_ex]
        comm = (refs[10:10 + n_ex], refs[15 + n_ex:15 + 2 * n_ex]) + tuple(refs[20 + 2 * n_ex:])

        @pl.when(pl.program_id(0) == 0)
        def _():
            _exchange_phase("start", *comm)
            dstate[...] = jnp.zeros_like(dstate)
            trow_s[...] = jnp.zeros_like(trow_s)
            dbias_ref[...] = jnp.zeros_like(dbias_ref)
            dalog_ref[...] = jnp.zeros_like(dalog_ref)
            ddcol_ref[...] = jnp.zeros_like(ddcol_ref)

        dtr = dtr_ref[...]
        a, dt, cs, dec, ecs, expand, row, col = _ssd_chunk_terms(dtr, bias_ref[...], alog_ref[...])
        cst = cs.T
        dt_x = _split_dot(dt, expand)
        dec_x = _split_dot(dec, expand)
        ecs_x = _split_dot(ecs, expand)
        xs = xs_ref[...]
        dyv = dy_ref[...]
        xdt = xs * dt_x
        lane_lo = col < SSM_HEAD_DIM
        causal = row >= col
        ddcol_ref[...] += jnp.sum(dyv * xs, axis=0, keepdims=True)
        dcs_col = jnp.zeros((SSM_CHUNK, LANES), F32)
        dcs_row = jnp.zeros((LANES, SSM_CHUNK), F32)
        for g in range(SSM_GROUPS):
            bg = b_ref[:, g * SSM_STATE:(g + 1) * SSM_STATE].astype(BF16)
            cg = c_ref[:, g * SSM_STATE:(g + 1) * SSM_STATE].astype(BF16)
            cb = _bdot(cg, bg, NT)
            dgm = jnp.zeros((SSM_CHUNK, SSM_CHUNK), F32)
            dbg = jnp.zeros((SSM_CHUNK, SSM_STATE), F32)
            dcg = jnp.zeros((SSM_CHUNK, SSM_STATE), F32)
            for q in range(SSM_PAIRS // SSM_GROUPS):
                pq = g * (SSM_PAIRS // SSM_GROUPS) + q
                sl = slice(pq * LANES, (pq + 1) * LANES)
                dyp = dyv[:, sl]
                xp = xdt[:, sl]
                dxh = []
                for hh in range(2):
                    h = 2 * pq + hh
                    lmat = jnp.exp(jnp.where(causal, cs[:, h:h + 1] - cst[h:h + 1, :], -jnp.inf))
                    mmat = cb * lmat
                    dyh = jnp.where(lane_lo if hh == 0 else jnp.logical_not(lane_lo), dyp, 0.0)
                    dmm = _bdot(dyh, xp, NT)
                    pm = dmm * mmat
                    dcs_col = jnp.where(col == h, jnp.sum(pm, axis=1, keepdims=True), dcs_col)
                    dcs_row = jnp.where(row == h, jnp.sum(pm, axis=0, keepdims=True), dcs_row)
                    dgm = dgm + dmm * lmat
                    dxh.append(_bdot(mmat, dyp, TN))
                s_in = sin_ref[0, pq]
                ecs_p = ecs_x[:, sl]
                dec_p = dec_x[:, sl]
                etot_p = ecs_x[SSM_CHUNK - 1:SSM_CHUNK, sl]
                yoff_s[:, sl] = dyp * (_bdot(cg, s_in) * ecs_p)
                dq = dyp * ecs_p
                dcg = dcg + _bdot(dq, s_in, NT)
                ds = dstate[pq]
                r = _bdot(bg, ds)
                rx_s[:, sl] = r * xp
                dxdt_s[:, sl] = jnp.where(lane_lo, dxh[0], dxh[1]) + dec_p * r
                dbg = dbg + _bdot(xp * dec_p, ds, NT)
                trow_s[0:1, sl] = jnp.sum(ds * s_in, axis=0, keepdims=True) * etot_p
                dstate[pq] = etot_p * ds + _bdot(cg, dq, TN)
            dcg = dcg + _bdot(dgm, bg)
            dbg = dbg + _bdot(dgm, cg, TN)
            dxbc_ref[:, SSM_INNER + g * SSM_STATE:SSM_INNER + (g + 1) * SSM_STATE] = dbg
            dxbc_ref[:, SSM_INNER + SSM_BC + g * SSM_STATE:SSM_INNER + SSM_BC + (g + 1) * SSM_STATE] = dcg
        ddec = _split_dot(rx_s[...], expand, NT) * dec
        dtot = _split_dot(trow_s[...], expand, NT)[0:1, :]
        dcs = dcs_col - dcs_row.T + _split_dot(yoff_s[...], expand, NT) - ddec
        dcs = dcs + jnp.where(row == SSM_CHUNK - 1, jnp.sum(ddec, axis=0, keepdims=True) + dtot, 0.0)
        da = _fdot((row <= col).astype(F32), dcs)
        dxdt = dxdt_s[...]
        ddt = da * a + _split_dot(dxdt * xs, expand, NT)
        dalog_ref[...] += jnp.sum(da * dt, axis=0, keepdims=True) * a
        ddtr = ddt * _sigmoid(dtr + bias_ref[...])
        ddtr_ref[...] = ddtr.astype(BF16)
        dbias_ref[...] += jnp.sum(ddtr, axis=0, keepdims=True)
        dxbc_ref[:, 0:SSM_INNER] = dxdt * dt_x + dyv * dexp_ref[...]

        @pl.when(pl.program_id(0) == last)
        def _():
            _exchange_phase("finish", *comm)

    out = pl.pallas_call(
        body, name="ssd_bwd", grid=(nc,),
        in_specs=[xs_s, bm_s, cm_s, dtr_s, sin_s, pl.BlockSpec((SSM_CHUNK, SSM_INNER), lambda c: (last - c, 0)),
                  vec, vec, wide, HBM] + [HBM] * n_ex,
        out_specs=[pl.BlockSpec((SSM_CHUNK, SSM_CONV_DIM), lambda c: (last - c, 0)), dtr_s, vec, vec, wide]
        + [HBM] * n_ex,
        out_shape=[jax.ShapeDtypeStruct((t, SSM_CONV_DIM), F32), jax.ShapeDtypeStruct(dproj.shape, dproj.dtype),
                   jax.ShapeDtypeStruct((1, LANES), F32), jax.ShapeDtypeStruct((1, LANES), F32),
                   jax.ShapeDtypeStruct((1, SSM_INNER), F32)]
        + [jax.ShapeDtypeStruct(p.shape, p.dtype) for p in exchange],
        input_output_aliases={9: 1},
        scratch_shapes=[pltpu.VMEM((SSM_PAIRS, SSM_STATE, LANES), F32),
                        pltpu.VMEM((SSM_CHUNK, SSM_INNER), F32), pltpu.VMEM((SSM_CHUNK, SSM_INNER), F32),
                        pltpu.VMEM((SSM_CHUNK, SSM_INNER), F32), pltpu.VMEM((SUBLANES, SSM_INNER), F32)]
        + _comm_scratch(n_ex),
        compiler_params=_params(("arbitrary",)),
    )(xbc_act, xbc_act, xbc_act, proj, s_in_all, dy, bias, alog, dexp, dproj, *exchange)
    return out[:5], out[5:]


def _group_rstd(y):
    n = SSM_INNER // SSM_GROUPS
    parts = []
    for g in range(SSM_GROUPS):
        yg = y[:, g * n:(g + 1) * n]
        r = lax.rsqrt(jnp.mean(yg * yg, axis=-1, keepdims=True) + RMS_EPS)
        parts.append(jnp.broadcast_to(r, yg.shape))
    return jnp.concatenate(parts, axis=1)


def _group_mean(v):
    n = SSM_INNER // SSM_GROUPS
    parts = []
    for g in range(SSM_GROUPS):
        vg = v[:, g * n:(g + 1) * n]
        parts.append(jnp.broadcast_to(jnp.mean(vg, axis=-1, keepdims=True), vg.shape))
    return jnp.concatenate(parts, axis=1)


def _ssm_post_fwd(y_ssd, proj, nw):
    t = proj.shape[0]
    n = SSM_INNER
    tr = _rows(t)

    def body(y_ref, z_ref, nw_ref, o_ref):
        z = z_ref[...]
        y = y_ref[...] * (z * _sigmoid(z))
        o_ref[...] = (y * _group_rstd(y) * nw_ref[...]).astype(BF16)

    row = pl.BlockSpec((tr, n), lambda i: (i, 0))
    return pl.pallas_call(
        body, name="ssm_post_fwd", grid=(t // tr,),
        in_specs=[row, pl.BlockSpec((tr, n), lambda i: (i, COL_Z // n)), pl.BlockSpec((1, n), lambda i: (0, 0))],
        out_specs=row, out_shape=jax.ShapeDtypeStruct((t, n), BF16),
        compiler_params=_params(("parallel",)),
    )(y_ssd, proj, nw)


def _ssm_post_bwd(y_ssd, proj, nw, dout, dproj):
    t = proj.shape[0]
    n = SSM_INNER
    tr = _rows(t)

    def body(y_ref, z_ref, nw_ref, do_ref, dproj_in, dy_ref, dz_ref, dnw_ref):
        i = pl.program_id(0)
        z = z_ref[...]
        sg = _sigmoid(z)
        sz = z * sg
        ys = y_ref[...]
        y = ys * sz
        rstd = _group_rstd(y)
        yn = y * rstd
        dov = do_ref[...]
        dyn = dov * nw_ref[...]
        dyg = rstd * (dyn - yn * _group_mean(dyn * yn))
        dy_ref[...] = dyg * sz
        dz_ref[...] = (dyg * ys * (sg * (1.0 + z * (1.0 - sg)))).astype(BF16)
        part = jnp.sum(dov * yn, axis=0, keepdims=True)

        @pl.when(i == 0)
        def _():
            dnw_ref[...] = part

        @pl.when(i > 0)
        def _():
            dnw_ref[...] += part

    row = pl.BlockSpec((tr, n), lambda i: (i, 0))
    vec = pl.BlockSpec((1, n), lambda i: (0, 0))
    return pl.pallas_call(
        body, name="ssm_post_bwd", grid=(t // tr,),
        in_specs=[row, pl.BlockSpec((tr, n), lambda i: (i, COL_Z // n)), vec, row, HBM],
        out_specs=[row, pl.BlockSpec((tr, n), lambda i: (i, COL_Z // n)), vec],
        out_shape=[jax.ShapeDtypeStruct((t, n), F32), jax.ShapeDtypeStruct(dproj.shape, dproj.dtype),
                   jax.ShapeDtypeStruct((1, n), F32)],
        input_output_aliases={4: 1},
        compiler_params=_params(("arbitrary",)),
    )(y_ssd, proj, nw, dout, dproj)


SCAN_UNROLL = 8
GELU_C = math.sqrt(2.0 / math.pi)
GELU_K = 0.044715


def _gelu_parts(y):
    th = jnp.tanh(GELU_C * (y + GELU_K * y * y * y))
    val = 0.5 * y * (1.0 + th)
    grad = 0.5 * (1.0 + th) + 0.5 * y * (1.0 - th * th) * GELU_C * (1.0 + 3.0 * GELU_K * y * y)
    return val, grad


def _scan_tiles(a_ref, b_ref, h_ref, n_rows, reverse):
    n_tiles = n_rows // SUBLANES
    shape = (SUBLANES, a_ref.shape[1])
    row = _iota(shape, 0)

    def in_tile(av, bv):
        for s in (1, 2, 4):
            if reverse:
                keep = row < SUBLANES - s
                a_sh = jnp.where(keep, pltpu.roll(av, SUBLANES - s, 0), 1.0)
                b_sh = jnp.where(keep, pltpu.roll(bv, SUBLANES - s, 0), 0.0)
            else:
                keep = row >= s
                a_sh = jnp.where(keep, pltpu.roll(av, s, 0), 1.0)
                b_sh = jnp.where(keep, pltpu.roll(bv, s, 0), 0.0)
            bv = av * b_sh + bv
            av = av * a_sh
        return av, bv

    def step(k, carry):
        first = (n_tiles // SCAN_UNROLL - 1 - k) if reverse else k
        tiles = [first * SCAN_UNROLL + j for j in range(SCAN_UNROLL)]
        if reverse:
            tiles = tiles[::-1]
        ats = [pl.ds(pl.multiple_of(tile * SUBLANES, SUBLANES), SUBLANES) for tile in tiles]
        scanned = [in_tile(a_ref[at, :], b_ref[at, :]) for at in ats]
        for at, (av, bv) in zip(ats, scanned):
            hv = bv + av * carry
            h_ref[at, :] = hv
            carry = hv[0:1, :] if reverse else hv[SUBLANES - 1:SUBLANES, :]
        return carry

    assert n_tiles % SCAN_UNROLL == 0, n_rows
    lax.fori_loop(0, n_tiles // SCAN_UNROLL, step, jnp.zeros((1, a_ref.shape[1]), F32))


def _lru_gates(xl, cw, cb, wr, br, wi, bi, lam):
    u = cb + cw[CONV_K - 1:CONV_K, :] * xl
    for k in range(CONV_K - 1):
        u = u + cw[k:k + 1, :] * _shift_down(xl, CONV_K - 1 - k)
    r = _sigmoid(_bdot(u, wr) + br)
    i = _sigmoid(_bdot(u, wi) + bi)
    sp = _softplus(-lam)
    la = -LRU_C * r * sp
    a = jnp.exp(la)
    mult = jnp.sqrt(-jnp.tanh(la) * (a * a + 1.0))
    return u, r, i, sp, a, mult


def _lru_specs(t):
    c0 = COL_LRU // LANES
    xl = pl.BlockSpec((t, LANES), lambda j: (0, c0 + 2 * j))
    yl = pl.BlockSpec((t, LANES), lambda j: (0, c0 + 2 * j + 1))
    col = pl.BlockSpec((t, LANES), lambda j: (0, j))
    cw = pl.BlockSpec((CONV_K, LANES), lambda j: (0, j))
    vec = pl.BlockSpec((1, LANES), lambda j: (0, j))
    wblk = pl.BlockSpec((1, LANES, LANES), lambda j: (j, 0, 0))
    return xl, yl, col, cw, vec, wblk


def _lru_fwd(proj, cw, cb, wr, br, wi, bi, lam, gather):
    t = proj.shape[0]
    xl_s, yl_s, col, cw_s, vec, wblk = _lru_specs(t)
    n = len(gather)

    def body(*refs):
        xl_ref, yl_ref, cw_ref, cb_ref, wr_ref, br_ref, wi_ref, bi_ref, lam_ref = refs[:9]
        o_ref, h_ref = refs[9 + n:11 + n]
        a_s, b_s = refs[11 + 2 * n:13 + 2 * n]
        comm = (refs[9:9 + n], refs[11 + n:11 + 2 * n]) + tuple(refs[13 + 2 * n:])
        j = pl.program_id(0)
        for step, phase in ((0, "start"), (LRU_BLOCKS // 2, "forward")):
            @pl.when(j == step)
            def _():
                _gather_phase(phase, *comm)

        u, r, i, sp, a, mult = _lru_gates(xl_ref[...], cw_ref[...], cb_ref[...], wr_ref[0], br_ref[...],
                                          wi_ref[0], bi_ref[...], lam_ref[...])
        a_s[...] = a
        b_s[...] = mult * (i * u)
        _scan_tiles(a_s, b_s, h_ref, t, reverse=False)
        o_ref[...] = (h_ref[...] * _gelu_parts(yl_ref[...])[0]).astype(BF16)

        @pl.when(j == LRU_BLOCKS - 1)
        def _():
            _gather_phase("finish", *comm)

    out = pl.pallas_call(
        body, name="lru_fwd", grid=(LRU_BLOCKS,),
        in_specs=[xl_s, yl_s, cw_s, vec, wblk, vec, wblk, vec, vec] + [HBM] * n,
        out_specs=[col, col] + [HBM] * n,
        out_shape=[jax.ShapeDtypeStruct((t, LRU_WIDTH), BF16), jax.ShapeDtypeStruct((t, LRU_WIDTH), F32)]
        + [jax.ShapeDtypeStruct((N_DEV,) + v.shape, v.dtype) for v in gather],
        scratch_shapes=[pltpu.VMEM((t, LANES), F32)] * 2 + _comm_scratch(n),
        compiler_params=_params(("arbitrary",), big=True),
    )(proj, proj, cw, cb, wr, br, wi, bi, lam, *gather)
    return out[:2], out[2:]


def _lru_bwd(proj, cw, cb, wr, br, wi, bi, lam, h_all, dout, dproj, exchange):
    t = proj.shape[0]
    xl_s, yl_s, col, cw_s, vec, wblk = _lru_specs(t)
    pair = pl.BlockSpec((t, 2 * LANES), lambda j: (0, COL_LRU // (2 * LANES) + j))
    n_ex = len(exchange)

    def body(*refs):
        xl_ref, yl_ref, cw_ref, cb_ref, wr_ref, br_ref, wi_ref, bi_ref, lam_ref, h_ref, do_ref = refs[:11]
        dxy_ref, dcw_ref, dcb_ref, dwr_ref, dbr_ref, dwi_ref, dbi_ref, dlam_ref = refs[12 + n_ex:20 + n_ex]
        a_s, b_s, g_s = refs[20 + 2 * n_ex:23 + 2 * n_ex]
        comm = (refs[12:12 + n_ex], refs[20 + n_ex:20 + 2 * n_ex]) + tuple(refs[23 + 2 * n_ex:])

        @pl.when(pl.program_id(0) == 0)
        def _():
            _exchange_phase("start", *comm)

        xl = xl_ref[...]
        cwv = cw_ref[...]
        lam = lam_ref[...]
        u, r, i, sp, a, mult = _lru_gates(xl, cwv, cb_ref[...], wr_ref[0], br_ref[...], wi_ref[0], bi_ref[...], lam)
        v = i * u
        gl, dgl = _gelu_parts(yl_ref[...])
        dov = do_ref[...]
        h = h_ref[...]
        dxy_ref[:, LANES:2 * LANES] = (dov * h * dgl).astype(BF16)
        b_s[...] = dov * gl
        a_s[...] = _shift_up(a, 1)
        _scan_tiles(a_s, b_s, g_s, t, reverse=True)
        g = g_s[...]
        da = g * _shift_down(h, 1)
        dmult = g * v
        dv = g * mult
        dla = da * a - dmult * (a * a) / mult
        dr = dla * (-LRU_C * sp)
        dsp = jnp.sum(dla * (-LRU_C * r), axis=0, keepdims=True)
        dlam_ref[...] = -dsp * _sigmoid(-lam)
        dpr = dr * r * (1.0 - r)
        dpi = dv * u * i * (1.0 - i)
        dbr_ref[...] = jnp.sum(dpr, axis=0, keepdims=True)
        dbi_ref[...] = jnp.sum(dpi, axis=0, keepdims=True)
        dwr_ref[0] = _bdot(u, dpr, TN)
        dwi_ref[0] = _bdot(u, dpi, TN)
        du = dv * i + _bdot(dpr, wr_ref[0], NT) + _bdot(dpi, wi_ref[0], NT)
        dxl = cwv[CONV_K - 1:CONV_K, :] * du
        for k in range(CONV_K - 1):
            dxl = dxl + cwv[k:k + 1, :] * _shift_up(du, CONV_K - 1 - k)
        dxy_ref[:, 0:LANES] = dxl.astype(BF16)
        for k in range(CONV_K):
            dcw_ref[k:k + 1, :] = jnp.sum(du * _shift_down(xl, CONV_K - 1 - k), axis=0, keepdims=True)
        dcb_ref[...] = jnp.sum(du, axis=0, keepdims=True)

        @pl.when(pl.program_id(0) == LRU_BLOCKS - 1)
        def _():
            _exchange_phase("finish", *comm)

    out = pl.pallas_call(
        body, name="lru_bwd", grid=(LRU_BLOCKS,),
        in_specs=[xl_s, yl_s, cw_s, vec, wblk, vec, wblk, vec, vec, col, col, HBM] + [HBM] * n_ex,
        out_specs=[pair, cw_s, vec, wblk, vec, wblk, vec, vec] + [HBM] * n_ex,
        input_output_aliases={11: 0},
        out_shape=[jax.ShapeDtypeStruct(dproj.shape, dproj.dtype),
                   jax.ShapeDtypeStruct((CONV_K, LRU_WIDTH), F32), jax.ShapeDtypeStruct((1, LRU_WIDTH), F32),
                   jax.ShapeDtypeStruct((LRU_BLOCKS, LANES, LANES), F32), jax.ShapeDtypeStruct((1, LRU_WIDTH), F32),
                   jax.ShapeDtypeStruct((LRU_BLOCKS, LANES, LANES), F32), jax.ShapeDtypeStruct((1, LRU_WIDTH), F32),
                   jax.ShapeDtypeStruct((1, LRU_WIDTH), F32)]
        + [jax.ShapeDtypeStruct(p.shape, p.dtype) for p in exchange],
        scratch_shapes=[pltpu.VMEM((t, LANES), F32)] * 3 + _comm_scratch(n_ex),
        compiler_params=_params(("arbitrary",), big=True),
    )(proj, proj, cw, cb, wr, br, wi, bi, lam, h_all, dout, dproj, *exchange)
    return out[:8], out[8:]


def _mesh_pos():
    return lax.axis_index("x"), lax.axis_index("y"), lax.axis_index("c")


HBM = pl.BlockSpec(memory_space=pl.ANY)


def _comm_scratch(n):
    return [pltpu.SemaphoreType.DMA((n, 7)), pltpu.SemaphoreType.DMA((n, 7)), pltpu.SemaphoreType.DMA((n,))]


def _gather_phase(phase, v_refs, out_refs, send_sems, recv_sems, local_sems):
    n = len(v_refs)
    x, y, c = _mesh_pos()
    me, sibling = (x, y, c), (x, y, 1 - c)
    chips = [(1 - x, y), (x, 1 - y), (1 - x, 1 - y)]

    def block(a, px, py, pc):
        return out_refs[a].at[4 * px + 2 * py + pc]

    def copy(a, k, blk, to, src=None):
        return pltpu.make_async_remote_copy(
            src_ref=block(a, *blk) if src is None else src, dst_ref=block(a, *blk),
            send_sem=send_sems.at[a, k], recv_sem=recv_sems.at[a, k], device_id=to, device_id_type=MESH)

    def own(a):
        return pltpu.make_async_copy(v_refs[a], block(a, *me), local_sems.at[a])

    def first(a):
        return ([copy(a, 0, me, sibling, src=v_refs[a])]
                + [copy(a, 1 + j, me, (*chip, c), src=v_refs[a]) for j, chip in enumerate(chips)])

    def forward(a, j):
        return copy(a, 4 + j, (*chips[j], c), sibling)

    if phase == "start":
        for a in range(n):
            own(a).start()
        for a in range(n):
            for cp in first(a):
                cp.start()
    elif phase == "forward":
        for j in range(3):
            for a in range(n):
                copy(a, 1 + j, (*chips[j], c), me).wait_recv()
                forward(a, j).start()
    else:
        for a in range(n):
            copy(a, 0, sibling, me).wait_recv()
            for j in range(3):
                copy(a, 4 + j, (*chips[j], 1 - c), me).wait_recv()
        for a in range(n):
            for cp in first(a) + [forward(a, j) for j in range(3)]:
                cp.wait_send()
            own(a).wait()


def _all_gather(vs, name):
    n = len(vs)

    def body(*refs):
        comm = (refs[:n], refs[n:2 * n]) + tuple(refs[2 * n:])
        for phase in ("start", "forward", "finish"):
            _gather_phase(phase, *comm)

    return pl.pallas_call(
        body, name=name,
        out_shape=[jax.ShapeDtypeStruct((N_DEV,) + v.shape, v.dtype) for v in vs],
        in_specs=[HBM] * n, out_specs=[HBM] * n, scratch_shapes=_comm_scratch(n),
    )(*vs)


def _exchange(parts, name):
    n = len(parts)

    def body(*refs):
        comm = (refs[:n], refs[n:2 * n]) + tuple(refs[2 * n:])
        _exchange_phase("start", *comm)
        _exchange_phase("finish", *comm)

    return pl.pallas_call(
        body, name=name,
        out_shape=[jax.ShapeDtypeStruct(p.shape, p.dtype) for p in parts],
        in_specs=[HBM] * n, out_specs=[HBM] * n, scratch_shapes=_comm_scratch(n),
    )(*parts)


def _exchange_phase(phase, p_refs, out_refs, send_sems, recv_sems, local_sems):
    n = len(p_refs)
    x, y, c = _mesh_pos()
    me = 4 * x + 2 * y + c
    local = [pltpu.make_async_copy(p_refs[a].at[me], out_refs[a].at[me], local_sems.at[a]) for a in range(n)]
    remote = []
    for k in range(1, N_DEV):
        px = (1 - x) if k & 4 else x
        py = (1 - y) if k & 2 else y
        pc = (1 - c) if k & 1 else c
        for a in range(n):
            remote.append(pltpu.make_async_remote_copy(
                src_ref=p_refs[a].at[4 * px + 2 * py + pc], dst_ref=out_refs[a].at[me],
                send_sem=send_sems.at[a, k - 1], recv_sem=recv_sems.at[a, k - 1],
                device_id=(px, py, pc), device_id_type=MESH))
    if phase == "start":
        for cp in local + remote:
            cp.start()
    else:
        for cp in remote:
            cp.wait()
        for cp in local:
            cp.wait()


def _sum_sources(recvs, name):
    k = len(recvs)

    def body(*refs):
        for r_ref, o_ref in zip(refs[:k], refs[k:]):
            acc = r_ref[0].astype(F32)
            for s in range(1, r_ref.shape[0]):
                acc = acc + r_ref[s].astype(F32)
            o_ref[...] = acc

    return pl.pallas_call(
        body, name=name, out_shape=[jax.ShapeDtypeStruct(r.shape[1:], F32) for r in recvs],
        compiler_params=_params(),
    )(*recvs)


def _row_tile(rows):
    for tile in range(128, 15, -16):
        if rows % tile == 0:
            return tile
    return rows


def _adam_update(w, g, m, v):
    nm = ADAM_B1 * m + (1.0 - ADAM_B1) * g
    nv = ADAM_B2 * v + (1.0 - ADAM_B2) * (g * g)
    m_hat = nm / (1.0 - ADAM_B1 ** ADAM_STEP)
    v_hat = nv / (1.0 - ADAM_B2 ** ADAM_STEP)
    return -ADAM_LR * (m_hat / (jnp.sqrt(v_hat) + ADAM_EPS) + ADAM_WD * w), nm, nv


def _vector_offsets(widths):
    offsets, end = [], 0
    for c in widths:
        offsets.append(end)
        end += c + (-c) % LANES
    return offsets, end


def _adamw_small(vec_parts, vec_state, mat_grads, mat_state):
    widths = [w.shape[1] for w, _, _ in vec_state]
    offsets, total = _vector_offsets(widths)
    assert vec_parts.shape == (N_DEV, total), (vec_parts.shape, total)
    n_vec, n_mat = len(vec_state), len(mat_state)

    def body(*refs):
        r_ref = refs[0]
        vec_in = refs[1:1 + 3 * n_vec]
        mat_in = refs[1 + 3 * n_vec:1 + 3 * n_vec + 4 * n_mat]
        outs = refs[1 + 3 * n_vec + 4 * n_mat:]
        for i, (off, c) in enumerate(zip(offsets, widths)):
            g = r_ref[0:1, off:off + c]
            for s in range(1, N_DEV):
                g = g + r_ref[s:s + 1, off:off + c]
            w_ref, m_ref, v_ref = vec_in[3 * i:3 * i + 3]
            g_out, d_out, m_out, v_out = outs[4 * i:4 * i + 4]
            g_out[...] = g
            d_out[...], m_out[...], v_out[...] = _adam_update(w_ref[...], g, m_ref[...], v_ref[...])
        for j in range(n_mat):
            g_ref, w_ref, m_ref, v_ref = mat_in[4 * j:4 * j + 4]
            d_out, m_out, v_out = outs[4 * n_vec + 3 * j:4 * n_vec + 3 * j + 3]
            d_out[...], m_out[...], v_out[...] = _adam_update(w_ref[...], g_ref[...], m_ref[...], v_ref[...])

    args = [vec_parts] + [a for state in vec_state for a in state]
    for g, state in zip(mat_grads, mat_state):
        args += [g, *state]
    out_shape = [jax.ShapeDtypeStruct(w.shape, F32) for w, _, _ in vec_state for _ in range(4)]
    out_shape += [jax.ShapeDtypeStruct(w.shape, F32) for w, _, _ in mat_state for _ in range(3)]
    out = pl.pallas_call(body, name="adamw_replicated", out_shape=out_shape, compiler_params=_params())(*args)
    vec_out = [tuple(out[4 * i:4 * i + 4]) for i in range(n_vec)]
    mat_out = [tuple(out[4 * n_vec + 3 * j:4 * n_vec + 3 * j + 3]) for j in range(n_mat)]
    return vec_out, mat_out


def _adamw(w, recv, m, v, name):
    rows, width = w.shape
    n = recv.shape[0]
    if rows % 16 == 0 or width % 256:
        tr, tc = _row_tile(rows), width
    else:
        tr, tc = rows, 256

    def body(w_ref, r_ref, m_ref, v_ref, g_ref, d_ref, nm_ref, nv_ref):
        gv = r_ref[0].astype(F32)
        for s in range(1, n):
            gv = gv + r_ref[s].astype(F32)
        g_ref[...] = gv
        d_ref[...], nm_ref[...], nv_ref[...] = _adam_update(w_ref[...], gv, m_ref[...], v_ref[...])

    spec = pl.BlockSpec((tr, tc), lambda i, j: (i, j))
    shape = jax.ShapeDtypeStruct((rows, width), F32)
    return pl.pallas_call(
        body, name=name, grid=(rows // tr, width // tc),
        in_specs=[spec, pl.BlockSpec((n, tr, tc), lambda i, j: (0, i, j)), spec, spec],
        out_specs=[spec] * 4, out_shape=[shape] * 4,
        compiler_params=_params(("parallel", "parallel")),
    )(w, recv, m, v)


BIG_NAMES = ("w_in", "w_out_ssm", "w_out_lru", "w_out", "w_ffn_in", "w_ffn_out", "ssm_conv_w", "lru_conv_w")
TRANSPOSED = ("w_in", "w_ffn_in")
CONV_NAMES = ("ssm_conv_w", "lru_conv_w")
MATMUL_NAMES = BIG_NAMES[:6]
NEEDED_FIRST = ("w_in", "ssm_conv_w", "lru_conv_w")
GATHERED_IN_SSD = ("w_ffn_in",)
GATHERED_IN_LRU = ("w_ffn_out", "w_out_ssm", "w_out_lru", "w_out")
EXCHANGED_IN_SSD = ("w_ffn_in", "w_ffn_out")
EXCHANGED_IN_LRU = ("w_out_ssm", "w_out_lru", "w_out")
SMALL_VECTORS = ("norm1_w", "b_branch_gate", "ssm_conv_b", "ssm_dt_bias", "ssm_a_log", "ssm_d", "ssm_norm_w",
                 "lru_conv_b", "lru_b_r", "lru_b_i", "lru_lambda", "norm2_w", "norm_f_w")
SMALL_MATRICES = ("lru_w_r", "lru_w_i")


def _col_shards(full):
    rows, cols = full.shape
    return full.reshape(rows, N_DEV, cols // N_DEV).transpose(1, 0, 2)


def _from_col_shards(g):
    n, rows, w = g.shape
    return g.transpose(1, 0, 2).reshape(rows, n * w)


def kernel(x, norm1_w, w_in, b_branch_gate, ssm_conv_w, ssm_conv_b, ssm_dt_bias, ssm_a_log, ssm_d, ssm_norm_w, w_out_ssm, lru_conv_w, lru_conv_b, lru_w_r, lru_b_r, lru_w_i, lru_b_i, lru_lambda, w_out_lru, w_out, norm2_w, w_ffn_in, w_ffn_out, norm_f_w, loss_target, m_norm1_w, m_w_in, m_b_branch_gate, m_ssm_conv_w, m_ssm_conv_b, m_ssm_dt_bias, m_ssm_a_log, m_ssm_d, m_ssm_norm_w, m_w_out_ssm, m_lru_conv_w, m_lru_conv_b, m_lru_w_r, m_lru_b_r, m_lru_w_i, m_lru_b_i, m_lru_lambda, m_w_out_lru, m_w_out, m_norm2_w, m_w_ffn_in, m_w_ffn_out, m_norm_f_w, v_norm1_w, v_w_in, v_b_branch_gate, v_ssm_conv_w, v_ssm_conv_b, v_ssm_dt_bias, v_ssm_a_log, v_ssm_d, v_ssm_norm_w, v_w_out_ssm, v_lru_conv_w, v_lru_conv_b, v_lru_w_r, v_lru_b_r, v_lru_w_i, v_lru_b_i, v_lru_lambda, v_w_out_lru, v_w_out, v_norm2_w, v_w_ffn_in, v_w_ffn_out, v_norm_f_w):
    given = dict(locals())
    weights = {n: given[n] for n in BIG_NAMES + SMALL_VECTORS + SMALL_MATRICES}
    t = x.shape[1]
    xt = x[0]
    tgt = loss_target[0]

    def local(n, a):
        return a[0].T if n in TRANSPOSED else a[0]

    def as_output(n, a):
        return a.T[None] if n in TRANSPOSED else a[None]

    def shard(n):
        s = local(n, weights[n])
        return s.astype(BF16) if n in MATMUL_NAMES else s

    def unshard(n, g):
        return _from_col_shards(g) if n in CONV_NAMES else g.reshape(-1, g.shape[-1])

    def grad_slices(n):
        g = grads[n]
        return (_col_shards(g) if n in CONV_NAMES else g.reshape(N_DEV, -1, g.shape[-1])).astype(BF16)

    gathered = _all_gather([shard(n) for n in NEEDED_FIRST], "gather_in_weights")
    full = {n: unshard(n, g) for n, g in zip(NEEDED_FIRST, gathered)}
    ssm_cw, lru_cw = full["ssm_conv_w"], full["lru_conv_w"]
    wi_t = full["w_in"]
    lru_rows = [wi_t[o + j * LANES:o + (j + 1) * LANES] for j in range(LRU_BLOCKS) for o in (ORIG_LRU_X, ORIG_LRU_Y)]
    w_pt = jnp.concatenate([wi_t[:ORIG_DT]] + lru_rows + [wi_t[ORIG_DT:ORIG_LRU_X],
                                                          jnp.zeros((PROJ_W - IN_PROJ, D_MODEL), BF16)], axis=0)

    def pad_heads(a):
        return jnp.pad(a.reshape(1, SSM_HEADS), ((0, 0), (0, LANES - SSM_HEADS)))

    dt_bias_p = pad_heads(ssm_dt_bias)
    a_log_p = pad_heads(ssm_a_log)
    d_exp = jnp.repeat(ssm_d.reshape(SSM_HEADS), SSM_HEAD_DIM).reshape(1, SSM_INNER)
    lru_wr, lru_wi = lru_w_r[0], lru_w_i[0]

    hn1 = _rmsnorm_fwd(xt, norm1_w, "norm1_fwd")
    proj = _mm(hn1, w_pt, tb=True, name="in_proj")
    xbc_act = _ssm_conv_fwd(proj, ssm_cw, ssm_conv_b)
    (y_ssd, s_in_all), gathered = _ssd_fwd(xbc_act, proj, dt_bias_p, a_log_p, d_exp,
                                           gather=[shard(n) for n in GATHERED_IN_SSD])
    full.update({n: unshard(n, g) for n, g in zip(GATHERED_IN_SSD, gathered)})
    (l_out, h_lru), gathered = _lru_fwd(proj, lru_cw, lru_conv_b, lru_wr, lru_b_r, lru_wi, lru_b_i, lru_lambda,
                                        gather=[shard(n) for n in GATHERED_IN_LRU])
    full.update({n: unshard(n, g) for n, g in zip(GATHERED_IN_LRU, gathered)})
    y_pre = _ssm_post_fwd(y_ssd, proj, ssm_norm_w)
    y_ssm = _mm(y_pre, full["w_out_ssm"], name="out_ssm")
    y_lru = _mm(l_out, full["w_out_lru"], name="out_lru")
    merged = _merge_fwd(proj, b_branch_gate, y_ssm, y_lru)
    h1 = _mm(merged, full["w_out"], add=xt, name="out_proj")
    hn2 = _rmsnorm_fwd(h1, norm2_w, "norm2_fwd")
    gu = _mm(hn2, full["w_ffn_in"], tb=True, name="ffn_in")
    act = _swiglu_fwd(gu)
    h2 = _mm(act, full["w_ffn_out"], add=h1, name="ffn_out")

    grads = {}
    dh2, grads["norm_f_w"], loss_cols = _loss_head(h2, norm_f_w.reshape(1, D_MODEL), tgt)
    loss = lax.psum(0.5 * jnp.sum(loss_cols) / D_MODEL, AXES)
    dact = _mm(dh2, full["w_ffn_out"], tb=True, name="d_act")
    grads["w_ffn_out"] = _mm(act, dh2, ta=True, out_dtype=BF16, name="dw_ffn_out")
    dgu = _swiglu_bwd(gu, dact)
    dhn2 = _mm(dgu, full["w_ffn_in"], name="d_hn2")
    grads["w_ffn_in"] = _mm(dgu, hn2, ta=True, out_dtype=BF16, name="dw_ffn_in")
    dh1, grads["norm2_w"] = _rmsnorm_bwd(h1, norm2_w, dhn2, dh2, "norm2_bwd")
    dmerged = _mm(dh1, full["w_out"], tb=True, name="d_merged")
    grads["w_out"] = _mm(merged, dh1, ta=True, out_dtype=BF16, name="dw_out")
    dy_ssm, dy_lru, dproj, grads["b_branch_gate"] = _merge_bwd(proj, b_branch_gate, y_ssm, y_lru, dmerged)
    dy_pre = _mm(dy_ssm, full["w_out_ssm"], tb=True, name="d_y_pre")
    grads["w_out_ssm"] = _mm(y_pre, dy_ssm, ta=True, out_dtype=BF16, name="dw_out_ssm")
    dl_out = _mm(dy_lru, full["w_out_lru"], tb=True, name="d_l_out")
    grads["w_out_lru"] = _mm(l_out, dy_lru, ta=True, out_dtype=BF16, name="dw_out_lru")
    dy_ssd, dproj, grads["ssm_norm_w"] = _ssm_post_bwd(y_ssd, proj, ssm_norm_w, dy_pre, dproj)
    (dxbc_act, dproj, dbias, dalog, ddcol), recv_in_ssd = _ssd_bwd(
        xbc_act, proj, s_in_all, dy_ssd, dt_bias_p, a_log_p, d_exp, dproj,
        exchange=[grad_slices(n) for n in EXCHANGED_IN_SSD])
    grads["ssm_dt_bias"] = dbias[:, :SSM_HEADS]
    grads["ssm_a_log"] = dalog[:, :SSM_HEADS]
    grads["ssm_d"] = ddcol.reshape(SSM_HEADS, SSM_HEAD_DIM).sum(axis=1).reshape(1, SSM_HEADS)
    dproj, grads["ssm_conv_w"], grads["ssm_conv_b"] = _ssm_conv_bwd(proj, ssm_cw, ssm_conv_b, dxbc_act, dproj)
    ((dproj, grads["lru_conv_w"], grads["lru_conv_b"], dwr, grads["lru_b_r"], dwi, grads["lru_b_i"],
      grads["lru_lambda"]), recv_in_lru) = _lru_bwd(
        proj, lru_cw, lru_conv_b, lru_wr, lru_b_r, lru_wi, lru_b_i, lru_lambda, h_lru, dl_out, dproj,
        exchange=[grad_slices(n) for n in EXCHANGED_IN_LRU])
    grads["lru_w_r"], grads["lru_w_i"] = dwr[None], dwi[None]
    dwpt = _mm(dproj, hn1, ta=True, out_dtype=BF16, name="dw_in")
    lru_rows = [dwpt[COL_LRU + (2 * j + k) * LANES:COL_LRU + (2 * j + k + 1) * LANES]
                for k in range(2) for j in range(LRU_BLOCKS)]
    grads["w_in"] = jnp.concatenate([dwpt[:ORIG_DT], dwpt[COL_DT:COL_DT + SSM_HEADS]] + lru_rows, axis=0)
    mat_parts = [grads[n].reshape(N_DEV, -1, LANES) for n in SMALL_MATRICES]
    dhn1, received = _mm(dproj, w_pt, exchange=[grad_slices(n) for n in NEEDED_FIRST] + mat_parts, name="d_hn1")
    recv_first, recv_mats = received[:len(NEEDED_FIRST)], received[len(NEEDED_FIRST):]
    grad_x, grads["norm1_w"] = _rmsnorm_bwd(xt, norm1_w, dhn1, dh1, "norm1_bwd")

    def as_rows(n, a):
        return a.reshape(1, -1) if n in SMALL_VECTORS else a.reshape(-1, LANES)

    vec_g = jnp.concatenate([jnp.pad(as_rows(n, grads[n]), ((0, 0), (0, (-grads[n].size) % LANES)))
                             for n in SMALL_VECTORS], axis=1)
    gathered = _all_gather([vec_g] + list(_sum_sources(recv_mats, "sum_gate_matrix_grads")), "gather_small_grads")
    vec_parts = gathered[0].reshape(N_DEV, -1)
    mat_g = [g.reshape(-1, LANES) for g in gathered[1:]]

    recv = dict(zip(EXCHANGED_IN_SSD + EXCHANGED_IN_LRU + NEEDED_FIRST,
                    list(recv_in_ssd) + list(recv_in_lru) + list(recv_first)))
    big_out = {n: _adamw(local(n, weights[n]), recv[n], local(n, given["m_" + n]), local(n, given["v_" + n]),
                         "adamw_" + n) for n in BIG_NAMES}

    def state(n):
        return tuple(as_rows(n, given[p + n]) for p in ("", "m_", "v_"))

    vec_out, mat_out = _adamw_small(vec_parts, [state(n) for n in SMALL_VECTORS],
                                    mat_g, [state(n) for n in SMALL_MATRICES])
    small_out = dict(zip(SMALL_VECTORS, vec_out))
    small_out.update({n: (g,) + out for n, g, out in zip(SMALL_MATRICES, mat_g, mat_out)})

    order = list(given)[1:24]
    results = []
    for q in range(4):
        vals = {n: as_output(n, big_out[n][q]) for n in BIG_NAMES}
        vals.update({n: out[q].reshape(weights[n].shape) for n, out in small_out.items()})
        results.extend(vals[n] for n in order)
    return (loss, grad_x[None], *results)
```

```python
import math

import jax
import jax.numpy as jnp
from jax import lax
from jax.experimental import pallas as pl
from jax.experimental.pallas import tpu as pltpu

F32 = jnp.float32
BF16 = jnp.bfloat16
HIGHEST = lax.Precision.HIGHEST
MESH = pl.DeviceIdType.MESH
AXES = ("x", "y", "c")
N_DEV = 8

D_MODEL = 1024
SSM_INNER = 2048
SSM_HEADS = 32
SSM_HEAD_DIM = 64
SSM_GROUPS = 4
SSM_STATE = 128
SSM_BC = SSM_GROUPS * SSM_STATE
SSM_CONV_DIM = SSM_INNER + 2 * SSM_BC
SSM_CHUNK = 128
SSM_PAIRS = SSM_HEADS // 2
CONV_K = 4
LRU_WIDTH = 1280
LRU_BLOCKS = 10
LRU_C = 8.0
FFN_HIDDEN = 2816
RMS_EPS = 1e-6
IN_PROJ = 9760

COL_GATES = 0
COL_Z = 2048
COL_XBC = 4096
COL_LRU = 7168
COL_DT = 9728
PROJ_W = 9856
ORIG_DT = 7168
ORIG_LRU_X = 7200
ORIG_LRU_Y = 8480

ADAM_LR = 0.001
ADAM_B1 = 0.9
ADAM_B2 = 0.999
ADAM_EPS = 1e-08
ADAM_WD = 0.01
ADAM_STEP = 10

LANES = 128
SUBLANES = 8
V7X_VMEM_BYTES = 64 * 1024 * 1024
VMEM_LIMIT = V7X_VMEM_BYTES * 3 // 4
VMEM_LIMIT_BIG = V7X_VMEM_BYTES * 15 // 16

NT = (((1,), (1,)), ((), ()))
TN = (((0,), (0,)), ((), ()))


def _params(sem=None, big=False):
    return pltpu.CompilerParams(dimension_semantics=sem,
                                vmem_limit_bytes=VMEM_LIMIT_BIG if big else VMEM_LIMIT)


def _blk(dim, cap):
    if dim <= cap:
        return dim
    for m in range(cap // LANES, 0, -1):
        if dim % (m * LANES) == 0:
            return m * LANES
    raise ValueError(f"no block for {dim}")


def _rows(t):
    return min(t, 256)


def _sigmoid(v):
    return 1.0 / (1.0 + jnp.exp(-v))


def _softplus(v):
    e = jnp.exp(-jnp.abs(v))
    u = 1.0 + e
    log1p = jnp.where(u == 1.0, e, jnp.log(u) * e / jnp.where(u == 1.0, 1.0, u - 1.0))
    return jnp.maximum(v, 0.0) + log1p


def _iota(shape, dim):
    return lax.broadcasted_iota(jnp.int32, shape, dim)


def _shift_down(v, s):
    if s == 0:
        return v
    return jnp.where(_iota(v.shape, 0) >= s, pltpu.roll(v, s, 0), 0.0)


def _shift_up(v, s):
    if s == 0:
        return v
    n = v.shape[0]
    return jnp.where(_iota(v.shape, 0) < n - s, pltpu.roll(v, n - s, 0), 0.0)


def _bdot(a, b, dn=None):
    a = a.astype(BF16)
    b = b.astype(BF16)
    if dn is None:
        return jnp.dot(a, b, preferred_element_type=F32)
    return lax.dot_general(a, b, dn, preferred_element_type=F32)


def _split_dot(a, e, dn=None):
    hi = a.astype(BF16)
    lo = (a - hi.astype(F32)).astype(BF16)
    return _bdot(hi, e, dn) + _bdot(lo, e, dn)


def _fdot(a, b, dn=None):
    if dn is None:
        return jnp.dot(a, b, precision=HIGHEST, preferred_element_type=F32)
    return lax.dot_general(a, b, dn, precision=HIGHEST, preferred_element_type=F32)


def _mm(a, b, *, ta=False, tb=False, add=None, hosted=(), out_dtype=F32, name):
    if ta:
        kdim, m = a.shape
    else:
        m, kdim = a.shape
    if tb:
        n, k2 = b.shape
    else:
        k2, n = b.shape
    assert kdim == k2, (a.shape, b.shape, ta, tb)
    bm, bn, bk = _blk(m, 1408), _blk(n, 1408), _blk(kdim, 1408)
    grid = (m // bm, n // bn, kdim // bk)
    nk = grid[2]
    dn = (((0 if ta else 1,), (1 if tb else 0,)), ((), ()))
    n_in = 2 if add is None else 3
    sizes = [len(arrays) for _, arrays, _ in hosted]
    n_ex = sum(sizes)

    def body(*refs):
        a_ref, b_ref = refs[:2]
        r_ref = None if add is None else refs[2]
        o_ref = refs[n_in + n_ex]
        acc = refs[n_in + 2 * n_ex + 1]
        comms, at = [], 0
        for g, size in enumerate(sizes):
            sems = refs[n_in + 2 * n_ex + 2 + 3 * g:n_in + 2 * n_ex + 5 + 3 * g]
            comms.append((refs[n_in + at:n_in + at + size],
                          refs[n_in + n_ex + 1 + at:n_in + n_ex + 1 + at + size]) + tuple(sems))
            at += size
        step = (pl.program_id(0) * grid[1] + pl.program_id(1)) * nk + pl.program_id(2)
        k = pl.program_id(2)
        if n_ex:
            @pl.when(step == 0)
            def _():
                for (phase_fn, _, _), comm in zip(hosted, comms):
                    phase_fn("start", *comm)

        @pl.when(k == 0)
        def _():
            acc[...] = jnp.zeros_like(acc)

        acc[...] += lax.dot_general(a_ref[...].astype(BF16), b_ref[...].astype(BF16), dn,
                                    preferred_element_type=F32)

        @pl.when(k == nk - 1)
        def _():
            r = acc[...]
            if add is not None:
                r = r + r_ref[...]
            o_ref[...] = r.astype(out_dtype)

        if n_ex:
            @pl.when(step == grid[0] * grid[1] * nk - 1)
            def _():
                for (phase_fn, _, _), comm in zip(hosted, comms):
                    phase_fn("finish", *comm)

    a_spec = pl.BlockSpec((bk, bm), lambda i, j, k: (k, i)) if ta else pl.BlockSpec((bm, bk), lambda i, j, k: (i, k))
    b_spec = pl.BlockSpec((bn, bk), lambda i, j, k: (j, k)) if tb else pl.BlockSpec((bk, bn), lambda i, j, k: (k, j))
    o_spec = pl.BlockSpec((bm, bn), lambda i, j, k: (i, j))
    in_specs = [a_spec, b_spec] + ([o_spec] if add is not None else []) + [HBM] * n_ex
    args = (a, b) + ((add,) if add is not None else ()) + tuple(p for _, arrays, _ in hosted for p in arrays)
    out = pl.pallas_call(
        body, name=name, grid=grid,
        in_specs=in_specs, out_specs=[o_spec] + [HBM] * n_ex,
        out_shape=[jax.ShapeDtypeStruct((m, n), out_dtype)]
        + [jax.ShapeDtypeStruct(r.shape, r.dtype) for _, _, results in hosted for r in results],
        scratch_shapes=[pltpu.VMEM((bm, bn), F32)] + [s for size in sizes for s in _comm_scratch(size)],
        compiler_params=_params(("arbitrary",) * 3 if n_ex else ("parallel", "parallel", "arbitrary")),
    )(*args)
    if not n_ex:
        return out[0]
    received, at = [], 1
    for size in sizes:
        received.append(out[at:at + size])
        at += size
    return out[0], received


def _rmsnorm_fwd(x, w, name):
    t, d = x.shape
    tr = _rows(t)

    def body(x_ref, w_ref, o_ref):
        xv = x_ref[...]
        rstd = lax.rsqrt(jnp.mean(xv * xv, axis=-1, keepdims=True) + RMS_EPS)
        o_ref[...] = (xv * rstd * w_ref[...]).astype(BF16)

    return pl.pallas_call(
        body, name=name, grid=(t // tr,),
        in_specs=[pl.BlockSpec((tr, d), lambda i: (i, 0)), pl.BlockSpec((1, d), lambda i: (0, 0))],
        out_specs=pl.BlockSpec((tr, d), lambda i: (i, 0)),
        out_shape=jax.ShapeDtypeStruct((t, d), BF16),
        compiler_params=_params(("parallel",)),
    )(x, w)


def _rmsnorm_bwd(x, w, dy, dres, name):
    t, d = x.shape
    tr = _rows(t)

    def body(x_ref, w_ref, dy_ref, dres_ref, dx_ref, dw_ref):
        i = pl.program_id(0)
        xv = x_ref[...]
        rstd = lax.rsqrt(jnp.mean(xv * xv, axis=-1, keepdims=True) + RMS_EPS)
        xhat = xv * rstd
        dyv = dy_ref[...]
        dxhat = dyv * w_ref[...]
        m = jnp.mean(dxhat * xhat, axis=-1, keepdims=True)
        dx_ref[...] = rstd * (dxhat - xhat * m) + dres_ref[...]
        part = jnp.sum(dyv * xhat, axis=0, keepdims=True)

        @pl.when(i == 0)
        def _():
            dw_ref[...] = part

        @pl.when(i > 0)
        def _():
            dw_ref[...] += part

    row = pl.BlockSpec((tr, d), lambda i: (i, 0))
    vec = pl.BlockSpec((1, d), lambda i: (0, 0))
    return pl.pallas_call(
        body, name=name, grid=(t // tr,),
        in_specs=[row, vec, row, row], out_specs=[row, vec],
        out_shape=[jax.ShapeDtypeStruct((t, d), F32), jax.ShapeDtypeStruct((1, d), F32)],
        compiler_params=_params(("arbitrary",)),
    )(x, w, dy, dres)


def _loss_head(h2, w, tgt):
    t, d = h2.shape
    tr = _rows(t)

    def body(x_ref, w_ref, t_ref, dx_ref, dw_ref, ls_ref):
        i = pl.program_id(0)
        xv = x_ref[...]
        wv = w_ref[...]
        rstd = lax.rsqrt(jnp.mean(xv * xv, axis=-1, keepdims=True) + RMS_EPS)
        xhat = xv * rstd
        err = xhat * wv - t_ref[...]
        dyv = err * (1.0 / d)
        dxhat = dyv * wv
        m = jnp.mean(dxhat * xhat, axis=-1, keepdims=True)
        dx_ref[...] = rstd * (dxhat - xhat * m)
        dw_part = jnp.sum(dyv * xhat, axis=0, keepdims=True)
        ls_part = jnp.sum(err * err, axis=0, keepdims=True)

        @pl.when(i == 0)
        def _():
            dw_ref[...] = dw_part
            ls_ref[...] = ls_part

        @pl.when(i > 0)
        def _():
            dw_ref[...] += dw_part
            ls_ref[...] += ls_part

    row = pl.BlockSpec((tr, d), lambda i: (i, 0))
    vec = pl.BlockSpec((1, d), lambda i: (0, 0))
    return pl.pallas_call(
        body, name="loss_head", grid=(t // tr,),
        in_specs=[row, vec, row], out_specs=[row, vec, vec],
        out_shape=[jax.ShapeDtypeStruct((t, d), F32), jax.ShapeDtypeStruct((1, d), F32),
                   jax.ShapeDtypeStruct((1, d), F32)],
        compiler_params=_params(("arbitrary",)),
    )(h2, w, tgt)


def _merge_fwd(proj, bg, ys, yl):
    t = proj.shape[0]
    d = D_MODEL
    tr = _rows(t)

    def body(ps_ref, pl_ref, bg_ref, ys_ref, yl_ref, o_ref):
        gs = _sigmoid(ps_ref[...] + bg_ref[:, 0:d])
        gl = _sigmoid(pl_ref[...] + bg_ref[:, d:2 * d])
        o_ref[...] = (gs * ys_ref[...] + gl * yl_ref[...]).astype(BF16)

    row = pl.BlockSpec((tr, d), lambda i: (i, 0))
    return pl.pallas_call(
        body, name="merge_fwd", grid=(t // tr,),
        in_specs=[row, pl.BlockSpec((tr, d), lambda i: (i, 1)), pl.BlockSpec((1, 2 * d), lambda i: (0, 0)), row, row],
        out_specs=row, out_shape=jax.ShapeDtypeStruct((t, d), BF16),
        compiler_params=_params(("parallel",)),
    )(proj, proj, bg, ys, yl)


def _merge_bwd(proj, bg, ys, yl, dm):
    t = proj.shape[0]
    d = D_MODEL
    tr = _rows(t)

    def body(ps_ref, pl_ref, bg_ref, ys_ref, yl_ref, dm_ref, dys_ref, dyl_ref, dg_ref, dbg_ref):
        i = pl.program_id(0)
        gs = _sigmoid(ps_ref[...] + bg_ref[:, 0:d])
        gl = _sigmoid(pl_ref[...] + bg_ref[:, d:2 * d])
        dmv = dm_ref[...]
        dys_ref[...] = (dmv * gs).astype(BF16)
        dyl_ref[...] = (dmv * gl).astype(BF16)
        dgs = dmv * ys_ref[...] * gs * (1.0 - gs)
        dgl = dmv * yl_ref[...] * gl * (1.0 - gl)
        dg_ref[:, 0:d] = dgs.astype(BF16)
        dg_ref[:, d:2 * d] = dgl.astype(BF16)

        @pl.when(i == 0)
        def _():
            dbg_ref[...] = jnp.zeros_like(dbg_ref)

        dbg_ref[:, 0:d] += jnp.sum(dgs, axis=0, keepdims=True)
        dbg_ref[:, d:2 * d] += jnp.sum(dgl, axis=0, keepdims=True)

    row = pl.BlockSpec((tr, d), lambda i: (i, 0))
    wide = pl.BlockSpec((tr, 2 * d), lambda i: (i, 0))
    vec = pl.BlockSpec((1, 2 * d), lambda i: (0, 0))
    return pl.pallas_call(
        body, name="merge_bwd", grid=(t // tr,),
        in_specs=[row, pl.BlockSpec((tr, d), lambda i: (i, 1)), vec, row, row, row],
        out_specs=[row, row, wide, vec],
        out_shape=[jax.ShapeDtypeStruct((t, d), BF16), jax.ShapeDtypeStruct((t, d), BF16),
                   jax.ShapeDtypeStruct((t, PROJ_W), BF16), jax.ShapeDtypeStruct((1, 2 * d), F32)],
        compiler_params=_params(("arbitrary",)),
    )(proj, proj, bg, ys, yl, dm)


def _swiglu_fwd(gu):
    t = gu.shape[0]
    f = FFN_HIDDEN
    tr = _rows(t)

    def body(g_ref, u_ref, o_ref):
        g = g_ref[...]
        o_ref[...] = (g * _sigmoid(g) * u_ref[...]).astype(BF16)

    return pl.pallas_call(
        body, name="swiglu_fwd", grid=(t // tr,),
        in_specs=[pl.BlockSpec((tr, f), lambda i: (i, 0)), pl.BlockSpec((tr, f), lambda i: (i, 1))],
        out_specs=pl.BlockSpec((tr, f), lambda i: (i, 0)),
        out_shape=jax.ShapeDtypeStruct((t, f), BF16),
        compiler_params=_params(("parallel",)),
    )(gu, gu)


def _swiglu_bwd(gu, dact):
    t = gu.shape[0]
    f = FFN_HIDDEN
    tr = _rows(t)

    def body(g_ref, u_ref, da_ref, o_ref):
        g = g_ref[...]
        sg = _sigmoid(g)
        da = da_ref[...]
        o_ref[:, 0:f] = (da * u_ref[...] * (sg * (1.0 + g * (1.0 - sg)))).astype(BF16)
        o_ref[:, f:2 * f] = (da * g * sg).astype(BF16)

    return pl.pallas_call(
        body, name="swiglu_bwd", grid=(t // tr,),
        in_specs=[pl.BlockSpec((tr, f), lambda i: (i, 0)), pl.BlockSpec((tr, f), lambda i: (i, 1)),
                  pl.BlockSpec((tr, f), lambda i: (i, 0))],
        out_specs=pl.BlockSpec((tr, 2 * f), lambda i: (i, 0)),
        out_shape=jax.ShapeDtypeStruct((t, 2 * f), BF16),
        compiler_params=_params(("parallel",)),
    )(gu, gu, dact)


def _conv_pre(xv, wv, bv):
    pre = bv + wv[CONV_K - 1:CONV_K, :] * xv
    for k in range(CONV_K - 1):
        pre = pre + wv[k:k + 1, :] * _shift_down(xv, CONV_K - 1 - k)
    return pre


def _ssm_conv_fwd(proj, w, b):
    t = proj.shape[0]
    nb = SSM_CONV_DIM // LANES
    c0 = COL_XBC // LANES

    def body(x_ref, w_ref, b_ref, o_ref):
        pre = _conv_pre(x_ref[...], w_ref[...], b_ref[...])
        o_ref[...] = pre * _sigmoid(pre)

    return pl.pallas_call(
        body, name="ssm_conv_fwd", grid=(nb,),
        in_specs=[pl.BlockSpec((t, LANES), lambda j: (0, c0 + j)), pl.BlockSpec((CONV_K, LANES), lambda j: (0, j)),
                  pl.BlockSpec((1, LANES), lambda j: (0, j))],
        out_specs=pl.BlockSpec((t, LANES), lambda j: (0, j)),
        out_shape=jax.ShapeDtypeStruct((t, SSM_CONV_DIM), F32),
        compiler_params=_params(("parallel",)),
    )(proj, w, b)


def _ssm_conv_bwd(proj, w, b, dact, dproj):
    t = proj.shape[0]
    nb = SSM_CONV_DIM // LANES
    c0 = COL_XBC // LANES

    def body(x_ref, w_ref, b_ref, da_ref, dproj_in, dx_ref, dw_ref, db_ref):
        xv = x_ref[...]
        wv = w_ref[...]
        pre = _conv_pre(xv, wv, b_ref[...])
        sg = _sigmoid(pre)
        dpre = da_ref[...] * (sg * (1.0 + pre * (1.0 - sg)))
        dx = wv[CONV_K - 1:CONV_K, :] * dpre
        for k in range(CONV_K - 1):
            dx = dx + wv[k:k + 1, :] * _shift_up(dpre, CONV_K - 1 - k)
        dx_ref[...] = dx.astype(BF16)
        for k in range(CONV_K):
            dw_ref[k:k + 1, :] = jnp.sum(dpre * _shift_down(xv, CONV_K - 1 - k), axis=0, keepdims=True)
        db_ref[...] = jnp.sum(dpre, axis=0, keepdims=True)

    col = pl.BlockSpec((t, LANES), lambda j: (0, j))
    wsp = pl.BlockSpec((CONV_K, LANES), lambda j: (0, j))
    bsp = pl.BlockSpec((1, LANES), lambda j: (0, j))
    return pl.pallas_call(
        body, name="ssm_conv_bwd", grid=(nb,),
        in_specs=[pl.BlockSpec((t, LANES), lambda j: (0, c0 + j)), wsp, bsp, col, HBM],
        out_specs=[pl.BlockSpec((t, LANES), lambda j: (0, c0 + j)), wsp, bsp],
        out_shape=[jax.ShapeDtypeStruct(dproj.shape, dproj.dtype), jax.ShapeDtypeStruct((CONV_K, SSM_CONV_DIM), F32),
                   jax.ShapeDtypeStruct((1, SSM_CONV_DIM), F32)],
        input_output_aliases={4: 0},
        compiler_params=_params(("parallel",)),
    )(proj, w, b, dact, dproj)


def _ssd_chunk_terms(dtr, bias, alog):
    a = -jnp.exp(alog)
    dt = _softplus(dtr + bias)
    row = _iota((SSM_CHUNK, SSM_CHUNK), 0)
    col = _iota((SSM_CHUNK, SSM_CHUNK), 1)
    tri = (row >= col).astype(F32)
    cs = _fdot(tri, dt * a)
    dec = jnp.exp(cs[SSM_CHUNK - 1:SSM_CHUNK, :] - cs)
    ecs = jnp.exp(cs)
    off = _iota((LANES, SSM_INNER), 1) - SSM_HEAD_DIM * _iota((LANES, SSM_INNER), 0)
    expand = jnp.where(jnp.logical_and(off >= 0, off < SSM_HEAD_DIM), 1.0, 0.0).astype(BF16)
    return a, dt, cs, dec, ecs, expand, row, col


def _ssd_specs(t):
    nc = t // SSM_CHUNK
    xs = pl.BlockSpec((SSM_CHUNK, SSM_INNER), lambda c: (c, 0))
    bm = pl.BlockSpec((SSM_CHUNK, SSM_BC), lambda c: (c, SSM_INNER // SSM_BC))
    cm = pl.BlockSpec((SSM_CHUNK, SSM_BC), lambda c: (c, SSM_INNER // SSM_BC + 1))
    dtr = pl.BlockSpec((SSM_CHUNK, LANES), lambda c: (c, COL_DT // LANES))
    vec = pl.BlockSpec((1, LANES), lambda c: (0, 0))
    wide = pl.BlockSpec((1, SSM_INNER), lambda c: (0, 0))
    return nc, xs, bm, cm, dtr, vec, wide


def _ssd_fwd(xbc_act, proj, bias, alog, dexp, gather):
    t = proj.shape[0]
    nc, xs_s, bm_s, cm_s, dtr_s, vec, wide = _ssd_specs(t)
    n = len(gather)

    def body(*refs):
        xs_ref, b_ref, c_ref, dtr_ref, bias_ref, alog_ref, dexp_ref = refs[:7]
        y_ref, sin_ref = refs[7 + n:9 + n]
        state = refs[9 + 2 * n]
        comm = (refs[7:7 + n], refs[9 + n:9 + 2 * n]) + tuple(refs[10 + 2 * n:])
        chunk = pl.program_id(0)

        @pl.when(chunk == 0)
        def _():
            _gather_phase("start", *comm)
            state[...] = jnp.zeros_like(state)

        @pl.when(chunk == nc // 2)
        def _():
            _gather_phase("forward", *comm)

        a, dt, cs, dec, ecs, expand, row, col = _ssd_chunk_terms(dtr_ref[...], bias_ref[...], alog_ref[...])
        cst = cs.T
        dt_x = _split_dot(dt, expand)
        dec_x = _split_dot(dec, expand)
        ecs_x = _split_dot(ecs, expand)
        xs = xs_ref[...]
        xdt = xs * dt_x
        xdec = xdt * dec_x
        lane_lo = col < SSM_HEAD_DIM
        causal = row >= col
        sin_ref[0] = state[...]
        for g in range(SSM_GROUPS):
            bg = b_ref[:, g * SSM_STATE:(g + 1) * SSM_STATE].astype(BF16)
            cg = c_ref[:, g * SSM_STATE:(g + 1) * SSM_STATE].astype(BF16)
            cb = _bdot(cg, bg, NT)
            for q in range(SSM_PAIRS // SSM_GROUPS):
                pq = g * (SSM_PAIRS // SSM_GROUPS) + q
                sl = slice(pq * LANES, (pq + 1) * LANES)
                xp = xdt[:, sl].astype(BF16)
                yd = []
                for hh in range(2):
                    h = 2 * pq + hh
                    lmat = jnp.exp(jnp.where(causal, cs[:, h:h + 1] - cst[h:h + 1, :], -jnp.inf))
                    yd.append(_bdot(cb * lmat, xp))
                s_in = state[pq]
                y_off = _bdot(cg, s_in) * ecs_x[:, sl]
                y_ref[:, sl] = jnp.where(lane_lo, yd[0], yd[1]) + y_off + xs[:, sl] * dexp_ref[:, sl]
                state[pq] = s_in * ecs_x[SSM_CHUNK - 1:SSM_CHUNK, sl] + _bdot(bg, xdec[:, sl], TN)

        @pl.when(chunk == nc - 1)
        def _():
            _gather_phase("finish", *comm)

    out = pl.pallas_call(
        body, name="ssd_fwd", grid=(nc,),
        in_specs=[xs_s, bm_s, cm_s, dtr_s, vec, vec, wide] + [HBM] * n,
        out_specs=[pl.BlockSpec((SSM_CHUNK, SSM_INNER), lambda c: (c, 0)),
                   pl.BlockSpec((1, SSM_PAIRS, SSM_STATE, LANES), lambda c: (c, 0, 0, 0))] + [HBM] * n,
        out_shape=[jax.ShapeDtypeStruct((t, SSM_INNER), F32),
                   jax.ShapeDtypeStruct((nc, SSM_PAIRS, SSM_STATE, LANES), F32)]
        + [jax.ShapeDtypeStruct((N_DEV,) + v.shape, v.dtype) for v in gather],
        scratch_shapes=[pltpu.VMEM((SSM_PAIRS, SSM_STATE, LANES), F32)] + _comm_scratch(n),
        compiler_params=_params(("arbitrary",)),
    )(xbc_act, xbc_act, xbc_act, proj, bias, alog, dexp, *gather)
    return out[:2], out[2:]


def _ssd_bwd(xbc_act, proj, s_in_all, dy, bias, alog, dexp, dproj, exchange):
    n_ex = len(exchange)
    t = proj.shape[0]
    nc = t // SSM_CHUNK
    last = nc - 1
    xs_s = pl.BlockSpec((SSM_CHUNK, SSM_INNER), lambda c: (last - c, 0))
    bm_s = pl.BlockSpec((SSM_CHUNK, SSM_BC), lambda c: (last - c, SSM_INNER // SSM_BC))
    cm_s = pl.BlockSpec((SSM_CHUNK, SSM_BC), lambda c: (last - c, SSM_INNER // SSM_BC + 1))
    dtr_s = pl.BlockSpec((SSM_CHUNK, LANES), lambda c: (last - c, COL_DT // LANES))
    sin_s = pl.BlockSpec((1, SSM_PAIRS, SSM_STATE, LANES), lambda c: (last - c, 0, 0, 0))
    vec = pl.BlockSpec((1, LANES), lambda c: (0, 0))
    wide = pl.BlockSpec((1, SSM_INNER), lambda c: (0, 0))

    def body(*refs):
        xs_ref, b_ref, c_ref, dtr_ref, sin_ref, dy_ref, bias_ref, alog_ref, dexp_ref = refs[:9]
        dxbc_ref, ddtr_ref, dbias_ref, dalog_ref, ddcol_ref = refs[10 + n_ex:15 + n_ex]
        dstate, dxdt_s, yoff_s, rx_s, trow_s = refs[15 + 2 * n_ex:20 + 2 * n_ex]
        comm = (refs[10:10 + n_ex], refs[15 + n_ex:15 + 2 * n_ex]) + tuple(refs[20 + 2 * n_ex:])

        @pl.when(pl.program_id(0) == 0)
        def _():
            _exchange_phase("start", *comm)
            dstate[...] = jnp.zeros_like(dstate)
            trow_s[...] = jnp.zeros_like(trow_s)
            dbias_ref[...] = jnp.zeros_like(dbias_ref)
            dalog_ref[...] = jnp.zeros_like(dalog_ref)
            ddcol_ref[...] = jnp.zeros_like(ddcol_ref)

        dtr = dtr_ref[...]
        a, dt, cs, dec, ecs, expand, row, col = _ssd_chunk_terms(dtr, bias_ref[...], alog_ref[...])
        cst = cs.T
        dt_x = _split_dot(dt, expand)
        dec_x = _split_dot(dec, expand)
        ecs_x = _split_dot(ecs, expand)
        xs = xs_ref[...]
        dyv = dy_ref[...]
        xdt = xs * dt_x
        lane_lo = col < SSM_HEAD_DIM
        causal = row >= col
        ddcol_ref[...] += jnp.sum(dyv * xs, axis=0, keepdims=True)
        dcs_col = jnp.zeros((SSM_CHUNK, LANES), F32)
        dcs_row = jnp.zeros((LANES, SSM_CHUNK), F32)
        for g in range(SSM_GROUPS):
            bg = b_ref[:, g * SSM_STATE:(g + 1) * SSM_STATE].astype(BF16)
            cg = c_ref[:, g * SSM_STATE:(g + 1) * SSM_STATE].astype(BF16)
            cb = _bdot(cg, bg, NT)
            dgm = jnp.zeros((SSM_CHUNK, SSM_CHUNK), F32)
            dbg = jnp.zeros((SSM_CHUNK, SSM_STATE), F32)
            dcg = jnp.zeros((SSM_CHUNK, SSM_STATE), F32)
            for q in range(SSM_PAIRS // SSM_GROUPS):
                pq = g * (SSM_PAIRS // SSM_GROUPS) + q
                sl = slice(pq * LANES, (pq + 1) * LANES)
                dyp = dyv[:, sl]
                xp = xdt[:, sl]
                dxh = []
                for hh in range(2):
                    h = 2 * pq + hh
                    lmat = jnp.exp(jnp.where(causal, cs[:, h:h + 1] - cst[h:h + 1, :], -jnp.inf))
                    mmat = cb * lmat
                    dyh = jnp.where(lane_lo if hh == 0 else jnp.logical_not(lane_lo), dyp, 0.0)
                    dmm = _bdot(dyh, xp, NT)
                    pm = dmm * mmat
                    dcs_col = jnp.where(col == h, jnp.sum(pm, axis=1, keepdims=True), dcs_col)
                    dcs_row = jnp.where(row == h, jnp.sum(pm, axis=0, keepdims=True), dcs_row)
                    dgm = dgm + dmm * lmat
                    dxh.append(_bdot(mmat, dyp, TN))
                s_in = sin_ref[0, pq]
                ecs_p = ecs_x[:, sl]
                dec_p = dec_x[:, sl]
                etot_p = ecs_x[SSM_CHUNK - 1:SSM_CHUNK, sl]
                yoff_s[:, sl] = dyp * (_bdot(cg, s_in) * ecs_p)
                dq = dyp * ecs_p
                dcg = dcg + _bdot(dq, s_in, NT)
                ds = dstate[pq]
                r = _bdot(bg, ds)
                rx_s[:, sl] = r * xp
                dxdt_s[:, sl] = jnp.where(lane_lo, dxh[0], dxh[1]) + dec_p * r
                dbg = dbg + _bdot(xp * dec_p, ds, NT)
                trow_s[0:1, sl] = jnp.sum(ds * s_in, axis=0, keepdims=True) * etot_p
                dstate[pq] = etot_p * ds + _bdot(cg, dq, TN)
            dcg = dcg + _bdot(dgm, bg)
            dbg = dbg + _bdot(dgm, cg, TN)
            dxbc_ref[:, SSM_INNER + g * SSM_STATE:SSM_INNER + (g + 1) * SSM_STATE] = dbg
            dxbc_ref[:, SSM_INNER + SSM_BC + g * SSM_STATE:SSM_INNER + SSM_BC + (g + 1) * SSM_STATE] = dcg
        ddec = _split_dot(rx_s[...], expand, NT) * dec
        dtot = _split_dot(trow_s[...], expand, NT)[0:1, :]
        dcs = dcs_col - dcs_row.T + _split_dot(yoff_s[...], expand, NT) - ddec
        dcs = dcs + jnp.where(row == SSM_CHUNK - 1, jnp.sum(ddec, axis=0, keepdims=True) + dtot, 0.0)
        da = _fdot((row <= col).astype(F32), dcs)
        dxdt = dxdt_s[...]
        ddt = da * a + _split_dot(dxdt * xs, expand, NT)
        dalog_ref[...] += jnp.sum(da * dt, axis=0, keepdims=True) * a
        ddtr = ddt * _sigmoid(dtr + bias_ref[...])
        ddtr_ref[...] = ddtr.astype(BF16)
        dbias_ref[...] += jnp.sum(ddtr, axis=0, keepdims=True)
        dxbc_ref[:, 0:SSM_INNER] = dxdt * dt_x + dyv * dexp_ref[...]

        @pl.when(pl.program_id(0) == last)
        def _():
            _exchange_phase("finish", *comm)

    out = pl.pallas_call(
        body, name="ssd_bwd", grid=(nc,),
        in_specs=[xs_s, bm_s, cm_s, dtr_s, sin_s, pl.BlockSpec((SSM_CHUNK, SSM_INNER), lambda c: (last - c, 0)),
                  vec, vec, wide, HBM] + [HBM] * n_ex,
        out_specs=[pl.BlockSpec((SSM_CHUNK, SSM_CONV_DIM), lambda c: (last - c, 0)), dtr_s, vec, vec, wide]
        + [HBM] * n_ex,
        out_shape=[jax.ShapeDtypeStruct((t, SSM_CONV_DIM), F32), jax.ShapeDtypeStruct(dproj.shape, dproj.dtype),
                   jax.ShapeDtypeStruct((1, LANES), F32), jax.ShapeDtypeStruct((1, LANES), F32),
                   jax.ShapeDtypeStruct((1, SSM_INNER), F32)]
        + [jax.ShapeDtypeStruct(p.shape, p.dtype) for p in exchange],
        input_output_aliases={9: 1},
        scratch_shapes=[pltpu.VMEM((SSM_PAIRS, SSM_STATE, LANES), F32),
                        pltpu.VMEM((SSM_CHUNK, SSM_INNER), F32), pltpu.VMEM((SSM_CHUNK, SSM_INNER), F32),
                        pltpu.VMEM((SSM_CHUNK, SSM_INNER), F32), pltpu.VMEM((SUBLANES, SSM_INNER), F32)]
        + _comm_scratch(n_ex),
        compiler_params=_params(("arbitrary",)),
    )(xbc_act, xbc_act, xbc_act, proj, s_in_all, dy, bias, alog, dexp, dproj, *exchange)
    return out[:5], out[5:]


def _group_rstd(y):
    n = SSM_INNER // SSM_GROUPS
    parts = []
    for g in range(SSM_GROUPS):
        yg = y[:, g * n:(g + 1) * n]
        r = lax.rsqrt(jnp.mean(yg * yg, axis=-1, keepdims=True) + RMS_EPS)
        parts.append(jnp.broadcast_to(r, yg.shape))
    return jnp.concatenate(parts, axis=1)


def _group_mean(v):
    n = SSM_INNER // SSM_GROUPS
    parts = []
    for g in range(SSM_GROUPS):
        vg = v[:, g * n:(g + 1) * n]
        parts.append(jnp.broadcast_to(jnp.mean(vg, axis=-1, keepdims=True), vg.shape))
    return jnp.concatenate(parts, axis=1)


def _ssm_post_fwd(y_ssd, proj, nw):
    t = proj.shape[0]
    n = SSM_INNER
    tr = _rows(t)

    def body(y_ref, z_ref, nw_ref, o_ref):
        z = z_ref[...]
        y = y_ref[...] * (z * _sigmoid(z))
        o_ref[...] = (y * _group_rstd(y) * nw_ref[...]).astype(BF16)

    row = pl.BlockSpec((tr, n), lambda i: (i, 0))
    return pl.pallas_call(
        body, name="ssm_post_fwd", grid=(t // tr,),
        in_specs=[row, pl.BlockSpec((tr, n), lambda i: (i, COL_Z // n)), pl.BlockSpec((1, n), lambda i: (0, 0))],
        out_specs=row, out_shape=jax.ShapeDtypeStruct((t, n), BF16),
        compiler_params=_params(("parallel",)),
    )(y_ssd, proj, nw)


def _ssm_post_bwd(y_ssd, proj, nw, dout, dproj):
    t = proj.shape[0]
    n = SSM_INNER
    tr = _rows(t)

    def body(y_ref, z_ref, nw_ref, do_ref, dproj_in, dy_ref, dz_ref, dnw_ref):
        i = pl.program_id(0)
        z = z_ref[...]
        sg = _sigmoid(z)
        sz = z * sg
        ys = y_ref[...]
        y = ys * sz
        rstd = _group_rstd(y)
        yn = y * rstd
        dov = do_ref[...]
        dyn = dov * nw_ref[...]
        dyg = rstd * (dyn - yn * _group_mean(dyn * yn))
        dy_ref[...] = dyg * sz
        dz_ref[...] = (dyg * ys * (sg * (1.0 + z * (1.0 - sg)))).astype(BF16)
        part = jnp.sum(dov * yn, axis=0, keepdims=True)

        @pl.when(i == 0)
        def _():
            dnw_ref[...] = part

        @pl.when(i > 0)
        def _():
            dnw_ref[...] += part

    row = pl.BlockSpec((tr, n), lambda i: (i, 0))
    vec = pl.BlockSpec((1, n), lambda i: (0, 0))
    return pl.pallas_call(
        body, name="ssm_post_bwd", grid=(t // tr,),
        in_specs=[row, pl.BlockSpec((tr, n), lambda i: (i, COL_Z // n)), vec, row, HBM],
        out_specs=[row, pl.BlockSpec((tr, n), lambda i: (i, COL_Z // n)), vec],
        out_shape=[jax.ShapeDtypeStruct((t, n), F32), jax.ShapeDtypeStruct(dproj.shape, dproj.dtype),
                   jax.ShapeDtypeStruct((1, n), F32)],
        input_output_aliases={4: 1},
        compiler_params=_params(("arbitrary",)),
    )(y_ssd, proj, nw, dout, dproj)


SCAN_UNROLL = 8
GELU_C = math.sqrt(2.0 / math.pi)
GELU_K = 0.044715


def _gelu_parts(y):
    th = jnp.tanh(GELU_C * (y + GELU_K * y * y * y))
    val = 0.5 * y * (1.0 + th)
    grad = 0.5 * (1.0 + th) + 0.5 * y * (1.0 - th * th) * GELU_C * (1.0 + 3.0 * GELU_K * y * y)
    return val, grad


def _scan_tiles(a_ref, b_ref, h_ref, n_rows, reverse):
    n_tiles = n_rows // SUBLANES
    shape = (SUBLANES, a_ref.shape[1])
    row = _iota(shape, 0)

    def in_tile(av, bv):
        for s in (1, 2, 4):
            if reverse:
                keep = row < SUBLANES - s
                a_sh = jnp.where(keep, pltpu.roll(av, SUBLANES - s, 0), 1.0)
                b_sh = jnp.where(keep, pltpu.roll(bv, SUBLANES - s, 0), 0.0)
            else:
                keep = row >= s
                a_sh = jnp.where(keep, pltpu.roll(av, s, 0), 1.0)
                b_sh = jnp.where(keep, pltpu.roll(bv, s, 0), 0.0)
            bv = av * b_sh + bv
            av = av * a_sh
        return av, bv

    def step(k, carry):
        first = (n_tiles // SCAN_UNROLL - 1 - k) if reverse else k
        tiles = [first * SCAN_UNROLL + j for j in range(SCAN_UNROLL)]
        if reverse:
            tiles = tiles[::-1]
        ats = [pl.ds(pl.multiple_of(tile * SUBLANES, SUBLANES), SUBLANES) for tile in tiles]
        scanned = [in_tile(a_ref[at, :], b_ref[at, :]) for at in ats]
        for at, (av, bv) in zip(ats, scanned):
            hv = bv + av * carry
            h_ref[at, :] = hv
            carry = hv[0:1, :] if reverse else hv[SUBLANES - 1:SUBLANES, :]
        return carry

    assert n_tiles % SCAN_UNROLL == 0, n_rows
    lax.fori_loop(0, n_tiles // SCAN_UNROLL, step, jnp.zeros((1, a_ref.shape[1]), F32))


def _lru_gates(xl, cw, cb, wr, br, wi, bi, lam):
    u = cb + cw[CONV_K - 1:CONV_K, :] * xl
    for k in range(CONV_K - 1):
        u = u + cw[k:k + 1, :] * _shift_down(xl, CONV_K - 1 - k)
    r = _sigmoid(_bdot(u, wr) + br)
    i = _sigmoid(_bdot(u, wi) + bi)
    sp = _softplus(-lam)
    la = -LRU_C * r * sp
    a = jnp.exp(la)
    mult = jnp.sqrt(-jnp.tanh(la) * (a * a + 1.0))
    return u, r, i, sp, a, mult


def _lru_specs(t):
    c0 = COL_LRU // LANES
    xl = pl.BlockSpec((t, LANES), lambda j: (0, c0 + 2 * j))
    yl = pl.BlockSpec((t, LANES), lambda j: (0, c0 + 2 * j + 1))
    col = pl.BlockSpec((t, LANES), lambda j: (0, j))
    cw = pl.BlockSpec((CONV_K, LANES), lambda j: (0, j))
    vec = pl.BlockSpec((1, LANES), lambda j: (0, j))
    wblk = pl.BlockSpec((1, LANES, LANES), lambda j: (j, 0, 0))
    return xl, yl, col, cw, vec, wblk


def _lru_fwd(proj, cw, cb, wr, br, wi, bi, lam, gather):
    t = proj.shape[0]
    xl_s, yl_s, col, cw_s, vec, wblk = _lru_specs(t)
    n = len(gather)

    def body(*refs):
        xl_ref, yl_ref, cw_ref, cb_ref, wr_ref, br_ref, wi_ref, bi_ref, lam_ref = refs[:9]
        o_ref, h_ref = refs[9 + n:11 + n]
        a_s, b_s = refs[11 + 2 * n:13 + 2 * n]
        comm = (refs[9:9 + n], refs[11 + n:11 + 2 * n]) + tuple(refs[13 + 2 * n:])
        j = pl.program_id(0)
        for step, phase in ((0, "start"), (LRU_BLOCKS // 2, "forward")):
            @pl.when(j == step)
            def _():
                _gather_phase(phase, *comm)

        u, r, i, sp, a, mult = _lru_gates(xl_ref[...], cw_ref[...], cb_ref[...], wr_ref[0], br_ref[...],
                                          wi_ref[0], bi_ref[...], lam_ref[...])
        a_s[...] = a
        b_s[...] = mult * (i * u)
        _scan_tiles(a_s, b_s, h_ref, t, reverse=False)
        o_ref[...] = (h_ref[...] * _gelu_parts(yl_ref[...])[0]).astype(BF16)

        @pl.when(j == LRU_BLOCKS - 1)
        def _():
            _gather_phase("finish", *comm)

    out = pl.pallas_call(
        body, name="lru_fwd", grid=(LRU_BLOCKS,),
        in_specs=[xl_s, yl_s, cw_s, vec, wblk, vec, wblk, vec, vec] + [HBM] * n,
        out_specs=[col, col] + [HBM] * n,
        out_shape=[jax.ShapeDtypeStruct((t, LRU_WIDTH), BF16), jax.ShapeDtypeStruct((t, LRU_WIDTH), F32)]
        + [jax.ShapeDtypeStruct((N_DEV,) + v.shape, v.dtype) for v in gather],
        scratch_shapes=[pltpu.VMEM((t, LANES), F32)] * 2 + _comm_scratch(n),
        compiler_params=_params(("arbitrary",), big=True),
    )(proj, proj, cw, cb, wr, br, wi, bi, lam, *gather)
    return out[:2], out[2:]


def _lru_bwd(proj, cw, cb, wr, br, wi, bi, lam, h_all, dout, dproj, exchange):
    t = proj.shape[0]
    xl_s, yl_s, col, cw_s, vec, wblk = _lru_specs(t)
    pair = pl.BlockSpec((t, 2 * LANES), lambda j: (0, COL_LRU // (2 * LANES) + j))
    n_ex = len(exchange)

    def body(*refs):
        xl_ref, yl_ref, cw_ref, cb_ref, wr_ref, br_ref, wi_ref, bi_ref, lam_ref, h_ref, do_ref = refs[:11]
        dxy_ref, dcw_ref, dcb_ref, dwr_ref, dbr_ref, dwi_ref, dbi_ref, dlam_ref = refs[12 + n_ex:20 + n_ex]
        a_s, b_s, g_s = refs[20 + 2 * n_ex:23 + 2 * n_ex]
        comm = (refs[12:12 + n_ex], refs[20 + n_ex:20 + 2 * n_ex]) + tuple(refs[23 + 2 * n_ex:])

        @pl.when(pl.program_id(0) == 0)
        def _():
            _exchange_phase("start", *comm)

        xl = xl_ref[...]
        cwv = cw_ref[...]
        lam = lam_ref[...]
        u, r, i, sp, a, mult = _lru_gates(xl, cwv, cb_ref[...], wr_ref[0], br_ref[...], wi_ref[0], bi_ref[...], lam)
        v = i * u
        gl, dgl = _gelu_parts(yl_ref[...])
        dov = do_ref[...]
        h = h_ref[...]
        dxy_ref[:, LANES:2 * LANES] = (dov * h * dgl).astype(BF16)
        b_s[...] = dov * gl
        a_s[...] = _shift_up(a, 1)
        _scan_tiles(a_s, b_s, g_s, t, reverse=True)
        g = g_s[...]
        da = g * _shift_down(h, 1)
        dmult = g * v
        dv = g * mult
        dla = da * a - dmult * (a * a) / mult
        dr = dla * (-LRU_C * sp)
        dsp = jnp.sum(dla * (-LRU_C * r), axis=0, keepdims=True)
        dlam_ref[...] = -dsp * _sigmoid(-lam)
        dpr = dr * r * (1.0 - r)
        dpi = dv * u * i * (1.0 - i)
        dbr_ref[...] = jnp.sum(dpr, axis=0, keepdims=True)
        dbi_ref[...] = jnp.sum(dpi, axis=0, keepdims=True)
        dwr_ref[0] = _bdot(u, dpr, TN)
        dwi_ref[0] = _bdot(u, dpi, TN)
        du = dv * i + _bdot(dpr, wr_ref[0], NT) + _bdot(dpi, wi_ref[0], NT)
        dxl = cwv[CONV_K - 1:CONV_K, :] * du
        for k in range(CONV_K - 1):
            dxl = dxl + cwv[k:k + 1, :] * _shift_up(du, CONV_K - 1 - k)
        dxy_ref[:, 0:LANES] = dxl.astype(BF16)
        for k in range(CONV_K):
            dcw_ref[k:k + 1, :] = jnp.sum(du * _shift_down(xl, CONV_K - 1 - k), axis=0, keepdims=True)
        dcb_ref[...] = jnp.sum(du, axis=0, keepdims=True)

        @pl.when(pl.program_id(0) == LRU_BLOCKS - 1)
        def _():
            _exchange_phase("finish", *comm)

    out = pl.pallas_call(
        body, name="lru_bwd", grid=(LRU_BLOCKS,),
        in_specs=[xl_s, yl_s, cw_s, vec, wblk, vec, wblk, vec, vec, col, col, HBM] + [HBM] * n_ex,
        out_specs=[pair, cw_s, vec, wblk, vec, wblk, vec, vec] + [HBM] * n_ex,
        input_output_aliases={11: 0},
        out_shape=[jax.ShapeDtypeStruct(dproj.shape, dproj.dtype),
                   jax.ShapeDtypeStruct((CONV_K, LRU_WIDTH), F32), jax.ShapeDtypeStruct((1, LRU_WIDTH), F32),
                   jax.ShapeDtypeStruct((LRU_BLOCKS, LANES, LANES), F32), jax.ShapeDtypeStruct((1, LRU_WIDTH), F32),
                   jax.ShapeDtypeStruct((LRU_BLOCKS, LANES, LANES), F32), jax.ShapeDtypeStruct((1, LRU_WIDTH), F32),
                   jax.ShapeDtypeStruct((1, LRU_WIDTH), F32)]
        + [jax.ShapeDtypeStruct(p.shape, p.dtype) for p in exchange],
        scratch_shapes=[pltpu.VMEM((t, LANES), F32)] * 3 + _comm_scratch(n_ex),
        compiler_params=_params(("arbitrary",), big=True),
    )(proj, proj, cw, cb, wr, br, wi, bi, lam, h_all, dout, dproj, *exchange)
    return out[:8], out[8:]


def _mesh_pos():
    return lax.axis_index("x"), lax.axis_index("y"), lax.axis_index("c")


HBM = pl.BlockSpec(memory_space=pl.ANY)


def _comm_scratch(n):
    return [pltpu.SemaphoreType.DMA((n, 7)), pltpu.SemaphoreType.DMA((n, 7)), pltpu.SemaphoreType.DMA((n,))]


def _gather_phase(phase, v_refs, out_refs, send_sems, recv_sems, local_sems):
    n = len(v_refs)
    x, y, c = _mesh_pos()
    me, sibling = (x, y, c), (x, y, 1 - c)
    chips = [(1 - x, y), (x, 1 - y), (1 - x, 1 - y)]

    def block(a, px, py, pc):
        return out_refs[a].at[4 * px + 2 * py + pc]

    def copy(a, k, blk, to, src=None):
        return pltpu.make_async_remote_copy(
            src_ref=block(a, *blk) if src is None else src, dst_ref=block(a, *blk),
            send_sem=send_sems.at[a, k], recv_sem=recv_sems.at[a, k], device_id=to, device_id_type=MESH)

    def own(a):
        return pltpu.make_async_copy(v_refs[a], block(a, *me), local_sems.at[a])

    def first(a):
        return ([copy(a, 0, me, sibling, src=v_refs[a])]
                + [copy(a, 1 + j, me, (*chip, c), src=v_refs[a]) for j, chip in enumerate(chips)])

    def forward(a, j):
        return copy(a, 4 + j, (*chips[j], c), sibling)

    if phase == "start":
        for a in range(n):
            own(a).start()
        for a in range(n):
            for cp in first(a):
                cp.start()
    elif phase == "forward":
        for j in range(3):
            for a in range(n):
                copy(a, 1 + j, (*chips[j], c), me).wait_recv()
                forward(a, j).start()
    else:
        for a in range(n):
            copy(a, 0, sibling, me).wait_recv()
            for j in range(3):
                copy(a, 4 + j, (*chips[j], 1 - c), me).wait_recv()
        for a in range(n):
            for cp in first(a) + [forward(a, j) for j in range(3)]:
                cp.wait_send()
            own(a).wait()


def _all_gather(vs, name):
    n = len(vs)

    def body(*refs):
        comm = (refs[:n], refs[n:2 * n]) + tuple(refs[2 * n:])
        for phase in ("start", "forward", "finish"):
            _gather_phase(phase, *comm)

    return pl.pallas_call(
        body, name=name,
        out_shape=[jax.ShapeDtypeStruct((N_DEV,) + v.shape, v.dtype) for v in vs],
        in_specs=[HBM] * n, out_specs=[HBM] * n, scratch_shapes=_comm_scratch(n),
    )(*vs)


def _run_copies(phase, local, remote):
    if phase == "start":
        for cp in local + remote:
            cp.start()
    else:
        for cp in remote:
            cp.wait()
        for cp in local:
            cp.wait()


def _exchange_phase(phase, p_refs, out_refs, send_sems, recv_sems, local_sems):
    n = len(p_refs)
    x, y, c = _mesh_pos()
    me = 4 * x + 2 * y + c
    local = [pltpu.make_async_copy(p_refs[a].at[me], out_refs[a].at[me], local_sems.at[a]) for a in range(n)]
    remote = []
    for k in range(1, N_DEV):
        px = (1 - x) if k & 4 else x
        py = (1 - y) if k & 2 else y
        pc = (1 - c) if k & 1 else c
        for a in range(n):
            remote.append(pltpu.make_async_remote_copy(
                src_ref=p_refs[a].at[4 * px + 2 * py + pc], dst_ref=out_refs[a].at[me],
                send_sem=send_sems.at[a, k - 1], recv_sem=recv_sems.at[a, k - 1],
                device_id=(px, py, pc), device_id_type=MESH))
    _run_copies(phase, local, remote)


def _chip_exchange_phase(phase, p_refs, out_refs, send_sems, recv_sems, local_sems):
    n = len(p_refs)
    x, y, c = _mesh_pos()
    me = 2 * x + y
    local = [pltpu.make_async_copy(p_refs[a].at[me], out_refs[a].at[me], local_sems.at[a]) for a in range(n)]
    remote = []
    for k in range(1, 4):
        px = (1 - x) if k & 2 else x
        py = (1 - y) if k & 1 else y
        for a in range(n):
            remote.append(pltpu.make_async_remote_copy(
                src_ref=p_refs[a].at[2 * px + py], dst_ref=out_refs[a].at[me],
                send_sem=send_sems.at[a, k - 1], recv_sem=recv_sems.at[a, k - 1],
                device_id=(px, py, c), device_id_type=MESH))
    _run_copies(phase, local, remote)


def _sibling_exchange(parts, name):
    chips = N_DEV // 2

    def body(p_ref, out_ref, send_sems, recv_sems):
        x, y, c = _mesh_pos()
        copies = [pltpu.make_async_remote_copy(
            src_ref=p_ref.at[2 * q + 1 - c], dst_ref=out_ref.at[q], send_sem=send_sems.at[q], recv_sem=recv_sems.at[q],
            device_id=(x, y, 1 - c), device_id_type=MESH) for q in range(chips)]
        _run_copies("start", [], copies)
        _run_copies("finish", [], copies)

    return pl.pallas_call(
        body, name=name, out_shape=jax.ShapeDtypeStruct((chips,) + parts.shape[1:], parts.dtype),
        in_specs=[HBM], out_specs=HBM,
        scratch_shapes=[pltpu.SemaphoreType.DMA((chips,)), pltpu.SemaphoreType.DMA((chips,))],
    )(parts)


def _pair_sum(mine, theirs, name):
    slots, rows, cols = mine.shape
    tc = 256

    def body(a_ref, b_ref, o_ref):
        o_ref[...] = (a_ref[...].astype(F32) + b_ref[...].astype(F32)).astype(o_ref.dtype)

    spec = pl.BlockSpec((1, rows, tc), lambda q, j: (q, 0, j))
    return pl.pallas_call(
        body, name=name, grid=(slots, cols // tc), in_specs=[spec, spec], out_specs=spec,
        out_shape=jax.ShapeDtypeStruct(mine.shape, mine.dtype),
        compiler_params=_params(("parallel", "parallel")),
    )(mine, theirs)


def _sum_sources(recvs, name):
    k = len(recvs)

    def body(*refs):
        for r_ref, o_ref in zip(refs[:k], refs[k:]):
            acc = r_ref[0].astype(F32)
            for s in range(1, r_ref.shape[0]):
                acc = acc + r_ref[s].astype(F32)
            o_ref[...] = acc

    return pl.pallas_call(
        body, name=name, out_shape=[jax.ShapeDtypeStruct(r.shape[1:], F32) for r in recvs],
        compiler_params=_params(),
    )(*recvs)


def _row_tile(rows):
    for tile in range(128, 15, -16):
        if rows % tile == 0:
            return tile
    return rows


def _adam_update(w, g, m, v):
    nm = ADAM_B1 * m + (1.0 - ADAM_B1) * g
    nv = ADAM_B2 * v + (1.0 - ADAM_B2) * (g * g)
    m_hat = nm / (1.0 - ADAM_B1 ** ADAM_STEP)
    v_hat = nv / (1.0 - ADAM_B2 ** ADAM_STEP)
    return -ADAM_LR * (m_hat / (jnp.sqrt(v_hat) + ADAM_EPS) + ADAM_WD * w), nm, nv


def _vector_offsets(widths):
    offsets, end = [], 0
    for c in widths:
        offsets.append(end)
        end += c + (-c) % LANES
    return offsets, end


def _adamw_small(vec_parts, vec_state, mat_grads, mat_state):
    widths = [w.shape[1] for w, _, _ in vec_state]
    offsets, total = _vector_offsets(widths)
    assert vec_parts.shape == (N_DEV, total), (vec_parts.shape, total)
    n_vec, n_mat = len(vec_state), len(mat_state)

    def body(*refs):
        r_ref = refs[0]
        vec_in = refs[1:1 + 3 * n_vec]
        mat_in = refs[1 + 3 * n_vec:1 + 3 * n_vec + 4 * n_mat]
        outs = refs[1 + 3 * n_vec + 4 * n_mat:]
        for i, (off, c) in enumerate(zip(offsets, widths)):
            g = r_ref[0:1, off:off + c]
            for s in range(1, N_DEV):
                g = g + r_ref[s:s + 1, off:off + c]
            w_ref, m_ref, v_ref = vec_in[3 * i:3 * i + 3]
            g_out, d_out, m_out, v_out = outs[4 * i:4 * i + 4]
            g_out[...] = g
            d_out[...], m_out[...], v_out[...] = _adam_update(w_ref[...], g, m_ref[...], v_ref[...])
        for j in range(n_mat):
            g_ref, w_ref, m_ref, v_ref = mat_in[4 * j:4 * j + 4]
            d_out, m_out, v_out = outs[4 * n_vec + 3 * j:4 * n_vec + 3 * j + 3]
            d_out[...], m_out[...], v_out[...] = _adam_update(w_ref[...], g_ref[...], m_ref[...], v_ref[...])

    args = [vec_parts] + [a for state in vec_state for a in state]
    for g, state in zip(mat_grads, mat_state):
        args += [g, *state]
    out_shape = [jax.ShapeDtypeStruct(w.shape, F32) for w, _, _ in vec_state for _ in range(4)]
    out_shape += [jax.ShapeDtypeStruct(w.shape, F32) for w, _, _ in mat_state for _ in range(3)]
    out = pl.pallas_call(body, name="adamw_replicated", out_shape=out_shape, compiler_params=_params())(*args)
    vec_out = [tuple(out[4 * i:4 * i + 4]) for i in range(n_vec)]
    mat_out = [tuple(out[4 * n_vec + 3 * j:4 * n_vec + 3 * j + 3]) for j in range(n_mat)]
    return vec_out, mat_out


def _adamw(w, recv, m, v, name):
    rows, width = w.shape
    n = recv.shape[0]
    if rows % 16 == 0 or width % 256:
        tr, tc = _row_tile(rows), width
    else:
        tr, tc = rows, 256

    def body(w_ref, r_ref, m_ref, v_ref, g_ref, d_ref, nm_ref, nv_ref):
        gv = r_ref[0].astype(F32)
        for s in range(1, n):
            gv = gv + r_ref[s].astype(F32)
        g_ref[...] = gv
        d_ref[...], nm_ref[...], nv_ref[...] = _adam_update(w_ref[...], gv, m_ref[...], v_ref[...])

    spec = pl.BlockSpec((tr, tc), lambda i, j: (i, j))
    shape = jax.ShapeDtypeStruct((rows, width), F32)
    return pl.pallas_call(
        body, name=name, grid=(rows // tr, width // tc),
        in_specs=[spec, pl.BlockSpec((n, tr, tc), lambda i, j: (0, i, j)), spec, spec],
        out_specs=[spec] * 4, out_shape=[shape] * 4,
        compiler_params=_params(("parallel", "parallel")),
    )(w, recv, m, v)


BIG_NAMES = ("w_in", "w_out_ssm", "w_out_lru", "w_out", "w_ffn_in", "w_ffn_out", "ssm_conv_w", "lru_conv_w")
TRANSPOSED = ("w_in", "w_ffn_in")
CONV_NAMES = ("ssm_conv_w", "lru_conv_w")
MATMUL_NAMES = BIG_NAMES[:6]
NEEDED_FIRST = ("w_in", "ssm_conv_w", "lru_conv_w")
GATHERED_IN_SSD = ("w_ffn_in",)
GATHERED_IN_LRU = ("w_ffn_out", "w_out_ssm", "w_out_lru", "w_out")
EXCHANGED_IN_SSD = ("w_ffn_in", "w_ffn_out")
EXCHANGED_IN_LRU = ("w_out_ssm", "w_out_lru", "w_out")
SMALL_VECTORS = ("norm1_w", "b_branch_gate", "ssm_conv_b", "ssm_dt_bias", "ssm_a_log", "ssm_d", "ssm_norm_w",
                 "lru_conv_b", "lru_b_r", "lru_b_i", "lru_lambda", "norm2_w", "norm_f_w")
SMALL_MATRICES = ("lru_w_r", "lru_w_i")


def _col_shards(full):
    rows, cols = full.shape
    return full.reshape(rows, N_DEV, cols // N_DEV).transpose(1, 0, 2)


def _from_col_shards(g):
    n, rows, w = g.shape
    return g.transpose(1, 0, 2).reshape(rows, n * w)


def kernel(x, norm1_w, w_in, b_branch_gate, ssm_conv_w, ssm_conv_b, ssm_dt_bias, ssm_a_log, ssm_d, ssm_norm_w, w_out_ssm, lru_conv_w, lru_conv_b, lru_w_r, lru_b_r, lru_w_i, lru_b_i, lru_lambda, w_out_lru, w_out, norm2_w, w_ffn_in, w_ffn_out, norm_f_w, loss_target, m_norm1_w, m_w_in, m_b_branch_gate, m_ssm_conv_w, m_ssm_conv_b, m_ssm_dt_bias, m_ssm_a_log, m_ssm_d, m_ssm_norm_w, m_w_out_ssm, m_lru_conv_w, m_lru_conv_b, m_lru_w_r, m_lru_b_r, m_lru_w_i, m_lru_b_i, m_lru_lambda, m_w_out_lru, m_w_out, m_norm2_w, m_w_ffn_in, m_w_ffn_out, m_norm_f_w, v_norm1_w, v_w_in, v_b_branch_gate, v_ssm_conv_w, v_ssm_conv_b, v_ssm_dt_bias, v_ssm_a_log, v_ssm_d, v_ssm_norm_w, v_w_out_ssm, v_lru_conv_w, v_lru_conv_b, v_lru_w_r, v_lru_b_r, v_lru_w_i, v_lru_b_i, v_lru_lambda, v_w_out_lru, v_w_out, v_norm2_w, v_w_ffn_in, v_w_ffn_out, v_norm_f_w):
    given = dict(locals())
    weights = {n: given[n] for n in BIG_NAMES + SMALL_VECTORS + SMALL_MATRICES}
    t = x.shape[1]
    xt = x[0]
    tgt = loss_target[0]

    def local(n, a):
        return a[0].T if n in TRANSPOSED else a[0]

    def as_output(n, a):
        return a.T[None] if n in TRANSPOSED else a[None]

    def shard(n):
        s = local(n, weights[n])
        return s.astype(BF16) if n in MATMUL_NAMES else s

    def unshard(n, g):
        return _from_col_shards(g) if n in CONV_NAMES else g.reshape(-1, g.shape[-1])

    def grad_slices(n):
        g = grads[n]
        return (_col_shards(g) if n in CONV_NAMES else g.reshape(N_DEV, -1, g.shape[-1])).astype(BF16)

    gathered = _all_gather([shard(n) for n in NEEDED_FIRST], "gather_in_weights")
    full = {n: unshard(n, g) for n, g in zip(NEEDED_FIRST, gathered)}
    ssm_cw, lru_cw = full["ssm_conv_w"], full["lru_conv_w"]
    wi_t = full["w_in"]
    lru_rows = [wi_t[o + j * LANES:o + (j + 1) * LANES] for j in range(LRU_BLOCKS) for o in (ORIG_LRU_X, ORIG_LRU_Y)]
    w_pt = jnp.concatenate([wi_t[:ORIG_DT]] + lru_rows + [wi_t[ORIG_DT:ORIG_LRU_X],
                                                          jnp.zeros((PROJ_W - IN_PROJ, D_MODEL), BF16)], axis=0)

    def pad_heads(a):
        return jnp.pad(a.reshape(1, SSM_HEADS), ((0, 0), (0, LANES - SSM_HEADS)))

    dt_bias_p = pad_heads(ssm_dt_bias)
    a_log_p = pad_heads(ssm_a_log)
    d_exp = jnp.repeat(ssm_d.reshape(SSM_HEADS), SSM_HEAD_DIM).reshape(1, SSM_INNER)
    lru_wr, lru_wi = lru_w_r[0], lru_w_i[0]

    hn1 = _rmsnorm_fwd(xt, norm1_w, "norm1_fwd")
    proj = _mm(hn1, w_pt, tb=True, name="in_proj")
    xbc_act = _ssm_conv_fwd(proj, ssm_cw, ssm_conv_b)
    (y_ssd, s_in_all), gathered = _ssd_fwd(xbc_act, proj, dt_bias_p, a_log_p, d_exp,
                                           gather=[shard(n) for n in GATHERED_IN_SSD])
    full.update({n: unshard(n, g) for n, g in zip(GATHERED_IN_SSD, gathered)})
    (l_out, h_lru), gathered = _lru_fwd(proj, lru_cw, lru_conv_b, lru_wr, lru_b_r, lru_wi, lru_b_i, lru_lambda,
                                        gather=[shard(n) for n in GATHERED_IN_LRU])
    full.update({n: unshard(n, g) for n, g in zip(GATHERED_IN_LRU, gathered)})
    y_pre = _ssm_post_fwd(y_ssd, proj, ssm_norm_w)
    y_ssm = _mm(y_pre, full["w_out_ssm"], name="out_ssm")
    y_lru = _mm(l_out, full["w_out_lru"], name="out_lru")
    merged = _merge_fwd(proj, b_branch_gate, y_ssm, y_lru)
    h1 = _mm(merged, full["w_out"], add=xt, name="out_proj")
    hn2 = _rmsnorm_fwd(h1, norm2_w, "norm2_fwd")
    gu = _mm(hn2, full["w_ffn_in"], tb=True, name="ffn_in")
    act = _swiglu_fwd(gu)
    h2 = _mm(act, full["w_ffn_out"], add=h1, name="ffn_out")

    grads = {}
    dh2, grads["norm_f_w"], loss_cols = _loss_head(h2, norm_f_w.reshape(1, D_MODEL), tgt)
    loss = lax.psum(0.5 * jnp.sum(loss_cols) / D_MODEL, AXES)
    dact = _mm(dh2, full["w_ffn_out"], tb=True, name="d_act")
    grads["w_ffn_out"] = _mm(act, dh2, ta=True, out_dtype=BF16, name="dw_ffn_out")
    dgu = _swiglu_bwd(gu, dact)
    dhn2 = _mm(dgu, full["w_ffn_in"], name="d_hn2")
    grads["w_ffn_in"] = _mm(dgu, hn2, ta=True, out_dtype=BF16, name="dw_ffn_in")
    dh1, grads["norm2_w"] = _rmsnorm_bwd(h1, norm2_w, dhn2, dh2, "norm2_bwd")
    dmerged = _mm(dh1, full["w_out"], tb=True, name="d_merged")
    grads["w_out"] = _mm(merged, dh1, ta=True, out_dtype=BF16, name="dw_out")
    dy_ssm, dy_lru, dproj, grads["b_branch_gate"] = _merge_bwd(proj, b_branch_gate, y_ssm, y_lru, dmerged)
    dy_pre = _mm(dy_ssm, full["w_out_ssm"], tb=True, name="d_y_pre")
    grads["w_out_ssm"] = _mm(y_pre, dy_ssm, ta=True, out_dtype=BF16, name="dw_out_ssm")
    dl_out = _mm(dy_lru, full["w_out_lru"], tb=True, name="d_l_out")
    grads["w_out_lru"] = _mm(l_out, dy_lru, ta=True, out_dtype=BF16, name="dw_out_lru")
    dy_ssd, dproj, grads["ssm_norm_w"] = _ssm_post_bwd(y_ssd, proj, ssm_norm_w, dy_pre, dproj)
    (dxbc_act, dproj, dbias, dalog, ddcol), recv_in_ssd = _ssd_bwd(
        xbc_act, proj, s_in_all, dy_ssd, dt_bias_p, a_log_p, d_exp, dproj,
        exchange=[grad_slices(n) for n in EXCHANGED_IN_SSD])
    grads["ssm_dt_bias"] = dbias[:, :SSM_HEADS]
    grads["ssm_a_log"] = dalog[:, :SSM_HEADS]
    grads["ssm_d"] = ddcol.reshape(SSM_HEADS, SSM_HEAD_DIM).sum(axis=1).reshape(1, SSM_HEADS)
    dproj, grads["ssm_conv_w"], grads["ssm_conv_b"] = _ssm_conv_bwd(proj, ssm_cw, ssm_conv_b, dxbc_act, dproj)
    ((dproj, grads["lru_conv_w"], grads["lru_conv_b"], dwr, grads["lru_b_r"], dwi, grads["lru_b_i"],
      grads["lru_lambda"]), recv_in_lru) = _lru_bwd(
        proj, lru_cw, lru_conv_b, lru_wr, lru_b_r, lru_wi, lru_b_i, lru_lambda, h_lru, dl_out, dproj,
        exchange=[grad_slices(n) for n in EXCHANGED_IN_LRU])
    grads["lru_w_r"], grads["lru_w_i"] = dwr[None], dwi[None]
    dwpt = _mm(dproj, hn1, ta=True, out_dtype=BF16, name="dw_in")
    lru_rows = [dwpt[COL_LRU + (2 * j + k) * LANES:COL_LRU + (2 * j + k + 1) * LANES]
                for k in range(2) for j in range(LRU_BLOCKS)]
    grads["w_in"] = jnp.concatenate([dwpt[:ORIG_DT], dwpt[COL_DT:COL_DT + SSM_HEADS]] + lru_rows, axis=0)
    w_in_parts = grad_slices("w_in")
    from_sibling = _sibling_exchange(w_in_parts, "sibling_exchange_dw_in")
    for_my_core = lax.dynamic_index_in_dim(w_in_parts.reshape((N_DEV // 2, 2) + w_in_parts.shape[1:]),
                                           lax.axis_index("c"), axis=1, keepdims=False)
    chip_parts = [_pair_sum(for_my_core, from_sibling, "pair_sum_dw_in")]
    direct = [grad_slices(n) for n in CONV_NAMES] + [grads[n].reshape(N_DEV, -1, LANES) for n in SMALL_MATRICES]
    dhn1, (recv_w_in, recv_direct) = _mm(
        dproj, w_pt, name="d_hn1",
        hosted=[(_chip_exchange_phase, chip_parts, chip_parts), (_exchange_phase, direct, direct)])
    recv_first = list(recv_w_in) + list(recv_direct[:len(CONV_NAMES)])
    recv_mats = recv_direct[len(CONV_NAMES):]
    grad_x, grads["norm1_w"] = _rmsnorm_bwd(xt, norm1_w, dhn1, dh1, "norm1_bwd")

    def as_rows(n, a):
        return a.reshape(1, -1) if n in SMALL_VECTORS else a.reshape(-1, LANES)

    vec_g = jnp.concatenate([jnp.pad(as_rows(n, grads[n]), ((0, 0), (0, (-grads[n].size) % LANES)))
                             for n in SMALL_VECTORS], axis=1)
    gathered = _all_gather([vec_g] + list(_sum_sources(recv_mats, "sum_gate_matrix_grads")), "gather_small_grads")
    vec_parts = gathered[0].reshape(N_DEV, -1)
    mat_g = [g.reshape(-1, LANES) for g in gathered[1:]]

    recv = dict(zip(EXCHANGED_IN_SSD + EXCHANGED_IN_LRU + NEEDED_FIRST,
                    list(recv_in_ssd) + list(recv_in_lru) + list(recv_first)))
    big_out = {n: _adamw(local(n, weights[n]), recv[n], local(n, given["m_" + n]), local(n, given["v_" + n]),
                         "adamw_" + n) for n in BIG_NAMES}

    def state(n):
        return tuple(as_rows(n, given[p + n]) for p in ("", "m_", "v_"))

    vec_out, mat_out = _adamw_small(vec_parts, [state(n) for n in SMALL_VECTORS],
                                    mat_g, [state(n) for n in SMALL_MATRICES])
    small_out = dict(zip(SMALL_VECTORS, vec_out))
    small_out.update({n: (g,) + out for n, g, out in zip(SMALL_MATRICES, mat_g, mat_out)})

    order = list(given)[1:24]
    results = []
    for q in range(4):
        vals = {n: as_output(n, big_out[n][q]) for n in BIG_NAMES}
        vals.update({n: out[q].reshape(weights[n].shape) for n, out in small_out.items()})
        results.extend(vals[n] for n in order)
    return (loss, grad_x[None], *results)
```

```python
import math

import jax
import jax.numpy as jnp
from jax import lax
from jax.experimental import pallas as pl
from jax.experimental.pallas import tpu as pltpu

F32 = jnp.float32
BF16 = jnp.bfloat16
HIGHEST = lax.Precision.HIGHEST
MESH = pl.DeviceIdType.MESH
AXES = ("x", "y", "c")
N_DEV = 8

D_MODEL = 1024
SSM_INNER = 2048
SSM_HEADS = 32
SSM_HEAD_DIM = 64
SSM_GROUPS = 4
SSM_STATE = 128
SSM_BC = SSM_GROUPS * SSM_STATE
SSM_CONV_DIM = SSM_INNER + 2 * SSM_BC
SSM_CHUNK = 128
SSM_PAIRS = SSM_HEADS // 2
CONV_K = 4
LRU_WIDTH = 1280
LRU_BLOCKS = 10
LRU_C = 8.0
FFN_HIDDEN = 2816
RMS_EPS = 1e-6
IN_PROJ = 9760

COL_GATES = 0
COL_Z = 2048
COL_XBC = 4096
COL_LRU = 7168
COL_DT = 9728
PROJ_W = 9856
ORIG_DT = 7168
ORIG_LRU_X = 7200
ORIG_LRU_Y = 8480

ADAM_LR = 0.001
ADAM_B1 = 0.9
ADAM_B2 = 0.999
ADAM_EPS = 1e-08
ADAM_WD = 0.01
ADAM_STEP = 10

LANES = 128
SUBLANES = 8
V7X_VMEM_BYTES = 64 * 1024 * 1024
VMEM_LIMIT = V7X_VMEM_BYTES * 3 // 4
VMEM_LIMIT_BIG = V7X_VMEM_BYTES * 15 // 16

NT = (((1,), (1,)), ((), ()))
TN = (((0,), (0,)), ((), ()))


def _params(sem=None, big=False):
    return pltpu.CompilerParams(dimension_semantics=sem,
                                vmem_limit_bytes=VMEM_LIMIT_BIG if big else VMEM_LIMIT)


def _blk(dim, cap):
    if dim <= cap:
        return dim
    for m in range(cap // LANES, 0, -1):
        if dim % (m * LANES) == 0:
            return m * LANES
    raise ValueError(f"no block for {dim}")


def _rows(t):
    return min(t, 256)


def _sigmoid(v):
    return 1.0 / (1.0 + jnp.exp(-v))


def _softplus(v):
    e = jnp.exp(-jnp.abs(v))
    u = 1.0 + e
    log1p = jnp.where(u == 1.0, e, jnp.log(u) * e / jnp.where(u == 1.0, 1.0, u - 1.0))
    return jnp.maximum(v, 0.0) + log1p


def _iota(shape, dim):
    return lax.broadcasted_iota(jnp.int32, shape, dim)


def _shift_down(v, s):
    if s == 0:
        return v
    return jnp.where(_iota(v.shape, 0) >= s, pltpu.roll(v, s, 0), 0.0)


def _shift_up(v, s):
    if s == 0:
        return v
    n = v.shape[0]
    return jnp.where(_iota(v.shape, 0) < n - s, pltpu.roll(v, n - s, 0), 0.0)


def _bdot(a, b, dn=None):
    a = a.astype(BF16)
    b = b.astype(BF16)
    if dn is None:
        return jnp.dot(a, b, preferred_element_type=F32)
    return lax.dot_general(a, b, dn, preferred_element_type=F32)


def _split_dot(a, e, dn=None):
    hi = a.astype(BF16)
    lo = (a - hi.astype(F32)).astype(BF16)
    return _bdot(hi, e, dn) + _bdot(lo, e, dn)


def _fdot(a, b, dn=None):
    if dn is None:
        return jnp.dot(a, b, precision=HIGHEST, preferred_element_type=F32)
    return lax.dot_general(a, b, dn, precision=HIGHEST, preferred_element_type=F32)


def _mm(a, b, *, ta=False, tb=False, add=None, hosted=(), out_dtype=F32, epilogue=None, name):
    if ta:
        kdim, m = a.shape
    else:
        m, kdim = a.shape
    if tb:
        n, k2 = b.shape
    else:
        k2, n = b.shape
    assert kdim == k2, (a.shape, b.shape, ta, tb)
    if epilogue is None:
        rows, vecs, row_dtypes, n_vec_out = ([] if add is None else [add]), [], [out_dtype], 0

        def finish(r, row_vals, vec_vals):
            return ((r + row_vals[0]) if row_vals else r,), ()
    else:
        assert add is None
        finish, rows, vecs, row_dtypes, n_vec_out = epilogue
    bm, bn, bk = _blk(m, 1408 if epilogue is None else 512), _blk(n, 1408), _blk(kdim, 1408)
    grid = (m // bm, n // bn, kdim // bk)
    nk = grid[2]
    assert n_vec_out == 0 or grid[1] == 1, "column sums are accumulated over the row tiles of whole rows"
    dn = (((0 if ta else 1,), (1 if tb else 0,)), ((), ()))
    n_in = 2 + len(rows) + len(vecs)
    n_out = len(row_dtypes) + n_vec_out
    sizes = [len(arrays) for _, arrays, _ in hosted]
    n_ex = sum(sizes)

    def body(*refs):
        a_ref, b_ref = refs[:2]
        row_refs, vec_refs = refs[2:2 + len(rows)], refs[2 + len(rows):n_in]
        out_refs = refs[n_in + n_ex:n_in + n_ex + n_out]
        acc = refs[n_in + 2 * n_ex + n_out]
        comms, at = [], 0
        for g, size in enumerate(sizes):
            sems = refs[n_in + 2 * n_ex + n_out + 1 + 3 * g:n_in + 2 * n_ex + n_out + 4 + 3 * g]
            comms.append((refs[n_in + at:n_in + at + size],
                          refs[n_in + n_ex + n_out + at:n_in + n_ex + n_out + at + size]) + tuple(sems))
            at += size
        step = (pl.program_id(0) * grid[1] + pl.program_id(1)) * nk + pl.program_id(2)
        k = pl.program_id(2)
        if n_ex:
            @pl.when(step == 0)
            def _():
                for (phase_fn, _, _), comm in zip(hosted, comms):
                    phase_fn("start", *comm)

        @pl.when(k == 0)
        def _():
            acc[...] = jnp.zeros_like(acc)

        acc[...] += lax.dot_general(a_ref[...].astype(BF16), b_ref[...].astype(BF16), dn,
                                    preferred_element_type=F32)

        @pl.when(k == nk - 1)
        def _():
            row_outs, col_sums = finish(acc[...], [r[...] for r in row_refs], [v[...] for v in vec_refs])
            for o_ref, val in zip(out_refs, row_outs):
                o_ref[...] = val.astype(o_ref.dtype)
            for o_ref, val in zip(out_refs[len(row_dtypes):], col_sums):
                @pl.when(pl.program_id(0) == 0)
                def _():
                    o_ref[...] = val

                @pl.when(pl.program_id(0) > 0)
                def _():
                    o_ref[...] += val

        if n_ex:
            @pl.when(step == grid[0] * grid[1] * nk - 1)
            def _():
                for (phase_fn, _, _), comm in zip(hosted, comms):
                    phase_fn("finish", *comm)

    a_spec = pl.BlockSpec((bk, bm), lambda i, j, k: (k, i)) if ta else pl.BlockSpec((bm, bk), lambda i, j, k: (i, k))
    b_spec = pl.BlockSpec((bn, bk), lambda i, j, k: (j, k)) if tb else pl.BlockSpec((bk, bn), lambda i, j, k: (k, j))
    o_spec = pl.BlockSpec((bm, bn), lambda i, j, k: (i, j))
    v_spec = pl.BlockSpec((1, bn), lambda i, j, k: (0, j))
    in_specs = [a_spec, b_spec] + [o_spec] * len(rows) + [v_spec] * len(vecs) + [HBM] * n_ex
    args = [a, b] + list(rows) + list(vecs) + [p for _, arrays, _ in hosted for p in arrays]
    sequential = n_ex or n_vec_out
    out = pl.pallas_call(
        body, name=name, grid=grid,
        in_specs=in_specs, out_specs=[o_spec] * len(row_dtypes) + [v_spec] * n_vec_out + [HBM] * n_ex,
        out_shape=[jax.ShapeDtypeStruct((m, n), dt) for dt in row_dtypes]
        + [jax.ShapeDtypeStruct((1, n), F32)] * n_vec_out
        + [jax.ShapeDtypeStruct(r.shape, r.dtype) for _, _, results in hosted for r in results],
        scratch_shapes=[pltpu.VMEM((bm, bn), F32)] + [s for size in sizes for s in _comm_scratch(size)],
        compiler_params=_params(("arbitrary",) * 3 if sequential else ("parallel", "parallel", "arbitrary")),
    )(*args)
    result = out[0] if n_out == 1 else tuple(out[:n_out])
    if not n_ex:
        return result
    received, at = [], n_out
    for size in sizes:
        received.append(out[at:at + size])
        at += size
    return result, received


def _rmsnorm_fwd(x, w, name):
    t, d = x.shape
    tr = _rows(t)

    def body(x_ref, w_ref, o_ref):
        xv = x_ref[...]
        rstd = lax.rsqrt(jnp.mean(xv * xv, axis=-1, keepdims=True) + RMS_EPS)
        o_ref[...] = (xv * rstd * w_ref[...]).astype(BF16)

    return pl.pallas_call(
        body, name=name, grid=(t // tr,),
        in_specs=[pl.BlockSpec((tr, d), lambda i: (i, 0)), pl.BlockSpec((1, d), lambda i: (0, 0))],
        out_specs=pl.BlockSpec((tr, d), lambda i: (i, 0)),
        out_shape=jax.ShapeDtypeStruct((t, d), BF16),
        compiler_params=_params(("parallel",)),
    )(x, w)


def _normalize(h):
    rstd = lax.rsqrt(jnp.mean(h * h, axis=-1, keepdims=True) + RMS_EPS)
    return rstd, h * rstd


def _residual_norm_epilogue(x, w):
    def finish(r, rows, vecs):
        h = r + rows[0]
        return (h, _normalize(h)[1] * vecs[0]), ()

    return finish, [x], [w], [F32, BF16], 0


def _norm_bwd_epilogue(x, w, dres):
    def finish(r, rows, vecs):
        rstd, xhat = _normalize(rows[0])
        dxhat = r * vecs[0]
        m = jnp.mean(dxhat * xhat, axis=-1, keepdims=True)
        return (rstd * (dxhat - xhat * m) + rows[1],), (jnp.sum(r * xhat, axis=0, keepdims=True),)

    return finish, [x, dres], [w], [F32], 1


def _loss_epilogue(h1, w, tgt):
    d = h1.shape[1]

    def finish(r, rows, vecs):
        rstd, xhat = _normalize(r + rows[0])
        err = xhat * vecs[0] - rows[1]
        dyv = err * (1.0 / d)
        dxhat = dyv * vecs[0]
        m = jnp.mean(dxhat * xhat, axis=-1, keepdims=True)
        return ((rstd * (dxhat - xhat * m),),
                (jnp.sum(dyv * xhat, axis=0, keepdims=True), jnp.sum(err * err, axis=0, keepdims=True)))

    return finish, [h1, tgt], [w], [F32], 2


def _merge_fwd(proj, bg, ys, yl):
    t = proj.shape[0]
    d = D_MODEL
    tr = _rows(t)

    def body(ps_ref, pl_ref, bg_ref, ys_ref, yl_ref, o_ref):
        gs = _sigmoid(ps_ref[...] + bg_ref[:, 0:d])
        gl = _sigmoid(pl_ref[...] + bg_ref[:, d:2 * d])
        o_ref[...] = (gs * ys_ref[...] + gl * yl_ref[...]).astype(BF16)

    row = pl.BlockSpec((tr, d), lambda i: (i, 0))
    return pl.pallas_call(
        body, name="merge_fwd", grid=(t // tr,),
        in_specs=[row, pl.BlockSpec((tr, d), lambda i: (i, 1)), pl.BlockSpec((1, 2 * d), lambda i: (0, 0)), row, row],
        out_specs=row, out_shape=jax.ShapeDtypeStruct((t, d), BF16),
        compiler_params=_params(("parallel",)),
    )(proj, proj, bg, ys, yl)


def _merge_bwd(proj, bg, ys, yl, dm):
    t = proj.shape[0]
    d = D_MODEL
    tr = _rows(t)

    def body(ps_ref, pl_ref, bg_ref, ys_ref, yl_ref, dm_ref, dys_ref, dyl_ref, dg_ref, dbg_ref):
        i = pl.program_id(0)
        gs = _sigmoid(ps_ref[...] + bg_ref[:, 0:d])
        gl = _sigmoid(pl_ref[...] + bg_ref[:, d:2 * d])
        dmv = dm_ref[...]
        dys_ref[...] = (dmv * gs).astype(BF16)
        dyl_ref[...] = (dmv * gl).astype(BF16)
        dgs = dmv * ys_ref[...] * gs * (1.0 - gs)
        dgl = dmv * yl_ref[...] * gl * (1.0 - gl)
        dg_ref[:, 0:d] = dgs.astype(BF16)
        dg_ref[:, d:2 * d] = dgl.astype(BF16)

        @pl.when(i == 0)
        def _():
            dbg_ref[...] = jnp.zeros_like(dbg_ref)

        dbg_ref[:, 0:d] += jnp.sum(dgs, axis=0, keepdims=True)
        dbg_ref[:, d:2 * d] += jnp.sum(dgl, axis=0, keepdims=True)

    row = pl.BlockSpec((tr, d), lambda i: (i, 0))
    wide = pl.BlockSpec((tr, 2 * d), lambda i: (i, 0))
    vec = pl.BlockSpec((1, 2 * d), lambda i: (0, 0))
    return pl.pallas_call(
        body, name="merge_bwd", grid=(t // tr,),
        in_specs=[row, pl.BlockSpec((tr, d), lambda i: (i, 1)), vec, row, row, row],
        out_specs=[row, row, wide, vec],
        out_shape=[jax.ShapeDtypeStruct((t, d), BF16), jax.ShapeDtypeStruct((t, d), BF16),
                   jax.ShapeDtypeStruct((t, PROJ_W), BF16), jax.ShapeDtypeStruct((1, 2 * d), F32)],
        compiler_params=_params(("arbitrary",)),
    )(proj, proj, bg, ys, yl, dm)


def _swiglu_fwd(gu):
    t = gu.shape[0]
    f = FFN_HIDDEN
    tr = _rows(t)

    def body(g_ref, u_ref, o_ref):
        g = g_ref[...]
        o_ref[...] = (g * _sigmoid(g) * u_ref[...]).astype(BF16)

    return pl.pallas_call(
        body, name="swiglu_fwd", grid=(t // tr,),
        in_specs=[pl.BlockSpec((tr, f), lambda i: (i, 0)), pl.BlockSpec((tr, f), lambda i: (i, 1))],
        out_specs=pl.BlockSpec((tr, f), lambda i: (i, 0)),
        out_shape=jax.ShapeDtypeStruct((t, f), BF16),
        compiler_params=_params(("parallel",)),
    )(gu, gu)


def _swiglu_bwd(gu, dact):
    t = gu.shape[0]
    f = FFN_HIDDEN
    tr = _rows(t)

    def body(g_ref, u_ref, da_ref, o_ref):
        g = g_ref[...]
        sg = _sigmoid(g)
        da = da_ref[...]
        o_ref[:, 0:f] = (da * u_ref[...] * (sg * (1.0 + g * (1.0 - sg)))).astype(BF16)
        o_ref[:, f:2 * f] = (da * g * sg).astype(BF16)

    return pl.pallas_call(
        body, name="swiglu_bwd", grid=(t // tr,),
        in_specs=[pl.BlockSpec((tr, f), lambda i: (i, 0)), pl.BlockSpec((tr, f), lambda i: (i, 1)),
                  pl.BlockSpec((tr, f), lambda i: (i, 0))],
        out_specs=pl.BlockSpec((tr, 2 * f), lambda i: (i, 0)),
        out_shape=jax.ShapeDtypeStruct((t, 2 * f), BF16),
        compiler_params=_params(("parallel",)),
    )(gu, gu, dact)


def _conv_pre(xv, wv, bv):
    pre = bv + wv[CONV_K - 1:CONV_K, :] * xv
    for k in range(CONV_K - 1):
        pre = pre + wv[k:k + 1, :] * _shift_down(xv, CONV_K - 1 - k)
    return pre


def _ssm_conv_fwd(proj, w, b):
    t = proj.shape[0]
    nb = SSM_CONV_DIM // LANES
    c0 = COL_XBC // LANES

    def body(x_ref, w_ref, b_ref, o_ref):
        pre = _conv_pre(x_ref[...], w_ref[...], b_ref[...])
        o_ref[...] = pre * _sigmoid(pre)

    return pl.pallas_call(
        body, name="ssm_conv_fwd", grid=(nb,),
        in_specs=[pl.BlockSpec((t, LANES), lambda j: (0, c0 + j)), pl.BlockSpec((CONV_K, LANES), lambda j: (0, j)),
                  pl.BlockSpec((1, LANES), lambda j: (0, j))],
        out_specs=pl.BlockSpec((t, LANES), lambda j: (0, j)),
        out_shape=jax.ShapeDtypeStruct((t, SSM_CONV_DIM), F32),
        compiler_params=_params(("parallel",)),
    )(proj, w, b)


def _ssm_conv_bwd(proj, w, b, dact, dproj):
    t = proj.shape[0]
    nb = SSM_CONV_DIM // LANES
    c0 = COL_XBC // LANES

    def body(x_ref, w_ref, b_ref, da_ref, dproj_in, dx_ref, dw_ref, db_ref):
        xv = x_ref[...]
        wv = w_ref[...]
        pre = _conv_pre(xv, wv, b_ref[...])
        sg = _sigmoid(pre)
        dpre = da_ref[...] * (sg * (1.0 + pre * (1.0 - sg)))
        dx = wv[CONV_K - 1:CONV_K, :] * dpre
        for k in range(CONV_K - 1):
            dx = dx + wv[k:k + 1, :] * _shift_up(dpre, CONV_K - 1 - k)
        dx_ref[...] = dx.astype(BF16)
        for k in range(CONV_K):
            dw_ref[k:k + 1, :] = jnp.sum(dpre * _shift_down(xv, CONV_K - 1 - k), axis=0, keepdims=True)
        db_ref[...] = jnp.sum(dpre, axis=0, keepdims=True)

    col = pl.BlockSpec((t, LANES), lambda j: (0, j))
    wsp = pl.BlockSpec((CONV_K, LANES), lambda j: (0, j))
    bsp = pl.BlockSpec((1, LANES), lambda j: (0, j))
    return pl.pallas_call(
        body, name="ssm_conv_bwd", grid=(nb,),
        in_specs=[pl.BlockSpec((t, LANES), lambda j: (0, c0 + j)), wsp, bsp, col, HBM],
        out_specs=[pl.BlockSpec((t, LANES), lambda j: (0, c0 + j)), wsp, bsp],
        out_shape=[jax.ShapeDtypeStruct(dproj.shape, dproj.dtype), jax.ShapeDtypeStruct((CONV_K, SSM_CONV_DIM), F32),
                   jax.ShapeDtypeStruct((1, SSM_CONV_DIM), F32)],
        input_output_aliases={4: 0},
        compiler_params=_params(("parallel",)),
    )(proj, w, b, dact, dproj)


def _ssd_chunk_terms(dtr, bias, alog):
    a = -jnp.exp(alog)
    dt = _softplus(dtr + bias)
    row = _iota((SSM_CHUNK, SSM_CHUNK), 0)
    col = _iota((SSM_CHUNK, SSM_CHUNK), 1)
    tri = (row >= col).astype(F32)
    cs = _fdot(tri, dt * a)
    dec = jnp.exp(cs[SSM_CHUNK - 1:SSM_CHUNK, :] - cs)
    ecs = jnp.exp(cs)
    off = _iota((LANES, SSM_INNER), 1) - SSM_HEAD_DIM * _iota((LANES, SSM_INNER), 0)
    expand = jnp.where(jnp.logical_and(off >= 0, off < SSM_HEAD_DIM), 1.0, 0.0).astype(BF16)
    return a, dt, cs, dec, ecs, expand, row, col


def _ssd_specs(t):
    nc = t // SSM_CHUNK
    xs = pl.BlockSpec((SSM_CHUNK, SSM_INNER), lambda c: (c, 0))
    bm = pl.BlockSpec((SSM_CHUNK, SSM_BC), lambda c: (c, SSM_INNER // SSM_BC))
    cm = pl.BlockSpec((SSM_CHUNK, SSM_BC), lambda c: (c, SSM_INNER // SSM_BC + 1))
    dtr = pl.BlockSpec((SSM_CHUNK, LANES), lambda c: (c, COL_DT // LANES))
    vec = pl.BlockSpec((1, LANES), lambda c: (0, 0))
    wide = pl.BlockSpec((1, SSM_INNER), lambda c: (0, 0))
    return nc, xs, bm, cm, dtr, vec, wide


def _ssd_fwd(xbc_act, proj, bias, alog, dexp, gather):
    t = proj.shape[0]
    nc, xs_s, bm_s, cm_s, dtr_s, vec, wide = _ssd_specs(t)
    n = len(gather)

    def body(*refs):
        xs_ref, b_ref, c_ref, dtr_ref, bias_ref, alog_ref, dexp_ref = refs[:7]
        y_ref, sin_ref = refs[7 + n:9 + n]
        state = refs[9 + 2 * n]
        comm = (refs[7:7 + n], refs[9 + n:9 + 2 * n]) + tuple(refs[10 + 2 * n:])
        chunk = pl.program_id(0)

        @pl.when(chunk == 0)
        def _():
            _gather_phase("start", *comm)
            state[...] = jnp.zeros_like(state)

        @pl.when(chunk == (3 * nc) // 4)
        def _():
            _gather_phase("forward", *comm)

        a, dt, cs, dec, ecs, expand, row, col = _ssd_chunk_terms(dtr_ref[...], bias_ref[...], alog_ref[...])
        cst = cs.T
        dt_x = _split_dot(dt, expand)
        dec_x = _split_dot(dec, expand)
        ecs_x = _split_dot(ecs, expand)
        xs = xs_ref[...]
        xdt = xs * dt_x
        xdec = xdt * dec_x
        lane_lo = col < SSM_HEAD_DIM
        causal = row >= col
        sin_ref[0] = state[...]
        for g in range(SSM_GROUPS):
            bg = b_ref[:, g * SSM_STATE:(g + 1) * SSM_STATE].astype(BF16)
            cg = c_ref[:, g * SSM_STATE:(g + 1) * SSM_STATE].astype(BF16)
            cb = _bdot(cg, bg, NT)
            for q in range(SSM_PAIRS // SSM_GROUPS):
                pq = g * (SSM_PAIRS // SSM_GROUPS) + q
                sl = slice(pq * LANES, (pq + 1) * LANES)
                xp = xdt[:, sl].astype(BF16)
                yd = []
                for hh in range(2):
                    h = 2 * pq + hh
                    lmat = jnp.exp(jnp.where(causal, cs[:, h:h + 1] - cst[h:h + 1, :], -jnp.inf))
                    yd.append(_bdot(cb * lmat, xp))
                s_in = state[pq]
                y_off = _bdot(cg, s_in) * ecs_x[:, sl]
                y_ref[:, sl] = jnp.where(lane_lo, yd[0], yd[1]) + y_off + xs[:, sl] * dexp_ref[:, sl]
                state[pq] = s_in * ecs_x[SSM_CHUNK - 1:SSM_CHUNK, sl] + _bdot(bg, xdec[:, sl], TN)

        @pl.when(chunk == nc - 1)
        def _():
            _gather_phase("finish", *comm)

    out = pl.pallas_call(
        body, name="ssd_fwd", grid=(nc,),
        in_specs=[xs_s, bm_s, cm_s, dtr_s, vec, vec, wide] + [HBM] * n,
        out_specs=[pl.BlockSpec((SSM_CHUNK, SSM_INNER), lambda c: (c, 0)),
                   pl.BlockSpec((1, SSM_PAIRS, SSM_STATE, LANES), lambda c: (c, 0, 0, 0))] + [HBM] * n,
        out_shape=[jax.ShapeDtypeStruct((t, SSM_INNER), F32),
                   jax.ShapeDtypeStruct((nc, SSM_PAIRS, SSM_STATE, LANES), F32)]
        + [jax.ShapeDtypeStruct((N_DEV,) + v.shape, v.dtype) for v in gather],
        scratch_shapes=[pltpu.VMEM((SSM_PAIRS, SSM_STATE, LANES), F32)] + _comm_scratch(n),
        compiler_params=_params(("arbitrary",)),
    )(xbc_act, xbc_act, xbc_act, proj, bias, alog, dexp, *gather)
    return out[:2], out[2:]


def _ssd_bwd(xbc_act, proj, s_in_all, dy, bias, alog, dexp, dproj, exchange):
    n_ex = len(exchange)
    t = proj.shape[0]
    nc = t // SSM_CHUNK
    last = nc - 1
    xs_s = pl.BlockSpec((SSM_CHUNK, SSM_INNER), lambda c: (last - c, 0))
    bm_s = pl.BlockSpec((SSM_CHUNK, SSM_BC), lambda c: (last - c, SSM_INNER // SSM_BC))
    cm_s = pl.BlockSpec((SSM_CHUNK, SSM_BC), lambda c: (last - c, SSM_INNER // SSM_BC + 1))
    dtr_s = pl.BlockSpec((SSM_CHUNK, LANES), lambda c: (last - c, COL_DT // LANES))
    sin_s = pl.BlockSpec((1, SSM_PAIRS, SSM_STATE, LANES), lambda c: (last - c, 0, 0, 0))
    vec = pl.BlockSpec((1, LANES), lambda c: (0, 0))
    wide = pl.BlockSpec((1, SSM_INNER), lambda c: (0, 0))

    def body(*refs):
        xs_ref, b_ref, c_ref, dtr_ref, sin_ref, dy_ref, bias_ref, alog_ref, dexp_ref = refs[:9]
        dxbc_ref, ddtr_ref, dbias_ref, dalog_ref, ddcol_ref = refs[10 + n_ex:15 + n_ex]
        dstate, dxdt_s, yoff_s, rx_s, trow_s = refs[15 + 2 * n_ex:20 + 2 * n_ex]
        comm = (refs[10:10 + n_ex], refs[15 + n_ex:15 + 2 * n_ex]) + tuple(refs[20 + 2 * n_ex:])

        @pl.when(pl.program_id(0) == 0)
        def _():
            _exchange_phase("start", *comm)
            dstate[...] = jnp.zeros_like(dstate)
            trow_s[...] = jnp.zeros_like(trow_s)
            dbias_ref[...] = jnp.zeros_like(dbias_ref)
            dalog_ref[...] = jnp.zeros_like(dalog_ref)
            ddcol_ref[...] = jnp.zeros_like(ddcol_ref)

        dtr = dtr_ref[...]
        a, dt, cs, dec, ecs, expand, row, col = _ssd_chunk_terms(dtr, bias_ref[...], alog_ref[...])
        cst = cs.T
        dt_x = _split_dot(dt, expand)
        dec_x = _split_dot(dec, expand)
        ecs_x = _split_dot(ecs, expand)
        xs = xs_ref[...]
        dyv = dy_ref[...]
        xdt = xs * dt_x
        lane_lo = col < SSM_HEAD_DIM
        causal = row >= col
        ddcol_ref[...] += jnp.sum(dyv * xs, axis=0, keepdims=True)
        dcs_col = jnp.zeros((SSM_CHUNK, LANES), F32)
        dcs_row = jnp.zeros((LANES, SSM_CHUNK), F32)
        for g in range(SSM_GROUPS):
            bg = b_ref[:, g * SSM_STATE:(g + 1) * SSM_STATE].astype(BF16)
            cg = c_ref[:, g * SSM_STATE:(g + 1) * SSM_STATE].astype(BF16)
            cb = _bdot(cg, bg, NT)
            dgm = jnp.zeros((SSM_CHUNK, SSM_CHUNK), F32)
            dbg = jnp.zeros((SSM_CHUNK, SSM_STATE), F32)
            dcg = jnp.zeros((SSM_CHUNK, SSM_STATE), F32)
            for q in range(SSM_PAIRS // SSM_GROUPS):
                pq = g * (SSM_PAIRS // SSM_GROUPS) + q
                sl = slice(pq * LANES, (pq + 1) * LANES)
                dyp = dyv[:, sl]
                xp = xdt[:, sl]
                dxh = []
                for hh in range(2):
                    h = 2 * pq + hh
                    lmat = jnp.exp(jnp.where(causal, cs[:, h:h + 1] - cst[h:h + 1, :], -jnp.inf))
                    mmat = cb * lmat
                    dyh = jnp.where(lane_lo if hh == 0 else jnp.logical_not(lane_lo), dyp, 0.0)
                    dmm = _bdot(dyh, xp, NT)
                    pm = dmm * mmat
                    dcs_col = jnp.where(col == h, jnp.sum(pm, axis=1, keepdims=True), dcs_col)
                    dcs_row = jnp.where(row == h, jnp.sum(pm, axis=0, keepdims=True), dcs_row)
                    dgm = dgm + dmm * lmat
                    dxh.append(_bdot(mmat, dyp, TN))
                s_in = sin_ref[0, pq]
                ecs_p = ecs_x[:, sl]
                dec_p = dec_x[:, sl]
                etot_p = ecs_x[SSM_CHUNK - 1:SSM_CHUNK, sl]
                yoff_s[:, sl] = dyp * (_bdot(cg, s_in) * ecs_p)
                dq = dyp * ecs_p
                dcg = dcg + _bdot(dq, s_in, NT)
                ds = dstate[pq]
                r = _bdot(bg, ds)
                rx_s[:, sl] = r * xp
                dxdt_s[:, sl] = jnp.where(lane_lo, dxh[0], dxh[1]) + dec_p * r
                dbg = dbg + _bdot(xp * dec_p, ds, NT)
                trow_s[0:1, sl] = jnp.sum(ds * s_in, axis=0, keepdims=True) * etot_p
                dstate[pq] = etot_p * ds + _bdot(cg, dq, TN)
            dcg = dcg + _bdot(dgm, bg)
            dbg = dbg + _bdot(dgm, cg, TN)
            dxbc_ref[:, SSM_INNER + g * SSM_STATE:SSM_INNER + (g + 1) * SSM_STATE] = dbg
            dxbc_ref[:, SSM_INNER + SSM_BC + g * SSM_STATE:SSM_INNER + SSM_BC + (g + 1) * SSM_STATE] = dcg
        ddec = _split_dot(rx_s[...], expand, NT) * dec
        dtot = _split_dot(trow_s[...], expand, NT)[0:1, :]
        dcs = dcs_col - dcs_row.T + _split_dot(yoff_s[...], expand, NT) - ddec
        dcs = dcs + jnp.where(row == SSM_CHUNK - 1, jnp.sum(ddec, axis=0, keepdims=True) + dtot, 0.0)
        da = _fdot((row <= col).astype(F32), dcs)
        dxdt = dxdt_s[...]
        ddt = da * a + _split_dot(dxdt * xs, expand, NT)
        dalog_ref[...] += jnp.sum(da * dt, axis=0, keepdims=True) * a
        ddtr = ddt * _sigmoid(dtr + bias_ref[...])
        ddtr_ref[...] = ddtr.astype(BF16)
        dbias_ref[...] += jnp.sum(ddtr, axis=0, keepdims=True)
        dxbc_ref[:, 0:SSM_INNER] = dxdt * dt_x + dyv * dexp_ref[...]

        @pl.when(pl.program_id(0) == last)
        def _():
            _exchange_phase("finish", *comm)

    out = pl.pallas_call(
        body, name="ssd_bwd", grid=(nc,),
        in_specs=[xs_s, bm_s, cm_s, dtr_s, sin_s, pl.BlockSpec((SSM_CHUNK, SSM_INNER), lambda c: (last - c, 0)),
                  vec, vec, wide, HBM] + [HBM] * n_ex,
        out_specs=[pl.BlockSpec((SSM_CHUNK, SSM_CONV_DIM), lambda c: (last - c, 0)), dtr_s, vec, vec, wide]
        + [HBM] * n_ex,
        out_shape=[jax.ShapeDtypeStruct((t, SSM_CONV_DIM), F32), jax.ShapeDtypeStruct(dproj.shape, dproj.dtype),
                   jax.ShapeDtypeStruct((1, LANES), F32), jax.ShapeDtypeStruct((1, LANES), F32),
                   jax.ShapeDtypeStruct((1, SSM_INNER), F32)]
        + [jax.ShapeDtypeStruct(p.shape, p.dtype) for p in exchange],
        input_output_aliases={9: 1},
        scratch_shapes=[pltpu.VMEM((SSM_PAIRS, SSM_STATE, LANES), F32),
                        pltpu.VMEM((SSM_CHUNK, SSM_INNER), F32), pltpu.VMEM((SSM_CHUNK, SSM_INNER), F32),
                        pltpu.VMEM((SSM_CHUNK, SSM_INNER), F32), pltpu.VMEM((SUBLANES, SSM_INNER), F32)]
        + _comm_scratch(n_ex),
        compiler_params=_params(("arbitrary",)),
    )(xbc_act, xbc_act, xbc_act, proj, s_in_all, dy, bias, alog, dexp, dproj, *exchange)
    return out[:5], out[5:]


def _group_rstd(y):
    n = SSM_INNER // SSM_GROUPS
    parts = []
    for g in range(SSM_GROUPS):
        yg = y[:, g * n:(g + 1) * n]
        r = lax.rsqrt(jnp.mean(yg * yg, axis=-1, keepdims=True) + RMS_EPS)
        parts.append(jnp.broadcast_to(r, yg.shape))
    return jnp.concatenate(parts, axis=1)


def _group_mean(v):
    n = SSM_INNER // SSM_GROUPS
    parts = []
    for g in range(SSM_GROUPS):
        vg = v[:, g * n:(g + 1) * n]
        parts.append(jnp.broadcast_to(jnp.mean(vg, axis=-1, keepdims=True), vg.shape))
    return jnp.concatenate(parts, axis=1)


def _ssm_post_fwd(y_ssd, proj, nw):
    t = proj.shape[0]
    n = SSM_INNER
    tr = _rows(t)

    def body(y_ref, z_ref, nw_ref, o_ref):
        z = z_ref[...]
        y = y_ref[...] * (z * _sigmoid(z))
        o_ref[...] = (y * _group_rstd(y) * nw_ref[...]).astype(BF16)

    row = pl.BlockSpec((tr, n), lambda i: (i, 0))
    return pl.pallas_call(
        body, name="ssm_post_fwd", grid=(t // tr,),
        in_specs=[row, pl.BlockSpec((tr, n), lambda i: (i, COL_Z // n)), pl.BlockSpec((1, n), lambda i: (0, 0))],
        out_specs=row, out_shape=jax.ShapeDtypeStruct((t, n), BF16),
        compiler_params=_params(("parallel",)),
    )(y_ssd, proj, nw)


def _ssm_post_bwd(y_ssd, proj, nw, dout, dproj):
    t = proj.shape[0]
    n = SSM_INNER
    tr = _rows(t)

    def body(y_ref, z_ref, nw_ref, do_ref, dproj_in, dy_ref, dz_ref, dnw_ref):
        i = pl.program_id(0)
        z = z_ref[...]
        sg = _sigmoid(z)
        sz = z * sg
        ys = y_ref[...]
        y = ys * sz
        rstd = _group_rstd(y)
        yn = y * rstd
        dov = do_ref[...]
        dyn = dov * nw_ref[...]
        dyg = rstd * (dyn - yn * _group_mean(dyn * yn))
        dy_ref[...] = dyg * sz
        dz_ref[...] = (dyg * ys * (sg * (1.0 + z * (1.0 - sg)))).astype(BF16)
        part = jnp.sum(dov * yn, axis=0, keepdims=True)

        @pl.when(i == 0)
        def _():
            dnw_ref[...] = part

        @pl.when(i > 0)
        def _():
            dnw_ref[...] += part

    row = pl.BlockSpec((tr, n), lambda i: (i, 0))
    vec = pl.BlockSpec((1, n), lambda i: (0, 0))
    return pl.pallas_call(
        body, name="ssm_post_bwd", grid=(t // tr,),
        in_specs=[row, pl.BlockSpec((tr, n), lambda i: (i, COL_Z // n)), vec, row, HBM],
        out_specs=[row, pl.BlockSpec((tr, n), lambda i: (i, COL_Z // n)), vec],
        out_shape=[jax.ShapeDtypeStruct((t, n), F32), jax.ShapeDtypeStruct(dproj.shape, dproj.dtype),
                   jax.ShapeDtypeStruct((1, n), F32)],
        input_output_aliases={4: 1},
        compiler_params=_params(("arbitrary",)),
    )(y_ssd, proj, nw, dout, dproj)


SCAN_UNROLL = 8
GELU_C = math.sqrt(2.0 / math.pi)
GELU_K = 0.044715


def _gelu_parts(y):
    th = jnp.tanh(GELU_C * (y + GELU_K * y * y * y))
    val = 0.5 * y * (1.0 + th)
    grad = 0.5 * (1.0 + th) + 0.5 * y * (1.0 - th * th) * GELU_C * (1.0 + 3.0 * GELU_K * y * y)
    return val, grad


def _scan_tiles(a_ref, b_ref, h_ref, n_rows, reverse):
    n_tiles = n_rows // SUBLANES
    shape = (SUBLANES, a_ref.shape[1])
    row = _iota(shape, 0)

    def in_tile(av, bv):
        for s in (1, 2, 4):
            if reverse:
                keep = row < SUBLANES - s
                a_sh = jnp.where(keep, pltpu.roll(av, SUBLANES - s, 0), 1.0)
                b_sh = jnp.where(keep, pltpu.roll(bv, SUBLANES - s, 0), 0.0)
            else:
                keep = row >= s
                a_sh = jnp.where(keep, pltpu.roll(av, s, 0), 1.0)
                b_sh = jnp.where(keep, pltpu.roll(bv, s, 0), 0.0)
            bv = av * b_sh + bv
            av = av * a_sh
        return av, bv

    def step(k, carry):
        first = (n_tiles // SCAN_UNROLL - 1 - k) if reverse else k
        tiles = [first * SCAN_UNROLL + j for j in range(SCAN_UNROLL)]
        if reverse:
            tiles = tiles[::-1]
        ats = [pl.ds(pl.multiple_of(tile * SUBLANES, SUBLANES), SUBLANES) for tile in tiles]
        scanned = [in_tile(a_ref[at, :], b_ref[at, :]) for at in ats]
        for at, (av, bv) in zip(ats, scanned):
            hv = bv + av * carry
            h_ref[at, :] = hv
            carry = hv[0:1, :] if reverse else hv[SUBLANES - 1:SUBLANES, :]
        return carry

    assert n_tiles % SCAN_UNROLL == 0, n_rows
    lax.fori_loop(0, n_tiles // SCAN_UNROLL, step, jnp.zeros((1, a_ref.shape[1]), F32))


def _lru_gates(xl, cw, cb, wr, br, wi, bi, lam):
    u = cb + cw[CONV_K - 1:CONV_K, :] * xl
    for k in range(CONV_K - 1):
        u = u + cw[k:k + 1, :] * _shift_down(xl, CONV_K - 1 - k)
    r = _sigmoid(_bdot(u, wr) + br)
    i = _sigmoid(_bdot(u, wi) + bi)
    sp = _softplus(-lam)
    la = -LRU_C * r * sp
    a = jnp.exp(la)
    mult = jnp.sqrt(-jnp.tanh(la) * (a * a + 1.0))
    return u, r, i, sp, a, mult


def _lru_specs(t):
    c0 = COL_LRU // LANES
    xl = pl.BlockSpec((t, LANES), lambda j: (0, c0 + 2 * j))
    yl = pl.BlockSpec((t, LANES), lambda j: (0, c0 + 2 * j + 1))
    col = pl.BlockSpec((t, LANES), lambda j: (0, j))
    cw = pl.BlockSpec((CONV_K, LANES), lambda j: (0, j))
    vec = pl.BlockSpec((1, LANES), lambda j: (0, j))
    wblk = pl.BlockSpec((1, LANES, LANES), lambda j: (j, 0, 0))
    return xl, yl, col, cw, vec, wblk


def _lru_fwd(proj, cw, cb, wr, br, wi, bi, lam, gather):
    t = proj.shape[0]
    xl_s, yl_s, col, cw_s, vec, wblk = _lru_specs(t)
    n = len(gather)

    def body(*refs):
        xl_ref, yl_ref, cw_ref, cb_ref, wr_ref, br_ref, wi_ref, bi_ref, lam_ref = refs[:9]
        o_ref, h_ref = refs[9 + n:11 + n]
        a_s, b_s = refs[11 + 2 * n:13 + 2 * n]
        comm = (refs[9:9 + n], refs[11 + n:11 + 2 * n]) + tuple(refs[13 + 2 * n:])
        j = pl.program_id(0)
        for step, phase in ((0, "start"), (LRU_BLOCKS - 2, "forward")):
            @pl.when(j == step)
            def _():
                _gather_phase(phase, *comm)

        u, r, i, sp, a, mult = _lru_gates(xl_ref[...], cw_ref[...], cb_ref[...], wr_ref[0], br_ref[...],
                                          wi_ref[0], bi_ref[...], lam_ref[...])
        a_s[...] = a
        b_s[...] = mult * (i * u)
        _scan_tiles(a_s, b_s, h_ref, t, reverse=False)
        o_ref[...] = (h_ref[...] * _gelu_parts(yl_ref[...])[0]).astype(BF16)

        @pl.when(j == LRU_BLOCKS - 1)
        def _():
            _gather_phase("finish", *comm)

    out = pl.pallas_call(
        body, name="lru_fwd", grid=(LRU_BLOCKS,),
        in_specs=[xl_s, yl_s, cw_s, vec, wblk, vec, wblk, vec, vec] + [HBM] * n,
        out_specs=[col, col] + [HBM] * n,
        out_shape=[jax.ShapeDtypeStruct((t, LRU_WIDTH), BF16), jax.ShapeDtypeStruct((t, LRU_WIDTH), F32)]
        + [jax.ShapeDtypeStruct((N_DEV,) + v.shape, v.dtype) for v in gather],
        scratch_shapes=[pltpu.VMEM((t, LANES), F32)] * 2 + _comm_scratch(n),
        compiler_params=_params(("arbitrary",), big=True),
    )(proj, proj, cw, cb, wr, br, wi, bi, lam, *gather)
    return out[:2], out[2:]


def _lru_bwd(proj, cw, cb, wr, br, wi, bi, lam, h_all, dout, dproj, exchange):
    t = proj.shape[0]
    xl_s, yl_s, col, cw_s, vec, wblk = _lru_specs(t)
    pair = pl.BlockSpec((t, 2 * LANES), lambda j: (0, COL_LRU // (2 * LANES) + j))
    n_ex = len(exchange)

    def body(*refs):
        xl_ref, yl_ref, cw_ref, cb_ref, wr_ref, br_ref, wi_ref, bi_ref, lam_ref, h_ref, do_ref = refs[:11]
        dxy_ref, dcw_ref, dcb_ref, dwr_ref, dbr_ref, dwi_ref, dbi_ref, dlam_ref = refs[12 + n_ex:20 + n_ex]
        a_s, b_s, g_s = refs[20 + 2 * n_ex:23 + 2 * n_ex]
        comm = (refs[12:12 + n_ex], refs[20 + n_ex:20 + 2 * n_ex]) + tuple(refs[23 + 2 * n_ex:])

        @pl.when(pl.program_id(0) == 0)
        def _():
            _exchange_phase("start", *comm)

        xl = xl_ref[...]
        cwv = cw_ref[...]
        lam = lam_ref[...]
        u, r, i, sp, a, mult = _lru_gates(xl, cwv, cb_ref[...], wr_ref[0], br_ref[...], wi_ref[0], bi_ref[...], lam)
        v = i * u
        gl, dgl = _gelu_parts(yl_ref[...])
        dov = do_ref[...]
        h = h_ref[...]
        dxy_ref[:, LANES:2 * LANES] = (dov * h * dgl).astype(BF16)
        b_s[...] = dov * gl
        a_s[...] = _shift_up(a, 1)
        _scan_tiles(a_s, b_s, g_s, t, reverse=True)
        g = g_s[...]
        da = g * _shift_down(h, 1)
        dmult = g * v
        dv = g * mult
        dla = da * a - dmult * (a * a) / mult
        dr = dla * (-LRU_C * sp)
        dsp = jnp.sum(dla * (-LRU_C * r), axis=0, keepdims=True)
        dlam_ref[...] = -dsp * _sigmoid(-lam)
        dpr = dr * r * (1.0 - r)
        dpi = dv * u * i * (1.0 - i)
        dbr_ref[...] = jnp.sum(dpr, axis=0, keepdims=True)
        dbi_ref[...] = jnp.sum(dpi, axis=0, keepdims=True)
        dwr_ref[0] = _bdot(u, dpr, TN)
        dwi_ref[0] = _bdot(u, dpi, TN)
        du = dv * i + _bdot(dpr, wr_ref[0], NT) + _bdot(dpi, wi_ref[0], NT)
        dxl = cwv[CONV_K - 1:CONV_K, :] * du
        for k in range(CONV_K - 1):
            dxl = dxl + cwv[k:k + 1, :] * _shift_up(du, CONV_K - 1 - k)
        dxy_ref[:, 0:LANES] = dxl.astype(BF16)
        for k in range(CONV_K):
            dcw_ref[k:k + 1, :] = jnp.sum(du * _shift_down(xl, CONV_K - 1 - k), axis=0, keepdims=True)
        dcb_ref[...] = jnp.sum(du, axis=0, keepdims=True)

        @pl.when(pl.program_id(0) == LRU_BLOCKS - 1)
        def _():
            _exchange_phase("finish", *comm)

    out = pl.pallas_call(
        body, name="lru_bwd", grid=(LRU_BLOCKS,),
        in_specs=[xl_s, yl_s, cw_s, vec, wblk, vec, wblk, vec, vec, col, col, HBM] + [HBM] * n_ex,
        out_specs=[pair, cw_s, vec, wblk, vec, wblk, vec, vec] + [HBM] * n_ex,
        input_output_aliases={11: 0},
        out_shape=[jax.ShapeDtypeStruct(dproj.shape, dproj.dtype),
                   jax.ShapeDtypeStruct((CONV_K, LRU_WIDTH), F32), jax.ShapeDtypeStruct((1, LRU_WIDTH), F32),
                   jax.ShapeDtypeStruct((LRU_BLOCKS, LANES, LANES), F32), jax.ShapeDtypeStruct((1, LRU_WIDTH), F32),
                   jax.ShapeDtypeStruct((LRU_BLOCKS, LANES, LANES), F32), jax.ShapeDtypeStruct((1, LRU_WIDTH), F32),
                   jax.ShapeDtypeStruct((1, LRU_WIDTH), F32)]
        + [jax.ShapeDtypeStruct(p.shape, p.dtype) for p in exchange],
        scratch_shapes=[pltpu.VMEM((t, LANES), F32)] * 3 + _comm_scratch(n_ex),
        compiler_params=_params(("arbitrary",), big=True),
    )(proj, proj, cw, cb, wr, br, wi, bi, lam, h_all, dout, dproj, *exchange)
    return out[:8], out[8:]


def _mesh_pos():
    return lax.axis_index("x"), lax.axis_index("y"), lax.axis_index("c")


HBM = pl.BlockSpec(memory_space=pl.ANY)


def _comm_scratch(n):
    return [pltpu.SemaphoreType.DMA((n, 7)), pltpu.SemaphoreType.DMA((n, 7)), pltpu.SemaphoreType.DMA((n,))]


def _gather_phase(phase, v_refs, out_refs, send_sems, recv_sems, local_sems):
    n = len(v_refs)
    x, y, c = _mesh_pos()
    me, sibling = (x, y, c), (x, y, 1 - c)
    chips = [(1 - x, y), (x, 1 - y), (1 - x, 1 - y)]

    def block(a, px, py, pc):
        return out_refs[a].at[4 * px + 2 * py + pc]

    def copy(a, k, blk, to, src=None):
        return pltpu.make_async_remote_copy(
            src_ref=block(a, *blk) if src is None else src, dst_ref=block(a, *blk),
            send_sem=send_sems.at[a, k], recv_sem=recv_sems.at[a, k], device_id=to, device_id_type=MESH)

    def own(a):
        return pltpu.make_async_copy(v_refs[a], block(a, *me), local_sems.at[a])

    def first(a):
        return ([copy(a, 0, me, sibling, src=v_refs[a])]
                + [copy(a, 1 + j, me, (*chip, c), src=v_refs[a]) for j, chip in enumerate(chips)])

    def forward(a, j):
        return copy(a, 4 + j, (*chips[j], c), sibling)

    if phase == "start":
        for a in range(n):
            own(a).start()
        for a in range(n):
            for cp in first(a):
                cp.start()
    elif phase == "forward":
        for j in range(3):
            for a in range(n):
                copy(a, 1 + j, (*chips[j], c), me).wait_recv()
                forward(a, j).start()
    else:
        for a in range(n):
            copy(a, 0, sibling, me).wait_recv()
            for j in range(3):
                copy(a, 4 + j, (*chips[j], 1 - c), me).wait_recv()
        for a in range(n):
            for cp in first(a) + [forward(a, j) for j in range(3)]:
                cp.wait_send()
            own(a).wait()


def _all_gather(vs, name):
    n = len(vs)

    def body(*refs):
        comm = (refs[:n], refs[n:2 * n]) + tuple(refs[2 * n:])
        for phase in ("start", "forward", "finish"):
            _gather_phase(phase, *comm)

    return pl.pallas_call(
        body, name=name,
        out_shape=[jax.ShapeDtypeStruct((N_DEV,) + v.shape, v.dtype) for v in vs],
        in_specs=[HBM] * n, out_specs=[HBM] * n, scratch_shapes=_comm_scratch(n),
    )(*vs)


def _run_copies(phase, local, remote):
    if phase == "start":
        for cp in local + remote:
            cp.start()
    else:
        for cp in remote:
            cp.wait()
        for cp in local:
            cp.wait()


def _exchange_phase(phase, p_refs, out_refs, send_sems, recv_sems, local_sems):
    n = len(p_refs)
    x, y, c = _mesh_pos()
    me = 4 * x + 2 * y + c
    local = [pltpu.make_async_copy(p_refs[a].at[me], out_refs[a].at[me], local_sems.at[a]) for a in range(n)]
    remote = []
    for k in range(1, N_DEV):
        px = (1 - x) if k & 4 else x
        py = (1 - y) if k & 2 else y
        pc = (1 - c) if k & 1 else c
        for a in range(n):
            remote.append(pltpu.make_async_remote_copy(
                src_ref=p_refs[a].at[4 * px + 2 * py + pc], dst_ref=out_refs[a].at[me],
                send_sem=send_sems.at[a, k - 1], recv_sem=recv_sems.at[a, k - 1],
                device_id=(px, py, pc), device_id_type=MESH))
    _run_copies(phase, local, remote)


def _chip_exchange_phase(phase, p_refs, out_refs, send_sems, recv_sems, local_sems):
    n = len(p_refs)
    x, y, c = _mesh_pos()
    me = 2 * x + y
    local = [pltpu.make_async_copy(p_refs[a].at[me], out_refs[a].at[me], local_sems.at[a]) for a in range(n)]
    remote = []
    for k in range(1, 4):
        px = (1 - x) if k & 2 else x
        py = (1 - y) if k & 1 else y
        for a in range(n):
            remote.append(pltpu.make_async_remote_copy(
                src_ref=p_refs[a].at[2 * px + py], dst_ref=out_refs[a].at[me],
                send_sem=send_sems.at[a, k - 1], recv_sem=recv_sems.at[a, k - 1],
                device_id=(px, py, c), device_id_type=MESH))
    _run_copies(phase, local, remote)


def _sibling_exchange(parts, name):
    chips = N_DEV // 2

    def body(p_ref, out_ref, send_sems, recv_sems):
        x, y, c = _mesh_pos()
        copies = [pltpu.make_async_remote_copy(
            src_ref=p_ref.at[2 * q + 1 - c], dst_ref=out_ref.at[q], send_sem=send_sems.at[q], recv_sem=recv_sems.at[q],
            device_id=(x, y, 1 - c), device_id_type=MESH) for q in range(chips)]
        _run_copies("start", [], copies)
        _run_copies("finish", [], copies)

    return pl.pallas_call(
        body, name=name, out_shape=jax.ShapeDtypeStruct((chips,) + parts.shape[1:], parts.dtype),
        in_specs=[HBM], out_specs=HBM,
        scratch_shapes=[pltpu.SemaphoreType.DMA((chips,)), pltpu.SemaphoreType.DMA((chips,))],
    )(parts)


def _pair_sum(mine, theirs, name):
    slots, rows, cols = mine.shape
    tc = 256

    def body(a_ref, b_ref, o_ref):
        o_ref[...] = (a_ref[...].astype(F32) + b_ref[...].astype(F32)).astype(o_ref.dtype)

    spec = pl.BlockSpec((1, rows, tc), lambda q, j: (q, 0, j))
    return pl.pallas_call(
        body, name=name, grid=(slots, cols // tc), in_specs=[spec, spec], out_specs=spec,
        out_shape=jax.ShapeDtypeStruct(mine.shape, mine.dtype),
        compiler_params=_params(("parallel", "parallel")),
    )(mine, theirs)


def _sum_sources(recvs, name):
    k = len(recvs)

    def body(*refs):
        for r_ref, o_ref in zip(refs[:k], refs[k:]):
            acc = r_ref[0].astype(F32)
            for s in range(1, r_ref.shape[0]):
                acc = acc + r_ref[s].astype(F32)
            o_ref[...] = acc

    return pl.pallas_call(
        body, name=name, out_shape=[jax.ShapeDtypeStruct(r.shape[1:], F32) for r in recvs],
        compiler_params=_params(),
    )(*recvs)


def _row_tile(rows):
    for tile in range(128, 15, -16):
        if rows % tile == 0:
            return tile
    return rows


def _adam_update(w, g, m, v):
    nm = ADAM_B1 * m + (1.0 - ADAM_B1) * g
    nv = ADAM_B2 * v + (1.0 - ADAM_B2) * (g * g)
    m_hat = nm / (1.0 - ADAM_B1 ** ADAM_STEP)
    v_hat = nv / (1.0 - ADAM_B2 ** ADAM_STEP)
    return -ADAM_LR * (m_hat / (jnp.sqrt(v_hat) + ADAM_EPS) + ADAM_WD * w), nm, nv


def _vector_offsets(widths):
    offsets, end = [], 0
    for c in widths:
        offsets.append(end)
        end += c + (-c) % LANES
    return offsets, end


def _adamw_small(vec_parts, vec_state, mat_grads, mat_state):
    widths = [w.shape[1] for w, _, _ in vec_state]
    offsets, total = _vector_offsets(widths)
    assert vec_parts.shape == (N_DEV, total), (vec_parts.shape, total)
    n_vec, n_mat = len(vec_state), len(mat_state)

    def body(*refs):
        r_ref = refs[0]
        vec_in = refs[1:1 + 3 * n_vec]
        mat_in = refs[1 + 3 * n_vec:1 + 3 * n_vec + 4 * n_mat]
        outs = refs[1 + 3 * n_vec + 4 * n_mat:]
        for i, (off, c) in enumerate(zip(offsets, widths)):
            g = r_ref[0:1, off:off + c]
            for s in range(1, N_DEV):
                g = g + r_ref[s:s + 1, off:off + c]
            w_ref, m_ref, v_ref = vec_in[3 * i:3 * i + 3]
            g_out, d_out, m_out, v_out = outs[4 * i:4 * i + 4]
            g_out[...] = g
            d_out[...], m_out[...], v_out[...] = _adam_update(w_ref[...], g, m_ref[...], v_ref[...])
        for j in range(n_mat):
            g_ref, w_ref, m_ref, v_ref = mat_in[4 * j:4 * j + 4]
            d_out, m_out, v_out = outs[4 * n_vec + 3 * j:4 * n_vec + 3 * j + 3]
            d_out[...], m_out[...], v_out[...] = _adam_update(w_ref[...], g_ref[...], m_ref[...], v_ref[...])

    args = [vec_parts] + [a for state in vec_state for a in state]
    for g, state in zip(mat_grads, mat_state):
        args += [g, *state]
    out_shape = [jax.ShapeDtypeStruct(w.shape, F32) for w, _, _ in vec_state for _ in range(4)]
    out_shape += [jax.ShapeDtypeStruct(w.shape, F32) for w, _, _ in mat_state for _ in range(3)]
    out = pl.pallas_call(body, name="adamw_replicated", out_shape=out_shape, compiler_params=_params())(*args)
    vec_out = [tuple(out[4 * i:4 * i + 4]) for i in range(n_vec)]
    mat_out = [tuple(out[4 * n_vec + 3 * j:4 * n_vec + 3 * j + 3]) for j in range(n_mat)]
    return vec_out, mat_out


def _adamw(w, recv, m, v, name):
    rows, width = w.shape
    n = recv.shape[0]
    if rows % 16 == 0 or width % 256:
        tr, tc = _row_tile(rows), width
    else:
        tr, tc = rows, 256

    def body(w_ref, r_ref, m_ref, v_ref, g_ref, d_ref, nm_ref, nv_ref):
        gv = r_ref[0].astype(F32)
        for s in range(1, n):
            gv = gv + r_ref[s].astype(F32)
        g_ref[...] = gv
        d_ref[...], nm_ref[...], nv_ref[...] = _adam_update(w_ref[...], gv, m_ref[...], v_ref[...])

    spec = pl.BlockSpec((tr, tc), lambda i, j: (i, j))
    shape = jax.ShapeDtypeStruct((rows, width), F32)
    return pl.pallas_call(
        body, name=name, grid=(rows // tr, width // tc),
        in_specs=[spec, pl.BlockSpec((n, tr, tc), lambda i, j: (0, i, j)), spec, spec],
        out_specs=[spec] * 4, out_shape=[shape] * 4,
        compiler_params=_params(("parallel", "parallel")),
    )(w, recv, m, v)


BIG_NAMES = ("w_in", "w_out_ssm", "w_out_lru", "w_out", "w_ffn_in", "w_ffn_out", "ssm_conv_w", "lru_conv_w")
TRANSPOSED = ("w_in", "w_ffn_in")
CONV_NAMES = ("ssm_conv_w", "lru_conv_w")
MATMUL_NAMES = BIG_NAMES[:6]
NEEDED_FIRST = ("w_in", "ssm_conv_w", "lru_conv_w")
GATHERED_IN_SSD = ("w_ffn_in",)
GATHERED_IN_LRU = ("w_ffn_out", "w_out_ssm", "w_out_lru", "w_out")
EXCHANGED_IN_SSD = ("w_ffn_in", "w_ffn_out")
EXCHANGED_IN_LRU = ("w_out_ssm", "w_out_lru", "w_out")
SMALL_VECTORS = ("norm1_w", "b_branch_gate", "ssm_conv_b", "ssm_dt_bias", "ssm_a_log", "ssm_d", "ssm_norm_w",
                 "lru_conv_b", "lru_b_r", "lru_b_i", "lru_lambda", "norm2_w", "norm_f_w")
SMALL_MATRICES = ("lru_w_r", "lru_w_i")


def _col_shards(full):
    rows, cols = full.shape
    return full.reshape(rows, N_DEV, cols // N_DEV).transpose(1, 0, 2)


def _from_col_shards(g):
    n, rows, w = g.shape
    return g.transpose(1, 0, 2).reshape(rows, n * w)


def kernel(x, norm1_w, w_in, b_branch_gate, ssm_conv_w, ssm_conv_b, ssm_dt_bias, ssm_a_log, ssm_d, ssm_norm_w, w_out_ssm, lru_conv_w, lru_conv_b, lru_w_r, lru_b_r, lru_w_i, lru_b_i, lru_lambda, w_out_lru, w_out, norm2_w, w_ffn_in, w_ffn_out, norm_f_w, loss_target, m_norm1_w, m_w_in, m_b_branch_gate, m_ssm_conv_w, m_ssm_conv_b, m_ssm_dt_bias, m_ssm_a_log, m_ssm_d, m_ssm_norm_w, m_w_out_ssm, m_lru_conv_w, m_lru_conv_b, m_lru_w_r, m_lru_b_r, m_lru_w_i, m_lru_b_i, m_lru_lambda, m_w_out_lru, m_w_out, m_norm2_w, m_w_ffn_in, m_w_ffn_out, m_norm_f_w, v_norm1_w, v_w_in, v_b_branch_gate, v_ssm_conv_w, v_ssm_conv_b, v_ssm_dt_bias, v_ssm_a_log, v_ssm_d, v_ssm_norm_w, v_w_out_ssm, v_lru_conv_w, v_lru_conv_b, v_lru_w_r, v_lru_b_r, v_lru_w_i, v_lru_b_i, v_lru_lambda, v_w_out_lru, v_w_out, v_norm2_w, v_w_ffn_in, v_w_ffn_out, v_norm_f_w):
    given = dict(locals())
    weights = {n: given[n] for n in BIG_NAMES + SMALL_VECTORS + SMALL_MATRICES}
    t = x.shape[1]
    xt = x[0]
    tgt = loss_target[0]

    def local(n, a):
        return a[0].T if n in TRANSPOSED else a[0]

    def as_output(n, a):
        return a.T[None] if n in TRANSPOSED else a[None]

    def shard(n):
        s = local(n, weights[n])
        return s.astype(BF16) if n in MATMUL_NAMES else s

    def unshard(n, g):
        return _from_col_shards(g) if n in CONV_NAMES else g.reshape(-1, g.shape[-1])

    def grad_slices(n):
        g = grads[n]
        return (_col_shards(g) if n in CONV_NAMES else g.reshape(N_DEV, -1, g.shape[-1])).astype(BF16)

    gathered = _all_gather([shard(n) for n in NEEDED_FIRST], "gather_in_weights")
    full = {n: unshard(n, g) for n, g in zip(NEEDED_FIRST, gathered)}
    ssm_cw, lru_cw = full["ssm_conv_w"], full["lru_conv_w"]
    wi_t = full["w_in"]
    lru_rows = wi_t[ORIG_LRU_X:].reshape(2, LRU_BLOCKS, LANES, D_MODEL).transpose(1, 0, 2, 3)
    w_pt = jnp.concatenate([wi_t[:ORIG_DT], lru_rows.reshape(2 * LRU_WIDTH, D_MODEL), wi_t[ORIG_DT:ORIG_LRU_X],
                            jnp.zeros((PROJ_W - IN_PROJ, D_MODEL), BF16)], axis=0)

    def pad_heads(a):
        return jnp.pad(a.reshape(1, SSM_HEADS), ((0, 0), (0, LANES - SSM_HEADS)))

    dt_bias_p = pad_heads(ssm_dt_bias)
    a_log_p = pad_heads(ssm_a_log)
    d_exp = jnp.repeat(ssm_d.reshape(SSM_HEADS), SSM_HEAD_DIM).reshape(1, SSM_INNER)
    lru_wr, lru_wi = lru_w_r[0], lru_w_i[0]

    hn1 = _rmsnorm_fwd(xt, norm1_w, "norm1_fwd")
    proj = _mm(hn1, w_pt, tb=True, name="in_proj")
    xbc_act = _ssm_conv_fwd(proj, ssm_cw, ssm_conv_b)
    (y_ssd, s_in_all), gathered = _ssd_fwd(xbc_act, proj, dt_bias_p, a_log_p, d_exp,
                                           gather=[shard(n) for n in GATHERED_IN_SSD])
    full.update({n: unshard(n, g) for n, g in zip(GATHERED_IN_SSD, gathered)})
    (l_out, h_lru), gathered = _lru_fwd(proj, lru_cw, lru_conv_b, lru_wr, lru_b_r, lru_wi, lru_b_i, lru_lambda,
                                        gather=[shard(n) for n in GATHERED_IN_LRU])
    full.update({n: unshard(n, g) for n, g in zip(GATHERED_IN_LRU, gathered)})
    y_pre = _ssm_post_fwd(y_ssd, proj, ssm_norm_w)
    y_ssm = _mm(y_pre, full["w_out_ssm"], name="out_ssm")
    y_lru = _mm(l_out, full["w_out_lru"], name="out_lru")
    merged = _merge_fwd(proj, b_branch_gate, y_ssm, y_lru)
    h1, hn2 = _mm(merged, full["w_out"], epilogue=_residual_norm_epilogue(xt, norm2_w), name="out_proj")
    gu = _mm(hn2, full["w_ffn_in"], tb=True, name="ffn_in")
    act = _swiglu_fwd(gu)

    grads = {}
    dh2, grads["norm_f_w"], loss_cols = _mm(
        act, full["w_ffn_out"], epilogue=_loss_epilogue(h1, norm_f_w.reshape(1, D_MODEL), tgt), name="ffn_out")
    loss = lax.psum(0.5 * jnp.sum(loss_cols) / D_MODEL, AXES)
    dact = _mm(dh2, full["w_ffn_out"], tb=True, name="d_act")
    grads["w_ffn_out"] = _mm(act, dh2, ta=True, out_dtype=BF16, name="dw_ffn_out")
    dgu = _swiglu_bwd(gu, dact)
    dh1, grads["norm2_w"] = _mm(dgu, full["w_ffn_in"], epilogue=_norm_bwd_epilogue(h1, norm2_w, dh2), name="d_hn2")
    grads["w_ffn_in"] = _mm(dgu, hn2, ta=True, out_dtype=BF16, name="dw_ffn_in")
    dmerged = _mm(dh1, full["w_out"], tb=True, name="d_merged")
    grads["w_out"] = _mm(merged, dh1, ta=True, out_dtype=BF16, name="dw_out")
    dy_ssm, dy_lru, dproj, grads["b_branch_gate"] = _merge_bwd(proj, b_branch_gate, y_ssm, y_lru, dmerged)
    dy_pre = _mm(dy_ssm, full["w_out_ssm"], tb=True, name="d_y_pre")
    grads["w_out_ssm"] = _mm(y_pre, dy_ssm, ta=True, out_dtype=BF16, name="dw_out_ssm")
    dl_out = _mm(dy_lru, full["w_out_lru"], tb=True, name="d_l_out")
    grads["w_out_lru"] = _mm(l_out, dy_lru, ta=True, out_dtype=BF16, name="dw_out_lru")
    dy_ssd, dproj, grads["ssm_norm_w"] = _ssm_post_bwd(y_ssd, proj, ssm_norm_w, dy_pre, dproj)
    (dxbc_act, dproj, dbias, dalog, ddcol), recv_in_ssd = _ssd_bwd(
        xbc_act, proj, s_in_all, dy_ssd, dt_bias_p, a_log_p, d_exp, dproj,
        exchange=[grad_slices(n) for n in EXCHANGED_IN_SSD])
    grads["ssm_dt_bias"] = dbias[:, :SSM_HEADS]
    grads["ssm_a_log"] = dalog[:, :SSM_HEADS]
    grads["ssm_d"] = ddcol.reshape(SSM_HEADS, SSM_HEAD_DIM).sum(axis=1).reshape(1, SSM_HEADS)
    dproj, grads["ssm_conv_w"], grads["ssm_conv_b"] = _ssm_conv_bwd(proj, ssm_cw, ssm_conv_b, dxbc_act, dproj)
    ((dproj, grads["lru_conv_w"], grads["lru_conv_b"], dwr, grads["lru_b_r"], dwi, grads["lru_b_i"],
      grads["lru_lambda"]), recv_in_lru) = _lru_bwd(
        proj, lru_cw, lru_conv_b, lru_wr, lru_b_r, lru_wi, lru_b_i, lru_lambda, h_lru, dl_out, dproj,
        exchange=[grad_slices(n) for n in EXCHANGED_IN_LRU])
    grads["lru_w_r"], grads["lru_w_i"] = dwr[None], dwi[None]
    dwpt = _mm(dproj, hn1, ta=True, out_dtype=BF16, name="dw_in")
    lru_rows = dwpt[COL_LRU:COL_DT].reshape(LRU_BLOCKS, 2, LANES, D_MODEL).transpose(1, 0, 2, 3)
    grads["w_in"] = jnp.concatenate([dwpt[:ORIG_DT], dwpt[COL_DT:COL_DT + SSM_HEADS],
                                     lru_rows.reshape(2 * LRU_WIDTH, D_MODEL)], axis=0)
    w_in_parts = grad_slices("w_in")
    from_sibling = _sibling_exchange(w_in_parts, "sibling_exchange_dw_in")
    for_my_core = lax.dynamic_index_in_dim(w_in_parts.reshape((N_DEV // 2, 2) + w_in_parts.shape[1:]),
                                           lax.axis_index("c"), axis=1, keepdims=False)
    chip_parts = [_pair_sum(for_my_core, from_sibling, "pair_sum_dw_in")]
    direct = [grad_slices(n) for n in CONV_NAMES] + [grads[n].reshape(N_DEV, -1, LANES) for n in SMALL_MATRICES]
    (grad_x, grads["norm1_w"]), (recv_w_in, recv_direct) = _mm(
        dproj, w_pt, epilogue=_norm_bwd_epilogue(xt, norm1_w, dh1), name="d_hn1",
        hosted=[(_chip_exchange_phase, chip_parts, chip_parts), (_exchange_phase, direct, direct)])
    recv_first = list(recv_w_in) + list(recv_direct[:len(CONV_NAMES)])
    recv_mats = recv_direct[len(CONV_NAMES):]

    def as_rows(n, a):
        return a.reshape(1, -1) if n in SMALL_VECTORS else a.reshape(-1, LANES)

    vec_g = jnp.concatenate([jnp.pad(as_rows(n, grads[n]), ((0, 0), (0, (-grads[n].size) % LANES)))
                             for n in SMALL_VECTORS], axis=1)
    gathered = _all_gather([vec_g] + list(_sum_sources(recv_mats, "sum_gate_matrix_grads")), "gather_small_grads")
    vec_parts = gathered[0].reshape(N_DEV, -1)
    mat_g = [g.reshape(-1, LANES) for g in gathered[1:]]

    recv = dict(zip(EXCHANGED_IN_SSD + EXCHANGED_IN_LRU + NEEDED_FIRST,
                    list(recv_in_ssd) + list(recv_in_lru) + list(recv_first)))
    big_out = {n: _adamw(local(n, weights[n]), recv[n], local(n, given["m_" + n]), local(n, given["v_" + n]),
                         "adamw_" + n) for n in BIG_NAMES}

    def state(n):
        return tuple(as_rows(n, given[p + n]) for p in ("", "m_", "v_"))

    vec_out, mat_out = _adamw_small(vec_parts, [state(n) for n in SMALL_VECTORS],
                                    mat_g, [state(n) for n in SMALL_MATRICES])
    small_out = dict(zip(SMALL_VECTORS, vec_out))
    small_out.update({n: (g,) + out for n, g, out in zip(SMALL_MATRICES, mat_g, mat_out)})

    order = list(given)[1:24]
    results = []
    for q in range(4):
        vals = {n: as_output(n, big_out[n][q]) for n in BIG_NAMES}
        vals.update({n: out[q].reshape(weights[n].shape) for n, out in small_out.items()})
        results.extend(vals[n] for n in order)
    return (loss, grad_x[None], *results)
```

```python
import math

import jax
import jax.numpy as jnp
from jax import lax
from jax.experimental import pallas as pl
from jax.experimental.pallas import tpu as pltpu

F32 = jnp.float32
BF16 = jnp.bfloat16
HIGHEST = lax.Precision.HIGHEST
MESH = pl.DeviceIdType.MESH
AXES = ("x", "y", "c")
N_DEV = 8

D_MODEL = 1024
SSM_INNER = 2048
SSM_HEADS = 32
SSM_HEAD_DIM = 64
SSM_GROUPS = 4
SSM_STATE = 128
SSM_BC = SSM_GROUPS * SSM_STATE
SSM_CONV_DIM = SSM_INNER + 2 * SSM_BC
SSM_CHUNK = 128
SSM_PAIRS = SSM_HEADS // 2
CONV_K = 4
LRU_WIDTH = 1280
LRU_BLOCKS = 10
LRU_C = 8.0
FFN_HIDDEN = 2816
RMS_EPS = 1e-6
IN_PROJ = 9760

COL_GATES = 0
COL_Z = 2048
COL_XBC = 4096
COL_LRU = 7168
COL_DT = 9728
PROJ_W = 9856
ORIG_DT = 7168
ORIG_LRU_X = 7200
ORIG_LRU_Y = 8480

ADAM_LR = 0.001
ADAM_B1 = 0.9
ADAM_B2 = 0.999
ADAM_EPS = 1e-08
ADAM_WD = 0.01
ADAM_STEP = 10

LANES = 128
SUBLANES = 8
V7X_VMEM_BYTES = 64 * 1024 * 1024
VMEM_LIMIT = V7X_VMEM_BYTES * 3 // 4
VMEM_LIMIT_BIG = V7X_VMEM_BYTES * 15 // 16

NT = (((1,), (1,)), ((), ()))
TN = (((0,), (0,)), ((), ()))


def _params(sem=None, big=False):
    return pltpu.CompilerParams(dimension_semantics=sem,
                                vmem_limit_bytes=VMEM_LIMIT_BIG if big else VMEM_LIMIT)


def _blk(dim, cap):
    if dim <= cap:
        return dim
    for m in range(cap // LANES, 0, -1):
        if dim % (m * LANES) == 0:
            return m * LANES
    raise ValueError(f"no block for {dim}")


def _rows(t):
    return min(t, 256)


def _sigmoid(v):
    return 1.0 / (1.0 + jnp.exp(-v))


def _softplus(v):
    e = jnp.exp(-jnp.abs(v))
    u = 1.0 + e
    log1p = jnp.where(u == 1.0, e, jnp.log(u) * e / jnp.where(u == 1.0, 1.0, u - 1.0))
    return jnp.maximum(v, 0.0) + log1p


def _iota(shape, dim):
    return lax.broadcasted_iota(jnp.int32, shape, dim)


def _shift_down(v, s):
    if s == 0:
        return v
    return jnp.where(_iota(v.shape, 0) >= s, pltpu.roll(v, s, 0), 0.0)


def _shift_up(v, s):
    if s == 0:
        return v
    n = v.shape[0]
    return jnp.where(_iota(v.shape, 0) < n - s, pltpu.roll(v, n - s, 0), 0.0)


def _bdot(a, b, dn=None):
    a = a.astype(BF16)
    b = b.astype(BF16)
    if dn is None:
        return jnp.dot(a, b, preferred_element_type=F32)
    return lax.dot_general(a, b, dn, preferred_element_type=F32)


def _split_dot(a, e, dn=None):
    hi = a.astype(BF16)
    lo = (a - hi.astype(F32)).astype(BF16)
    return _bdot(hi, e, dn) + _bdot(lo, e, dn)


def _fdot(a, b, dn=None):
    if dn is None:
        return jnp.dot(a, b, precision=HIGHEST, preferred_element_type=F32)
    return lax.dot_general(a, b, dn, precision=HIGHEST, preferred_element_type=F32)


def _mm(a, b, *, ta=False, tb=False, add=None, hosted=(), out_dtype=F32, epilogue=None, name):
    if ta:
        kdim, m = a.shape
    else:
        m, kdim = a.shape
    if tb:
        n, k2 = b.shape
    else:
        k2, n = b.shape
    assert kdim == k2, (a.shape, b.shape, ta, tb)
    if epilogue is None:
        rows, vecs, row_dtypes, n_vec_out = ([] if add is None else [add]), [], [out_dtype], 0

        def finish(r, row_vals, vec_vals):
            return ((r + row_vals[0]) if row_vals else r,), ()
    else:
        assert add is None
        finish, rows, vecs, row_dtypes, n_vec_out = epilogue
    bm, bn, bk = _blk(m, 1408 if epilogue is None else 512), _blk(n, 1408), _blk(kdim, 1408)
    grid = (m // bm, n // bn, kdim // bk)
    nk = grid[2]
    assert n_vec_out == 0 or grid[1] == 1, "column sums are accumulated over the row tiles of whole rows"
    dn = (((0 if ta else 1,), (1 if tb else 0,)), ((), ()))
    n_in = 2 + len(rows) + len(vecs)
    n_out = len(row_dtypes) + n_vec_out
    sizes = [len(arrays) for _, arrays, _ in hosted]
    n_ex = sum(sizes)

    def body(*refs):
        a_ref, b_ref = refs[:2]
        row_refs, vec_refs = refs[2:2 + len(rows)], refs[2 + len(rows):n_in]
        out_refs = refs[n_in + n_ex:n_in + n_ex + n_out]
        acc = refs[n_in + 2 * n_ex + n_out]
        comms, at = [], 0
        for g, size in enumerate(sizes):
            sems = refs[n_in + 2 * n_ex + n_out + 1 + 3 * g:n_in + 2 * n_ex + n_out + 4 + 3 * g]
            comms.append((refs[n_in + at:n_in + at + size],
                          refs[n_in + n_ex + n_out + at:n_in + n_ex + n_out + at + size]) + tuple(sems))
            at += size
        step = (pl.program_id(0) * grid[1] + pl.program_id(1)) * nk + pl.program_id(2)
        k = pl.program_id(2)
        if n_ex:
            @pl.when(step == 0)
            def _():
                for (phase_fn, _, _), comm in zip(hosted, comms):
                    phase_fn("start", *comm)

        @pl.when(k == 0)
        def _():
            acc[...] = jnp.zeros_like(acc)

        acc[...] += lax.dot_general(a_ref[...].astype(BF16), b_ref[...].astype(BF16), dn,
                                    preferred_element_type=F32)

        @pl.when(k == nk - 1)
        def _():
            row_outs, col_sums = finish(acc[...], [r[...] for r in row_refs], [v[...] for v in vec_refs])
            for o_ref, val in zip(out_refs, row_outs):
                o_ref[...] = val.astype(o_ref.dtype)
            for o_ref, val in zip(out_refs[len(row_dtypes):], col_sums):
                @pl.when(pl.program_id(0) == 0)
                def _():
                    o_ref[...] = val

                @pl.when(pl.program_id(0) > 0)
                def _():
                    o_ref[...] += val

        if n_ex:
            @pl.when(step == grid[0] * grid[1] * nk - 1)
            def _():
                for (phase_fn, _, _), comm in zip(hosted, comms):
                    phase_fn("finish", *comm)

    a_spec = pl.BlockSpec((bk, bm), lambda i, j, k: (k, i)) if ta else pl.BlockSpec((bm, bk), lambda i, j, k: (i, k))
    b_spec = pl.BlockSpec((bn, bk), lambda i, j, k: (j, k)) if tb else pl.BlockSpec((bk, bn), lambda i, j, k: (k, j))
    o_spec = pl.BlockSpec((bm, bn), lambda i, j, k: (i, j))
    v_spec = pl.BlockSpec((1, bn), lambda i, j, k: (0, j))
    in_specs = [a_spec, b_spec] + [o_spec] * len(rows) + [v_spec] * len(vecs) + [HBM] * n_ex
    args = [a, b] + list(rows) + list(vecs) + [p for _, arrays, _ in hosted for p in arrays]
    sequential = n_ex or n_vec_out
    out = pl.pallas_call(
        body, name=name, grid=grid,
        in_specs=in_specs, out_specs=[o_spec] * len(row_dtypes) + [v_spec] * n_vec_out + [HBM] * n_ex,
        out_shape=[jax.ShapeDtypeStruct((m, n), dt) for dt in row_dtypes]
        + [jax.ShapeDtypeStruct((1, n), F32)] * n_vec_out
        + [jax.ShapeDtypeStruct(r.shape, r.dtype) for _, _, results in hosted for r in results],
        scratch_shapes=[pltpu.VMEM((bm, bn), F32)] + [s for size in sizes for s in _comm_scratch(size)],
        compiler_params=_params(("arbitrary",) * 3 if sequential else ("parallel", "parallel", "arbitrary")),
    )(*args)
    result = out[0] if n_out == 1 else tuple(out[:n_out])
    if not n_ex:
        return result
    received, at = [], n_out
    for size in sizes:
        received.append(out[at:at + size])
        at += size
    return result, received


def _rmsnorm_fwd(x, w, name):
    t, d = x.shape
    tr = _rows(t)

    def body(x_ref, w_ref, o_ref):
        xv = x_ref[...]
        rstd = lax.rsqrt(jnp.mean(xv * xv, axis=-1, keepdims=True) + RMS_EPS)
        o_ref[...] = (xv * rstd * w_ref[...]).astype(BF16)

    return pl.pallas_call(
        body, name=name, grid=(t // tr,),
        in_specs=[pl.BlockSpec((tr, d), lambda i: (i, 0)), pl.BlockSpec((1, d), lambda i: (0, 0))],
        out_specs=pl.BlockSpec((tr, d), lambda i: (i, 0)),
        out_shape=jax.ShapeDtypeStruct((t, d), BF16),
        compiler_params=_params(("parallel",)),
    )(x, w)


def _normalize(h):
    rstd = lax.rsqrt(jnp.mean(h * h, axis=-1, keepdims=True) + RMS_EPS)
    return rstd, h * rstd


def _residual_norm_epilogue(x, w):
    def finish(r, rows, vecs):
        h = r + rows[0]
        return (h, _normalize(h)[1] * vecs[0]), ()

    return finish, [x], [w], [F32, BF16], 0


def _norm_bwd_epilogue(x, w, dres):
    def finish(r, rows, vecs):
        rstd, xhat = _normalize(rows[0])
        dxhat = r * vecs[0]
        m = jnp.mean(dxhat * xhat, axis=-1, keepdims=True)
        return (rstd * (dxhat - xhat * m) + rows[1],), (jnp.sum(r * xhat, axis=0, keepdims=True),)

    return finish, [x, dres], [w], [F32], 1


def _loss_epilogue(h1, w, tgt):
    d = h1.shape[1]

    def finish(r, rows, vecs):
        rstd, xhat = _normalize(r + rows[0])
        err = xhat * vecs[0] - rows[1]
        dyv = err * (1.0 / d)
        dxhat = dyv * vecs[0]
        m = jnp.mean(dxhat * xhat, axis=-1, keepdims=True)
        return ((rstd * (dxhat - xhat * m),),
                (jnp.sum(dyv * xhat, axis=0, keepdims=True), jnp.sum(err * err, axis=0, keepdims=True)))

    return finish, [h1, tgt], [w], [F32], 2


def _merge_fwd(proj, bg, ys, yl):
    t = proj.shape[0]
    d = D_MODEL
    tr = _rows(t)

    def body(ps_ref, pl_ref, bg_ref, ys_ref, yl_ref, o_ref):
        gs = _sigmoid(ps_ref[...] + bg_ref[:, 0:d])
        gl = _sigmoid(pl_ref[...] + bg_ref[:, d:2 * d])
        o_ref[...] = (gs * ys_ref[...] + gl * yl_ref[...]).astype(BF16)

    row = pl.BlockSpec((tr, d), lambda i: (i, 0))
    return pl.pallas_call(
        body, name="merge_fwd", grid=(t // tr,),
        in_specs=[row, pl.BlockSpec((tr, d), lambda i: (i, 1)), pl.BlockSpec((1, 2 * d), lambda i: (0, 0)), row, row],
        out_specs=row, out_shape=jax.ShapeDtypeStruct((t, d), BF16),
        compiler_params=_params(("parallel",)),
    )(proj, proj, bg, ys, yl)


def _merge_bwd(proj, bg, ys, yl, dm):
    t = proj.shape[0]
    d = D_MODEL
    tr = _rows(t)

    def body(ps_ref, pl_ref, bg_ref, ys_ref, yl_ref, dm_ref, dys_ref, dyl_ref, dg_ref, dbg_ref):
        i = pl.program_id(0)
        gs = _sigmoid(ps_ref[...] + bg_ref[:, 0:d])
        gl = _sigmoid(pl_ref[...] + bg_ref[:, d:2 * d])
        dmv = dm_ref[...]
        dys_ref[...] = (dmv * gs).astype(BF16)
        dyl_ref[...] = (dmv * gl).astype(BF16)
        dgs = dmv * ys_ref[...] * gs * (1.0 - gs)
        dgl = dmv * yl_ref[...] * gl * (1.0 - gl)
        dg_ref[:, 0:d] = dgs.astype(BF16)
        dg_ref[:, d:2 * d] = dgl.astype(BF16)

        @pl.when(i == 0)
        def _():
            dbg_ref[...] = jnp.zeros_like(dbg_ref)

        dbg_ref[:, 0:d] += jnp.sum(dgs, axis=0, keepdims=True)
        dbg_ref[:, d:2 * d] += jnp.sum(dgl, axis=0, keepdims=True)

    row = pl.BlockSpec((tr, d), lambda i: (i, 0))
    wide = pl.BlockSpec((tr, 2 * d), lambda i: (i, 0))
    vec = pl.BlockSpec((1, 2 * d), lambda i: (0, 0))
    return pl.pallas_call(
        body, name="merge_bwd", grid=(t // tr,),
        in_specs=[row, pl.BlockSpec((tr, d), lambda i: (i, 1)), vec, row, row, row],
        out_specs=[row, row, wide, vec],
        out_shape=[jax.ShapeDtypeStruct((t, d), BF16), jax.ShapeDtypeStruct((t, d), BF16),
                   jax.ShapeDtypeStruct((t, PROJ_W), BF16), jax.ShapeDtypeStruct((1, 2 * d), F32)],
        compiler_params=_params(("arbitrary",)),
    )(proj, proj, bg, ys, yl, dm)


def _ffn_in_swiglu(hn, wt):
    t, d = hn.shape
    f = FFN_HIDDEN
    bm, bn = _blk(t, 1024), _blk(f, 1408)
    nj = f // bn

    def body(a_ref, wg_ref, wu_ref, g_ref, u_ref, act_ref):
        a = a_ref[...]
        g = _bdot(a, wg_ref[...], NT)
        u = _bdot(a, wu_ref[...], NT)
        g_ref[...] = g.astype(BF16)
        u_ref[...] = u.astype(BF16)
        act_ref[...] = (g * _sigmoid(g) * u).astype(BF16)

    out = pl.BlockSpec((bm, bn), lambda i, j: (i, j))
    shape = jax.ShapeDtypeStruct((t, f), BF16)
    return pl.pallas_call(
        body, name="ffn_in_swiglu", grid=(t // bm, nj),
        in_specs=[pl.BlockSpec((bm, d), lambda i, j: (i, 0)), pl.BlockSpec((bn, d), lambda i, j: (j, 0)),
                  pl.BlockSpec((bn, d), lambda i, j: (nj + j, 0))],
        out_specs=[out, out, out], out_shape=[shape, shape, shape],
        compiler_params=_params(("parallel", "parallel")),
    )(hn, wt, wt)


def _swiglu_bwd(g_all, u_all, dact):
    t, f = g_all.shape
    tr = _rows(t)

    def body(g_ref, u_ref, da_ref, o_ref):
        g = g_ref[...].astype(F32)
        sg = _sigmoid(g)
        da = da_ref[...].astype(F32)
        o_ref[:, 0:f] = (da * u_ref[...].astype(F32) * (sg * (1.0 + g * (1.0 - sg)))).astype(BF16)
        o_ref[:, f:2 * f] = (da * g * sg).astype(BF16)

    row = pl.BlockSpec((tr, f), lambda i: (i, 0))
    return pl.pallas_call(
        body, name="swiglu_bwd", grid=(t // tr,),
        in_specs=[row, row, row],
        out_specs=pl.BlockSpec((tr, 2 * f), lambda i: (i, 0)),
        out_shape=jax.ShapeDtypeStruct((t, 2 * f), BF16),
        compiler_params=_params(("parallel",)),
    )(g_all, u_all, dact)


def _conv_pre(xv, wv, bv):
    pre = bv + wv[CONV_K - 1:CONV_K, :] * xv
    for k in range(CONV_K - 1):
        pre = pre + wv[k:k + 1, :] * _shift_down(xv, CONV_K - 1 - k)
    return pre


def _ssm_conv_fwd(proj, w, b):
    t = proj.shape[0]
    nb = SSM_CONV_DIM // LANES
    c0 = COL_XBC // LANES

    def body(x_ref, w_ref, b_ref, o_ref):
        pre = _conv_pre(x_ref[...], w_ref[...], b_ref[...])
        o_ref[...] = pre * _sigmoid(pre)

    return pl.pallas_call(
        body, name="ssm_conv_fwd", grid=(nb,),
        in_specs=[pl.BlockSpec((t, LANES), lambda j: (0, c0 + j)), pl.BlockSpec((CONV_K, LANES), lambda j: (0, j)),
                  pl.BlockSpec((1, LANES), lambda j: (0, j))],
        out_specs=pl.BlockSpec((t, LANES), lambda j: (0, j)),
        out_shape=jax.ShapeDtypeStruct((t, SSM_CONV_DIM), F32),
        compiler_params=_params(("parallel",)),
    )(proj, w, b)


def _ssm_conv_bwd(proj, w, b, dact, dproj):
    t = proj.shape[0]
    nb = SSM_CONV_DIM // LANES
    c0 = COL_XBC // LANES

    def body(x_ref, w_ref, b_ref, da_ref, dproj_in, dx_ref, dw_ref, db_ref):
        xv = x_ref[...]
        wv = w_ref[...]
        pre = _conv_pre(xv, wv, b_ref[...])
        sg = _sigmoid(pre)
        dpre = da_ref[...] * (sg * (1.0 + pre * (1.0 - sg)))
        dx = wv[CONV_K - 1:CONV_K, :] * dpre
        for k in range(CONV_K - 1):
            dx = dx + wv[k:k + 1, :] * _shift_up(dpre, CONV_K - 1 - k)
        dx_ref[...] = dx.astype(BF16)
        for k in range(CONV_K):
            dw_ref[k:k + 1, :] = jnp.sum(dpre * _shift_down(xv, CONV_K - 1 - k), axis=0, keepdims=True)
        db_ref[...] = jnp.sum(dpre, axis=0, keepdims=True)

    col = pl.BlockSpec((t, LANES), lambda j: (0, j))
    wsp = pl.BlockSpec((CONV_K, LANES), lambda j: (0, j))
    bsp = pl.BlockSpec((1, LANES), lambda j: (0, j))
    return pl.pallas_call(
        body, name="ssm_conv_bwd", grid=(nb,),
        in_specs=[pl.BlockSpec((t, LANES), lambda j: (0, c0 + j)), wsp, bsp, col, HBM],
        out_specs=[pl.BlockSpec((t, LANES), lambda j: (0, c0 + j)), wsp, bsp],
        out_shape=[jax.ShapeDtypeStruct(dproj.shape, dproj.dtype), jax.ShapeDtypeStruct((CONV_K, SSM_CONV_DIM), F32),
                   jax.ShapeDtypeStruct((1, SSM_CONV_DIM), F32)],
        input_output_aliases={4: 0},
        compiler_params=_params(("parallel",)),
    )(proj, w, b, dact, dproj)


def _ssd_chunk_terms(dtr, bias, alog):
    a = -jnp.exp(alog)
    dt = _softplus(dtr + bias)
    row = _iota((SSM_CHUNK, SSM_CHUNK), 0)
    col = _iota((SSM_CHUNK, SSM_CHUNK), 1)
    tri = (row >= col).astype(F32)
    cs = _fdot(tri, dt * a)
    dec = jnp.exp(cs[SSM_CHUNK - 1:SSM_CHUNK, :] - cs)
    ecs = jnp.exp(cs)
    off = _iota((LANES, SSM_INNER), 1) - SSM_HEAD_DIM * _iota((LANES, SSM_INNER), 0)
    expand = jnp.where(jnp.logical_and(off >= 0, off < SSM_HEAD_DIM), 1.0, 0.0).astype(BF16)
    return a, dt, cs, dec, ecs, expand, row, col


def _ssd_specs(t):
    nc = t // SSM_CHUNK
    xs = pl.BlockSpec((SSM_CHUNK, SSM_INNER), lambda c: (c, 0))
    bm = pl.BlockSpec((SSM_CHUNK, SSM_BC), lambda c: (c, SSM_INNER // SSM_BC))
    cm = pl.BlockSpec((SSM_CHUNK, SSM_BC), lambda c: (c, SSM_INNER // SSM_BC + 1))
    dtr = pl.BlockSpec((SSM_CHUNK, LANES), lambda c: (c, COL_DT // LANES))
    vec = pl.BlockSpec((1, LANES), lambda c: (0, 0))
    wide = pl.BlockSpec((1, SSM_INNER), lambda c: (0, 0))
    return nc, xs, bm, cm, dtr, vec, wide


def _ssd_fwd(xbc_act, proj, bias, alog, dexp, gather):
    t = proj.shape[0]
    nc, xs_s, bm_s, cm_s, dtr_s, vec, wide = _ssd_specs(t)
    n = len(gather)

    def body(*refs):
        xs_ref, b_ref, c_ref, dtr_ref, bias_ref, alog_ref, dexp_ref = refs[:7]
        y_ref, sin_ref = refs[7 + n:9 + n]
        state = refs[9 + 2 * n]
        comm = (refs[7:7 + n], refs[9 + n:9 + 2 * n]) + tuple(refs[10 + 2 * n:])
        chunk = pl.program_id(0)

        @pl.when(chunk == 0)
        def _():
            _gather_phase("start", *comm)
            state[...] = jnp.zeros_like(state)

        @pl.when(chunk == (3 * nc) // 4)
        def _():
            _gather_phase("forward", *comm)

        a, dt, cs, dec, ecs, expand, row, col = _ssd_chunk_terms(dtr_ref[...], bias_ref[...], alog_ref[...])
        cst = cs.T
        dt_x = _split_dot(dt, expand)
        dec_x = _split_dot(dec, expand)
        ecs_x = _split_dot(ecs, expand)
        xs = xs_ref[...]
        xdt = xs * dt_x
        xdec = xdt * dec_x
        lane_lo = col < SSM_HEAD_DIM
        causal = row >= col
        sin_ref[0] = state[...]
        for g in range(SSM_GROUPS):
            bg = b_ref[:, g * SSM_STATE:(g + 1) * SSM_STATE].astype(BF16)
            cg = c_ref[:, g * SSM_STATE:(g + 1) * SSM_STATE].astype(BF16)
            cb = _bdot(cg, bg, NT)
            for q in range(SSM_PAIRS // SSM_GROUPS):
                pq = g * (SSM_PAIRS // SSM_GROUPS) + q
                sl = slice(pq * LANES, (pq + 1) * LANES)
                xp = xdt[:, sl].astype(BF16)
                yd = []
                for hh in range(2):
                    h = 2 * pq + hh
                    lmat = jnp.exp(jnp.where(causal, cs[:, h:h + 1] - cst[h:h + 1, :], -jnp.inf))
                    yd.append(_bdot(cb * lmat, xp))
                s_in = state[pq]
                y_off = _bdot(cg, s_in) * ecs_x[:, sl]
                y_ref[:, sl] = jnp.where(lane_lo, yd[0], yd[1]) + y_off + xs[:, sl] * dexp_ref[:, sl]
                state[pq] = s_in * ecs_x[SSM_CHUNK - 1:SSM_CHUNK, sl] + _bdot(bg, xdec[:, sl], TN)

        @pl.when(chunk == nc - 1)
        def _():
            _gather_phase("finish", *comm)

    out = pl.pallas_call(
        body, name="ssd_fwd", grid=(nc,),
        in_specs=[xs_s, bm_s, cm_s, dtr_s, vec, vec, wide] + [HBM] * n,
        out_specs=[pl.BlockSpec((SSM_CHUNK, SSM_INNER), lambda c: (c, 0)),
                   pl.BlockSpec((1, SSM_PAIRS, SSM_STATE, LANES), lambda c: (c, 0, 0, 0))] + [HBM] * n,
        out_shape=[jax.ShapeDtypeStruct((t, SSM_INNER), F32),
                   jax.ShapeDtypeStruct((nc, SSM_PAIRS, SSM_STATE, LANES), F32)]
        + [jax.ShapeDtypeStruct((N_DEV,) + v.shape, v.dtype) for v in gather],
        scratch_shapes=[pltpu.VMEM((SSM_PAIRS, SSM_STATE, LANES), F32)] + _comm_scratch(n),
        compiler_params=_params(("arbitrary",)),
    )(xbc_act, xbc_act, xbc_act, proj, bias, alog, dexp, *gather)
    return out[:2], out[2:]


def _ssd_bwd(xbc_act, proj, s_in_all, dy, bias, alog, dexp, dproj, exchange):
    n_ex = len(exchange)
    t = proj.shape[0]
    nc = t // SSM_CHUNK
    last = nc - 1
    xs_s = pl.BlockSpec((SSM_CHUNK, SSM_INNER), lambda c: (last - c, 0))
    bm_s = pl.BlockSpec((SSM_CHUNK, SSM_BC), lambda c: (last - c, SSM_INNER // SSM_BC))
    cm_s = pl.BlockSpec((SSM_CHUNK, SSM_BC), lambda c: (last - c, SSM_INNER // SSM_BC + 1))
    dtr_s = pl.BlockSpec((SSM_CHUNK, LANES), lambda c: (last - c, COL_DT // LANES))
    sin_s = pl.BlockSpec((1, SSM_PAIRS, SSM_STATE, LANES), lambda c: (last - c, 0, 0, 0))
    vec = pl.BlockSpec((1, LANES), lambda c: (0, 0))
    wide = pl.BlockSpec((1, SSM_INNER), lambda c: (0, 0))

    def body(*refs):
        xs_ref, b_ref, c_ref, dtr_ref, sin_ref, dy_ref, bias_ref, alog_ref, dexp_ref = refs[:9]
        dxbc_ref, ddtr_ref, dbias_ref, dalog_ref, ddcol_ref = refs[10 + n_ex:15 + n_ex]
        dstate, dxdt_s, yoff_s, rx_s, trow_s = refs[15 + 2 * n_ex:20 + 2 * n_ex]
        comm = (refs[10:10 + n_ex], refs[15 + n_ex:15 + 2 * n_ex]) + tuple(refs[20 + 2 * n_ex:])

        @pl.when(pl.program_id(0) == 0)
        def _():
            _exchange_phase("start", *comm)
            dstate[...] = jnp.zeros_like(dstate)
            trow_s[...] = jnp.zeros_like(trow_s)
            dbias_ref[...] = jnp.zeros_like(dbias_ref)
            dalog_ref[...] = jnp.zeros_like(dalog_ref)
            ddcol_ref[...] = jnp.zeros_like(ddcol_ref)

        dtr = dtr_ref[...]
        a, dt, cs, dec, ecs, expand, row, col = _ssd_chunk_terms(dtr, bias_ref[...], alog_ref[...])
        cst = cs.T
        dt_x = _split_dot(dt, expand)
        dec_x = _split_dot(dec, expand)
        ecs_x = _split_dot(ecs, expand)
        xs = xs_ref[...]
        dyv = dy_ref[...]
        xdt = xs * dt_x
        lane_lo = col < SSM_HEAD_DIM
        causal = row >= col
        ddcol_ref[...] += jnp.sum(dyv * xs, axis=0, keepdims=True)
        dcs_col = jnp.zeros((SSM_CHUNK, LANES), F32)
        dcs_row = jnp.zeros((LANES, SSM_CHUNK), F32)
        for g in range(SSM_GROUPS):
            bg = b_ref[:, g * SSM_STATE:(g + 1) * SSM_STATE].astype(BF16)
            cg = c_ref[:, g * SSM_STATE:(g + 1) * SSM_STATE].astype(BF16)
            cb = _bdot(cg, bg, NT)
            dgm = jnp.zeros((SSM_CHUNK, SSM_CHUNK), F32)
            dbg = jnp.zeros((SSM_CHUNK, SSM_STATE), F32)
            dcg = jnp.zeros((SSM_CHUNK, SSM_STATE), F32)
            for q in range(SSM_PAIRS // SSM_GROUPS):
                pq = g * (SSM_PAIRS // SSM_GROUPS) + q
                sl = slice(pq * LANES, (pq + 1) * LANES)
                dyp = dyv[:, sl]
                xp = xdt[:, sl]
                dxh = []
                for hh in range(2):
                    h = 2 * pq + hh
                    lmat = jnp.exp(jnp.where(causal, cs[:, h:h + 1] - cst[h:h + 1, :], -jnp.inf))
                    mmat = cb * lmat
                    dyh = jnp.where(lane_lo if hh == 0 else jnp.logical_not(lane_lo), dyp, 0.0)
                    dmm = _bdot(dyh, xp, NT)
                    pm = dmm * mmat
                    dcs_col = jnp.where(col == h, jnp.sum(pm, axis=1, keepdims=True), dcs_col)
                    dcs_row = jnp.where(row == h, jnp.sum(pm, axis=0, keepdims=True), dcs_row)
                    dgm = dgm + dmm * lmat
                    dxh.append(_bdot(mmat, dyp, TN))
                s_in = sin_ref[0, pq]
                ecs_p = ecs_x[:, sl]
                dec_p = dec_x[:, sl]
                etot_p = ecs_x[SSM_CHUNK - 1:SSM_CHUNK, sl]
                yoff_s[:, sl] = dyp * (_bdot(cg, s_in) * ecs_p)
                dq = dyp * ecs_p
                dcg = dcg + _bdot(dq, s_in, NT)
                ds = dstate[pq]
                r = _bdot(bg, ds)
                rx_s[:, sl] = r * xp
                dxdt_s[:, sl] = jnp.where(lane_lo, dxh[0], dxh[1]) + dec_p * r
                dbg = dbg + _bdot(xp * dec_p, ds, NT)
                trow_s[0:1, sl] = jnp.sum(ds * s_in, axis=0, keepdims=True) * etot_p
                dstate[pq] = etot_p * ds + _bdot(cg, dq, TN)
            dcg = dcg + _bdot(dgm, bg)
            dbg = dbg + _bdot(dgm, cg, TN)
            dxbc_ref[:, SSM_INNER + g * SSM_STATE:SSM_INNER + (g + 1) * SSM_STATE] = dbg
            dxbc_ref[:, SSM_INNER + SSM_BC + g * SSM_STATE:SSM_INNER + SSM_BC + (g + 1) * SSM_STATE] = dcg
        ddec = _split_dot(rx_s[...], expand, NT) * dec
        dtot = _split_dot(trow_s[...], expand, NT)[0:1, :]
        dcs = dcs_col - dcs_row.T + _split_dot(yoff_s[...], expand, NT) - ddec
        dcs = dcs + jnp.where(row == SSM_CHUNK - 1, jnp.sum(ddec, axis=0, keepdims=True) + dtot, 0.0)
        da = _fdot((row <= col).astype(F32), dcs)
        dxdt = dxdt_s[...]
        ddt = da * a + _split_dot(dxdt * xs, expand, NT)
        dalog_ref[...] += jnp.sum(da * dt, axis=0, keepdims=True) * a
        ddtr = ddt * _sigmoid(dtr + bias_ref[...])
        ddtr_ref[...] = ddtr.astype(BF16)
        dbias_ref[...] += jnp.sum(ddtr, axis=0, keepdims=True)
        dxbc_ref[:, 0:SSM_INNER] = dxdt * dt_x + dyv * dexp_ref[...]

        @pl.when(pl.program_id(0) == last)
        def _():
            _exchange_phase("finish", *comm)

    out = pl.pallas_call(
        body, name="ssd_bwd", grid=(nc,),
        in_specs=[xs_s, bm_s, cm_s, dtr_s, sin_s, pl.BlockSpec((SSM_CHUNK, SSM_INNER), lambda c: (last - c, 0)),
                  vec, vec, wide, HBM] + [HBM] * n_ex,
        out_specs=[pl.BlockSpec((SSM_CHUNK, SSM_CONV_DIM), lambda c: (last - c, 0)), dtr_s, vec, vec, wide]
        + [HBM] * n_ex,
        out_shape=[jax.ShapeDtypeStruct((t, SSM_CONV_DIM), F32), jax.ShapeDtypeStruct(dproj.shape, dproj.dtype),
                   jax.ShapeDtypeStruct((1, LANES), F32), jax.ShapeDtypeStruct((1, LANES), F32),
                   jax.ShapeDtypeStruct((1, SSM_INNER), F32)]
        + [jax.ShapeDtypeStruct(p.shape, p.dtype) for p in exchange],
        input_output_aliases={9: 1},
        scratch_shapes=[pltpu.VMEM((SSM_PAIRS, SSM_STATE, LANES), F32),
                        pltpu.VMEM((SSM_CHUNK, SSM_INNER), F32), pltpu.VMEM((SSM_CHUNK, SSM_INNER), F32),
                        pltpu.VMEM((SSM_CHUNK, SSM_INNER), F32), pltpu.VMEM((SUBLANES, SSM_INNER), F32)]
        + _comm_scratch(n_ex),
        compiler_params=_params(("arbitrary",)),
    )(xbc_act, xbc_act, xbc_act, proj, s_in_all, dy, bias, alog, dexp, dproj, *exchange)
    return out[:5], out[5:]


def _group_rstd(y):
    n = SSM_INNER // SSM_GROUPS
    parts = []
    for g in range(SSM_GROUPS):
        yg = y[:, g * n:(g + 1) * n]
        r = lax.rsqrt(jnp.mean(yg * yg, axis=-1, keepdims=True) + RMS_EPS)
        parts.append(jnp.broadcast_to(r, yg.shape))
    return jnp.concatenate(parts, axis=1)


def _group_mean(v):
    n = SSM_INNER // SSM_GROUPS
    parts = []
    for g in range(SSM_GROUPS):
        vg = v[:, g * n:(g + 1) * n]
        parts.append(jnp.broadcast_to(jnp.mean(vg, axis=-1, keepdims=True), vg.shape))
    return jnp.concatenate(parts, axis=1)


def _ssm_post_fwd(y_ssd, proj, nw):
    t = proj.shape[0]
    n = SSM_INNER
    tr = _rows(t)

    def body(y_ref, z_ref, nw_ref, o_ref):
        z = z_ref[...]
        y = y_ref[...] * (z * _sigmoid(z))
        o_ref[...] = (y * _group_rstd(y) * nw_ref[...]).astype(BF16)

    row = pl.BlockSpec((tr, n), lambda i: (i, 0))
    return pl.pallas_call(
        body, name="ssm_post_fwd", grid=(t // tr,),
        in_specs=[row, pl.BlockSpec((tr, n), lambda i: (i, COL_Z // n)), pl.BlockSpec((1, n), lambda i: (0, 0))],
        out_specs=row, out_shape=jax.ShapeDtypeStruct((t, n), BF16),
        compiler_params=_params(("parallel",)),
    )(y_ssd, proj, nw)


def _ssm_post_bwd(y_ssd, proj, nw, dout, dproj):
    t = proj.shape[0]
    n = SSM_INNER
    tr = _rows(t)

    def body(y_ref, z_ref, nw_ref, do_ref, dproj_in, dy_ref, dz_ref, dnw_ref):
        i = pl.program_id(0)
        z = z_ref[...]
        sg = _sigmoid(z)
        sz = z * sg
        ys = y_ref[...]
        y = ys * sz
        rstd = _group_rstd(y)
        yn = y * rstd
        dov = do_ref[...]
        dyn = dov * nw_ref[...]
        dyg = rstd * (dyn - yn * _group_mean(dyn * yn))
        dy_ref[...] = dyg * sz
        dz_ref[...] = (dyg * ys * (sg * (1.0 + z * (1.0 - sg)))).astype(BF16)
        part = jnp.sum(dov * yn, axis=0, keepdims=True)

        @pl.when(i == 0)
        def _():
            dnw_ref[...] = part

        @pl.when(i > 0)
        def _():
            dnw_ref[...] += part

    row = pl.BlockSpec((tr, n), lambda i: (i, 0))
    vec = pl.BlockSpec((1, n), lambda i: (0, 0))
    return pl.pallas_call(
        body, name="ssm_post_bwd", grid=(t // tr,),
        in_specs=[row, pl.BlockSpec((tr, n), lambda i: (i, COL_Z // n)), vec, row, HBM],
        out_specs=[row, pl.BlockSpec((tr, n), lambda i: (i, COL_Z // n)), vec],
        out_shape=[jax.ShapeDtypeStruct((t, n), F32), jax.ShapeDtypeStruct(dproj.shape, dproj.dtype),
                   jax.ShapeDtypeStruct((1, n), F32)],
        input_output_aliases={4: 1},
        compiler_params=_params(("arbitrary",)),
    )(y_ssd, proj, nw, dout, dproj)


SCAN_UNROLL = 8
GELU_C = math.sqrt(2.0 / math.pi)
GELU_K = 0.044715


def _gelu_parts(y):
    th = jnp.tanh(GELU_C * (y + GELU_K * y * y * y))
    val = 0.5 * y * (1.0 + th)
    grad = 0.5 * (1.0 + th) + 0.5 * y * (1.0 - th * th) * GELU_C * (1.0 + 3.0 * GELU_K * y * y)
    return val, grad


def _scan_tiles(a_ref, b_ref, h_ref, n_rows, reverse):
    n_tiles = n_rows // SUBLANES
    shape = (SUBLANES, a_ref.shape[1])
    row = _iota(shape, 0)

    def in_tile(av, bv):
        for s in (1, 2, 4):
            if reverse:
                keep = row < SUBLANES - s
                a_sh = jnp.where(keep, pltpu.roll(av, SUBLANES - s, 0), 1.0)
                b_sh = jnp.where(keep, pltpu.roll(bv, SUBLANES - s, 0), 0.0)
            else:
                keep = row >= s
                a_sh = jnp.where(keep, pltpu.roll(av, s, 0), 1.0)
                b_sh = jnp.where(keep, pltpu.roll(bv, s, 0), 0.0)
            bv = av * b_sh + bv
            av = av * a_sh
        return av, bv

    def step(k, carry):
        first = (n_tiles // SCAN_UNROLL - 1 - k) if reverse else k
        tiles = [first * SCAN_UNROLL + j for j in range(SCAN_UNROLL)]
        if reverse:
            tiles = tiles[::-1]
        ats = [pl.ds(pl.multiple_of(tile * SUBLANES, SUBLANES), SUBLANES) for tile in tiles]
        scanned = [in_tile(a_ref[at, :], b_ref[at, :]) for at in ats]
        for at, (av, bv) in zip(ats, scanned):
            hv = bv + av * carry
            h_ref[at, :] = hv
            carry = hv[0:1, :] if reverse else hv[SUBLANES - 1:SUBLANES, :]
        return carry

    assert n_tiles % SCAN_UNROLL == 0, n_rows
    lax.fori_loop(0, n_tiles // SCAN_UNROLL, step, jnp.zeros((1, a_ref.shape[1]), F32))


def _lru_gates(xl, cw, cb, wr, br, wi, bi, lam):
    u = cb + cw[CONV_K - 1:CONV_K, :] * xl
    for k in range(CONV_K - 1):
        u = u + cw[k:k + 1, :] * _shift_down(xl, CONV_K - 1 - k)
    r = _sigmoid(_bdot(u, wr) + br)
    i = _sigmoid(_bdot(u, wi) + bi)
    sp = _softplus(-lam)
    la = -LRU_C * r * sp
    a = jnp.exp(la)
    mult = jnp.sqrt(-jnp.tanh(la) * (a * a + 1.0))
    return u, r, i, sp, a, mult


def _lru_specs(t):
    c0 = COL_LRU // LANES
    xl = pl.BlockSpec((t, LANES), lambda j: (0, c0 + 2 * j))
    yl = pl.BlockSpec((t, LANES), lambda j: (0, c0 + 2 * j + 1))
    col = pl.BlockSpec((t, LANES), lambda j: (0, j))
    cw = pl.BlockSpec((CONV_K, LANES), lambda j: (0, j))
    vec = pl.BlockSpec((1, LANES), lambda j: (0, j))
    wblk = pl.BlockSpec((1, LANES, LANES), lambda j: (j, 0, 0))
    return xl, yl, col, cw, vec, wblk


def _lru_fwd(proj, cw, cb, wr, br, wi, bi, lam, gather):
    t = proj.shape[0]
    xl_s, yl_s, col, cw_s, vec, wblk = _lru_specs(t)
    n = len(gather)

    def body(*refs):
        xl_ref, yl_ref, cw_ref, cb_ref, wr_ref, br_ref, wi_ref, bi_ref, lam_ref = refs[:9]
        o_ref, h_ref = refs[9 + n:11 + n]
        a_s, b_s = refs[11 + 2 * n:13 + 2 * n]
        comm = (refs[9:9 + n], refs[11 + n:11 + 2 * n]) + tuple(refs[13 + 2 * n:])
        j = pl.program_id(0)
        for step, phase in ((0, "start"), (LRU_BLOCKS - 2, "forward")):
            @pl.when(j == step)
            def _():
                _gather_phase(phase, *comm)

        u, r, i, sp, a, mult = _lru_gates(xl_ref[...], cw_ref[...], cb_ref[...], wr_ref[0], br_ref[...],
                                          wi_ref[0], bi_ref[...], lam_ref[...])
        a_s[...] = a
        b_s[...] = mult * (i * u)
        _scan_tiles(a_s, b_s, h_ref, t, reverse=False)
        o_ref[...] = (h_ref[...] * _gelu_parts(yl_ref[...])[0]).astype(BF16)

        @pl.when(j == LRU_BLOCKS - 1)
        def _():
            _gather_phase("finish", *comm)

    out = pl.pallas_call(
        body, name="lru_fwd", grid=(LRU_BLOCKS,),
        in_specs=[xl_s, yl_s, cw_s, vec, wblk, vec, wblk, vec, vec] + [HBM] * n,
        out_specs=[col, col] + [HBM] * n,
        out_shape=[jax.ShapeDtypeStruct((t, LRU_WIDTH), BF16), jax.ShapeDtypeStruct((t, LRU_WIDTH), F32)]
        + [jax.ShapeDtypeStruct((N_DEV,) + v.shape, v.dtype) for v in gather],
        scratch_shapes=[pltpu.VMEM((t, LANES), F32)] * 2 + _comm_scratch(n),
        compiler_params=_params(("arbitrary",), big=True),
    )(proj, proj, cw, cb, wr, br, wi, bi, lam, *gather)
    return out[:2], out[2:]


def _lru_bwd(proj, cw, cb, wr, br, wi, bi, lam, h_all, dout, dproj, exchange):
    t = proj.shape[0]
    xl_s, yl_s, col, cw_s, vec, wblk = _lru_specs(t)
    pair = pl.BlockSpec((t, 2 * LANES), lambda j: (0, COL_LRU // (2 * LANES) + j))
    n_ex = len(exchange)

    def body(*refs):
        xl_ref, yl_ref, cw_ref, cb_ref, wr_ref, br_ref, wi_ref, bi_ref, lam_ref, h_ref, do_ref = refs[:11]
        dxy_ref, dcw_ref, dcb_ref, dwr_ref, dbr_ref, dwi_ref, dbi_ref, dlam_ref = refs[12 + n_ex:20 + n_ex]
        a_s, b_s, g_s = refs[20 + 2 * n_ex:23 + 2 * n_ex]
        comm = (refs[12:12 + n_ex], refs[20 + n_ex:20 + 2 * n_ex]) + tuple(refs[23 + 2 * n_ex:])

        @pl.when(pl.program_id(0) == 0)
        def _():
            _exchange_phase("start", *comm)

        xl = xl_ref[...]
        cwv = cw_ref[...]
        lam = lam_ref[...]
        u, r, i, sp, a, mult = _lru_gates(xl, cwv, cb_ref[...], wr_ref[0], br_ref[...], wi_ref[0], bi_ref[...], lam)
        v = i * u
        gl, dgl = _gelu_parts(yl_ref[...])
        dov = do_ref[...]
        h = h_ref[...]
        dxy_ref[:, LANES:2 * LANES] = (dov * h * dgl).astype(BF16)
        b_s[...] = dov * gl
        a_s[...] = _shift_up(a, 1)
        _scan_tiles(a_s, b_s, g_s, t, reverse=True)
        g = g_s[...]
        da = g * _shift_down(h, 1)
        dmult = g * v
        dv = g * mult
        dla = da * a - dmult * (a * a) / mult
        dr = dla * (-LRU_C * sp)
        dsp = jnp.sum(dla * (-LRU_C * r), axis=0, keepdims=True)
        dlam_ref[...] = -dsp * _sigmoid(-lam)
        dpr = dr * r * (1.0 - r)
        dpi = dv * u * i * (1.0 - i)
        dbr_ref[...] = jnp.sum(dpr, axis=0, keepdims=True)
        dbi_ref[...] = jnp.sum(dpi, axis=0, keepdims=True)
        dwr_ref[0] = _bdot(u, dpr, TN)
        dwi_ref[0] = _bdot(u, dpi, TN)
        du = dv * i + _bdot(dpr, wr_ref[0], NT) + _bdot(dpi, wi_ref[0], NT)
        dxl = cwv[CONV_K - 1:CONV_K, :] * du
        for k in range(CONV_K - 1):
            dxl = dxl + cwv[k:k + 1, :] * _shift_up(du, CONV_K - 1 - k)
        dxy_ref[:, 0:LANES] = dxl.astype(BF16)
        for k in range(CONV_K):
            dcw_ref[k:k + 1, :] = jnp.sum(du * _shift_down(xl, CONV_K - 1 - k), axis=0, keepdims=True)
        dcb_ref[...] = jnp.sum(du, axis=0, keepdims=True)

        @pl.when(pl.program_id(0) == LRU_BLOCKS - 1)
        def _():
            _exchange_phase("finish", *comm)

    out = pl.pallas_call(
        body, name="lru_bwd", grid=(LRU_BLOCKS,),
        in_specs=[xl_s, yl_s, cw_s, vec, wblk, vec, wblk, vec, vec, col, col, HBM] + [HBM] * n_ex,
        out_specs=[pair, cw_s, vec, wblk, vec, wblk, vec, vec] + [HBM] * n_ex,
        input_output_aliases={11: 0},
        out_shape=[jax.ShapeDtypeStruct(dproj.shape, dproj.dtype),
                   jax.ShapeDtypeStruct((CONV_K, LRU_WIDTH), F32), jax.ShapeDtypeStruct((1, LRU_WIDTH), F32),
                   jax.ShapeDtypeStruct((LRU_BLOCKS, LANES, LANES), F32), jax.ShapeDtypeStruct((1, LRU_WIDTH), F32),
                   jax.ShapeDtypeStruct((LRU_BLOCKS, LANES, LANES), F32), jax.ShapeDtypeStruct((1, LRU_WIDTH), F32),
                   jax.ShapeDtypeStruct((1, LRU_WIDTH), F32)]
        + [jax.ShapeDtypeStruct(p.shape, p.dtype) for p in exchange],
        scratch_shapes=[pltpu.VMEM((t, LANES), F32)] * 3 + _comm_scratch(n_ex),
        compiler_params=_params(("arbitrary",), big=True),
    )(proj, proj, cw, cb, wr, br, wi, bi, lam, h_all, dout, dproj, *exchange)
    return out[:8], out[8:]


def _mesh_pos():
    return lax.axis_index("x"), lax.axis_index("y"), lax.axis_index("c")


HBM = pl.BlockSpec(memory_space=pl.ANY)


def _comm_scratch(n):
    return [pltpu.SemaphoreType.DMA((n, 7)), pltpu.SemaphoreType.DMA((n, 7)), pltpu.SemaphoreType.DMA((n,))]


def _gather_phase(phase, v_refs, out_refs, send_sems, recv_sems, local_sems):
    n = len(v_refs)
    x, y, c = _mesh_pos()
    me, sibling = (x, y, c), (x, y, 1 - c)
    chips = [(1 - x, y), (x, 1 - y), (1 - x, 1 - y)]

    def block(a, px, py, pc):
        return out_refs[a].at[4 * px + 2 * py + pc]

    def copy(a, k, blk, to, src=None):
        return pltpu.make_async_remote_copy(
            src_ref=block(a, *blk) if src is None else src, dst_ref=block(a, *blk),
            send_sem=send_sems.at[a, k], recv_sem=recv_sems.at[a, k], device_id=to, device_id_type=MESH)

    def own(a):
        return pltpu.make_async_copy(v_refs[a], block(a, *me), local_sems.at[a])

    def first(a):
        return ([copy(a, 0, me, sibling, src=v_refs[a])]
                + [copy(a, 1 + j, me, (*chip, c), src=v_refs[a]) for j, chip in enumerate(chips)])

    def forward(a, j):
        return copy(a, 4 + j, (*chips[j], c), sibling)

    if phase == "start":
        for a in range(n):
            own(a).start()
        for a in range(n):
            for cp in first(a):
                cp.start()
    elif phase == "forward":
        for j in range(3):
            for a in range(n):
                copy(a, 1 + j, (*chips[j], c), me).wait_recv()
                forward(a, j).start()
    else:
        for a in range(n):
            copy(a, 0, sibling, me).wait_recv()
            for j in range(3):
                copy(a, 4 + j, (*chips[j], 1 - c), me).wait_recv()
        for a in range(n):
            for cp in first(a) + [forward(a, j) for j in range(3)]:
                cp.wait_send()
            own(a).wait()


def _all_gather(vs, name):
    n = len(vs)

    def body(*refs):
        comm = (refs[:n], refs[n:2 * n]) + tuple(refs[2 * n:])
        for phase in ("start", "forward", "finish"):
            _gather_phase(phase, *comm)

    return pl.pallas_call(
        body, name=name,
        out_shape=[jax.ShapeDtypeStruct((N_DEV,) + v.shape, v.dtype) for v in vs],
        in_specs=[HBM] * n, out_specs=[HBM] * n, scratch_shapes=_comm_scratch(n),
    )(*vs)


def _run_copies(phase, local, remote):
    if phase == "start":
        for cp in local + remote:
            cp.start()
    else:
        for cp in remote:
            cp.wait()
        for cp in local:
            cp.wait()


def _exchange_phase(phase, p_refs, out_refs, send_sems, recv_sems, local_sems):
    n = len(p_refs)
    x, y, c = _mesh_pos()
    me = 4 * x + 2 * y + c
    local = [pltpu.make_async_copy(p_refs[a].at[me], out_refs[a].at[me], local_sems.at[a]) for a in range(n)]
    remote = []
    for k in range(1, N_DEV):
        px = (1 - x) if k & 4 else x
        py = (1 - y) if k & 2 else y
        pc = (1 - c) if k & 1 else c
        for a in range(n):
            remote.append(pltpu.make_async_remote_copy(
                src_ref=p_refs[a].at[4 * px + 2 * py + pc], dst_ref=out_refs[a].at[me],
                send_sem=send_sems.at[a, k - 1], recv_sem=recv_sems.at[a, k - 1],
                device_id=(px, py, pc), device_id_type=MESH))
    _run_copies(phase, local, remote)


def _chip_exchange_phase(phase, p_refs, out_refs, send_sems, recv_sems, local_sems):
    n = len(p_refs)
    x, y, c = _mesh_pos()
    me = 2 * x + y
    local = [pltpu.make_async_copy(p_refs[a].at[me], out_refs[a].at[me], local_sems.at[a]) for a in range(n)]
    remote = []
    for k in range(1, 4):
        px = (1 - x) if k & 2 else x
        py = (1 - y) if k & 1 else y
        for a in range(n):
            remote.append(pltpu.make_async_remote_copy(
                src_ref=p_refs[a].at[2 * px + py], dst_ref=out_refs[a].at[me],
                send_sem=send_sems.at[a, k - 1], recv_sem=recv_sems.at[a, k - 1],
                device_id=(px, py, c), device_id_type=MESH))
    _run_copies(phase, local, remote)


def _sibling_exchange(parts, name):
    chips = N_DEV // 2

    def body(p_ref, out_ref, send_sems, recv_sems):
        x, y, c = _mesh_pos()
        copies = [pltpu.make_async_remote_copy(
            src_ref=p_ref.at[2 * q + 1 - c], dst_ref=out_ref.at[q], send_sem=send_sems.at[q], recv_sem=recv_sems.at[q],
            device_id=(x, y, 1 - c), device_id_type=MESH) for q in range(chips)]
        _run_copies("start", [], copies)
        _run_copies("finish", [], copies)

    return pl.pallas_call(
        body, name=name, out_shape=jax.ShapeDtypeStruct((chips,) + parts.shape[1:], parts.dtype),
        in_specs=[HBM], out_specs=HBM,
        scratch_shapes=[pltpu.SemaphoreType.DMA((chips,)), pltpu.SemaphoreType.DMA((chips,))],
    )(parts)


def _pair_sum(mine, theirs, name):
    slots, rows, cols = mine.shape
    tc = 256

    def body(a_ref, b_ref, o_ref):
        o_ref[...] = (a_ref[...].astype(F32) + b_ref[...].astype(F32)).astype(o_ref.dtype)

    spec = pl.BlockSpec((1, rows, tc), lambda q, j: (q, 0, j))
    return pl.pallas_call(
        body, name=name, grid=(slots, cols // tc), in_specs=[spec, spec], out_specs=spec,
        out_shape=jax.ShapeDtypeStruct(mine.shape, mine.dtype),
        compiler_params=_params(("parallel", "parallel")),
    )(mine, theirs)


def _sum_sources(recvs, name):
    k = len(recvs)

    def body(*refs):
        for r_ref, o_ref in zip(refs[:k], refs[k:]):
            acc = r_ref[0].astype(F32)
            for s in range(1, r_ref.shape[0]):
                acc = acc + r_ref[s].astype(F32)
            o_ref[...] = acc

    return pl.pallas_call(
        body, name=name, out_shape=[jax.ShapeDtypeStruct(r.shape[1:], F32) for r in recvs],
        compiler_params=_params(),
    )(*recvs)


def _row_tile(rows):
    for tile in range(128, 15, -16):
        if rows % tile == 0:
            return tile
    return rows


def _adam_update(w, g, m, v):
    nm = ADAM_B1 * m + (1.0 - ADAM_B1) * g
    nv = ADAM_B2 * v + (1.0 - ADAM_B2) * (g * g)
    m_hat = nm / (1.0 - ADAM_B1 ** ADAM_STEP)
    v_hat = nv / (1.0 - ADAM_B2 ** ADAM_STEP)
    return -ADAM_LR * (m_hat / (jnp.sqrt(v_hat) + ADAM_EPS) + ADAM_WD * w), nm, nv


def _vector_offsets(widths):
    offsets, end = [], 0
    for c in widths:
        offsets.append(end)
        end += c + (-c) % LANES
    return offsets, end


def _adamw_small(vec_parts, vec_state, mat_grads, mat_state):
    widths = [w.shape[1] for w, _, _ in vec_state]
    offsets, total = _vector_offsets(widths)
    assert vec_parts.shape == (N_DEV, total), (vec_parts.shape, total)
    n_vec, n_mat = len(vec_state), len(mat_state)

    def body(*refs):
        r_ref = refs[0]
        vec_in = refs[1:1 + 3 * n_vec]
        mat_in = refs[1 + 3 * n_vec:1 + 3 * n_vec + 4 * n_mat]
        outs = refs[1 + 3 * n_vec + 4 * n_mat:]
        for i, (off, c) in enumerate(zip(offsets, widths)):
            g = r_ref[0:1, off:off + c]
            for s in range(1, N_DEV):
                g = g + r_ref[s:s + 1, off:off + c]
            w_ref, m_ref, v_ref = vec_in[3 * i:3 * i + 3]
            g_out, d_out, m_out, v_out = outs[4 * i:4 * i + 4]
            g_out[...] = g
            d_out[...], m_out[...], v_out[...] = _adam_update(w_ref[...], g, m_ref[...], v_ref[...])
        for j in range(n_mat):
            g_ref, w_ref, m_ref, v_ref = mat_in[4 * j:4 * j + 4]
            d_out, m_out, v_out = outs[4 * n_vec + 3 * j:4 * n_vec + 3 * j + 3]
            d_out[...], m_out[...], v_out[...] = _adam_update(w_ref[...], g_ref[...], m_ref[...], v_ref[...])

    args = [vec_parts] + [a for state in vec_state for a in state]
    for g, state in zip(mat_grads, mat_state):
        args += [g, *state]
    out_shape = [jax.ShapeDtypeStruct(w.shape, F32) for w, _, _ in vec_state for _ in range(4)]
    out_shape += [jax.ShapeDtypeStruct(w.shape, F32) for w, _, _ in mat_state for _ in range(3)]
    out = pl.pallas_call(body, name="adamw_replicated", out_shape=out_shape, compiler_params=_params())(*args)
    vec_out = [tuple(out[4 * i:4 * i + 4]) for i in range(n_vec)]
    mat_out = [tuple(out[4 * n_vec + 3 * j:4 * n_vec + 3 * j + 3]) for j in range(n_mat)]
    return vec_out, mat_out


def _adamw(w, recv, m, v, name):
    rows, width = w.shape
    n = recv.shape[0]
    if rows % 16 == 0 or width % 256:
        tr, tc = _row_tile(rows), width
    else:
        tr, tc = rows, 256

    def body(w_ref, r_ref, m_ref, v_ref, g_ref, d_ref, nm_ref, nv_ref):
        gv = r_ref[0].astype(F32)
        for s in range(1, n):
            gv = gv + r_ref[s].astype(F32)
        g_ref[...] = gv
        d_ref[...], nm_ref[...], nv_ref[...] = _adam_update(w_ref[...], gv, m_ref[...], v_ref[...])

    spec = pl.BlockSpec((tr, tc), lambda i, j: (i, j))
    shape = jax.ShapeDtypeStruct((rows, width), F32)
    return pl.pallas_call(
        body, name=name, grid=(rows // tr, width // tc),
        in_specs=[spec, pl.BlockSpec((n, tr, tc), lambda i, j: (0, i, j)), spec, spec],
        out_specs=[spec] * 4, out_shape=[shape] * 4,
        compiler_params=_params(("parallel", "parallel")),
    )(w, recv, m, v)


BIG_NAMES = ("w_in", "w_out_ssm", "w_out_lru", "w_out", "w_ffn_in", "w_ffn_out", "ssm_conv_w", "lru_conv_w")
TRANSPOSED = ("w_in", "w_ffn_in")
CONV_NAMES = ("ssm_conv_w", "lru_conv_w")
MATMUL_NAMES = BIG_NAMES[:6]
NEEDED_FIRST = ("w_in", "ssm_conv_w", "lru_conv_w")
GATHERED_IN_SSD = ("w_ffn_in",)
GATHERED_IN_LRU = ("w_ffn_out", "w_out_ssm", "w_out_lru", "w_out")
EXCHANGED_IN_SSD = ("w_ffn_in", "w_ffn_out")
EXCHANGED_IN_LRU = ("w_out_ssm", "w_out_lru", "w_out")
SMALL_VECTORS = ("norm1_w", "b_branch_gate", "ssm_conv_b", "ssm_dt_bias", "ssm_a_log", "ssm_d", "ssm_norm_w",
                 "lru_conv_b", "lru_b_r", "lru_b_i", "lru_lambda", "norm2_w", "norm_f_w")
SMALL_MATRICES = ("lru_w_r", "lru_w_i")


def _col_shards(full):
    rows, cols = full.shape
    return full.reshape(rows, N_DEV, cols // N_DEV).transpose(1, 0, 2)


def _from_col_shards(g):
    n, rows, w = g.shape
    return g.transpose(1, 0, 2).reshape(rows, n * w)


def kernel(x, norm1_w, w_in, b_branch_gate, ssm_conv_w, ssm_conv_b, ssm_dt_bias, ssm_a_log, ssm_d, ssm_norm_w, w_out_ssm, lru_conv_w, lru_conv_b, lru_w_r, lru_b_r, lru_w_i, lru_b_i, lru_lambda, w_out_lru, w_out, norm2_w, w_ffn_in, w_ffn_out, norm_f_w, loss_target, m_norm1_w, m_w_in, m_b_branch_gate, m_ssm_conv_w, m_ssm_conv_b, m_ssm_dt_bias, m_ssm_a_log, m_ssm_d, m_ssm_norm_w, m_w_out_ssm, m_lru_conv_w, m_lru_conv_b, m_lru_w_r, m_lru_b_r, m_lru_w_i, m_lru_b_i, m_lru_lambda, m_w_out_lru, m_w_out, m_norm2_w, m_w_ffn_in, m_w_ffn_out, m_norm_f_w, v_norm1_w, v_w_in, v_b_branch_gate, v_ssm_conv_w, v_ssm_conv_b, v_ssm_dt_bias, v_ssm_a_log, v_ssm_d, v_ssm_norm_w, v_w_out_ssm, v_lru_conv_w, v_lru_conv_b, v_lru_w_r, v_lru_b_r, v_lru_w_i, v_lru_b_i, v_lru_lambda, v_w_out_lru, v_w_out, v_norm2_w, v_w_ffn_in, v_w_ffn_out, v_norm_f_w):
    given = dict(locals())
    weights = {n: given[n] for n in BIG_NAMES + SMALL_VECTORS + SMALL_MATRICES}
    t = x.shape[1]
    xt = x[0]
    tgt = loss_target[0]

    def local(n, a):
        return a[0].T if n in TRANSPOSED else a[0]

    def as_output(n, a):
        return a.T[None] if n in TRANSPOSED else a[None]

    def shard(n):
        s = local(n, weights[n])
        return s.astype(BF16) if n in MATMUL_NAMES else s

    def unshard(n, g):
        return _from_col_shards(g) if n in CONV_NAMES else g.reshape(-1, g.shape[-1])

    def grad_slices(n):
        g = grads[n]
        return (_col_shards(g) if n in CONV_NAMES else g.reshape(N_DEV, -1, g.shape[-1])).astype(BF16)

    gathered = _all_gather([shard(n) for n in NEEDED_FIRST], "gather_in_weights")
    full = {n: unshard(n, g) for n, g in zip(NEEDED_FIRST, gathered)}
    ssm_cw, lru_cw = full["ssm_conv_w"], full["lru_conv_w"]
    wi_t = full["w_in"]
    lru_rows = wi_t[ORIG_LRU_X:].reshape(2, LRU_BLOCKS, LANES, D_MODEL).transpose(1, 0, 2, 3)
    w_pt = jnp.concatenate([wi_t[:ORIG_DT], lru_rows.reshape(2 * LRU_WIDTH, D_MODEL), wi_t[ORIG_DT:ORIG_LRU_X],
                            jnp.zeros((PROJ_W - IN_PROJ, D_MODEL), BF16)], axis=0)

    def pad_heads(a):
        return jnp.pad(a.reshape(1, SSM_HEADS), ((0, 0), (0, LANES - SSM_HEADS)))

    dt_bias_p = pad_heads(ssm_dt_bias)
    a_log_p = pad_heads(ssm_a_log)
    d_exp = jnp.repeat(ssm_d.reshape(SSM_HEADS), SSM_HEAD_DIM).reshape(1, SSM_INNER)
    lru_wr, lru_wi = lru_w_r[0], lru_w_i[0]

    hn1 = _rmsnorm_fwd(xt, norm1_w, "norm1_fwd")
    proj = _mm(hn1, w_pt, tb=True, name="in_proj")
    xbc_act = _ssm_conv_fwd(proj, ssm_cw, ssm_conv_b)
    (y_ssd, s_in_all), gathered = _ssd_fwd(xbc_act, proj, dt_bias_p, a_log_p, d_exp,
                                           gather=[shard(n) for n in GATHERED_IN_SSD])
    full.update({n: unshard(n, g) for n, g in zip(GATHERED_IN_SSD, gathered)})
    (l_out, h_lru), gathered = _lru_fwd(proj, lru_cw, lru_conv_b, lru_wr, lru_b_r, lru_wi, lru_b_i, lru_lambda,
                                        gather=[shard(n) for n in GATHERED_IN_LRU])
    full.update({n: unshard(n, g) for n, g in zip(GATHERED_IN_LRU, gathered)})
    y_pre = _ssm_post_fwd(y_ssd, proj, ssm_norm_w)
    y_ssm = _mm(y_pre, full["w_out_ssm"], name="out_ssm")
    y_lru = _mm(l_out, full["w_out_lru"], name="out_lru")
    merged = _merge_fwd(proj, b_branch_gate, y_ssm, y_lru)
    h1, hn2 = _mm(merged, full["w_out"], epilogue=_residual_norm_epilogue(xt, norm2_w), name="out_proj")
    gate, up, act = _ffn_in_swiglu(hn2, full["w_ffn_in"])

    grads = {}
    dh2, grads["norm_f_w"], loss_cols = _mm(
        act, full["w_ffn_out"], epilogue=_loss_epilogue(h1, norm_f_w.reshape(1, D_MODEL), tgt), name="ffn_out")
    loss = lax.psum(0.5 * jnp.sum(loss_cols) / D_MODEL, AXES)
    dact = _mm(dh2, full["w_ffn_out"], tb=True, out_dtype=BF16, name="d_act")
    grads["w_ffn_out"] = _mm(act, dh2, ta=True, out_dtype=BF16, name="dw_ffn_out")
    dgu = _swiglu_bwd(gate, up, dact)
    dh1, grads["norm2_w"] = _mm(dgu, full["w_ffn_in"], epilogue=_norm_bwd_epilogue(h1, norm2_w, dh2), name="d_hn2")
    grads["w_ffn_in"] = _mm(dgu, hn2, ta=True, out_dtype=BF16, name="dw_ffn_in")
    dmerged = _mm(dh1, full["w_out"], tb=True, name="d_merged")
    grads["w_out"] = _mm(merged, dh1, ta=True, out_dtype=BF16, name="dw_out")
    dy_ssm, dy_lru, dproj, grads["b_branch_gate"] = _merge_bwd(proj, b_branch_gate, y_ssm, y_lru, dmerged)
    dy_pre = _mm(dy_ssm, full["w_out_ssm"], tb=True, name="d_y_pre")
    grads["w_out_ssm"] = _mm(y_pre, dy_ssm, ta=True, out_dtype=BF16, name="dw_out_ssm")
    dl_out = _mm(dy_lru, full["w_out_lru"], tb=True, name="d_l_out")
    grads["w_out_lru"] = _mm(l_out, dy_lru, ta=True, out_dtype=BF16, name="dw_out_lru")
    dy_ssd, dproj, grads["ssm_norm_w"] = _ssm_post_bwd(y_ssd, proj, ssm_norm_w, dy_pre, dproj)
    (dxbc_act, dproj, dbias, dalog, ddcol), recv_in_ssd = _ssd_bwd(
        xbc_act, proj, s_in_all, dy_ssd, dt_bias_p, a_log_p, d_exp, dproj,
        exchange=[grad_slices(n) for n in EXCHANGED_IN_SSD])
    grads["ssm_dt_bias"] = dbias[:, :SSM_HEADS]
    grads["ssm_a_log"] = dalog[:, :SSM_HEADS]
    grads["ssm_d"] = ddcol.reshape(SSM_HEADS, SSM_HEAD_DIM).sum(axis=1).reshape(1, SSM_HEADS)
    dproj, grads["ssm_conv_w"], grads["ssm_conv_b"] = _ssm_conv_bwd(proj, ssm_cw, ssm_conv_b, dxbc_act, dproj)
    ((dproj, grads["lru_conv_w"], grads["lru_conv_b"], dwr, grads["lru_b_r"], dwi, grads["lru_b_i"],
      grads["lru_lambda"]), recv_in_lru) = _lru_bwd(
        proj, lru_cw, lru_conv_b, lru_wr, lru_b_r, lru_wi, lru_b_i, lru_lambda, h_lru, dl_out, dproj,
        exchange=[grad_slices(n) for n in EXCHANGED_IN_LRU])
    grads["lru_w_r"], grads["lru_w_i"] = dwr[None], dwi[None]
    dwpt = _mm(dproj, hn1, ta=True, out_dtype=BF16, name="dw_in")
    lru_rows = dwpt[COL_LRU:COL_DT].reshape(LRU_BLOCKS, 2, LANES, D_MODEL).transpose(1, 0, 2, 3)
    grads["w_in"] = jnp.concatenate([dwpt[:ORIG_DT], dwpt[COL_DT:COL_DT + SSM_HEADS],
                                     lru_rows.reshape(2 * LRU_WIDTH, D_MODEL)], axis=0)
    w_in_parts = grad_slices("w_in")
    from_sibling = _sibling_exchange(w_in_parts, "sibling_exchange_dw_in")
    for_my_core = lax.dynamic_index_in_dim(w_in_parts.reshape((N_DEV // 2, 2) + w_in_parts.shape[1:]),
                                           lax.axis_index("c"), axis=1, keepdims=False)
    chip_parts = [_pair_sum(for_my_core, from_sibling, "pair_sum_dw_in")]
    direct = [grad_slices(n) for n in CONV_NAMES] + [grads[n].reshape(N_DEV, -1, LANES) for n in SMALL_MATRICES]
    (grad_x, grads["norm1_w"]), (recv_w_in, recv_direct) = _mm(
        dproj, w_pt, epilogue=_norm_bwd_epilogue(xt, norm1_w, dh1), name="d_hn1",
        hosted=[(_chip_exchange_phase, chip_parts, chip_parts), (_exchange_phase, direct, direct)])
    recv_first = list(recv_w_in) + list(recv_direct[:len(CONV_NAMES)])
    recv_mats = recv_direct[len(CONV_NAMES):]

    def as_rows(n, a):
        return a.reshape(1, -1) if n in SMALL_VECTORS else a.reshape(-1, LANES)

    vec_g = jnp.concatenate([jnp.pad(as_rows(n, grads[n]), ((0, 0), (0, (-grads[n].size) % LANES)))
                             for n in SMALL_VECTORS], axis=1)
    gathered = _all_gather([vec_g] + list(_sum_sources(recv_mats, "sum_gate_matrix_grads")), "gather_small_grads")
    vec_parts = gathered[0].reshape(N_DEV, -1)
    mat_g = [g.reshape(-1, LANES) for g in gathered[1:]]

    recv = dict(zip(EXCHANGED_IN_SSD + EXCHANGED_IN_LRU + NEEDED_FIRST,
                    list(recv_in_ssd) + list(recv_in_lru) + list(recv_first)))
    big_out = {n: _adamw(local(n, weights[n]), recv[n], local(n, given["m_" + n]), local(n, given["v_" + n]),
                         "adamw_" + n) for n in BIG_NAMES}

    def state(n):
        return tuple(as_rows(n, given[p + n]) for p in ("", "m_", "v_"))

    vec_out, mat_out = _adamw_small(vec_parts, [state(n) for n in SMALL_VECTORS],
                                    mat_g, [state(n) for n in SMALL_MATRICES])
    small_out = dict(zip(SMALL_VECTORS, vec_out))
    small_out.update({n: (g,) + out for n, g, out in zip(SMALL_MATRICES, mat_g, mat_out)})

    order = list(given)[1:24]
    results = []
    for q in range(4):
        vals = {n: as_output(n, big_out[n][q]) for n in BIG_NAMES}
        vals.update({n: out[q].reshape(weights[n].shape) for n, out in small_out.items()})
        results.extend(vals[n] for n in order)
    return (loss, grad_x[None], *results)
```

```python
import math

import jax
import jax.numpy as jnp
from jax import lax
from jax.experimental import pallas as pl
from jax.experimental.pallas import tpu as pltpu

F32 = jnp.float32
BF16 = jnp.bfloat16
HIGHEST = lax.Precision.HIGHEST
MESH = pl.DeviceIdType.MESH
AXES = ("x", "y", "c")
N_DEV = 8

D_MODEL = 1024
SSM_INNER = 2048
SSM_HEADS = 32
SSM_HEAD_DIM = 64
SSM_GROUPS = 4
SSM_STATE = 128
SSM_BC = SSM_GROUPS * SSM_STATE
SSM_CONV_DIM = SSM_INNER + 2 * SSM_BC
SSM_CHUNK = 128
SSM_PAIRS = SSM_HEADS // 2
CONV_K = 4
LRU_WIDTH = 1280
LRU_BLOCKS = 10
LRU_C = 8.0
FFN_HIDDEN = 2816
RMS_EPS = 1e-6
IN_PROJ = 9760

COL_GATES = 0
COL_Z = 2048
COL_XBC = 4096
COL_LRU = 7168
COL_DT = 9728
PROJ_W = 9856
ORIG_DT = 7168
ORIG_LRU_X = 7200
ORIG_LRU_Y = 8480

ADAM_LR = 0.001
ADAM_B1 = 0.9
ADAM_B2 = 0.999
ADAM_EPS = 1e-08
ADAM_WD = 0.01
ADAM_STEP = 10

LANES = 128
SUBLANES = 8
V7X_VMEM_BYTES = 64 * 1024 * 1024
VMEM_LIMIT = V7X_VMEM_BYTES * 3 // 4
VMEM_LIMIT_BIG = V7X_VMEM_BYTES * 15 // 16

NT = (((1,), (1,)), ((), ()))
TN = (((0,), (0,)), ((), ()))


def _params(sem=None, big=False):
    return pltpu.CompilerParams(dimension_semantics=sem,
                                vmem_limit_bytes=VMEM_LIMIT_BIG if big else VMEM_LIMIT)


def _blk(dim, cap):
    if dim <= cap:
        return dim
    for m in range(cap // LANES, 0, -1):
        if dim % (m * LANES) == 0:
            return m * LANES
    raise ValueError(f"no block for {dim}")


def _rows(t):
    return min(t, 256)


def _sigmoid(v):
    return 1.0 / (1.0 + jnp.exp(-v))


def _softplus(v):
    e = jnp.exp(-jnp.abs(v))
    u = 1.0 + e
    log1p = jnp.where(u == 1.0, e, jnp.log(u) * e / jnp.where(u == 1.0, 1.0, u - 1.0))
    return jnp.maximum(v, 0.0) + log1p


def _iota(shape, dim):
    return lax.broadcasted_iota(jnp.int32, shape, dim)


def _shift_down(v, s):
    if s == 0:
        return v
    return jnp.where(_iota(v.shape, 0) >= s, pltpu.roll(v, s, 0), 0.0)


def _shift_up(v, s):
    if s == 0:
        return v
    n = v.shape[0]
    return jnp.where(_iota(v.shape, 0) < n - s, pltpu.roll(v, n - s, 0), 0.0)


def _bdot(a, b, dn=None):
    a = a.astype(BF16)
    b = b.astype(BF16)
    if dn is None:
        return jnp.dot(a, b, preferred_element_type=F32)
    return lax.dot_general(a, b, dn, preferred_element_type=F32)


def _split_dot(a, e, dn=None):
    hi = a.astype(BF16)
    lo = (a - hi.astype(F32)).astype(BF16)
    return _bdot(hi, e, dn) + _bdot(lo, e, dn)


def _fdot(a, b, dn=None):
    if dn is None:
        return jnp.dot(a, b, precision=HIGHEST, preferred_element_type=F32)
    return lax.dot_general(a, b, dn, precision=HIGHEST, preferred_element_type=F32)


def _mm(a, b, *, ta=False, tb=False, add=None, hosted=(), out_dtype=F32, epilogue=None, name):
    if ta:
        kdim, m = a.shape
    else:
        m, kdim = a.shape
    if tb:
        n, k2 = b.shape
    else:
        k2, n = b.shape
    assert kdim == k2, (a.shape, b.shape, ta, tb)
    if epilogue is None:
        rows, vecs, row_dtypes, n_vec_out = ([] if add is None else [add]), [], [out_dtype], 0

        def finish(r, row_vals, vec_vals):
            return ((r + row_vals[0]) if row_vals else r,), ()
    else:
        assert add is None
        finish, rows, vecs, row_dtypes, n_vec_out = epilogue
    bm, bn, bk = _blk(m, 1408 if epilogue is None else 512), _blk(n, 1408), _blk(kdim, 1408)
    grid = (m // bm, n // bn, kdim // bk)
    nk = grid[2]
    assert n_vec_out == 0 or grid[1] == 1, "column sums are accumulated over the row tiles of whole rows"
    dn = (((0 if ta else 1,), (1 if tb else 0,)), ((), ()))
    n_in = 2 + len(rows) + len(vecs)
    n_out = len(row_dtypes) + n_vec_out
    sizes = [len(arrays) for _, arrays, _ in hosted]
    n_ex = sum(sizes)

    def body(*refs):
        a_ref, b_ref = refs[:2]
        row_refs, vec_refs = refs[2:2 + len(rows)], refs[2 + len(rows):n_in]
        out_refs = refs[n_in + n_ex:n_in + n_ex + n_out]
        acc = refs[n_in + 2 * n_ex + n_out]
        comms, at = [], 0
        for g, size in enumerate(sizes):
            sems = refs[n_in + 2 * n_ex + n_out + 1 + 3 * g:n_in + 2 * n_ex + n_out + 4 + 3 * g]
            comms.append((refs[n_in + at:n_in + at + size],
                          refs[n_in + n_ex + n_out + at:n_in + n_ex + n_out + at + size]) + tuple(sems))
            at += size
        step = (pl.program_id(0) * grid[1] + pl.program_id(1)) * nk + pl.program_id(2)
        k = pl.program_id(2)
        if n_ex:
            @pl.when(step == 0)
            def _():
                for (phase_fn, _, _), comm in zip(hosted, comms):
                    phase_fn("start", *comm)

        @pl.when(k == 0)
        def _():
            acc[...] = jnp.zeros_like(acc)

        acc[...] += lax.dot_general(a_ref[...].astype(BF16), b_ref[...].astype(BF16), dn,
                                    preferred_element_type=F32)

        @pl.when(k == nk - 1)
        def _():
            row_outs, col_sums = finish(acc[...], [r[...] for r in row_refs], [v[...] for v in vec_refs])
            for o_ref, val in zip(out_refs, row_outs):
                o_ref[...] = val.astype(o_ref.dtype)
            for o_ref, val in zip(out_refs[len(row_dtypes):], col_sums):
                @pl.when(pl.program_id(0) == 0)
                def _():
                    o_ref[...] = val

                @pl.when(pl.program_id(0) > 0)
                def _():
                    o_ref[...] += val

        if n_ex:
            @pl.when(step == grid[0] * grid[1] * nk - 1)
            def _():
                for (phase_fn, _, _), comm in zip(hosted, comms):
                    phase_fn("finish", *comm)

    a_spec = pl.BlockSpec((bk, bm), lambda i, j, k: (k, i)) if ta else pl.BlockSpec((bm, bk), lambda i, j, k: (i, k))
    b_spec = pl.BlockSpec((bn, bk), lambda i, j, k: (j, k)) if tb else pl.BlockSpec((bk, bn), lambda i, j, k: (k, j))
    o_spec = pl.BlockSpec((bm, bn), lambda i, j, k: (i, j))
    v_spec = pl.BlockSpec((1, bn), lambda i, j, k: (0, j))
    in_specs = [a_spec, b_spec] + [o_spec] * len(rows) + [v_spec] * len(vecs) + [HBM] * n_ex
    args = [a, b] + list(rows) + list(vecs) + [p for _, arrays, _ in hosted for p in arrays]
    sequential = n_ex or n_vec_out
    out = pl.pallas_call(
        body, name=name, grid=grid,
        in_specs=in_specs, out_specs=[o_spec] * len(row_dtypes) + [v_spec] * n_vec_out + [HBM] * n_ex,
        out_shape=[jax.ShapeDtypeStruct((m, n), dt) for dt in row_dtypes]
        + [jax.ShapeDtypeStruct((1, n), F32)] * n_vec_out
        + [jax.ShapeDtypeStruct(r.shape, r.dtype) for _, _, results in hosted for r in results],
        scratch_shapes=[pltpu.VMEM((bm, bn), F32)] + [s for size in sizes for s in _comm_scratch(size)],
        compiler_params=_params(("arbitrary",) * 3 if sequential else ("parallel", "parallel", "arbitrary")),
    )(*args)
    result = out[0] if n_out == 1 else tuple(out[:n_out])
    if not n_ex:
        return result
    received, at = [], n_out
    for size in sizes:
        received.append(out[at:at + size])
        at += size
    return result, received


def _rmsnorm_fwd(x, w, name):
    t, d = x.shape
    tr = _rows(t)

    def body(x_ref, w_ref, o_ref):
        xv = x_ref[...]
        rstd = lax.rsqrt(jnp.mean(xv * xv, axis=-1, keepdims=True) + RMS_EPS)
        o_ref[...] = (xv * rstd * w_ref[...]).astype(BF16)

    return pl.pallas_call(
        body, name=name, grid=(t // tr,),
        in_specs=[pl.BlockSpec((tr, d), lambda i: (i, 0)), pl.BlockSpec((1, d), lambda i: (0, 0))],
        out_specs=pl.BlockSpec((tr, d), lambda i: (i, 0)),
        out_shape=jax.ShapeDtypeStruct((t, d), BF16),
        compiler_params=_params(("parallel",)),
    )(x, w)


def _normalize(h):
    rstd = lax.rsqrt(jnp.mean(h * h, axis=-1, keepdims=True) + RMS_EPS)
    return rstd, h * rstd


def _residual_norm_epilogue(x, w):
    def finish(r, rows, vecs):
        h = r + rows[0]
        return (h, _normalize(h)[1] * vecs[0]), ()

    return finish, [x], [w], [F32, BF16], 0


def _norm_bwd_epilogue(x, w, dres):
    def finish(r, rows, vecs):
        rstd, xhat = _normalize(rows[0])
        dxhat = r * vecs[0]
        m = jnp.mean(dxhat * xhat, axis=-1, keepdims=True)
        return (rstd * (dxhat - xhat * m) + rows[1],), (jnp.sum(r * xhat, axis=0, keepdims=True),)

    return finish, [x, dres], [w], [F32], 1


def _loss_epilogue(h1, w, tgt):
    d = h1.shape[1]

    def finish(r, rows, vecs):
        rstd, xhat = _normalize(r + rows[0])
        err = xhat * vecs[0] - rows[1]
        dyv = err * (1.0 / d)
        dxhat = dyv * vecs[0]
        m = jnp.mean(dxhat * xhat, axis=-1, keepdims=True)
        return ((rstd * (dxhat - xhat * m),),
                (jnp.sum(dyv * xhat, axis=0, keepdims=True), jnp.sum(err * err, axis=0, keepdims=True)))

    return finish, [h1, tgt], [w], [F32], 2


def _merge_fwd(proj, bg, ys, yl):
    t = proj.shape[0]
    d = D_MODEL
    tr = _rows(t)

    def body(ps_ref, pl_ref, bg_ref, ys_ref, yl_ref, o_ref):
        gs = _sigmoid(ps_ref[...] + bg_ref[:, 0:d])
        gl = _sigmoid(pl_ref[...] + bg_ref[:, d:2 * d])
        o_ref[...] = (gs * ys_ref[...] + gl * yl_ref[...]).astype(BF16)

    row = pl.BlockSpec((tr, d), lambda i: (i, 0))
    return pl.pallas_call(
        body, name="merge_fwd", grid=(t // tr,),
        in_specs=[row, pl.BlockSpec((tr, d), lambda i: (i, 1)), pl.BlockSpec((1, 2 * d), lambda i: (0, 0)), row, row],
        out_specs=row, out_shape=jax.ShapeDtypeStruct((t, d), BF16),
        compiler_params=_params(("parallel",)),
    )(proj, proj, bg, ys, yl)


def _merge_bwd(proj, bg, ys, yl, dm):
    t = proj.shape[0]
    d = D_MODEL
    tr = _rows(t)

    def body(ps_ref, pl_ref, bg_ref, ys_ref, yl_ref, dm_ref, dys_ref, dyl_ref, dg_ref, dbg_ref):
        i = pl.program_id(0)
        gs = _sigmoid(ps_ref[...] + bg_ref[:, 0:d])
        gl = _sigmoid(pl_ref[...] + bg_ref[:, d:2 * d])
        dmv = dm_ref[...]
        dys_ref[...] = (dmv * gs).astype(BF16)
        dyl_ref[...] = (dmv * gl).astype(BF16)
        dgs = dmv * ys_ref[...] * gs * (1.0 - gs)
        dgl = dmv * yl_ref[...] * gl * (1.0 - gl)
        dg_ref[:, 0:d] = dgs.astype(BF16)
        dg_ref[:, d:2 * d] = dgl.astype(BF16)

        @pl.when(i == 0)
        def _():
            dbg_ref[...] = jnp.zeros_like(dbg_ref)

        dbg_ref[:, 0:d] += jnp.sum(dgs, axis=0, keepdims=True)
        dbg_ref[:, d:2 * d] += jnp.sum(dgl, axis=0, keepdims=True)

    row = pl.BlockSpec((tr, d), lambda i: (i, 0))
    wide = pl.BlockSpec((tr, 2 * d), lambda i: (i, 0))
    vec = pl.BlockSpec((1, 2 * d), lambda i: (0, 0))
    return pl.pallas_call(
        body, name="merge_bwd", grid=(t // tr,),
        in_specs=[row, pl.BlockSpec((tr, d), lambda i: (i, 1)), vec, row, row, row],
        out_specs=[row, row, wide, vec],
        out_shape=[jax.ShapeDtypeStruct((t, d), BF16), jax.ShapeDtypeStruct((t, d), BF16),
                   jax.ShapeDtypeStruct((t, PROJ_W), BF16), jax.ShapeDtypeStruct((1, 2 * d), F32)],
        compiler_params=_params(("arbitrary",)),
    )(proj, proj, bg, ys, yl, dm)


def _ffn_in_swiglu(hn, wt):
    t, d = hn.shape
    f = FFN_HIDDEN
    bm, bn = _blk(t, 1024), _blk(f, 1408)
    nj = f // bn

    def body(a_ref, wg_ref, wu_ref, g_ref, u_ref, act_ref):
        a = a_ref[...]
        g = _bdot(a, wg_ref[...], NT)
        u = _bdot(a, wu_ref[...], NT)
        g_ref[...] = g.astype(BF16)
        u_ref[...] = u.astype(BF16)
        act_ref[...] = (g * _sigmoid(g) * u).astype(BF16)

    out = pl.BlockSpec((bm, bn), lambda i, j: (i, j))
    shape = jax.ShapeDtypeStruct((t, f), BF16)
    return pl.pallas_call(
        body, name="ffn_in_swiglu", grid=(t // bm, nj),
        in_specs=[pl.BlockSpec((bm, d), lambda i, j: (i, 0)), pl.BlockSpec((bn, d), lambda i, j: (j, 0)),
                  pl.BlockSpec((bn, d), lambda i, j: (nj + j, 0))],
        out_specs=[out, out, out], out_shape=[shape, shape, shape],
        compiler_params=_params(("parallel", "parallel")),
    )(hn, wt, wt)


def _swiglu_bwd(g_all, u_all, dact):
    t, f = g_all.shape
    tr = _rows(t)

    def body(g_ref, u_ref, da_ref, o_ref):
        g = g_ref[...].astype(F32)
        sg = _sigmoid(g)
        da = da_ref[...].astype(F32)
        o_ref[:, 0:f] = (da * u_ref[...].astype(F32) * (sg * (1.0 + g * (1.0 - sg)))).astype(BF16)
        o_ref[:, f:2 * f] = (da * g * sg).astype(BF16)

    row = pl.BlockSpec((tr, f), lambda i: (i, 0))
    return pl.pallas_call(
        body, name="swiglu_bwd", grid=(t // tr,),
        in_specs=[row, row, row],
        out_specs=pl.BlockSpec((tr, 2 * f), lambda i: (i, 0)),
        out_shape=jax.ShapeDtypeStruct((t, 2 * f), BF16),
        compiler_params=_params(("parallel",)),
    )(g_all, u_all, dact)


def _conv_pre(xv, wv, bv):
    pre = bv + wv[CONV_K - 1:CONV_K, :] * xv
    for k in range(CONV_K - 1):
        pre = pre + wv[k:k + 1, :] * _shift_down(xv, CONV_K - 1 - k)
    return pre


def _ssm_conv_fwd(proj, w, b):
    t = proj.shape[0]
    nb = SSM_CONV_DIM // LANES
    c0 = COL_XBC // LANES

    def body(x_ref, w_ref, b_ref, o_ref):
        pre = _conv_pre(x_ref[...], w_ref[...], b_ref[...])
        o_ref[...] = (pre * _sigmoid(pre)).astype(BF16)

    return pl.pallas_call(
        body, name="ssm_conv_fwd", grid=(nb,),
        in_specs=[pl.BlockSpec((t, LANES), lambda j: (0, c0 + j)), pl.BlockSpec((CONV_K, LANES), lambda j: (0, j)),
                  pl.BlockSpec((1, LANES), lambda j: (0, j))],
        out_specs=pl.BlockSpec((t, LANES), lambda j: (0, j)),
        out_shape=jax.ShapeDtypeStruct((t, SSM_CONV_DIM), BF16),
        compiler_params=_params(("parallel",)),
    )(proj, w, b)


def _ssm_conv_bwd(proj, w, b, dact, dproj):
    t = proj.shape[0]
    nb = SSM_CONV_DIM // LANES
    c0 = COL_XBC // LANES

    def body(x_ref, w_ref, b_ref, da_ref, dproj_in, dx_ref, dw_ref, db_ref):
        xv = x_ref[...]
        wv = w_ref[...]
        pre = _conv_pre(xv, wv, b_ref[...])
        sg = _sigmoid(pre)
        dpre = da_ref[...] * (sg * (1.0 + pre * (1.0 - sg)))
        dx = wv[CONV_K - 1:CONV_K, :] * dpre
        for k in range(CONV_K - 1):
            dx = dx + wv[k:k + 1, :] * _shift_up(dpre, CONV_K - 1 - k)
        dx_ref[...] = dx.astype(BF16)
        for k in range(CONV_K):
            dw_ref[k:k + 1, :] = jnp.sum(dpre * _shift_down(xv, CONV_K - 1 - k), axis=0, keepdims=True)
        db_ref[...] = jnp.sum(dpre, axis=0, keepdims=True)

    col = pl.BlockSpec((t, LANES), lambda j: (0, j))
    wsp = pl.BlockSpec((CONV_K, LANES), lambda j: (0, j))
    bsp = pl.BlockSpec((1, LANES), lambda j: (0, j))
    return pl.pallas_call(
        body, name="ssm_conv_bwd", grid=(nb,),
        in_specs=[pl.BlockSpec((t, LANES), lambda j: (0, c0 + j)), wsp, bsp, col, HBM],
        out_specs=[pl.BlockSpec((t, LANES), lambda j: (0, c0 + j)), wsp, bsp],
        out_shape=[jax.ShapeDtypeStruct(dproj.shape, dproj.dtype), jax.ShapeDtypeStruct((CONV_K, SSM_CONV_DIM), F32),
                   jax.ShapeDtypeStruct((1, SSM_CONV_DIM), F32)],
        input_output_aliases={4: 0},
        compiler_params=_params(("parallel",)),
    )(proj, w, b, dact, dproj)


def _ssd_chunk_terms(dtr, bias, alog):
    a = -jnp.exp(alog)
    dt = _softplus(dtr + bias)
    row = _iota((SSM_CHUNK, SSM_CHUNK), 0)
    col = _iota((SSM_CHUNK, SSM_CHUNK), 1)
    tri = (row >= col).astype(F32)
    cs = _fdot(tri, dt * a)
    dec = jnp.exp(cs[SSM_CHUNK - 1:SSM_CHUNK, :] - cs)
    ecs = jnp.exp(cs)
    off = _iota((LANES, SSM_INNER), 1) - SSM_HEAD_DIM * _iota((LANES, SSM_INNER), 0)
    expand = jnp.where(jnp.logical_and(off >= 0, off < SSM_HEAD_DIM), 1.0, 0.0).astype(BF16)
    return a, dt, cs, dec, ecs, expand, row, col


def _ssd_specs(t):
    nc = t // SSM_CHUNK
    xs = pl.BlockSpec((SSM_CHUNK, SSM_INNER), lambda c: (c, 0))
    bm = pl.BlockSpec((SSM_CHUNK, SSM_BC), lambda c: (c, SSM_INNER // SSM_BC))
    cm = pl.BlockSpec((SSM_CHUNK, SSM_BC), lambda c: (c, SSM_INNER // SSM_BC + 1))
    dtr = pl.BlockSpec((SSM_CHUNK, LANES), lambda c: (c, COL_DT // LANES))
    vec = pl.BlockSpec((1, LANES), lambda c: (0, 0))
    wide = pl.BlockSpec((1, SSM_INNER), lambda c: (0, 0))
    return nc, xs, bm, cm, dtr, vec, wide


def _ssd_fwd(xbc_act, proj, bias, alog, dexp, gather):
    t = proj.shape[0]
    nc, xs_s, bm_s, cm_s, dtr_s, vec, wide = _ssd_specs(t)
    n = len(gather)

    def body(*refs):
        xs_ref, b_ref, c_ref, dtr_ref, bias_ref, alog_ref, dexp_ref = refs[:7]
        y_ref, sin_ref = refs[7 + n:9 + n]
        state = refs[9 + 2 * n]
        comm = (refs[7:7 + n], refs[9 + n:9 + 2 * n]) + tuple(refs[10 + 2 * n:])
        chunk = pl.program_id(0)

        @pl.when(chunk == 0)
        def _():
            _gather_phase("start", *comm)
            state[...] = jnp.zeros_like(state)

        @pl.when(chunk == (3 * nc) // 4)
        def _():
            _gather_phase("forward", *comm)

        a, dt, cs, dec, ecs, expand, row, col = _ssd_chunk_terms(dtr_ref[...], bias_ref[...], alog_ref[...])
        cst = cs.T
        dt_x = _split_dot(dt, expand)
        dec_x = _split_dot(dec, expand)
        ecs_x = _split_dot(ecs, expand)
        xs = xs_ref[...].astype(F32)
        xdt = xs * dt_x
        xdec = xdt * dec_x
        lane_lo = col < SSM_HEAD_DIM
        causal = row >= col
        sin_ref[0] = state[...]
        for g in range(SSM_GROUPS):
            bg = b_ref[:, g * SSM_STATE:(g + 1) * SSM_STATE].astype(BF16)
            cg = c_ref[:, g * SSM_STATE:(g + 1) * SSM_STATE].astype(BF16)
            cb = _bdot(cg, bg, NT)
            for q in range(SSM_PAIRS // SSM_GROUPS):
                pq = g * (SSM_PAIRS // SSM_GROUPS) + q
                sl = slice(pq * LANES, (pq + 1) * LANES)
                xp = xdt[:, sl].astype(BF16)
                yd = []
                for hh in range(2):
                    h = 2 * pq + hh
                    lmat = jnp.exp(jnp.where(causal, cs[:, h:h + 1] - cst[h:h + 1, :], -jnp.inf))
                    yd.append(_bdot(cb * lmat, xp))
                s_in = state[pq]
                y_off = _bdot(cg, s_in) * ecs_x[:, sl]
                y_ref[:, sl] = (jnp.where(lane_lo, yd[0], yd[1]) + y_off + xs[:, sl] * dexp_ref[:, sl]).astype(BF16)
                state[pq] = s_in * ecs_x[SSM_CHUNK - 1:SSM_CHUNK, sl] + _bdot(bg, xdec[:, sl], TN)

        @pl.when(chunk == nc - 1)
        def _():
            _gather_phase("finish", *comm)

    out = pl.pallas_call(
        body, name="ssd_fwd", grid=(nc,),
        in_specs=[xs_s, bm_s, cm_s, dtr_s, vec, vec, wide] + [HBM] * n,
        out_specs=[pl.BlockSpec((SSM_CHUNK, SSM_INNER), lambda c: (c, 0)),
                   pl.BlockSpec((1, SSM_PAIRS, SSM_STATE, LANES), lambda c: (c, 0, 0, 0))] + [HBM] * n,
        out_shape=[jax.ShapeDtypeStruct((t, SSM_INNER), BF16),
                   jax.ShapeDtypeStruct((nc, SSM_PAIRS, SSM_STATE, LANES), F32)]
        + [jax.ShapeDtypeStruct((N_DEV,) + v.shape, v.dtype) for v in gather],
        scratch_shapes=[pltpu.VMEM((SSM_PAIRS, SSM_STATE, LANES), F32)] + _comm_scratch(n),
        compiler_params=_params(("arbitrary",)),
    )(xbc_act, xbc_act, xbc_act, proj, bias, alog, dexp, *gather)
    return out[:2], out[2:]


def _ssd_bwd(xbc_act, proj, s_in_all, dy, bias, alog, dexp, dproj, exchange):
    n_ex = len(exchange)
    t = proj.shape[0]
    nc = t // SSM_CHUNK
    last = nc - 1
    xs_s = pl.BlockSpec((SSM_CHUNK, SSM_INNER), lambda c: (last - c, 0))
    bm_s = pl.BlockSpec((SSM_CHUNK, SSM_BC), lambda c: (last - c, SSM_INNER // SSM_BC))
    cm_s = pl.BlockSpec((SSM_CHUNK, SSM_BC), lambda c: (last - c, SSM_INNER // SSM_BC + 1))
    dtr_s = pl.BlockSpec((SSM_CHUNK, LANES), lambda c: (last - c, COL_DT // LANES))
    sin_s = pl.BlockSpec((1, SSM_PAIRS, SSM_STATE, LANES), lambda c: (last - c, 0, 0, 0))
    vec = pl.BlockSpec((1, LANES), lambda c: (0, 0))
    wide = pl.BlockSpec((1, SSM_INNER), lambda c: (0, 0))

    def body(*refs):
        xs_ref, b_ref, c_ref, dtr_ref, sin_ref, dy_ref, bias_ref, alog_ref, dexp_ref = refs[:9]
        dxbc_ref, ddtr_ref, dbias_ref, dalog_ref, ddcol_ref = refs[10 + n_ex:15 + n_ex]
        dstate, dxdt_s, yoff_s, rx_s, trow_s = refs[15 + 2 * n_ex:20 + 2 * n_ex]
        comm = (refs[10:10 + n_ex], refs[15 + n_ex:15 + 2 * n_ex]) + tuple(refs[20 + 2 * n_ex:])

        @pl.when(pl.program_id(0) == 0)
        def _():
            _exchange_phase("start", *comm)
            dstate[...] = jnp.zeros_like(dstate)
            trow_s[...] = jnp.zeros_like(trow_s)
            dbias_ref[...] = jnp.zeros_like(dbias_ref)
            dalog_ref[...] = jnp.zeros_like(dalog_ref)
            ddcol_ref[...] = jnp.zeros_like(ddcol_ref)

        dtr = dtr_ref[...]
        a, dt, cs, dec, ecs, expand, row, col = _ssd_chunk_terms(dtr, bias_ref[...], alog_ref[...])
        cst = cs.T
        dt_x = _split_dot(dt, expand)
        dec_x = _split_dot(dec, expand)
        ecs_x = _split_dot(ecs, expand)
        xs = xs_ref[...].astype(F32)
        dyv = dy_ref[...].astype(F32)
        xdt = xs * dt_x
        lane_lo = col < SSM_HEAD_DIM
        causal = row >= col
        ddcol_ref[...] += jnp.sum(dyv * xs, axis=0, keepdims=True)
        dcs_col = jnp.zeros((SSM_CHUNK, LANES), F32)
        dcs_row = jnp.zeros((LANES, SSM_CHUNK), F32)
        for g in range(SSM_GROUPS):
            bg = b_ref[:, g * SSM_STATE:(g + 1) * SSM_STATE].astype(BF16)
            cg = c_ref[:, g * SSM_STATE:(g + 1) * SSM_STATE].astype(BF16)
            cb = _bdot(cg, bg, NT)
            dgm = jnp.zeros((SSM_CHUNK, SSM_CHUNK), F32)
            dbg = jnp.zeros((SSM_CHUNK, SSM_STATE), F32)
            dcg = jnp.zeros((SSM_CHUNK, SSM_STATE), F32)
            for q in range(SSM_PAIRS // SSM_GROUPS):
                pq = g * (SSM_PAIRS // SSM_GROUPS) + q
                sl = slice(pq * LANES, (pq + 1) * LANES)
                dyp = dyv[:, sl]
                xp = xdt[:, sl]
                dxh = []
                for hh in range(2):
                    h = 2 * pq + hh
                    lmat = jnp.exp(jnp.where(causal, cs[:, h:h + 1] - cst[h:h + 1, :], -jnp.inf))
                    mmat = cb * lmat
                    dyh = jnp.where(lane_lo if hh == 0 else jnp.logical_not(lane_lo), dyp, 0.0)
                    dmm = _bdot(dyh, xp, NT)
                    pm = dmm * mmat
                    dcs_col = jnp.where(col == h, jnp.sum(pm, axis=1, keepdims=True), dcs_col)
                    dcs_row = jnp.where(row == h, jnp.sum(pm, axis=0, keepdims=True), dcs_row)
                    dgm = dgm + dmm * lmat
                    dxh.append(_bdot(mmat, dyp, TN))
                s_in = sin_ref[0, pq]
                ecs_p = ecs_x[:, sl]
                dec_p = dec_x[:, sl]
                etot_p = ecs_x[SSM_CHUNK - 1:SSM_CHUNK, sl]
                yoff_s[:, sl] = dyp * (_bdot(cg, s_in) * ecs_p)
                dq = dyp * ecs_p
                dcg = dcg + _bdot(dq, s_in, NT)
                ds = dstate[pq]
                r = _bdot(bg, ds)
                rx_s[:, sl] = r * xp
                dxdt_s[:, sl] = jnp.where(lane_lo, dxh[0], dxh[1]) + dec_p * r
                dbg = dbg + _bdot(xp * dec_p, ds, NT)
                trow_s[0:1, sl] = jnp.sum(ds * s_in, axis=0, keepdims=True) * etot_p
                dstate[pq] = etot_p * ds + _bdot(cg, dq, TN)
            dcg = dcg + _bdot(dgm, bg)
            dbg = dbg + _bdot(dgm, cg, TN)
            dxbc_ref[:, SSM_INNER + g * SSM_STATE:SSM_INNER + (g + 1) * SSM_STATE] = dbg.astype(BF16)
            dxbc_ref[:, SSM_INNER + SSM_BC + g * SSM_STATE:SSM_INNER + SSM_BC + (g + 1) * SSM_STATE] = dcg.astype(BF16)
        ddec = _split_dot(rx_s[...], expand, NT) * dec
        dtot = _split_dot(trow_s[...], expand, NT)[0:1, :]
        dcs = dcs_col - dcs_row.T + _split_dot(yoff_s[...], expand, NT) - ddec
        dcs = dcs + jnp.where(row == SSM_CHUNK - 1, jnp.sum(ddec, axis=0, keepdims=True) + dtot, 0.0)
        da = _fdot((row <= col).astype(F32), dcs)
        dxdt = dxdt_s[...]
        ddt = da * a + _split_dot(dxdt * xs, expand, NT)
        dalog_ref[...] += jnp.sum(da * dt, axis=0, keepdims=True) * a
        ddtr = ddt * _sigmoid(dtr + bias_ref[...])
        ddtr_ref[...] = ddtr.astype(BF16)
        dbias_ref[...] += jnp.sum(ddtr, axis=0, keepdims=True)
        dxbc_ref[:, 0:SSM_INNER] = (dxdt * dt_x + dyv * dexp_ref[...]).astype(BF16)

        @pl.when(pl.program_id(0) == last)
        def _():
            _exchange_phase("finish", *comm)

    out = pl.pallas_call(
        body, name="ssd_bwd", grid=(nc,),
        in_specs=[xs_s, bm_s, cm_s, dtr_s, sin_s, pl.BlockSpec((SSM_CHUNK, SSM_INNER), lambda c: (last - c, 0)),
                  vec, vec, wide, HBM] + [HBM] * n_ex,
        out_specs=[pl.BlockSpec((SSM_CHUNK, SSM_CONV_DIM), lambda c: (last - c, 0)), dtr_s, vec, vec, wide]
        + [HBM] * n_ex,
        out_shape=[jax.ShapeDtypeStruct((t, SSM_CONV_DIM), BF16), jax.ShapeDtypeStruct(dproj.shape, dproj.dtype),
                   jax.ShapeDtypeStruct((1, LANES), F32), jax.ShapeDtypeStruct((1, LANES), F32),
                   jax.ShapeDtypeStruct((1, SSM_INNER), F32)]
        + [jax.ShapeDtypeStruct(p.shape, p.dtype) for p in exchange],
        input_output_aliases={9: 1},
        scratch_shapes=[pltpu.VMEM((SSM_PAIRS, SSM_STATE, LANES), F32),
                        pltpu.VMEM((SSM_CHUNK, SSM_INNER), F32), pltpu.VMEM((SSM_CHUNK, SSM_INNER), F32),
                        pltpu.VMEM((SSM_CHUNK, SSM_INNER), F32), pltpu.VMEM((SUBLANES, SSM_INNER), F32)]
        + _comm_scratch(n_ex),
        compiler_params=_params(("arbitrary",)),
    )(xbc_act, xbc_act, xbc_act, proj, s_in_all, dy, bias, alog, dexp, dproj, *exchange)
    return out[:5], out[5:]


def _group_rstd(y):
    n = SSM_INNER // SSM_GROUPS
    parts = []
    for g in range(SSM_GROUPS):
        yg = y[:, g * n:(g + 1) * n]
        r = lax.rsqrt(jnp.mean(yg * yg, axis=-1, keepdims=True) + RMS_EPS)
        parts.append(jnp.broadcast_to(r, yg.shape))
    return jnp.concatenate(parts, axis=1)


def _group_mean(v):
    n = SSM_INNER // SSM_GROUPS
    parts = []
    for g in range(SSM_GROUPS):
        vg = v[:, g * n:(g + 1) * n]
        parts.append(jnp.broadcast_to(jnp.mean(vg, axis=-1, keepdims=True), vg.shape))
    return jnp.concatenate(parts, axis=1)


def _ssm_post_fwd(y_ssd, proj, nw):
    t = proj.shape[0]
    n = SSM_INNER
    tr = _rows(t)

    def body(y_ref, z_ref, nw_ref, o_ref):
        z = z_ref[...]
        y = y_ref[...] * (z * _sigmoid(z))
        o_ref[...] = (y * _group_rstd(y) * nw_ref[...]).astype(BF16)

    row = pl.BlockSpec((tr, n), lambda i: (i, 0))
    return pl.pallas_call(
        body, name="ssm_post_fwd", grid=(t // tr,),
        in_specs=[row, pl.BlockSpec((tr, n), lambda i: (i, COL_Z // n)), pl.BlockSpec((1, n), lambda i: (0, 0))],
        out_specs=row, out_shape=jax.ShapeDtypeStruct((t, n), BF16),
        compiler_params=_params(("parallel",)),
    )(y_ssd, proj, nw)


def _ssm_post_bwd(y_ssd, proj, nw, dout, dproj):
    t = proj.shape[0]
    n = SSM_INNER
    tr = _rows(t)

    def body(y_ref, z_ref, nw_ref, do_ref, dproj_in, dy_ref, dz_ref, dnw_ref):
        i = pl.program_id(0)
        z = z_ref[...]
        sg = _sigmoid(z)
        sz = z * sg
        ys = y_ref[...]
        y = ys * sz
        rstd = _group_rstd(y)
        yn = y * rstd
        dov = do_ref[...]
        dyn = dov * nw_ref[...]
        dyg = rstd * (dyn - yn * _group_mean(dyn * yn))
        dy_ref[...] = (dyg * sz).astype(BF16)
        dz_ref[...] = (dyg * ys * (sg * (1.0 + z * (1.0 - sg)))).astype(BF16)
        part = jnp.sum(dov * yn, axis=0, keepdims=True)

        @pl.when(i == 0)
        def _():
            dnw_ref[...] = part

        @pl.when(i > 0)
        def _():
            dnw_ref[...] += part

    row = pl.BlockSpec((tr, n), lambda i: (i, 0))
    vec = pl.BlockSpec((1, n), lambda i: (0, 0))
    return pl.pallas_call(
        body, name="ssm_post_bwd", grid=(t // tr,),
        in_specs=[row, pl.BlockSpec((tr, n), lambda i: (i, COL_Z // n)), vec, row, HBM],
        out_specs=[row, pl.BlockSpec((tr, n), lambda i: (i, COL_Z // n)), vec],
        out_shape=[jax.ShapeDtypeStruct((t, n), BF16), jax.ShapeDtypeStruct(dproj.shape, dproj.dtype),
                   jax.ShapeDtypeStruct((1, n), F32)],
        input_output_aliases={4: 1},
        compiler_params=_params(("arbitrary",)),
    )(y_ssd, proj, nw, dout, dproj)


SCAN_UNROLL = 8
GELU_C = math.sqrt(2.0 / math.pi)
GELU_K = 0.044715


def _gelu_parts(y):
    th = jnp.tanh(GELU_C * (y + GELU_K * y * y * y))
    val = 0.5 * y * (1.0 + th)
    grad = 0.5 * (1.0 + th) + 0.5 * y * (1.0 - th * th) * GELU_C * (1.0 + 3.0 * GELU_K * y * y)
    return val, grad


def _scan_tiles(a_ref, b_ref, h_ref, n_rows, reverse):
    n_tiles = n_rows // SUBLANES
    shape = (SUBLANES, a_ref.shape[1])
    row = _iota(shape, 0)

    def in_tile(av, bv):
        for s in (1, 2, 4):
            if reverse:
                keep = row < SUBLANES - s
                a_sh = jnp.where(keep, pltpu.roll(av, SUBLANES - s, 0), 1.0)
                b_sh = jnp.where(keep, pltpu.roll(bv, SUBLANES - s, 0), 0.0)
            else:
                keep = row >= s
                a_sh = jnp.where(keep, pltpu.roll(av, s, 0), 1.0)
                b_sh = jnp.where(keep, pltpu.roll(bv, s, 0), 0.0)
            bv = av * b_sh + bv
            av = av * a_sh
        return av, bv

    def step(k, carry):
        first = (n_tiles // SCAN_UNROLL - 1 - k) if reverse else k
        tiles = [first * SCAN_UNROLL + j for j in range(SCAN_UNROLL)]
        if reverse:
            tiles = tiles[::-1]
        ats = [pl.ds(pl.multiple_of(tile * SUBLANES, SUBLANES), SUBLANES) for tile in tiles]
        scanned = [in_tile(a_ref[at, :], b_ref[at, :]) for at in ats]
        for at, (av, bv) in zip(ats, scanned):
            hv = bv + av * carry
            h_ref[at, :] = hv
            carry = hv[0:1, :] if reverse else hv[SUBLANES - 1:SUBLANES, :]
        return carry

    assert n_tiles % SCAN_UNROLL == 0, n_rows
    lax.fori_loop(0, n_tiles // SCAN_UNROLL, step, jnp.zeros((1, a_ref.shape[1]), F32))


def _lru_gates(xl, cw, cb, wr, br, wi, bi, lam):
    u = cb + cw[CONV_K - 1:CONV_K, :] * xl
    for k in range(CONV_K - 1):
        u = u + cw[k:k + 1, :] * _shift_down(xl, CONV_K - 1 - k)
    r = _sigmoid(_bdot(u, wr) + br)
    i = _sigmoid(_bdot(u, wi) + bi)
    sp = _softplus(-lam)
    la = -LRU_C * r * sp
    a = jnp.exp(la)
    mult = jnp.sqrt(-jnp.tanh(la) * (a * a + 1.0))
    return u, r, i, sp, a, mult


def _lru_specs(t):
    c0 = COL_LRU // LANES
    xl = pl.BlockSpec((t, LANES), lambda j: (0, c0 + 2 * j))
    yl = pl.BlockSpec((t, LANES), lambda j: (0, c0 + 2 * j + 1))
    col = pl.BlockSpec((t, LANES), lambda j: (0, j))
    cw = pl.BlockSpec((CONV_K, LANES), lambda j: (0, j))
    vec = pl.BlockSpec((1, LANES), lambda j: (0, j))
    wblk = pl.BlockSpec((1, LANES, LANES), lambda j: (j, 0, 0))
    return xl, yl, col, cw, vec, wblk


def _lru_fwd(proj, cw, cb, wr, br, wi, bi, lam, gather):
    t = proj.shape[0]
    xl_s, yl_s, col, cw_s, vec, wblk = _lru_specs(t)
    n = len(gather)

    def body(*refs):
        xl_ref, yl_ref, cw_ref, cb_ref, wr_ref, br_ref, wi_ref, bi_ref, lam_ref = refs[:9]
        o_ref, h_ref = refs[9 + n:11 + n]
        a_s, b_s = refs[11 + 2 * n:13 + 2 * n]
        comm = (refs[9:9 + n], refs[11 + n:11 + 2 * n]) + tuple(refs[13 + 2 * n:])
        j = pl.program_id(0)
        for step, phase in ((0, "start"), (LRU_BLOCKS - 2, "forward")):
            @pl.when(j == step)
            def _():
                _gather_phase(phase, *comm)

        u, r, i, sp, a, mult = _lru_gates(xl_ref[...], cw_ref[...], cb_ref[...], wr_ref[0], br_ref[...],
                                          wi_ref[0], bi_ref[...], lam_ref[...])
        a_s[...] = a
        b_s[...] = mult * (i * u)
        _scan_tiles(a_s, b_s, h_ref, t, reverse=False)
        o_ref[...] = (h_ref[...] * _gelu_parts(yl_ref[...])[0]).astype(BF16)

        @pl.when(j == LRU_BLOCKS - 1)
        def _():
            _gather_phase("finish", *comm)

    out = pl.pallas_call(
        body, name="lru_fwd", grid=(LRU_BLOCKS,),
        in_specs=[xl_s, yl_s, cw_s, vec, wblk, vec, wblk, vec, vec] + [HBM] * n,
        out_specs=[col, col] + [HBM] * n,
        out_shape=[jax.ShapeDtypeStruct((t, LRU_WIDTH), BF16), jax.ShapeDtypeStruct((t, LRU_WIDTH), F32)]
        + [jax.ShapeDtypeStruct((N_DEV,) + v.shape, v.dtype) for v in gather],
        scratch_shapes=[pltpu.VMEM((t, LANES), F32)] * 2 + _comm_scratch(n),
        compiler_params=_params(("arbitrary",), big=True),
    )(proj, proj, cw, cb, wr, br, wi, bi, lam, *gather)
    return out[:2], out[2:]


def _lru_bwd(proj, cw, cb, wr, br, wi, bi, lam, h_all, dout, dproj, exchange):
    t = proj.shape[0]
    xl_s, yl_s, col, cw_s, vec, wblk = _lru_specs(t)
    pair = pl.BlockSpec((t, 2 * LANES), lambda j: (0, COL_LRU // (2 * LANES) + j))
    n_ex = len(exchange)

    def body(*refs):
        xl_ref, yl_ref, cw_ref, cb_ref, wr_ref, br_ref, wi_ref, bi_ref, lam_ref, h_ref, do_ref = refs[:11]
        dxy_ref, dcw_ref, dcb_ref, dwr_ref, dbr_ref, dwi_ref, dbi_ref, dlam_ref = refs[12 + n_ex:20 + n_ex]
        a_s, b_s, g_s = refs[20 + 2 * n_ex:23 + 2 * n_ex]
        comm = (refs[12:12 + n_ex], refs[20 + n_ex:20 + 2 * n_ex]) + tuple(refs[23 + 2 * n_ex:])

        @pl.when(pl.program_id(0) == 0)
        def _():
            _exchange_phase("start", *comm)

        xl = xl_ref[...]
        cwv = cw_ref[...]
        lam = lam_ref[...]
        u, r, i, sp, a, mult = _lru_gates(xl, cwv, cb_ref[...], wr_ref[0], br_ref[...], wi_ref[0], bi_ref[...], lam)
        v = i * u
        gl, dgl = _gelu_parts(yl_ref[...])
        dov = do_ref[...]
        h = h_ref[...]
        dxy_ref[:, LANES:2 * LANES] = (dov * h * dgl).astype(BF16)
        b_s[...] = dov * gl
        a_s[...] = _shift_up(a, 1)
        _scan_tiles(a_s, b_s, g_s, t, reverse=True)
        g = g_s[...]
        da = g * _shift_down(h, 1)
        dmult = g * v
        dv = g * mult
        dla = da * a - dmult * (a * a) / mult
        dr = dla * (-LRU_C * sp)
        dsp = jnp.sum(dla * (-LRU_C * r), axis=0, keepdims=True)
        dlam_ref[...] = -dsp * _sigmoid(-lam)
        dpr = dr * r * (1.0 - r)
        dpi = dv * u * i * (1.0 - i)
        dbr_ref[...] = jnp.sum(dpr, axis=0, keepdims=True)
        dbi_ref[...] = jnp.sum(dpi, axis=0, keepdims=True)
        dwr_ref[0] = _bdot(u, dpr, TN)
        dwi_ref[0] = _bdot(u, dpi, TN)
        du = dv * i + _bdot(dpr, wr_ref[0], NT) + _bdot(dpi, wi_ref[0], NT)
        dxl = cwv[CONV_K - 1:CONV_K, :] * du
        for k in range(CONV_K - 1):
            dxl = dxl + cwv[k:k + 1, :] * _shift_up(du, CONV_K - 1 - k)
        dxy_ref[:, 0:LANES] = dxl.astype(BF16)
        for k in range(CONV_K):
            dcw_ref[k:k + 1, :] = jnp.sum(du * _shift_down(xl, CONV_K - 1 - k), axis=0, keepdims=True)
        dcb_ref[...] = jnp.sum(du, axis=0, keepdims=True)

        @pl.when(pl.program_id(0) == LRU_BLOCKS - 1)
        def _():
            _exchange_phase("finish", *comm)

    out = pl.pallas_call(
        body, name="lru_bwd", grid=(LRU_BLOCKS,),
        in_specs=[xl_s, yl_s, cw_s, vec, wblk, vec, wblk, vec, vec, col, col, HBM] + [HBM] * n_ex,
        out_specs=[pair, cw_s, vec, wblk, vec, wblk, vec, vec] + [HBM] * n_ex,
        input_output_aliases={11: 0},
        out_shape=[jax.ShapeDtypeStruct(dproj.shape, dproj.dtype),
                   jax.ShapeDtypeStruct((CONV_K, LRU_WIDTH), F32), jax.ShapeDtypeStruct((1, LRU_WIDTH), F32),
                   jax.ShapeDtypeStruct((LRU_BLOCKS, LANES, LANES), F32), jax.ShapeDtypeStruct((1, LRU_WIDTH), F32),
                   jax.ShapeDtypeStruct((LRU_BLOCKS, LANES, LANES), F32), jax.ShapeDtypeStruct((1, LRU_WIDTH), F32),
                   jax.ShapeDtypeStruct((1, LRU_WIDTH), F32)]
        + [jax.ShapeDtypeStruct(p.shape, p.dtype) for p in exchange],
        scratch_shapes=[pltpu.VMEM((t, LANES), F32)] * 3 + _comm_scratch(n_ex),
        compiler_params=_params(("arbitrary",), big=True),
    )(proj, proj, cw, cb, wr, br, wi, bi, lam, h_all, dout, dproj, *exchange)
    return out[:8], out[8:]


def _mesh_pos():
    return lax.axis_index("x"), lax.axis_index("y"), lax.axis_index("c")


HBM = pl.BlockSpec(memory_space=pl.ANY)


def _comm_scratch(n):
    return [pltpu.SemaphoreType.DMA((n, 7)), pltpu.SemaphoreType.DMA((n, 7)), pltpu.SemaphoreType.DMA((n,))]


def _gather_phase(phase, v_refs, out_refs, send_sems, recv_sems, local_sems):
    n = len(v_refs)
    x, y, c = _mesh_pos()
    me, sibling = (x, y, c), (x, y, 1 - c)
    chips = [(1 - x, y), (x, 1 - y), (1 - x, 1 - y)]

    def block(a, px, py, pc):
        return out_refs[a].at[4 * px + 2 * py + pc]

    def copy(a, k, blk, to, src=None):
        return pltpu.make_async_remote_copy(
            src_ref=block(a, *blk) if src is None else src, dst_ref=block(a, *blk),
            send_sem=send_sems.at[a, k], recv_sem=recv_sems.at[a, k], device_id=to, device_id_type=MESH)

    def own(a):
        return pltpu.make_async_copy(v_refs[a], block(a, *me), local_sems.at[a])

    def first(a):
        return ([copy(a, 0, me, sibling, src=v_refs[a])]
                + [copy(a, 1 + j, me, (*chip, c), src=v_refs[a]) for j, chip in enumerate(chips)])

    def forward(a, j):
        return copy(a, 4 + j, (*chips[j], c), sibling)

    if phase == "start":
        for a in range(n):
            own(a).start()
        for a in range(n):
            for cp in first(a):
                cp.start()
    elif phase == "forward":
        for j in range(3):
            for a in range(n):
                copy(a, 1 + j, (*chips[j], c), me).wait_recv()
                forward(a, j).start()
    else:
        for a in range(n):
            copy(a, 0, sibling, me).wait_recv()
            for j in range(3):
                copy(a, 4 + j, (*chips[j], 1 - c), me).wait_recv()
        for a in range(n):
            for cp in first(a) + [forward(a, j) for j in range(3)]:
                cp.wait_send()
            own(a).wait()


def _all_gather(vs, name):
    n = len(vs)

    def body(*refs):
        comm = (refs[:n], refs[n:2 * n]) + tuple(refs[2 * n:])
        for phase in ("start", "forward", "finish"):
            _gather_phase(phase, *comm)

    return pl.pallas_call(
        body, name=name,
        out_shape=[jax.ShapeDtypeStruct((N_DEV,) + v.shape, v.dtype) for v in vs],
        in_specs=[HBM] * n, out_specs=[HBM] * n, scratch_shapes=_comm_scratch(n),
    )(*vs)


def _run_copies(phase, local, remote):
    if phase == "start":
        for cp in local + remote:
            cp.start()
    else:
        for cp in remote:
            cp.wait()
        for cp in local:
            cp.wait()


def _exchange_phase(phase, p_refs, out_refs, send_sems, recv_sems, local_sems):
    n = len(p_refs)
    x, y, c = _mesh_pos()
    me = 4 * x + 2 * y + c
    local = [pltpu.make_async_copy(p_refs[a].at[me], out_refs[a].at[me], local_sems.at[a]) for a in range(n)]
    remote = []
    for k in range(1, N_DEV):
        px = (1 - x) if k & 4 else x
        py = (1 - y) if k & 2 else y
        pc = (1 - c) if k & 1 else c
        for a in range(n):
            remote.append(pltpu.make_async_remote_copy(
                src_ref=p_refs[a].at[4 * px + 2 * py + pc], dst_ref=out_refs[a].at[me],
                send_sem=send_sems.at[a, k - 1], recv_sem=recv_sems.at[a, k - 1],
                device_id=(px, py, pc), device_id_type=MESH))
    _run_copies(phase, local, remote)


def _chip_exchange_phase(phase, p_refs, out_refs, send_sems, recv_sems, local_sems):
    n = len(p_refs)
    x, y, c = _mesh_pos()
    me = 2 * x + y
    local = [pltpu.make_async_copy(p_refs[a].at[me], out_refs[a].at[me], local_sems.at[a]) for a in range(n)]
    remote = []
    for k in range(1, 4):
        px = (1 - x) if k & 2 else x
        py = (1 - y) if k & 1 else y
        for a in range(n):
            remote.append(pltpu.make_async_remote_copy(
                src_ref=p_refs[a].at[2 * px + py], dst_ref=out_refs[a].at[me],
                send_sem=send_sems.at[a, k - 1], recv_sem=recv_sems.at[a, k - 1],
                device_id=(px, py, c), device_id_type=MESH))
    _run_copies(phase, local, remote)


def _sibling_exchange(parts, name):
    chips = N_DEV // 2

    def body(p_ref, out_ref, send_sems, recv_sems):
        x, y, c = _mesh_pos()
        copies = [pltpu.make_async_remote_copy(
            src_ref=p_ref.at[2 * q + 1 - c], dst_ref=out_ref.at[q], send_sem=send_sems.at[q], recv_sem=recv_sems.at[q],
            device_id=(x, y, 1 - c), device_id_type=MESH) for q in range(chips)]
        _run_copies("start", [], copies)
        _run_copies("finish", [], copies)

    return pl.pallas_call(
        body, name=name, out_shape=jax.ShapeDtypeStruct((chips,) + parts.shape[1:], parts.dtype),
        in_specs=[HBM], out_specs=HBM,
        scratch_shapes=[pltpu.SemaphoreType.DMA((chips,)), pltpu.SemaphoreType.DMA((chips,))],
    )(parts)


def _pair_sum(mine, theirs, name):
    slots, rows, cols = mine.shape
    tc = 256

    def body(a_ref, b_ref, o_ref):
        o_ref[...] = (a_ref[...].astype(F32) + b_ref[...].astype(F32)).astype(o_ref.dtype)

    spec = pl.BlockSpec((1, rows, tc), lambda q, j: (q, 0, j))
    return pl.pallas_call(
        body, name=name, grid=(slots, cols // tc), in_specs=[spec, spec], out_specs=spec,
        out_shape=jax.ShapeDtypeStruct(mine.shape, mine.dtype),
        compiler_params=_params(("parallel", "parallel")),
    )(mine, theirs)


def _sum_sources(recvs, name):
    k = len(recvs)

    def body(*refs):
        for r_ref, o_ref in zip(refs[:k], refs[k:]):
            acc = r_ref[0].astype(F32)
            for s in range(1, r_ref.shape[0]):
                acc = acc + r_ref[s].astype(F32)
            o_ref[...] = acc

    return pl.pallas_call(
        body, name=name, out_shape=[jax.ShapeDtypeStruct(r.shape[1:], F32) for r in recvs],
        compiler_params=_params(),
    )(*recvs)


def _row_tile(rows):
    for tile in range(128, 15, -16):
        if rows % tile == 0:
            return tile
    return rows


def _adam_update(w, g, m, v):
    nm = ADAM_B1 * m + (1.0 - ADAM_B1) * g
    nv = ADAM_B2 * v + (1.0 - ADAM_B2) * (g * g)
    m_hat = nm / (1.0 - ADAM_B1 ** ADAM_STEP)
    v_hat = nv / (1.0 - ADAM_B2 ** ADAM_STEP)
    return -ADAM_LR * (m_hat / (jnp.sqrt(v_hat) + ADAM_EPS) + ADAM_WD * w), nm, nv


def _vector_offsets(widths):
    offsets, end = [], 0
    for c in widths:
        offsets.append(end)
        end += c + (-c) % LANES
    return offsets, end


def _adamw_small(vec_parts, vec_state, mat_grads, mat_state):
    widths = [w.shape[1] for w, _, _ in vec_state]
    offsets, total = _vector_offsets(widths)
    assert vec_parts.shape == (N_DEV, total), (vec_parts.shape, total)
    n_vec, n_mat = len(vec_state), len(mat_state)

    def body(*refs):
        r_ref = refs[0]
        vec_in = refs[1:1 + 3 * n_vec]
        mat_in = refs[1 + 3 * n_vec:1 + 3 * n_vec + 4 * n_mat]
        outs = refs[1 + 3 * n_vec + 4 * n_mat:]
        for i, (off, c) in enumerate(zip(offsets, widths)):
            g = r_ref[0:1, off:off + c]
            for s in range(1, N_DEV):
                g = g + r_ref[s:s + 1, off:off + c]
            w_ref, m_ref, v_ref = vec_in[3 * i:3 * i + 3]
            g_out, d_out, m_out, v_out = outs[4 * i:4 * i + 4]
            g_out[...] = g
            d_out[...], m_out[...], v_out[...] = _adam_update(w_ref[...], g, m_ref[...], v_ref[...])
        for j in range(n_mat):
            g_ref, w_ref, m_ref, v_ref = mat_in[4 * j:4 * j + 4]
            d_out, m_out, v_out = outs[4 * n_vec + 3 * j:4 * n_vec + 3 * j + 3]
            d_out[...], m_out[...], v_out[...] = _adam_update(w_ref[...], g_ref[...], m_ref[...], v_ref[...])

    args = [vec_parts] + [a for state in vec_state for a in state]
    for g, state in zip(mat_grads, mat_state):
        args += [g, *state]
    out_shape = [jax.ShapeDtypeStruct(w.shape, F32) for w, _, _ in vec_state for _ in range(4)]
    out_shape += [jax.ShapeDtypeStruct(w.shape, F32) for w, _, _ in mat_state for _ in range(3)]
    out = pl.pallas_call(body, name="adamw_replicated", out_shape=out_shape, compiler_params=_params())(*args)
    vec_out = [tuple(out[4 * i:4 * i + 4]) for i in range(n_vec)]
    mat_out = [tuple(out[4 * n_vec + 3 * j:4 * n_vec + 3 * j + 3]) for j in range(n_mat)]
    return vec_out, mat_out


def _adamw(w, recv, m, v, name):
    rows, width = w.shape
    n = recv.shape[0]
    if rows % 16 == 0 or width % 256:
        tr, tc = _row_tile(rows), width
    else:
        tr, tc = rows, 256

    def body(w_ref, r_ref, m_ref, v_ref, g_ref, d_ref, nm_ref, nv_ref):
        gv = r_ref[0].astype(F32)
        for s in range(1, n):
            gv = gv + r_ref[s].astype(F32)
        g_ref[...] = gv
        d_ref[...], nm_ref[...], nv_ref[...] = _adam_update(w_ref[...], gv, m_ref[...], v_ref[...])

    spec = pl.BlockSpec((tr, tc), lambda i, j: (i, j))
    shape = jax.ShapeDtypeStruct((rows, width), F32)
    return pl.pallas_call(
        body, name=name, grid=(rows // tr, width // tc),
        in_specs=[spec, pl.BlockSpec((n, tr, tc), lambda i, j: (0, i, j)), spec, spec],
        out_specs=[spec] * 4, out_shape=[shape] * 4,
        compiler_params=_params(("parallel", "parallel")),
    )(w, recv, m, v)


BIG_NAMES = ("w_in", "w_out_ssm", "w_out_lru", "w_out", "w_ffn_in", "w_ffn_out", "ssm_conv_w", "lru_conv_w")
TRANSPOSED = ("w_in", "w_ffn_in")
CONV_NAMES = ("ssm_conv_w", "lru_conv_w")
MATMUL_NAMES = BIG_NAMES[:6]
NEEDED_FIRST = ("w_in", "ssm_conv_w", "lru_conv_w")
GATHERED_IN_SSD = ("w_ffn_in",)
GATHERED_IN_LRU = ("w_ffn_out", "w_out_ssm", "w_out_lru", "w_out")
EXCHANGED_IN_SSD = ("w_ffn_in", "w_ffn_out")
EXCHANGED_IN_LRU = ("w_out_ssm", "w_out_lru", "w_out")
SMALL_VECTORS = ("norm1_w", "b_branch_gate", "ssm_conv_b", "ssm_dt_bias", "ssm_a_log", "ssm_d", "ssm_norm_w",
                 "lru_conv_b", "lru_b_r", "lru_b_i", "lru_lambda", "norm2_w", "norm_f_w")
SMALL_MATRICES = ("lru_w_r", "lru_w_i")


def _col_shards(full):
    rows, cols = full.shape
    return full.reshape(rows, N_DEV, cols // N_DEV).transpose(1, 0, 2)


def _from_col_shards(g):
    n, rows, w = g.shape
    return g.transpose(1, 0, 2).reshape(rows, n * w)


def kernel(x, norm1_w, w_in, b_branch_gate, ssm_conv_w, ssm_conv_b, ssm_dt_bias, ssm_a_log, ssm_d, ssm_norm_w, w_out_ssm, lru_conv_w, lru_conv_b, lru_w_r, lru_b_r, lru_w_i, lru_b_i, lru_lambda, w_out_lru, w_out, norm2_w, w_ffn_in, w_ffn_out, norm_f_w, loss_target, m_norm1_w, m_w_in, m_b_branch_gate, m_ssm_conv_w, m_ssm_conv_b, m_ssm_dt_bias, m_ssm_a_log, m_ssm_d, m_ssm_norm_w, m_w_out_ssm, m_lru_conv_w, m_lru_conv_b, m_lru_w_r, m_lru_b_r, m_lru_w_i, m_lru_b_i, m_lru_lambda, m_w_out_lru, m_w_out, m_norm2_w, m_w_ffn_in, m_w_ffn_out, m_norm_f_w, v_norm1_w, v_w_in, v_b_branch_gate, v_ssm_conv_w, v_ssm_conv_b, v_ssm_dt_bias, v_ssm_a_log, v_ssm_d, v_ssm_norm_w, v_w_out_ssm, v_lru_conv_w, v_lru_conv_b, v_lru_w_r, v_lru_b_r, v_lru_w_i, v_lru_b_i, v_lru_lambda, v_w_out_lru, v_w_out, v_norm2_w, v_w_ffn_in, v_w_ffn_out, v_norm_f_w):
    given = dict(locals())
    weights = {n: given[n] for n in BIG_NAMES + SMALL_VECTORS + SMALL_MATRICES}
    t = x.shape[1]
    xt = x[0]
    tgt = loss_target[0]

    def local(n, a):
        return a[0].T if n in TRANSPOSED else a[0]

    def as_output(n, a):
        return a.T[None] if n in TRANSPOSED else a[None]

    def shard(n):
        s = local(n, weights[n])
        return s.astype(BF16) if n in MATMUL_NAMES else s

    def unshard(n, g):
        return _from_col_shards(g) if n in CONV_NAMES else g.reshape(-1, g.shape[-1])

    def grad_slices(n):
        g = grads[n]
        return (_col_shards(g) if n in CONV_NAMES else g.reshape(N_DEV, -1, g.shape[-1])).astype(BF16)

    gathered = _all_gather([shard(n) for n in NEEDED_FIRST], "gather_in_weights")
    full = {n: unshard(n, g) for n, g in zip(NEEDED_FIRST, gathered)}
    ssm_cw, lru_cw = full["ssm_conv_w"], full["lru_conv_w"]
    wi_t = full["w_in"]
    lru_rows = wi_t[ORIG_LRU_X:].reshape(2, LRU_BLOCKS, LANES, D_MODEL).transpose(1, 0, 2, 3)
    w_pt = jnp.concatenate([wi_t[:ORIG_DT], lru_rows.reshape(2 * LRU_WIDTH, D_MODEL), wi_t[ORIG_DT:ORIG_LRU_X],
                            jnp.zeros((PROJ_W - IN_PROJ, D_MODEL), BF16)], axis=0)

    def pad_heads(a):
        return jnp.pad(a.reshape(1, SSM_HEADS), ((0, 0), (0, LANES - SSM_HEADS)))

    dt_bias_p = pad_heads(ssm_dt_bias)
    a_log_p = pad_heads(ssm_a_log)
    d_exp = jnp.repeat(ssm_d.reshape(SSM_HEADS), SSM_HEAD_DIM).reshape(1, SSM_INNER)
    lru_wr, lru_wi = lru_w_r[0], lru_w_i[0]

    hn1 = _rmsnorm_fwd(xt, norm1_w, "norm1_fwd")
    proj = _mm(hn1, w_pt, tb=True, name="in_proj")
    xbc_act = _ssm_conv_fwd(proj, ssm_cw, ssm_conv_b)
    (y_ssd, s_in_all), gathered = _ssd_fwd(xbc_act, proj, dt_bias_p, a_log_p, d_exp,
                                           gather=[shard(n) for n in GATHERED_IN_SSD])
    full.update({n: unshard(n, g) for n, g in zip(GATHERED_IN_SSD, gathered)})
    (l_out, h_lru), gathered = _lru_fwd(proj, lru_cw, lru_conv_b, lru_wr, lru_b_r, lru_wi, lru_b_i, lru_lambda,
                                        gather=[shard(n) for n in GATHERED_IN_LRU])
    full.update({n: unshard(n, g) for n, g in zip(GATHERED_IN_LRU, gathered)})
    y_pre = _ssm_post_fwd(y_ssd, proj, ssm_norm_w)
    y_ssm = _mm(y_pre, full["w_out_ssm"], out_dtype=BF16, name="out_ssm")
    y_lru = _mm(l_out, full["w_out_lru"], out_dtype=BF16, name="out_lru")
    merged = _merge_fwd(proj, b_branch_gate, y_ssm, y_lru)
    h1, hn2 = _mm(merged, full["w_out"], epilogue=_residual_norm_epilogue(xt, norm2_w), name="out_proj")
    gate, up, act = _ffn_in_swiglu(hn2, full["w_ffn_in"])

    grads = {}
    dh2, grads["norm_f_w"], loss_cols = _mm(
        act, full["w_ffn_out"], epilogue=_loss_epilogue(h1, norm_f_w.reshape(1, D_MODEL), tgt), name="ffn_out")
    loss = lax.psum(0.5 * jnp.sum(loss_cols) / D_MODEL, AXES)
    dact = _mm(dh2, full["w_ffn_out"], tb=True, out_dtype=BF16, name="d_act")
    grads["w_ffn_out"] = _mm(act, dh2, ta=True, out_dtype=BF16, name="dw_ffn_out")
    dgu = _swiglu_bwd(gate, up, dact)
    dh1, grads["norm2_w"] = _mm(dgu, full["w_ffn_in"], epilogue=_norm_bwd_epilogue(h1, norm2_w, dh2), name="d_hn2")
    grads["w_ffn_in"] = _mm(dgu, hn2, ta=True, out_dtype=BF16, name="dw_ffn_in")
    dmerged = _mm(dh1, full["w_out"], tb=True, out_dtype=BF16, name="d_merged")
    grads["w_out"] = _mm(merged, dh1, ta=True, out_dtype=BF16, name="dw_out")
    dy_ssm, dy_lru, dproj, grads["b_branch_gate"] = _merge_bwd(proj, b_branch_gate, y_ssm, y_lru, dmerged)
    dy_pre = _mm(dy_ssm, full["w_out_ssm"], tb=True, out_dtype=BF16, name="d_y_pre")
    grads["w_out_ssm"] = _mm(y_pre, dy_ssm, ta=True, out_dtype=BF16, name="dw_out_ssm")
    dl_out = _mm(dy_lru, full["w_out_lru"], tb=True, out_dtype=BF16, name="d_l_out")
    grads["w_out_lru"] = _mm(l_out, dy_lru, ta=True, out_dtype=BF16, name="dw_out_lru")
    dy_ssd, dproj, grads["ssm_norm_w"] = _ssm_post_bwd(y_ssd, proj, ssm_norm_w, dy_pre, dproj)
    (dxbc_act, dproj, dbias, dalog, ddcol), recv_in_ssd = _ssd_bwd(
        xbc_act, proj, s_in_all, dy_ssd, dt_bias_p, a_log_p, d_exp, dproj,
        exchange=[grad_slices(n) for n in EXCHANGED_IN_SSD])
    grads["ssm_dt_bias"] = dbias[:, :SSM_HEADS]
    grads["ssm_a_log"] = dalog[:, :SSM_HEADS]
    grads["ssm_d"] = ddcol.reshape(SSM_HEADS, SSM_HEAD_DIM).sum(axis=1).reshape(1, SSM_HEADS)
    dproj, grads["ssm_conv_w"], grads["ssm_conv_b"] = _ssm_conv_bwd(proj, ssm_cw, ssm_conv_b, dxbc_act, dproj)
    ((dproj, grads["lru_conv_w"], grads["lru_conv_b"], dwr, grads["lru_b_r"], dwi, grads["lru_b_i"],
      grads["lru_lambda"]), recv_in_lru) = _lru_bwd(
        proj, lru_cw, lru_conv_b, lru_wr, lru_b_r, lru_wi, lru_b_i, lru_lambda, h_lru, dl_out, dproj,
        exchange=[grad_slices(n) for n in EXCHANGED_IN_LRU])
    grads["lru_w_r"], grads["lru_w_i"] = dwr[None], dwi[None]
    dwpt = _mm(dproj, hn1, ta=True, out_dtype=BF16, name="dw_in")
    lru_rows = dwpt[COL_LRU:COL_DT].reshape(LRU_BLOCKS, 2, LANES, D_MODEL).transpose(1, 0, 2, 3)
    grads["w_in"] = jnp.concatenate([dwpt[:ORIG_DT], dwpt[COL_DT:COL_DT + SSM_HEADS],
                                     lru_rows.reshape(2 * LRU_WIDTH, D_MODEL)], axis=0)
    w_in_parts = grad_slices("w_in")
    from_sibling = _sibling_exchange(w_in_parts, "sibling_exchange_dw_in")
    for_my_core = lax.dynamic_index_in_dim(w_in_parts.reshape((N_DEV // 2, 2) + w_in_parts.shape[1:]),
                                           lax.axis_index("c"), axis=1, keepdims=False)
    chip_parts = [_pair_sum(for_my_core, from_sibling, "pair_sum_dw_in")]
    direct = [grad_slices(n) for n in CONV_NAMES] + [grads[n].reshape(N_DEV, -1, LANES) for n in SMALL_MATRICES]
    (grad_x, grads["norm1_w"]), (recv_w_in, recv_direct) = _mm(
        dproj, w_pt, epilogue=_norm_bwd_epilogue(xt, norm1_w, dh1), name="d_hn1",
        hosted=[(_chip_exchange_phase, chip_parts, chip_parts), (_exchange_phase, direct, direct)])
    recv_first = list(recv_w_in) + list(recv_direct[:len(CONV_NAMES)])
    recv_mats = recv_direct[len(CONV_NAMES):]

    def as_rows(n, a):
        return a.reshape(1, -1) if n in SMALL_VECTORS else a.reshape(-1, LANES)

    vec_g = jnp.concatenate([jnp.pad(as_rows(n, grads[n]), ((0, 0), (0, (-grads[n].size) % LANES)))
                             for n in SMALL_VECTORS], axis=1)
    gathered = _all_gather([vec_g] + list(_sum_sources(recv_mats, "sum_gate_matrix_grads")), "gather_small_grads")
    vec_parts = gathered[0].reshape(N_DEV, -1)
    mat_g = [g.reshape(-1, LANES) for g in gathered[1:]]

    recv = dict(zip(EXCHANGED_IN_SSD + EXCHANGED_IN_LRU + NEEDED_FIRST,
                    list(recv_in_ssd) + list(recv_in_lru) + list(recv_first)))
    big_out = {n: _adamw(local(n, weights[n]), recv[n], local(n, given["m_" + n]), local(n, given["v_" + n]),
                         "adamw_" + n) for n in BIG_NAMES}

    def state(n):
        return tuple(as_rows(n, given[p + n]) for p in ("", "m_", "v_"))

    vec_out, mat_out = _adamw_small(vec_parts, [state(n) for n in SMALL_VECTORS],
                                    mat_g, [state(n) for n in SMALL_MATRICES])
    small_out = dict(zip(SMALL_VECTORS, vec_out))
    small_out.update({n: (g,) + out for n, g, out in zip(SMALL_MATRICES, mat_g, mat_out)})

    order = list(given)[1:24]
    results = []
    for q in range(4):
        vals = {n: as_output(n, big_out[n][q]) for n in BIG_NAMES}
        vals.update({n: out[q].reshape(weights[n].shape) for n, out in small_out.items()})
        results.extend(vals[n] for n in order)
    return (loss, grad_x[None], *results)
```

```python
import math

import jax
import jax.numpy as jnp
from jax import lax
from jax.experimental import pallas as pl
from jax.experimental.pallas import tpu as pltpu

F32 = jnp.float32
BF16 = jnp.bfloat16
HIGHEST = lax.Precision.HIGHEST
MESH = pl.DeviceIdType.MESH
AXES = ("x", "y", "c")
N_DEV = 8

D_MODEL = 1024
SSM_INNER = 2048
SSM_HEADS = 32
SSM_HEAD_DIM = 64
SSM_GROUPS = 4
SSM_STATE = 128
SSM_BC = SSM_GROUPS * SSM_STATE
SSM_CONV_DIM = SSM_INNER + 2 * SSM_BC
SSM_CHUNK = 128
SSM_PAIRS = SSM_HEADS // 2
CONV_K = 4
LRU_WIDTH = 1280
LRU_BLOCKS = 10
LRU_C = 8.0
FFN_HIDDEN = 2816
RMS_EPS = 1e-6
IN_PROJ = 9760

COL_GATES = 0
COL_Z = 2048
COL_XBC = 4096
COL_LRU = 7168
COL_DT = 9728
PROJ_W = 9856
ORIG_DT = 7168
ORIG_LRU_X = 7200
ORIG_LRU_Y = 8480
W_IN_MAIN = 7040

ADAM_LR = 0.001
ADAM_B1 = 0.9
ADAM_B2 = 0.999
ADAM_EPS = 1e-08
ADAM_WD = 0.01
ADAM_STEP = 10

LANES = 128
SUBLANES = 8
V7X_VMEM_BYTES = 64 * 1024 * 1024
VMEM_LIMIT = V7X_VMEM_BYTES * 3 // 4
VMEM_LIMIT_BIG = V7X_VMEM_BYTES * 15 // 16

NT = (((1,), (1,)), ((), ()))
TN = (((0,), (0,)), ((), ()))


def _params(sem=None, big=False):
    return pltpu.CompilerParams(dimension_semantics=sem,
                                vmem_limit_bytes=VMEM_LIMIT_BIG if big else VMEM_LIMIT)


def _blk(dim, cap):
    if dim <= cap:
        return dim
    for m in range(cap // LANES, 0, -1):
        if dim % (m * LANES) == 0:
            return m * LANES
    raise ValueError(f"no block for {dim}")


def _rows(t):
    return min(t, 256)


def _sigmoid(v):
    return 1.0 / (1.0 + jnp.exp(-v))


def _softplus(v):
    e = jnp.exp(-jnp.abs(v))
    u = 1.0 + e
    log1p = jnp.where(u == 1.0, e, jnp.log(u) * e / jnp.where(u == 1.0, 1.0, u - 1.0))
    return jnp.maximum(v, 0.0) + log1p


def _iota(shape, dim):
    return lax.broadcasted_iota(jnp.int32, shape, dim)


def _shift_down(v, s):
    if s == 0:
        return v
    return jnp.where(_iota(v.shape, 0) >= s, pltpu.roll(v, s, 0), 0.0)


def _shift_up(v, s):
    if s == 0:
        return v
    n = v.shape[0]
    return jnp.where(_iota(v.shape, 0) < n - s, pltpu.roll(v, n - s, 0), 0.0)


def _bdot(a, b, dn=None):
    a = a.astype(BF16)
    b = b.astype(BF16)
    if dn is None:
        return jnp.dot(a, b, preferred_element_type=F32)
    return lax.dot_general(a, b, dn, preferred_element_type=F32)


def _split_dot(a, e, dn=None):
    hi = a.astype(BF16)
    lo = (a - hi.astype(F32)).astype(BF16)
    return _bdot(hi, e, dn) + _bdot(lo, e, dn)


def _fdot(a, b, dn=None):
    if dn is None:
        return jnp.dot(a, b, precision=HIGHEST, preferred_element_type=F32)
    return lax.dot_general(a, b, dn, precision=HIGHEST, preferred_element_type=F32)


def _mm(a, b, *, ta=False, tb=False, add=None, hosted=(), out_dtype=F32, epilogue=None, b_tail=None, b_main=0, name):
    if ta:
        kdim, m = a.shape
    else:
        m, kdim = a.shape
    if tb:
        n, k2 = b.shape
    else:
        k2, n = b.shape
    if b_tail is not None:
        if tb:
            n = b_main + b_tail.shape[0]
        else:
            k2 = b_main + b_tail.shape[0]
    assert kdim == k2, (a.shape, b.shape, ta, tb)
    if epilogue is None:
        rows, vecs, row_dtypes, n_vec_out = ([] if add is None else [add]), [], [out_dtype], 0

        def finish(r, row_vals, vec_vals):
            return ((r + row_vals[0]) if row_vals else r,), ()
    else:
        assert add is None
        finish, rows, vecs, row_dtypes, n_vec_out = epilogue
    bm, bn, bk = _blk(m, 1408 if epilogue is None else 512), _blk(n, 1408), _blk(kdim, 1408)
    grid = (m // bm, n // bn, kdim // bk)
    nk = grid[2]
    assert n_vec_out == 0 or grid[1] == 1, "column sums are accumulated over the row tiles of whole rows"
    dn = (((0 if ta else 1,), (1 if tb else 0,)), ((), ()))
    n_ab = 2 if b_tail is None else 3
    main_blocks = b_main // (bn if tb else bk)
    assert b_main % (bn if tb else bk) == 0
    n_in = n_ab + len(rows) + len(vecs)
    n_out = len(row_dtypes) + n_vec_out
    sizes = [len(arrays) for _, arrays, _ in hosted]
    n_ex = sum(sizes)

    def body(*refs):
        a_ref, b_ref = refs[:2]
        row_refs, vec_refs = refs[n_ab:n_ab + len(rows)], refs[n_ab + len(rows):n_in]
        out_refs = refs[n_in + n_ex:n_in + n_ex + n_out]
        acc = refs[n_in + 2 * n_ex + n_out]
        comms, at = [], 0
        for g, size in enumerate(sizes):
            sems = refs[n_in + 2 * n_ex + n_out + 1 + 3 * g:n_in + 2 * n_ex + n_out + 4 + 3 * g]
            comms.append((refs[n_in + at:n_in + at + size],
                          refs[n_in + n_ex + n_out + at:n_in + n_ex + n_out + at + size]) + tuple(sems))
            at += size
        step = (pl.program_id(0) * grid[1] + pl.program_id(1)) * nk + pl.program_id(2)
        k = pl.program_id(2)
        if n_ex:
            @pl.when(step == 0)
            def _():
                for (phase_fn, _, _), comm in zip(hosted, comms):
                    phase_fn("start", *comm)

        @pl.when(k == 0)
        def _():
            acc[...] = jnp.zeros_like(acc)

        bv = b_ref[...]
        if b_tail is not None:
            bv = jnp.where(pl.program_id(1 if tb else 2) < main_blocks, bv, refs[2][...])
        acc[...] += lax.dot_general(a_ref[...].astype(BF16), bv.astype(BF16), dn, preferred_element_type=F32)

        @pl.when(k == nk - 1)
        def _():
            row_outs, col_sums = finish(acc[...], [r[...] for r in row_refs], [v[...] for v in vec_refs])
            for o_ref, val in zip(out_refs, row_outs):
                o_ref[...] = val.astype(o_ref.dtype)
            for o_ref, val in zip(out_refs[len(row_dtypes):], col_sums):
                @pl.when(pl.program_id(0) == 0)
                def _():
                    o_ref[...] = val

                @pl.when(pl.program_id(0) > 0)
                def _():
                    o_ref[...] += val

        if n_ex:
            @pl.when(step == grid[0] * grid[1] * nk - 1)
            def _():
                for (phase_fn, _, _), comm in zip(hosted, comms):
                    phase_fn("finish", *comm)

    a_spec = pl.BlockSpec((bk, bm), lambda i, j, k: (k, i)) if ta else pl.BlockSpec((bm, bk), lambda i, j, k: (i, k))
    b_specs = [pl.BlockSpec((bn, bk), lambda i, j, k: (j, k)) if tb else pl.BlockSpec((bk, bn), lambda i, j, k: (k, j))]
    if b_tail is not None:
        last = main_blocks - 1
        if tb:
            b_specs = [pl.BlockSpec((bn, bk), lambda i, j, k: (jnp.minimum(j, last), k)),
                       pl.BlockSpec((bn, bk), lambda i, j, k: (jnp.maximum(j - main_blocks, 0), k))]
        else:
            b_specs = [pl.BlockSpec((bk, bn), lambda i, j, k: (jnp.minimum(k, last), j)),
                       pl.BlockSpec((bk, bn), lambda i, j, k: (jnp.maximum(k - main_blocks, 0), j))]
    o_spec = pl.BlockSpec((bm, bn), lambda i, j, k: (i, j))
    v_spec = pl.BlockSpec((1, bn), lambda i, j, k: (0, j))
    in_specs = [a_spec] + b_specs + [o_spec] * len(rows) + [v_spec] * len(vecs) + [HBM] * n_ex
    args = ([a, b] + ([] if b_tail is None else [b_tail]) + list(rows) + list(vecs)
            + [p for _, arrays, _ in hosted for p in arrays])
    sequential = n_ex or n_vec_out
    out = pl.pallas_call(
        body, name=name, grid=grid,
        in_specs=in_specs, out_specs=[o_spec] * len(row_dtypes) + [v_spec] * n_vec_out + [HBM] * n_ex,
        out_shape=[jax.ShapeDtypeStruct((m, n), dt) for dt in row_dtypes]
        + [jax.ShapeDtypeStruct((1, n), F32)] * n_vec_out
        + [jax.ShapeDtypeStruct(r.shape, r.dtype) for _, _, results in hosted for r in results],
        scratch_shapes=[pltpu.VMEM((bm, bn), F32)] + [s for size in sizes for s in _comm_scratch(size)],
        compiler_params=_params(("arbitrary",) * 3 if sequential else ("parallel", "parallel", "arbitrary")),
    )(*args)
    result = out[0] if n_out == 1 else tuple(out[:n_out])
    if not n_ex:
        return result
    received, at = [], n_out
    for size in sizes:
        received.append(out[at:at + size])
        at += size
    return result, received


def _rmsnorm_fwd(x, w, name):
    t, d = x.shape
    tr = _rows(t)

    def body(x_ref, w_ref, o_ref):
        xv = x_ref[...]
        rstd = lax.rsqrt(jnp.mean(xv * xv, axis=-1, keepdims=True) + RMS_EPS)
        o_ref[...] = (xv * rstd * w_ref[...]).astype(BF16)

    return pl.pallas_call(
        body, name=name, grid=(t // tr,),
        in_specs=[pl.BlockSpec((tr, d), lambda i: (i, 0)), pl.BlockSpec((1, d), lambda i: (0, 0))],
        out_specs=pl.BlockSpec((tr, d), lambda i: (i, 0)),
        out_shape=jax.ShapeDtypeStruct((t, d), BF16),
        compiler_params=_params(("parallel",)),
    )(x, w)


def _normalize(h):
    rstd = lax.rsqrt(jnp.mean(h * h, axis=-1, keepdims=True) + RMS_EPS)
    return rstd, h * rstd


def _residual_norm_epilogue(x, w):
    def finish(r, rows, vecs):
        h = r + rows[0]
        return (h, _normalize(h)[1] * vecs[0]), ()

    return finish, [x], [w], [F32, BF16], 0


def _norm_bwd_epilogue(x, w, dres):
    def finish(r, rows, vecs):
        rstd, xhat = _normalize(rows[0])
        dxhat = r * vecs[0]
        m = jnp.mean(dxhat * xhat, axis=-1, keepdims=True)
        return (rstd * (dxhat - xhat * m) + rows[1],), (jnp.sum(r * xhat, axis=0, keepdims=True),)

    return finish, [x, dres], [w], [F32], 1


def _loss_epilogue(h1, w, tgt):
    d = h1.shape[1]

    def finish(r, rows, vecs):
        rstd, xhat = _normalize(r + rows[0])
        err = xhat * vecs[0] - rows[1]
        dyv = err * (1.0 / d)
        dxhat = dyv * vecs[0]
        m = jnp.mean(dxhat * xhat, axis=-1, keepdims=True)
        return ((rstd * (dxhat - xhat * m),),
                (jnp.sum(dyv * xhat, axis=0, keepdims=True), jnp.sum(err * err, axis=0, keepdims=True)))

    return finish, [h1, tgt], [w], [F32], 2


def _merge_fwd(proj, bg, ys, yl):
    t = proj.shape[0]
    d = D_MODEL
    tr = _rows(t)

    def body(ps_ref, pl_ref, bg_ref, ys_ref, yl_ref, o_ref):
        gs = _sigmoid(ps_ref[...] + bg_ref[:, 0:d])
        gl = _sigmoid(pl_ref[...] + bg_ref[:, d:2 * d])
        o_ref[...] = (gs * ys_ref[...] + gl * yl_ref[...]).astype(BF16)

    row = pl.BlockSpec((tr, d), lambda i: (i, 0))
    return pl.pallas_call(
        body, name="merge_fwd", grid=(t // tr,),
        in_specs=[row, pl.BlockSpec((tr, d), lambda i: (i, 1)), pl.BlockSpec((1, 2 * d), lambda i: (0, 0)), row, row],
        out_specs=row, out_shape=jax.ShapeDtypeStruct((t, d), BF16),
        compiler_params=_params(("parallel",)),
    )(proj, proj, bg, ys, yl)


def _merge_bwd(proj, bg, ys, yl, dm):
    t = proj.shape[0]
    d = D_MODEL
    tr = _rows(t)

    def body(ps_ref, pl_ref, bg_ref, ys_ref, yl_ref, dm_ref, dys_ref, dyl_ref, dg_ref, dbg_ref):
        i = pl.program_id(0)
        gs = _sigmoid(ps_ref[...] + bg_ref[:, 0:d])
        gl = _sigmoid(pl_ref[...] + bg_ref[:, d:2 * d])
        dmv = dm_ref[...]
        dys_ref[...] = (dmv * gs).astype(BF16)
        dyl_ref[...] = (dmv * gl).astype(BF16)
        dgs = dmv * ys_ref[...] * gs * (1.0 - gs)
        dgl = dmv * yl_ref[...] * gl * (1.0 - gl)
        dg_ref[:, 0:d] = dgs.astype(BF16)
        dg_ref[:, d:2 * d] = dgl.astype(BF16)

        @pl.when(i == 0)
        def _():
            dbg_ref[...] = jnp.zeros_like(dbg_ref)

        dbg_ref[:, 0:d] += jnp.sum(dgs, axis=0, keepdims=True)
        dbg_ref[:, d:2 * d] += jnp.sum(dgl, axis=0, keepdims=True)

    row = pl.BlockSpec((tr, d), lambda i: (i, 0))
    wide = pl.BlockSpec((tr, 2 * d), lambda i: (i, 0))
    vec = pl.BlockSpec((1, 2 * d), lambda i: (0, 0))
    return pl.pallas_call(
        body, name="merge_bwd", grid=(t // tr,),
        in_specs=[row, pl.BlockSpec((tr, d), lambda i: (i, 1)), vec, row, row, row],
        out_specs=[row, row, wide, vec],
        out_shape=[jax.ShapeDtypeStruct((t, d), BF16), jax.ShapeDtypeStruct((t, d), BF16),
                   jax.ShapeDtypeStruct((t, PROJ_W), BF16), jax.ShapeDtypeStruct((1, 2 * d), F32)],
        compiler_params=_params(("arbitrary",)),
    )(proj, proj, bg, ys, yl, dm)


def _ffn_in_swiglu(hn, wt):
    t, d = hn.shape
    f = FFN_HIDDEN
    bm, bn = _blk(t, 1024), _blk(f, 1408)
    nj = f // bn

    def body(a_ref, wg_ref, wu_ref, g_ref, u_ref, act_ref):
        a = a_ref[...]
        g = _bdot(a, wg_ref[...], NT)
        u = _bdot(a, wu_ref[...], NT)
        g_ref[...] = g.astype(BF16)
        u_ref[...] = u.astype(BF16)
        act_ref[...] = (g * _sigmoid(g) * u).astype(BF16)

    out = pl.BlockSpec((bm, bn), lambda i, j: (i, j))
    shape = jax.ShapeDtypeStruct((t, f), BF16)
    return pl.pallas_call(
        body, name="ffn_in_swiglu", grid=(t // bm, nj),
        in_specs=[pl.BlockSpec((bm, d), lambda i, j: (i, 0)), pl.BlockSpec((bn, d), lambda i, j: (j, 0)),
                  pl.BlockSpec((bn, d), lambda i, j: (nj + j, 0))],
        out_specs=[out, out, out], out_shape=[shape, shape, shape],
        compiler_params=_params(("parallel", "parallel")),
    )(hn, wt, wt)


def _swiglu_bwd(g_all, u_all, dact):
    t, f = g_all.shape
    tr = _rows(t)

    def body(g_ref, u_ref, da_ref, o_ref):
        g = g_ref[...].astype(F32)
        sg = _sigmoid(g)
        da = da_ref[...].astype(F32)
        o_ref[:, 0:f] = (da * u_ref[...].astype(F32) * (sg * (1.0 + g * (1.0 - sg)))).astype(BF16)
        o_ref[:, f:2 * f] = (da * g * sg).astype(BF16)

    row = pl.BlockSpec((tr, f), lambda i: (i, 0))
    return pl.pallas_call(
        body, name="swiglu_bwd", grid=(t // tr,),
        in_specs=[row, row, row],
        out_specs=pl.BlockSpec((tr, 2 * f), lambda i: (i, 0)),
        out_shape=jax.ShapeDtypeStruct((t, 2 * f), BF16),
        compiler_params=_params(("parallel",)),
    )(g_all, u_all, dact)


def _conv_pre(xv, wv, bv):
    pre = bv + wv[CONV_K - 1:CONV_K, :] * xv
    for k in range(CONV_K - 1):
        pre = pre + wv[k:k + 1, :] * _shift_down(xv, CONV_K - 1 - k)
    return pre


def _ssm_conv_fwd(proj, w, b):
    t = proj.shape[0]
    nb = SSM_CONV_DIM // LANES
    c0 = COL_XBC // LANES

    def body(x_ref, w_ref, b_ref, o_ref):
        pre = _conv_pre(x_ref[...], w_ref[...], b_ref[...])
        o_ref[...] = (pre * _sigmoid(pre)).astype(BF16)

    return pl.pallas_call(
        body, name="ssm_conv_fwd", grid=(nb,),
        in_specs=[pl.BlockSpec((t, LANES), lambda j: (0, c0 + j)), pl.BlockSpec((CONV_K, LANES), lambda j: (0, j)),
                  pl.BlockSpec((1, LANES), lambda j: (0, j))],
        out_specs=pl.BlockSpec((t, LANES), lambda j: (0, j)),
        out_shape=jax.ShapeDtypeStruct((t, SSM_CONV_DIM), BF16),
        compiler_params=_params(("parallel",)),
    )(proj, w, b)


def _ssm_conv_bwd(proj, w, b, dact, dproj):
    t = proj.shape[0]
    nb = SSM_CONV_DIM // LANES
    c0 = COL_XBC // LANES

    def body(x_ref, w_ref, b_ref, da_ref, dproj_in, dx_ref, dw_ref, db_ref):
        xv = x_ref[...]
        wv = w_ref[...]
        pre = _conv_pre(xv, wv, b_ref[...])
        sg = _sigmoid(pre)
        dpre = da_ref[...] * (sg * (1.0 + pre * (1.0 - sg)))
        dx = wv[CONV_K - 1:CONV_K, :] * dpre
        for k in range(CONV_K - 1):
            dx = dx + wv[k:k + 1, :] * _shift_up(dpre, CONV_K - 1 - k)
        dx_ref[...] = dx.astype(BF16)
        for k in range(CONV_K):
            dw_ref[k:k + 1, :] = jnp.sum(dpre * _shift_down(xv, CONV_K - 1 - k), axis=0, keepdims=True)
        db_ref[...] = jnp.sum(dpre, axis=0, keepdims=True)

    col = pl.BlockSpec((t, LANES), lambda j: (0, j))
    wsp = pl.BlockSpec((CONV_K, LANES), lambda j: (0, j))
    bsp = pl.BlockSpec((1, LANES), lambda j: (0, j))
    return pl.pallas_call(
        body, name="ssm_conv_bwd", grid=(nb,),
        in_specs=[pl.BlockSpec((t, LANES), lambda j: (0, c0 + j)), wsp, bsp, col, HBM],
        out_specs=[pl.BlockSpec((t, LANES), lambda j: (0, c0 + j)), wsp, bsp],
        out_shape=[jax.ShapeDtypeStruct(dproj.shape, dproj.dtype), jax.ShapeDtypeStruct((CONV_K, SSM_CONV_DIM), F32),
                   jax.ShapeDtypeStruct((1, SSM_CONV_DIM), F32)],
        input_output_aliases={4: 0},
        compiler_params=_params(("parallel",)),
    )(proj, w, b, dact, dproj)


def _ssd_chunk_terms(dtr, bias, alog):
    a = -jnp.exp(alog)
    dt = _softplus(dtr + bias)
    row = _iota((SSM_CHUNK, SSM_CHUNK), 0)
    col = _iota((SSM_CHUNK, SSM_CHUNK), 1)
    tri = (row >= col).astype(F32)
    cs = _fdot(tri, dt * a)
    dec = jnp.exp(cs[SSM_CHUNK - 1:SSM_CHUNK, :] - cs)
    ecs = jnp.exp(cs)
    off = _iota((LANES, SSM_INNER), 1) - SSM_HEAD_DIM * _iota((LANES, SSM_INNER), 0)
    expand = jnp.where(jnp.logical_and(off >= 0, off < SSM_HEAD_DIM), 1.0, 0.0).astype(BF16)
    return a, dt, cs, dec, ecs, expand, row, col


def _ssd_specs(t):
    nc = t // SSM_CHUNK
    xs = pl.BlockSpec((SSM_CHUNK, SSM_INNER), lambda c: (c, 0))
    bm = pl.BlockSpec((SSM_CHUNK, SSM_BC), lambda c: (c, SSM_INNER // SSM_BC))
    cm = pl.BlockSpec((SSM_CHUNK, SSM_BC), lambda c: (c, SSM_INNER // SSM_BC + 1))
    dtr = pl.BlockSpec((SSM_CHUNK, LANES), lambda c: (c, COL_DT // LANES))
    vec = pl.BlockSpec((1, LANES), lambda c: (0, 0))
    wide = pl.BlockSpec((1, SSM_INNER), lambda c: (0, 0))
    return nc, xs, bm, cm, dtr, vec, wide


def _ssd_fwd(xbc_act, proj, bias, alog, dexp, gather):
    t = proj.shape[0]
    nc, xs_s, bm_s, cm_s, dtr_s, vec, wide = _ssd_specs(t)
    n = len(gather)

    def body(*refs):
        xs_ref, b_ref, c_ref, dtr_ref, bias_ref, alog_ref, dexp_ref = refs[:7]
        y_ref, sin_ref = refs[7 + n:9 + n]
        state = refs[9 + 2 * n]
        comm = (refs[7:7 + n], refs[9 + n:9 + 2 * n]) + tuple(refs[10 + 2 * n:])
        chunk = pl.program_id(0)

        @pl.when(chunk == 0)
        def _():
            _gather_phase("start", *comm)
            state[...] = jnp.zeros_like(state)

        @pl.when(chunk == (3 * nc) // 4)
        def _():
            _gather_phase("forward", *comm)

        a, dt, cs, dec, ecs, expand, row, col = _ssd_chunk_terms(dtr_ref[...], bias_ref[...], alog_ref[...])
        cst = cs.T
        dt_x = _split_dot(dt, expand)
        dec_x = _split_dot(dec, expand)
        ecs_x = _split_dot(ecs, expand)
        xs = xs_ref[...].astype(F32)
        xdt = xs * dt_x
        xdec = xdt * dec_x
        lane_lo = col < SSM_HEAD_DIM
        causal = row >= col
        sin_ref[0] = state[...]
        for g in range(SSM_GROUPS):
            bg = b_ref[:, g * SSM_STATE:(g + 1) * SSM_STATE].astype(BF16)
            cg = c_ref[:, g * SSM_STATE:(g + 1) * SSM_STATE].astype(BF16)
            cb = _bdot(cg, bg, NT)
            for q in range(SSM_PAIRS // SSM_GROUPS):
                pq = g * (SSM_PAIRS // SSM_GROUPS) + q
                sl = slice(pq * LANES, (pq + 1) * LANES)
                xp = xdt[:, sl].astype(BF16)
                yd = []
                for hh in range(2):
                    h = 2 * pq + hh
                    lmat = jnp.exp(jnp.where(causal, cs[:, h:h + 1] - cst[h:h + 1, :], -jnp.inf))
                    yd.append(_bdot(cb * lmat, xp))
                s_in = state[pq]
                y_off = _bdot(cg, s_in) * ecs_x[:, sl]
                y_ref[:, sl] = (jnp.where(lane_lo, yd[0], yd[1]) + y_off + xs[:, sl] * dexp_ref[:, sl]).astype(BF16)
                state[pq] = s_in * ecs_x[SSM_CHUNK - 1:SSM_CHUNK, sl] + _bdot(bg, xdec[:, sl], TN)

        @pl.when(chunk == nc - 1)
        def _():
            _gather_phase("finish", *comm)

    out = pl.pallas_call(
        body, name="ssd_fwd", grid=(nc,),
        in_specs=[xs_s, bm_s, cm_s, dtr_s, vec, vec, wide] + [HBM] * n,
        out_specs=[pl.BlockSpec((SSM_CHUNK, SSM_INNER), lambda c: (c, 0)),
                   pl.BlockSpec((1, SSM_PAIRS, SSM_STATE, LANES), lambda c: (c, 0, 0, 0))] + [HBM] * n,
        out_shape=[jax.ShapeDtypeStruct((t, SSM_INNER), BF16),
                   jax.ShapeDtypeStruct((nc, SSM_PAIRS, SSM_STATE, LANES), F32)]
        + [jax.ShapeDtypeStruct((N_DEV,) + v.shape, v.dtype) for v in gather],
        scratch_shapes=[pltpu.VMEM((SSM_PAIRS, SSM_STATE, LANES), F32)] + _comm_scratch(n),
        compiler_params=_params(("arbitrary",)),
    )(xbc_act, xbc_act, xbc_act, proj, bias, alog, dexp, *gather)
    return out[:2], out[2:]


def _ssd_bwd(xbc_act, proj, s_in_all, dy, bias, alog, dexp, dproj, exchange):
    n_ex = len(exchange)
    t = proj.shape[0]
    nc = t // SSM_CHUNK
    last = nc - 1
    xs_s = pl.BlockSpec((SSM_CHUNK, SSM_INNER), lambda c: (last - c, 0))
    bm_s = pl.BlockSpec((SSM_CHUNK, SSM_BC), lambda c: (last - c, SSM_INNER // SSM_BC))
    cm_s = pl.BlockSpec((SSM_CHUNK, SSM_BC), lambda c: (last - c, SSM_INNER // SSM_BC + 1))
    dtr_s = pl.BlockSpec((SSM_CHUNK, LANES), lambda c: (last - c, COL_DT // LANES))
    sin_s = pl.BlockSpec((1, SSM_PAIRS, SSM_STATE, LANES), lambda c: (last - c, 0, 0, 0))
    vec = pl.BlockSpec((1, LANES), lambda c: (0, 0))
    wide = pl.BlockSpec((1, SSM_INNER), lambda c: (0, 0))

    def body(*refs):
        xs_ref, b_ref, c_ref, dtr_ref, sin_ref, dy_ref, bias_ref, alog_ref, dexp_ref = refs[:9]
        dxbc_ref, ddtr_ref, dbias_ref, dalog_ref, ddcol_ref = refs[10 + n_ex:15 + n_ex]
        dstate, dxdt_s, yoff_s, rx_s, trow_s = refs[15 + 2 * n_ex:20 + 2 * n_ex]
        comm = (refs[10:10 + n_ex], refs[15 + n_ex:15 + 2 * n_ex]) + tuple(refs[20 + 2 * n_ex:])

        @pl.when(pl.program_id(0) == 0)
        def _():
            _exchange_phase("start", *comm)
            dstate[...] = jnp.zeros_like(dstate)
            trow_s[...] = jnp.zeros_like(trow_s)
            dbias_ref[...] = jnp.zeros_like(dbias_ref)
            dalog_ref[...] = jnp.zeros_like(dalog_ref)
            ddcol_ref[...] = jnp.zeros_like(ddcol_ref)

        dtr = dtr_ref[...]
        a, dt, cs, dec, ecs, expand, row, col = _ssd_chunk_terms(dtr, bias_ref[...], alog_ref[...])
        cst = cs.T
        dt_x = _split_dot(dt, expand)
        dec_x = _split_dot(dec, expand)
        ecs_x = _split_dot(ecs, expand)
        xs = xs_ref[...].astype(F32)
        dyv = dy_ref[...].astype(F32)
        xdt = xs * dt_x
        lane_lo = col < SSM_HEAD_DIM
        causal = row >= col
        ddcol_ref[...] += jnp.sum(dyv * xs, axis=0, keepdims=True)
        dcs_col = jnp.zeros((SSM_CHUNK, LANES), F32)
        dcs_row = jnp.zeros((LANES, SSM_CHUNK), F32)
        for g in range(SSM_GROUPS):
            bg = b_ref[:, g * SSM_STATE:(g + 1) * SSM_STATE].astype(BF16)
            cg = c_ref[:, g * SSM_STATE:(g + 1) * SSM_STATE].astype(BF16)
            cb = _bdot(cg, bg, NT)
            dgm = jnp.zeros((SSM_CHUNK, SSM_CHUNK), F32)
            dbg = jnp.zeros((SSM_CHUNK, SSM_STATE), F32)
            dcg = jnp.zeros((SSM_CHUNK, SSM_STATE), F32)
            for q in range(SSM_PAIRS // SSM_GROUPS):
                pq = g * (SSM_PAIRS // SSM_GROUPS) + q
                sl = slice(pq * LANES, (pq + 1) * LANES)
                dyp = dyv[:, sl]
                xp = xdt[:, sl]
                dxh = []
                for hh in range(2):
                    h = 2 * pq + hh
                    lmat = jnp.exp(jnp.where(causal, cs[:, h:h + 1] - cst[h:h + 1, :], -jnp.inf))
                    mmat = cb * lmat
                    dyh = jnp.where(lane_lo if hh == 0 else jnp.logical_not(lane_lo), dyp, 0.0)
                    dmm = _bdot(dyh, xp, NT)
                    pm = dmm * mmat
                    dcs_col = jnp.where(col == h, jnp.sum(pm, axis=1, keepdims=True), dcs_col)
                    dcs_row = jnp.where(row == h, jnp.sum(pm, axis=0, keepdims=True), dcs_row)
                    dgm = dgm + dmm * lmat
                    dxh.append(_bdot(mmat, dyp, TN))
                s_in = sin_ref[0, pq]
                ecs_p = ecs_x[:, sl]
                dec_p = dec_x[:, sl]
                etot_p = ecs_x[SSM_CHUNK - 1:SSM_CHUNK, sl]
                yoff_s[:, sl] = dyp * (_bdot(cg, s_in) * ecs_p)
                dq = dyp * ecs_p
                dcg = dcg + _bdot(dq, s_in, NT)
                ds = dstate[pq]
                r = _bdot(bg, ds)
                rx_s[:, sl] = r * xp
                dxdt_s[:, sl] = jnp.where(lane_lo, dxh[0], dxh[1]) + dec_p * r
                dbg = dbg + _bdot(xp * dec_p, ds, NT)
                trow_s[0:1, sl] = jnp.sum(ds * s_in, axis=0, keepdims=True) * etot_p
                dstate[pq] = etot_p * ds + _bdot(cg, dq, TN)
            dcg = dcg + _bdot(dgm, bg)
            dbg = dbg + _bdot(dgm, cg, TN)
            dxbc_ref[:, SSM_INNER + g * SSM_STATE:SSM_INNER + (g + 1) * SSM_STATE] = dbg.astype(BF16)
            dxbc_ref[:, SSM_INNER + SSM_BC + g * SSM_STATE:SSM_INNER + SSM_BC + (g + 1) * SSM_STATE] = dcg.astype(BF16)
        ddec = _split_dot(rx_s[...], expand, NT) * dec
        dtot = _split_dot(trow_s[...], expand, NT)[0:1, :]
        dcs = dcs_col - dcs_row.T + _split_dot(yoff_s[...], expand, NT) - ddec
        dcs = dcs + jnp.where(row == SSM_CHUNK - 1, jnp.sum(ddec, axis=0, keepdims=True) + dtot, 0.0)
        da = _fdot((row <= col).astype(F32), dcs)
        dxdt = dxdt_s[...]
        ddt = da * a + _split_dot(dxdt * xs, expand, NT)
        dalog_ref[...] += jnp.sum(da * dt, axis=0, keepdims=True) * a
        ddtr = ddt * _sigmoid(dtr + bias_ref[...])
        ddtr_ref[...] = ddtr.astype(BF16)
        dbias_ref[...] += jnp.sum(ddtr, axis=0, keepdims=True)
        dxbc_ref[:, 0:SSM_INNER] = (dxdt * dt_x + dyv * dexp_ref[...]).astype(BF16)

        @pl.when(pl.program_id(0) == last)
        def _():
            _exchange_phase("finish", *comm)

    out = pl.pallas_call(
        body, name="ssd_bwd", grid=(nc,),
        in_specs=[xs_s, bm_s, cm_s, dtr_s, sin_s, pl.BlockSpec((SSM_CHUNK, SSM_INNER), lambda c: (last - c, 0)),
                  vec, vec, wide, HBM] + [HBM] * n_ex,
        out_specs=[pl.BlockSpec((SSM_CHUNK, SSM_CONV_DIM), lambda c: (last - c, 0)), dtr_s, vec, vec, wide]
        + [HBM] * n_ex,
        out_shape=[jax.ShapeDtypeStruct((t, SSM_CONV_DIM), BF16), jax.ShapeDtypeStruct(dproj.shape, dproj.dtype),
                   jax.ShapeDtypeStruct((1, LANES), F32), jax.ShapeDtypeStruct((1, LANES), F32),
                   jax.ShapeDtypeStruct((1, SSM_INNER), F32)]
        + [jax.ShapeDtypeStruct(p.shape, p.dtype) for p in exchange],
        input_output_aliases={9: 1},
        scratch_shapes=[pltpu.VMEM((SSM_PAIRS, SSM_STATE, LANES), F32),
                        pltpu.VMEM((SSM_CHUNK, SSM_INNER), F32), pltpu.VMEM((SSM_CHUNK, SSM_INNER), F32),
                        pltpu.VMEM((SSM_CHUNK, SSM_INNER), F32), pltpu.VMEM((SUBLANES, SSM_INNER), F32)]
        + _comm_scratch(n_ex),
        compiler_params=_params(("arbitrary",)),
    )(xbc_act, xbc_act, xbc_act, proj, s_in_all, dy, bias, alog, dexp, dproj, *exchange)
    return out[:5], out[5:]


def _group_rstd(y):
    n = SSM_INNER // SSM_GROUPS
    parts = []
    for g in range(SSM_GROUPS):
        yg = y[:, g * n:(g + 1) * n]
        r = lax.rsqrt(jnp.mean(yg * yg, axis=-1, keepdims=True) + RMS_EPS)
        parts.append(jnp.broadcast_to(r, yg.shape))
    return jnp.concatenate(parts, axis=1)


def _group_mean(v):
    n = SSM_INNER // SSM_GROUPS
    parts = []
    for g in range(SSM_GROUPS):
        vg = v[:, g * n:(g + 1) * n]
        parts.append(jnp.broadcast_to(jnp.mean(vg, axis=-1, keepdims=True), vg.shape))
    return jnp.concatenate(parts, axis=1)


def _ssm_post_fwd(y_ssd, proj, nw):
    t = proj.shape[0]
    n = SSM_INNER
    tr = _rows(t)

    def body(y_ref, z_ref, nw_ref, o_ref):
        z = z_ref[...]
        y = y_ref[...] * (z * _sigmoid(z))
        o_ref[...] = (y * _group_rstd(y) * nw_ref[...]).astype(BF16)

    row = pl.BlockSpec((tr, n), lambda i: (i, 0))
    return pl.pallas_call(
        body, name="ssm_post_fwd", grid=(t // tr,),
        in_specs=[row, pl.BlockSpec((tr, n), lambda i: (i, COL_Z // n)), pl.BlockSpec((1, n), lambda i: (0, 0))],
        out_specs=row, out_shape=jax.ShapeDtypeStruct((t, n), BF16),
        compiler_params=_params(("parallel",)),
    )(y_ssd, proj, nw)


def _ssm_post_bwd(y_ssd, proj, nw, dout, dproj):
    t = proj.shape[0]
    n = SSM_INNER
    tr = _rows(t)

    def body(y_ref, z_ref, nw_ref, do_ref, dproj_in, dy_ref, dz_ref, dnw_ref):
        i = pl.program_id(0)
        z = z_ref[...]
        sg = _sigmoid(z)
        sz = z * sg
        ys = y_ref[...]
        y = ys * sz
        rstd = _group_rstd(y)
        yn = y * rstd
        dov = do_ref[...]
        dyn = dov * nw_ref[...]
        dyg = rstd * (dyn - yn * _group_mean(dyn * yn))
        dy_ref[...] = (dyg * sz).astype(BF16)
        dz_ref[...] = (dyg * ys * (sg * (1.0 + z * (1.0 - sg)))).astype(BF16)
        part = jnp.sum(dov * yn, axis=0, keepdims=True)

        @pl.when(i == 0)
        def _():
            dnw_ref[...] = part

        @pl.when(i > 0)
        def _():
            dnw_ref[...] += part

    row = pl.BlockSpec((tr, n), lambda i: (i, 0))
    vec = pl.BlockSpec((1, n), lambda i: (0, 0))
    return pl.pallas_call(
        body, name="ssm_post_bwd", grid=(t // tr,),
        in_specs=[row, pl.BlockSpec((tr, n), lambda i: (i, COL_Z // n)), vec, row, HBM],
        out_specs=[row, pl.BlockSpec((tr, n), lambda i: (i, COL_Z // n)), vec],
        out_shape=[jax.ShapeDtypeStruct((t, n), BF16), jax.ShapeDtypeStruct(dproj.shape, dproj.dtype),
                   jax.ShapeDtypeStruct((1, n), F32)],
        input_output_aliases={4: 1},
        compiler_params=_params(("arbitrary",)),
    )(y_ssd, proj, nw, dout, dproj)


SCAN_UNROLL = 8
GELU_C = math.sqrt(2.0 / math.pi)
GELU_K = 0.044715


def _gelu_parts(y):
    th = jnp.tanh(GELU_C * (y + GELU_K * y * y * y))
    val = 0.5 * y * (1.0 + th)
    grad = 0.5 * (1.0 + th) + 0.5 * y * (1.0 - th * th) * GELU_C * (1.0 + 3.0 * GELU_K * y * y)
    return val, grad


def _scan_tiles(a_ref, b_ref, h_ref, n_rows, reverse):
    n_tiles = n_rows // SUBLANES
    shape = (SUBLANES, a_ref.shape[1])
    row = _iota(shape, 0)

    def in_tile(av, bv):
        for s in (1, 2, 4):
            if reverse:
                keep = row < SUBLANES - s
                a_sh = jnp.where(keep, pltpu.roll(av, SUBLANES - s, 0), 1.0)
                b_sh = jnp.where(keep, pltpu.roll(bv, SUBLANES - s, 0), 0.0)
            else:
                keep = row >= s
                a_sh = jnp.where(keep, pltpu.roll(av, s, 0), 1.0)
                b_sh = jnp.where(keep, pltpu.roll(bv, s, 0), 0.0)
            bv = av * b_sh + bv
            av = av * a_sh
        return av, bv

    def step(k, carry):
        first = (n_tiles // SCAN_UNROLL - 1 - k) if reverse else k
        tiles = [first * SCAN_UNROLL + j for j in range(SCAN_UNROLL)]
        if reverse:
            tiles = tiles[::-1]
        ats = [pl.ds(pl.multiple_of(tile * SUBLANES, SUBLANES), SUBLANES) for tile in tiles]
        scanned = [in_tile(a_ref[at, :], b_ref[at, :]) for at in ats]
        for at, (av, bv) in zip(ats, scanned):
            hv = bv + av * carry
            h_ref[at, :] = hv
            carry = hv[0:1, :] if reverse else hv[SUBLANES - 1:SUBLANES, :]
        return carry

    assert n_tiles % SCAN_UNROLL == 0, n_rows
    lax.fori_loop(0, n_tiles // SCAN_UNROLL, step, jnp.zeros((1, a_ref.shape[1]), F32))


def _lru_gates(xl, cw, cb, wr, br, wi, bi, lam):
    u = cb + cw[CONV_K - 1:CONV_K, :] * xl
    for k in range(CONV_K - 1):
        u = u + cw[k:k + 1, :] * _shift_down(xl, CONV_K - 1 - k)
    r = _sigmoid(_bdot(u, wr) + br)
    i = _sigmoid(_bdot(u, wi) + bi)
    sp = _softplus(-lam)
    la = -LRU_C * r * sp
    a = jnp.exp(la)
    mult = jnp.sqrt(-jnp.tanh(la) * (a * a + 1.0))
    return u, r, i, sp, a, mult


def _lru_specs(t):
    c0 = COL_LRU // LANES
    xl = pl.BlockSpec((t, LANES), lambda j: (0, c0 + 2 * j))
    yl = pl.BlockSpec((t, LANES), lambda j: (0, c0 + 2 * j + 1))
    col = pl.BlockSpec((t, LANES), lambda j: (0, j))
    cw = pl.BlockSpec((CONV_K, LANES), lambda j: (0, j))
    vec = pl.BlockSpec((1, LANES), lambda j: (0, j))
    wblk = pl.BlockSpec((1, LANES, LANES), lambda j: (j, 0, 0))
    return xl, yl, col, cw, vec, wblk


def _lru_fwd(proj, cw, cb, wr, br, wi, bi, lam, gather):
    t = proj.shape[0]
    xl_s, yl_s, col, cw_s, vec, wblk = _lru_specs(t)
    n = len(gather)

    def body(*refs):
        xl_ref, yl_ref, cw_ref, cb_ref, wr_ref, br_ref, wi_ref, bi_ref, lam_ref = refs[:9]
        o_ref, h_ref = refs[9 + n:11 + n]
        a_s, b_s = refs[11 + 2 * n:13 + 2 * n]
        comm = (refs[9:9 + n], refs[11 + n:11 + 2 * n]) + tuple(refs[13 + 2 * n:])
        j = pl.program_id(0)
        for step, phase in ((0, "start"), (LRU_BLOCKS - 2, "forward")):
            @pl.when(j == step)
            def _():
                _gather_phase(phase, *comm)

        u, r, i, sp, a, mult = _lru_gates(xl_ref[...], cw_ref[...], cb_ref[...], wr_ref[0], br_ref[...],
                                          wi_ref[0], bi_ref[...], lam_ref[...])
        a_s[...] = a
        b_s[...] = mult * (i * u)
        _scan_tiles(a_s, b_s, h_ref, t, reverse=False)
        o_ref[...] = (h_ref[...] * _gelu_parts(yl_ref[...])[0]).astype(BF16)

        @pl.when(j == LRU_BLOCKS - 1)
        def _():
            _gather_phase("finish", *comm)

    out = pl.pallas_call(
        body, name="lru_fwd", grid=(LRU_BLOCKS,),
        in_specs=[xl_s, yl_s, cw_s, vec, wblk, vec, wblk, vec, vec] + [HBM] * n,
        out_specs=[col, col] + [HBM] * n,
        out_shape=[jax.ShapeDtypeStruct((t, LRU_WIDTH), BF16), jax.ShapeDtypeStruct((t, LRU_WIDTH), F32)]
        + [jax.ShapeDtypeStruct((N_DEV,) + v.shape, v.dtype) for v in gather],
        scratch_shapes=[pltpu.VMEM((t, LANES), F32)] * 2 + _comm_scratch(n),
        compiler_params=_params(("arbitrary",), big=True),
    )(proj, proj, cw, cb, wr, br, wi, bi, lam, *gather)
    return out[:2], out[2:]


def _lru_bwd(proj, cw, cb, wr, br, wi, bi, lam, h_all, dout, dproj, exchange):
    t = proj.shape[0]
    xl_s, yl_s, col, cw_s, vec, wblk = _lru_specs(t)
    pair = pl.BlockSpec((t, 2 * LANES), lambda j: (0, COL_LRU // (2 * LANES) + j))
    n_ex = len(exchange)

    def body(*refs):
        xl_ref, yl_ref, cw_ref, cb_ref, wr_ref, br_ref, wi_ref, bi_ref, lam_ref, h_ref, do_ref = refs[:11]
        dxy_ref, dcw_ref, dcb_ref, dwr_ref, dbr_ref, dwi_ref, dbi_ref, dlam_ref = refs[12 + n_ex:20 + n_ex]
        a_s, b_s, g_s = refs[20 + 2 * n_ex:23 + 2 * n_ex]
        comm = (refs[12:12 + n_ex], refs[20 + n_ex:20 + 2 * n_ex]) + tuple(refs[23 + 2 * n_ex:])

        @pl.when(pl.program_id(0) == 0)
        def _():
            _exchange_phase("start", *comm)

        xl = xl_ref[...]
        cwv = cw_ref[...]
        lam = lam_ref[...]
        u, r, i, sp, a, mult = _lru_gates(xl, cwv, cb_ref[...], wr_ref[0], br_ref[...], wi_ref[0], bi_ref[...], lam)
        v = i * u
        gl, dgl = _gelu_parts(yl_ref[...])
        dov = do_ref[...]
        h = h_ref[...]
        dxy_ref[:, LANES:2 * LANES] = (dov * h * dgl).astype(BF16)
        b_s[...] = dov * gl
        a_s[...] = _shift_up(a, 1)
        _scan_tiles(a_s, b_s, g_s, t, reverse=True)
        g = g_s[...]
        da = g * _shift_down(h, 1)
        dmult = g * v
        dv = g * mult
        dla = da * a - dmult * (a * a) / mult
        dr = dla * (-LRU_C * sp)
        dsp = jnp.sum(dla * (-LRU_C * r), axis=0, keepdims=True)
        dlam_ref[...] = -dsp * _sigmoid(-lam)
        dpr = dr * r * (1.0 - r)
        dpi = dv * u * i * (1.0 - i)
        dbr_ref[...] = jnp.sum(dpr, axis=0, keepdims=True)
        dbi_ref[...] = jnp.sum(dpi, axis=0, keepdims=True)
        dwr_ref[0] = _bdot(u, dpr, TN)
        dwi_ref[0] = _bdot(u, dpi, TN)
        du = dv * i + _bdot(dpr, wr_ref[0], NT) + _bdot(dpi, wi_ref[0], NT)
        dxl = cwv[CONV_K - 1:CONV_K, :] * du
        for k in range(CONV_K - 1):
            dxl = dxl + cwv[k:k + 1, :] * _shift_up(du, CONV_K - 1 - k)
        dxy_ref[:, 0:LANES] = dxl.astype(BF16)
        for k in range(CONV_K):
            dcw_ref[k:k + 1, :] = jnp.sum(du * _shift_down(xl, CONV_K - 1 - k), axis=0, keepdims=True)
        dcb_ref[...] = jnp.sum(du, axis=0, keepdims=True)

        @pl.when(pl.program_id(0) == LRU_BLOCKS - 1)
        def _():
            _exchange_phase("finish", *comm)

    out = pl.pallas_call(
        body, name="lru_bwd", grid=(LRU_BLOCKS,),
        in_specs=[xl_s, yl_s, cw_s, vec, wblk, vec, wblk, vec, vec, col, col, HBM] + [HBM] * n_ex,
        out_specs=[pair, cw_s, vec, wblk, vec, wblk, vec, vec] + [HBM] * n_ex,
        input_output_aliases={11: 0},
        out_shape=[jax.ShapeDtypeStruct(dproj.shape, dproj.dtype),
                   jax.ShapeDtypeStruct((CONV_K, LRU_WIDTH), F32), jax.ShapeDtypeStruct((1, LRU_WIDTH), F32),
                   jax.ShapeDtypeStruct((LRU_BLOCKS, LANES, LANES), F32), jax.ShapeDtypeStruct((1, LRU_WIDTH), F32),
                   jax.ShapeDtypeStruct((LRU_BLOCKS, LANES, LANES), F32), jax.ShapeDtypeStruct((1, LRU_WIDTH), F32),
                   jax.ShapeDtypeStruct((1, LRU_WIDTH), F32)]
        + [jax.ShapeDtypeStruct(p.shape, p.dtype) for p in exchange],
        scratch_shapes=[pltpu.VMEM((t, LANES), F32)] * 3 + _comm_scratch(n_ex),
        compiler_params=_params(("arbitrary",), big=True),
    )(proj, proj, cw, cb, wr, br, wi, bi, lam, h_all, dout, dproj, *exchange)
    return out[:8], out[8:]


def _mesh_pos():
    return lax.axis_index("x"), lax.axis_index("y"), lax.axis_index("c")


HBM = pl.BlockSpec(memory_space=pl.ANY)


def _comm_scratch(n):
    return [pltpu.SemaphoreType.DMA((n, 7)), pltpu.SemaphoreType.DMA((n, 7)), pltpu.SemaphoreType.DMA((n,))]


def _gather_phase(phase, v_refs, out_refs, send_sems, recv_sems, local_sems):
    n = len(v_refs)
    x, y, c = _mesh_pos()
    me, sibling = (x, y, c), (x, y, 1 - c)
    chips = [(1 - x, y), (x, 1 - y), (1 - x, 1 - y)]

    def block(a, px, py, pc):
        return out_refs[a].at[4 * px + 2 * py + pc]

    def copy(a, k, blk, to, src=None):
        return pltpu.make_async_remote_copy(
            src_ref=block(a, *blk) if src is None else src, dst_ref=block(a, *blk),
            send_sem=send_sems.at[a, k], recv_sem=recv_sems.at[a, k], device_id=to, device_id_type=MESH)

    def own(a):
        return pltpu.make_async_copy(v_refs[a], block(a, *me), local_sems.at[a])

    def first(a):
        return ([copy(a, 0, me, sibling, src=v_refs[a])]
                + [copy(a, 1 + j, me, (*chip, c), src=v_refs[a]) for j, chip in enumerate(chips)])

    def forward(a, j):
        return copy(a, 4 + j, (*chips[j], c), sibling)

    if phase == "start":
        for a in range(n):
            own(a).start()
        for a in range(n):
            for cp in first(a):
                cp.start()
    elif phase == "forward":
        for j in range(3):
            for a in range(n):
                copy(a, 1 + j, (*chips[j], c), me).wait_recv()
                forward(a, j).start()
    else:
        for a in range(n):
            copy(a, 0, sibling, me).wait_recv()
            for j in range(3):
                copy(a, 4 + j, (*chips[j], 1 - c), me).wait_recv()
        for a in range(n):
            for cp in first(a) + [forward(a, j) for j in range(3)]:
                cp.wait_send()
            own(a).wait()


def _all_gather(vs, name):
    n = len(vs)

    def body(*refs):
        comm = (refs[:n], refs[n:2 * n]) + tuple(refs[2 * n:])
        for phase in ("start", "forward", "finish"):
            _gather_phase(phase, *comm)

    return pl.pallas_call(
        body, name=name,
        out_shape=[jax.ShapeDtypeStruct((N_DEV,) + v.shape, v.dtype) for v in vs],
        in_specs=[HBM] * n, out_specs=[HBM] * n, scratch_shapes=_comm_scratch(n),
    )(*vs)


def _run_copies(phase, local, remote):
    if phase == "start":
        for cp in local + remote:
            cp.start()
    else:
        for cp in remote:
            cp.wait()
        for cp in local:
            cp.wait()


def _exchange_phase(phase, p_refs, out_refs, send_sems, recv_sems, local_sems):
    n = len(p_refs)
    x, y, c = _mesh_pos()
    me = 4 * x + 2 * y + c
    local = [pltpu.make_async_copy(p_refs[a].at[me], out_refs[a].at[me], local_sems.at[a]) for a in range(n)]
    remote = []
    for k in range(1, N_DEV):
        px = (1 - x) if k & 4 else x
        py = (1 - y) if k & 2 else y
        pc = (1 - c) if k & 1 else c
        for a in range(n):
            remote.append(pltpu.make_async_remote_copy(
                src_ref=p_refs[a].at[4 * px + 2 * py + pc], dst_ref=out_refs[a].at[me],
                send_sem=send_sems.at[a, k - 1], recv_sem=recv_sems.at[a, k - 1],
                device_id=(px, py, pc), device_id_type=MESH))
    _run_copies(phase, local, remote)


def _chip_exchange_phase(phase, p_refs, out_refs, send_sems, recv_sems, local_sems):
    n = len(p_refs)
    x, y, c = _mesh_pos()
    me = 2 * x + y
    local = [pltpu.make_async_copy(p_refs[a].at[me], out_refs[a].at[me], local_sems.at[a]) for a in range(n)]
    remote = []
    for k in range(1, 4):
        px = (1 - x) if k & 2 else x
        py = (1 - y) if k & 1 else y
        for a in range(n):
            remote.append(pltpu.make_async_remote_copy(
                src_ref=p_refs[a].at[2 * px + py], dst_ref=out_refs[a].at[me],
                send_sem=send_sems.at[a, k - 1], recv_sem=recv_sems.at[a, k - 1],
                device_id=(px, py, c), device_id_type=MESH))
    _run_copies(phase, local, remote)


def _sibling_exchange(parts, name):
    chips = N_DEV // 2

    def body(p_ref, out_ref, send_sems, recv_sems):
        x, y, c = _mesh_pos()
        copies = [pltpu.make_async_remote_copy(
            src_ref=p_ref.at[2 * q + 1 - c], dst_ref=out_ref.at[q], send_sem=send_sems.at[q], recv_sem=recv_sems.at[q],
            device_id=(x, y, 1 - c), device_id_type=MESH) for q in range(chips)]
        _run_copies("start", [], copies)
        _run_copies("finish", [], copies)

    return pl.pallas_call(
        body, name=name, out_shape=jax.ShapeDtypeStruct((chips,) + parts.shape[1:], parts.dtype),
        in_specs=[HBM], out_specs=HBM,
        scratch_shapes=[pltpu.SemaphoreType.DMA((chips,)), pltpu.SemaphoreType.DMA((chips,))],
    )(parts)


def _pair_sum(parts, theirs, name):
    slots, rows, cols = theirs.shape
    tc = 256
    core = lax.axis_index("c").astype(jnp.int32).reshape(1)

    def body(c_ref, a_ref, b_ref, o_ref):
        o_ref[...] = (a_ref[...].astype(F32) + b_ref[...].astype(F32)).astype(o_ref.dtype)

    spec = pl.BlockSpec((1, rows, tc), lambda q, j, c: (q, 0, j))
    return pl.pallas_call(
        body, name=name,
        grid_spec=pltpu.PrefetchScalarGridSpec(
            num_scalar_prefetch=1, grid=(slots, cols // tc),
            in_specs=[pl.BlockSpec((1, rows, tc), lambda q, j, c: (2 * q + c[0], 0, j)), spec], out_specs=spec),
        out_shape=jax.ShapeDtypeStruct(theirs.shape, theirs.dtype),
        compiler_params=_params(("parallel", "parallel")),
    )(core, parts, theirs)


def _sum_sources(recvs, name):
    k = len(recvs)

    def body(*refs):
        for r_ref, o_ref in zip(refs[:k], refs[k:]):
            acc = r_ref[0].astype(F32)
            for s in range(1, r_ref.shape[0]):
                acc = acc + r_ref[s].astype(F32)
            o_ref[...] = acc

    return pl.pallas_call(
        body, name=name, out_shape=[jax.ShapeDtypeStruct(r.shape[1:], F32) for r in recvs],
        compiler_params=_params(),
    )(*recvs)


def _row_tile(rows):
    for tile in range(128, 15, -16):
        if rows % tile == 0:
            return tile
    return rows


def _adam_update(w, g, m, v):
    nm = ADAM_B1 * m + (1.0 - ADAM_B1) * g
    nv = ADAM_B2 * v + (1.0 - ADAM_B2) * (g * g)
    m_hat = nm / (1.0 - ADAM_B1 ** ADAM_STEP)
    v_hat = nv / (1.0 - ADAM_B2 ** ADAM_STEP)
    return -ADAM_LR * (m_hat / (jnp.sqrt(v_hat) + ADAM_EPS) + ADAM_WD * w), nm, nv


def _vector_offsets(widths):
    offsets, end = [], 0
    for c in widths:
        offsets.append(end)
        end += c + (-c) % LANES
    return offsets, end


def _adamw_small(vec_parts, vec_state, mat_grads, mat_state):
    widths = [w.shape[1] for w, _, _ in vec_state]
    offsets, total = _vector_offsets(widths)
    assert vec_parts.shape == (N_DEV, total), (vec_parts.shape, total)
    n_vec, n_mat = len(vec_state), len(mat_state)

    def body(*refs):
        r_ref = refs[0]
        vec_in = refs[1:1 + 3 * n_vec]
        mat_in = refs[1 + 3 * n_vec:1 + 3 * n_vec + 4 * n_mat]
        outs = refs[1 + 3 * n_vec + 4 * n_mat:]
        for i, (off, c) in enumerate(zip(offsets, widths)):
            g = r_ref[0:1, off:off + c]
            for s in range(1, N_DEV):
                g = g + r_ref[s:s + 1, off:off + c]
            w_ref, m_ref, v_ref = vec_in[3 * i:3 * i + 3]
            g_out, d_out, m_out, v_out = outs[4 * i:4 * i + 4]
            g_out[...] = g
            d_out[...], m_out[...], v_out[...] = _adam_update(w_ref[...], g, m_ref[...], v_ref[...])
        for j in range(n_mat):
            g_ref, w_ref, m_ref, v_ref = mat_in[4 * j:4 * j + 4]
            d_out, m_out, v_out = outs[4 * n_vec + 3 * j:4 * n_vec + 3 * j + 3]
            d_out[...], m_out[...], v_out[...] = _adam_update(w_ref[...], g_ref[...], m_ref[...], v_ref[...])

    args = [vec_parts] + [a for state in vec_state for a in state]
    for g, state in zip(mat_grads, mat_state):
        args += [g, *state]
    out_shape = [jax.ShapeDtypeStruct(w.shape, F32) for w, _, _ in vec_state for _ in range(4)]
    out_shape += [jax.ShapeDtypeStruct(w.shape, F32) for w, _, _ in mat_state for _ in range(3)]
    out = pl.pallas_call(body, name="adamw_replicated", out_shape=out_shape, compiler_params=_params())(*args)
    vec_out = [tuple(out[4 * i:4 * i + 4]) for i in range(n_vec)]
    mat_out = [tuple(out[4 * n_vec + 3 * j:4 * n_vec + 3 * j + 3]) for j in range(n_mat)]
    return vec_out, mat_out


def _adamw(w, recv, m, v, name):
    rows, width = w.shape
    n = recv.shape[0]
    if rows % 16 == 0 or width % 256:
        tr, tc = _row_tile(rows), width
    else:
        tr, tc = rows, 256

    def body(w_ref, r_ref, m_ref, v_ref, g_ref, d_ref, nm_ref, nv_ref):
        gv = r_ref[0].astype(F32)
        for s in range(1, n):
            gv = gv + r_ref[s].astype(F32)
        g_ref[...] = gv
        d_ref[...], nm_ref[...], nv_ref[...] = _adam_update(w_ref[...], gv, m_ref[...], v_ref[...])

    spec = pl.BlockSpec((tr, tc), lambda i, j: (i, j))
    shape = jax.ShapeDtypeStruct((rows, width), F32)
    return pl.pallas_call(
        body, name=name, grid=(rows // tr, width // tc),
        in_specs=[spec, pl.BlockSpec((n, tr, tc), lambda i, j: (0, i, j)), spec, spec],
        out_specs=[spec] * 4, out_shape=[shape] * 4,
        compiler_params=_params(("parallel", "parallel")),
    )(w, recv, m, v)


BIG_NAMES = ("w_in", "w_out_ssm", "w_out_lru", "w_out", "w_ffn_in", "w_ffn_out", "ssm_conv_w", "lru_conv_w")
TRANSPOSED = ("w_in", "w_ffn_in")
CONV_NAMES = ("ssm_conv_w", "lru_conv_w")
MATMUL_NAMES = BIG_NAMES[:6]
NEEDED_FIRST = ("w_in", "ssm_conv_w", "lru_conv_w")
GATHERED_IN_SSD = ("w_ffn_in",)
GATHERED_IN_LRU = ("w_ffn_out", "w_out_ssm", "w_out_lru", "w_out")
EXCHANGED_IN_SSD = ("w_ffn_in", "w_ffn_out")
EXCHANGED_IN_LRU = ("w_out_ssm", "w_out_lru", "w_out")
SMALL_VECTORS = ("norm1_w", "b_branch_gate", "ssm_conv_b", "ssm_dt_bias", "ssm_a_log", "ssm_d", "ssm_norm_w",
                 "lru_conv_b", "lru_b_r", "lru_b_i", "lru_lambda", "norm2_w", "norm_f_w")
SMALL_MATRICES = ("lru_w_r", "lru_w_i")


def _col_shards(full):
    rows, cols = full.shape
    return full.reshape(rows, N_DEV, cols // N_DEV).transpose(1, 0, 2)


def _from_col_shards(g):
    n, rows, w = g.shape
    return g.transpose(1, 0, 2).reshape(rows, n * w)


def kernel(x, norm1_w, w_in, b_branch_gate, ssm_conv_w, ssm_conv_b, ssm_dt_bias, ssm_a_log, ssm_d, ssm_norm_w, w_out_ssm, lru_conv_w, lru_conv_b, lru_w_r, lru_b_r, lru_w_i, lru_b_i, lru_lambda, w_out_lru, w_out, norm2_w, w_ffn_in, w_ffn_out, norm_f_w, loss_target, m_norm1_w, m_w_in, m_b_branch_gate, m_ssm_conv_w, m_ssm_conv_b, m_ssm_dt_bias, m_ssm_a_log, m_ssm_d, m_ssm_norm_w, m_w_out_ssm, m_lru_conv_w, m_lru_conv_b, m_lru_w_r, m_lru_b_r, m_lru_w_i, m_lru_b_i, m_lru_lambda, m_w_out_lru, m_w_out, m_norm2_w, m_w_ffn_in, m_w_ffn_out, m_norm_f_w, v_norm1_w, v_w_in, v_b_branch_gate, v_ssm_conv_w, v_ssm_conv_b, v_ssm_dt_bias, v_ssm_a_log, v_ssm_d, v_ssm_norm_w, v_w_out_ssm, v_lru_conv_w, v_lru_conv_b, v_lru_w_r, v_lru_b_r, v_lru_w_i, v_lru_b_i, v_lru_lambda, v_w_out_lru, v_w_out, v_norm2_w, v_w_ffn_in, v_w_ffn_out, v_norm_f_w):
    given = dict(locals())
    weights = {n: given[n] for n in BIG_NAMES + SMALL_VECTORS + SMALL_MATRICES}
    t = x.shape[1]
    xt = x[0]
    tgt = loss_target[0]

    def local(n, a):
        return a[0].T if n in TRANSPOSED else a[0]

    def as_output(n, a):
        return a.T[None] if n in TRANSPOSED else a[None]

    def shard(n):
        s = local(n, weights[n])
        return s.astype(BF16) if n in MATMUL_NAMES else s

    def unshard(n, g):
        return _from_col_shards(g) if n in CONV_NAMES else g.reshape(-1, g.shape[-1])

    def grad_slices(n):
        g = grads[n]
        return (_col_shards(g) if n in CONV_NAMES else g.reshape(N_DEV, -1, g.shape[-1])).astype(BF16)

    gathered = _all_gather([shard(n) for n in NEEDED_FIRST], "gather_in_weights")
    full = {n: unshard(n, g) for n, g in zip(NEEDED_FIRST, gathered)}
    ssm_cw, lru_cw = full["ssm_conv_w"], full["lru_conv_w"]
    wi_t = full["w_in"]
    lru_rows = wi_t[ORIG_LRU_X:].reshape(2, LRU_BLOCKS, LANES, D_MODEL).transpose(1, 0, 2, 3)
    w_tail = jnp.concatenate([wi_t[W_IN_MAIN:ORIG_DT], lru_rows.reshape(2 * LRU_WIDTH, D_MODEL),
                              wi_t[ORIG_DT:ORIG_LRU_X], jnp.zeros((PROJ_W - IN_PROJ, D_MODEL), BF16)], axis=0)

    def pad_heads(a):
        return jnp.pad(a.reshape(1, SSM_HEADS), ((0, 0), (0, LANES - SSM_HEADS)))

    dt_bias_p = pad_heads(ssm_dt_bias)
    a_log_p = pad_heads(ssm_a_log)
    d_exp = jnp.repeat(ssm_d.reshape(SSM_HEADS), SSM_HEAD_DIM).reshape(1, SSM_INNER)
    lru_wr, lru_wi = lru_w_r[0], lru_w_i[0]

    hn1 = _rmsnorm_fwd(xt, norm1_w, "norm1_fwd")
    proj = _mm(hn1, wi_t, tb=True, b_tail=w_tail, b_main=W_IN_MAIN, name="in_proj")
    xbc_act = _ssm_conv_fwd(proj, ssm_cw, ssm_conv_b)
    (y_ssd, s_in_all), gathered = _ssd_fwd(xbc_act, proj, dt_bias_p, a_log_p, d_exp,
                                           gather=[shard(n) for n in GATHERED_IN_SSD])
    full.update({n: unshard(n, g) for n, g in zip(GATHERED_IN_SSD, gathered)})
    (l_out, h_lru), gathered = _lru_fwd(proj, lru_cw, lru_conv_b, lru_wr, lru_b_r, lru_wi, lru_b_i, lru_lambda,
                                        gather=[shard(n) for n in GATHERED_IN_LRU])
    full.update({n: unshard(n, g) for n, g in zip(GATHERED_IN_LRU, gathered)})
    y_pre = _ssm_post_fwd(y_ssd, proj, ssm_norm_w)
    y_ssm = _mm(y_pre, full["w_out_ssm"], out_dtype=BF16, name="out_ssm")
    y_lru = _mm(l_out, full["w_out_lru"], out_dtype=BF16, name="out_lru")
    merged = _merge_fwd(proj, b_branch_gate, y_ssm, y_lru)
    h1, hn2 = _mm(merged, full["w_out"], epilogue=_residual_norm_epilogue(xt, norm2_w), name="out_proj")
    gate, up, act = _ffn_in_swiglu(hn2, full["w_ffn_in"])

    grads = {}
    dh2, grads["norm_f_w"], loss_cols = _mm(
        act, full["w_ffn_out"], epilogue=_loss_epilogue(h1, norm_f_w.reshape(1, D_MODEL), tgt), name="ffn_out")
    loss = lax.psum(0.5 * jnp.sum(loss_cols) / D_MODEL, AXES)
    dact = _mm(dh2, full["w_ffn_out"], tb=True, out_dtype=BF16, name="d_act")
    grads["w_ffn_out"] = _mm(act, dh2, ta=True, out_dtype=BF16, name="dw_ffn_out")
    dgu = _swiglu_bwd(gate, up, dact)
    dh1, grads["norm2_w"] = _mm(dgu, full["w_ffn_in"], epilogue=_norm_bwd_epilogue(h1, norm2_w, dh2), name="d_hn2")
    grads["w_ffn_in"] = _mm(dgu, hn2, ta=True, out_dtype=BF16, name="dw_ffn_in")
    dmerged = _mm(dh1, full["w_out"], tb=True, out_dtype=BF16, name="d_merged")
    grads["w_out"] = _mm(merged, dh1, ta=True, out_dtype=BF16, name="dw_out")
    dy_ssm, dy_lru, dproj, grads["b_branch_gate"] = _merge_bwd(proj, b_branch_gate, y_ssm, y_lru, dmerged)
    dy_pre = _mm(dy_ssm, full["w_out_ssm"], tb=True, out_dtype=BF16, name="d_y_pre")
    grads["w_out_ssm"] = _mm(y_pre, dy_ssm, ta=True, out_dtype=BF16, name="dw_out_ssm")
    dl_out = _mm(dy_lru, full["w_out_lru"], tb=True, out_dtype=BF16, name="d_l_out")
    grads["w_out_lru"] = _mm(l_out, dy_lru, ta=True, out_dtype=BF16, name="dw_out_lru")
    dy_ssd, dproj, grads["ssm_norm_w"] = _ssm_post_bwd(y_ssd, proj, ssm_norm_w, dy_pre, dproj)
    (dxbc_act, dproj, dbias, dalog, ddcol), recv_in_ssd = _ssd_bwd(
        xbc_act, proj, s_in_all, dy_ssd, dt_bias_p, a_log_p, d_exp, dproj,
        exchange=[grad_slices(n) for n in EXCHANGED_IN_SSD])
    grads["ssm_dt_bias"] = dbias[:, :SSM_HEADS]
    grads["ssm_a_log"] = dalog[:, :SSM_HEADS]
    grads["ssm_d"] = ddcol.reshape(SSM_HEADS, SSM_HEAD_DIM).sum(axis=1).reshape(1, SSM_HEADS)
    dproj, grads["ssm_conv_w"], grads["ssm_conv_b"] = _ssm_conv_bwd(proj, ssm_cw, ssm_conv_b, dxbc_act, dproj)
    ((dproj, grads["lru_conv_w"], grads["lru_conv_b"], dwr, grads["lru_b_r"], dwi, grads["lru_b_i"],
      grads["lru_lambda"]), recv_in_lru) = _lru_bwd(
        proj, lru_cw, lru_conv_b, lru_wr, lru_b_r, lru_wi, lru_b_i, lru_lambda, h_lru, dl_out, dproj,
        exchange=[grad_slices(n) for n in EXCHANGED_IN_LRU])
    grads["lru_w_r"], grads["lru_w_i"] = dwr[None], dwi[None]
    dwpt = _mm(dproj, hn1, ta=True, out_dtype=BF16, name="dw_in")
    lru_rows = dwpt[COL_LRU:COL_DT].reshape(LRU_BLOCKS, 2, LANES, D_MODEL).transpose(1, 0, 2, 3)
    grads["w_in"] = jnp.concatenate([dwpt[:ORIG_DT], dwpt[COL_DT:COL_DT + SSM_HEADS],
                                     lru_rows.reshape(2 * LRU_WIDTH, D_MODEL)], axis=0)
    w_in_parts = grad_slices("w_in")
    from_sibling = _sibling_exchange(w_in_parts, "sibling_exchange_dw_in")
    chip_parts = [_pair_sum(w_in_parts, from_sibling, "pair_sum_dw_in")]
    direct = [grad_slices(n) for n in CONV_NAMES] + [grads[n].reshape(N_DEV, -1, LANES) for n in SMALL_MATRICES]
    (grad_x, grads["norm1_w"]), (recv_w_in, recv_direct) = _mm(
        dproj, wi_t, b_tail=w_tail, b_main=W_IN_MAIN, epilogue=_norm_bwd_epilogue(xt, norm1_w, dh1), name="d_hn1",
        hosted=[(_chip_exchange_phase, chip_parts, chip_parts), (_exchange_phase, direct, direct)])
    recv_first = list(recv_w_in) + list(recv_direct[:len(CONV_NAMES)])
    recv_mats = recv_direct[len(CONV_NAMES):]

    def as_rows(n, a):
        return a.reshape(1, -1) if n in SMALL_VECTORS else a.reshape(-1, LANES)

    vec_g = jnp.concatenate([jnp.pad(as_rows(n, grads[n]), ((0, 0), (0, (-grads[n].size) % LANES)))
                             for n in SMALL_VECTORS], axis=1)
    gathered = _all_gather([vec_g] + list(_sum_sources(recv_mats, "sum_gate_matrix_grads")), "gather_small_grads")
    vec_parts = gathered[0].reshape(N_DEV, -1)
    mat_g = [g.reshape(-1, LANES) for g in gathered[1:]]

    recv = dict(zip(EXCHANGED_IN_SSD + EXCHANGED_IN_LRU + NEEDED_FIRST,
                    list(recv_in_ssd) + list(recv_in_lru) + list(recv_first)))
    big_out = {n: _adamw(local(n, weights[n]), recv[n], local(n, given["m_" + n]), local(n, given["v_" + n]),
                         "adamw_" + n) for n in BIG_NAMES}

    def state(n):
        return tuple(as_rows(n, given[p + n]) for p in ("", "m_", "v_"))

    vec_out, mat_out = _adamw_small(vec_parts, [state(n) for n in SMALL_VECTORS],
                                    mat_g, [state(n) for n in SMALL_MATRICES])
    small_out = dict(zip(SMALL_VECTORS, vec_out))
    small_out.update({n: (g,) + out for n, g, out in zip(SMALL_MATRICES, mat_g, mat_out)})

    order = list(given)[1:24]
    results = []
    for q in range(4):
        vals = {n: as_output(n, big_out[n][q]) for n in BIG_NAMES}
        vals.update({n: out[q].reshape(weights[n].shape) for n, out in small_out.items()})
        results.extend(vals[n] for n in order)
    return (loss, grad_x[None], *results)
```

```python
import math

import jax
import jax.numpy as jnp
from jax import lax
from jax.experimental import pallas as pl
from jax.experimental.pallas import tpu as pltpu

F32 = jnp.float32
BF16 = jnp.bfloat16
HIGHEST = lax.Precision.HIGHEST
MESH = pl.DeviceIdType.MESH
AXES = ("x", "y", "c")
N_DEV = 8

D_MODEL = 1024
SSM_INNER = 2048
SSM_HEADS = 32
SSM_HEAD_DIM = 64
SSM_GROUPS = 4
SSM_STATE = 128
SSM_BC = SSM_GROUPS * SSM_STATE
SSM_CONV_DIM = SSM_INNER + 2 * SSM_BC
SSM_CHUNK = 128
SSM_PAIRS = SSM_HEADS // 2
CONV_K = 4
LRU_WIDTH = 1280
LRU_BLOCKS = 10
LRU_C = 8.0
FFN_HIDDEN = 2816
RMS_EPS = 1e-6
IN_PROJ = 9760

COL_GATES = 0
COL_Z = 2048
COL_XBC = 4096
COL_LRU = 7168
COL_DT = 9728
PROJ_W = 9856
ORIG_DT = 7168
ORIG_LRU_X = 7200
ORIG_LRU_Y = 8480
W_IN_MAIN = 7040

ADAM_LR = 0.001
ADAM_B1 = 0.9
ADAM_B2 = 0.999
ADAM_EPS = 1e-08
ADAM_WD = 0.01
ADAM_STEP = 10

LANES = 128
SUBLANES = 8
V7X_VMEM_BYTES = 64 * 1024 * 1024
VMEM_LIMIT = V7X_VMEM_BYTES * 3 // 4
VMEM_LIMIT_BIG = V7X_VMEM_BYTES * 15 // 16

NT = (((1,), (1,)), ((), ()))
TN = (((0,), (0,)), ((), ()))


def _params(sem=None, big=False):
    return pltpu.CompilerParams(dimension_semantics=sem,
                                vmem_limit_bytes=VMEM_LIMIT_BIG if big else VMEM_LIMIT)


def _blk(dim, cap):
    if dim <= cap:
        return dim
    for m in range(cap // LANES, 0, -1):
        if dim % (m * LANES) == 0:
            return m * LANES
    raise ValueError(f"no block for {dim}")


def _rows(t):
    return min(t, 256)


def _sigmoid(v):
    return 1.0 / (1.0 + jnp.exp(-v))


def _softplus(v):
    e = jnp.exp(-jnp.abs(v))
    u = 1.0 + e
    log1p = jnp.where(u == 1.0, e, jnp.log(u) * e / jnp.where(u == 1.0, 1.0, u - 1.0))
    return jnp.maximum(v, 0.0) + log1p


def _iota(shape, dim):
    return lax.broadcasted_iota(jnp.int32, shape, dim)


def _shift_down(v, s):
    if s == 0:
        return v
    return jnp.where(_iota(v.shape, 0) >= s, pltpu.roll(v, s, 0), 0.0)


def _shift_up(v, s):
    if s == 0:
        return v
    n = v.shape[0]
    return jnp.where(_iota(v.shape, 0) < n - s, pltpu.roll(v, n - s, 0), 0.0)


def _bdot(a, b, dn=None):
    a = a.astype(BF16)
    b = b.astype(BF16)
    if dn is None:
        return jnp.dot(a, b, preferred_element_type=F32)
    return lax.dot_general(a, b, dn, preferred_element_type=F32)


def _split_dot(a, e, dn=None):
    hi = a.astype(BF16)
    lo = (a - hi.astype(F32)).astype(BF16)
    return _bdot(hi, e, dn) + _bdot(lo, e, dn)


def _fdot(a, b, dn=None):
    if dn is None:
        return jnp.dot(a, b, precision=HIGHEST, preferred_element_type=F32)
    return lax.dot_general(a, b, dn, precision=HIGHEST, preferred_element_type=F32)


def _mm(a, b, *, ta=False, tb=False, add=None, hosted=(), out_dtype=F32, epilogue=None, b_tail=None, b_main=0, name):
    if ta:
        kdim, m = a.shape
    else:
        m, kdim = a.shape
    if tb:
        n, k2 = b.shape
    else:
        k2, n = b.shape
    if b_tail is not None:
        if tb:
            n = b_main + b_tail.shape[0]
        else:
            k2 = b_main + b_tail.shape[0]
    assert kdim == k2, (a.shape, b.shape, ta, tb)
    if epilogue is None:
        rows, vecs, row_dtypes, n_vec_out = ([] if add is None else [add]), [], [out_dtype], 0

        def finish(r, row_vals, vec_vals):
            return ((r + row_vals[0]) if row_vals else r,), ()
    else:
        assert add is None
        finish, rows, vecs, row_dtypes, n_vec_out = epilogue
    bm, bn, bk = _blk(m, 1408 if epilogue is None else 512), _blk(n, 1408), _blk(kdim, 1408)
    grid = (m // bm, n // bn, kdim // bk)
    nk = grid[2]
    assert n_vec_out == 0 or grid[1] == 1, "column sums are accumulated over the row tiles of whole rows"
    dn = (((0 if ta else 1,), (1 if tb else 0,)), ((), ()))
    n_ab = 2 if b_tail is None else 3
    main_blocks = b_main // (bn if tb else bk)
    assert b_main % (bn if tb else bk) == 0
    n_in = n_ab + len(rows) + len(vecs)
    n_out = len(row_dtypes) + n_vec_out
    sizes = [len(arrays) for _, arrays, _ in hosted]
    n_ex = sum(sizes)

    def body(*refs):
        a_ref, b_ref = refs[:2]
        row_refs, vec_refs = refs[n_ab:n_ab + len(rows)], refs[n_ab + len(rows):n_in]
        out_refs = refs[n_in + n_ex:n_in + n_ex + n_out]
        acc = refs[n_in + 2 * n_ex + n_out]
        comms, at = [], 0
        for g, size in enumerate(sizes):
            sems = refs[n_in + 2 * n_ex + n_out + 1 + 3 * g:n_in + 2 * n_ex + n_out + 4 + 3 * g]
            comms.append((refs[n_in + at:n_in + at + size],
                          refs[n_in + n_ex + n_out + at:n_in + n_ex + n_out + at + size]) + tuple(sems))
            at += size
        step = (pl.program_id(0) * grid[1] + pl.program_id(1)) * nk + pl.program_id(2)
        k = pl.program_id(2)
        if n_ex:
            @pl.when(step == 0)
            def _():
                for (phase_fn, _, _), comm in zip(hosted, comms):
                    phase_fn("start", *comm)

        @pl.when(k == 0)
        def _():
            acc[...] = jnp.zeros_like(acc)

        bv = b_ref[...]
        if b_tail is not None:
            bv = jnp.where(pl.program_id(1 if tb else 2) < main_blocks, bv, refs[2][...])
        acc[...] += lax.dot_general(a_ref[...].astype(BF16), bv.astype(BF16), dn, preferred_element_type=F32)

        @pl.when(k == nk - 1)
        def _():
            row_outs, col_sums = finish(acc[...], [r[...] for r in row_refs], [v[...] for v in vec_refs])
            for o_ref, val in zip(out_refs, row_outs):
                o_ref[...] = val.astype(o_ref.dtype)
            for o_ref, val in zip(out_refs[len(row_dtypes):], col_sums):
                @pl.when(pl.program_id(0) == 0)
                def _():
                    o_ref[...] = val

                @pl.when(pl.program_id(0) > 0)
                def _():
                    o_ref[...] += val

        if n_ex:
            @pl.when(step == grid[0] * grid[1] * nk - 1)
            def _():
                for (phase_fn, _, _), comm in zip(hosted, comms):
                    phase_fn("finish", *comm)

    a_spec = pl.BlockSpec((bk, bm), lambda i, j, k: (k, i)) if ta else pl.BlockSpec((bm, bk), lambda i, j, k: (i, k))
    b_specs = [pl.BlockSpec((bn, bk), lambda i, j, k: (j, k)) if tb else pl.BlockSpec((bk, bn), lambda i, j, k: (k, j))]
    if b_tail is not None:
        last = main_blocks - 1
        if tb:
            b_specs = [pl.BlockSpec((bn, bk), lambda i, j, k: (jnp.minimum(j, last), k)),
                       pl.BlockSpec((bn, bk), lambda i, j, k: (jnp.maximum(j - main_blocks, 0), k))]
        else:
            b_specs = [pl.BlockSpec((bk, bn), lambda i, j, k: (jnp.minimum(k, last), j)),
                       pl.BlockSpec((bk, bn), lambda i, j, k: (jnp.maximum(k - main_blocks, 0), j))]
    o_spec = pl.BlockSpec((bm, bn), lambda i, j, k: (i, j))
    v_spec = pl.BlockSpec((1, bn), lambda i, j, k: (0, j))
    in_specs = [a_spec] + b_specs + [o_spec] * len(rows) + [v_spec] * len(vecs) + [HBM] * n_ex
    args = ([a, b] + ([] if b_tail is None else [b_tail]) + list(rows) + list(vecs)
            + [p for _, arrays, _ in hosted for p in arrays])
    sequential = n_ex or n_vec_out
    out = pl.pallas_call(
        body, name=name, grid=grid,
        in_specs=in_specs, out_specs=[o_spec] * len(row_dtypes) + [v_spec] * n_vec_out + [HBM] * n_ex,
        out_shape=[jax.ShapeDtypeStruct((m, n), dt) for dt in row_dtypes]
        + [jax.ShapeDtypeStruct((1, n), F32)] * n_vec_out
        + [jax.ShapeDtypeStruct(r.shape, r.dtype) for _, _, results in hosted for r in results],
        scratch_shapes=[pltpu.VMEM((bm, bn), F32)] + [s for size in sizes for s in _comm_scratch(size)],
        compiler_params=_params(("arbitrary",) * 3 if sequential else ("parallel", "parallel", "arbitrary")),
    )(*args)
    result = out[0] if n_out == 1 else tuple(out[:n_out])
    if not n_ex:
        return result
    received, at = [], n_out
    for size in sizes:
        received.append(out[at:at + size])
        at += size
    return result, received


def _rmsnorm_fwd(x, w, name):
    t, d = x.shape
    tr = _rows(t)

    def body(x_ref, w_ref, o_ref):
        xv = x_ref[...]
        rstd = lax.rsqrt(jnp.mean(xv * xv, axis=-1, keepdims=True) + RMS_EPS)
        o_ref[...] = (xv * rstd * w_ref[...]).astype(BF16)

    return pl.pallas_call(
        body, name=name, grid=(t // tr,),
        in_specs=[pl.BlockSpec((tr, d), lambda i: (i, 0)), pl.BlockSpec((1, d), lambda i: (0, 0))],
        out_specs=pl.BlockSpec((tr, d), lambda i: (i, 0)),
        out_shape=jax.ShapeDtypeStruct((t, d), BF16),
        compiler_params=_params(("parallel",)),
    )(x, w)


def _normalize(h):
    rstd = lax.rsqrt(jnp.mean(h * h, axis=-1, keepdims=True) + RMS_EPS)
    return rstd, h * rstd


def _residual_norm_epilogue(x, w):
    def finish(r, rows, vecs):
        h = r + rows[0]
        return (h, _normalize(h)[1] * vecs[0]), ()

    return finish, [x], [w], [F32, BF16], 0


def _norm_bwd_epilogue(x, w, dres):
    def finish(r, rows, vecs):
        rstd, xhat = _normalize(rows[0])
        dxhat = r * vecs[0]
        m = jnp.mean(dxhat * xhat, axis=-1, keepdims=True)
        return (rstd * (dxhat - xhat * m) + rows[1],), (jnp.sum(r * xhat, axis=0, keepdims=True),)

    return finish, [x, dres], [w], [F32], 1


def _loss_epilogue(h1, w, tgt):
    d = h1.shape[1]

    def finish(r, rows, vecs):
        rstd, xhat = _normalize(r + rows[0])
        err = xhat * vecs[0] - rows[1]
        dyv = err * (1.0 / d)
        dxhat = dyv * vecs[0]
        m = jnp.mean(dxhat * xhat, axis=-1, keepdims=True)
        return ((rstd * (dxhat - xhat * m),),
                (jnp.sum(dyv * xhat, axis=0, keepdims=True), jnp.sum(err * err, axis=0, keepdims=True)))

    return finish, [h1, tgt], [w], [F32], 2


def _merge_fwd(proj, bg, ys, yl):
    t = proj.shape[0]
    d = D_MODEL
    tr = _rows(t)

    def body(ps_ref, pl_ref, bg_ref, ys_ref, yl_ref, o_ref):
        gs = _sigmoid(ps_ref[...] + bg_ref[:, 0:d])
        gl = _sigmoid(pl_ref[...] + bg_ref[:, d:2 * d])
        o_ref[...] = (gs * ys_ref[...] + gl * yl_ref[...]).astype(BF16)

    row = pl.BlockSpec((tr, d), lambda i: (i, 0))
    return pl.pallas_call(
        body, name="merge_fwd", grid=(t // tr,),
        in_specs=[row, pl.BlockSpec((tr, d), lambda i: (i, 1)), pl.BlockSpec((1, 2 * d), lambda i: (0, 0)), row, row],
        out_specs=row, out_shape=jax.ShapeDtypeStruct((t, d), BF16),
        compiler_params=_params(("parallel",)),
    )(proj, proj, bg, ys, yl)


def _merge_bwd(proj, bg, ys, yl, dm):
    t = proj.shape[0]
    d = D_MODEL
    tr = _rows(t)

    def body(ps_ref, pl_ref, bg_ref, ys_ref, yl_ref, dm_ref, dys_ref, dyl_ref, dg_ref, dbg_ref):
        i = pl.program_id(0)
        gs = _sigmoid(ps_ref[...] + bg_ref[:, 0:d])
        gl = _sigmoid(pl_ref[...] + bg_ref[:, d:2 * d])
        dmv = dm_ref[...]
        dys_ref[...] = (dmv * gs).astype(BF16)
        dyl_ref[...] = (dmv * gl).astype(BF16)
        dgs = dmv * ys_ref[...] * gs * (1.0 - gs)
        dgl = dmv * yl_ref[...] * gl * (1.0 - gl)
        dg_ref[:, 0:d] = dgs.astype(BF16)
        dg_ref[:, d:2 * d] = dgl.astype(BF16)

        @pl.when(i == 0)
        def _():
            dbg_ref[...] = jnp.zeros_like(dbg_ref)

        dbg_ref[:, 0:d] += jnp.sum(dgs, axis=0, keepdims=True)
        dbg_ref[:, d:2 * d] += jnp.sum(dgl, axis=0, keepdims=True)

    row = pl.BlockSpec((tr, d), lambda i: (i, 0))
    wide = pl.BlockSpec((tr, 2 * d), lambda i: (i, 0))
    vec = pl.BlockSpec((1, 2 * d), lambda i: (0, 0))
    return pl.pallas_call(
        body, name="merge_bwd", grid=(t // tr,),
        in_specs=[row, pl.BlockSpec((tr, d), lambda i: (i, 1)), vec, row, row, row],
        out_specs=[row, row, wide, vec],
        out_shape=[jax.ShapeDtypeStruct((t, d), BF16), jax.ShapeDtypeStruct((t, d), BF16),
                   jax.ShapeDtypeStruct((t, PROJ_W), BF16), jax.ShapeDtypeStruct((1, 2 * d), F32)],
        compiler_params=_params(("arbitrary",)),
    )(proj, proj, bg, ys, yl, dm)


def _ffn_in_swiglu(hn, wt):
    t, d = hn.shape
    f = FFN_HIDDEN
    bm, bn = _blk(t, 1024), _blk(f, 1408)
    nj = f // bn

    def body(a_ref, wg_ref, wu_ref, g_ref, u_ref, act_ref):
        a = a_ref[...]
        g = _bdot(a, wg_ref[...], NT)
        u = _bdot(a, wu_ref[...], NT)
        g_ref[...] = g.astype(BF16)
        u_ref[...] = u.astype(BF16)
        act_ref[...] = (g * _sigmoid(g) * u).astype(BF16)

    out = pl.BlockSpec((bm, bn), lambda i, j: (i, j))
    shape = jax.ShapeDtypeStruct((t, f), BF16)
    return pl.pallas_call(
        body, name="ffn_in_swiglu", grid=(t // bm, nj),
        in_specs=[pl.BlockSpec((bm, d), lambda i, j: (i, 0)), pl.BlockSpec((bn, d), lambda i, j: (j, 0)),
                  pl.BlockSpec((bn, d), lambda i, j: (nj + j, 0))],
        out_specs=[out, out, out], out_shape=[shape, shape, shape],
        compiler_params=_params(("parallel", "parallel")),
    )(hn, wt, wt)


def _swiglu_bwd(g_all, u_all, dact):
    t, f = g_all.shape
    tr = _rows(t)

    def body(g_ref, u_ref, da_ref, o_ref):
        g = g_ref[...].astype(F32)
        sg = _sigmoid(g)
        da = da_ref[...].astype(F32)
        o_ref[:, 0:f] = (da * u_ref[...].astype(F32) * (sg * (1.0 + g * (1.0 - sg)))).astype(BF16)
        o_ref[:, f:2 * f] = (da * g * sg).astype(BF16)

    row = pl.BlockSpec((tr, f), lambda i: (i, 0))
    return pl.pallas_call(
        body, name="swiglu_bwd", grid=(t // tr,),
        in_specs=[row, row, row],
        out_specs=pl.BlockSpec((tr, 2 * f), lambda i: (i, 0)),
        out_shape=jax.ShapeDtypeStruct((t, 2 * f), BF16),
        compiler_params=_params(("parallel",)),
    )(g_all, u_all, dact)


def _conv_pre(xv, wv, bv):
    pre = bv + wv[CONV_K - 1:CONV_K, :] * xv
    for k in range(CONV_K - 1):
        pre = pre + wv[k:k + 1, :] * _shift_down(xv, CONV_K - 1 - k)
    return pre


def _ssm_conv_fwd(proj, w, b):
    t = proj.shape[0]
    nb = SSM_CONV_DIM // LANES
    c0 = COL_XBC // LANES

    def body(x_ref, w_ref, b_ref, o_ref):
        pre = _conv_pre(x_ref[...], w_ref[...], b_ref[...])
        o_ref[...] = (pre * _sigmoid(pre)).astype(BF16)

    return pl.pallas_call(
        body, name="ssm_conv_fwd", grid=(nb,),
        in_specs=[pl.BlockSpec((t, LANES), lambda j: (0, c0 + j)), pl.BlockSpec((CONV_K, LANES), lambda j: (0, j)),
                  pl.BlockSpec((1, LANES), lambda j: (0, j))],
        out_specs=pl.BlockSpec((t, LANES), lambda j: (0, j)),
        out_shape=jax.ShapeDtypeStruct((t, SSM_CONV_DIM), BF16),
        compiler_params=_params(("parallel",)),
    )(proj, w, b)


def _ssm_conv_bwd(proj, w, b, dact, dproj):
    t = proj.shape[0]
    nb = SSM_CONV_DIM // LANES
    c0 = COL_XBC // LANES

    def body(x_ref, w_ref, b_ref, da_ref, dproj_in, dx_ref, dw_ref, db_ref):
        xv = x_ref[...]
        wv = w_ref[...]
        pre = _conv_pre(xv, wv, b_ref[...])
        sg = _sigmoid(pre)
        dpre = da_ref[...] * (sg * (1.0 + pre * (1.0 - sg)))
        dx = wv[CONV_K - 1:CONV_K, :] * dpre
        for k in range(CONV_K - 1):
            dx = dx + wv[k:k + 1, :] * _shift_up(dpre, CONV_K - 1 - k)
        dx_ref[...] = dx.astype(BF16)
        for k in range(CONV_K):
            dw_ref[k:k + 1, :] = jnp.sum(dpre * _shift_down(xv, CONV_K - 1 - k), axis=0, keepdims=True)
        db_ref[...] = jnp.sum(dpre, axis=0, keepdims=True)

    col = pl.BlockSpec((t, LANES), lambda j: (0, j))
    wsp = pl.BlockSpec((CONV_K, LANES), lambda j: (0, j))
    bsp = pl.BlockSpec((1, LANES), lambda j: (0, j))
    return pl.pallas_call(
        body, name="ssm_conv_bwd", grid=(nb,),
        in_specs=[pl.BlockSpec((t, LANES), lambda j: (0, c0 + j)), wsp, bsp, col, HBM],
        out_specs=[pl.BlockSpec((t, LANES), lambda j: (0, c0 + j)), wsp, bsp],
        out_shape=[jax.ShapeDtypeStruct(dproj.shape, dproj.dtype), jax.ShapeDtypeStruct((CONV_K, SSM_CONV_DIM), F32),
                   jax.ShapeDtypeStruct((1, SSM_CONV_DIM), F32)],
        input_output_aliases={4: 0},
        compiler_params=_params(("parallel",)),
    )(proj, w, b, dact, dproj)


def _ssd_chunk_terms(dtr, bias, alog):
    a = -jnp.exp(alog)
    dt = _softplus(dtr + bias)
    row = _iota((SSM_CHUNK, SSM_CHUNK), 0)
    col = _iota((SSM_CHUNK, SSM_CHUNK), 1)
    tri = (row >= col).astype(F32)
    cs = _fdot(tri, dt * a)
    dec = jnp.exp(cs[SSM_CHUNK - 1:SSM_CHUNK, :] - cs)
    ecs = jnp.exp(cs)
    off = _iota((LANES, SSM_INNER), 1) - SSM_HEAD_DIM * _iota((LANES, SSM_INNER), 0)
    expand = jnp.where(jnp.logical_and(off >= 0, off < SSM_HEAD_DIM), 1.0, 0.0).astype(BF16)
    return a, dt, cs, dec, ecs, expand, row, col


def _ssd_specs(t):
    nc = t // SSM_CHUNK
    xs = pl.BlockSpec((SSM_CHUNK, SSM_INNER), lambda c: (c, 0))
    bm = pl.BlockSpec((SSM_CHUNK, SSM_BC), lambda c: (c, SSM_INNER // SSM_BC))
    cm = pl.BlockSpec((SSM_CHUNK, SSM_BC), lambda c: (c, SSM_INNER // SSM_BC + 1))
    dtr = pl.BlockSpec((SSM_CHUNK, LANES), lambda c: (c, COL_DT // LANES))
    vec = pl.BlockSpec((1, LANES), lambda c: (0, 0))
    wide = pl.BlockSpec((1, SSM_INNER), lambda c: (0, 0))
    return nc, xs, bm, cm, dtr, vec, wide


def _ssd_fwd(xbc_act, proj, bias, alog, dexp, gather):
    t = proj.shape[0]
    nc, xs_s, bm_s, cm_s, dtr_s, vec, wide = _ssd_specs(t)
    n = len(gather)

    def body(*refs):
        xs_ref, b_ref, c_ref, dtr_ref, bias_ref, alog_ref, dexp_ref = refs[:7]
        y_ref, sin_ref = refs[7 + n:9 + n]
        state = refs[9 + 2 * n]
        comm = (refs[7:7 + n], refs[9 + n:9 + 2 * n]) + tuple(refs[10 + 2 * n:])
        chunk = pl.program_id(0)

        @pl.when(chunk == 0)
        def _():
            _gather_phase("start", *comm)
            state[...] = jnp.zeros_like(state)

        @pl.when(chunk == (3 * nc) // 4)
        def _():
            _gather_phase("forward", *comm)

        a, dt, cs, dec, ecs, expand, row, col = _ssd_chunk_terms(dtr_ref[...], bias_ref[...], alog_ref[...])
        cst = cs.T
        dt_x = _split_dot(dt, expand)
        dec_x = _split_dot(dec, expand)
        ecs_x = _split_dot(ecs, expand)
        xs = xs_ref[...].astype(F32)
        xdt = xs * dt_x
        xdec = xdt * dec_x
        lane_lo = col < SSM_HEAD_DIM
        causal = row >= col
        sin_ref[0] = state[...]
        for g in range(SSM_GROUPS):
            bg = b_ref[:, g * SSM_STATE:(g + 1) * SSM_STATE].astype(BF16)
            cg = c_ref[:, g * SSM_STATE:(g + 1) * SSM_STATE].astype(BF16)
            cb = _bdot(cg, bg, NT)
            for q in range(SSM_PAIRS // SSM_GROUPS):
                pq = g * (SSM_PAIRS // SSM_GROUPS) + q
                sl = slice(pq * LANES, (pq + 1) * LANES)
                xp = xdt[:, sl].astype(BF16)
                yd = []
                for hh in range(2):
                    h = 2 * pq + hh
                    lmat = jnp.exp(jnp.where(causal, cs[:, h:h + 1] - cst[h:h + 1, :], -jnp.inf))
                    yd.append(_bdot(cb * lmat, xp))
                s_in = state[pq]
                y_off = _bdot(cg, s_in) * ecs_x[:, sl]
                y_ref[:, sl] = (jnp.where(lane_lo, yd[0], yd[1]) + y_off + xs[:, sl] * dexp_ref[:, sl]).astype(BF16)
                state[pq] = s_in * ecs_x[SSM_CHUNK - 1:SSM_CHUNK, sl] + _bdot(bg, xdec[:, sl], TN)

        @pl.when(chunk == nc - 1)
        def _():
            _gather_phase("finish", *comm)

    out = pl.pallas_call(
        body, name="ssd_fwd", grid=(nc,),
        in_specs=[xs_s, bm_s, cm_s, dtr_s, vec, vec, wide] + [HBM] * n,
        out_specs=[pl.BlockSpec((SSM_CHUNK, SSM_INNER), lambda c: (c, 0)),
                   pl.BlockSpec((1, SSM_PAIRS, SSM_STATE, LANES), lambda c: (c, 0, 0, 0))] + [HBM] * n,
        out_shape=[jax.ShapeDtypeStruct((t, SSM_INNER), BF16),
                   jax.ShapeDtypeStruct((nc, SSM_PAIRS, SSM_STATE, LANES), F32)]
        + [jax.ShapeDtypeStruct((N_DEV,) + v.shape, v.dtype) for v in gather],
        scratch_shapes=[pltpu.VMEM((SSM_PAIRS, SSM_STATE, LANES), F32)] + _comm_scratch(n),
        compiler_params=_params(("arbitrary",)),
    )(xbc_act, xbc_act, xbc_act, proj, bias, alog, dexp, *gather)
    return out[:2], out[2:]


def _ssd_bwd(xbc_act, proj, s_in_all, dy, bias, alog, dexp, dproj, exchange):
    n_ex = len(exchange)
    t = proj.shape[0]
    nc = t // SSM_CHUNK
    last = nc - 1
    xs_s = pl.BlockSpec((SSM_CHUNK, SSM_INNER), lambda c: (last - c, 0))
    bm_s = pl.BlockSpec((SSM_CHUNK, SSM_BC), lambda c: (last - c, SSM_INNER // SSM_BC))
    cm_s = pl.BlockSpec((SSM_CHUNK, SSM_BC), lambda c: (last - c, SSM_INNER // SSM_BC + 1))
    dtr_s = pl.BlockSpec((SSM_CHUNK, LANES), lambda c: (last - c, COL_DT // LANES))
    sin_s = pl.BlockSpec((1, SSM_PAIRS, SSM_STATE, LANES), lambda c: (last - c, 0, 0, 0))
    vec = pl.BlockSpec((1, LANES), lambda c: (0, 0))
    wide = pl.BlockSpec((1, SSM_INNER), lambda c: (0, 0))

    def body(*refs):
        xs_ref, b_ref, c_ref, dtr_ref, sin_ref, dy_ref, bias_ref, alog_ref, dexp_ref = refs[:9]
        dxbc_ref, ddtr_ref, dbias_ref, dalog_ref, ddcol_ref = refs[10 + n_ex:15 + n_ex]
        dstate, dxdt_s, yoff_s, rx_s, trow_s = refs[15 + 2 * n_ex:20 + 2 * n_ex]
        comm = (refs[10:10 + n_ex], refs[15 + n_ex:15 + 2 * n_ex]) + tuple(refs[20 + 2 * n_ex:])

        @pl.when(pl.program_id(0) == 0)
        def _():
            _exchange_phase("start", *comm)
            dstate[...] = jnp.zeros_like(dstate)
            trow_s[...] = jnp.zeros_like(trow_s)
            dbias_ref[...] = jnp.zeros_like(dbias_ref)
            dalog_ref[...] = jnp.zeros_like(dalog_ref)
            ddcol_ref[...] = jnp.zeros_like(ddcol_ref)

        dtr = dtr_ref[...]
        a, dt, cs, dec, ecs, expand, row, col = _ssd_chunk_terms(dtr, bias_ref[...], alog_ref[...])
        cst = cs.T
        dt_x = _split_dot(dt, expand)
        dec_x = _split_dot(dec, expand)
        ecs_x = _split_dot(ecs, expand)
        xs = xs_ref[...].astype(F32)
        dyv = dy_ref[...].astype(F32)
        xdt = xs * dt_x
        lane_lo = col < SSM_HEAD_DIM
        causal = row >= col
        ddcol_ref[...] += jnp.sum(dyv * xs, axis=0, keepdims=True)
        dcs_col = jnp.zeros((SSM_CHUNK, LANES), F32)
        dcs_row = jnp.zeros((LANES, SSM_CHUNK), F32)
        for g in range(SSM_GROUPS):
            bg = b_ref[:, g * SSM_STATE:(g + 1) * SSM_STATE].astype(BF16)
            cg = c_ref[:, g * SSM_STATE:(g + 1) * SSM_STATE].astype(BF16)
            cb = _bdot(cg, bg, NT)
            dgm = jnp.zeros((SSM_CHUNK, SSM_CHUNK), F32)
            dbg = jnp.zeros((SSM_CHUNK, SSM_STATE), F32)
            dcg = jnp.zeros((SSM_CHUNK, SSM_STATE), F32)
            for q in range(SSM_PAIRS // SSM_GROUPS):
                pq = g * (SSM_PAIRS // SSM_GROUPS) + q
                sl = slice(pq * LANES, (pq + 1) * LANES)
                dyp = dyv[:, sl]
                xp = xdt[:, sl]
                dxh = []
                for hh in range(2):
                    h = 2 * pq + hh
                    lmat = jnp.exp(jnp.where(causal, cs[:, h:h + 1] - cst[h:h + 1, :], -jnp.inf))
                    mmat = cb * lmat
                    dyh = jnp.where(lane_lo if hh == 0 else jnp.logical_not(lane_lo), dyp, 0.0)
                    dmm = _bdot(dyh, xp, NT)
                    pm = dmm * mmat
                    dcs_col = jnp.where(col == h, jnp.sum(pm, axis=1, keepdims=True), dcs_col)
                    dcs_row = jnp.where(row == h, jnp.sum(pm, axis=0, keepdims=True), dcs_row)
                    dgm = dgm + dmm * lmat
                    dxh.append(_bdot(mmat, dyp, TN))
                s_in = sin_ref[0, pq]
                ecs_p = ecs_x[:, sl]
                dec_p = dec_x[:, sl]
                etot_p = ecs_x[SSM_CHUNK - 1:SSM_CHUNK, sl]
                yoff_s[:, sl] = dyp * (_bdot(cg, s_in) * ecs_p)
                dq = dyp * ecs_p
                dcg = dcg + _bdot(dq, s_in, NT)
                ds = dstate[pq]
                r = _bdot(bg, ds)
                rx_s[:, sl] = r * xp
                dxdt_s[:, sl] = jnp.where(lane_lo, dxh[0], dxh[1]) + dec_p * r
                dbg = dbg + _bdot(xp * dec_p, ds, NT)
                trow_s[0:1, sl] = jnp.sum(ds * s_in, axis=0, keepdims=True) * etot_p
                dstate[pq] = etot_p * ds + _bdot(cg, dq, TN)
            dcg = dcg + _bdot(dgm, bg)
            dbg = dbg + _bdot(dgm, cg, TN)
            dxbc_ref[:, SSM_INNER + g * SSM_STATE:SSM_INNER + (g + 1) * SSM_STATE] = dbg.astype(BF16)
            dxbc_ref[:, SSM_INNER + SSM_BC + g * SSM_STATE:SSM_INNER + SSM_BC + (g + 1) * SSM_STATE] = dcg.astype(BF16)
        ddec = _split_dot(rx_s[...], expand, NT) * dec
        dtot = _split_dot(trow_s[...], expand, NT)[0:1, :]
        dcs = dcs_col - dcs_row.T + _split_dot(yoff_s[...], expand, NT) - ddec
        dcs = dcs + jnp.where(row == SSM_CHUNK - 1, jnp.sum(ddec, axis=0, keepdims=True) + dtot, 0.0)
        da = _fdot((row <= col).astype(F32), dcs)
        dxdt = dxdt_s[...]
        ddt = da * a + _split_dot(dxdt * xs, expand, NT)
        dalog_ref[...] += jnp.sum(da * dt, axis=0, keepdims=True) * a
        ddtr = ddt * _sigmoid(dtr + bias_ref[...])
        ddtr_ref[...] = ddtr.astype(BF16)
        dbias_ref[...] += jnp.sum(ddtr, axis=0, keepdims=True)
        dxbc_ref[:, 0:SSM_INNER] = (dxdt * dt_x + dyv * dexp_ref[...]).astype(BF16)

        @pl.when(pl.program_id(0) == last)
        def _():
            _exchange_phase("finish", *comm)

    out = pl.pallas_call(
        body, name="ssd_bwd", grid=(nc,),
        in_specs=[xs_s, bm_s, cm_s, dtr_s, sin_s, pl.BlockSpec((SSM_CHUNK, SSM_INNER), lambda c: (last - c, 0)),
                  vec, vec, wide, HBM] + [HBM] * n_ex,
        out_specs=[pl.BlockSpec((SSM_CHUNK, SSM_CONV_DIM), lambda c: (last - c, 0)), dtr_s, vec, vec, wide]
        + [HBM] * n_ex,
        out_shape=[jax.ShapeDtypeStruct((t, SSM_CONV_DIM), BF16), jax.ShapeDtypeStruct(dproj.shape, dproj.dtype),
                   jax.ShapeDtypeStruct((1, LANES), F32), jax.ShapeDtypeStruct((1, LANES), F32),
                   jax.ShapeDtypeStruct((1, SSM_INNER), F32)]
        + [jax.ShapeDtypeStruct(p.shape, p.dtype) for p in exchange],
        input_output_aliases={9: 1},
        scratch_shapes=[pltpu.VMEM((SSM_PAIRS, SSM_STATE, LANES), F32),
                        pltpu.VMEM((SSM_CHUNK, SSM_INNER), F32), pltpu.VMEM((SSM_CHUNK, SSM_INNER), F32),
                        pltpu.VMEM((SSM_CHUNK, SSM_INNER), F32), pltpu.VMEM((SUBLANES, SSM_INNER), F32)]
        + _comm_scratch(n_ex),
        compiler_params=_params(("arbitrary",)),
    )(xbc_act, xbc_act, xbc_act, proj, s_in_all, dy, bias, alog, dexp, dproj, *exchange)
    return out[:5], out[5:]


def _group_rstd(y):
    n = SSM_INNER // SSM_GROUPS
    parts = []
    for g in range(SSM_GROUPS):
        yg = y[:, g * n:(g + 1) * n]
        r = lax.rsqrt(jnp.mean(yg * yg, axis=-1, keepdims=True) + RMS_EPS)
        parts.append(jnp.broadcast_to(r, yg.shape))
    return jnp.concatenate(parts, axis=1)


def _group_mean(v):
    n = SSM_INNER // SSM_GROUPS
    parts = []
    for g in range(SSM_GROUPS):
        vg = v[:, g * n:(g + 1) * n]
        parts.append(jnp.broadcast_to(jnp.mean(vg, axis=-1, keepdims=True), vg.shape))
    return jnp.concatenate(parts, axis=1)


def _ssm_post_fwd(y_ssd, proj, nw):
    t = proj.shape[0]
    n = SSM_INNER
    tr = _rows(t)

    def body(y_ref, z_ref, nw_ref, o_ref):
        z = z_ref[...]
        y = y_ref[...] * (z * _sigmoid(z))
        o_ref[...] = (y * _group_rstd(y) * nw_ref[...]).astype(BF16)

    row = pl.BlockSpec((tr, n), lambda i: (i, 0))
    return pl.pallas_call(
        body, name="ssm_post_fwd", grid=(t // tr,),
        in_specs=[row, pl.BlockSpec((tr, n), lambda i: (i, COL_Z // n)), pl.BlockSpec((1, n), lambda i: (0, 0))],
        out_specs=row, out_shape=jax.ShapeDtypeStruct((t, n), BF16),
        compiler_params=_params(("parallel",)),
    )(y_ssd, proj, nw)


def _ssm_post_bwd(y_ssd, proj, nw, dout, dproj):
    t = proj.shape[0]
    n = SSM_INNER
    tr = _rows(t)

    def body(y_ref, z_ref, nw_ref, do_ref, dproj_in, dy_ref, dz_ref, dnw_ref):
        i = pl.program_id(0)
        z = z_ref[...]
        sg = _sigmoid(z)
        sz = z * sg
        ys = y_ref[...]
        y = ys * sz
        rstd = _group_rstd(y)
        yn = y * rstd
        dov = do_ref[...]
        dyn = dov * nw_ref[...]
        dyg = rstd * (dyn - yn * _group_mean(dyn * yn))
        dy_ref[...] = (dyg * sz).astype(BF16)
        dz_ref[...] = (dyg * ys * (sg * (1.0 + z * (1.0 - sg)))).astype(BF16)
        part = jnp.sum(dov * yn, axis=0, keepdims=True)

        @pl.when(i == 0)
        def _():
            dnw_ref[...] = part

        @pl.when(i > 0)
        def _():
            dnw_ref[...] += part

    row = pl.BlockSpec((tr, n), lambda i: (i, 0))
    vec = pl.BlockSpec((1, n), lambda i: (0, 0))
    return pl.pallas_call(
        body, name="ssm_post_bwd", grid=(t // tr,),
        in_specs=[row, pl.BlockSpec((tr, n), lambda i: (i, COL_Z // n)), vec, row, HBM],
        out_specs=[row, pl.BlockSpec((tr, n), lambda i: (i, COL_Z // n)), vec],
        out_shape=[jax.ShapeDtypeStruct((t, n), BF16), jax.ShapeDtypeStruct(dproj.shape, dproj.dtype),
                   jax.ShapeDtypeStruct((1, n), F32)],
        input_output_aliases={4: 1},
        compiler_params=_params(("arbitrary",)),
    )(y_ssd, proj, nw, dout, dproj)


SCAN_UNROLL = 8
GELU_C = math.sqrt(2.0 / math.pi)
GELU_K = 0.044715


def _gelu_parts(y):
    th = jnp.tanh(GELU_C * (y + GELU_K * y * y * y))
    val = 0.5 * y * (1.0 + th)
    grad = 0.5 * (1.0 + th) + 0.5 * y * (1.0 - th * th) * GELU_C * (1.0 + 3.0 * GELU_K * y * y)
    return val, grad


def _scan_tiles(a_ref, b_ref, h_ref, n_rows, reverse):
    n_tiles = n_rows // SUBLANES
    shape = (SUBLANES, a_ref.shape[1])
    row = _iota(shape, 0)

    def in_tile(av, bv):
        for s in (1, 2, 4):
            if reverse:
                keep = row < SUBLANES - s
                a_sh = jnp.where(keep, pltpu.roll(av, SUBLANES - s, 0), 1.0)
                b_sh = jnp.where(keep, pltpu.roll(bv, SUBLANES - s, 0), 0.0)
            else:
                keep = row >= s
                a_sh = jnp.where(keep, pltpu.roll(av, s, 0), 1.0)
                b_sh = jnp.where(keep, pltpu.roll(bv, s, 0), 0.0)
            bv = av * b_sh + bv
            av = av * a_sh
        return av, bv

    def step(k, carry):
        first = (n_tiles // SCAN_UNROLL - 1 - k) if reverse else k
        tiles = [first * SCAN_UNROLL + j for j in range(SCAN_UNROLL)]
        if reverse:
            tiles = tiles[::-1]
        ats = [pl.ds(pl.multiple_of(tile * SUBLANES, SUBLANES), SUBLANES) for tile in tiles]
        scanned = [in_tile(a_ref[at, :], b_ref[at, :]) for at in ats]
        for at, (av, bv) in zip(ats, scanned):
            hv = bv + av * carry
            h_ref[at, :] = hv
            carry = hv[0:1, :] if reverse else hv[SUBLANES - 1:SUBLANES, :]
        return carry

    assert n_tiles % SCAN_UNROLL == 0, n_rows
    lax.fori_loop(0, n_tiles // SCAN_UNROLL, step, jnp.zeros((1, a_ref.shape[1]), F32))


def _lru_gates(xl, cw, cb, wr, br, wi, bi, lam):
    u = cb + cw[CONV_K - 1:CONV_K, :] * xl
    for k in range(CONV_K - 1):
        u = u + cw[k:k + 1, :] * _shift_down(xl, CONV_K - 1 - k)
    r = _sigmoid(_bdot(u, wr) + br)
    i = _sigmoid(_bdot(u, wi) + bi)
    sp = _softplus(-lam)
    la = -LRU_C * r * sp
    a = jnp.exp(la)
    mult = jnp.sqrt(-jnp.tanh(la) * (a * a + 1.0))
    return u, r, i, sp, a, mult


def _lru_specs(t):
    c0 = COL_LRU // LANES
    xl = pl.BlockSpec((t, LANES), lambda j: (0, c0 + 2 * j))
    yl = pl.BlockSpec((t, LANES), lambda j: (0, c0 + 2 * j + 1))
    col = pl.BlockSpec((t, LANES), lambda j: (0, j))
    cw = pl.BlockSpec((CONV_K, LANES), lambda j: (0, j))
    vec = pl.BlockSpec((1, LANES), lambda j: (0, j))
    wblk = pl.BlockSpec((1, LANES, LANES), lambda j: (j, 0, 0))
    return xl, yl, col, cw, vec, wblk


def _lru_fwd(proj, cw, cb, wr, br, wi, bi, lam, gather):
    t = proj.shape[0]
    xl_s, yl_s, col, cw_s, vec, wblk = _lru_specs(t)
    n = len(gather)

    def body(*refs):
        xl_ref, yl_ref, cw_ref, cb_ref, wr_ref, br_ref, wi_ref, bi_ref, lam_ref = refs[:9]
        o_ref, h_ref = refs[9 + n:11 + n]
        a_s, b_s = refs[11 + 2 * n:13 + 2 * n]
        comm = (refs[9:9 + n], refs[11 + n:11 + 2 * n]) + tuple(refs[13 + 2 * n:])
        j = pl.program_id(0)
        for step, phase in ((0, "start"), (LRU_BLOCKS - 2, "forward")):
            @pl.when(j == step)
            def _():
                _gather_phase(phase, *comm)

        u, r, i, sp, a, mult = _lru_gates(xl_ref[...], cw_ref[...], cb_ref[...], wr_ref[0], br_ref[...],
                                          wi_ref[0], bi_ref[...], lam_ref[...])
        a_s[...] = a
        b_s[...] = mult * (i * u)
        _scan_tiles(a_s, b_s, h_ref, t, reverse=False)
        o_ref[...] = (h_ref[...] * _gelu_parts(yl_ref[...])[0]).astype(BF16)

        @pl.when(j == LRU_BLOCKS - 1)
        def _():
            _gather_phase("finish", *comm)

    out = pl.pallas_call(
        body, name="lru_fwd", grid=(LRU_BLOCKS,),
        in_specs=[xl_s, yl_s, cw_s, vec, wblk, vec, wblk, vec, vec] + [HBM] * n,
        out_specs=[col, col] + [HBM] * n,
        out_shape=[jax.ShapeDtypeStruct((t, LRU_WIDTH), BF16), jax.ShapeDtypeStruct((t, LRU_WIDTH), F32)]
        + [jax.ShapeDtypeStruct((N_DEV,) + v.shape, v.dtype) for v in gather],
        scratch_shapes=[pltpu.VMEM((t, LANES), F32)] * 2 + _comm_scratch(n),
        compiler_params=_params(("arbitrary",), big=True),
    )(proj, proj, cw, cb, wr, br, wi, bi, lam, *gather)
    return out[:2], out[2:]


def _lru_bwd(proj, cw, cb, wr, br, wi, bi, lam, h_all, dout, dproj, exchange):
    t = proj.shape[0]
    xl_s, yl_s, col, cw_s, vec, wblk = _lru_specs(t)
    pair = pl.BlockSpec((t, 2 * LANES), lambda j: (0, COL_LRU // (2 * LANES) + j))
    n_ex = len(exchange)

    def body(*refs):
        xl_ref, yl_ref, cw_ref, cb_ref, wr_ref, br_ref, wi_ref, bi_ref, lam_ref, h_ref, do_ref = refs[:11]
        dxy_ref, dcw_ref, dcb_ref, dwr_ref, dbr_ref, dwi_ref, dbi_ref, dlam_ref = refs[12 + n_ex:20 + n_ex]
        a_s, b_s, g_s = refs[20 + 2 * n_ex:23 + 2 * n_ex]
        comm = (refs[12:12 + n_ex], refs[20 + n_ex:20 + 2 * n_ex]) + tuple(refs[23 + 2 * n_ex:])

        @pl.when(pl.program_id(0) == 0)
        def _():
            _exchange_phase("start", *comm)

        xl = xl_ref[...]
        cwv = cw_ref[...]
        lam = lam_ref[...]
        u, r, i, sp, a, mult = _lru_gates(xl, cwv, cb_ref[...], wr_ref[0], br_ref[...], wi_ref[0], bi_ref[...], lam)
        v = i * u
        gl, dgl = _gelu_parts(yl_ref[...])
        dov = do_ref[...]
        h = h_ref[...]
        dxy_ref[:, LANES:2 * LANES] = (dov * h * dgl).astype(BF16)
        b_s[...] = dov * gl
        a_s[...] = _shift_up(a, 1)
        _scan_tiles(a_s, b_s, g_s, t, reverse=True)
        g = g_s[...]
        da = g * _shift_down(h, 1)
        dmult = g * v
        dv = g * mult
        dla = da * a - dmult * (a * a) / mult
        dr = dla * (-LRU_C * sp)
        dsp = jnp.sum(dla * (-LRU_C * r), axis=0, keepdims=True)
        dlam_ref[...] = -dsp * _sigmoid(-lam)
        dpr = dr * r * (1.0 - r)
        dpi = dv * u * i * (1.0 - i)
        dbr_ref[...] = jnp.sum(dpr, axis=0, keepdims=True)
        dbi_ref[...] = jnp.sum(dpi, axis=0, keepdims=True)
        dwr_ref[0] = _bdot(u, dpr, TN)
        dwi_ref[0] = _bdot(u, dpi, TN)
        du = dv * i + _bdot(dpr, wr_ref[0], NT) + _bdot(dpi, wi_ref[0], NT)
        dxl = cwv[CONV_K - 1:CONV_K, :] * du
        for k in range(CONV_K - 1):
            dxl = dxl + cwv[k:k + 1, :] * _shift_up(du, CONV_K - 1 - k)
        dxy_ref[:, 0:LANES] = dxl.astype(BF16)
        for k in range(CONV_K):
            dcw_ref[k:k + 1, :] = jnp.sum(du * _shift_down(xl, CONV_K - 1 - k), axis=0, keepdims=True)
        dcb_ref[...] = jnp.sum(du, axis=0, keepdims=True)

        @pl.when(pl.program_id(0) == LRU_BLOCKS - 1)
        def _():
            _exchange_phase("finish", *comm)

    out = pl.pallas_call(
        body, name="lru_bwd", grid=(LRU_BLOCKS,),
        in_specs=[xl_s, yl_s, cw_s, vec, wblk, vec, wblk, vec, vec, col, col, HBM] + [HBM] * n_ex,
        out_specs=[pair, cw_s, vec, wblk, vec, wblk, vec, vec] + [HBM] * n_ex,
        input_output_aliases={11: 0},
        out_shape=[jax.ShapeDtypeStruct(dproj.shape, dproj.dtype),
                   jax.ShapeDtypeStruct((CONV_K, LRU_WIDTH), F32), jax.ShapeDtypeStruct((1, LRU_WIDTH), F32),
                   jax.ShapeDtypeStruct((LRU_BLOCKS, LANES, LANES), F32), jax.ShapeDtypeStruct((1, LRU_WIDTH), F32),
                   jax.ShapeDtypeStruct((LRU_BLOCKS, LANES, LANES), F32), jax.ShapeDtypeStruct((1, LRU_WIDTH), F32),
                   jax.ShapeDtypeStruct((1, LRU_WIDTH), F32)]
        + [jax.ShapeDtypeStruct(p.shape, p.dtype) for p in exchange],
        scratch_shapes=[pltpu.VMEM((t, LANES), F32)] * 3 + _comm_scratch(n_ex),
        compiler_params=_params(("arbitrary",), big=True),
    )(proj, proj, cw, cb, wr, br, wi, bi, lam, h_all, dout, dproj, *exchange)
    return out[:8], out[8:]


def _mesh_pos():
    return lax.axis_index("x"), lax.axis_index("y"), lax.axis_index("c")


HBM = pl.BlockSpec(memory_space=pl.ANY)


def _comm_scratch(n):
    return [pltpu.SemaphoreType.DMA((n, 7)), pltpu.SemaphoreType.DMA((n, 7)), pltpu.SemaphoreType.DMA((n,))]


def _gather_phase(phase, v_refs, out_refs, send_sems, recv_sems, local_sems):
    n = len(v_refs)
    x, y, c = _mesh_pos()
    me, sibling = (x, y, c), (x, y, 1 - c)
    chips = [(1 - x, y), (x, 1 - y), (1 - x, 1 - y)]

    def block(a, px, py, pc):
        return out_refs[a].at[4 * px + 2 * py + pc]

    def copy(a, k, blk, to, src=None):
        return pltpu.make_async_remote_copy(
            src_ref=block(a, *blk) if src is None else src, dst_ref=block(a, *blk),
            send_sem=send_sems.at[a, k], recv_sem=recv_sems.at[a, k], device_id=to, device_id_type=MESH)

    def own(a):
        return pltpu.make_async_copy(v_refs[a], block(a, *me), local_sems.at[a])

    def first(a):
        return ([copy(a, 0, me, sibling, src=v_refs[a])]
                + [copy(a, 1 + j, me, (*chip, c), src=v_refs[a]) for j, chip in enumerate(chips)])

    def forward(a, j):
        return copy(a, 4 + j, (*chips[j], c), sibling)

    if phase == "start":
        for a in range(n):
            own(a).start()
        for a in range(n):
            for cp in first(a):
                cp.start()
    elif phase == "forward":
        for j in range(3):
            for a in range(n):
                copy(a, 1 + j, (*chips[j], c), me).wait_recv()
                forward(a, j).start()
    else:
        for a in range(n):
            copy(a, 0, sibling, me).wait_recv()
            for j in range(3):
                copy(a, 4 + j, (*chips[j], 1 - c), me).wait_recv()
        for a in range(n):
            for cp in first(a) + [forward(a, j) for j in range(3)]:
                cp.wait_send()
            own(a).wait()


def _gather_two_phase(phase, *comm):
    if phase == "start":
        _gather_phase("start", *comm)
    else:
        _gather_phase("forward", *comm)
        _gather_phase("finish", *comm)


def _all_gather(vs, name):
    n = len(vs)

    def body(*refs):
        comm = (refs[:n], refs[n:2 * n]) + tuple(refs[2 * n:])
        for phase in ("start", "forward", "finish"):
            _gather_phase(phase, *comm)

    return pl.pallas_call(
        body, name=name,
        out_shape=[jax.ShapeDtypeStruct((N_DEV,) + v.shape, v.dtype) for v in vs],
        in_specs=[HBM] * n, out_specs=[HBM] * n, scratch_shapes=_comm_scratch(n),
    )(*vs)


def _run_copies(phase, local, remote):
    if phase == "start":
        for cp in local + remote:
            cp.start()
    else:
        for cp in remote:
            cp.wait()
        for cp in local:
            cp.wait()


def _exchange_phase(phase, p_refs, out_refs, send_sems, recv_sems, local_sems):
    n = len(p_refs)
    x, y, c = _mesh_pos()
    me = 4 * x + 2 * y + c
    local = [pltpu.make_async_copy(p_refs[a].at[me], out_refs[a].at[me], local_sems.at[a]) for a in range(n)]
    remote = []
    for k in range(1, N_DEV):
        px = (1 - x) if k & 4 else x
        py = (1 - y) if k & 2 else y
        pc = (1 - c) if k & 1 else c
        for a in range(n):
            remote.append(pltpu.make_async_remote_copy(
                src_ref=p_refs[a].at[4 * px + 2 * py + pc], dst_ref=out_refs[a].at[me],
                send_sem=send_sems.at[a, k - 1], recv_sem=recv_sems.at[a, k - 1],
                device_id=(px, py, pc), device_id_type=MESH))
    _run_copies(phase, local, remote)


def _chip_exchange_phase(phase, p_refs, out_refs, send_sems, recv_sems, local_sems):
    n = len(p_refs)
    x, y, c = _mesh_pos()
    me = 2 * x + y
    local = [pltpu.make_async_copy(p_refs[a].at[me], out_refs[a].at[me], local_sems.at[a]) for a in range(n)]
    remote = []
    for k in range(1, 4):
        px = (1 - x) if k & 2 else x
        py = (1 - y) if k & 1 else y
        for a in range(n):
            remote.append(pltpu.make_async_remote_copy(
                src_ref=p_refs[a].at[2 * px + py], dst_ref=out_refs[a].at[me],
                send_sem=send_sems.at[a, k - 1], recv_sem=recv_sems.at[a, k - 1],
                device_id=(px, py, c), device_id_type=MESH))
    _run_copies(phase, local, remote)


def _sibling_exchange(parts, name):
    chips = N_DEV // 2

    def body(p_ref, out_ref, send_sems, recv_sems):
        x, y, c = _mesh_pos()
        copies = [pltpu.make_async_remote_copy(
            src_ref=p_ref.at[2 * q + 1 - c], dst_ref=out_ref.at[q], send_sem=send_sems.at[q], recv_sem=recv_sems.at[q],
            device_id=(x, y, 1 - c), device_id_type=MESH) for q in range(chips)]
        _run_copies("start", [], copies)
        _run_copies("finish", [], copies)

    return pl.pallas_call(
        body, name=name, out_shape=jax.ShapeDtypeStruct((chips,) + parts.shape[1:], parts.dtype),
        in_specs=[HBM], out_specs=HBM,
        scratch_shapes=[pltpu.SemaphoreType.DMA((chips,)), pltpu.SemaphoreType.DMA((chips,))],
    )(parts)


def _pair_sum(parts, theirs, name):
    slots, rows, cols = theirs.shape
    tc = 256
    core = lax.axis_index("c").astype(jnp.int32).reshape(1)

    def body(c_ref, a_ref, b_ref, o_ref):
        o_ref[...] = (a_ref[...].astype(F32) + b_ref[...].astype(F32)).astype(o_ref.dtype)

    spec = pl.BlockSpec((1, rows, tc), lambda q, j, c: (q, 0, j))
    return pl.pallas_call(
        body, name=name,
        grid_spec=pltpu.PrefetchScalarGridSpec(
            num_scalar_prefetch=1, grid=(slots, cols // tc),
            in_specs=[pl.BlockSpec((1, rows, tc), lambda q, j, c: (2 * q + c[0], 0, j)), spec], out_specs=spec),
        out_shape=jax.ShapeDtypeStruct(theirs.shape, theirs.dtype),
        compiler_params=_params(("parallel", "parallel")),
    )(core, parts, theirs)


def _sum_sources(recvs, name):
    k = len(recvs)

    def body(*refs):
        for r_ref, o_ref in zip(refs[:k], refs[k:]):
            acc = r_ref[0].astype(F32)
            for s in range(1, r_ref.shape[0]):
                acc = acc + r_ref[s].astype(F32)
            o_ref[...] = acc

    return pl.pallas_call(
        body, name=name, out_shape=[jax.ShapeDtypeStruct(r.shape[1:], F32) for r in recvs],
        compiler_params=_params(),
    )(*recvs)


def _row_tile(rows):
    for tile in range(128, 15, -16):
        if rows % tile == 0:
            return tile
    return rows


def _adam_update(w, g, m, v):
    nm = ADAM_B1 * m + (1.0 - ADAM_B1) * g
    nv = ADAM_B2 * v + (1.0 - ADAM_B2) * (g * g)
    m_hat = nm / (1.0 - ADAM_B1 ** ADAM_STEP)
    v_hat = nv / (1.0 - ADAM_B2 ** ADAM_STEP)
    return -ADAM_LR * (m_hat / (jnp.sqrt(v_hat) + ADAM_EPS) + ADAM_WD * w), nm, nv


def _vector_offsets(widths):
    offsets, end = [], 0
    for c in widths:
        offsets.append(end)
        end += c + (-c) % LANES
    return offsets, end


def _adamw_small(vec_parts, vec_state, mat_grads, mat_state):
    widths = [w.shape[1] for w, _, _ in vec_state]
    offsets, total = _vector_offsets(widths)
    assert vec_parts.shape == (N_DEV, total), (vec_parts.shape, total)
    n_vec, n_mat = len(vec_state), len(mat_state)

    def body(*refs):
        r_ref = refs[0]
        vec_in = refs[1:1 + 3 * n_vec]
        mat_in = refs[1 + 3 * n_vec:1 + 3 * n_vec + 4 * n_mat]
        outs = refs[1 + 3 * n_vec + 4 * n_mat:]
        for i, (off, c) in enumerate(zip(offsets, widths)):
            g = r_ref[0:1, off:off + c]
            for s in range(1, N_DEV):
                g = g + r_ref[s:s + 1, off:off + c]
            w_ref, m_ref, v_ref = vec_in[3 * i:3 * i + 3]
            g_out, d_out, m_out, v_out = outs[4 * i:4 * i + 4]
            g_out[...] = g
            d_out[...], m_out[...], v_out[...] = _adam_update(w_ref[...], g, m_ref[...], v_ref[...])
        for j in range(n_mat):
            g_ref, w_ref, m_ref, v_ref = mat_in[4 * j:4 * j + 4]
            d_out, m_out, v_out = outs[4 * n_vec + 3 * j:4 * n_vec + 3 * j + 3]
            d_out[...], m_out[...], v_out[...] = _adam_update(w_ref[...], g_ref[...], m_ref[...], v_ref[...])

    args = [vec_parts] + [a for state in vec_state for a in state]
    for g, state in zip(mat_grads, mat_state):
        args += [g, *state]
    out_shape = [jax.ShapeDtypeStruct(w.shape, F32) for w, _, _ in vec_state for _ in range(4)]
    out_shape += [jax.ShapeDtypeStruct(w.shape, F32) for w, _, _ in mat_state for _ in range(3)]
    out = pl.pallas_call(body, name="adamw_replicated", out_shape=out_shape, compiler_params=_params())(*args)
    vec_out = [tuple(out[4 * i:4 * i + 4]) for i in range(n_vec)]
    mat_out = [tuple(out[4 * n_vec + 3 * j:4 * n_vec + 3 * j + 3]) for j in range(n_mat)]
    return vec_out, mat_out


def _adamw(w, recv, m, v, name):
    rows, width = w.shape
    n = recv.shape[0]
    if rows % 16 == 0 or width % 256:
        tr, tc = _row_tile(rows), width
    else:
        tr, tc = rows, 256

    def body(w_ref, r_ref, m_ref, v_ref, g_ref, d_ref, nm_ref, nv_ref):
        gv = r_ref[0].astype(F32)
        for s in range(1, n):
            gv = gv + r_ref[s].astype(F32)
        g_ref[...] = gv
        d_ref[...], nm_ref[...], nv_ref[...] = _adam_update(w_ref[...], gv, m_ref[...], v_ref[...])

    spec = pl.BlockSpec((tr, tc), lambda i, j: (i, j))
    shape = jax.ShapeDtypeStruct((rows, width), F32)
    return pl.pallas_call(
        body, name=name, grid=(rows // tr, width // tc),
        in_specs=[spec, pl.BlockSpec((n, tr, tc), lambda i, j: (0, i, j)), spec, spec],
        out_specs=[spec] * 4, out_shape=[shape] * 4,
        compiler_params=_params(("parallel", "parallel")),
    )(w, recv, m, v)


BIG_NAMES = ("w_in", "w_out_ssm", "w_out_lru", "w_out", "w_ffn_in", "w_ffn_out", "ssm_conv_w", "lru_conv_w")
TRANSPOSED = ("w_in", "w_ffn_in")
CONV_NAMES = ("ssm_conv_w", "lru_conv_w")
MATMUL_NAMES = BIG_NAMES[:6]
NEEDED_FIRST = ("w_in", "ssm_conv_w", "lru_conv_w")
GATHERED_IN_PROJ = ("w_ffn_out",)
GATHERED_IN_SSD = ("w_ffn_in",)
GATHERED_IN_LRU = ("w_out_ssm", "w_out_lru", "w_out")
EXCHANGED_IN_SSD = ("w_ffn_in", "w_ffn_out")
EXCHANGED_IN_LRU = ("w_out_ssm", "w_out_lru", "w_out")
SMALL_VECTORS = ("norm1_w", "b_branch_gate", "ssm_conv_b", "ssm_dt_bias", "ssm_a_log", "ssm_d", "ssm_norm_w",
                 "lru_conv_b", "lru_b_r", "lru_b_i", "lru_lambda", "norm2_w", "norm_f_w")
SMALL_MATRICES = ("lru_w_r", "lru_w_i")


def _col_shards(full):
    rows, cols = full.shape
    return full.reshape(rows, N_DEV, cols // N_DEV).transpose(1, 0, 2)


def _from_col_shards(g):
    n, rows, w = g.shape
    return g.transpose(1, 0, 2).reshape(rows, n * w)


def kernel(x, norm1_w, w_in, b_branch_gate, ssm_conv_w, ssm_conv_b, ssm_dt_bias, ssm_a_log, ssm_d, ssm_norm_w, w_out_ssm, lru_conv_w, lru_conv_b, lru_w_r, lru_b_r, lru_w_i, lru_b_i, lru_lambda, w_out_lru, w_out, norm2_w, w_ffn_in, w_ffn_out, norm_f_w, loss_target, m_norm1_w, m_w_in, m_b_branch_gate, m_ssm_conv_w, m_ssm_conv_b, m_ssm_dt_bias, m_ssm_a_log, m_ssm_d, m_ssm_norm_w, m_w_out_ssm, m_lru_conv_w, m_lru_conv_b, m_lru_w_r, m_lru_b_r, m_lru_w_i, m_lru_b_i, m_lru_lambda, m_w_out_lru, m_w_out, m_norm2_w, m_w_ffn_in, m_w_ffn_out, m_norm_f_w, v_norm1_w, v_w_in, v_b_branch_gate, v_ssm_conv_w, v_ssm_conv_b, v_ssm_dt_bias, v_ssm_a_log, v_ssm_d, v_ssm_norm_w, v_w_out_ssm, v_lru_conv_w, v_lru_conv_b, v_lru_w_r, v_lru_b_r, v_lru_w_i, v_lru_b_i, v_lru_lambda, v_w_out_lru, v_w_out, v_norm2_w, v_w_ffn_in, v_w_ffn_out, v_norm_f_w):
    given = dict(locals())
    weights = {n: given[n] for n in BIG_NAMES + SMALL_VECTORS + SMALL_MATRICES}
    t = x.shape[1]
    xt = x[0]
    tgt = loss_target[0]

    def local(n, a):
        return a[0].T if n in TRANSPOSED else a[0]

    def as_output(n, a):
        return a.T[None] if n in TRANSPOSED else a[None]

    def shard(n):
        s = local(n, weights[n])
        return s.astype(BF16) if n in MATMUL_NAMES else s

    def unshard(n, g):
        return _from_col_shards(g) if n in CONV_NAMES else g.reshape(-1, g.shape[-1])

    def grad_slices(n):
        g = grads[n]
        return (_col_shards(g) if n in CONV_NAMES else g.reshape(N_DEV, -1, g.shape[-1])).astype(BF16)

    gathered = _all_gather([shard(n) for n in NEEDED_FIRST], "gather_in_weights")
    full = {n: unshard(n, g) for n, g in zip(NEEDED_FIRST, gathered)}
    ssm_cw, lru_cw = full["ssm_conv_w"], full["lru_conv_w"]
    wi_t = full["w_in"]
    lru_rows = wi_t[ORIG_LRU_X:].reshape(2, LRU_BLOCKS, LANES, D_MODEL).transpose(1, 0, 2, 3)
    w_tail = jnp.concatenate([wi_t[W_IN_MAIN:ORIG_DT], lru_rows.reshape(2 * LRU_WIDTH, D_MODEL),
                              wi_t[ORIG_DT:ORIG_LRU_X], jnp.zeros((PROJ_W - IN_PROJ, D_MODEL), BF16)], axis=0)

    def pad_heads(a):
        return jnp.pad(a.reshape(1, SSM_HEADS), ((0, 0), (0, LANES - SSM_HEADS)))

    dt_bias_p = pad_heads(ssm_dt_bias)
    a_log_p = pad_heads(ssm_a_log)
    d_exp = jnp.repeat(ssm_d.reshape(SSM_HEADS), SSM_HEAD_DIM).reshape(1, SSM_INNER)
    lru_wr, lru_wi = lru_w_r[0], lru_w_i[0]

    hn1 = _rmsnorm_fwd(xt, norm1_w, "norm1_fwd")
    later = [shard(n) for n in GATHERED_IN_PROJ]
    proj, (gathered,) = _mm(
        hn1, wi_t, tb=True, b_tail=w_tail, b_main=W_IN_MAIN, name="in_proj",
        hosted=[(_gather_two_phase, later, [jax.ShapeDtypeStruct((N_DEV,) + v.shape, v.dtype) for v in later])])
    full.update({n: unshard(n, g) for n, g in zip(GATHERED_IN_PROJ, gathered)})
    xbc_act = _ssm_conv_fwd(proj, ssm_cw, ssm_conv_b)
    (y_ssd, s_in_all), gathered = _ssd_fwd(xbc_act, proj, dt_bias_p, a_log_p, d_exp,
                                           gather=[shard(n) for n in GATHERED_IN_SSD])
    full.update({n: unshard(n, g) for n, g in zip(GATHERED_IN_SSD, gathered)})
    (l_out, h_lru), gathered = _lru_fwd(proj, lru_cw, lru_conv_b, lru_wr, lru_b_r, lru_wi, lru_b_i, lru_lambda,
                                        gather=[shard(n) for n in GATHERED_IN_LRU])
    full.update({n: unshard(n, g) for n, g in zip(GATHERED_IN_LRU, gathered)})
    y_pre = _ssm_post_fwd(y_ssd, proj, ssm_norm_w)
    y_ssm = _mm(y_pre, full["w_out_ssm"], out_dtype=BF16, name="out_ssm")
    y_lru = _mm(l_out, full["w_out_lru"], out_dtype=BF16, name="out_lru")
    merged = _merge_fwd(proj, b_branch_gate, y_ssm, y_lru)
    h1, hn2 = _mm(merged, full["w_out"], epilogue=_residual_norm_epilogue(xt, norm2_w), name="out_proj")
    gate, up, act = _ffn_in_swiglu(hn2, full["w_ffn_in"])

    grads = {}
    dh2, grads["norm_f_w"], loss_cols = _mm(
        act, full["w_ffn_out"], epilogue=_loss_epilogue(h1, norm_f_w.reshape(1, D_MODEL), tgt), name="ffn_out")
    loss = lax.psum(0.5 * jnp.sum(loss_cols) / D_MODEL, AXES)
    dact = _mm(dh2, full["w_ffn_out"], tb=True, out_dtype=BF16, name="d_act")
    grads["w_ffn_out"] = _mm(act, dh2, ta=True, out_dtype=BF16, name="dw_ffn_out")
    dgu = _swiglu_bwd(gate, up, dact)
    dh1, grads["norm2_w"] = _mm(dgu, full["w_ffn_in"], epilogue=_norm_bwd_epilogue(h1, norm2_w, dh2), name="d_hn2")
    grads["w_ffn_in"] = _mm(dgu, hn2, ta=True, out_dtype=BF16, name="dw_ffn_in")
    dmerged = _mm(dh1, full["w_out"], tb=True, out_dtype=BF16, name="d_merged")
    grads["w_out"] = _mm(merged, dh1, ta=True, out_dtype=BF16, name="dw_out")
    dy_ssm, dy_lru, dproj, grads["b_branch_gate"] = _merge_bwd(proj, b_branch_gate, y_ssm, y_lru, dmerged)
    dy_pre = _mm(dy_ssm, full["w_out_ssm"], tb=True, out_dtype=BF16, name="d_y_pre")
    grads["w_out_ssm"] = _mm(y_pre, dy_ssm, ta=True, out_dtype=BF16, name="dw_out_ssm")
    dl_out = _mm(dy_lru, full["w_out_lru"], tb=True, out_dtype=BF16, name="d_l_out")
    grads["w_out_lru"] = _mm(l_out, dy_lru, ta=True, out_dtype=BF16, name="dw_out_lru")
    dy_ssd, dproj, grads["ssm_norm_w"] = _ssm_post_bwd(y_ssd, proj, ssm_norm_w, dy_pre, dproj)
    (dxbc_act, dproj, dbias, dalog, ddcol), recv_in_ssd = _ssd_bwd(
        xbc_act, proj, s_in_all, dy_ssd, dt_bias_p, a_log_p, d_exp, dproj,
        exchange=[grad_slices(n) for n in EXCHANGED_IN_SSD])
    grads["ssm_dt_bias"] = dbias[:, :SSM_HEADS]
    grads["ssm_a_log"] = dalog[:, :SSM_HEADS]
    grads["ssm_d"] = ddcol.reshape(SSM_HEADS, SSM_HEAD_DIM).sum(axis=1).reshape(1, SSM_HEADS)
    dproj, grads["ssm_conv_w"], grads["ssm_conv_b"] = _ssm_conv_bwd(proj, ssm_cw, ssm_conv_b, dxbc_act, dproj)
    ((dproj, grads["lru_conv_w"], grads["lru_conv_b"], dwr, grads["lru_b_r"], dwi, grads["lru_b_i"],
      grads["lru_lambda"]), recv_in_lru) = _lru_bwd(
        proj, lru_cw, lru_conv_b, lru_wr, lru_b_r, lru_wi, lru_b_i, lru_lambda, h_lru, dl_out, dproj,
        exchange=[grad_slices(n) for n in EXCHANGED_IN_LRU])
    grads["lru_w_r"], grads["lru_w_i"] = dwr[None], dwi[None]
    dwpt = _mm(dproj, hn1, ta=True, out_dtype=BF16, name="dw_in")
    lru_rows = dwpt[COL_LRU:COL_DT].reshape(LRU_BLOCKS, 2, LANES, D_MODEL).transpose(1, 0, 2, 3)
    grads["w_in"] = jnp.concatenate([dwpt[:ORIG_DT], dwpt[COL_DT:COL_DT + SSM_HEADS],
                                     lru_rows.reshape(2 * LRU_WIDTH, D_MODEL)], axis=0)
    w_in_parts = grad_slices("w_in")
    from_sibling = _sibling_exchange(w_in_parts, "sibling_exchange_dw_in")
    chip_parts = [_pair_sum(w_in_parts, from_sibling, "pair_sum_dw_in")]
    direct = [grad_slices(n) for n in CONV_NAMES] + [grads[n].reshape(N_DEV, -1, LANES) for n in SMALL_MATRICES]
    (grad_x, grads["norm1_w"]), (recv_w_in, recv_direct) = _mm(
        dproj, wi_t, b_tail=w_tail, b_main=W_IN_MAIN, epilogue=_norm_bwd_epilogue(xt, norm1_w, dh1), name="d_hn1",
        hosted=[(_chip_exchange_phase, chip_parts, chip_parts), (_exchange_phase, direct, direct)])
    recv_first = list(recv_w_in) + list(recv_direct[:len(CONV_NAMES)])
    recv_mats = recv_direct[len(CONV_NAMES):]

    def as_rows(n, a):
        return a.reshape(1, -1) if n in SMALL_VECTORS else a.reshape(-1, LANES)

    vec_g = jnp.concatenate([jnp.pad(as_rows(n, grads[n]), ((0, 0), (0, (-grads[n].size) % LANES)))
                             for n in SMALL_VECTORS], axis=1)
    gathered = _all_gather([vec_g] + list(_sum_sources(recv_mats, "sum_gate_matrix_grads")), "gather_small_grads")
    vec_parts = gathered[0].reshape(N_DEV, -1)
    mat_g = [g.reshape(-1, LANES) for g in gathered[1:]]

    recv = dict(zip(EXCHANGED_IN_SSD + EXCHANGED_IN_LRU + NEEDED_FIRST,
                    list(recv_in_ssd) + list(recv_in_lru) + list(recv_first)))
    big_out = {n: _adamw(local(n, weights[n]), recv[n], local(n, given["m_" + n]), local(n, given["v_" + n]),
                         "adamw_" + n) for n in BIG_NAMES}

    def state(n):
        return tuple(as_rows(n, given[p + n]) for p in ("", "m_", "v_"))

    vec_out, mat_out = _adamw_small(vec_parts, [state(n) for n in SMALL_VECTORS],
                                    mat_g, [state(n) for n in SMALL_MATRICES])
    small_out = dict(zip(SMALL_VECTORS, vec_out))
    small_out.update({n: (g,) + out for n, g, out in zip(SMALL_MATRICES, mat_g, mat_out)})

    order = list(given)[1:24]
    results = []
    for q in range(4):
        vals = {n: as_output(n, big_out[n][q]) for n in BIG_NAMES}
        vals.update({n: out[q].reshape(weights[n].shape) for n, out in small_out.items()})
        results.extend(vals[n] for n in order)
    return (loss, grad_x[None], *results)
```

```python
import math

import jax
import jax.numpy as jnp
from jax import lax
from jax.experimental import pallas as pl
from jax.experimental.pallas import tpu as pltpu

F32 = jnp.float32
BF16 = jnp.bfloat16
HIGHEST = lax.Precision.HIGHEST
MESH = pl.DeviceIdType.MESH
AXES = ("x", "y", "c")
N_DEV = 8

D_MODEL = 1024
SSM_INNER = 2048
SSM_HEADS = 32
SSM_HEAD_DIM = 64
SSM_GROUPS = 4
SSM_STATE = 128
SSM_BC = SSM_GROUPS * SSM_STATE
SSM_CONV_DIM = SSM_INNER + 2 * SSM_BC
SSM_CHUNK = 128
SSM_PAIRS = SSM_HEADS // 2
CONV_K = 4
LRU_WIDTH = 1280
LRU_BLOCKS = 10
LRU_C = 8.0
FFN_HIDDEN = 2816
RMS_EPS = 1e-6
IN_PROJ = 9760

COL_GATES = 0
COL_Z = 2048
COL_XBC = 4096
COL_LRU = 7168
COL_DT = 9728
PROJ_W = 9856
ORIG_DT = 7168
ORIG_LRU_X = 7200
ORIG_LRU_Y = 8480
W_IN_MAIN = 7040

ADAM_LR = 0.001
ADAM_B1 = 0.9
ADAM_B2 = 0.999
ADAM_EPS = 1e-08
ADAM_WD = 0.01
ADAM_STEP = 10

LANES = 128
SUBLANES = 8
V7X_VMEM_BYTES = 64 * 1024 * 1024
VMEM_LIMIT = V7X_VMEM_BYTES * 3 // 4
VMEM_LIMIT_BIG = V7X_VMEM_BYTES * 15 // 16

NT = (((1,), (1,)), ((), ()))
TN = (((0,), (0,)), ((), ()))


def _params(sem=None, big=False):
    return pltpu.CompilerParams(dimension_semantics=sem,
                                vmem_limit_bytes=VMEM_LIMIT_BIG if big else VMEM_LIMIT)


def _blk(dim, cap):
    if dim <= cap:
        return dim
    for m in range(cap // LANES, 0, -1):
        if dim % (m * LANES) == 0:
            return m * LANES
    raise ValueError(f"no block for {dim}")


def _rows(t):
    return min(t, 256)


def _sigmoid(v):
    return 1.0 / (1.0 + jnp.exp(-v))


def _softplus(v):
    e = jnp.exp(-jnp.abs(v))
    u = 1.0 + e
    log1p = jnp.where(u == 1.0, e, jnp.log(u) * e / jnp.where(u == 1.0, 1.0, u - 1.0))
    return jnp.maximum(v, 0.0) + log1p


def _iota(shape, dim):
    return lax.broadcasted_iota(jnp.int32, shape, dim)


def _shift_down(v, s):
    if s == 0:
        return v
    return jnp.where(_iota(v.shape, 0) >= s, pltpu.roll(v, s, 0), 0.0)


def _shift_up(v, s):
    if s == 0:
        return v
    n = v.shape[0]
    return jnp.where(_iota(v.shape, 0) < n - s, pltpu.roll(v, n - s, 0), 0.0)


def _bdot(a, b, dn=None):
    a = a.astype(BF16)
    b = b.astype(BF16)
    if dn is None:
        return jnp.dot(a, b, preferred_element_type=F32)
    return lax.dot_general(a, b, dn, preferred_element_type=F32)


def _split_dot(a, e, dn=None, exact=True):
    hi = a.astype(BF16)
    if not exact:
        return _bdot(hi, e, dn)
    lo = (a - hi.astype(F32)).astype(BF16)
    return _bdot(hi, e, dn) + _bdot(lo, e, dn)


def _fdot(a, b, dn=None):
    if dn is None:
        return jnp.dot(a, b, precision=HIGHEST, preferred_element_type=F32)
    return lax.dot_general(a, b, dn, precision=HIGHEST, preferred_element_type=F32)


def _mm(a, b, *, ta=False, tb=False, add=None, hosted=(), out_dtype=F32, epilogue=None, b_tail=None, b_main=0, name):
    if ta:
        kdim, m = a.shape
    else:
        m, kdim = a.shape
    if tb:
        n, k2 = b.shape
    else:
        k2, n = b.shape
    if b_tail is not None:
        if tb:
            n = b_main + b_tail.shape[0]
        else:
            k2 = b_main + b_tail.shape[0]
    assert kdim == k2, (a.shape, b.shape, ta, tb)
    if epilogue is None:
        rows, vecs, row_dtypes, n_vec_out = ([] if add is None else [add]), [], [out_dtype], 0

        def finish(r, row_vals, vec_vals):
            return ((r + row_vals[0]) if row_vals else r,), ()
    else:
        assert add is None
        finish, rows, vecs, row_dtypes, n_vec_out = epilogue
    bm, bn, bk = _blk(m, 1408 if epilogue is None else 512), _blk(n, 1408), _blk(kdim, 1408)
    grid = (m // bm, n // bn, kdim // bk)
    nk = grid[2]
    assert n_vec_out == 0 or grid[1] == 1, "column sums are accumulated over the row tiles of whole rows"
    dn = (((0 if ta else 1,), (1 if tb else 0,)), ((), ()))
    n_ab = 2 if b_tail is None else 3
    main_blocks = b_main // (bn if tb else bk)
    assert b_main % (bn if tb else bk) == 0
    n_in = n_ab + len(rows) + len(vecs)
    n_out = len(row_dtypes) + n_vec_out
    sizes = [len(arrays) for _, arrays, _ in hosted]
    n_ex = sum(sizes)

    def body(*refs):
        a_ref, b_ref = refs[:2]
        row_refs, vec_refs = refs[n_ab:n_ab + len(rows)], refs[n_ab + len(rows):n_in]
        out_refs = refs[n_in + n_ex:n_in + n_ex + n_out]
        acc = refs[n_in + 2 * n_ex + n_out]
        comms, at = [], 0
        for g, size in enumerate(sizes):
            sems = refs[n_in + 2 * n_ex + n_out + 1 + 3 * g:n_in + 2 * n_ex + n_out + 4 + 3 * g]
            comms.append((refs[n_in + at:n_in + at + size],
                          refs[n_in + n_ex + n_out + at:n_in + n_ex + n_out + at + size]) + tuple(sems))
            at += size
        step = (pl.program_id(0) * grid[1] + pl.program_id(1)) * nk + pl.program_id(2)
        k = pl.program_id(2)
        if n_ex:
            @pl.when(step == 0)
            def _():
                for (phase_fn, _, _), comm in zip(hosted, comms):
                    phase_fn("start", *comm)

        @pl.when(k == 0)
        def _():
            acc[...] = jnp.zeros_like(acc)

        bv = b_ref[...]
        if b_tail is not None:
            bv = jnp.where(pl.program_id(1 if tb else 2) < main_blocks, bv, refs[2][...])
        acc[...] += lax.dot_general(a_ref[...].astype(BF16), bv.astype(BF16), dn, preferred_element_type=F32)

        @pl.when(k == nk - 1)
        def _():
            row_outs, col_sums = finish(acc[...], [r[...] for r in row_refs], [v[...] for v in vec_refs])
            for o_ref, val in zip(out_refs, row_outs):
                o_ref[...] = val.astype(o_ref.dtype)
            for o_ref, val in zip(out_refs[len(row_dtypes):], col_sums):
                @pl.when(pl.program_id(0) == 0)
                def _():
                    o_ref[...] = val

                @pl.when(pl.program_id(0) > 0)
                def _():
                    o_ref[...] += val

        if n_ex:
            @pl.when(step == grid[0] * grid[1] * nk - 1)
            def _():
                for (phase_fn, _, _), comm in zip(hosted, comms):
                    phase_fn("finish", *comm)

    a_spec = pl.BlockSpec((bk, bm), lambda i, j, k: (k, i)) if ta else pl.BlockSpec((bm, bk), lambda i, j, k: (i, k))
    b_specs = [pl.BlockSpec((bn, bk), lambda i, j, k: (j, k)) if tb else pl.BlockSpec((bk, bn), lambda i, j, k: (k, j))]
    if b_tail is not None:
        last = main_blocks - 1
        if tb:
            b_specs = [pl.BlockSpec((bn, bk), lambda i, j, k: (jnp.minimum(j, last), k)),
                       pl.BlockSpec((bn, bk), lambda i, j, k: (jnp.maximum(j - main_blocks, 0), k))]
        else:
            b_specs = [pl.BlockSpec((bk, bn), lambda i, j, k: (jnp.minimum(k, last), j)),
                       pl.BlockSpec((bk, bn), lambda i, j, k: (jnp.maximum(k - main_blocks, 0), j))]
    o_spec = pl.BlockSpec((bm, bn), lambda i, j, k: (i, j))
    v_spec = pl.BlockSpec((1, bn), lambda i, j, k: (0, j))
    in_specs = [a_spec] + b_specs + [o_spec] * len(rows) + [v_spec] * len(vecs) + [HBM] * n_ex
    args = ([a, b] + ([] if b_tail is None else [b_tail]) + list(rows) + list(vecs)
            + [p for _, arrays, _ in hosted for p in arrays])
    sequential = n_ex or n_vec_out
    out = pl.pallas_call(
        body, name=name, grid=grid,
        in_specs=in_specs, out_specs=[o_spec] * len(row_dtypes) + [v_spec] * n_vec_out + [HBM] * n_ex,
        out_shape=[jax.ShapeDtypeStruct((m, n), dt) for dt in row_dtypes]
        + [jax.ShapeDtypeStruct((1, n), F32)] * n_vec_out
        + [jax.ShapeDtypeStruct(r.shape, r.dtype) for _, _, results in hosted for r in results],
        scratch_shapes=[pltpu.VMEM((bm, bn), F32)] + [s for size in sizes for s in _comm_scratch(size)],
        compiler_params=_params(("arbitrary",) * 3 if sequential else ("parallel", "parallel", "arbitrary")),
    )(*args)
    result = out[0] if n_out == 1 else tuple(out[:n_out])
    if not n_ex:
        return result
    received, at = [], n_out
    for size in sizes:
        received.append(out[at:at + size])
        at += size
    return result, received


def _rmsnorm_fwd(x, w, name):
    t, d = x.shape
    tr = _rows(t)

    def body(x_ref, w_ref, o_ref):
        xv = x_ref[...]
        rstd = lax.rsqrt(jnp.mean(xv * xv, axis=-1, keepdims=True) + RMS_EPS)
        o_ref[...] = (xv * rstd * w_ref[...]).astype(BF16)

    return pl.pallas_call(
        body, name=name, grid=(t // tr,),
        in_specs=[pl.BlockSpec((tr, d), lambda i: (i, 0)), pl.BlockSpec((1, d), lambda i: (0, 0))],
        out_specs=pl.BlockSpec((tr, d), lambda i: (i, 0)),
        out_shape=jax.ShapeDtypeStruct((t, d), BF16),
        compiler_params=_params(("parallel",)),
    )(x, w)


def _normalize(h):
    rstd = lax.rsqrt(jnp.mean(h * h, axis=-1, keepdims=True) + RMS_EPS)
    return rstd, h * rstd


def _residual_norm_epilogue(x, w):
    def finish(r, rows, vecs):
        h = r + rows[0]
        return (h, _normalize(h)[1] * vecs[0]), ()

    return finish, [x], [w], [F32, BF16], 0


def _norm_bwd_epilogue(x, w, dres):
    def finish(r, rows, vecs):
        rstd, xhat = _normalize(rows[0])
        dxhat = r * vecs[0]
        m = jnp.mean(dxhat * xhat, axis=-1, keepdims=True)
        return (rstd * (dxhat - xhat * m) + rows[1],), (jnp.sum(r * xhat, axis=0, keepdims=True),)

    return finish, [x, dres], [w], [F32], 1


def _loss_epilogue(h1, w, tgt):
    d = h1.shape[1]

    def finish(r, rows, vecs):
        rstd, xhat = _normalize(r + rows[0])
        err = xhat * vecs[0] - rows[1]
        dyv = err * (1.0 / d)
        dxhat = dyv * vecs[0]
        m = jnp.mean(dxhat * xhat, axis=-1, keepdims=True)
        return ((rstd * (dxhat - xhat * m),),
                (jnp.sum(dyv * xhat, axis=0, keepdims=True), jnp.sum(err * err, axis=0, keepdims=True)))

    return finish, [h1, tgt], [w], [F32], 2


def _merge_fwd(proj, bg, ys, yl):
    t = proj.shape[0]
    d = D_MODEL
    tr = _rows(t)

    def body(ps_ref, pl_ref, bg_ref, ys_ref, yl_ref, o_ref):
        gs = _sigmoid(ps_ref[...] + bg_ref[:, 0:d])
        gl = _sigmoid(pl_ref[...] + bg_ref[:, d:2 * d])
        o_ref[...] = (gs * ys_ref[...] + gl * yl_ref[...]).astype(BF16)

    row = pl.BlockSpec((tr, d), lambda i: (i, 0))
    return pl.pallas_call(
        body, name="merge_fwd", grid=(t // tr,),
        in_specs=[row, pl.BlockSpec((tr, d), lambda i: (i, 1)), pl.BlockSpec((1, 2 * d), lambda i: (0, 0)), row, row],
        out_specs=row, out_shape=jax.ShapeDtypeStruct((t, d), BF16),
        compiler_params=_params(("parallel",)),
    )(proj, proj, bg, ys, yl)


def _merge_bwd(proj, bg, ys, yl, dm):
    t = proj.shape[0]
    d = D_MODEL
    tr = _rows(t)

    def body(ps_ref, pl_ref, bg_ref, ys_ref, yl_ref, dm_ref, dys_ref, dyl_ref, dg_ref, dbg_ref):
        i = pl.program_id(0)
        gs = _sigmoid(ps_ref[...] + bg_ref[:, 0:d])
        gl = _sigmoid(pl_ref[...] + bg_ref[:, d:2 * d])
        dmv = dm_ref[...]
        dys_ref[...] = (dmv * gs).astype(BF16)
        dyl_ref[...] = (dmv * gl).astype(BF16)
        dgs = dmv * ys_ref[...] * gs * (1.0 - gs)
        dgl = dmv * yl_ref[...] * gl * (1.0 - gl)
        dg_ref[:, 0:d] = dgs.astype(BF16)
        dg_ref[:, d:2 * d] = dgl.astype(BF16)

        @pl.when(i == 0)
        def _():
            dbg_ref[...] = jnp.zeros_like(dbg_ref)

        dbg_ref[:, 0:d] += jnp.sum(dgs, axis=0, keepdims=True)
        dbg_ref[:, d:2 * d] += jnp.sum(dgl, axis=0, keepdims=True)

    row = pl.BlockSpec((tr, d), lambda i: (i, 0))
    wide = pl.BlockSpec((tr, 2 * d), lambda i: (i, 0))
    vec = pl.BlockSpec((1, 2 * d), lambda i: (0, 0))
    return pl.pallas_call(
        body, name="merge_bwd", grid=(t // tr,),
        in_specs=[row, pl.BlockSpec((tr, d), lambda i: (i, 1)), vec, row, row, row],
        out_specs=[row, row, wide, vec],
        out_shape=[jax.ShapeDtypeStruct((t, d), BF16), jax.ShapeDtypeStruct((t, d), BF16),
                   jax.ShapeDtypeStruct((t, PROJ_W), BF16), jax.ShapeDtypeStruct((1, 2 * d), F32)],
        compiler_params=_params(("arbitrary",)),
    )(proj, proj, bg, ys, yl, dm)


def _ffn_in_swiglu(hn, wt):
    t, d = hn.shape
    f = FFN_HIDDEN
    bm, bn = _blk(t, 1024), _blk(f, 1408)
    nj = f // bn

    def body(a_ref, wg_ref, wu_ref, g_ref, u_ref, act_ref):
        a = a_ref[...]
        g = _bdot(a, wg_ref[...], NT)
        u = _bdot(a, wu_ref[...], NT)
        g_ref[...] = g.astype(BF16)
        u_ref[...] = u.astype(BF16)
        act_ref[...] = (g * _sigmoid(g) * u).astype(BF16)

    out = pl.BlockSpec((bm, bn), lambda i, j: (i, j))
    shape = jax.ShapeDtypeStruct((t, f), BF16)
    return pl.pallas_call(
        body, name="ffn_in_swiglu", grid=(t // bm, nj),
        in_specs=[pl.BlockSpec((bm, d), lambda i, j: (i, 0)), pl.BlockSpec((bn, d), lambda i, j: (j, 0)),
                  pl.BlockSpec((bn, d), lambda i, j: (nj + j, 0))],
        out_specs=[out, out, out], out_shape=[shape, shape, shape],
        compiler_params=_params(("parallel", "parallel")),
    )(hn, wt, wt)


def _swiglu_bwd(g_all, u_all, dact):
    t, f = g_all.shape
    tr = _rows(t)

    def body(g_ref, u_ref, da_ref, o_ref):
        g = g_ref[...].astype(F32)
        sg = _sigmoid(g)
        da = da_ref[...].astype(F32)
        o_ref[:, 0:f] = (da * u_ref[...].astype(F32) * (sg * (1.0 + g * (1.0 - sg)))).astype(BF16)
        o_ref[:, f:2 * f] = (da * g * sg).astype(BF16)

    row = pl.BlockSpec((tr, f), lambda i: (i, 0))
    return pl.pallas_call(
        body, name="swiglu_bwd", grid=(t // tr,),
        in_specs=[row, row, row],
        out_specs=pl.BlockSpec((tr, 2 * f), lambda i: (i, 0)),
        out_shape=jax.ShapeDtypeStruct((t, 2 * f), BF16),
        compiler_params=_params(("parallel",)),
    )(g_all, u_all, dact)


def _conv_pre(xv, wv, bv):
    pre = bv + wv[CONV_K - 1:CONV_K, :] * xv
    for k in range(CONV_K - 1):
        pre = pre + wv[k:k + 1, :] * _shift_down(xv, CONV_K - 1 - k)
    return pre


def _ssm_conv_fwd(proj, w, b):
    t = proj.shape[0]
    nb = SSM_CONV_DIM // LANES
    c0 = COL_XBC // LANES

    def body(x_ref, w_ref, b_ref, o_ref):
        pre = _conv_pre(x_ref[...], w_ref[...], b_ref[...])
        o_ref[...] = (pre * _sigmoid(pre)).astype(BF16)

    return pl.pallas_call(
        body, name="ssm_conv_fwd", grid=(nb,),
        in_specs=[pl.BlockSpec((t, LANES), lambda j: (0, c0 + j)), pl.BlockSpec((CONV_K, LANES), lambda j: (0, j)),
                  pl.BlockSpec((1, LANES), lambda j: (0, j))],
        out_specs=pl.BlockSpec((t, LANES), lambda j: (0, j)),
        out_shape=jax.ShapeDtypeStruct((t, SSM_CONV_DIM), BF16),
        compiler_params=_params(("parallel",)),
    )(proj, w, b)


def _ssm_conv_bwd(proj, w, b, dact, dproj):
    t = proj.shape[0]
    nb = SSM_CONV_DIM // LANES
    c0 = COL_XBC // LANES

    def body(x_ref, w_ref, b_ref, da_ref, dproj_in, dx_ref, dw_ref, db_ref):
        xv = x_ref[...]
        wv = w_ref[...]
        pre = _conv_pre(xv, wv, b_ref[...])
        sg = _sigmoid(pre)
        dpre = da_ref[...] * (sg * (1.0 + pre * (1.0 - sg)))
        dx = wv[CONV_K - 1:CONV_K, :] * dpre
        for k in range(CONV_K - 1):
            dx = dx + wv[k:k + 1, :] * _shift_up(dpre, CONV_K - 1 - k)
        dx_ref[...] = dx.astype(BF16)
        for k in range(CONV_K):
            dw_ref[k:k + 1, :] = jnp.sum(dpre * _shift_down(xv, CONV_K - 1 - k), axis=0, keepdims=True)
        db_ref[...] = jnp.sum(dpre, axis=0, keepdims=True)

    col = pl.BlockSpec((t, LANES), lambda j: (0, j))
    wsp = pl.BlockSpec((CONV_K, LANES), lambda j: (0, j))
    bsp = pl.BlockSpec((1, LANES), lambda j: (0, j))
    return pl.pallas_call(
        body, name="ssm_conv_bwd", grid=(nb,),
        in_specs=[pl.BlockSpec((t, LANES), lambda j: (0, c0 + j)), wsp, bsp, col, HBM],
        out_specs=[pl.BlockSpec((t, LANES), lambda j: (0, c0 + j)), wsp, bsp],
        out_shape=[jax.ShapeDtypeStruct(dproj.shape, dproj.dtype), jax.ShapeDtypeStruct((CONV_K, SSM_CONV_DIM), F32),
                   jax.ShapeDtypeStruct((1, SSM_CONV_DIM), F32)],
        input_output_aliases={4: 0},
        compiler_params=_params(("parallel",)),
    )(proj, w, b, dact, dproj)


def _ssd_chunk_terms(dtr, bias, alog):
    a = -jnp.exp(alog)
    dt = _softplus(dtr + bias)
    row = _iota((SSM_CHUNK, SSM_CHUNK), 0)
    col = _iota((SSM_CHUNK, SSM_CHUNK), 1)
    tri = (row >= col).astype(F32)
    cs = _fdot(tri, dt * a)
    dec = jnp.exp(cs[SSM_CHUNK - 1:SSM_CHUNK, :] - cs)
    ecs = jnp.exp(cs)
    off = _iota((LANES, SSM_INNER), 1) - SSM_HEAD_DIM * _iota((LANES, SSM_INNER), 0)
    expand = jnp.where(jnp.logical_and(off >= 0, off < SSM_HEAD_DIM), 1.0, 0.0).astype(BF16)
    return a, dt, cs, dec, ecs, expand, row, col


def _ssd_specs(t):
    nc = t // SSM_CHUNK
    xs = pl.BlockSpec((SSM_CHUNK, SSM_INNER), lambda c: (c, 0))
    bm = pl.BlockSpec((SSM_CHUNK, SSM_BC), lambda c: (c, SSM_INNER // SSM_BC))
    cm = pl.BlockSpec((SSM_CHUNK, SSM_BC), lambda c: (c, SSM_INNER // SSM_BC + 1))
    dtr = pl.BlockSpec((SSM_CHUNK, LANES), lambda c: (c, COL_DT // LANES))
    vec = pl.BlockSpec((1, LANES), lambda c: (0, 0))
    wide = pl.BlockSpec((1, SSM_INNER), lambda c: (0, 0))
    return nc, xs, bm, cm, dtr, vec, wide


def _ssd_fwd(xbc_act, proj, bias, alog, dexp, gather):
    t = proj.shape[0]
    nc, xs_s, bm_s, cm_s, dtr_s, vec, wide = _ssd_specs(t)
    n = len(gather)

    def body(*refs):
        xs_ref, b_ref, c_ref, dtr_ref, bias_ref, alog_ref, dexp_ref = refs[:7]
        y_ref, sin_ref = refs[7 + n:9 + n]
        state = refs[9 + 2 * n]
        comm = (refs[7:7 + n], refs[9 + n:9 + 2 * n]) + tuple(refs[10 + 2 * n:])
        chunk = pl.program_id(0)

        @pl.when(chunk == 0)
        def _():
            _gather_phase("start", *comm)
            state[...] = jnp.zeros_like(state)

        @pl.when(chunk == (3 * nc) // 4)
        def _():
            _gather_phase("forward", *comm)

        a, dt, cs, dec, ecs, expand, row, col = _ssd_chunk_terms(dtr_ref[...], bias_ref[...], alog_ref[...])
        cst = cs.T
        dt_x = _split_dot(dt, expand, exact=False)
        dec_x = _split_dot(dec, expand, exact=False)
        ecs_x = _split_dot(ecs, expand)
        xs = xs_ref[...].astype(F32)
        xdt = xs * dt_x
        xdec = xdt * dec_x
        lane_lo = col < SSM_HEAD_DIM
        causal = row >= col
        sin_ref[0] = state[...]
        for g in range(SSM_GROUPS):
            bg = b_ref[:, g * SSM_STATE:(g + 1) * SSM_STATE].astype(BF16)
            cg = c_ref[:, g * SSM_STATE:(g + 1) * SSM_STATE].astype(BF16)
            cb = _bdot(cg, bg, NT)
            for q in range(SSM_PAIRS // SSM_GROUPS):
                pq = g * (SSM_PAIRS // SSM_GROUPS) + q
                sl = slice(pq * LANES, (pq + 1) * LANES)
                xp = xdt[:, sl].astype(BF16)
                yd = []
                for hh in range(2):
                    h = 2 * pq + hh
                    lmat = jnp.exp(jnp.where(causal, cs[:, h:h + 1] - cst[h:h + 1, :], -jnp.inf))
                    yd.append(_bdot(cb * lmat, xp))
                s_in = state[pq]
                y_off = _bdot(cg, s_in) * ecs_x[:, sl]
                y_ref[:, sl] = (jnp.where(lane_lo, yd[0], yd[1]) + y_off + xs[:, sl] * dexp_ref[:, sl]).astype(BF16)
                state[pq] = s_in * ecs_x[SSM_CHUNK - 1:SSM_CHUNK, sl] + _bdot(bg, xdec[:, sl], TN)

        @pl.when(chunk == nc - 1)
        def _():
            _gather_phase("finish", *comm)

    out = pl.pallas_call(
        body, name="ssd_fwd", grid=(nc,),
        in_specs=[xs_s, bm_s, cm_s, dtr_s, vec, vec, wide] + [HBM] * n,
        out_specs=[pl.BlockSpec((SSM_CHUNK, SSM_INNER), lambda c: (c, 0)),
                   pl.BlockSpec((1, SSM_PAIRS, SSM_STATE, LANES), lambda c: (c, 0, 0, 0))] + [HBM] * n,
        out_shape=[jax.ShapeDtypeStruct((t, SSM_INNER), BF16),
                   jax.ShapeDtypeStruct((nc, SSM_PAIRS, SSM_STATE, LANES), F32)]
        + [jax.ShapeDtypeStruct((N_DEV,) + v.shape, v.dtype) for v in gather],
        scratch_shapes=[pltpu.VMEM((SSM_PAIRS, SSM_STATE, LANES), F32)] + _comm_scratch(n),
        compiler_params=_params(("arbitrary",)),
    )(xbc_act, xbc_act, xbc_act, proj, bias, alog, dexp, *gather)
    return out[:2], out[2:]


def _ssd_bwd(xbc_act, proj, s_in_all, dy, bias, alog, dexp, dproj, exchange):
    n_ex = len(exchange)
    t = proj.shape[0]
    nc = t // SSM_CHUNK
    last = nc - 1
    xs_s = pl.BlockSpec((SSM_CHUNK, SSM_INNER), lambda c: (last - c, 0))
    bm_s = pl.BlockSpec((SSM_CHUNK, SSM_BC), lambda c: (last - c, SSM_INNER // SSM_BC))
    cm_s = pl.BlockSpec((SSM_CHUNK, SSM_BC), lambda c: (last - c, SSM_INNER // SSM_BC + 1))
    dtr_s = pl.BlockSpec((SSM_CHUNK, LANES), lambda c: (last - c, COL_DT // LANES))
    sin_s = pl.BlockSpec((1, SSM_PAIRS, SSM_STATE, LANES), lambda c: (last - c, 0, 0, 0))
    vec = pl.BlockSpec((1, LANES), lambda c: (0, 0))
    wide = pl.BlockSpec((1, SSM_INNER), lambda c: (0, 0))

    def body(*refs):
        xs_ref, b_ref, c_ref, dtr_ref, sin_ref, dy_ref, bias_ref, alog_ref, dexp_ref = refs[:9]
        dxbc_ref, ddtr_ref, dbias_ref, dalog_ref, ddcol_ref = refs[10 + n_ex:15 + n_ex]
        dstate, dxdt_s, yoff_s, rx_s, trow_s = refs[15 + 2 * n_ex:20 + 2 * n_ex]
        comm = (refs[10:10 + n_ex], refs[15 + n_ex:15 + 2 * n_ex]) + tuple(refs[20 + 2 * n_ex:])

        @pl.when(pl.program_id(0) == 0)
        def _():
            _exchange_phase("start", *comm)
            dstate[...] = jnp.zeros_like(dstate)
            trow_s[...] = jnp.zeros_like(trow_s)
            dbias_ref[...] = jnp.zeros_like(dbias_ref)
            dalog_ref[...] = jnp.zeros_like(dalog_ref)
            ddcol_ref[...] = jnp.zeros_like(ddcol_ref)

        dtr = dtr_ref[...]
        a, dt, cs, dec, ecs, expand, row, col = _ssd_chunk_terms(dtr, bias_ref[...], alog_ref[...])
        cst = cs.T
        dt_x = _split_dot(dt, expand, exact=False)
        dec_x = _split_dot(dec, expand, exact=False)
        ecs_x = _split_dot(ecs, expand)
        xs = xs_ref[...].astype(F32)
        dyv = dy_ref[...].astype(F32)
        xdt = xs * dt_x
        lane_lo = col < SSM_HEAD_DIM
        causal = row >= col
        ddcol_ref[...] += jnp.sum(dyv * xs, axis=0, keepdims=True)
        dcs_col = jnp.zeros((SSM_CHUNK, LANES), F32)
        dcs_row = jnp.zeros((LANES, SSM_CHUNK), F32)
        for g in range(SSM_GROUPS):
            bg = b_ref[:, g * SSM_STATE:(g + 1) * SSM_STATE].astype(BF16)
            cg = c_ref[:, g * SSM_STATE:(g + 1) * SSM_STATE].astype(BF16)
            cb = _bdot(cg, bg, NT)
            dgm = jnp.zeros((SSM_CHUNK, SSM_CHUNK), F32)
            dbg = jnp.zeros((SSM_CHUNK, SSM_STATE), F32)
            dcg = jnp.zeros((SSM_CHUNK, SSM_STATE), F32)
            for q in range(SSM_PAIRS // SSM_GROUPS):
                pq = g * (SSM_PAIRS // SSM_GROUPS) + q
                sl = slice(pq * LANES, (pq + 1) * LANES)
                dyp = dyv[:, sl]
                xp = xdt[:, sl]
                dxh = []
                for hh in range(2):
                    h = 2 * pq + hh
                    lmat = jnp.exp(jnp.where(causal, cs[:, h:h + 1] - cst[h:h + 1, :], -jnp.inf))
                    mmat = cb * lmat
                    dyh = jnp.where(lane_lo if hh == 0 else jnp.logical_not(lane_lo), dyp, 0.0)
                    dmm = _bdot(dyh, xp, NT)
                    pm = dmm * mmat
                    dcs_col = jnp.where(col == h, jnp.sum(pm, axis=1, keepdims=True), dcs_col)
                    dcs_row = jnp.where(row == h, jnp.sum(pm, axis=0, keepdims=True), dcs_row)
                    dgm = dgm + dmm * lmat
                    dxh.append(_bdot(mmat, dyp, TN))
                s_in = sin_ref[0, pq]
                ecs_p = ecs_x[:, sl]
                dec_p = dec_x[:, sl]
                etot_p = ecs_x[SSM_CHUNK - 1:SSM_CHUNK, sl]
                yoff_s[:, sl] = dyp * (_bdot(cg, s_in) * ecs_p)
                dq = dyp * ecs_p
                dcg = dcg + _bdot(dq, s_in, NT)
                ds = dstate[pq]
                r = _bdot(bg, ds)
                rx_s[:, sl] = r * xp
                dxdt_s[:, sl] = jnp.where(lane_lo, dxh[0], dxh[1]) + dec_p * r
                dbg = dbg + _bdot(xp * dec_p, ds, NT)
                trow_s[0:1, sl] = jnp.sum(ds * s_in, axis=0, keepdims=True) * etot_p
                dstate[pq] = etot_p * ds + _bdot(cg, dq, TN)
            dcg = dcg + _bdot(dgm, bg)
            dbg = dbg + _bdot(dgm, cg, TN)
            dxbc_ref[:, SSM_INNER + g * SSM_STATE:SSM_INNER + (g + 1) * SSM_STATE] = dbg.astype(BF16)
            dxbc_ref[:, SSM_INNER + SSM_BC + g * SSM_STATE:SSM_INNER + SSM_BC + (g + 1) * SSM_STATE] = dcg.astype(BF16)
        ddec = _split_dot(rx_s[...], expand, NT, exact=False) * dec
        dtot = _split_dot(trow_s[...], expand, NT, exact=False)[0:1, :]
        dcs = dcs_col - dcs_row.T + _split_dot(yoff_s[...], expand, NT, exact=False) - ddec
        dcs = dcs + jnp.where(row == SSM_CHUNK - 1, jnp.sum(ddec, axis=0, keepdims=True) + dtot, 0.0)
        da = _fdot((row <= col).astype(F32), dcs)
        dxdt = dxdt_s[...]
        ddt = da * a + _split_dot(dxdt * xs, expand, NT, exact=False)
        dalog_ref[...] += jnp.sum(da * dt, axis=0, keepdims=True) * a
        ddtr = ddt * _sigmoid(dtr + bias_ref[...])
        ddtr_ref[...] = ddtr.astype(BF16)
        dbias_ref[...] += jnp.sum(ddtr, axis=0, keepdims=True)
        dxbc_ref[:, 0:SSM_INNER] = (dxdt * dt_x + dyv * dexp_ref[...]).astype(BF16)

        @pl.when(pl.program_id(0) == last)
        def _():
            _exchange_phase("finish", *comm)

    out = pl.pallas_call(
        body, name="ssd_bwd", grid=(nc,),
        in_specs=[xs_s, bm_s, cm_s, dtr_s, sin_s, pl.BlockSpec((SSM_CHUNK, SSM_INNER), lambda c: (last - c, 0)),
                  vec, vec, wide, HBM] + [HBM] * n_ex,
        out_specs=[pl.BlockSpec((SSM_CHUNK, SSM_CONV_DIM), lambda c: (last - c, 0)), dtr_s, vec, vec, wide]
        + [HBM] * n_ex,
        out_shape=[jax.ShapeDtypeStruct((t, SSM_CONV_DIM), BF16), jax.ShapeDtypeStruct(dproj.shape, dproj.dtype),
                   jax.ShapeDtypeStruct((1, LANES), F32), jax.ShapeDtypeStruct((1, LANES), F32),
                   jax.ShapeDtypeStruct((1, SSM_INNER), F32)]
        + [jax.ShapeDtypeStruct(p.shape, p.dtype) for p in exchange],
        input_output_aliases={9: 1},
        scratch_shapes=[pltpu.VMEM((SSM_PAIRS, SSM_STATE, LANES), F32),
                        pltpu.VMEM((SSM_CHUNK, SSM_INNER), F32), pltpu.VMEM((SSM_CHUNK, SSM_INNER), F32),
                        pltpu.VMEM((SSM_CHUNK, SSM_INNER), F32), pltpu.VMEM((SUBLANES, SSM_INNER), F32)]
        + _comm_scratch(n_ex),
        compiler_params=_params(("arbitrary",)),
    )(xbc_act, xbc_act, xbc_act, proj, s_in_all, dy, bias, alog, dexp, dproj, *exchange)
    return out[:5], out[5:]


def _group_rstd(y):
    n = SSM_INNER // SSM_GROUPS
    parts = []
    for g in range(SSM_GROUPS):
        yg = y[:, g * n:(g + 1) * n]
        r = lax.rsqrt(jnp.mean(yg * yg, axis=-1, keepdims=True) + RMS_EPS)
        parts.append(jnp.broadcast_to(r, yg.shape))
    return jnp.concatenate(parts, axis=1)


def _group_mean(v):
    n = SSM_INNER // SSM_GROUPS
    parts = []
    for g in range(SSM_GROUPS):
        vg = v[:, g * n:(g + 1) * n]
        parts.append(jnp.broadcast_to(jnp.mean(vg, axis=-1, keepdims=True), vg.shape))
    return jnp.concatenate(parts, axis=1)


def _ssm_post_fwd(y_ssd, proj, nw):
    t = proj.shape[0]
    n = SSM_INNER
    tr = _rows(t)

    def body(y_ref, z_ref, nw_ref, o_ref):
        z = z_ref[...]
        y = y_ref[...] * (z * _sigmoid(z))
        o_ref[...] = (y * _group_rstd(y) * nw_ref[...]).astype(BF16)

    row = pl.BlockSpec((tr, n), lambda i: (i, 0))
    return pl.pallas_call(
        body, name="ssm_post_fwd", grid=(t // tr,),
        in_specs=[row, pl.BlockSpec((tr, n), lambda i: (i, COL_Z // n)), pl.BlockSpec((1, n), lambda i: (0, 0))],
        out_specs=row, out_shape=jax.ShapeDtypeStruct((t, n), BF16),
        compiler_params=_params(("parallel",)),
    )(y_ssd, proj, nw)


def _ssm_post_bwd(y_ssd, proj, nw, dout, dproj):
    t = proj.shape[0]
    n = SSM_INNER
    tr = _rows(t)

    def body(y_ref, z_ref, nw_ref, do_ref, dproj_in, dy_ref, dz_ref, dnw_ref):
        i = pl.program_id(0)
        z = z_ref[...]
        sg = _sigmoid(z)
        sz = z * sg
        ys = y_ref[...]
        y = ys * sz
        rstd = _group_rstd(y)
        yn = y * rstd
        dov = do_ref[...]
        dyn = dov * nw_ref[...]
        dyg = rstd * (dyn - yn * _group_mean(dyn * yn))
        dy_ref[...] = (dyg * sz).astype(BF16)
        dz_ref[...] = (dyg * ys * (sg * (1.0 + z * (1.0 - sg)))).astype(BF16)
        part = jnp.sum(dov * yn, axis=0, keepdims=True)

        @pl.when(i == 0)
        def _():
            dnw_ref[...] = part

        @pl.when(i > 0)
        def _():
            dnw_ref[...] += part

    row = pl.BlockSpec((tr, n), lambda i: (i, 0))
    vec = pl.BlockSpec((1, n), lambda i: (0, 0))
    return pl.pallas_call(
        body, name="ssm_post_bwd", grid=(t // tr,),
        in_specs=[row, pl.BlockSpec((tr, n), lambda i: (i, COL_Z // n)), vec, row, HBM],
        out_specs=[row, pl.BlockSpec((tr, n), lambda i: (i, COL_Z // n)), vec],
        out_shape=[jax.ShapeDtypeStruct((t, n), BF16), jax.ShapeDtypeStruct(dproj.shape, dproj.dtype),
                   jax.ShapeDtypeStruct((1, n), F32)],
        input_output_aliases={4: 1},
        compiler_params=_params(("arbitrary",)),
    )(y_ssd, proj, nw, dout, dproj)


SCAN_UNROLL = 8
GELU_C = math.sqrt(2.0 / math.pi)
GELU_K = 0.044715


def _gelu_parts(y):
    th = jnp.tanh(GELU_C * (y + GELU_K * y * y * y))
    val = 0.5 * y * (1.0 + th)
    grad = 0.5 * (1.0 + th) + 0.5 * y * (1.0 - th * th) * GELU_C * (1.0 + 3.0 * GELU_K * y * y)
    return val, grad


def _scan_tiles(a_ref, b_ref, h_ref, n_rows, reverse):
    n_tiles = n_rows // SUBLANES
    shape = (SUBLANES, a_ref.shape[1])
    row = _iota(shape, 0)

    def in_tile(av, bv):
        for s in (1, 2, 4):
            if reverse:
                keep = row < SUBLANES - s
                a_sh = jnp.where(keep, pltpu.roll(av, SUBLANES - s, 0), 1.0)
                b_sh = jnp.where(keep, pltpu.roll(bv, SUBLANES - s, 0), 0.0)
            else:
                keep = row >= s
                a_sh = jnp.where(keep, pltpu.roll(av, s, 0), 1.0)
                b_sh = jnp.where(keep, pltpu.roll(bv, s, 0), 0.0)
            bv = av * b_sh + bv
            av = av * a_sh
        return av, bv

    def step(k, carry):
        first = (n_tiles // SCAN_UNROLL - 1 - k) if reverse else k
        tiles = [first * SCAN_UNROLL + j for j in range(SCAN_UNROLL)]
        if reverse:
            tiles = tiles[::-1]
        ats = [pl.ds(pl.multiple_of(tile * SUBLANES, SUBLANES), SUBLANES) for tile in tiles]
        scanned = [in_tile(a_ref[at, :], b_ref[at, :]) for at in ats]
        for at, (av, bv) in zip(ats, scanned):
            hv = bv + av * carry
            h_ref[at, :] = hv
            carry = hv[0:1, :] if reverse else hv[SUBLANES - 1:SUBLANES, :]
        return carry

    assert n_tiles % SCAN_UNROLL == 0, n_rows
    lax.fori_loop(0, n_tiles // SCAN_UNROLL, step, jnp.zeros((1, a_ref.shape[1]), F32))


def _lru_gates(xl, cw, cb, wr, br, wi, bi, lam):
    u = cb + cw[CONV_K - 1:CONV_K, :] * xl
    for k in range(CONV_K - 1):
        u = u + cw[k:k + 1, :] * _shift_down(xl, CONV_K - 1 - k)
    r = _sigmoid(_bdot(u, wr) + br)
    i = _sigmoid(_bdot(u, wi) + bi)
    sp = _softplus(-lam)
    la = -LRU_C * r * sp
    a = jnp.exp(la)
    mult = jnp.sqrt(-jnp.tanh(la) * (a * a + 1.0))
    return u, r, i, sp, a, mult


def _lru_specs(t):
    c0 = COL_LRU // LANES
    xl = pl.BlockSpec((t, LANES), lambda j: (0, c0 + 2 * j))
    yl = pl.BlockSpec((t, LANES), lambda j: (0, c0 + 2 * j + 1))
    col = pl.BlockSpec((t, LANES), lambda j: (0, j))
    cw = pl.BlockSpec((CONV_K, LANES), lambda j: (0, j))
    vec = pl.BlockSpec((1, LANES), lambda j: (0, j))
    wblk = pl.BlockSpec((1, LANES, LANES), lambda j: (j, 0, 0))
    return xl, yl, col, cw, vec, wblk


def _lru_fwd(proj, cw, cb, wr, br, wi, bi, lam, gather):
    t = proj.shape[0]
    xl_s, yl_s, col, cw_s, vec, wblk = _lru_specs(t)
    n = len(gather)

    def body(*refs):
        xl_ref, yl_ref, cw_ref, cb_ref, wr_ref, br_ref, wi_ref, bi_ref, lam_ref = refs[:9]
        o_ref, h_ref = refs[9 + n:11 + n]
        a_s, b_s = refs[11 + 2 * n:13 + 2 * n]
        comm = (refs[9:9 + n], refs[11 + n:11 + 2 * n]) + tuple(refs[13 + 2 * n:])
        j = pl.program_id(0)
        for step, phase in ((0, "start"), (LRU_BLOCKS - 2, "forward")):
            @pl.when(j == step)
            def _():
                _gather_phase(phase, *comm)

        u, r, i, sp, a, mult = _lru_gates(xl_ref[...], cw_ref[...], cb_ref[...], wr_ref[0], br_ref[...],
                                          wi_ref[0], bi_ref[...], lam_ref[...])
        a_s[...] = a
        b_s[...] = mult * (i * u)
        _scan_tiles(a_s, b_s, h_ref, t, reverse=False)
        o_ref[...] = (h_ref[...] * _gelu_parts(yl_ref[...])[0]).astype(BF16)

        @pl.when(j == LRU_BLOCKS - 1)
        def _():
            _gather_phase("finish", *comm)

    out = pl.pallas_call(
        body, name="lru_fwd", grid=(LRU_BLOCKS,),
        in_specs=[xl_s, yl_s, cw_s, vec, wblk, vec, wblk, vec, vec] + [HBM] * n,
        out_specs=[col, col] + [HBM] * n,
        out_shape=[jax.ShapeDtypeStruct((t, LRU_WIDTH), BF16), jax.ShapeDtypeStruct((t, LRU_WIDTH), F32)]
        + [jax.ShapeDtypeStruct((N_DEV,) + v.shape, v.dtype) for v in gather],
        scratch_shapes=[pltpu.VMEM((t, LANES), F32)] * 2 + _comm_scratch(n),
        compiler_params=_params(("arbitrary",), big=True),
    )(proj, proj, cw, cb, wr, br, wi, bi, lam, *gather)
    return out[:2], out[2:]


def _lru_bwd(proj, cw, cb, wr, br, wi, bi, lam, h_all, dout, dproj, exchange):
    t = proj.shape[0]
    xl_s, yl_s, col, cw_s, vec, wblk = _lru_specs(t)
    pair = pl.BlockSpec((t, 2 * LANES), lambda j: (0, COL_LRU // (2 * LANES) + j))
    n_ex = len(exchange)

    def body(*refs):
        xl_ref, yl_ref, cw_ref, cb_ref, wr_ref, br_ref, wi_ref, bi_ref, lam_ref, h_ref, do_ref = refs[:11]
        dxy_ref, dcw_ref, dcb_ref, dwr_ref, dbr_ref, dwi_ref, dbi_ref, dlam_ref = refs[12 + n_ex:20 + n_ex]
        a_s, b_s, g_s = refs[20 + 2 * n_ex:23 + 2 * n_ex]
        comm = (refs[12:12 + n_ex], refs[20 + n_ex:20 + 2 * n_ex]) + tuple(refs[23 + 2 * n_ex:])

        @pl.when(pl.program_id(0) == 0)
        def _():
            _exchange_phase("start", *comm)

        xl = xl_ref[...]
        cwv = cw_ref[...]
        lam = lam_ref[...]
        u, r, i, sp, a, mult = _lru_gates(xl, cwv, cb_ref[...], wr_ref[0], br_ref[...], wi_ref[0], bi_ref[...], lam)
        v = i * u
        gl, dgl = _gelu_parts(yl_ref[...])
        dov = do_ref[...]
        h = h_ref[...]
        dxy_ref[:, LANES:2 * LANES] = (dov * h * dgl).astype(BF16)
        b_s[...] = dov * gl
        a_s[...] = _shift_up(a, 1)
        _scan_tiles(a_s, b_s, g_s, t, reverse=True)
        g = g_s[...]
        da = g * _shift_down(h, 1)
        dmult = g * v
        dv = g * mult
        dla = da * a - dmult * (a * a) / mult
        dr = dla * (-LRU_C * sp)
        dsp = jnp.sum(dla * (-LRU_C * r), axis=0, keepdims=True)
        dlam_ref[...] = -dsp * _sigmoid(-lam)
        dpr = dr * r * (1.0 - r)
        dpi = dv * u * i * (1.0 - i)
        dbr_ref[...] = jnp.sum(dpr, axis=0, keepdims=True)
        dbi_ref[...] = jnp.sum(dpi, axis=0, keepdims=True)
        dwr_ref[0] = _bdot(u, dpr, TN)
        dwi_ref[0] = _bdot(u, dpi, TN)
        du = dv * i + _bdot(dpr, wr_ref[0], NT) + _bdot(dpi, wi_ref[0], NT)
        dxl = cwv[CONV_K - 1:CONV_K, :] * du
        for k in range(CONV_K - 1):
            dxl = dxl + cwv[k:k + 1, :] * _shift_up(du, CONV_K - 1 - k)
        dxy_ref[:, 0:LANES] = dxl.astype(BF16)
        for k in range(CONV_K):
            dcw_ref[k:k + 1, :] = jnp.sum(du * _shift_down(xl, CONV_K - 1 - k), axis=0, keepdims=True)
        dcb_ref[...] = jnp.sum(du, axis=0, keepdims=True)

        @pl.when(pl.program_id(0) == LRU_BLOCKS - 1)
        def _():
            _exchange_phase("finish", *comm)

    out = pl.pallas_call(
        body, name="lru_bwd", grid=(LRU_BLOCKS,),
        in_specs=[xl_s, yl_s, cw_s, vec, wblk, vec, wblk, vec, vec, col, col, HBM] + [HBM] * n_ex,
        out_specs=[pair, cw_s, vec, wblk, vec, wblk, vec, vec] + [HBM] * n_ex,
        input_output_aliases={11: 0},
        out_shape=[jax.ShapeDtypeStruct(dproj.shape, dproj.dtype),
                   jax.ShapeDtypeStruct((CONV_K, LRU_WIDTH), F32), jax.ShapeDtypeStruct((1, LRU_WIDTH), F32),
                   jax.ShapeDtypeStruct((LRU_BLOCKS, LANES, LANES), F32), jax.ShapeDtypeStruct((1, LRU_WIDTH), F32),
                   jax.ShapeDtypeStruct((LRU_BLOCKS, LANES, LANES), F32), jax.ShapeDtypeStruct((1, LRU_WIDTH), F32),
                   jax.ShapeDtypeStruct((1, LRU_WIDTH), F32)]
        + [jax.ShapeDtypeStruct(p.shape, p.dtype) for p in exchange],
        scratch_shapes=[pltpu.VMEM((t, LANES), F32)] * 3 + _comm_scratch(n_ex),
        compiler_params=_params(("arbitrary",), big=True),
    )(proj, proj, cw, cb, wr, br, wi, bi, lam, h_all, dout, dproj, *exchange)
    return out[:8], out[8:]


def _mesh_pos():
    return lax.axis_index("x"), lax.axis_index("y"), lax.axis_index("c")


HBM = pl.BlockSpec(memory_space=pl.ANY)


def _comm_scratch(n):
    return [pltpu.SemaphoreType.DMA((n, 7)), pltpu.SemaphoreType.DMA((n, 7)), pltpu.SemaphoreType.DMA((n,))]


def _gather_phase(phase, v_refs, out_refs, send_sems, recv_sems, local_sems):
    n = len(v_refs)
    x, y, c = _mesh_pos()
    me, sibling = (x, y, c), (x, y, 1 - c)
    chips = [(1 - x, y), (x, 1 - y), (1 - x, 1 - y)]

    def block(a, px, py, pc):
        return out_refs[a].at[4 * px + 2 * py + pc]

    def copy(a, k, blk, to, src=None):
        return pltpu.make_async_remote_copy(
            src_ref=block(a, *blk) if src is None else src, dst_ref=block(a, *blk),
            send_sem=send_sems.at[a, k], recv_sem=recv_sems.at[a, k], device_id=to, device_id_type=MESH)

    def own(a):
        return pltpu.make_async_copy(v_refs[a], block(a, *me), local_sems.at[a])

    def first(a):
        return ([copy(a, 0, me, sibling, src=v_refs[a])]
                + [copy(a, 1 + j, me, (*chip, c), src=v_refs[a]) for j, chip in enumerate(chips)])

    def forward(a, j):
        return copy(a, 4 + j, (*chips[j], c), sibling)

    if phase == "start":
        for a in range(n):
            own(a).start()
        for a in range(n):
            for cp in first(a):
                cp.start()
    elif phase == "forward":
        for j in range(3):
            for a in range(n):
                copy(a, 1 + j, (*chips[j], c), me).wait_recv()
                forward(a, j).start()
    else:
        for a in range(n):
            copy(a, 0, sibling, me).wait_recv()
            for j in range(3):
                copy(a, 4 + j, (*chips[j], 1 - c), me).wait_recv()
        for a in range(n):
            for cp in first(a) + [forward(a, j) for j in range(3)]:
                cp.wait_send()
            own(a).wait()


def _gather_two_phase(phase, *comm):
    if phase == "start":
        _gather_phase("start", *comm)
    else:
        _gather_phase("forward", *comm)
        _gather_phase("finish", *comm)


def _all_gather(vs, name):
    n = len(vs)

    def body(*refs):
        comm = (refs[:n], refs[n:2 * n]) + tuple(refs[2 * n:])
        for phase in ("start", "forward", "finish"):
            _gather_phase(phase, *comm)

    return pl.pallas_call(
        body, name=name,
        out_shape=[jax.ShapeDtypeStruct((N_DEV,) + v.shape, v.dtype) for v in vs],
        in_specs=[HBM] * n, out_specs=[HBM] * n, scratch_shapes=_comm_scratch(n),
    )(*vs)


def _run_copies(phase, local, remote):
    if phase == "start":
        for cp in local + remote:
            cp.start()
    else:
        for cp in remote:
            cp.wait()
        for cp in local:
            cp.wait()


def _exchange_phase(phase, p_refs, out_refs, send_sems, recv_sems, local_sems):
    n = len(p_refs)
    x, y, c = _mesh_pos()
    me = 4 * x + 2 * y + c
    local = [pltpu.make_async_copy(p_refs[a].at[me], out_refs[a].at[me], local_sems.at[a]) for a in range(n)]
    remote = []
    for k in range(1, N_DEV):
        px = (1 - x) if k & 4 else x
        py = (1 - y) if k & 2 else y
        pc = (1 - c) if k & 1 else c
        for a in range(n):
            remote.append(pltpu.make_async_remote_copy(
                src_ref=p_refs[a].at[4 * px + 2 * py + pc], dst_ref=out_refs[a].at[me],
                send_sem=send_sems.at[a, k - 1], recv_sem=recv_sems.at[a, k - 1],
                device_id=(px, py, pc), device_id_type=MESH))
    _run_copies(phase, local, remote)


def _chip_exchange_phase(phase, p_refs, out_refs, send_sems, recv_sems, local_sems):
    n = len(p_refs)
    x, y, c = _mesh_pos()
    me = 2 * x + y
    local = [pltpu.make_async_copy(p_refs[a].at[me], out_refs[a].at[me], local_sems.at[a]) for a in range(n)]
    remote = []
    for k in range(1, 4):
        px = (1 - x) if k & 2 else x
        py = (1 - y) if k & 1 else y
        for a in range(n):
            remote.append(pltpu.make_async_remote_copy(
                src_ref=p_refs[a].at[2 * px + py], dst_ref=out_refs[a].at[me],
                send_sem=send_sems.at[a, k - 1], recv_sem=recv_sems.at[a, k - 1],
                device_id=(px, py, c), device_id_type=MESH))
    _run_copies(phase, local, remote)


def _sibling_exchange(parts, name):
    chips = N_DEV // 2

    def body(p_ref, out_ref, send_sems, recv_sems):
        x, y, c = _mesh_pos()
        copies = [pltpu.make_async_remote_copy(
            src_ref=p_ref.at[2 * q + 1 - c], dst_ref=out_ref.at[q], send_sem=send_sems.at[q], recv_sem=recv_sems.at[q],
            device_id=(x, y, 1 - c), device_id_type=MESH) for q in range(chips)]
        _run_copies("start", [], copies)
        _run_copies("finish", [], copies)

    return pl.pallas_call(
        body, name=name, out_shape=jax.ShapeDtypeStruct((chips,) + parts.shape[1:], parts.dtype),
        in_specs=[HBM], out_specs=HBM,
        scratch_shapes=[pltpu.SemaphoreType.DMA((chips,)), pltpu.SemaphoreType.DMA((chips,))],
    )(parts)


def _pair_sum(parts, theirs, name):
    slots, rows, cols = theirs.shape
    tc = 256
    core = lax.axis_index("c").astype(jnp.int32).reshape(1)

    def body(c_ref, a_ref, b_ref, o_ref):
        o_ref[...] = (a_ref[...].astype(F32) + b_ref[...].astype(F32)).astype(o_ref.dtype)

    spec = pl.BlockSpec((1, rows, tc), lambda q, j, c: (q, 0, j))
    return pl.pallas_call(
        body, name=name,
        grid_spec=pltpu.PrefetchScalarGridSpec(
            num_scalar_prefetch=1, grid=(slots, cols // tc),
            in_specs=[pl.BlockSpec((1, rows, tc), lambda q, j, c: (2 * q + c[0], 0, j)), spec], out_specs=spec),
        out_shape=jax.ShapeDtypeStruct(theirs.shape, theirs.dtype),
        compiler_params=_params(("parallel", "parallel")),
    )(core, parts, theirs)


def _sum_sources(recvs, name):
    k = len(recvs)

    def body(*refs):
        for r_ref, o_ref in zip(refs[:k], refs[k:]):
            acc = r_ref[0].astype(F32)
            for s in range(1, r_ref.shape[0]):
                acc = acc + r_ref[s].astype(F32)
            o_ref[...] = acc

    return pl.pallas_call(
        body, name=name, out_shape=[jax.ShapeDtypeStruct(r.shape[1:], F32) for r in recvs],
        compiler_params=_params(),
    )(*recvs)


def _row_tile(rows):
    for tile in range(128, 15, -16):
        if rows % tile == 0:
            return tile
    return rows


def _adam_update(w, g, m, v):
    nm = ADAM_B1 * m + (1.0 - ADAM_B1) * g
    nv = ADAM_B2 * v + (1.0 - ADAM_B2) * (g * g)
    m_hat = nm / (1.0 - ADAM_B1 ** ADAM_STEP)
    v_hat = nv / (1.0 - ADAM_B2 ** ADAM_STEP)
    return -ADAM_LR * (m_hat / (jnp.sqrt(v_hat) + ADAM_EPS) + ADAM_WD * w), nm, nv


def _vector_offsets(widths):
    offsets, end = [], 0
    for c in widths:
        offsets.append(end)
        end += c + (-c) % LANES
    return offsets, end


def _adamw_small(vec_parts, vec_state, mat_grads, mat_state):
    widths = [w.shape[1] for w, _, _ in vec_state]
    offsets, total = _vector_offsets(widths)
    assert vec_parts.shape == (N_DEV, total), (vec_parts.shape, total)
    n_vec, n_mat = len(vec_state), len(mat_state)

    def body(*refs):
        r_ref = refs[0]
        vec_in = refs[1:1 + 3 * n_vec]
        mat_in = refs[1 + 3 * n_vec:1 + 3 * n_vec + 4 * n_mat]
        outs = refs[1 + 3 * n_vec + 4 * n_mat:]
        for i, (off, c) in enumerate(zip(offsets, widths)):
            g = r_ref[0:1, off:off + c]
            for s in range(1, N_DEV):
                g = g + r_ref[s:s + 1, off:off + c]
            w_ref, m_ref, v_ref = vec_in[3 * i:3 * i + 3]
            g_out, d_out, m_out, v_out = outs[4 * i:4 * i + 4]
            g_out[...] = g
            d_out[...], m_out[...], v_out[...] = _adam_update(w_ref[...], g, m_ref[...], v_ref[...])
        for j in range(n_mat):
            g_ref, w_ref, m_ref, v_ref = mat_in[4 * j:4 * j + 4]
            d_out, m_out, v_out = outs[4 * n_vec + 3 * j:4 * n_vec + 3 * j + 3]
            d_out[...], m_out[...], v_out[...] = _adam_update(w_ref[...], g_ref[...], m_ref[...], v_ref[...])

    args = [vec_parts] + [a for state in vec_state for a in state]
    for g, state in zip(mat_grads, mat_state):
        args += [g, *state]
    out_shape = [jax.ShapeDtypeStruct(w.shape, F32) for w, _, _ in vec_state for _ in range(4)]
    out_shape += [jax.ShapeDtypeStruct(w.shape, F32) for w, _, _ in mat_state for _ in range(3)]
    out = pl.pallas_call(body, name="adamw_replicated", out_shape=out_shape, compiler_params=_params())(*args)
    vec_out = [tuple(out[4 * i:4 * i + 4]) for i in range(n_vec)]
    mat_out = [tuple(out[4 * n_vec + 3 * j:4 * n_vec + 3 * j + 3]) for j in range(n_mat)]
    return vec_out, mat_out


def _adamw(w, recv, m, v, name):
    rows, width = w.shape
    n = recv.shape[0]
    if rows % 16 == 0 or width % 256:
        tr, tc = _row_tile(rows), width
    else:
        tr, tc = rows, 256

    def body(w_ref, r_ref, m_ref, v_ref, g_ref, d_ref, nm_ref, nv_ref):
        gv = r_ref[0].astype(F32)
        for s in range(1, n):
            gv = gv + r_ref[s].astype(F32)
        g_ref[...] = gv
        d_ref[...], nm_ref[...], nv_ref[...] = _adam_update(w_ref[...], gv, m_ref[...], v_ref[...])

    spec = pl.BlockSpec((tr, tc), lambda i, j: (i, j))
    shape = jax.ShapeDtypeStruct((rows, width), F32)
    return pl.pallas_call(
        body, name=name, grid=(rows // tr, width // tc),
        in_specs=[spec, pl.BlockSpec((n, tr, tc), lambda i, j: (0, i, j)), spec, spec],
        out_specs=[spec] * 4, out_shape=[shape] * 4,
        compiler_params=_params(("parallel", "parallel")),
    )(w, recv, m, v)


BIG_NAMES = ("w_in", "w_out_ssm", "w_out_lru", "w_out", "w_ffn_in", "w_ffn_out", "ssm_conv_w", "lru_conv_w")
TRANSPOSED = ("w_in", "w_ffn_in")
CONV_NAMES = ("ssm_conv_w", "lru_conv_w")
MATMUL_NAMES = BIG_NAMES[:6]
NEEDED_FIRST = ("w_in", "ssm_conv_w", "lru_conv_w")
GATHERED_IN_PROJ = ("w_ffn_out",)
GATHERED_IN_SSD = ("w_ffn_in",)
GATHERED_IN_LRU = ("w_out_ssm", "w_out_lru", "w_out")
EXCHANGED_IN_SSD = ("w_ffn_in", "w_ffn_out")
EXCHANGED_IN_LRU = ("w_out_ssm", "w_out_lru", "w_out")
SMALL_VECTORS = ("norm1_w", "b_branch_gate", "ssm_conv_b", "ssm_dt_bias", "ssm_a_log", "ssm_d", "ssm_norm_w",
                 "lru_conv_b", "lru_b_r", "lru_b_i", "lru_lambda", "norm2_w", "norm_f_w")
SMALL_MATRICES = ("lru_w_r", "lru_w_i")


def _col_shards(full):
    rows, cols = full.shape
    return full.reshape(rows, N_DEV, cols // N_DEV).transpose(1, 0, 2)


def _from_col_shards(g):
    n, rows, w = g.shape
    return g.transpose(1, 0, 2).reshape(rows, n * w)


def kernel(x, norm1_w, w_in, b_branch_gate, ssm_conv_w, ssm_conv_b, ssm_dt_bias, ssm_a_log, ssm_d, ssm_norm_w, w_out_ssm, lru_conv_w, lru_conv_b, lru_w_r, lru_b_r, lru_w_i, lru_b_i, lru_lambda, w_out_lru, w_out, norm2_w, w_ffn_in, w_ffn_out, norm_f_w, loss_target, m_norm1_w, m_w_in, m_b_branch_gate, m_ssm_conv_w, m_ssm_conv_b, m_ssm_dt_bias, m_ssm_a_log, m_ssm_d, m_ssm_norm_w, m_w_out_ssm, m_lru_conv_w, m_lru_conv_b, m_lru_w_r, m_lru_b_r, m_lru_w_i, m_lru_b_i, m_lru_lambda, m_w_out_lru, m_w_out, m_norm2_w, m_w_ffn_in, m_w_ffn_out, m_norm_f_w, v_norm1_w, v_w_in, v_b_branch_gate, v_ssm_conv_w, v_ssm_conv_b, v_ssm_dt_bias, v_ssm_a_log, v_ssm_d, v_ssm_norm_w, v_w_out_ssm, v_lru_conv_w, v_lru_conv_b, v_lru_w_r, v_lru_b_r, v_lru_w_i, v_lru_b_i, v_lru_lambda, v_w_out_lru, v_w_out, v_norm2_w, v_w_ffn_in, v_w_ffn_out, v_norm_f_w):
    given = dict(locals())
    weights = {n: given[n] for n in BIG_NAMES + SMALL_VECTORS + SMALL_MATRICES}
    t = x.shape[1]
    xt = x[0]
    tgt = loss_target[0]

    def local(n, a):
        return a[0].T if n in TRANSPOSED else a[0]

    def as_output(n, a):
        return a.T[None] if n in TRANSPOSED else a[None]

    def shard(n):
        s = local(n, weights[n])
        return s.astype(BF16) if n in MATMUL_NAMES else s

    def unshard(n, g):
        return _from_col_shards(g) if n in CONV_NAMES else g.reshape(-1, g.shape[-1])

    def grad_slices(n):
        g = grads[n]
        return (_col_shards(g) if n in CONV_NAMES else g.reshape(N_DEV, -1, g.shape[-1])).astype(BF16)

    gathered = _all_gather([shard(n) for n in NEEDED_FIRST], "gather_in_weights")
    full = {n: unshard(n, g) for n, g in zip(NEEDED_FIRST, gathered)}
    ssm_cw, lru_cw = full["ssm_conv_w"], full["lru_conv_w"]
    wi_t = full["w_in"]
    lru_rows = wi_t[ORIG_LRU_X:].reshape(2, LRU_BLOCKS, LANES, D_MODEL).transpose(1, 0, 2, 3)
    w_tail = jnp.concatenate([wi_t[W_IN_MAIN:ORIG_DT], lru_rows.reshape(2 * LRU_WIDTH, D_MODEL),
                              wi_t[ORIG_DT:ORIG_LRU_X], jnp.zeros((PROJ_W - IN_PROJ, D_MODEL), BF16)], axis=0)

    def pad_heads(a):
        return jnp.pad(a.reshape(1, SSM_HEADS), ((0, 0), (0, LANES - SSM_HEADS)))

    dt_bias_p = pad_heads(ssm_dt_bias)
    a_log_p = pad_heads(ssm_a_log)
    d_exp = jnp.repeat(ssm_d.reshape(SSM_HEADS), SSM_HEAD_DIM).reshape(1, SSM_INNER)
    lru_wr, lru_wi = lru_w_r[0], lru_w_i[0]

    hn1 = _rmsnorm_fwd(xt, norm1_w, "norm1_fwd")
    later = [shard(n) for n in GATHERED_IN_PROJ]
    proj, (gathered,) = _mm(
        hn1, wi_t, tb=True, b_tail=w_tail, b_main=W_IN_MAIN, name="in_proj",
        hosted=[(_gather_two_phase, later, [jax.ShapeDtypeStruct((N_DEV,) + v.shape, v.dtype) for v in later])])
    full.update({n: unshard(n, g) for n, g in zip(GATHERED_IN_PROJ, gathered)})
    xbc_act = _ssm_conv_fwd(proj, ssm_cw, ssm_conv_b)
    (y_ssd, s_in_all), gathered = _ssd_fwd(xbc_act, proj, dt_bias_p, a_log_p, d_exp,
                                           gather=[shard(n) for n in GATHERED_IN_SSD])
    full.update({n: unshard(n, g) for n, g in zip(GATHERED_IN_SSD, gathered)})
    (l_out, h_lru), gathered = _lru_fwd(proj, lru_cw, lru_conv_b, lru_wr, lru_b_r, lru_wi, lru_b_i, lru_lambda,
                                        gather=[shard(n) for n in GATHERED_IN_LRU])
    full.update({n: unshard(n, g) for n, g in zip(GATHERED_IN_LRU, gathered)})
    y_pre = _ssm_post_fwd(y_ssd, proj, ssm_norm_w)
    y_ssm = _mm(y_pre, full["w_out_ssm"], out_dtype=BF16, name="out_ssm")
    y_lru = _mm(l_out, full["w_out_lru"], out_dtype=BF16, name="out_lru")
    merged = _merge_fwd(proj, b_branch_gate, y_ssm, y_lru)
    h1, hn2 = _mm(merged, full["w_out"], epilogue=_residual_norm_epilogue(xt, norm2_w), name="out_proj")
    gate, up, act = _ffn_in_swiglu(hn2, full["w_ffn_in"])

    grads = {}
    dh2, grads["norm_f_w"], loss_cols = _mm(
        act, full["w_ffn_out"], epilogue=_loss_epilogue(h1, norm_f_w.reshape(1, D_MODEL), tgt), name="ffn_out")
    loss = lax.psum(0.5 * jnp.sum(loss_cols) / D_MODEL, AXES)
    dact = _mm(dh2, full["w_ffn_out"], tb=True, out_dtype=BF16, name="d_act")
    grads["w_ffn_out"] = _mm(act, dh2, ta=True, out_dtype=BF16, name="dw_ffn_out")
    dgu = _swiglu_bwd(gate, up, dact)
    dh1, grads["norm2_w"] = _mm(dgu, full["w_ffn_in"], epilogue=_norm_bwd_epilogue(h1, norm2_w, dh2), name="d_hn2")
    grads["w_ffn_in"] = _mm(dgu, hn2, ta=True, out_dtype=BF16, name="dw_ffn_in")
    dmerged = _mm(dh1, full["w_out"], tb=True, out_dtype=BF16, name="d_merged")
    grads["w_out"] = _mm(merged, dh1, ta=True, out_dtype=BF16, name="dw_out")
    dy_ssm, dy_lru, dproj, grads["b_branch_gate"] = _merge_bwd(proj, b_branch_gate, y_ssm, y_lru, dmerged)
    dy_pre = _mm(dy_ssm, full["w_out_ssm"], tb=True, out_dtype=BF16, name="d_y_pre")
    grads["w_out_ssm"] = _mm(y_pre, dy_ssm, ta=True, out_dtype=BF16, name="dw_out_ssm")
    dl_out = _mm(dy_lru, full["w_out_lru"], tb=True, out_dtype=BF16, name="d_l_out")
    grads["w_out_lru"] = _mm(l_out, dy_lru, ta=True, out_dtype=BF16, name="dw_out_lru")
    dy_ssd, dproj, grads["ssm_norm_w"] = _ssm_post_bwd(y_ssd, proj, ssm_norm_w, dy_pre, dproj)
    (dxbc_act, dproj, dbias, dalog, ddcol), recv_in_ssd = _ssd_bwd(
        xbc_act, proj, s_in_all, dy_ssd, dt_bias_p, a_log_p, d_exp, dproj,
        exchange=[grad_slices(n) for n in EXCHANGED_IN_SSD])
    grads["ssm_dt_bias"] = dbias[:, :SSM_HEADS]
    grads["ssm_a_log"] = dalog[:, :SSM_HEADS]
    grads["ssm_d"] = ddcol.reshape(SSM_HEADS, SSM_HEAD_DIM).sum(axis=1).reshape(1, SSM_HEADS)
    dproj, grads["ssm_conv_w"], grads["ssm_conv_b"] = _ssm_conv_bwd(proj, ssm_cw, ssm_conv_b, dxbc_act, dproj)
    ((dproj, grads["lru_conv_w"], grads["lru_conv_b"], dwr, grads["lru_b_r"], dwi, grads["lru_b_i"],
      grads["lru_lambda"]), recv_in_lru) = _lru_bwd(
        proj, lru_cw, lru_conv_b, lru_wr, lru_b_r, lru_wi, lru_b_i, lru_lambda, h_lru, dl_out, dproj,
        exchange=[grad_slices(n) for n in EXCHANGED_IN_LRU])
    grads["lru_w_r"], grads["lru_w_i"] = dwr[None], dwi[None]
    dwpt = _mm(dproj, hn1, ta=True, out_dtype=BF16, name="dw_in")
    lru_rows = dwpt[COL_LRU:COL_DT].reshape(LRU_BLOCKS, 2, LANES, D_MODEL).transpose(1, 0, 2, 3)
    grads["w_in"] = jnp.concatenate([dwpt[:ORIG_DT], dwpt[COL_DT:COL_DT + SSM_HEADS],
                                     lru_rows.reshape(2 * LRU_WIDTH, D_MODEL)], axis=0)
    w_in_parts = grad_slices("w_in")
    from_sibling = _sibling_exchange(w_in_parts, "sibling_exchange_dw_in")
    chip_parts = [_pair_sum(w_in_parts, from_sibling, "pair_sum_dw_in")]
    direct = [grad_slices(n) for n in CONV_NAMES] + [grads[n].reshape(N_DEV, -1, LANES) for n in SMALL_MATRICES]
    (grad_x, grads["norm1_w"]), (recv_w_in, recv_direct) = _mm(
        dproj, wi_t, b_tail=w_tail, b_main=W_IN_MAIN, epilogue=_norm_bwd_epilogue(xt, norm1_w, dh1), name="d_hn1",
        hosted=[(_chip_exchange_phase, chip_parts, chip_parts), (_exchange_phase, direct, direct)])
    recv_first = list(recv_w_in) + list(recv_direct[:len(CONV_NAMES)])
    recv_mats = recv_direct[len(CONV_NAMES):]

    def as_rows(n, a):
        return a.reshape(1, -1) if n in SMALL_VECTORS else a.reshape(-1, LANES)

    vec_g = jnp.concatenate([jnp.pad(as_rows(n, grads[n]), ((0, 0), (0, (-grads[n].size) % LANES)))
                             for n in SMALL_VECTORS], axis=1)
    gathered = _all_gather([vec_g] + list(_sum_sources(recv_mats, "sum_gate_matrix_grads")), "gather_small_grads")
    vec_parts = gathered[0].reshape(N_DEV, -1)
    mat_g = [g.reshape(-1, LANES) for g in gathered[1:]]

    recv = dict(zip(EXCHANGED_IN_SSD + EXCHANGED_IN_LRU + NEEDED_FIRST,
                    list(recv_in_ssd) + list(recv_in_lru) + list(recv_first)))
    big_out = {n: _adamw(local(n, weights[n]), recv[n], local(n, given["m_" + n]), local(n, given["v_" + n]),
                         "adamw_" + n) for n in BIG_NAMES}

    def state(n):
        return tuple(as_rows(n, given[p + n]) for p in ("", "m_", "v_"))

    vec_out, mat_out = _adamw_small(vec_parts, [state(n) for n in SMALL_VECTORS],
                                    mat_g, [state(n) for n in SMALL_MATRICES])
    small_out = dict(zip(SMALL_VECTORS, vec_out))
    small_out.update({n: (g,) + out for n, g, out in zip(SMALL_MATRICES, mat_g, mat_out)})

    order = list(given)[1:24]
    results = []
    for q in range(4):
        vals = {n: as_output(n, big_out[n][q]) for n in BIG_NAMES}
        vals.update({n: out[q].reshape(weights[n].shape) for n, out in small_out.items()})
        results.extend(vals[n] for n in order)
    return (loss, grad_x[None], *results)
```

```python
import math

import jax
import jax.numpy as jnp
from jax import lax
from jax.experimental import pallas as pl
from jax.experimental.pallas import tpu as pltpu

F32 = jnp.float32
BF16 = jnp.bfloat16
HIGHEST = lax.Precision.HIGHEST
MESH = pl.DeviceIdType.MESH
AXES = ("x", "y", "c")
N_DEV = 8

D_MODEL = 1024
SSM_INNER = 2048
SSM_HEADS = 32
SSM_HEAD_DIM = 64
SSM_GROUPS = 4
SSM_STATE = 128
SSM_BC = SSM_GROUPS * SSM_STATE
SSM_CONV_DIM = SSM_INNER + 2 * SSM_BC
SSM_CHUNK = 128
SSM_PAIRS = SSM_HEADS // 2
CONV_K = 4
LRU_WIDTH = 1280
LRU_BLOCKS = 10
LRU_C = 8.0
FFN_HIDDEN = 2816
RMS_EPS = 1e-6
IN_PROJ = 9760

COL_GATES = 0
COL_Z = 2048
COL_XBC = 4096
COL_LRU = 7168
COL_DT = 9728
PROJ_W = 9856
ORIG_DT = 7168
ORIG_LRU_X = 7200
ORIG_LRU_Y = 8480
W_IN_MAIN = 7040

ADAM_LR = 0.001
ADAM_B1 = 0.9
ADAM_B2 = 0.999
ADAM_EPS = 1e-08
ADAM_WD = 0.01
ADAM_STEP = 10

LANES = 128
SUBLANES = 8
V7X_VMEM_BYTES = 64 * 1024 * 1024
VMEM_LIMIT = V7X_VMEM_BYTES * 3 // 4
VMEM_LIMIT_BIG = V7X_VMEM_BYTES * 15 // 16

NT = (((1,), (1,)), ((), ()))
TN = (((0,), (0,)), ((), ()))


def _params(sem=None, big=False):
    return pltpu.CompilerParams(dimension_semantics=sem,
                                vmem_limit_bytes=VMEM_LIMIT_BIG if big else VMEM_LIMIT)


def _blk(dim, cap):
    if dim <= cap:
        return dim
    for m in range(cap // LANES, 0, -1):
        if dim % (m * LANES) == 0:
            return m * LANES
    raise ValueError(f"no block for {dim}")


def _rows(t):
    return min(t, 256)


def _sigmoid(v):
    return 1.0 / (1.0 + jnp.exp(-v))


def _softplus(v):
    e = jnp.exp(-jnp.abs(v))
    u = 1.0 + e
    log1p = jnp.where(u == 1.0, e, jnp.log(u) * e / jnp.where(u == 1.0, 1.0, u - 1.0))
    return jnp.maximum(v, 0.0) + log1p


def _iota(shape, dim):
    return lax.broadcasted_iota(jnp.int32, shape, dim)


def _shift_down(v, s):
    if s == 0:
        return v
    return jnp.where(_iota(v.shape, 0) >= s, pltpu.roll(v, s, 0), 0.0)


def _shift_up(v, s):
    if s == 0:
        return v
    n = v.shape[0]
    return jnp.where(_iota(v.shape, 0) < n - s, pltpu.roll(v, n - s, 0), 0.0)


def _bdot(a, b, dn=None):
    a = a.astype(BF16)
    b = b.astype(BF16)
    if dn is None:
        return jnp.dot(a, b, preferred_element_type=F32)
    return lax.dot_general(a, b, dn, preferred_element_type=F32)


def _split_dot(a, e, dn=None, exact=True):
    hi = a.astype(BF16)
    if not exact:
        return _bdot(hi, e, dn)
    lo = (a - hi.astype(F32)).astype(BF16)
    return _bdot(hi, e, dn) + _bdot(lo, e, dn)


def _fdot(a, b, dn=None):
    if dn is None:
        return jnp.dot(a, b, precision=HIGHEST, preferred_element_type=F32)
    return lax.dot_general(a, b, dn, precision=HIGHEST, preferred_element_type=F32)


def _mm(a, b, *, ta=False, tb=False, add=None, hosted=(), out_dtype=F32, epilogue=None, b_tail=None, b_main=0, name):
    if ta:
        kdim, m = a.shape
    else:
        m, kdim = a.shape
    if tb:
        n, k2 = b.shape
    else:
        k2, n = b.shape
    if b_tail is not None:
        if tb:
            n = b_main + b_tail.shape[0]
        else:
            k2 = b_main + b_tail.shape[0]
    assert kdim == k2, (a.shape, b.shape, ta, tb)
    if epilogue is None:
        rows, vecs, row_dtypes, n_vec_out = ([] if add is None else [add]), [], [out_dtype], 0

        def finish(r, row_vals, vec_vals):
            return ((r + row_vals[0]) if row_vals else r,), ()
    else:
        assert add is None
        finish, rows, vecs, row_dtypes, n_vec_out = epilogue
    bm, bn, bk = _blk(m, 1408 if epilogue is None else 512), _blk(n, 1408), _blk(kdim, 1408)
    grid = (m // bm, n // bn, kdim // bk)
    nk = grid[2]
    assert n_vec_out == 0 or grid[1] == 1, "column sums are accumulated over the row tiles of whole rows"
    dn = (((0 if ta else 1,), (1 if tb else 0,)), ((), ()))
    n_ab = 2 if b_tail is None else 3
    main_blocks = b_main // (bn if tb else bk)
    assert b_main % (bn if tb else bk) == 0
    n_in = n_ab + len(rows) + len(vecs)
    n_out = len(row_dtypes) + n_vec_out
    sizes = [len(arrays) for _, arrays, _ in hosted]
    n_ex = sum(sizes)

    def body(*refs):
        a_ref, b_ref = refs[:2]
        row_refs, vec_refs = refs[n_ab:n_ab + len(rows)], refs[n_ab + len(rows):n_in]
        out_refs = refs[n_in + n_ex:n_in + n_ex + n_out]
        acc = refs[n_in + 2 * n_ex + n_out]
        comms, at = [], 0
        for g, size in enumerate(sizes):
            sems = refs[n_in + 2 * n_ex + n_out + 1 + 3 * g:n_in + 2 * n_ex + n_out + 4 + 3 * g]
            comms.append((refs[n_in + at:n_in + at + size],
                          refs[n_in + n_ex + n_out + at:n_in + n_ex + n_out + at + size]) + tuple(sems))
            at += size
        step = (pl.program_id(0) * grid[1] + pl.program_id(1)) * nk + pl.program_id(2)
        k = pl.program_id(2)
        if n_ex:
            @pl.when(step == 0)
            def _():
                for (phase_fn, _, _), comm in zip(hosted, comms):
                    phase_fn("start", *comm)

            @pl.when(step == (7 * grid[0] * grid[1] * nk) // 8)
            def _():
                for (phase_fn, _, _), comm in zip(hosted, comms):
                    phase_fn("late", *comm)

        @pl.when(k == 0)
        def _():
            acc[...] = jnp.zeros_like(acc)

        bv = b_ref[...]
        if b_tail is not None:
            bv = jnp.where(pl.program_id(1 if tb else 2) < main_blocks, bv, refs[2][...])
        acc[...] += lax.dot_general(a_ref[...].astype(BF16), bv.astype(BF16), dn, preferred_element_type=F32)

        @pl.when(k == nk - 1)
        def _():
            row_outs, col_sums = finish(acc[...], [r[...] for r in row_refs], [v[...] for v in vec_refs])
            for o_ref, val in zip(out_refs, row_outs):
                o_ref[...] = val.astype(o_ref.dtype)
            for o_ref, val in zip(out_refs[len(row_dtypes):], col_sums):
                @pl.when(pl.program_id(0) == 0)
                def _():
                    o_ref[...] = val

                @pl.when(pl.program_id(0) > 0)
                def _():
                    o_ref[...] += val

        if n_ex:
            @pl.when(step == grid[0] * grid[1] * nk - 1)
            def _():
                for (phase_fn, _, _), comm in zip(hosted, comms):
                    phase_fn("finish", *comm)

    a_spec = pl.BlockSpec((bk, bm), lambda i, j, k: (k, i)) if ta else pl.BlockSpec((bm, bk), lambda i, j, k: (i, k))
    b_specs = [pl.BlockSpec((bn, bk), lambda i, j, k: (j, k)) if tb else pl.BlockSpec((bk, bn), lambda i, j, k: (k, j))]
    if b_tail is not None:
        last = main_blocks - 1
        if tb:
            b_specs = [pl.BlockSpec((bn, bk), lambda i, j, k: (jnp.minimum(j, last), k)),
                       pl.BlockSpec((bn, bk), lambda i, j, k: (jnp.maximum(j - main_blocks, 0), k))]
        else:
            b_specs = [pl.BlockSpec((bk, bn), lambda i, j, k: (jnp.minimum(k, last), j)),
                       pl.BlockSpec((bk, bn), lambda i, j, k: (jnp.maximum(k - main_blocks, 0), j))]
    o_spec = pl.BlockSpec((bm, bn), lambda i, j, k: (i, j))
    v_spec = pl.BlockSpec((1, bn), lambda i, j, k: (0, j))
    in_specs = [a_spec] + b_specs + [o_spec] * len(rows) + [v_spec] * len(vecs) + [HBM] * n_ex
    args = ([a, b] + ([] if b_tail is None else [b_tail]) + list(rows) + list(vecs)
            + [p for _, arrays, _ in hosted for p in arrays])
    sequential = n_ex or n_vec_out
    out = pl.pallas_call(
        body, name=name, grid=grid,
        in_specs=in_specs, out_specs=[o_spec] * len(row_dtypes) + [v_spec] * n_vec_out + [HBM] * n_ex,
        out_shape=[jax.ShapeDtypeStruct((m, n), dt) for dt in row_dtypes]
        + [jax.ShapeDtypeStruct((1, n), F32)] * n_vec_out
        + [jax.ShapeDtypeStruct(r.shape, r.dtype) for _, _, results in hosted for r in results],
        scratch_shapes=[pltpu.VMEM((bm, bn), F32)] + [s for size in sizes for s in _comm_scratch(size)],
        compiler_params=_params(("arbitrary",) * 3 if sequential else ("parallel", "parallel", "arbitrary")),
    )(*args)
    result = out[0] if n_out == 1 else tuple(out[:n_out])
    if not n_ex:
        return result
    received, at = [], n_out
    for size in sizes:
        received.append(out[at:at + size])
        at += size
    return result, received


def _rmsnorm_fwd(x, w, name):
    t, d = x.shape
    tr = _rows(t)

    def body(x_ref, w_ref, o_ref):
        xv = x_ref[...]
        rstd = lax.rsqrt(jnp.mean(xv * xv, axis=-1, keepdims=True) + RMS_EPS)
        o_ref[...] = (xv * rstd * w_ref[...]).astype(BF16)

    return pl.pallas_call(
        body, name=name, grid=(t // tr,),
        in_specs=[pl.BlockSpec((tr, d), lambda i: (i, 0)), pl.BlockSpec((1, d), lambda i: (0, 0))],
        out_specs=pl.BlockSpec((tr, d), lambda i: (i, 0)),
        out_shape=jax.ShapeDtypeStruct((t, d), BF16),
        compiler_params=_params(("parallel",)),
    )(x, w)


def _normalize(h):
    rstd = lax.rsqrt(jnp.mean(h * h, axis=-1, keepdims=True) + RMS_EPS)
    return rstd, h * rstd


def _residual_norm_epilogue(x, w):
    def finish(r, rows, vecs):
        h = r + rows[0]
        return (h, _normalize(h)[1] * vecs[0]), ()

    return finish, [x], [w], [F32, BF16], 0


def _norm_bwd_epilogue(x, w, dres):
    def finish(r, rows, vecs):
        rstd, xhat = _normalize(rows[0])
        dxhat = r * vecs[0]
        m = jnp.mean(dxhat * xhat, axis=-1, keepdims=True)
        return (rstd * (dxhat - xhat * m) + rows[1],), (jnp.sum(r * xhat, axis=0, keepdims=True),)

    return finish, [x, dres], [w], [F32], 1


def _loss_epilogue(h1, w, tgt):
    d = h1.shape[1]

    def finish(r, rows, vecs):
        rstd, xhat = _normalize(r + rows[0])
        err = xhat * vecs[0] - rows[1]
        dyv = err * (1.0 / d)
        dxhat = dyv * vecs[0]
        m = jnp.mean(dxhat * xhat, axis=-1, keepdims=True)
        return ((rstd * (dxhat - xhat * m),),
                (jnp.sum(dyv * xhat, axis=0, keepdims=True), jnp.sum(err * err, axis=0, keepdims=True)))

    return finish, [h1, tgt], [w], [F32], 2


def _merge_fwd(proj, bg, ys, yl):
    t = proj.shape[0]
    d = D_MODEL
    tr = _rows(t)

    def body(ps_ref, pl_ref, bg_ref, ys_ref, yl_ref, o_ref):
        gs = _sigmoid(ps_ref[...] + bg_ref[:, 0:d])
        gl = _sigmoid(pl_ref[...] + bg_ref[:, d:2 * d])
        o_ref[...] = (gs * ys_ref[...] + gl * yl_ref[...]).astype(BF16)

    row = pl.BlockSpec((tr, d), lambda i: (i, 0))
    return pl.pallas_call(
        body, name="merge_fwd", grid=(t // tr,),
        in_specs=[row, pl.BlockSpec((tr, d), lambda i: (i, 1)), pl.BlockSpec((1, 2 * d), lambda i: (0, 0)), row, row],
        out_specs=row, out_shape=jax.ShapeDtypeStruct((t, d), BF16),
        compiler_params=_params(("parallel",)),
    )(proj, proj, bg, ys, yl)


def _merge_bwd(proj, bg, ys, yl, dm):
    t = proj.shape[0]
    d = D_MODEL
    tr = _rows(t)

    def body(ps_ref, pl_ref, bg_ref, ys_ref, yl_ref, dm_ref, dys_ref, dyl_ref, dg_ref, dbg_ref):
        i = pl.program_id(0)
        gs = _sigmoid(ps_ref[...] + bg_ref[:, 0:d])
        gl = _sigmoid(pl_ref[...] + bg_ref[:, d:2 * d])
        dmv = dm_ref[...]
        dys_ref[...] = (dmv * gs).astype(BF16)
        dyl_ref[...] = (dmv * gl).astype(BF16)
        dgs = dmv * ys_ref[...] * gs * (1.0 - gs)
        dgl = dmv * yl_ref[...] * gl * (1.0 - gl)
        dg_ref[:, 0:d] = dgs.astype(BF16)
        dg_ref[:, d:2 * d] = dgl.astype(BF16)

        @pl.when(i == 0)
        def _():
            dbg_ref[...] = jnp.zeros_like(dbg_ref)

        dbg_ref[:, 0:d] += jnp.sum(dgs, axis=0, keepdims=True)
        dbg_ref[:, d:2 * d] += jnp.sum(dgl, axis=0, keepdims=True)

    row = pl.BlockSpec((tr, d), lambda i: (i, 0))
    wide = pl.BlockSpec((tr, 2 * d), lambda i: (i, 0))
    vec = pl.BlockSpec((1, 2 * d), lambda i: (0, 0))
    return pl.pallas_call(
        body, name="merge_bwd", grid=(t // tr,),
        in_specs=[row, pl.BlockSpec((tr, d), lambda i: (i, 1)), vec, row, row, row],
        out_specs=[row, row, wide, vec],
        out_shape=[jax.ShapeDtypeStruct((t, d), BF16), jax.ShapeDtypeStruct((t, d), BF16),
                   jax.ShapeDtypeStruct((t, PROJ_W), BF16), jax.ShapeDtypeStruct((1, 2 * d), F32)],
        compiler_params=_params(("arbitrary",)),
    )(proj, proj, bg, ys, yl, dm)


def _ffn_in_swiglu(hn, wt):
    t, d = hn.shape
    f = FFN_HIDDEN
    bm, bn = _blk(t, 1024), _blk(f, 1408)
    nj = f // bn

    def body(a_ref, wg_ref, wu_ref, g_ref, u_ref, act_ref):
        a = a_ref[...]
        g = _bdot(a, wg_ref[...], NT)
        u = _bdot(a, wu_ref[...], NT)
        g_ref[...] = g.astype(BF16)
        u_ref[...] = u.astype(BF16)
        act_ref[...] = (g * _sigmoid(g) * u).astype(BF16)

    out = pl.BlockSpec((bm, bn), lambda i, j: (i, j))
    shape = jax.ShapeDtypeStruct((t, f), BF16)
    return pl.pallas_call(
        body, name="ffn_in_swiglu", grid=(t // bm, nj),
        in_specs=[pl.BlockSpec((bm, d), lambda i, j: (i, 0)), pl.BlockSpec((bn, d), lambda i, j: (j, 0)),
                  pl.BlockSpec((bn, d), lambda i, j: (nj + j, 0))],
        out_specs=[out, out, out], out_shape=[shape, shape, shape],
        compiler_params=_params(("parallel", "parallel")),
    )(hn, wt, wt)


def _swiglu_bwd(g_all, u_all, dact):
    t, f = g_all.shape
    tr = _rows(t)

    def body(g_ref, u_ref, da_ref, o_ref):
        g = g_ref[...].astype(F32)
        sg = _sigmoid(g)
        da = da_ref[...].astype(F32)
        o_ref[:, 0:f] = (da * u_ref[...].astype(F32) * (sg * (1.0 + g * (1.0 - sg)))).astype(BF16)
        o_ref[:, f:2 * f] = (da * g * sg).astype(BF16)

    row = pl.BlockSpec((tr, f), lambda i: (i, 0))
    return pl.pallas_call(
        body, name="swiglu_bwd", grid=(t // tr,),
        in_specs=[row, row, row],
        out_specs=pl.BlockSpec((tr, 2 * f), lambda i: (i, 0)),
        out_shape=jax.ShapeDtypeStruct((t, 2 * f), BF16),
        compiler_params=_params(("parallel",)),
    )(g_all, u_all, dact)


def _conv_pre(xv, wv, bv):
    pre = bv + wv[CONV_K - 1:CONV_K, :] * xv
    for k in range(CONV_K - 1):
        pre = pre + wv[k:k + 1, :] * _shift_down(xv, CONV_K - 1 - k)
    return pre


def _ssm_conv_fwd(proj, w, b):
    t = proj.shape[0]
    nb = SSM_CONV_DIM // LANES
    c0 = COL_XBC // LANES

    def body(x_ref, w_ref, b_ref, o_ref):
        pre = _conv_pre(x_ref[...], w_ref[...], b_ref[...])
        o_ref[...] = (pre * _sigmoid(pre)).astype(BF16)

    return pl.pallas_call(
        body, name="ssm_conv_fwd", grid=(nb,),
        in_specs=[pl.BlockSpec((t, LANES), lambda j: (0, c0 + j)), pl.BlockSpec((CONV_K, LANES), lambda j: (0, j)),
                  pl.BlockSpec((1, LANES), lambda j: (0, j))],
        out_specs=pl.BlockSpec((t, LANES), lambda j: (0, j)),
        out_shape=jax.ShapeDtypeStruct((t, SSM_CONV_DIM), BF16),
        compiler_params=_params(("parallel",)),
    )(proj, w, b)


def _ssm_conv_bwd(proj, w, b, dact, dproj):
    t = proj.shape[0]
    nb = SSM_CONV_DIM // LANES
    c0 = COL_XBC // LANES

    def body(x_ref, w_ref, b_ref, da_ref, dproj_in, dx_ref, dw_ref, db_ref):
        xv = x_ref[...]
        wv = w_ref[...]
        pre = _conv_pre(xv, wv, b_ref[...])
        sg = _sigmoid(pre)
        dpre = da_ref[...] * (sg * (1.0 + pre * (1.0 - sg)))
        dx = wv[CONV_K - 1:CONV_K, :] * dpre
        for k in range(CONV_K - 1):
            dx = dx + wv[k:k + 1, :] * _shift_up(dpre, CONV_K - 1 - k)
        dx_ref[...] = dx.astype(BF16)
        for k in range(CONV_K):
            dw_ref[k:k + 1, :] = jnp.sum(dpre * _shift_down(xv, CONV_K - 1 - k), axis=0, keepdims=True)
        db_ref[...] = jnp.sum(dpre, axis=0, keepdims=True)

    col = pl.BlockSpec((t, LANES), lambda j: (0, j))
    wsp = pl.BlockSpec((CONV_K, LANES), lambda j: (0, j))
    bsp = pl.BlockSpec((1, LANES), lambda j: (0, j))
    return pl.pallas_call(
        body, name="ssm_conv_bwd", grid=(nb,),
        in_specs=[pl.BlockSpec((t, LANES), lambda j: (0, c0 + j)), wsp, bsp, col, HBM],
        out_specs=[pl.BlockSpec((t, LANES), lambda j: (0, c0 + j)), wsp, bsp],
        out_shape=[jax.ShapeDtypeStruct(dproj.shape, dproj.dtype), jax.ShapeDtypeStruct((CONV_K, SSM_CONV_DIM), F32),
                   jax.ShapeDtypeStruct((1, SSM_CONV_DIM), F32)],
        input_output_aliases={4: 0},
        compiler_params=_params(("parallel",)),
    )(proj, w, b, dact, dproj)


def _ssd_chunk_terms(dtr, bias, alog):
    a = -jnp.exp(alog)
    dt = _softplus(dtr + bias)
    row = _iota((SSM_CHUNK, SSM_CHUNK), 0)
    col = _iota((SSM_CHUNK, SSM_CHUNK), 1)
    tri = (row >= col).astype(F32)
    cs = _fdot(tri, dt * a)
    dec = jnp.exp(cs[SSM_CHUNK - 1:SSM_CHUNK, :] - cs)
    ecs = jnp.exp(cs)
    off = _iota((LANES, SSM_INNER), 1) - SSM_HEAD_DIM * _iota((LANES, SSM_INNER), 0)
    expand = jnp.where(jnp.logical_and(off >= 0, off < SSM_HEAD_DIM), 1.0, 0.0).astype(BF16)
    return a, dt, cs, dec, ecs, expand, row, col


def _ssd_specs(t):
    nc = t // SSM_CHUNK
    xs = pl.BlockSpec((SSM_CHUNK, SSM_INNER), lambda c: (c, 0))
    bm = pl.BlockSpec((SSM_CHUNK, SSM_BC), lambda c: (c, SSM_INNER // SSM_BC))
    cm = pl.BlockSpec((SSM_CHUNK, SSM_BC), lambda c: (c, SSM_INNER // SSM_BC + 1))
    dtr = pl.BlockSpec((SSM_CHUNK, LANES), lambda c: (c, COL_DT // LANES))
    vec = pl.BlockSpec((1, LANES), lambda c: (0, 0))
    wide = pl.BlockSpec((1, SSM_INNER), lambda c: (0, 0))
    return nc, xs, bm, cm, dtr, vec, wide


def _ssd_fwd(xbc_act, proj, bias, alog, dexp, gather):
    t = proj.shape[0]
    nc, xs_s, bm_s, cm_s, dtr_s, vec, wide = _ssd_specs(t)
    n = len(gather)

    def body(*refs):
        xs_ref, b_ref, c_ref, dtr_ref, bias_ref, alog_ref, dexp_ref = refs[:7]
        y_ref, sin_ref = refs[7 + n:9 + n]
        state = refs[9 + 2 * n]
        comm = (refs[7:7 + n], refs[9 + n:9 + 2 * n]) + tuple(refs[10 + 2 * n:])
        chunk = pl.program_id(0)

        @pl.when(chunk == 0)
        def _():
            _gather_phase("start", *comm)
            state[...] = jnp.zeros_like(state)

        @pl.when(chunk == (3 * nc) // 4)
        def _():
            _gather_phase("forward", *comm)

        a, dt, cs, dec, ecs, expand, row, col = _ssd_chunk_terms(dtr_ref[...], bias_ref[...], alog_ref[...])
        cst = cs.T
        dt_x = _split_dot(dt, expand, exact=False)
        dec_x = _split_dot(dec, expand, exact=False)
        ecs_x = _split_dot(ecs, expand)
        xs = xs_ref[...].astype(F32)
        xdt = xs * dt_x
        xdec = xdt * dec_x
        lane_lo = col < SSM_HEAD_DIM
        causal = row >= col
        sin_ref[0] = state[...]
        for g in range(SSM_GROUPS):
            bg = b_ref[:, g * SSM_STATE:(g + 1) * SSM_STATE].astype(BF16)
            cg = c_ref[:, g * SSM_STATE:(g + 1) * SSM_STATE].astype(BF16)
            cb = _bdot(cg, bg, NT)
            for q in range(SSM_PAIRS // SSM_GROUPS):
                pq = g * (SSM_PAIRS // SSM_GROUPS) + q
                sl = slice(pq * LANES, (pq + 1) * LANES)
                xp = xdt[:, sl].astype(BF16)
                yd = []
                for hh in range(2):
                    h = 2 * pq + hh
                    lmat = jnp.exp(jnp.where(causal, cs[:, h:h + 1] - cst[h:h + 1, :], -jnp.inf))
                    yd.append(_bdot(cb * lmat, xp))
                s_in = state[pq]
                y_off = _bdot(cg, s_in) * ecs_x[:, sl]
                y_ref[:, sl] = (jnp.where(lane_lo, yd[0], yd[1]) + y_off + xs[:, sl] * dexp_ref[:, sl]).astype(BF16)
                state[pq] = s_in * ecs_x[SSM_CHUNK - 1:SSM_CHUNK, sl] + _bdot(bg, xdec[:, sl], TN)

        @pl.when(chunk == nc - 1)
        def _():
            _gather_phase("finish", *comm)

    out = pl.pallas_call(
        body, name="ssd_fwd", grid=(nc,),
        in_specs=[xs_s, bm_s, cm_s, dtr_s, vec, vec, wide] + [HBM] * n,
        out_specs=[pl.BlockSpec((SSM_CHUNK, SSM_INNER), lambda c: (c, 0)),
                   pl.BlockSpec((1, SSM_PAIRS, SSM_STATE, LANES), lambda c: (c, 0, 0, 0))] + [HBM] * n,
        out_shape=[jax.ShapeDtypeStruct((t, SSM_INNER), BF16),
                   jax.ShapeDtypeStruct((nc, SSM_PAIRS, SSM_STATE, LANES), F32)]
        + [jax.ShapeDtypeStruct((N_DEV,) + v.shape, v.dtype) for v in gather],
        scratch_shapes=[pltpu.VMEM((SSM_PAIRS, SSM_STATE, LANES), F32)] + _comm_scratch(n),
        compiler_params=_params(("arbitrary",)),
    )(xbc_act, xbc_act, xbc_act, proj, bias, alog, dexp, *gather)
    return out[:2], out[2:]


def _ssd_bwd(xbc_act, proj, s_in_all, dy, bias, alog, dexp, dproj, exchange):
    n_ex = len(exchange)
    t = proj.shape[0]
    nc = t // SSM_CHUNK
    last = nc - 1
    xs_s = pl.BlockSpec((SSM_CHUNK, SSM_INNER), lambda c: (last - c, 0))
    bm_s = pl.BlockSpec((SSM_CHUNK, SSM_BC), lambda c: (last - c, SSM_INNER // SSM_BC))
    cm_s = pl.BlockSpec((SSM_CHUNK, SSM_BC), lambda c: (last - c, SSM_INNER // SSM_BC + 1))
    dtr_s = pl.BlockSpec((SSM_CHUNK, LANES), lambda c: (last - c, COL_DT // LANES))
    sin_s = pl.BlockSpec((1, SSM_PAIRS, SSM_STATE, LANES), lambda c: (last - c, 0, 0, 0))
    vec = pl.BlockSpec((1, LANES), lambda c: (0, 0))
    wide = pl.BlockSpec((1, SSM_INNER), lambda c: (0, 0))

    def body(*refs):
        xs_ref, b_ref, c_ref, dtr_ref, sin_ref, dy_ref, bias_ref, alog_ref, dexp_ref = refs[:9]
        dxbc_ref, ddtr_ref, dbias_ref, dalog_ref, ddcol_ref = refs[10 + n_ex:15 + n_ex]
        dstate, dxdt_s, yoff_s, rx_s, trow_s = refs[15 + 2 * n_ex:20 + 2 * n_ex]
        comm = (refs[10:10 + n_ex], refs[15 + n_ex:15 + 2 * n_ex]) + tuple(refs[20 + 2 * n_ex:])

        @pl.when(pl.program_id(0) == 0)
        def _():
            _exchange_phase("start", *comm)
            dstate[...] = jnp.zeros_like(dstate)
            trow_s[...] = jnp.zeros_like(trow_s)
            dbias_ref[...] = jnp.zeros_like(dbias_ref)
            dalog_ref[...] = jnp.zeros_like(dalog_ref)
            ddcol_ref[...] = jnp.zeros_like(ddcol_ref)

        dtr = dtr_ref[...]
        a, dt, cs, dec, ecs, expand, row, col = _ssd_chunk_terms(dtr, bias_ref[...], alog_ref[...])
        cst = cs.T
        dt_x = _split_dot(dt, expand, exact=False)
        dec_x = _split_dot(dec, expand, exact=False)
        ecs_x = _split_dot(ecs, expand)
        xs = xs_ref[...].astype(F32)
        dyv = dy_ref[...].astype(F32)
        xdt = xs * dt_x
        lane_lo = col < SSM_HEAD_DIM
        causal = row >= col
        ddcol_ref[...] += jnp.sum(dyv * xs, axis=0, keepdims=True)
        dcs_col = jnp.zeros((SSM_CHUNK, LANES), F32)
        dcs_row = jnp.zeros((LANES, SSM_CHUNK), F32)
        for g in range(SSM_GROUPS):
            bg = b_ref[:, g * SSM_STATE:(g + 1) * SSM_STATE].astype(BF16)
            cg = c_ref[:, g * SSM_STATE:(g + 1) * SSM_STATE].astype(BF16)
            cb = _bdot(cg, bg, NT)
            dgm = jnp.zeros((SSM_CHUNK, SSM_CHUNK), F32)
            dbg = jnp.zeros((SSM_CHUNK, SSM_STATE), F32)
            dcg = jnp.zeros((SSM_CHUNK, SSM_STATE), F32)
            for q in range(SSM_PAIRS // SSM_GROUPS):
                pq = g * (SSM_PAIRS // SSM_GROUPS) + q
                sl = slice(pq * LANES, (pq + 1) * LANES)
                dyp = dyv[:, sl]
                xp = xdt[:, sl]
                dxh = []
                for hh in range(2):
                    h = 2 * pq + hh
                    lmat = jnp.exp(jnp.where(causal, cs[:, h:h + 1] - cst[h:h + 1, :], -jnp.inf))
                    mmat = cb * lmat
                    dyh = jnp.where(lane_lo if hh == 0 else jnp.logical_not(lane_lo), dyp, 0.0)
                    dmm = _bdot(dyh, xp, NT)
                    pm = dmm * mmat
                    dcs_col = jnp.where(col == h, jnp.sum(pm, axis=1, keepdims=True), dcs_col)
                    dcs_row = jnp.where(row == h, jnp.sum(pm, axis=0, keepdims=True), dcs_row)
                    dgm = dgm + dmm * lmat
                    dxh.append(_bdot(mmat, dyp, TN))
                s_in = sin_ref[0, pq]
                ecs_p = ecs_x[:, sl]
                dec_p = dec_x[:, sl]
                etot_p = ecs_x[SSM_CHUNK - 1:SSM_CHUNK, sl]
                yoff_s[:, sl] = dyp * (_bdot(cg, s_in) * ecs_p)
                dq = dyp * ecs_p
                dcg = dcg + _bdot(dq, s_in, NT)
                ds = dstate[pq]
                r = _bdot(bg, ds)
                rx_s[:, sl] = r * xp
                dxdt_s[:, sl] = jnp.where(lane_lo, dxh[0], dxh[1]) + dec_p * r
                dbg = dbg + _bdot(xp * dec_p, ds, NT)
                trow_s[0:1, sl] = jnp.sum(ds * s_in, axis=0, keepdims=True) * etot_p
                dstate[pq] = etot_p * ds + _bdot(cg, dq, TN)
            dcg = dcg + _bdot(dgm, bg)
            dbg = dbg + _bdot(dgm, cg, TN)
            dxbc_ref[:, SSM_INNER + g * SSM_STATE:SSM_INNER + (g + 1) * SSM_STATE] = dbg.astype(BF16)
            dxbc_ref[:, SSM_INNER + SSM_BC + g * SSM_STATE:SSM_INNER + SSM_BC + (g + 1) * SSM_STATE] = dcg.astype(BF16)
        ddec = _split_dot(rx_s[...], expand, NT, exact=False) * dec
        dtot = _split_dot(trow_s[...], expand, NT, exact=False)[0:1, :]
        dcs = dcs_col - dcs_row.T + _split_dot(yoff_s[...], expand, NT, exact=False) - ddec
        dcs = dcs + jnp.where(row == SSM_CHUNK - 1, jnp.sum(ddec, axis=0, keepdims=True) + dtot, 0.0)
        da = _fdot((row <= col).astype(F32), dcs)
        dxdt = dxdt_s[...]
        ddt = da * a + _split_dot(dxdt * xs, expand, NT, exact=False)
        dalog_ref[...] += jnp.sum(da * dt, axis=0, keepdims=True) * a
        ddtr = ddt * _sigmoid(dtr + bias_ref[...])
        ddtr_ref[...] = ddtr.astype(BF16)
        dbias_ref[...] += jnp.sum(ddtr, axis=0, keepdims=True)
        dxbc_ref[:, 0:SSM_INNER] = (dxdt * dt_x + dyv * dexp_ref[...]).astype(BF16)

        @pl.when(pl.program_id(0) == last)
        def _():
            _exchange_phase("finish", *comm)

    out = pl.pallas_call(
        body, name="ssd_bwd", grid=(nc,),
        in_specs=[xs_s, bm_s, cm_s, dtr_s, sin_s, pl.BlockSpec((SSM_CHUNK, SSM_INNER), lambda c: (last - c, 0)),
                  vec, vec, wide, HBM] + [HBM] * n_ex,
        out_specs=[pl.BlockSpec((SSM_CHUNK, SSM_CONV_DIM), lambda c: (last - c, 0)), dtr_s, vec, vec, wide]
        + [HBM] * n_ex,
        out_shape=[jax.ShapeDtypeStruct((t, SSM_CONV_DIM), BF16), jax.ShapeDtypeStruct(dproj.shape, dproj.dtype),
                   jax.ShapeDtypeStruct((1, LANES), F32), jax.ShapeDtypeStruct((1, LANES), F32),
                   jax.ShapeDtypeStruct((1, SSM_INNER), F32)]
        + [jax.ShapeDtypeStruct(p.shape, p.dtype) for p in exchange],
        input_output_aliases={9: 1},
        scratch_shapes=[pltpu.VMEM((SSM_PAIRS, SSM_STATE, LANES), F32),
                        pltpu.VMEM((SSM_CHUNK, SSM_INNER), F32), pltpu.VMEM((SSM_CHUNK, SSM_INNER), F32),
                        pltpu.VMEM((SSM_CHUNK, SSM_INNER), F32), pltpu.VMEM((SUBLANES, SSM_INNER), F32)]
        + _comm_scratch(n_ex),
        compiler_params=_params(("arbitrary",)),
    )(xbc_act, xbc_act, xbc_act, proj, s_in_all, dy, bias, alog, dexp, dproj, *exchange)
    return out[:5], out[5:]


def _group_rstd(y):
    n = SSM_INNER // SSM_GROUPS
    parts = []
    for g in range(SSM_GROUPS):
        yg = y[:, g * n:(g + 1) * n]
        r = lax.rsqrt(jnp.mean(yg * yg, axis=-1, keepdims=True) + RMS_EPS)
        parts.append(jnp.broadcast_to(r, yg.shape))
    return jnp.concatenate(parts, axis=1)


def _group_mean(v):
    n = SSM_INNER // SSM_GROUPS
    parts = []
    for g in range(SSM_GROUPS):
        vg = v[:, g * n:(g + 1) * n]
        parts.append(jnp.broadcast_to(jnp.mean(vg, axis=-1, keepdims=True), vg.shape))
    return jnp.concatenate(parts, axis=1)


def _ssm_post_fwd(y_ssd, proj, nw):
    t = proj.shape[0]
    n = SSM_INNER
    tr = _rows(t)

    def body(y_ref, z_ref, nw_ref, o_ref):
        z = z_ref[...]
        y = y_ref[...] * (z * _sigmoid(z))
        o_ref[...] = (y * _group_rstd(y) * nw_ref[...]).astype(BF16)

    row = pl.BlockSpec((tr, n), lambda i: (i, 0))
    return pl.pallas_call(
        body, name="ssm_post_fwd", grid=(t // tr,),
        in_specs=[row, pl.BlockSpec((tr, n), lambda i: (i, COL_Z // n)), pl.BlockSpec((1, n), lambda i: (0, 0))],
        out_specs=row, out_shape=jax.ShapeDtypeStruct((t, n), BF16),
        compiler_params=_params(("parallel",)),
    )(y_ssd, proj, nw)


def _ssm_post_bwd(y_ssd, proj, nw, dout, dproj):
    t = proj.shape[0]
    n = SSM_INNER
    tr = _rows(t)

    def body(y_ref, z_ref, nw_ref, do_ref, dproj_in, dy_ref, dz_ref, dnw_ref):
        i = pl.program_id(0)
        z = z_ref[...]
        sg = _sigmoid(z)
        sz = z * sg
        ys = y_ref[...]
        y = ys * sz
        rstd = _group_rstd(y)
        yn = y * rstd
        dov = do_ref[...]
        dyn = dov * nw_ref[...]
        dyg = rstd * (dyn - yn * _group_mean(dyn * yn))
        dy_ref[...] = (dyg * sz).astype(BF16)
        dz_ref[...] = (dyg * ys * (sg * (1.0 + z * (1.0 - sg)))).astype(BF16)
        part = jnp.sum(dov * yn, axis=0, keepdims=True)

        @pl.when(i == 0)
        def _():
            dnw_ref[...] = part

        @pl.when(i > 0)
        def _():
            dnw_ref[...] += part

    row = pl.BlockSpec((tr, n), lambda i: (i, 0))
    vec = pl.BlockSpec((1, n), lambda i: (0, 0))
    return pl.pallas_call(
        body, name="ssm_post_bwd", grid=(t // tr,),
        in_specs=[row, pl.BlockSpec((tr, n), lambda i: (i, COL_Z // n)), vec, row, HBM],
        out_specs=[row, pl.BlockSpec((tr, n), lambda i: (i, COL_Z // n)), vec],
        out_shape=[jax.ShapeDtypeStruct((t, n), BF16), jax.ShapeDtypeStruct(dproj.shape, dproj.dtype),
                   jax.ShapeDtypeStruct((1, n), F32)],
        input_output_aliases={4: 1},
        compiler_params=_params(("arbitrary",)),
    )(y_ssd, proj, nw, dout, dproj)


SCAN_UNROLL = 8
GELU_C = math.sqrt(2.0 / math.pi)
GELU_K = 0.044715


def _gelu_parts(y):
    th = jnp.tanh(GELU_C * (y + GELU_K * y * y * y))
    val = 0.5 * y * (1.0 + th)
    grad = 0.5 * (1.0 + th) + 0.5 * y * (1.0 - th * th) * GELU_C * (1.0 + 3.0 * GELU_K * y * y)
    return val, grad


def _scan_tiles(a_ref, b_ref, h_ref, n_rows, reverse):
    n_tiles = n_rows // SUBLANES
    shape = (SUBLANES, a_ref.shape[1])
    row = _iota(shape, 0)

    def in_tile(av, bv):
        for s in (1, 2, 4):
            if reverse:
                keep = row < SUBLANES - s
                a_sh = jnp.where(keep, pltpu.roll(av, SUBLANES - s, 0), 1.0)
                b_sh = jnp.where(keep, pltpu.roll(bv, SUBLANES - s, 0), 0.0)
            else:
                keep = row >= s
                a_sh = jnp.where(keep, pltpu.roll(av, s, 0), 1.0)
                b_sh = jnp.where(keep, pltpu.roll(bv, s, 0), 0.0)
            bv = av * b_sh + bv
            av = av * a_sh
        return av, bv

    def step(k, carry):
        first = (n_tiles // SCAN_UNROLL - 1 - k) if reverse else k
        tiles = [first * SCAN_UNROLL + j for j in range(SCAN_UNROLL)]
        if reverse:
            tiles = tiles[::-1]
        ats = [pl.ds(pl.multiple_of(tile * SUBLANES, SUBLANES), SUBLANES) for tile in tiles]
        scanned = [in_tile(a_ref[at, :], b_ref[at, :]) for at in ats]
        for at, (av, bv) in zip(ats, scanned):
            hv = bv + av * carry
            h_ref[at, :] = hv
            carry = hv[0:1, :] if reverse else hv[SUBLANES - 1:SUBLANES, :]
        return carry

    assert n_tiles % SCAN_UNROLL == 0, n_rows
    lax.fori_loop(0, n_tiles // SCAN_UNROLL, step, jnp.zeros((1, a_ref.shape[1]), F32))


def _lru_gates(xl, cw, cb, wr, br, wi, bi, lam):
    u = cb + cw[CONV_K - 1:CONV_K, :] * xl
    for k in range(CONV_K - 1):
        u = u + cw[k:k + 1, :] * _shift_down(xl, CONV_K - 1 - k)
    r = _sigmoid(_bdot(u, wr) + br)
    i = _sigmoid(_bdot(u, wi) + bi)
    sp = _softplus(-lam)
    la = -LRU_C * r * sp
    a = jnp.exp(la)
    mult = jnp.sqrt(-jnp.tanh(la) * (a * a + 1.0))
    return u, r, i, sp, a, mult


def _lru_specs(t):
    c0 = COL_LRU // LANES
    xl = pl.BlockSpec((t, LANES), lambda j: (0, c0 + 2 * j))
    yl = pl.BlockSpec((t, LANES), lambda j: (0, c0 + 2 * j + 1))
    col = pl.BlockSpec((t, LANES), lambda j: (0, j))
    cw = pl.BlockSpec((CONV_K, LANES), lambda j: (0, j))
    vec = pl.BlockSpec((1, LANES), lambda j: (0, j))
    wblk = pl.BlockSpec((1, LANES, LANES), lambda j: (j, 0, 0))
    return xl, yl, col, cw, vec, wblk


def _lru_fwd(proj, cw, cb, wr, br, wi, bi, lam, gather):
    t = proj.shape[0]
    xl_s, yl_s, col, cw_s, vec, wblk = _lru_specs(t)
    n = len(gather)

    def body(*refs):
        xl_ref, yl_ref, cw_ref, cb_ref, wr_ref, br_ref, wi_ref, bi_ref, lam_ref = refs[:9]
        o_ref, h_ref = refs[9 + n:11 + n]
        a_s, b_s = refs[11 + 2 * n:13 + 2 * n]
        comm = (refs[9:9 + n], refs[11 + n:11 + 2 * n]) + tuple(refs[13 + 2 * n:])
        j = pl.program_id(0)
        for step, phase in ((0, "start"), (LRU_BLOCKS - 2, "forward")):
            @pl.when(j == step)
            def _():
                _gather_phase(phase, *comm)

        u, r, i, sp, a, mult = _lru_gates(xl_ref[...], cw_ref[...], cb_ref[...], wr_ref[0], br_ref[...],
                                          wi_ref[0], bi_ref[...], lam_ref[...])
        a_s[...] = a
        b_s[...] = mult * (i * u)
        _scan_tiles(a_s, b_s, h_ref, t, reverse=False)
        o_ref[...] = (h_ref[...] * _gelu_parts(yl_ref[...])[0]).astype(BF16)

        @pl.when(j == LRU_BLOCKS - 1)
        def _():
            _gather_phase("finish", *comm)

    out = pl.pallas_call(
        body, name="lru_fwd", grid=(LRU_BLOCKS,),
        in_specs=[xl_s, yl_s, cw_s, vec, wblk, vec, wblk, vec, vec] + [HBM] * n,
        out_specs=[col, col] + [HBM] * n,
        out_shape=[jax.ShapeDtypeStruct((t, LRU_WIDTH), BF16), jax.ShapeDtypeStruct((t, LRU_WIDTH), F32)]
        + [jax.ShapeDtypeStruct((N_DEV,) + v.shape, v.dtype) for v in gather],
        scratch_shapes=[pltpu.VMEM((t, LANES), F32)] * 2 + _comm_scratch(n),
        compiler_params=_params(("arbitrary",), big=True),
    )(proj, proj, cw, cb, wr, br, wi, bi, lam, *gather)
    return out[:2], out[2:]


def _lru_bwd(proj, cw, cb, wr, br, wi, bi, lam, h_all, dout, dproj, exchange):
    t = proj.shape[0]
    xl_s, yl_s, col, cw_s, vec, wblk = _lru_specs(t)
    pair = pl.BlockSpec((t, 2 * LANES), lambda j: (0, COL_LRU // (2 * LANES) + j))
    n_ex = len(exchange)

    def body(*refs):
        xl_ref, yl_ref, cw_ref, cb_ref, wr_ref, br_ref, wi_ref, bi_ref, lam_ref, h_ref, do_ref = refs[:11]
        dxy_ref, dcw_ref, dcb_ref, dwr_ref, dbr_ref, dwi_ref, dbi_ref, dlam_ref = refs[12 + n_ex:20 + n_ex]
        a_s, b_s, g_s = refs[20 + 2 * n_ex:23 + 2 * n_ex]
        comm = (refs[12:12 + n_ex], refs[20 + n_ex:20 + 2 * n_ex]) + tuple(refs[23 + 2 * n_ex:])

        @pl.when(pl.program_id(0) == 0)
        def _():
            _exchange_phase("start", *comm)

        xl = xl_ref[...]
        cwv = cw_ref[...]
        lam = lam_ref[...]
        u, r, i, sp, a, mult = _lru_gates(xl, cwv, cb_ref[...], wr_ref[0], br_ref[...], wi_ref[0], bi_ref[...], lam)
        v = i * u
        gl, dgl = _gelu_parts(yl_ref[...])
        dov = do_ref[...]
        h = h_ref[...]
        dxy_ref[:, LANES:2 * LANES] = (dov * h * dgl).astype(BF16)
        b_s[...] = dov * gl
        a_s[...] = _shift_up(a, 1)
        _scan_tiles(a_s, b_s, g_s, t, reverse=True)
        g = g_s[...]
        da = g * _shift_down(h, 1)
        dmult = g * v
        dv = g * mult
        dla = da * a - dmult * (a * a) / mult
        dr = dla * (-LRU_C * sp)
        dsp = jnp.sum(dla * (-LRU_C * r), axis=0, keepdims=True)
        dlam_ref[...] = -dsp * _sigmoid(-lam)
        dpr = dr * r * (1.0 - r)
        dpi = dv * u * i * (1.0 - i)
        dbr_ref[...] = jnp.sum(dpr, axis=0, keepdims=True)
        dbi_ref[...] = jnp.sum(dpi, axis=0, keepdims=True)
        dwr_ref[0] = _bdot(u, dpr, TN)
        dwi_ref[0] = _bdot(u, dpi, TN)
        du = dv * i + _bdot(dpr, wr_ref[0], NT) + _bdot(dpi, wi_ref[0], NT)
        dxl = cwv[CONV_K - 1:CONV_K, :] * du
        for k in range(CONV_K - 1):
            dxl = dxl + cwv[k:k + 1, :] * _shift_up(du, CONV_K - 1 - k)
        dxy_ref[:, 0:LANES] = dxl.astype(BF16)
        for k in range(CONV_K):
            dcw_ref[k:k + 1, :] = jnp.sum(du * _shift_down(xl, CONV_K - 1 - k), axis=0, keepdims=True)
        dcb_ref[...] = jnp.sum(du, axis=0, keepdims=True)

        @pl.when(pl.program_id(0) == LRU_BLOCKS - 1)
        def _():
            _exchange_phase("finish", *comm)

    out = pl.pallas_call(
        body, name="lru_bwd", grid=(LRU_BLOCKS,),
        in_specs=[xl_s, yl_s, cw_s, vec, wblk, vec, wblk, vec, vec, col, col, HBM] + [HBM] * n_ex,
        out_specs=[pair, cw_s, vec, wblk, vec, wblk, vec, vec] + [HBM] * n_ex,
        input_output_aliases={11: 0},
        out_shape=[jax.ShapeDtypeStruct(dproj.shape, dproj.dtype),
                   jax.ShapeDtypeStruct((CONV_K, LRU_WIDTH), F32), jax.ShapeDtypeStruct((1, LRU_WIDTH), F32),
                   jax.ShapeDtypeStruct((LRU_BLOCKS, LANES, LANES), F32), jax.ShapeDtypeStruct((1, LRU_WIDTH), F32),
                   jax.ShapeDtypeStruct((LRU_BLOCKS, LANES, LANES), F32), jax.ShapeDtypeStruct((1, LRU_WIDTH), F32),
                   jax.ShapeDtypeStruct((1, LRU_WIDTH), F32)]
        + [jax.ShapeDtypeStruct(p.shape, p.dtype) for p in exchange],
        scratch_shapes=[pltpu.VMEM((t, LANES), F32)] * 3 + _comm_scratch(n_ex),
        compiler_params=_params(("arbitrary",), big=True),
    )(proj, proj, cw, cb, wr, br, wi, bi, lam, h_all, dout, dproj, *exchange)
    return out[:8], out[8:]


def _mesh_pos():
    return lax.axis_index("x"), lax.axis_index("y"), lax.axis_index("c")


HBM = pl.BlockSpec(memory_space=pl.ANY)


def _comm_scratch(n):
    return [pltpu.SemaphoreType.DMA((n, 7)), pltpu.SemaphoreType.DMA((n, 7)), pltpu.SemaphoreType.DMA((n,))]


def _gather_phase(phase, v_refs, out_refs, send_sems, recv_sems, local_sems):
    n = len(v_refs)
    x, y, c = _mesh_pos()
    me, sibling = (x, y, c), (x, y, 1 - c)
    chips = [(1 - x, y), (x, 1 - y), (1 - x, 1 - y)]

    def block(a, px, py, pc):
        return out_refs[a].at[4 * px + 2 * py + pc]

    def copy(a, k, blk, to, src=None):
        return pltpu.make_async_remote_copy(
            src_ref=block(a, *blk) if src is None else src, dst_ref=block(a, *blk),
            send_sem=send_sems.at[a, k], recv_sem=recv_sems.at[a, k], device_id=to, device_id_type=MESH)

    def own(a):
        return pltpu.make_async_copy(v_refs[a], block(a, *me), local_sems.at[a])

    def first(a):
        return ([copy(a, 0, me, sibling, src=v_refs[a])]
                + [copy(a, 1 + j, me, (*chip, c), src=v_refs[a]) for j, chip in enumerate(chips)])

    def forward(a, j):
        return copy(a, 4 + j, (*chips[j], c), sibling)

    if phase == "start":
        for a in range(n):
            own(a).start()
        for a in range(n):
            for cp in first(a):
                cp.start()
    elif phase == "forward":
        for j in range(3):
            for a in range(n):
                copy(a, 1 + j, (*chips[j], c), me).wait_recv()
                forward(a, j).start()
    else:
        for a in range(n):
            copy(a, 0, sibling, me).wait_recv()
            for j in range(3):
                copy(a, 4 + j, (*chips[j], 1 - c), me).wait_recv()
        for a in range(n):
            for cp in first(a) + [forward(a, j) for j in range(3)]:
                cp.wait_send()
            own(a).wait()


def _gather_hosted(phase, *comm):
    _gather_phase("forward" if phase == "late" else phase, *comm)


def _all_gather(vs, name):
    n = len(vs)

    def body(*refs):
        comm = (refs[:n], refs[n:2 * n]) + tuple(refs[2 * n:])
        for phase in ("start", "forward", "finish"):
            _gather_phase(phase, *comm)

    return pl.pallas_call(
        body, name=name,
        out_shape=[jax.ShapeDtypeStruct((N_DEV,) + v.shape, v.dtype) for v in vs],
        in_specs=[HBM] * n, out_specs=[HBM] * n, scratch_shapes=_comm_scratch(n),
    )(*vs)


def _run_copies(phase, local, remote):
    if phase == "start":
        for cp in local + remote:
            cp.start()
    else:
        for cp in remote:
            cp.wait()
        for cp in local:
            cp.wait()


def _exchange_phase(phase, p_refs, out_refs, send_sems, recv_sems, local_sems):
    if phase == "late":
        return
    n = len(p_refs)
    x, y, c = _mesh_pos()
    me = 4 * x + 2 * y + c
    local = [pltpu.make_async_copy(p_refs[a].at[me], out_refs[a].at[me], local_sems.at[a]) for a in range(n)]
    remote = []
    for k in range(1, N_DEV):
        px = (1 - x) if k & 4 else x
        py = (1 - y) if k & 2 else y
        pc = (1 - c) if k & 1 else c
        for a in range(n):
            remote.append(pltpu.make_async_remote_copy(
                src_ref=p_refs[a].at[4 * px + 2 * py + pc], dst_ref=out_refs[a].at[me],
                send_sem=send_sems.at[a, k - 1], recv_sem=recv_sems.at[a, k - 1],
                device_id=(px, py, pc), device_id_type=MESH))
    _run_copies(phase, local, remote)


def _chip_exchange_phase(phase, p_refs, out_refs, send_sems, recv_sems, local_sems):
    if phase == "late":
        return
    n = len(p_refs)
    x, y, c = _mesh_pos()
    me = 2 * x + y
    local = [pltpu.make_async_copy(p_refs[a].at[me], out_refs[a].at[me], local_sems.at[a]) for a in range(n)]
    remote = []
    for k in range(1, 4):
        px = (1 - x) if k & 2 else x
        py = (1 - y) if k & 1 else y
        for a in range(n):
            remote.append(pltpu.make_async_remote_copy(
                src_ref=p_refs[a].at[2 * px + py], dst_ref=out_refs[a].at[me],
                send_sem=send_sems.at[a, k - 1], recv_sem=recv_sems.at[a, k - 1],
                device_id=(px, py, c), device_id_type=MESH))
    _run_copies(phase, local, remote)


def _sibling_exchange(parts, name):
    chips = N_DEV // 2

    def body(p_ref, out_ref, send_sems, recv_sems):
        x, y, c = _mesh_pos()
        copies = [pltpu.make_async_remote_copy(
            src_ref=p_ref.at[2 * q + 1 - c], dst_ref=out_ref.at[q], send_sem=send_sems.at[q], recv_sem=recv_sems.at[q],
            device_id=(x, y, 1 - c), device_id_type=MESH) for q in range(chips)]
        _run_copies("start", [], copies)
        _run_copies("finish", [], copies)

    return pl.pallas_call(
        body, name=name, out_shape=jax.ShapeDtypeStruct((chips,) + parts.shape[1:], parts.dtype),
        in_specs=[HBM], out_specs=HBM,
        scratch_shapes=[pltpu.SemaphoreType.DMA((chips,)), pltpu.SemaphoreType.DMA((chips,))],
    )(parts)


def _pair_sum(parts, theirs, name):
    slots, rows, cols = theirs.shape
    tc = 256
    core = lax.axis_index("c").astype(jnp.int32).reshape(1)

    def body(c_ref, a_ref, b_ref, o_ref):
        o_ref[...] = (a_ref[...].astype(F32) + b_ref[...].astype(F32)).astype(o_ref.dtype)

    spec = pl.BlockSpec((1, rows, tc), lambda q, j, c: (q, 0, j))
    return pl.pallas_call(
        body, name=name,
        grid_spec=pltpu.PrefetchScalarGridSpec(
            num_scalar_prefetch=1, grid=(slots, cols // tc),
            in_specs=[pl.BlockSpec((1, rows, tc), lambda q, j, c: (2 * q + c[0], 0, j)), spec], out_specs=spec),
        out_shape=jax.ShapeDtypeStruct(theirs.shape, theirs.dtype),
        compiler_params=_params(("parallel", "parallel")),
    )(core, parts, theirs)


def _sum_sources(recvs, name):
    k = len(recvs)

    def body(*refs):
        for r_ref, o_ref in zip(refs[:k], refs[k:]):
            acc = r_ref[0].astype(F32)
            for s in range(1, r_ref.shape[0]):
                acc = acc + r_ref[s].astype(F32)
            o_ref[...] = acc

    return pl.pallas_call(
        body, name=name, out_shape=[jax.ShapeDtypeStruct(r.shape[1:], F32) for r in recvs],
        compiler_params=_params(),
    )(*recvs)


def _row_tile(rows):
    for tile in range(128, 15, -16):
        if rows % tile == 0:
            return tile
    return rows


def _adam_update(w, g, m, v):
    nm = ADAM_B1 * m + (1.0 - ADAM_B1) * g
    nv = ADAM_B2 * v + (1.0 - ADAM_B2) * (g * g)
    m_hat = nm / (1.0 - ADAM_B1 ** ADAM_STEP)
    v_hat = nv / (1.0 - ADAM_B2 ** ADAM_STEP)
    return -ADAM_LR * (m_hat / (jnp.sqrt(v_hat) + ADAM_EPS) + ADAM_WD * w), nm, nv


def _vector_offsets(widths):
    offsets, end = [], 0
    for c in widths:
        offsets.append(end)
        end += c + (-c) % LANES
    return offsets, end


def _adamw_small(vec_parts, vec_state, mat_grads, mat_state):
    widths = [w.shape[1] for w, _, _ in vec_state]
    offsets, total = _vector_offsets(widths)
    assert vec_parts.shape == (N_DEV, total), (vec_parts.shape, total)
    n_vec, n_mat = len(vec_state), len(mat_state)

    def body(*refs):
        r_ref = refs[0]
        vec_in = refs[1:1 + 3 * n_vec]
        mat_in = refs[1 + 3 * n_vec:1 + 3 * n_vec + 4 * n_mat]
        outs = refs[1 + 3 * n_vec + 4 * n_mat:]
        for i, (off, c) in enumerate(zip(offsets, widths)):
            g = r_ref[0:1, off:off + c]
            for s in range(1, N_DEV):
                g = g + r_ref[s:s + 1, off:off + c]
            w_ref, m_ref, v_ref = vec_in[3 * i:3 * i + 3]
            g_out, d_out, m_out, v_out = outs[4 * i:4 * i + 4]
            g_out[...] = g
            d_out[...], m_out[...], v_out[...] = _adam_update(w_ref[...], g, m_ref[...], v_ref[...])
        for j in range(n_mat):
            g_ref, w_ref, m_ref, v_ref = mat_in[4 * j:4 * j + 4]
            d_out, m_out, v_out = outs[4 * n_vec + 3 * j:4 * n_vec + 3 * j + 3]
            d_out[...], m_out[...], v_out[...] = _adam_update(w_ref[...], g_ref[...], m_ref[...], v_ref[...])

    args = [vec_parts] + [a for state in vec_state for a in state]
    for g, state in zip(mat_grads, mat_state):
        args += [g, *state]
    out_shape = [jax.ShapeDtypeStruct(w.shape, F32) for w, _, _ in vec_state for _ in range(4)]
    out_shape += [jax.ShapeDtypeStruct(w.shape, F32) for w, _, _ in mat_state for _ in range(3)]
    out = pl.pallas_call(body, name="adamw_replicated", out_shape=out_shape, compiler_params=_params())(*args)
    vec_out = [tuple(out[4 * i:4 * i + 4]) for i in range(n_vec)]
    mat_out = [tuple(out[4 * n_vec + 3 * j:4 * n_vec + 3 * j + 3]) for j in range(n_mat)]
    return vec_out, mat_out


def _adamw(w, recv, m, v, name):
    rows, width = w.shape
    n = recv.shape[0]
    if rows % 16 == 0 or width % 256:
        tr, tc = _row_tile(rows), width
    else:
        tr, tc = rows, 256

    def body(w_ref, r_ref, m_ref, v_ref, g_ref, d_ref, nm_ref, nv_ref):
        gv = r_ref[0].astype(F32)
        for s in range(1, n):
            gv = gv + r_ref[s].astype(F32)
        g_ref[...] = gv
        d_ref[...], nm_ref[...], nv_ref[...] = _adam_update(w_ref[...], gv, m_ref[...], v_ref[...])

    spec = pl.BlockSpec((tr, tc), lambda i, j: (i, j))
    shape = jax.ShapeDtypeStruct((rows, width), F32)
    return pl.pallas_call(
        body, name=name, grid=(rows // tr, width // tc),
        in_specs=[spec, pl.BlockSpec((n, tr, tc), lambda i, j: (0, i, j)), spec, spec],
        out_specs=[spec] * 4, out_shape=[shape] * 4,
        compiler_params=_params(("parallel", "parallel")),
    )(w, recv, m, v)


BIG_NAMES = ("w_in", "w_out_ssm", "w_out_lru", "w_out", "w_ffn_in", "w_ffn_out", "ssm_conv_w", "lru_conv_w")
TRANSPOSED = ("w_in", "w_ffn_in")
CONV_NAMES = ("ssm_conv_w", "lru_conv_w")
MATMUL_NAMES = BIG_NAMES[:6]
NEEDED_FIRST = ("w_in", "ssm_conv_w", "lru_conv_w")
GATHERED_IN_PROJ = ("w_ffn_in", "w_ffn_out")
GATHERED_IN_SSD = ("w_out_ssm",)
GATHERED_IN_LRU = ("w_out_lru", "w_out")
EXCHANGED_IN_SSD = ("w_ffn_in", "w_ffn_out")
EXCHANGED_IN_LRU = ("w_out_ssm", "w_out_lru", "w_out")
SMALL_VECTORS = ("norm1_w", "b_branch_gate", "ssm_conv_b", "ssm_dt_bias", "ssm_a_log", "ssm_d", "ssm_norm_w",
                 "lru_conv_b", "lru_b_r", "lru_b_i", "lru_lambda", "norm2_w", "norm_f_w")
SMALL_MATRICES = ("lru_w_r", "lru_w_i")


def _col_shards(full):
    rows, cols = full.shape
    return full.reshape(rows, N_DEV, cols // N_DEV).transpose(1, 0, 2)


def _from_col_shards(g):
    n, rows, w = g.shape
    return g.transpose(1, 0, 2).reshape(rows, n * w)


def kernel(x, norm1_w, w_in, b_branch_gate, ssm_conv_w, ssm_conv_b, ssm_dt_bias, ssm_a_log, ssm_d, ssm_norm_w, w_out_ssm, lru_conv_w, lru_conv_b, lru_w_r, lru_b_r, lru_w_i, lru_b_i, lru_lambda, w_out_lru, w_out, norm2_w, w_ffn_in, w_ffn_out, norm_f_w, loss_target, m_norm1_w, m_w_in, m_b_branch_gate, m_ssm_conv_w, m_ssm_conv_b, m_ssm_dt_bias, m_ssm_a_log, m_ssm_d, m_ssm_norm_w, m_w_out_ssm, m_lru_conv_w, m_lru_conv_b, m_lru_w_r, m_lru_b_r, m_lru_w_i, m_lru_b_i, m_lru_lambda, m_w_out_lru, m_w_out, m_norm2_w, m_w_ffn_in, m_w_ffn_out, m_norm_f_w, v_norm1_w, v_w_in, v_b_branch_gate, v_ssm_conv_w, v_ssm_conv_b, v_ssm_dt_bias, v_ssm_a_log, v_ssm_d, v_ssm_norm_w, v_w_out_ssm, v_lru_conv_w, v_lru_conv_b, v_lru_w_r, v_lru_b_r, v_lru_w_i, v_lru_b_i, v_lru_lambda, v_w_out_lru, v_w_out, v_norm2_w, v_w_ffn_in, v_w_ffn_out, v_norm_f_w):
    given = dict(locals())
    weights = {n: given[n] for n in BIG_NAMES + SMALL_VECTORS + SMALL_MATRICES}
    t = x.shape[1]
    xt = x[0]
    tgt = loss_target[0]

    def local(n, a):
        return a[0].T if n in TRANSPOSED else a[0]

    def as_output(n, a):
        return a.T[None] if n in TRANSPOSED else a[None]

    def shard(n):
        s = local(n, weights[n])
        return s.astype(BF16) if n in MATMUL_NAMES else s

    def unshard(n, g):
        return _from_col_shards(g) if n in CONV_NAMES else g.reshape(-1, g.shape[-1])

    def grad_slices(n):
        g = grads[n]
        return (_col_shards(g) if n in CONV_NAMES else g.reshape(N_DEV, -1, g.shape[-1])).astype(BF16)

    gathered = _all_gather([shard(n) for n in NEEDED_FIRST], "gather_in_weights")
    full = {n: unshard(n, g) for n, g in zip(NEEDED_FIRST, gathered)}
    ssm_cw, lru_cw = full["ssm_conv_w"], full["lru_conv_w"]
    wi_t = full["w_in"]
    lru_rows = wi_t[ORIG_LRU_X:].reshape(2, LRU_BLOCKS, LANES, D_MODEL).transpose(1, 0, 2, 3)
    w_tail = jnp.concatenate([wi_t[W_IN_MAIN:ORIG_DT], lru_rows.reshape(2 * LRU_WIDTH, D_MODEL),
                              wi_t[ORIG_DT:ORIG_LRU_X], jnp.zeros((PROJ_W - IN_PROJ, D_MODEL), BF16)], axis=0)

    def pad_heads(a):
        return jnp.pad(a.reshape(1, SSM_HEADS), ((0, 0), (0, LANES - SSM_HEADS)))

    dt_bias_p = pad_heads(ssm_dt_bias)
    a_log_p = pad_heads(ssm_a_log)
    d_exp = jnp.repeat(ssm_d.reshape(SSM_HEADS), SSM_HEAD_DIM).reshape(1, SSM_INNER)
    lru_wr, lru_wi = lru_w_r[0], lru_w_i[0]

    hn1 = _rmsnorm_fwd(xt, norm1_w, "norm1_fwd")
    later = [shard(n) for n in GATHERED_IN_PROJ]
    proj, (gathered,) = _mm(
        hn1, wi_t, tb=True, b_tail=w_tail, b_main=W_IN_MAIN, name="in_proj",
        hosted=[(_gather_hosted, later, [jax.ShapeDtypeStruct((N_DEV,) + v.shape, v.dtype) for v in later])])
    full.update({n: unshard(n, g) for n, g in zip(GATHERED_IN_PROJ, gathered)})
    xbc_act = _ssm_conv_fwd(proj, ssm_cw, ssm_conv_b)
    (y_ssd, s_in_all), gathered = _ssd_fwd(xbc_act, proj, dt_bias_p, a_log_p, d_exp,
                                           gather=[shard(n) for n in GATHERED_IN_SSD])
    full.update({n: unshard(n, g) for n, g in zip(GATHERED_IN_SSD, gathered)})
    (l_out, h_lru), gathered = _lru_fwd(proj, lru_cw, lru_conv_b, lru_wr, lru_b_r, lru_wi, lru_b_i, lru_lambda,
                                        gather=[shard(n) for n in GATHERED_IN_LRU])
    full.update({n: unshard(n, g) for n, g in zip(GATHERED_IN_LRU, gathered)})
    y_pre = _ssm_post_fwd(y_ssd, proj, ssm_norm_w)
    y_ssm = _mm(y_pre, full["w_out_ssm"], out_dtype=BF16, name="out_ssm")
    y_lru = _mm(l_out, full["w_out_lru"], out_dtype=BF16, name="out_lru")
    merged = _merge_fwd(proj, b_branch_gate, y_ssm, y_lru)
    h1, hn2 = _mm(merged, full["w_out"], epilogue=_residual_norm_epilogue(xt, norm2_w), name="out_proj")
    gate, up, act = _ffn_in_swiglu(hn2, full["w_ffn_in"])

    grads = {}
    dh2, grads["norm_f_w"], loss_cols = _mm(
        act, full["w_ffn_out"], epilogue=_loss_epilogue(h1, norm_f_w.reshape(1, D_MODEL), tgt), name="ffn_out")
    loss = lax.psum(0.5 * jnp.sum(loss_cols) / D_MODEL, AXES)
    dact = _mm(dh2, full["w_ffn_out"], tb=True, out_dtype=BF16, name="d_act")
    grads["w_ffn_out"] = _mm(act, dh2, ta=True, out_dtype=BF16, name="dw_ffn_out")
    dgu = _swiglu_bwd(gate, up, dact)
    dh1, grads["norm2_w"] = _mm(dgu, full["w_ffn_in"], epilogue=_norm_bwd_epilogue(h1, norm2_w, dh2), name="d_hn2")
    grads["w_ffn_in"] = _mm(dgu, hn2, ta=True, out_dtype=BF16, name="dw_ffn_in")
    dmerged = _mm(dh1, full["w_out"], tb=True, out_dtype=BF16, name="d_merged")
    grads["w_out"] = _mm(merged, dh1, ta=True, out_dtype=BF16, name="dw_out")
    dy_ssm, dy_lru, dproj, grads["b_branch_gate"] = _merge_bwd(proj, b_branch_gate, y_ssm, y_lru, dmerged)
    dy_pre = _mm(dy_ssm, full["w_out_ssm"], tb=True, out_dtype=BF16, name="d_y_pre")
    grads["w_out_ssm"] = _mm(y_pre, dy_ssm, ta=True, out_dtype=BF16, name="dw_out_ssm")
    dl_out = _mm(dy_lru, full["w_out_lru"], tb=True, out_dtype=BF16, name="d_l_out")
    grads["w_out_lru"] = _mm(l_out, dy_lru, ta=True, out_dtype=BF16, name="dw_out_lru")
    dy_ssd, dproj, grads["ssm_norm_w"] = _ssm_post_bwd(y_ssd, proj, ssm_norm_w, dy_pre, dproj)
    (dxbc_act, dproj, dbias, dalog, ddcol), recv_in_ssd = _ssd_bwd(
        xbc_act, proj, s_in_all, dy_ssd, dt_bias_p, a_log_p, d_exp, dproj,
        exchange=[grad_slices(n) for n in EXCHANGED_IN_SSD])
    grads["ssm_dt_bias"] = dbias[:, :SSM_HEADS]
    grads["ssm_a_log"] = dalog[:, :SSM_HEADS]
    grads["ssm_d"] = ddcol.reshape(SSM_HEADS, SSM_HEAD_DIM).sum(axis=1).reshape(1, SSM_HEADS)
    dproj, grads["ssm_conv_w"], grads["ssm_conv_b"] = _ssm_conv_bwd(proj, ssm_cw, ssm_conv_b, dxbc_act, dproj)
    ((dproj, grads["lru_conv_w"], grads["lru_conv_b"], dwr, grads["lru_b_r"], dwi, grads["lru_b_i"],
      grads["lru_lambda"]), recv_in_lru) = _lru_bwd(
        proj, lru_cw, lru_conv_b, lru_wr, lru_b_r, lru_wi, lru_b_i, lru_lambda, h_lru, dl_out, dproj,
        exchange=[grad_slices(n) for n in EXCHANGED_IN_LRU])
    grads["lru_w_r"], grads["lru_w_i"] = dwr[None], dwi[None]
    dwpt = _mm(dproj, hn1, ta=True, out_dtype=BF16, name="dw_in")
    lru_rows = dwpt[COL_LRU:COL_DT].reshape(LRU_BLOCKS, 2, LANES, D_MODEL).transpose(1, 0, 2, 3)
    grads["w_in"] = jnp.concatenate([dwpt[:ORIG_DT], dwpt[COL_DT:COL_DT + SSM_HEADS],
                                     lru_rows.reshape(2 * LRU_WIDTH, D_MODEL)], axis=0)
    w_in_parts = grad_slices("w_in")
    from_sibling = _sibling_exchange(w_in_parts, "sibling_exchange_dw_in")
    chip_parts = [_pair_sum(w_in_parts, from_sibling, "pair_sum_dw_in")]
    direct = [grad_slices(n) for n in CONV_NAMES] + [grads[n].reshape(N_DEV, -1, LANES) for n in SMALL_MATRICES]
    (grad_x, grads["norm1_w"]), (recv_w_in, recv_direct) = _mm(
        dproj, wi_t, b_tail=w_tail, b_main=W_IN_MAIN, epilogue=_norm_bwd_epilogue(xt, norm1_w, dh1), name="d_hn1",
        hosted=[(_chip_exchange_phase, chip_parts, chip_parts), (_exchange_phase, direct, direct)])
    recv_first = list(recv_w_in) + list(recv_direct[:len(CONV_NAMES)])
    recv_mats = recv_direct[len(CONV_NAMES):]

    def as_rows(n, a):
        return a.reshape(1, -1) if n in SMALL_VECTORS else a.reshape(-1, LANES)

    vec_g = jnp.concatenate([jnp.pad(as_rows(n, grads[n]), ((0, 0), (0, (-grads[n].size) % LANES)))
                             for n in SMALL_VECTORS], axis=1)
    gathered = _all_gather([vec_g] + list(_sum_sources(recv_mats, "sum_gate_matrix_grads")), "gather_small_grads")
    vec_parts = gathered[0].reshape(N_DEV, -1)
    mat_g = [g.reshape(-1, LANES) for g in gathered[1:]]

    recv = dict(zip(EXCHANGED_IN_SSD + EXCHANGED_IN_LRU + NEEDED_FIRST,
                    list(recv_in_ssd) + list(recv_in_lru) + list(recv_first)))
    big_out = {n: _adamw(local(n, weights[n]), recv[n], local(n, given["m_" + n]), local(n, given["v_" + n]),
                         "adamw_" + n) for n in BIG_NAMES}

    def state(n):
        return tuple(as_rows(n, given[p + n]) for p in ("", "m_", "v_"))

    vec_out, mat_out = _adamw_small(vec_parts, [state(n) for n in SMALL_VECTORS],
                                    mat_g, [state(n) for n in SMALL_MATRICES])
    small_out = dict(zip(SMALL_VECTORS, vec_out))
    small_out.update({n: (g,) + out for n, g, out in zip(SMALL_MATRICES, mat_g, mat_out)})

    order = list(given)[1:24]
    results = []
    for q in range(4):
        vals = {n: as_output(n, big_out[n][q]) for n in BIG_NAMES}
        vals.update({n: out[q].reshape(weights[n].shape) for n, out in small_out.items()})
        results.extend(vals[n] for n in order)
    return (loss, grad_x[None], *results)
```

```python
import math

import jax
import jax.numpy as jnp
from jax import lax
from jax.experimental import pallas as pl
from jax.experimental.pallas import tpu as pltpu

F32 = jnp.float32
BF16 = jnp.bfloat16
HIGHEST = lax.Precision.HIGHEST
MESH = pl.DeviceIdType.MESH
AXES = ("x", "y", "c")
N_DEV = 8

D_MODEL = 1024
SSM_INNER = 2048
SSM_HEADS = 32
SSM_HEAD_DIM = 64
SSM_GROUPS = 4
SSM_STATE = 128
SSM_BC = SSM_GROUPS * SSM_STATE
SSM_CONV_DIM = SSM_INNER + 2 * SSM_BC
SSM_CHUNK = 128
SSM_PAIRS = SSM_HEADS // 2
CONV_K = 4
LRU_WIDTH = 1280
LRU_BLOCKS = 10
LRU_C = 8.0
FFN_HIDDEN = 2816
RMS_EPS = 1e-6
IN_PROJ = 9760

COL_GATES = 0
COL_Z = 2048
COL_XBC = 4096
COL_LRU = 7168
COL_DT = 9728
PROJ_W = 9856
ORIG_DT = 7168
ORIG_LRU_X = 7200
ORIG_LRU_Y = 8480
W_IN_MAIN = 7040

ADAM_LR = 0.001
ADAM_B1 = 0.9
ADAM_B2 = 0.999
ADAM_EPS = 1e-08
ADAM_WD = 0.01
ADAM_STEP = 10

LANES = 128
SUBLANES = 8
V7X_VMEM_BYTES = 64 * 1024 * 1024
VMEM_LIMIT = V7X_VMEM_BYTES * 3 // 4
VMEM_LIMIT_BIG = V7X_VMEM_BYTES * 15 // 16

NT = (((1,), (1,)), ((), ()))
TN = (((0,), (0,)), ((), ()))


def _params(sem=None, big=False):
    return pltpu.CompilerParams(dimension_semantics=sem,
                                vmem_limit_bytes=VMEM_LIMIT_BIG if big else VMEM_LIMIT)


def _blk(dim, cap):
    if dim <= cap:
        return dim
    for m in range(cap // LANES, 0, -1):
        if dim % (m * LANES) == 0:
            return m * LANES
    raise ValueError(f"no block for {dim}")


def _rows(t):
    return min(t, 256)


def _sigmoid(v):
    return 1.0 / (1.0 + jnp.exp(-v))


def _softplus(v):
    e = jnp.exp(-jnp.abs(v))
    u = 1.0 + e
    log1p = jnp.where(u == 1.0, e, jnp.log(u) * e / jnp.where(u == 1.0, 1.0, u - 1.0))
    return jnp.maximum(v, 0.0) + log1p


def _iota(shape, dim):
    return lax.broadcasted_iota(jnp.int32, shape, dim)


def _shift_down(v, s):
    if s == 0:
        return v
    return jnp.where(_iota(v.shape, 0) >= s, pltpu.roll(v, s, 0), 0.0)


def _shift_up(v, s):
    if s == 0:
        return v
    n = v.shape[0]
    return jnp.where(_iota(v.shape, 0) < n - s, pltpu.roll(v, n - s, 0), 0.0)


def _bdot(a, b, dn=None):
    a = a.astype(BF16)
    b = b.astype(BF16)
    if dn is None:
        return jnp.dot(a, b, preferred_element_type=F32)
    return lax.dot_general(a, b, dn, preferred_element_type=F32)


def _split_dot(a, e, dn=None, exact=True):
    hi = a.astype(BF16)
    if not exact:
        return _bdot(hi, e, dn)
    lo = (a - hi.astype(F32)).astype(BF16)
    return _bdot(hi, e, dn) + _bdot(lo, e, dn)


def _fdot(a, b, dn=None):
    if dn is None:
        return jnp.dot(a, b, precision=HIGHEST, preferred_element_type=F32)
    return lax.dot_general(a, b, dn, precision=HIGHEST, preferred_element_type=F32)


def _mm(a, b, *, ta=False, tb=False, add=None, hosted=(), out_dtype=F32, epilogue=None, b_tail=None, b_main=0, name):
    if ta:
        kdim, m = a.shape
    else:
        m, kdim = a.shape
    if tb:
        n, k2 = b.shape
    else:
        k2, n = b.shape
    if b_tail is not None:
        if tb:
            n = b_main + b_tail.shape[0]
        else:
            k2 = b_main + b_tail.shape[0]
    assert kdim == k2, (a.shape, b.shape, ta, tb)
    if epilogue is None:
        rows, vecs, row_dtypes, n_vec_out = ([] if add is None else [add]), [], [out_dtype], 0

        def finish(r, row_vals, vec_vals):
            return ((r + row_vals[0]) if row_vals else r,), ()
    else:
        assert add is None
        finish, rows, vecs, row_dtypes, n_vec_out = epilogue
    bm, bn, bk = _blk(m, 1408 if epilogue is None else 512), _blk(n, 1408), _blk(kdim, 1408)
    grid = (m // bm, n // bn, kdim // bk)
    nk = grid[2]
    assert n_vec_out == 0 or grid[1] == 1, "column sums are accumulated over the row tiles of whole rows"
    dn = (((0 if ta else 1,), (1 if tb else 0,)), ((), ()))
    n_ab = 2 if b_tail is None else 3
    main_blocks = b_main // (bn if tb else bk)
    assert b_main % (bn if tb else bk) == 0
    n_in = n_ab + len(rows) + len(vecs)
    n_out = len(row_dtypes) + n_vec_out
    sizes = [len(arrays) for _, arrays, _ in hosted]
    n_ex = sum(sizes)

    def body(*refs):
        a_ref, b_ref = refs[:2]
        row_refs, vec_refs = refs[n_ab:n_ab + len(rows)], refs[n_ab + len(rows):n_in]
        out_refs = refs[n_in + n_ex:n_in + n_ex + n_out]
        acc = refs[n_in + 2 * n_ex + n_out]
        comms, at = [], 0
        for g, size in enumerate(sizes):
            sems = refs[n_in + 2 * n_ex + n_out + 1 + 3 * g:n_in + 2 * n_ex + n_out + 4 + 3 * g]
            comms.append((refs[n_in + at:n_in + at + size],
                          refs[n_in + n_ex + n_out + at:n_in + n_ex + n_out + at + size]) + tuple(sems))
            at += size
        step = (pl.program_id(0) * grid[1] + pl.program_id(1)) * nk + pl.program_id(2)
        k = pl.program_id(2)
        if n_ex:
            @pl.when(step == 0)
            def _():
                for (phase_fn, _, _), comm in zip(hosted, comms):
                    phase_fn("start", *comm)

            @pl.when(step == (7 * grid[0] * grid[1] * nk) // 8)
            def _():
                for (phase_fn, _, _), comm in zip(hosted, comms):
                    phase_fn("late", *comm)

        @pl.when(k == 0)
        def _():
            acc[...] = jnp.zeros_like(acc)

        bv = b_ref[...]
        if b_tail is not None:
            bv = jnp.where(pl.program_id(1 if tb else 2) < main_blocks, bv, refs[2][...])
        acc[...] += lax.dot_general(a_ref[...].astype(BF16), bv.astype(BF16), dn, preferred_element_type=F32)

        @pl.when(k == nk - 1)
        def _():
            row_outs, col_sums = finish(acc[...], [r[...] for r in row_refs], [v[...] for v in vec_refs])
            for o_ref, val in zip(out_refs, row_outs):
                o_ref[...] = val.astype(o_ref.dtype)
            for o_ref, val in zip(out_refs[len(row_dtypes):], col_sums):
                @pl.when(pl.program_id(0) == 0)
                def _():
                    o_ref[...] = val

                @pl.when(pl.program_id(0) > 0)
                def _():
                    o_ref[...] += val

        if n_ex:
            @pl.when(step == grid[0] * grid[1] * nk - 1)
            def _():
                for (phase_fn, _, _), comm in zip(hosted, comms):
                    phase_fn("finish", *comm)

    a_spec = pl.BlockSpec((bk, bm), lambda i, j, k: (k, i)) if ta else pl.BlockSpec((bm, bk), lambda i, j, k: (i, k))
    b_specs = [pl.BlockSpec((bn, bk), lambda i, j, k: (j, k)) if tb else pl.BlockSpec((bk, bn), lambda i, j, k: (k, j))]
    if b_tail is not None:
        last = main_blocks - 1
        if tb:
            b_specs = [pl.BlockSpec((bn, bk), lambda i, j, k: (jnp.minimum(j, last), k)),
                       pl.BlockSpec((bn, bk), lambda i, j, k: (jnp.maximum(j - main_blocks, 0), k))]
        else:
            b_specs = [pl.BlockSpec((bk, bn), lambda i, j, k: (jnp.minimum(k, last), j)),
                       pl.BlockSpec((bk, bn), lambda i, j, k: (jnp.maximum(k - main_blocks, 0), j))]
    o_spec = pl.BlockSpec((bm, bn), lambda i, j, k: (i, j))
    v_spec = pl.BlockSpec((1, bn), lambda i, j, k: (0, j))
    in_specs = [a_spec] + b_specs + [o_spec] * len(rows) + [v_spec] * len(vecs) + [HBM] * n_ex
    args = ([a, b] + ([] if b_tail is None else [b_tail]) + list(rows) + list(vecs)
            + [p for _, arrays, _ in hosted for p in arrays])
    sequential = n_ex or n_vec_out
    out = pl.pallas_call(
        body, name=name, grid=grid,
        in_specs=in_specs, out_specs=[o_spec] * len(row_dtypes) + [v_spec] * n_vec_out + [HBM] * n_ex,
        out_shape=[jax.ShapeDtypeStruct((m, n), dt) for dt in row_dtypes]
        + [jax.ShapeDtypeStruct((1, n), F32)] * n_vec_out
        + [jax.ShapeDtypeStruct(r.shape, r.dtype) for _, _, results in hosted for r in results],
        scratch_shapes=[pltpu.VMEM((bm, bn), F32)] + [s for size in sizes for s in _comm_scratch(size)],
        compiler_params=_params(("arbitrary",) * 3 if sequential else ("parallel", "parallel", "arbitrary")),
    )(*args)
    result = out[0] if n_out == 1 else tuple(out[:n_out])
    if not n_ex:
        return result
    received, at = [], n_out
    for size in sizes:
        received.append(out[at:at + size])
        at += size
    return result, received


def _rmsnorm_fwd(x, w, name):
    t, d = x.shape
    tr = _rows(t)

    def body(x_ref, w_ref, o_ref):
        xv = x_ref[...]
        rstd = lax.rsqrt(jnp.mean(xv * xv, axis=-1, keepdims=True) + RMS_EPS)
        o_ref[...] = (xv * rstd * w_ref[...]).astype(BF16)

    return pl.pallas_call(
        body, name=name, grid=(t // tr,),
        in_specs=[pl.BlockSpec((tr, d), lambda i: (i, 0)), pl.BlockSpec((1, d), lambda i: (0, 0))],
        out_specs=pl.BlockSpec((tr, d), lambda i: (i, 0)),
        out_shape=jax.ShapeDtypeStruct((t, d), BF16),
        compiler_params=_params(("parallel",)),
    )(x, w)


def _normalize(h):
    rstd = lax.rsqrt(jnp.mean(h * h, axis=-1, keepdims=True) + RMS_EPS)
    return rstd, h * rstd


def _residual_norm_epilogue(x, w):
    def finish(r, rows, vecs):
        h = r + rows[0]
        return (h, _normalize(h)[1] * vecs[0]), ()

    return finish, [x], [w], [F32, BF16], 0


def _norm_bwd_epilogue(x, w, dres):
    def finish(r, rows, vecs):
        rstd, xhat = _normalize(rows[0])
        dxhat = r * vecs[0]
        m = jnp.mean(dxhat * xhat, axis=-1, keepdims=True)
        return (rstd * (dxhat - xhat * m) + rows[1],), (jnp.sum(r * xhat, axis=0, keepdims=True),)

    return finish, [x, dres], [w], [F32], 1


def _loss_epilogue(h1, w, tgt):
    d = h1.shape[1]

    def finish(r, rows, vecs):
        rstd, xhat = _normalize(r + rows[0])
        err = xhat * vecs[0] - rows[1]
        dyv = err * (1.0 / d)
        dxhat = dyv * vecs[0]
        m = jnp.mean(dxhat * xhat, axis=-1, keepdims=True)
        return ((rstd * (dxhat - xhat * m),),
                (jnp.sum(dyv * xhat, axis=0, keepdims=True), jnp.sum(err * err, axis=0, keepdims=True)))

    return finish, [h1, tgt], [w], [F32], 2


def _merge_fwd(proj, bg, ys, yl):
    t = proj.shape[0]
    d = D_MODEL
    tr = _rows(t)

    def body(ps_ref, pl_ref, bg_ref, ys_ref, yl_ref, o_ref):
        gs = _sigmoid(ps_ref[...] + bg_ref[:, 0:d])
        gl = _sigmoid(pl_ref[...] + bg_ref[:, d:2 * d])
        o_ref[...] = (gs * ys_ref[...] + gl * yl_ref[...]).astype(BF16)

    row = pl.BlockSpec((tr, d), lambda i: (i, 0))
    return pl.pallas_call(
        body, name="merge_fwd", grid=(t // tr,),
        in_specs=[row, pl.BlockSpec((tr, d), lambda i: (i, 1)), pl.BlockSpec((1, 2 * d), lambda i: (0, 0)), row, row],
        out_specs=row, out_shape=jax.ShapeDtypeStruct((t, d), BF16),
        compiler_params=_params(("parallel",)),
    )(proj, proj, bg, ys, yl)


def _merge_bwd(proj, bg, ys, yl, dm):
    t = proj.shape[0]
    d = D_MODEL
    tr = _rows(t)

    def body(ps_ref, pl_ref, bg_ref, ys_ref, yl_ref, dm_ref, dys_ref, dyl_ref, dg_ref, dbg_ref):
        i = pl.program_id(0)
        gs = _sigmoid(ps_ref[...] + bg_ref[:, 0:d])
        gl = _sigmoid(pl_ref[...] + bg_ref[:, d:2 * d])
        dmv = dm_ref[...]
        dys_ref[...] = (dmv * gs).astype(BF16)
        dyl_ref[...] = (dmv * gl).astype(BF16)
        dgs = dmv * ys_ref[...] * gs * (1.0 - gs)
        dgl = dmv * yl_ref[...] * gl * (1.0 - gl)
        dg_ref[:, 0:d] = dgs.astype(BF16)
        dg_ref[:, d:2 * d] = dgl.astype(BF16)

        @pl.when(i == 0)
        def _():
            dbg_ref[...] = jnp.zeros_like(dbg_ref)

        dbg_ref[:, 0:d] += jnp.sum(dgs, axis=0, keepdims=True)
        dbg_ref[:, d:2 * d] += jnp.sum(dgl, axis=0, keepdims=True)

    row = pl.BlockSpec((tr, d), lambda i: (i, 0))
    wide = pl.BlockSpec((tr, 2 * d), lambda i: (i, 0))
    vec = pl.BlockSpec((1, 2 * d), lambda i: (0, 0))
    return pl.pallas_call(
        body, name="merge_bwd", grid=(t // tr,),
        in_specs=[row, pl.BlockSpec((tr, d), lambda i: (i, 1)), vec, row, row, row],
        out_specs=[row, row, wide, vec],
        out_shape=[jax.ShapeDtypeStruct((t, d), BF16), jax.ShapeDtypeStruct((t, d), BF16),
                   jax.ShapeDtypeStruct((t, PROJ_W), BF16), jax.ShapeDtypeStruct((1, 2 * d), F32)],
        compiler_params=_params(("arbitrary",)),
    )(proj, proj, bg, ys, yl, dm)


def _ffn_in_swiglu(hn, wt):
    t, d = hn.shape
    f = FFN_HIDDEN
    bm, bn = _blk(t, 1024), _blk(f, 1408)
    nj = f // bn

    def body(a_ref, wg_ref, wu_ref, g_ref, u_ref, act_ref):
        a = a_ref[...]
        g = _bdot(a, wg_ref[...], NT)
        u = _bdot(a, wu_ref[...], NT)
        g_ref[...] = g.astype(BF16)
        u_ref[...] = u.astype(BF16)
        act_ref[...] = (g * _sigmoid(g) * u).astype(BF16)

    out = pl.BlockSpec((bm, bn), lambda i, j: (i, j))
    shape = jax.ShapeDtypeStruct((t, f), BF16)
    return pl.pallas_call(
        body, name="ffn_in_swiglu", grid=(t // bm, nj),
        in_specs=[pl.BlockSpec((bm, d), lambda i, j: (i, 0)), pl.BlockSpec((bn, d), lambda i, j: (j, 0)),
                  pl.BlockSpec((bn, d), lambda i, j: (nj + j, 0))],
        out_specs=[out, out, out], out_shape=[shape, shape, shape],
        compiler_params=_params(("parallel", "parallel")),
    )(hn, wt, wt)


def _swiglu_bwd(g_all, u_all, dact):
    t, f = g_all.shape
    tr = _rows(t)

    def body(g_ref, u_ref, da_ref, o_ref):
        g = g_ref[...].astype(F32)
        sg = _sigmoid(g)
        da = da_ref[...].astype(F32)
        o_ref[:, 0:f] = (da * u_ref[...].astype(F32) * (sg * (1.0 + g * (1.0 - sg)))).astype(BF16)
        o_ref[:, f:2 * f] = (da * g * sg).astype(BF16)

    row = pl.BlockSpec((tr, f), lambda i: (i, 0))
    return pl.pallas_call(
        body, name="swiglu_bwd", grid=(t // tr,),
        in_specs=[row, row, row],
        out_specs=pl.BlockSpec((tr, 2 * f), lambda i: (i, 0)),
        out_shape=jax.ShapeDtypeStruct((t, 2 * f), BF16),
        compiler_params=_params(("parallel",)),
    )(g_all, u_all, dact)


def _conv_pre(xv, wv, bv):
    pre = bv + wv[CONV_K - 1:CONV_K, :] * xv
    for k in range(CONV_K - 1):
        pre = pre + wv[k:k + 1, :] * _shift_down(xv, CONV_K - 1 - k)
    return pre


def _ssm_conv_fwd(proj, w, b):
    t = proj.shape[0]
    nb = SSM_CONV_DIM // LANES
    c0 = COL_XBC // LANES

    def body(x_ref, w_ref, b_ref, o_ref):
        pre = _conv_pre(x_ref[...], w_ref[...], b_ref[...])
        o_ref[...] = (pre * _sigmoid(pre)).astype(BF16)

    return pl.pallas_call(
        body, name="ssm_conv_fwd", grid=(nb,),
        in_specs=[pl.BlockSpec((t, LANES), lambda j: (0, c0 + j)), pl.BlockSpec((CONV_K, LANES), lambda j: (0, j)),
                  pl.BlockSpec((1, LANES), lambda j: (0, j))],
        out_specs=pl.BlockSpec((t, LANES), lambda j: (0, j)),
        out_shape=jax.ShapeDtypeStruct((t, SSM_CONV_DIM), BF16),
        compiler_params=_params(("parallel",)),
    )(proj, w, b)


def _ssm_conv_bwd(proj, w, b, dact, dproj):
    t = proj.shape[0]
    nb = SSM_CONV_DIM // LANES
    c0 = COL_XBC // LANES

    def body(x_ref, w_ref, b_ref, da_ref, dproj_in, dx_ref, dw_ref, db_ref):
        xv = x_ref[...]
        wv = w_ref[...]
        pre = _conv_pre(xv, wv, b_ref[...])
        sg = _sigmoid(pre)
        dpre = da_ref[...] * (sg * (1.0 + pre * (1.0 - sg)))
        dx = wv[CONV_K - 1:CONV_K, :] * dpre
        for k in range(CONV_K - 1):
            dx = dx + wv[k:k + 1, :] * _shift_up(dpre, CONV_K - 1 - k)
        dx_ref[...] = dx.astype(BF16)
        for k in range(CONV_K):
            dw_ref[k:k + 1, :] = jnp.sum(dpre * _shift_down(xv, CONV_K - 1 - k), axis=0, keepdims=True)
        db_ref[...] = jnp.sum(dpre, axis=0, keepdims=True)

    col = pl.BlockSpec((t, LANES), lambda j: (0, j))
    wsp = pl.BlockSpec((CONV_K, LANES), lambda j: (0, j))
    bsp = pl.BlockSpec((1, LANES), lambda j: (0, j))
    return pl.pallas_call(
        body, name="ssm_conv_bwd", grid=(nb,),
        in_specs=[pl.BlockSpec((t, LANES), lambda j: (0, c0 + j)), wsp, bsp, col, HBM],
        out_specs=[pl.BlockSpec((t, LANES), lambda j: (0, c0 + j)), wsp, bsp],
        out_shape=[jax.ShapeDtypeStruct(dproj.shape, dproj.dtype), jax.ShapeDtypeStruct((CONV_K, SSM_CONV_DIM), F32),
                   jax.ShapeDtypeStruct((1, SSM_CONV_DIM), F32)],
        input_output_aliases={4: 0},
        compiler_params=_params(("parallel",)),
    )(proj, w, b, dact, dproj)


def _ssd_chunk_terms(dtr, bias, alog):
    a = -jnp.exp(alog)
    dt = _softplus(dtr + bias)
    row = _iota((SSM_CHUNK, SSM_CHUNK), 0)
    col = _iota((SSM_CHUNK, SSM_CHUNK), 1)
    tri = (row >= col).astype(F32)
    cs = _fdot(tri, dt * a)
    dec = jnp.exp(cs[SSM_CHUNK - 1:SSM_CHUNK, :] - cs)
    ecs = jnp.exp(cs)
    off = _iota((LANES, SSM_INNER), 1) - SSM_HEAD_DIM * _iota((LANES, SSM_INNER), 0)
    expand = jnp.where(jnp.logical_and(off >= 0, off < SSM_HEAD_DIM), 1.0, 0.0).astype(BF16)
    return a, dt, cs, dec, ecs, expand, row, col


def _ssd_specs(t):
    nc = t // SSM_CHUNK
    xs = pl.BlockSpec((SSM_CHUNK, SSM_INNER), lambda c: (c, 0))
    bm = pl.BlockSpec((SSM_CHUNK, SSM_BC), lambda c: (c, SSM_INNER // SSM_BC))
    cm = pl.BlockSpec((SSM_CHUNK, SSM_BC), lambda c: (c, SSM_INNER // SSM_BC + 1))
    dtr = pl.BlockSpec((SSM_CHUNK, LANES), lambda c: (c, COL_DT // LANES))
    vec = pl.BlockSpec((1, LANES), lambda c: (0, 0))
    wide = pl.BlockSpec((1, SSM_INNER), lambda c: (0, 0))
    return nc, xs, bm, cm, dtr, vec, wide


def _ssd_fwd(xbc_act, proj, bias, alog, dexp, gather):
    t = proj.shape[0]
    nc, xs_s, bm_s, cm_s, dtr_s, vec, wide = _ssd_specs(t)
    n = len(gather)

    def body(*refs):
        xs_ref, b_ref, c_ref, dtr_ref, bias_ref, alog_ref, dexp_ref = refs[:7]
        y_ref, sin_ref = refs[7 + n:9 + n]
        state = refs[9 + 2 * n]
        comm = (refs[7:7 + n], refs[9 + n:9 + 2 * n]) + tuple(refs[10 + 2 * n:])
        chunk = pl.program_id(0)

        @pl.when(chunk == 0)
        def _():
            _gather_phase("start", *comm)
            state[...] = jnp.zeros_like(state)

        @pl.when(chunk == (3 * nc) // 4)
        def _():
            _gather_phase("forward", *comm)

        a, dt, cs, dec, ecs, expand, row, col = _ssd_chunk_terms(dtr_ref[...], bias_ref[...], alog_ref[...])
        cst = cs.T
        dt_x = _split_dot(dt, expand, exact=False)
        dec_x = _split_dot(dec, expand, exact=False)
        ecs_x = _split_dot(ecs, expand)
        xs = xs_ref[...].astype(F32)
        xdt = xs * dt_x
        xdec = xdt * dec_x
        lane_lo = col < SSM_HEAD_DIM
        causal = row >= col
        sin_ref[0] = state[...]
        for g in range(SSM_GROUPS):
            bg = b_ref[:, g * SSM_STATE:(g + 1) * SSM_STATE].astype(BF16)
            cg = c_ref[:, g * SSM_STATE:(g + 1) * SSM_STATE].astype(BF16)
            cb = _bdot(cg, bg, NT)
            for q in range(SSM_PAIRS // SSM_GROUPS):
                pq = g * (SSM_PAIRS // SSM_GROUPS) + q
                sl = slice(pq * LANES, (pq + 1) * LANES)
                xp = xdt[:, sl].astype(BF16)
                yd = []
                for hh in range(2):
                    h = 2 * pq + hh
                    lmat = jnp.exp(jnp.where(causal, cs[:, h:h + 1] - cst[h:h + 1, :], -jnp.inf))
                    yd.append(_bdot(cb * lmat, xp))
                s_in = state[pq]
                y_off = _bdot(cg, s_in) * ecs_x[:, sl]
                y_ref[:, sl] = (jnp.where(lane_lo, yd[0], yd[1]) + y_off + xs[:, sl] * dexp_ref[:, sl]).astype(BF16)
                state[pq] = s_in * ecs_x[SSM_CHUNK - 1:SSM_CHUNK, sl] + _bdot(bg, xdec[:, sl], TN)

        @pl.when(chunk == nc - 1)
        def _():
            _gather_phase("finish", *comm)

    out = pl.pallas_call(
        body, name="ssd_fwd", grid=(nc,),
        in_specs=[xs_s, bm_s, cm_s, dtr_s, vec, vec, wide] + [HBM] * n,
        out_specs=[pl.BlockSpec((SSM_CHUNK, SSM_INNER), lambda c: (c, 0)),
                   pl.BlockSpec((1, SSM_PAIRS, SSM_STATE, LANES), lambda c: (c, 0, 0, 0))] + [HBM] * n,
        out_shape=[jax.ShapeDtypeStruct((t, SSM_INNER), BF16),
                   jax.ShapeDtypeStruct((nc, SSM_PAIRS, SSM_STATE, LANES), F32)]
        + [jax.ShapeDtypeStruct((N_DEV,) + v.shape, v.dtype) for v in gather],
        scratch_shapes=[pltpu.VMEM((SSM_PAIRS, SSM_STATE, LANES), F32)] + _comm_scratch(n),
        compiler_params=_params(("arbitrary",)),
    )(xbc_act, xbc_act, xbc_act, proj, bias, alog, dexp, *gather)
    return out[:2], out[2:]


def _ssd_bwd(xbc_act, proj, s_in_all, dy, bias, alog, dexp, dproj, exchange):
    n_ex = len(exchange)
    t = proj.shape[0]
    nc = t // SSM_CHUNK
    last = nc - 1
    xs_s = pl.BlockSpec((SSM_CHUNK, SSM_INNER), lambda c: (last - c, 0))
    bm_s = pl.BlockSpec((SSM_CHUNK, SSM_BC), lambda c: (last - c, SSM_INNER // SSM_BC))
    cm_s = pl.BlockSpec((SSM_CHUNK, SSM_BC), lambda c: (last - c, SSM_INNER // SSM_BC + 1))
    dtr_s = pl.BlockSpec((SSM_CHUNK, LANES), lambda c: (last - c, COL_DT // LANES))
    sin_s = pl.BlockSpec((1, SSM_PAIRS, SSM_STATE, LANES), lambda c: (last - c, 0, 0, 0))
    vec = pl.BlockSpec((1, LANES), lambda c: (0, 0))
    wide = pl.BlockSpec((1, SSM_INNER), lambda c: (0, 0))

    def body(*refs):
        xs_ref, b_ref, c_ref, dtr_ref, sin_ref, dy_ref, bias_ref, alog_ref, dexp_ref = refs[:9]
        dxbc_ref, ddtr_ref, dbias_ref, dalog_ref, ddcol_ref = refs[10 + n_ex:15 + n_ex]
        dstate, dxdt_s, yoff_s, rx_s, trow_s = refs[15 + 2 * n_ex:20 + 2 * n_ex]
        comm = (refs[10:10 + n_ex], refs[15 + n_ex:15 + 2 * n_ex]) + tuple(refs[20 + 2 * n_ex:])

        @pl.when(pl.program_id(0) == 0)
        def _():
            _exchange_phase("start", *comm)
            dstate[...] = jnp.zeros_like(dstate)
            trow_s[...] = jnp.zeros_like(trow_s)
            dbias_ref[...] = jnp.zeros_like(dbias_ref)
            dalog_ref[...] = jnp.zeros_like(dalog_ref)
            ddcol_ref[...] = jnp.zeros_like(ddcol_ref)

        dtr = dtr_ref[...]
        a, dt, cs, dec, ecs, expand, row, col = _ssd_chunk_terms(dtr, bias_ref[...], alog_ref[...])
        cst = cs.T
        dt_x = _split_dot(dt, expand, exact=False)
        dec_x = _split_dot(dec, expand, exact=False)
        ecs_x = _split_dot(ecs, expand)
        xs = xs_ref[...].astype(F32)
        dyv = dy_ref[...].astype(F32)
        xdt = xs * dt_x
        lane_lo = col < SSM_HEAD_DIM
        causal = row >= col
        ddcol_ref[...] += jnp.sum(dyv * xs, axis=0, keepdims=True)
        dcs_col = jnp.zeros((SSM_CHUNK, LANES), F32)
        dcs_row = jnp.zeros((LANES, SSM_CHUNK), F32)
        for g in range(SSM_GROUPS):
            bg = b_ref[:, g * SSM_STATE:(g + 1) * SSM_STATE].astype(BF16)
            cg = c_ref[:, g * SSM_STATE:(g + 1) * SSM_STATE].astype(BF16)
            cb = _bdot(cg, bg, NT)
            dgm = jnp.zeros((SSM_CHUNK, SSM_CHUNK), F32)
            dbg = jnp.zeros((SSM_CHUNK, SSM_STATE), F32)
            dcg = jnp.zeros((SSM_CHUNK, SSM_STATE), F32)
            for q in range(SSM_PAIRS // SSM_GROUPS):
                pq = g * (SSM_PAIRS // SSM_GROUPS) + q
                sl = slice(pq * LANES, (pq + 1) * LANES)
                dyp = dyv[:, sl]
                xp = xdt[:, sl]
                dxh = []
                for hh in range(2):
                    h = 2 * pq + hh
                    lmat = jnp.exp(jnp.where(causal, cs[:, h:h + 1] - cst[h:h + 1, :], -jnp.inf))
                    mmat = cb * lmat
                    dyh = jnp.where(lane_lo if hh == 0 else jnp.logical_not(lane_lo), dyp, 0.0)
                    dmm = _bdot(dyh, xp, NT)
                    pm = dmm * mmat
                    dcs_col = jnp.where(col == h, jnp.sum(pm, axis=1, keepdims=True), dcs_col)
                    dcs_row = jnp.where(row == h, jnp.sum(pm, axis=0, keepdims=True), dcs_row)
                    dgm = dgm + dmm * lmat
                    dxh.append(_bdot(mmat, dyp, TN))
                s_in = sin_ref[0, pq]
                ecs_p = ecs_x[:, sl]
                dec_p = dec_x[:, sl]
                etot_p = ecs_x[SSM_CHUNK - 1:SSM_CHUNK, sl]
                yoff_s[:, sl] = dyp * (_bdot(cg, s_in) * ecs_p)
                dq = dyp * ecs_p
                dcg = dcg + _bdot(dq, s_in, NT)
                ds = dstate[pq]
                r = _bdot(bg, ds)
                rx_s[:, sl] = r * xp
                dxdt_s[:, sl] = jnp.where(lane_lo, dxh[0], dxh[1]) + dec_p * r
                dbg = dbg + _bdot(xp * dec_p, ds, NT)
                trow_s[0:1, sl] = jnp.sum(ds * s_in, axis=0, keepdims=True) * etot_p
                dstate[pq] = etot_p * ds + _bdot(cg, dq, TN)
            dcg = dcg + _bdot(dgm, bg)
            dbg = dbg + _bdot(dgm, cg, TN)
            dxbc_ref[:, SSM_INNER + g * SSM_STATE:SSM_INNER + (g + 1) * SSM_STATE] = dbg.astype(BF16)
            dxbc_ref[:, SSM_INNER + SSM_BC + g * SSM_STATE:SSM_INNER + SSM_BC + (g + 1) * SSM_STATE] = dcg.astype(BF16)
        ddec = _split_dot(rx_s[...], expand, NT, exact=False) * dec
        dtot = _split_dot(trow_s[...], expand, NT, exact=False)[0:1, :]
        dcs = dcs_col - dcs_row.T + _split_dot(yoff_s[...], expand, NT, exact=False) - ddec
        dcs = dcs + jnp.where(row == SSM_CHUNK - 1, jnp.sum(ddec, axis=0, keepdims=True) + dtot, 0.0)
        da = _fdot((row <= col).astype(F32), dcs)
        dxdt = dxdt_s[...]
        ddt = da * a + _split_dot(dxdt * xs, expand, NT, exact=False)
        dalog_ref[...] += jnp.sum(da * dt, axis=0, keepdims=True) * a
        ddtr = ddt * _sigmoid(dtr + bias_ref[...])
        ddtr_ref[...] = ddtr.astype(BF16)
        dbias_ref[...] += jnp.sum(ddtr, axis=0, keepdims=True)
        dxbc_ref[:, 0:SSM_INNER] = (dxdt * dt_x + dyv * dexp_ref[...]).astype(BF16)

        @pl.when(pl.program_id(0) == last)
        def _():
            _exchange_phase("finish", *comm)

    out = pl.pallas_call(
        body, name="ssd_bwd", grid=(nc,),
        in_specs=[xs_s, bm_s, cm_s, dtr_s, sin_s, pl.BlockSpec((SSM_CHUNK, SSM_INNER), lambda c: (last - c, 0)),
                  vec, vec, wide, HBM] + [HBM] * n_ex,
        out_specs=[pl.BlockSpec((SSM_CHUNK, SSM_CONV_DIM), lambda c: (last - c, 0)), dtr_s, vec, vec, wide]
        + [HBM] * n_ex,
        out_shape=[jax.ShapeDtypeStruct((t, SSM_CONV_DIM), BF16), jax.ShapeDtypeStruct(dproj.shape, dproj.dtype),
                   jax.ShapeDtypeStruct((1, LANES), F32), jax.ShapeDtypeStruct((1, LANES), F32),
                   jax.ShapeDtypeStruct((1, SSM_INNER), F32)]
        + [jax.ShapeDtypeStruct(p.shape, p.dtype) for p in exchange],
        input_output_aliases={9: 1},
        scratch_shapes=[pltpu.VMEM((SSM_PAIRS, SSM_STATE, LANES), F32),
                        pltpu.VMEM((SSM_CHUNK, SSM_INNER), F32), pltpu.VMEM((SSM_CHUNK, SSM_INNER), F32),
                        pltpu.VMEM((SSM_CHUNK, SSM_INNER), F32), pltpu.VMEM((SUBLANES, SSM_INNER), F32)]
        + _comm_scratch(n_ex),
        compiler_params=_params(("arbitrary",)),
    )(xbc_act, xbc_act, xbc_act, proj, s_in_all, dy, bias, alog, dexp, dproj, *exchange)
    return out[:5], out[5:]


def _group_rstd(y):
    n = SSM_INNER // SSM_GROUPS
    parts = []
    for g in range(SSM_GROUPS):
        yg = y[:, g * n:(g + 1) * n]
        r = lax.rsqrt(jnp.mean(yg * yg, axis=-1, keepdims=True) + RMS_EPS)
        parts.append(jnp.broadcast_to(r, yg.shape))
    return jnp.concatenate(parts, axis=1)


def _group_mean(v):
    n = SSM_INNER // SSM_GROUPS
    parts = []
    for g in range(SSM_GROUPS):
        vg = v[:, g * n:(g + 1) * n]
        parts.append(jnp.broadcast_to(jnp.mean(vg, axis=-1, keepdims=True), vg.shape))
    return jnp.concatenate(parts, axis=1)


def _ssm_post_fwd(y_ssd, proj, nw):
    t = proj.shape[0]
    n = SSM_INNER
    tr = _rows(t)

    def body(y_ref, z_ref, nw_ref, o_ref):
        z = z_ref[...]
        y = y_ref[...] * (z * _sigmoid(z))
        o_ref[...] = (y * _group_rstd(y) * nw_ref[...]).astype(BF16)

    row = pl.BlockSpec((tr, n), lambda i: (i, 0))
    return pl.pallas_call(
        body, name="ssm_post_fwd", grid=(t // tr,),
        in_specs=[row, pl.BlockSpec((tr, n), lambda i: (i, COL_Z // n)), pl.BlockSpec((1, n), lambda i: (0, 0))],
        out_specs=row, out_shape=jax.ShapeDtypeStruct((t, n), BF16),
        compiler_params=_params(("parallel",)),
    )(y_ssd, proj, nw)


def _ssm_post_bwd(y_ssd, proj, nw, dout, dproj):
    t = proj.shape[0]
    n = SSM_INNER
    tr = _rows(t)

    def body(y_ref, z_ref, nw_ref, do_ref, dproj_in, dy_ref, dz_ref, dnw_ref):
        i = pl.program_id(0)
        z = z_ref[...]
        sg = _sigmoid(z)
        sz = z * sg
        ys = y_ref[...]
        y = ys * sz
        rstd = _group_rstd(y)
        yn = y * rstd
        dov = do_ref[...]
        dyn = dov * nw_ref[...]
        dyg = rstd * (dyn - yn * _group_mean(dyn * yn))
        dy_ref[...] = (dyg * sz).astype(BF16)
        dz_ref[...] = (dyg * ys * (sg * (1.0 + z * (1.0 - sg)))).astype(BF16)
        part = jnp.sum(dov * yn, axis=0, keepdims=True)

        @pl.when(i == 0)
        def _():
            dnw_ref[...] = part

        @pl.when(i > 0)
        def _():
            dnw_ref[...] += part

    row = pl.BlockSpec((tr, n), lambda i: (i, 0))
    vec = pl.BlockSpec((1, n), lambda i: (0, 0))
    return pl.pallas_call(
        body, name="ssm_post_bwd", grid=(t // tr,),
        in_specs=[row, pl.BlockSpec((tr, n), lambda i: (i, COL_Z // n)), vec, row, HBM],
        out_specs=[row, pl.BlockSpec((tr, n), lambda i: (i, COL_Z // n)), vec],
        out_shape=[jax.ShapeDtypeStruct((t, n), BF16), jax.ShapeDtypeStruct(dproj.shape, dproj.dtype),
                   jax.ShapeDtypeStruct((1, n), F32)],
        input_output_aliases={4: 1},
        compiler_params=_params(("arbitrary",)),
    )(y_ssd, proj, nw, dout, dproj)


SCAN_UNROLL = 8
GELU_C = math.sqrt(2.0 / math.pi)
GELU_K = 0.044715


def _gelu_parts(y):
    th = jnp.tanh(GELU_C * (y + GELU_K * y * y * y))
    val = 0.5 * y * (1.0 + th)
    grad = 0.5 * (1.0 + th) + 0.5 * y * (1.0 - th * th) * GELU_C * (1.0 + 3.0 * GELU_K * y * y)
    return val, grad


def _scan_tiles(a_ref, b_ref, h_ref, n_rows, reverse):
    n_tiles = n_rows // SUBLANES
    shape = (SUBLANES, a_ref.shape[1])
    row = _iota(shape, 0)

    def in_tile(av, bv):
        for s in (1, 2, 4):
            if reverse:
                keep = row < SUBLANES - s
                a_sh = jnp.where(keep, pltpu.roll(av, SUBLANES - s, 0), 1.0)
                b_sh = jnp.where(keep, pltpu.roll(bv, SUBLANES - s, 0), 0.0)
            else:
                keep = row >= s
                a_sh = jnp.where(keep, pltpu.roll(av, s, 0), 1.0)
                b_sh = jnp.where(keep, pltpu.roll(bv, s, 0), 0.0)
            bv = av * b_sh + bv
            av = av * a_sh
        return av, bv

    def step(k, carry):
        first = (n_tiles // SCAN_UNROLL - 1 - k) if reverse else k
        tiles = [first * SCAN_UNROLL + j for j in range(SCAN_UNROLL)]
        if reverse:
            tiles = tiles[::-1]
        ats = [pl.ds(pl.multiple_of(tile * SUBLANES, SUBLANES), SUBLANES) for tile in tiles]
        scanned = [in_tile(a_ref[at, :], b_ref[at, :]) for at in ats]
        for at, (av, bv) in zip(ats, scanned):
            hv = bv + av * carry
            h_ref[at, :] = hv
            carry = hv[0:1, :] if reverse else hv[SUBLANES - 1:SUBLANES, :]
        return carry

    assert n_tiles % SCAN_UNROLL == 0, n_rows
    lax.fori_loop(0, n_tiles // SCAN_UNROLL, step, jnp.zeros((1, a_ref.shape[1]), F32))


def _lru_gates(xl, cw, cb, wr, br, wi, bi, lam):
    u = cb + cw[CONV_K - 1:CONV_K, :] * xl
    for k in range(CONV_K - 1):
        u = u + cw[k:k + 1, :] * _shift_down(xl, CONV_K - 1 - k)
    r = _sigmoid(_bdot(u, wr) + br)
    i = _sigmoid(_bdot(u, wi) + bi)
    sp = _softplus(-lam)
    la = -LRU_C * r * sp
    a = jnp.exp(la)
    mult = jnp.sqrt(-jnp.tanh(la) * (a * a + 1.0))
    return u, r, i, sp, a, mult


def _lru_specs(t):
    c0 = COL_LRU // LANES
    xl = pl.BlockSpec((t, LANES), lambda j: (0, c0 + 2 * j))
    yl = pl.BlockSpec((t, LANES), lambda j: (0, c0 + 2 * j + 1))
    col = pl.BlockSpec((t, LANES), lambda j: (0, j))
    cw = pl.BlockSpec((CONV_K, LANES), lambda j: (0, j))
    vec = pl.BlockSpec((1, LANES), lambda j: (0, j))
    wblk = pl.BlockSpec((1, LANES, LANES), lambda j: (j, 0, 0))
    return xl, yl, col, cw, vec, wblk


def _lru_fwd(proj, cw, cb, wr, br, wi, bi, lam, gather):
    t = proj.shape[0]
    xl_s, yl_s, col, cw_s, vec, wblk = _lru_specs(t)
    n = len(gather)

    def body(*refs):
        xl_ref, yl_ref, cw_ref, cb_ref, wr_ref, br_ref, wi_ref, bi_ref, lam_ref = refs[:9]
        o_ref, h_ref = refs[9 + n:11 + n]
        a_s, b_s = refs[11 + 2 * n:13 + 2 * n]
        comm = (refs[9:9 + n], refs[11 + n:11 + 2 * n]) + tuple(refs[13 + 2 * n:])
        j = pl.program_id(0)
        for step, phase in ((0, "start"), (LRU_BLOCKS - 2, "forward")):
            @pl.when(j == step)
            def _():
                _gather_phase(phase, *comm)

        u, r, i, sp, a, mult = _lru_gates(xl_ref[...], cw_ref[...], cb_ref[...], wr_ref[0], br_ref[...],
                                          wi_ref[0], bi_ref[...], lam_ref[...])
        a_s[...] = a
        b_s[...] = mult * (i * u)
        _scan_tiles(a_s, b_s, h_ref, t, reverse=False)
        o_ref[...] = (h_ref[...] * _gelu_parts(yl_ref[...])[0]).astype(BF16)

        @pl.when(j == LRU_BLOCKS - 1)
        def _():
            _gather_phase("finish", *comm)

    out = pl.pallas_call(
        body, name="lru_fwd", grid=(LRU_BLOCKS,),
        in_specs=[xl_s, yl_s, cw_s, vec, wblk, vec, wblk, vec, vec] + [HBM] * n,
        out_specs=[col, col] + [HBM] * n,
        out_shape=[jax.ShapeDtypeStruct((t, LRU_WIDTH), BF16), jax.ShapeDtypeStruct((t, LRU_WIDTH), F32)]
        + [jax.ShapeDtypeStruct((N_DEV,) + v.shape, v.dtype) for v in gather],
        scratch_shapes=[pltpu.VMEM((t, LANES), F32)] * 2 + _comm_scratch(n),
        compiler_params=_params(("arbitrary",), big=True),
    )(proj, proj, cw, cb, wr, br, wi, bi, lam, *gather)
    return out[:2], out[2:]


def _lru_bwd(proj, cw, cb, wr, br, wi, bi, lam, h_all, dout, dproj, exchange):
    t = proj.shape[0]
    xl_s, yl_s, col, cw_s, vec, wblk = _lru_specs(t)
    pair = pl.BlockSpec((t, 2 * LANES), lambda j: (0, COL_LRU // (2 * LANES) + j))
    n_ex = len(exchange)

    def body(*refs):
        xl_ref, yl_ref, cw_ref, cb_ref, wr_ref, br_ref, wi_ref, bi_ref, lam_ref, h_ref, do_ref = refs[:11]
        dxy_ref, dcw_ref, dcb_ref, dwr_ref, dbr_ref, dwi_ref, dbi_ref, dlam_ref = refs[12 + n_ex:20 + n_ex]
        a_s, b_s, g_s = refs[20 + 2 * n_ex:23 + 2 * n_ex]
        comm = (refs[12:12 + n_ex], refs[20 + n_ex:20 + 2 * n_ex]) + tuple(refs[23 + 2 * n_ex:])

        @pl.when(pl.program_id(0) == 0)
        def _():
            _exchange_phase("start", *comm)

        xl = xl_ref[...]
        cwv = cw_ref[...]
        lam = lam_ref[...]
        u, r, i, sp, a, mult = _lru_gates(xl, cwv, cb_ref[...], wr_ref[0], br_ref[...], wi_ref[0], bi_ref[...], lam)
        v = i * u
        gl, dgl = _gelu_parts(yl_ref[...])
        dov = do_ref[...]
        h = h_ref[...]
        dxy_ref[:, LANES:2 * LANES] = (dov * h * dgl).astype(BF16)
        b_s[...] = dov * gl
        a_s[...] = _shift_up(a, 1)
        _scan_tiles(a_s, b_s, g_s, t, reverse=True)
        g = g_s[...]
        da = g * _shift_down(h, 1)
        dmult = g * v
        dv = g * mult
        dla = da * a - dmult * (a * a) / mult
        dr = dla * (-LRU_C * sp)
        dsp = jnp.sum(dla * (-LRU_C * r), axis=0, keepdims=True)
        dlam_ref[...] = -dsp * _sigmoid(-lam)
        dpr = dr * r * (1.0 - r)
        dpi = dv * u * i * (1.0 - i)
        dbr_ref[...] = jnp.sum(dpr, axis=0, keepdims=True)
        dbi_ref[...] = jnp.sum(dpi, axis=0, keepdims=True)
        dwr_ref[0] = _bdot(u, dpr, TN)
        dwi_ref[0] = _bdot(u, dpi, TN)
        du = dv * i + _bdot(dpr, wr_ref[0], NT) + _bdot(dpi, wi_ref[0], NT)
        dxl = cwv[CONV_K - 1:CONV_K, :] * du
        for k in range(CONV_K - 1):
            dxl = dxl + cwv[k:k + 1, :] * _shift_up(du, CONV_K - 1 - k)
        dxy_ref[:, 0:LANES] = dxl.astype(BF16)
        for k in range(CONV_K):
            dcw_ref[k:k + 1, :] = jnp.sum(du * _shift_down(xl, CONV_K - 1 - k), axis=0, keepdims=True)
        dcb_ref[...] = jnp.sum(du, axis=0, keepdims=True)

        @pl.when(pl.program_id(0) == LRU_BLOCKS - 1)
        def _():
            _exchange_phase("finish", *comm)

    out = pl.pallas_call(
        body, name="lru_bwd", grid=(LRU_BLOCKS,),
        in_specs=[xl_s, yl_s, cw_s, vec, wblk, vec, wblk, vec, vec, col, col, HBM] + [HBM] * n_ex,
        out_specs=[pair, cw_s, vec, wblk, vec, wblk, vec, vec] + [HBM] * n_ex,
        input_output_aliases={11: 0},
        out_shape=[jax.ShapeDtypeStruct(dproj.shape, dproj.dtype),
                   jax.ShapeDtypeStruct((CONV_K, LRU_WIDTH), F32), jax.ShapeDtypeStruct((1, LRU_WIDTH), F32),
                   jax.ShapeDtypeStruct((LRU_BLOCKS, LANES, LANES), F32), jax.ShapeDtypeStruct((1, LRU_WIDTH), F32),
                   jax.ShapeDtypeStruct((LRU_BLOCKS, LANES, LANES), F32), jax.ShapeDtypeStruct((1, LRU_WIDTH), F32),
                   jax.ShapeDtypeStruct((1, LRU_WIDTH), F32)]
        + [jax.ShapeDtypeStruct(p.shape, p.dtype) for p in exchange],
        scratch_shapes=[pltpu.VMEM((t, LANES), F32)] * 3 + _comm_scratch(n_ex),
        compiler_params=_params(("arbitrary",), big=True),
    )(proj, proj, cw, cb, wr, br, wi, bi, lam, h_all, dout, dproj, *exchange)
    return out[:8], out[8:]


def _mesh_pos():
    return lax.axis_index("x"), lax.axis_index("y"), lax.axis_index("c")


HBM = pl.BlockSpec(memory_space=pl.ANY)


def _comm_scratch(n):
    return [pltpu.SemaphoreType.DMA((n, 7)), pltpu.SemaphoreType.DMA((n, 7)), pltpu.SemaphoreType.DMA((n,))]


def _gather_phase(phase, v_refs, out_refs, send_sems, recv_sems, local_sems):
    n = len(v_refs)
    x, y, c = _mesh_pos()
    me, sibling = (x, y, c), (x, y, 1 - c)
    chips = [(1 - x, y), (x, 1 - y), (1 - x, 1 - y)]

    def block(a, px, py, pc):
        return out_refs[a].at[4 * px + 2 * py + pc]

    def copy(a, k, blk, to, src=None):
        return pltpu.make_async_remote_copy(
            src_ref=block(a, *blk) if src is None else src, dst_ref=block(a, *blk),
            send_sem=send_sems.at[a, k], recv_sem=recv_sems.at[a, k], device_id=to, device_id_type=MESH)

    def own(a):
        return pltpu.make_async_copy(v_refs[a], block(a, *me), local_sems.at[a])

    def first(a):
        return ([copy(a, 0, me, sibling, src=v_refs[a])]
                + [copy(a, 1 + j, me, (*chip, c), src=v_refs[a]) for j, chip in enumerate(chips)])

    def forward(a, j):
        return copy(a, 4 + j, (*chips[j], c), sibling)

    if phase == "start":
        for a in range(n):
            own(a).start()
        for a in range(n):
            for cp in first(a):
                cp.start()
    elif phase == "forward":
        for j in range(3):
            for a in range(n):
                copy(a, 1 + j, (*chips[j], c), me).wait_recv()
                forward(a, j).start()
    else:
        for a in range(n):
            copy(a, 0, sibling, me).wait_recv()
            for j in range(3):
                copy(a, 4 + j, (*chips[j], 1 - c), me).wait_recv()
        for a in range(n):
            for cp in first(a) + [forward(a, j) for j in range(3)]:
                cp.wait_send()
            own(a).wait()


def _gather_hosted(phase, *comm):
    _gather_phase("forward" if phase == "late" else phase, *comm)


def _all_gather(vs, name):
    n = len(vs)

    def body(*refs):
        comm = (refs[:n], refs[n:2 * n]) + tuple(refs[2 * n:])
        for phase in ("start", "forward", "finish"):
            _gather_phase(phase, *comm)

    return pl.pallas_call(
        body, name=name,
        out_shape=[jax.ShapeDtypeStruct((N_DEV,) + v.shape, v.dtype) for v in vs],
        in_specs=[HBM] * n, out_specs=[HBM] * n, scratch_shapes=_comm_scratch(n),
    )(*vs)


def _run_copies(phase, local, remote):
    if phase == "start":
        for cp in local + remote:
            cp.start()
    else:
        for cp in remote:
            cp.wait()
        for cp in local:
            cp.wait()


def _exchange_phase(phase, p_refs, out_refs, send_sems, recv_sems, local_sems):
    if phase == "late":
        return
    n = len(p_refs)
    x, y, c = _mesh_pos()
    me = 4 * x + 2 * y + c
    local = [pltpu.make_async_copy(p_refs[a].at[me], out_refs[a].at[me], local_sems.at[a]) for a in range(n)]
    remote = []
    for k in range(1, N_DEV):
        px = (1 - x) if k & 4 else x
        py = (1 - y) if k & 2 else y
        pc = (1 - c) if k & 1 else c
        for a in range(n):
            remote.append(pltpu.make_async_remote_copy(
                src_ref=p_refs[a].at[4 * px + 2 * py + pc], dst_ref=out_refs[a].at[me],
                send_sem=send_sems.at[a, k - 1], recv_sem=recv_sems.at[a, k - 1],
                device_id=(px, py, pc), device_id_type=MESH))
    _run_copies(phase, local, remote)


def _chip_exchange_phase(phase, p_refs, out_refs, send_sems, recv_sems, local_sems):
    if phase == "late":
        return
    n = len(p_refs)
    x, y, c = _mesh_pos()
    me = 2 * x + y
    local = [pltpu.make_async_copy(p_refs[a].at[me], out_refs[a].at[me], local_sems.at[a]) for a in range(n)]
    remote = []
    for k in range(1, 4):
        px = (1 - x) if k & 2 else x
        py = (1 - y) if k & 1 else y
        for a in range(n):
            remote.append(pltpu.make_async_remote_copy(
                src_ref=p_refs[a].at[2 * px + py], dst_ref=out_refs[a].at[me],
                send_sem=send_sems.at[a, k - 1], recv_sem=recv_sems.at[a, k - 1],
                device_id=(px, py, c), device_id_type=MESH))
    _run_copies(phase, local, remote)


def _sibling_exchange(parts, name):
    chips = N_DEV // 2

    def body(p_ref, out_ref, send_sems, recv_sems):
        x, y, c = _mesh_pos()
        copies = [pltpu.make_async_remote_copy(
            src_ref=p_ref.at[2 * q + 1 - c], dst_ref=out_ref.at[q], send_sem=send_sems.at[q], recv_sem=recv_sems.at[q],
            device_id=(x, y, 1 - c), device_id_type=MESH) for q in range(chips)]
        _run_copies("start", [], copies)
        _run_copies("finish", [], copies)

    return pl.pallas_call(
        body, name=name, out_shape=jax.ShapeDtypeStruct((chips,) + parts.shape[1:], parts.dtype),
        in_specs=[HBM], out_specs=HBM,
        scratch_shapes=[pltpu.SemaphoreType.DMA((chips,)), pltpu.SemaphoreType.DMA((chips,))],
    )(parts)


def _pair_sum(parts, theirs, name):
    slots, rows, cols = theirs.shape
    tc = 256
    core = lax.axis_index("c").astype(jnp.int32).reshape(1)

    def body(c_ref, a_ref, b_ref, o_ref):
        o_ref[...] = (a_ref[...].astype(F32) + b_ref[...].astype(F32)).astype(o_ref.dtype)

    spec = pl.BlockSpec((1, rows, tc), lambda q, j, c: (q, 0, j))
    return pl.pallas_call(
        body, name=name,
        grid_spec=pltpu.PrefetchScalarGridSpec(
            num_scalar_prefetch=1, grid=(slots, cols // tc),
            in_specs=[pl.BlockSpec((1, rows, tc), lambda q, j, c: (2 * q + c[0], 0, j)), spec], out_specs=spec),
        out_shape=jax.ShapeDtypeStruct(theirs.shape, theirs.dtype),
        compiler_params=_params(("parallel", "parallel")),
    )(core, parts, theirs)


def _sum_sources(recvs, name):
    k = len(recvs)

    def body(*refs):
        for r_ref, o_ref in zip(refs[:k], refs[k:]):
            acc = r_ref[0].astype(F32)
            for s in range(1, r_ref.shape[0]):
                acc = acc + r_ref[s].astype(F32)
            o_ref[...] = acc

    return pl.pallas_call(
        body, name=name, out_shape=[jax.ShapeDtypeStruct(r.shape[1:], F32) for r in recvs],
        compiler_params=_params(),
    )(*recvs)


def _row_tile(rows):
    for tile in range(512, 15, -16):
        if rows % tile == 0:
            return tile
    return rows


def _adam_update(w, g, m, v):
    nm = ADAM_B1 * m + (1.0 - ADAM_B1) * g
    nv = ADAM_B2 * v + (1.0 - ADAM_B2) * (g * g)
    m_hat = nm / (1.0 - ADAM_B1 ** ADAM_STEP)
    v_hat = nv / (1.0 - ADAM_B2 ** ADAM_STEP)
    return -ADAM_LR * (m_hat / (jnp.sqrt(v_hat) + ADAM_EPS) + ADAM_WD * w), nm, nv


def _vector_offsets(widths):
    offsets, end = [], 0
    for c in widths:
        offsets.append(end)
        end += c + (-c) % LANES
    return offsets, end


def _adamw_small(vec_parts, vec_state, mat_grads, mat_state):
    widths = [w.shape[1] for w, _, _ in vec_state]
    offsets, total = _vector_offsets(widths)
    assert vec_parts.shape == (N_DEV, total), (vec_parts.shape, total)
    n_vec, n_mat = len(vec_state), len(mat_state)

    def body(*refs):
        r_ref = refs[0]
        vec_in = refs[1:1 + 3 * n_vec]
        mat_in = refs[1 + 3 * n_vec:1 + 3 * n_vec + 4 * n_mat]
        outs = refs[1 + 3 * n_vec + 4 * n_mat:]
        for i, (off, c) in enumerate(zip(offsets, widths)):
            g = r_ref[0:1, off:off + c]
            for s in range(1, N_DEV):
                g = g + r_ref[s:s + 1, off:off + c]
            w_ref, m_ref, v_ref = vec_in[3 * i:3 * i + 3]
            g_out, d_out, m_out, v_out = outs[4 * i:4 * i + 4]
            g_out[...] = g
            d_out[...], m_out[...], v_out[...] = _adam_update(w_ref[...], g, m_ref[...], v_ref[...])
        for j in range(n_mat):
            g_ref, w_ref, m_ref, v_ref = mat_in[4 * j:4 * j + 4]
            d_out, m_out, v_out = outs[4 * n_vec + 3 * j:4 * n_vec + 3 * j + 3]
            d_out[...], m_out[...], v_out[...] = _adam_update(w_ref[...], g_ref[...], m_ref[...], v_ref[...])

    args = [vec_parts] + [a for state in vec_state for a in state]
    for g, state in zip(mat_grads, mat_state):
        args += [g, *state]
    out_shape = [jax.ShapeDtypeStruct(w.shape, F32) for w, _, _ in vec_state for _ in range(4)]
    out_shape += [jax.ShapeDtypeStruct(w.shape, F32) for w, _, _ in mat_state for _ in range(3)]
    out = pl.pallas_call(body, name="adamw_replicated", out_shape=out_shape, compiler_params=_params())(*args)
    vec_out = [tuple(out[4 * i:4 * i + 4]) for i in range(n_vec)]
    mat_out = [tuple(out[4 * n_vec + 3 * j:4 * n_vec + 3 * j + 3]) for j in range(n_mat)]
    return vec_out, mat_out


def _adamw(w, recv, m, v, name):
    rows, width = w.shape
    n = recv.shape[0]
    if rows % 16 == 0 or width % 256:
        tr, tc = _row_tile(rows), width
    else:
        tr, tc = rows, 256

    def body(w_ref, r_ref, m_ref, v_ref, g_ref, d_ref, nm_ref, nv_ref):
        gv = r_ref[0].astype(F32)
        for s in range(1, n):
            gv = gv + r_ref[s].astype(F32)
        g_ref[...] = gv
        d_ref[...], nm_ref[...], nv_ref[...] = _adam_update(w_ref[...], gv, m_ref[...], v_ref[...])

    spec = pl.BlockSpec((tr, tc), lambda i, j: (i, j))
    shape = jax.ShapeDtypeStruct((rows, width), F32)
    return pl.pallas_call(
        body, name=name, grid=(rows // tr, width // tc),
        in_specs=[spec, pl.BlockSpec((n, tr, tc), lambda i, j: (0, i, j)), spec, spec],
        out_specs=[spec] * 4, out_shape=[shape] * 4,
        compiler_params=_params(("parallel", "parallel")),
    )(w, recv, m, v)


BIG_NAMES = ("w_in", "w_out_ssm", "w_out_lru", "w_out", "w_ffn_in", "w_ffn_out", "ssm_conv_w", "lru_conv_w")
TRANSPOSED = ("w_in", "w_ffn_in")
CONV_NAMES = ("ssm_conv_w", "lru_conv_w")
MATMUL_NAMES = BIG_NAMES[:6]
NEEDED_FIRST = ("w_in", "ssm_conv_w", "lru_conv_w")
GATHERED_IN_PROJ = ("w_ffn_in", "w_ffn_out")
GATHERED_IN_SSD = ("w_out_ssm",)
GATHERED_IN_LRU = ("w_out_lru", "w_out")
EXCHANGED_IN_SSD = ("w_ffn_in", "w_ffn_out")
EXCHANGED_IN_LRU = ("w_out_ssm", "w_out_lru", "w_out")
SMALL_VECTORS = ("norm1_w", "b_branch_gate", "ssm_conv_b", "ssm_dt_bias", "ssm_a_log", "ssm_d", "ssm_norm_w",
                 "lru_conv_b", "lru_b_r", "lru_b_i", "lru_lambda", "norm2_w", "norm_f_w")
SMALL_MATRICES = ("lru_w_r", "lru_w_i")


def _col_shards(full):
    rows, cols = full.shape
    return full.reshape(rows, N_DEV, cols // N_DEV).transpose(1, 0, 2)


def _from_col_shards(g):
    n, rows, w = g.shape
    return g.transpose(1, 0, 2).reshape(rows, n * w)


def kernel(x, norm1_w, w_in, b_branch_gate, ssm_conv_w, ssm_conv_b, ssm_dt_bias, ssm_a_log, ssm_d, ssm_norm_w, w_out_ssm, lru_conv_w, lru_conv_b, lru_w_r, lru_b_r, lru_w_i, lru_b_i, lru_lambda, w_out_lru, w_out, norm2_w, w_ffn_in, w_ffn_out, norm_f_w, loss_target, m_norm1_w, m_w_in, m_b_branch_gate, m_ssm_conv_w, m_ssm_conv_b, m_ssm_dt_bias, m_ssm_a_log, m_ssm_d, m_ssm_norm_w, m_w_out_ssm, m_lru_conv_w, m_lru_conv_b, m_lru_w_r, m_lru_b_r, m_lru_w_i, m_lru_b_i, m_lru_lambda, m_w_out_lru, m_w_out, m_norm2_w, m_w_ffn_in, m_w_ffn_out, m_norm_f_w, v_norm1_w, v_w_in, v_b_branch_gate, v_ssm_conv_w, v_ssm_conv_b, v_ssm_dt_bias, v_ssm_a_log, v_ssm_d, v_ssm_norm_w, v_w_out_ssm, v_lru_conv_w, v_lru_conv_b, v_lru_w_r, v_lru_b_r, v_lru_w_i, v_lru_b_i, v_lru_lambda, v_w_out_lru, v_w_out, v_norm2_w, v_w_ffn_in, v_w_ffn_out, v_norm_f_w):
    given = dict(locals())
    weights = {n: given[n] for n in BIG_NAMES + SMALL_VECTORS + SMALL_MATRICES}
    t = x.shape[1]
    xt = x[0]
    tgt = loss_target[0]

    def local(n, a):
        return a[0].T if n in TRANSPOSED else a[0]

    def as_output(n, a):
        return a.T[None] if n in TRANSPOSED else a[None]

    def shard(n):
        s = local(n, weights[n])
        return s.astype(BF16) if n in MATMUL_NAMES else s

    def unshard(n, g):
        return _from_col_shards(g) if n in CONV_NAMES else g.reshape(-1, g.shape[-1])

    def grad_slices(n):
        g = grads[n]
        return (_col_shards(g) if n in CONV_NAMES else g.reshape(N_DEV, -1, g.shape[-1])).astype(BF16)

    gathered = _all_gather([shard(n) for n in NEEDED_FIRST], "gather_in_weights")
    full = {n: unshard(n, g) for n, g in zip(NEEDED_FIRST, gathered)}
    ssm_cw, lru_cw = full["ssm_conv_w"], full["lru_conv_w"]
    wi_t = full["w_in"]
    lru_rows = wi_t[ORIG_LRU_X:].reshape(2, LRU_BLOCKS, LANES, D_MODEL).transpose(1, 0, 2, 3)
    w_tail = jnp.concatenate([wi_t[W_IN_MAIN:ORIG_DT], lru_rows.reshape(2 * LRU_WIDTH, D_MODEL),
                              wi_t[ORIG_DT:ORIG_LRU_X], jnp.zeros((PROJ_W - IN_PROJ, D_MODEL), BF16)], axis=0)

    def pad_heads(a):
        return jnp.pad(a.reshape(1, SSM_HEADS), ((0, 0), (0, LANES - SSM_HEADS)))

    dt_bias_p = pad_heads(ssm_dt_bias)
    a_log_p = pad_heads(ssm_a_log)
    d_exp = jnp.repeat(ssm_d.reshape(SSM_HEADS), SSM_HEAD_DIM).reshape(1, SSM_INNER)
    lru_wr, lru_wi = lru_w_r[0], lru_w_i[0]

    hn1 = _rmsnorm_fwd(xt, norm1_w, "norm1_fwd")
    later = [shard(n) for n in GATHERED_IN_PROJ]
    proj, (gathered,) = _mm(
        hn1, wi_t, tb=True, b_tail=w_tail, b_main=W_IN_MAIN, name="in_proj",
        hosted=[(_gather_hosted, later, [jax.ShapeDtypeStruct((N_DEV,) + v.shape, v.dtype) for v in later])])
    full.update({n: unshard(n, g) for n, g in zip(GATHERED_IN_PROJ, gathered)})
    xbc_act = _ssm_conv_fwd(proj, ssm_cw, ssm_conv_b)
    (y_ssd, s_in_all), gathered = _ssd_fwd(xbc_act, proj, dt_bias_p, a_log_p, d_exp,
                                           gather=[shard(n) for n in GATHERED_IN_SSD])
    full.update({n: unshard(n, g) for n, g in zip(GATHERED_IN_SSD, gathered)})
    (l_out, h_lru), gathered = _lru_fwd(proj, lru_cw, lru_conv_b, lru_wr, lru_b_r, lru_wi, lru_b_i, lru_lambda,
                                        gather=[shard(n) for n in GATHERED_IN_LRU])
    full.update({n: unshard(n, g) for n, g in zip(GATHERED_IN_LRU, gathered)})
    y_pre = _ssm_post_fwd(y_ssd, proj, ssm_norm_w)
    y_ssm = _mm(y_pre, full["w_out_ssm"], out_dtype=BF16, name="out_ssm")
    y_lru = _mm(l_out, full["w_out_lru"], out_dtype=BF16, name="out_lru")
    merged = _merge_fwd(proj, b_branch_gate, y_ssm, y_lru)
    h1, hn2 = _mm(merged, full["w_out"], epilogue=_residual_norm_epilogue(xt, norm2_w), name="out_proj")
    gate, up, act = _ffn_in_swiglu(hn2, full["w_ffn_in"])

    grads = {}
    dh2, grads["norm_f_w"], loss_cols = _mm(
        act, full["w_ffn_out"], epilogue=_loss_epilogue(h1, norm_f_w.reshape(1, D_MODEL), tgt), name="ffn_out")
    loss = lax.psum(0.5 * jnp.sum(loss_cols) / D_MODEL, AXES)
    dact = _mm(dh2, full["w_ffn_out"], tb=True, out_dtype=BF16, name="d_act")
    grads["w_ffn_out"] = _mm(act, dh2, ta=True, out_dtype=BF16, name="dw_ffn_out")
    dgu = _swiglu_bwd(gate, up, dact)
    dh1, grads["norm2_w"] = _mm(dgu, full["w_ffn_in"], epilogue=_norm_bwd_epilogue(h1, norm2_w, dh2), name="d_hn2")
    grads["w_ffn_in"] = _mm(dgu, hn2, ta=True, out_dtype=BF16, name="dw_ffn_in")
    dmerged = _mm(dh1, full["w_out"], tb=True, out_dtype=BF16, name="d_merged")
    grads["w_out"] = _mm(merged, dh1, ta=True, out_dtype=BF16, name="dw_out")
    dy_ssm, dy_lru, dproj, grads["b_branch_gate"] = _merge_bwd(proj, b_branch_gate, y_ssm, y_lru, dmerged)
    dy_pre = _mm(dy_ssm, full["w_out_ssm"], tb=True, out_dtype=BF16, name="d_y_pre")
    grads["w_out_ssm"] = _mm(y_pre, dy_ssm, ta=True, out_dtype=BF16, name="dw_out_ssm")
    dl_out = _mm(dy_lru, full["w_out_lru"], tb=True, out_dtype=BF16, name="d_l_out")
    grads["w_out_lru"] = _mm(l_out, dy_lru, ta=True, out_dtype=BF16, name="dw_out_lru")
    dy_ssd, dproj, grads["ssm_norm_w"] = _ssm_post_bwd(y_ssd, proj, ssm_norm_w, dy_pre, dproj)
    (dxbc_act, dproj, dbias, dalog, ddcol), recv_in_ssd = _ssd_bwd(
        xbc_act, proj, s_in_all, dy_ssd, dt_bias_p, a_log_p, d_exp, dproj,
        exchange=[grad_slices(n) for n in EXCHANGED_IN_SSD])
    grads["ssm_dt_bias"] = dbias[:, :SSM_HEADS]
    grads["ssm_a_log"] = dalog[:, :SSM_HEADS]
    grads["ssm_d"] = ddcol.reshape(SSM_HEADS, SSM_HEAD_DIM).sum(axis=1).reshape(1, SSM_HEADS)
    dproj, grads["ssm_conv_w"], grads["ssm_conv_b"] = _ssm_conv_bwd(proj, ssm_cw, ssm_conv_b, dxbc_act, dproj)
    ((dproj, grads["lru_conv_w"], grads["lru_conv_b"], dwr, grads["lru_b_r"], dwi, grads["lru_b_i"],
      grads["lru_lambda"]), recv_in_lru) = _lru_bwd(
        proj, lru_cw, lru_conv_b, lru_wr, lru_b_r, lru_wi, lru_b_i, lru_lambda, h_lru, dl_out, dproj,
        exchange=[grad_slices(n) for n in EXCHANGED_IN_LRU])
    grads["lru_w_r"], grads["lru_w_i"] = dwr[None], dwi[None]
    dwpt = _mm(dproj, hn1, ta=True, out_dtype=BF16, name="dw_in")
    lru_rows = dwpt[COL_LRU:COL_DT].reshape(LRU_BLOCKS, 2, LANES, D_MODEL).transpose(1, 0, 2, 3)
    grads["w_in"] = jnp.concatenate([dwpt[:ORIG_DT], dwpt[COL_DT:COL_DT + SSM_HEADS],
                                     lru_rows.reshape(2 * LRU_WIDTH, D_MODEL)], axis=0)
    w_in_parts = grad_slices("w_in")
    from_sibling = _sibling_exchange(w_in_parts, "sibling_exchange_dw_in")
    chip_parts = [_pair_sum(w_in_parts, from_sibling, "pair_sum_dw_in")]
    direct = [grad_slices(n) for n in CONV_NAMES] + [grads[n].reshape(N_DEV, -1, LANES) for n in SMALL_MATRICES]
    (grad_x, grads["norm1_w"]), (recv_w_in, recv_direct) = _mm(
        dproj, wi_t, b_tail=w_tail, b_main=W_IN_MAIN, epilogue=_norm_bwd_epilogue(xt, norm1_w, dh1), name="d_hn1",
        hosted=[(_chip_exchange_phase, chip_parts, chip_parts), (_exchange_phase, direct, direct)])
    recv_first = list(recv_w_in) + list(recv_direct[:len(CONV_NAMES)])
    recv_mats = recv_direct[len(CONV_NAMES):]

    def as_rows(n, a):
        return a.reshape(1, -1) if n in SMALL_VECTORS else a.reshape(-1, LANES)

    vec_g = jnp.concatenate([jnp.pad(as_rows(n, grads[n]), ((0, 0), (0, (-grads[n].size) % LANES)))
                             for n in SMALL_VECTORS], axis=1)
    gathered = _all_gather([vec_g] + list(_sum_sources(recv_mats, "sum_gate_matrix_grads")), "gather_small_grads")
    vec_parts = gathered[0].reshape(N_DEV, -1)
    mat_g = [g.reshape(-1, LANES) for g in gathered[1:]]

    recv = dict(zip(EXCHANGED_IN_SSD + EXCHANGED_IN_LRU + NEEDED_FIRST,
                    list(recv_in_ssd) + list(recv_in_lru) + list(recv_first)))
    big_out = {n: _adamw(local(n, weights[n]), recv[n], local(n, given["m_" + n]), local(n, given["v_" + n]),
                         "adamw_" + n) for n in BIG_NAMES}

    def state(n):
        return tuple(as_rows(n, given[p + n]) for p in ("", "m_", "v_"))

    vec_out, mat_out = _adamw_small(vec_parts, [state(n) for n in SMALL_VECTORS],
                                    mat_g, [state(n) for n in SMALL_MATRICES])
    small_out = dict(zip(SMALL_VECTORS, vec_out))
    small_out.update({n: (g,) + out for n, g, out in zip(SMALL_MATRICES, mat_g, mat_out)})

    order = list(given)[1:24]
    results = []
    for q in range(4):
        vals = {n: as_output(n, big_out[n][q]) for n in BIG_NAMES}
        vals.update({n: out[q].reshape(weights[n].shape) for n, out in small_out.items()})
        results.extend(vals[n] for n in order)
    return (loss, grad_x[None], *results)
```

```python
import math

import jax
import jax.numpy as jnp
from jax import lax
from jax.experimental import pallas as pl
from jax.experimental.pallas import tpu as pltpu

F32 = jnp.float32
BF16 = jnp.bfloat16
HIGHEST = lax.Precision.HIGHEST
MESH = pl.DeviceIdType.MESH
AXES = ("x", "y", "c")
N_DEV = 8

D_MODEL = 1024
SSM_INNER = 2048
SSM_HEADS = 32
SSM_HEAD_DIM = 64
SSM_GROUPS = 4
SSM_STATE = 128
SSM_BC = SSM_GROUPS * SSM_STATE
SSM_CONV_DIM = SSM_INNER + 2 * SSM_BC
SSM_CHUNK = 128
SSM_PAIRS = SSM_HEADS // 2
CONV_K = 4
LRU_WIDTH = 1280
LRU_BLOCKS = 10
LRU_C = 8.0
FFN_HIDDEN = 2816
RMS_EPS = 1e-6
IN_PROJ = 9760

COL_GATES = 0
COL_Z = 2048
COL_XBC = 4096
COL_LRU = 7168
COL_DT = 9728
PROJ_W = 9856
ORIG_DT = 7168
ORIG_LRU_X = 7200
ORIG_LRU_Y = 8480
W_IN_MAIN = 7040

ADAM_LR = 0.001
ADAM_B1 = 0.9
ADAM_B2 = 0.999
ADAM_EPS = 1e-08
ADAM_WD = 0.01
ADAM_STEP = 10

LANES = 128
SUBLANES = 8
V7X_VMEM_BYTES = 64 * 1024 * 1024
VMEM_LIMIT = V7X_VMEM_BYTES * 3 // 4
VMEM_LIMIT_BIG = V7X_VMEM_BYTES * 15 // 16

NT = (((1,), (1,)), ((), ()))
TN = (((0,), (0,)), ((), ()))


def _params(sem=None, big=False):
    return pltpu.CompilerParams(dimension_semantics=sem,
                                vmem_limit_bytes=VMEM_LIMIT_BIG if big else VMEM_LIMIT)


def _blk(dim, cap):
    if dim <= cap:
        return dim
    for m in range(cap // LANES, 0, -1):
        if dim % (m * LANES) == 0:
            return m * LANES
    raise ValueError(f"no block for {dim}")


def _rows(t, cap=256):
    return min(t, cap)


def _sigmoid(v):
    return 1.0 / (1.0 + jnp.exp(-v))


def _softplus(v):
    e = jnp.exp(-jnp.abs(v))
    u = 1.0 + e
    log1p = jnp.where(u == 1.0, e, jnp.log(u) * e / jnp.where(u == 1.0, 1.0, u - 1.0))
    return jnp.maximum(v, 0.0) + log1p


def _iota(shape, dim):
    return lax.broadcasted_iota(jnp.int32, shape, dim)


def _shift_down(v, s):
    if s == 0:
        return v
    return jnp.where(_iota(v.shape, 0) >= s, pltpu.roll(v, s, 0), 0.0)


def _shift_up(v, s):
    if s == 0:
        return v
    n = v.shape[0]
    return jnp.where(_iota(v.shape, 0) < n - s, pltpu.roll(v, n - s, 0), 0.0)


def _bdot(a, b, dn=None):
    a = a.astype(BF16)
    b = b.astype(BF16)
    if dn is None:
        return jnp.dot(a, b, preferred_element_type=F32)
    return lax.dot_general(a, b, dn, preferred_element_type=F32)


def _split_dot(a, e, dn=None, exact=True):
    hi = a.astype(BF16)
    if not exact:
        return _bdot(hi, e, dn)
    lo = (a - hi.astype(F32)).astype(BF16)
    return _bdot(hi, e, dn) + _bdot(lo, e, dn)


def _fdot(a, b, dn=None):
    if dn is None:
        return jnp.dot(a, b, precision=HIGHEST, preferred_element_type=F32)
    return lax.dot_general(a, b, dn, precision=HIGHEST, preferred_element_type=F32)


def _mm(a, b, *, ta=False, tb=False, add=None, hosted=(), out_dtype=F32, epilogue=None, b_tail=None, b_main=0, name):
    if ta:
        kdim, m = a.shape
    else:
        m, kdim = a.shape
    if tb:
        n, k2 = b.shape
    else:
        k2, n = b.shape
    if b_tail is not None:
        if tb:
            n = b_main + b_tail.shape[0]
        else:
            k2 = b_main + b_tail.shape[0]
    assert kdim == k2, (a.shape, b.shape, ta, tb)
    if epilogue is None:
        rows, vecs, row_dtypes, n_vec_out = ([] if add is None else [add]), [], [out_dtype], 0

        def finish(r, row_vals, vec_vals):
            return ((r + row_vals[0]) if row_vals else r,), ()
    else:
        assert add is None
        finish, rows, vecs, row_dtypes, n_vec_out = epilogue
    bm, bn, bk = _blk(m, 1408 if epilogue is None else 512), _blk(n, 1408), _blk(kdim, 1408)
    grid = (m // bm, n // bn, kdim // bk)
    nk = grid[2]
    assert n_vec_out == 0 or grid[1] == 1, "column sums are accumulated over the row tiles of whole rows"
    dn = (((0 if ta else 1,), (1 if tb else 0,)), ((), ()))
    n_ab = 2 if b_tail is None else 3
    main_blocks = b_main // (bn if tb else bk)
    assert b_main % (bn if tb else bk) == 0
    n_in = n_ab + len(rows) + len(vecs)
    n_out = len(row_dtypes) + n_vec_out
    sizes = [len(arrays) for _, arrays, _ in hosted]
    n_ex = sum(sizes)

    def body(*refs):
        a_ref, b_ref = refs[:2]
        row_refs, vec_refs = refs[n_ab:n_ab + len(rows)], refs[n_ab + len(rows):n_in]
        out_refs = refs[n_in + n_ex:n_in + n_ex + n_out]
        acc = refs[n_in + 2 * n_ex + n_out]
        comms, at = [], 0
        for g, size in enumerate(sizes):
            sems = refs[n_in + 2 * n_ex + n_out + 1 + 3 * g:n_in + 2 * n_ex + n_out + 4 + 3 * g]
            comms.append((refs[n_in + at:n_in + at + size],
                          refs[n_in + n_ex + n_out + at:n_in + n_ex + n_out + at + size]) + tuple(sems))
            at += size
        step = (pl.program_id(0) * grid[1] + pl.program_id(1)) * nk + pl.program_id(2)
        k = pl.program_id(2)
        if n_ex:
            @pl.when(step == 0)
            def _():
                for (phase_fn, _, _), comm in zip(hosted, comms):
                    phase_fn("start", *comm)

            @pl.when(step == (7 * grid[0] * grid[1] * nk) // 8)
            def _():
                for (phase_fn, _, _), comm in zip(hosted, comms):
                    phase_fn("late", *comm)

        @pl.when(k == 0)
        def _():
            acc[...] = jnp.zeros_like(acc)

        bv = b_ref[...]
        if b_tail is not None:
            bv = jnp.where(pl.program_id(1 if tb else 2) < main_blocks, bv, refs[2][...])
        acc[...] += lax.dot_general(a_ref[...].astype(BF16), bv.astype(BF16), dn, preferred_element_type=F32)

        @pl.when(k == nk - 1)
        def _():
            row_outs, col_sums = finish(acc[...], [r[...] for r in row_refs], [v[...] for v in vec_refs])
            for o_ref, val in zip(out_refs, row_outs):
                o_ref[...] = val.astype(o_ref.dtype)
            for o_ref, val in zip(out_refs[len(row_dtypes):], col_sums):
                @pl.when(pl.program_id(0) == 0)
                def _():
                    o_ref[...] = val

                @pl.when(pl.program_id(0) > 0)
                def _():
                    o_ref[...] += val

        if n_ex:
            @pl.when(step == grid[0] * grid[1] * nk - 1)
            def _():
                for (phase_fn, _, _), comm in zip(hosted, comms):
                    phase_fn("finish", *comm)

    a_spec = pl.BlockSpec((bk, bm), lambda i, j, k: (k, i)) if ta else pl.BlockSpec((bm, bk), lambda i, j, k: (i, k))
    b_specs = [pl.BlockSpec((bn, bk), lambda i, j, k: (j, k)) if tb else pl.BlockSpec((bk, bn), lambda i, j, k: (k, j))]
    if b_tail is not None:
        last = main_blocks - 1
        if tb:
            b_specs = [pl.BlockSpec((bn, bk), lambda i, j, k: (jnp.minimum(j, last), k)),
                       pl.BlockSpec((bn, bk), lambda i, j, k: (jnp.maximum(j - main_blocks, 0), k))]
        else:
            b_specs = [pl.BlockSpec((bk, bn), lambda i, j, k: (jnp.minimum(k, last), j)),
                       pl.BlockSpec((bk, bn), lambda i, j, k: (jnp.maximum(k - main_blocks, 0), j))]
    o_spec = pl.BlockSpec((bm, bn), lambda i, j, k: (i, j))
    v_spec = pl.BlockSpec((1, bn), lambda i, j, k: (0, j))
    in_specs = [a_spec] + b_specs + [o_spec] * len(rows) + [v_spec] * len(vecs) + [HBM] * n_ex
    args = ([a, b] + ([] if b_tail is None else [b_tail]) + list(rows) + list(vecs)
            + [p for _, arrays, _ in hosted for p in arrays])
    sequential = n_ex or n_vec_out
    out = pl.pallas_call(
        body, name=name, grid=grid,
        in_specs=in_specs, out_specs=[o_spec] * len(row_dtypes) + [v_spec] * n_vec_out + [HBM] * n_ex,
        out_shape=[jax.ShapeDtypeStruct((m, n), dt) for dt in row_dtypes]
        + [jax.ShapeDtypeStruct((1, n), F32)] * n_vec_out
        + [jax.ShapeDtypeStruct(r.shape, r.dtype) for _, _, results in hosted for r in results],
        scratch_shapes=[pltpu.VMEM((bm, bn), F32)] + [s for size in sizes for s in _comm_scratch(size)],
        compiler_params=_params(("arbitrary",) * 3 if sequential else ("parallel", "parallel", "arbitrary")),
    )(*args)
    result = out[0] if n_out == 1 else tuple(out[:n_out])
    if not n_ex:
        return result
    received, at = [], n_out
    for size in sizes:
        received.append(out[at:at + size])
        at += size
    return result, received


def _rmsnorm_fwd(x, w, name):
    t, d = x.shape
    tr = _rows(t, 512)

    def body(x_ref, w_ref, o_ref):
        xv = x_ref[...]
        rstd = lax.rsqrt(jnp.mean(xv * xv, axis=-1, keepdims=True) + RMS_EPS)
        o_ref[...] = (xv * rstd * w_ref[...]).astype(BF16)

    return pl.pallas_call(
        body, name=name, grid=(t // tr,),
        in_specs=[pl.BlockSpec((tr, d), lambda i: (i, 0)), pl.BlockSpec((1, d), lambda i: (0, 0))],
        out_specs=pl.BlockSpec((tr, d), lambda i: (i, 0)),
        out_shape=jax.ShapeDtypeStruct((t, d), BF16),
        compiler_params=_params(("parallel",)),
    )(x, w)


def _normalize(h):
    rstd = lax.rsqrt(jnp.mean(h * h, axis=-1, keepdims=True) + RMS_EPS)
    return rstd, h * rstd


def _residual_norm_epilogue(x, w):
    def finish(r, rows, vecs):
        h = r + rows[0]
        return (h, _normalize(h)[1] * vecs[0]), ()

    return finish, [x], [w], [F32, BF16], 0


def _norm_bwd_epilogue(x, w, dres):
    def finish(r, rows, vecs):
        rstd, xhat = _normalize(rows[0])
        dxhat = r * vecs[0]
        m = jnp.mean(dxhat * xhat, axis=-1, keepdims=True)
        return (rstd * (dxhat - xhat * m) + rows[1],), (jnp.sum(r * xhat, axis=0, keepdims=True),)

    return finish, [x, dres], [w], [F32], 1


def _loss_epilogue(h1, w, tgt):
    d = h1.shape[1]

    def finish(r, rows, vecs):
        rstd, xhat = _normalize(r + rows[0])
        err = xhat * vecs[0] - rows[1]
        dyv = err * (1.0 / d)
        dxhat = dyv * vecs[0]
        m = jnp.mean(dxhat * xhat, axis=-1, keepdims=True)
        return ((rstd * (dxhat - xhat * m),),
                (jnp.sum(dyv * xhat, axis=0, keepdims=True), jnp.sum(err * err, axis=0, keepdims=True)))

    return finish, [h1, tgt], [w], [F32], 2


def _merge_fwd(proj, bg, ys, yl):
    t = proj.shape[0]
    d = D_MODEL
    tr = _rows(t, 512)

    def body(ps_ref, pl_ref, bg_ref, ys_ref, yl_ref, o_ref):
        gs = _sigmoid(ps_ref[...] + bg_ref[:, 0:d])
        gl = _sigmoid(pl_ref[...] + bg_ref[:, d:2 * d])
        o_ref[...] = (gs * ys_ref[...] + gl * yl_ref[...]).astype(BF16)

    row = pl.BlockSpec((tr, d), lambda i: (i, 0))
    return pl.pallas_call(
        body, name="merge_fwd", grid=(t // tr,),
        in_specs=[row, pl.BlockSpec((tr, d), lambda i: (i, 1)), pl.BlockSpec((1, 2 * d), lambda i: (0, 0)), row, row],
        out_specs=row, out_shape=jax.ShapeDtypeStruct((t, d), BF16),
        compiler_params=_params(("parallel",)),
    )(proj, proj, bg, ys, yl)


def _merge_bwd(proj, bg, ys, yl, dm):
    t = proj.shape[0]
    d = D_MODEL
    tr = _rows(t)

    def body(ps_ref, pl_ref, bg_ref, ys_ref, yl_ref, dm_ref, dys_ref, dyl_ref, dg_ref, dbg_ref):
        i = pl.program_id(0)
        gs = _sigmoid(ps_ref[...] + bg_ref[:, 0:d])
        gl = _sigmoid(pl_ref[...] + bg_ref[:, d:2 * d])
        dmv = dm_ref[...]
        dys_ref[...] = (dmv * gs).astype(BF16)
        dyl_ref[...] = (dmv * gl).astype(BF16)
        dgs = dmv * ys_ref[...] * gs * (1.0 - gs)
        dgl = dmv * yl_ref[...] * gl * (1.0 - gl)
        dg_ref[:, 0:d] = dgs.astype(BF16)
        dg_ref[:, d:2 * d] = dgl.astype(BF16)

        @pl.when(i == 0)
        def _():
            dbg_ref[...] = jnp.zeros_like(dbg_ref)

        dbg_ref[:, 0:d] += jnp.sum(dgs, axis=0, keepdims=True)
        dbg_ref[:, d:2 * d] += jnp.sum(dgl, axis=0, keepdims=True)

    row = pl.BlockSpec((tr, d), lambda i: (i, 0))
    wide = pl.BlockSpec((tr, 2 * d), lambda i: (i, 0))
    vec = pl.BlockSpec((1, 2 * d), lambda i: (0, 0))
    return pl.pallas_call(
        body, name="merge_bwd", grid=(t // tr,),
        in_specs=[row, pl.BlockSpec((tr, d), lambda i: (i, 1)), vec, row, row, row],
        out_specs=[row, row, wide, vec],
        out_shape=[jax.ShapeDtypeStruct((t, d), BF16), jax.ShapeDtypeStruct((t, d), BF16),
                   jax.ShapeDtypeStruct((t, PROJ_W), BF16), jax.ShapeDtypeStruct((1, 2 * d), F32)],
        compiler_params=_params(("arbitrary",)),
    )(proj, proj, bg, ys, yl, dm)


def _ffn_in_swiglu(hn, wt):
    t, d = hn.shape
    f = FFN_HIDDEN
    bm, bn = _blk(t, 1024), _blk(f, 1408)
    nj = f // bn

    def body(a_ref, wg_ref, wu_ref, g_ref, u_ref, act_ref):
        a = a_ref[...]
        g = _bdot(a, wg_ref[...], NT)
        u = _bdot(a, wu_ref[...], NT)
        g_ref[...] = g.astype(BF16)
        u_ref[...] = u.astype(BF16)
        act_ref[...] = (g * _sigmoid(g) * u).astype(BF16)

    out = pl.BlockSpec((bm, bn), lambda i, j: (i, j))
    shape = jax.ShapeDtypeStruct((t, f), BF16)
    return pl.pallas_call(
        body, name="ffn_in_swiglu", grid=(t // bm, nj),
        in_specs=[pl.BlockSpec((bm, d), lambda i, j: (i, 0)), pl.BlockSpec((bn, d), lambda i, j: (j, 0)),
                  pl.BlockSpec((bn, d), lambda i, j: (nj + j, 0))],
        out_specs=[out, out, out], out_shape=[shape, shape, shape],
        compiler_params=_params(("parallel", "parallel")),
    )(hn, wt, wt)


def _swiglu_bwd(g_all, u_all, dact):
    t, f = g_all.shape
    tr = _rows(t)

    def body(g_ref, u_ref, da_ref, o_ref):
        g = g_ref[...].astype(F32)
        sg = _sigmoid(g)
        da = da_ref[...].astype(F32)
        o_ref[:, 0:f] = (da * u_ref[...].astype(F32) * (sg * (1.0 + g * (1.0 - sg)))).astype(BF16)
        o_ref[:, f:2 * f] = (da * g * sg).astype(BF16)

    row = pl.BlockSpec((tr, f), lambda i: (i, 0))
    return pl.pallas_call(
        body, name="swiglu_bwd", grid=(t // tr,),
        in_specs=[row, row, row],
        out_specs=pl.BlockSpec((tr, 2 * f), lambda i: (i, 0)),
        out_shape=jax.ShapeDtypeStruct((t, 2 * f), BF16),
        compiler_params=_params(("parallel",)),
    )(g_all, u_all, dact)


def _conv_pre(xv, wv, bv):
    pre = bv + wv[CONV_K - 1:CONV_K, :] * xv
    for k in range(CONV_K - 1):
        pre = pre + wv[k:k + 1, :] * _shift_down(xv, CONV_K - 1 - k)
    return pre


def _ssm_conv_fwd(proj, w, b):
    t = proj.shape[0]
    nb = SSM_CONV_DIM // LANES
    c0 = COL_XBC // LANES

    def body(x_ref, w_ref, b_ref, o_ref):
        pre = _conv_pre(x_ref[...], w_ref[...], b_ref[...])
        o_ref[...] = (pre * _sigmoid(pre)).astype(BF16)

    return pl.pallas_call(
        body, name="ssm_conv_fwd", grid=(nb,),
        in_specs=[pl.BlockSpec((t, LANES), lambda j: (0, c0 + j)), pl.BlockSpec((CONV_K, LANES), lambda j: (0, j)),
                  pl.BlockSpec((1, LANES), lambda j: (0, j))],
        out_specs=pl.BlockSpec((t, LANES), lambda j: (0, j)),
        out_shape=jax.ShapeDtypeStruct((t, SSM_CONV_DIM), BF16),
        compiler_params=_params(("parallel",)),
    )(proj, w, b)


def _ssm_conv_bwd(proj, w, b, dact, dproj):
    t = proj.shape[0]
    nb = SSM_CONV_DIM // LANES
    c0 = COL_XBC // LANES

    def body(x_ref, w_ref, b_ref, da_ref, dproj_in, dx_ref, dw_ref, db_ref):
        xv = x_ref[...]
        wv = w_ref[...]
        pre = _conv_pre(xv, wv, b_ref[...])
        sg = _sigmoid(pre)
        dpre = da_ref[...] * (sg * (1.0 + pre * (1.0 - sg)))
        dx = wv[CONV_K - 1:CONV_K, :] * dpre
        for k in range(CONV_K - 1):
            dx = dx + wv[k:k + 1, :] * _shift_up(dpre, CONV_K - 1 - k)
        dx_ref[...] = dx.astype(BF16)
        for k in range(CONV_K):
            dw_ref[k:k + 1, :] = jnp.sum(dpre * _shift_down(xv, CONV_K - 1 - k), axis=0, keepdims=True)
        db_ref[...] = jnp.sum(dpre, axis=0, keepdims=True)

    col = pl.BlockSpec((t, LANES), lambda j: (0, j))
    wsp = pl.BlockSpec((CONV_K, LANES), lambda j: (0, j))
    bsp = pl.BlockSpec((1, LANES), lambda j: (0, j))
    return pl.pallas_call(
        body, name="ssm_conv_bwd", grid=(nb,),
        in_specs=[pl.BlockSpec((t, LANES), lambda j: (0, c0 + j)), wsp, bsp, col, HBM],
        out_specs=[pl.BlockSpec((t, LANES), lambda j: (0, c0 + j)), wsp, bsp],
        out_shape=[jax.ShapeDtypeStruct(dproj.shape, dproj.dtype), jax.ShapeDtypeStruct((CONV_K, SSM_CONV_DIM), F32),
                   jax.ShapeDtypeStruct((1, SSM_CONV_DIM), F32)],
        input_output_aliases={4: 0},
        compiler_params=_params(("parallel",)),
    )(proj, w, b, dact, dproj)


def _ssd_chunk_terms(dtr, bias, alog):
    a = -jnp.exp(alog)
    dt = _softplus(dtr + bias)
    row = _iota((SSM_CHUNK, SSM_CHUNK), 0)
    col = _iota((SSM_CHUNK, SSM_CHUNK), 1)
    tri = (row >= col).astype(F32)
    cs = _fdot(tri, dt * a)
    dec = jnp.exp(cs[SSM_CHUNK - 1:SSM_CHUNK, :] - cs)
    ecs = jnp.exp(cs)
    off = _iota((LANES, SSM_INNER), 1) - SSM_HEAD_DIM * _iota((LANES, SSM_INNER), 0)
    expand = jnp.where(jnp.logical_and(off >= 0, off < SSM_HEAD_DIM), 1.0, 0.0).astype(BF16)
    return a, dt, cs, dec, ecs, expand, row, col


def _ssd_specs(t):
    nc = t // SSM_CHUNK
    xs = pl.BlockSpec((SSM_CHUNK, SSM_INNER), lambda c: (c, 0))
    bm = pl.BlockSpec((SSM_CHUNK, SSM_BC), lambda c: (c, SSM_INNER // SSM_BC))
    cm = pl.BlockSpec((SSM_CHUNK, SSM_BC), lambda c: (c, SSM_INNER // SSM_BC + 1))
    dtr = pl.BlockSpec((SSM_CHUNK, LANES), lambda c: (c, COL_DT // LANES))
    vec = pl.BlockSpec((1, LANES), lambda c: (0, 0))
    wide = pl.BlockSpec((1, SSM_INNER), lambda c: (0, 0))
    return nc, xs, bm, cm, dtr, vec, wide


def _ssd_fwd(xbc_act, proj, bias, alog, dexp, gather):
    t = proj.shape[0]
    nc, xs_s, bm_s, cm_s, dtr_s, vec, wide = _ssd_specs(t)
    n = len(gather)

    def body(*refs):
        xs_ref, b_ref, c_ref, dtr_ref, bias_ref, alog_ref, dexp_ref = refs[:7]
        y_ref, sin_ref = refs[7 + n:9 + n]
        state = refs[9 + 2 * n]
        comm = (refs[7:7 + n], refs[9 + n:9 + 2 * n]) + tuple(refs[10 + 2 * n:])
        chunk = pl.program_id(0)

        @pl.when(chunk == 0)
        def _():
            _gather_phase("start", *comm)
            state[...] = jnp.zeros_like(state)

        @pl.when(chunk == (3 * nc) // 4)
        def _():
            _gather_phase("forward", *comm)

        a, dt, cs, dec, ecs, expand, row, col = _ssd_chunk_terms(dtr_ref[...], bias_ref[...], alog_ref[...])
        cst = cs.T
        dt_x = _split_dot(dt, expand, exact=False)
        dec_x = _split_dot(dec, expand, exact=False)
        ecs_x = _split_dot(ecs, expand)
        xs = xs_ref[...].astype(F32)
        xdt = xs * dt_x
        xdec = xdt * dec_x
        lane_lo = col < SSM_HEAD_DIM
        causal = row >= col
        sin_ref[0] = state[...]
        for g in range(SSM_GROUPS):
            bg = b_ref[:, g * SSM_STATE:(g + 1) * SSM_STATE].astype(BF16)
            cg = c_ref[:, g * SSM_STATE:(g + 1) * SSM_STATE].astype(BF16)
            cb = _bdot(cg, bg, NT)
            for q in range(SSM_PAIRS // SSM_GROUPS):
                pq = g * (SSM_PAIRS // SSM_GROUPS) + q
                sl = slice(pq * LANES, (pq + 1) * LANES)
                xp = xdt[:, sl].astype(BF16)
                yd = []
                for hh in range(2):
                    h = 2 * pq + hh
                    lmat = jnp.exp(jnp.where(causal, cs[:, h:h + 1] - cst[h:h + 1, :], -jnp.inf))
                    yd.append(_bdot(cb * lmat, xp))
                s_in = state[pq]
                y_off = _bdot(cg, s_in) * ecs_x[:, sl]
                y_ref[:, sl] = (jnp.where(lane_lo, yd[0], yd[1]) + y_off + xs[:, sl] * dexp_ref[:, sl]).astype(BF16)
                state[pq] = s_in * ecs_x[SSM_CHUNK - 1:SSM_CHUNK, sl] + _bdot(bg, xdec[:, sl], TN)

        @pl.when(chunk == nc - 1)
        def _():
            _gather_phase("finish", *comm)

    out = pl.pallas_call(
        body, name="ssd_fwd", grid=(nc,),
        in_specs=[xs_s, bm_s, cm_s, dtr_s, vec, vec, wide] + [HBM] * n,
        out_specs=[pl.BlockSpec((SSM_CHUNK, SSM_INNER), lambda c: (c, 0)),
                   pl.BlockSpec((1, SSM_PAIRS, SSM_STATE, LANES), lambda c: (c, 0, 0, 0))] + [HBM] * n,
        out_shape=[jax.ShapeDtypeStruct((t, SSM_INNER), BF16),
                   jax.ShapeDtypeStruct((nc, SSM_PAIRS, SSM_STATE, LANES), F32)]
        + [jax.ShapeDtypeStruct((N_DEV,) + v.shape, v.dtype) for v in gather],
        scratch_shapes=[pltpu.VMEM((SSM_PAIRS, SSM_STATE, LANES), F32)] + _comm_scratch(n),
        compiler_params=_params(("arbitrary",)),
    )(xbc_act, xbc_act, xbc_act, proj, bias, alog, dexp, *gather)
    return out[:2], out[2:]


def _ssd_bwd(xbc_act, proj, s_in_all, dy, bias, alog, dexp, dproj, exchange):
    n_ex = len(exchange)
    t = proj.shape[0]
    nc = t // SSM_CHUNK
    last = nc - 1
    xs_s = pl.BlockSpec((SSM_CHUNK, SSM_INNER), lambda c: (last - c, 0))
    bm_s = pl.BlockSpec((SSM_CHUNK, SSM_BC), lambda c: (last - c, SSM_INNER // SSM_BC))
    cm_s = pl.BlockSpec((SSM_CHUNK, SSM_BC), lambda c: (last - c, SSM_INNER // SSM_BC + 1))
    dtr_s = pl.BlockSpec((SSM_CHUNK, LANES), lambda c: (last - c, COL_DT // LANES))
    sin_s = pl.BlockSpec((1, SSM_PAIRS, SSM_STATE, LANES), lambda c: (last - c, 0, 0, 0))
    vec = pl.BlockSpec((1, LANES), lambda c: (0, 0))
    wide = pl.BlockSpec((1, SSM_INNER), lambda c: (0, 0))

    def body(*refs):
        xs_ref, b_ref, c_ref, dtr_ref, sin_ref, dy_ref, bias_ref, alog_ref, dexp_ref = refs[:9]
        dxbc_ref, ddtr_ref, dbias_ref, dalog_ref, ddcol_ref = refs[10 + n_ex:15 + n_ex]
        dstate, dxdt_s, yoff_s, rx_s, trow_s = refs[15 + 2 * n_ex:20 + 2 * n_ex]
        comm = (refs[10:10 + n_ex], refs[15 + n_ex:15 + 2 * n_ex]) + tuple(refs[20 + 2 * n_ex:])

        @pl.when(pl.program_id(0) == 0)
        def _():
            _exchange_phase("start", *comm)
            dstate[...] = jnp.zeros_like(dstate)
            trow_s[...] = jnp.zeros_like(trow_s)
            dbias_ref[...] = jnp.zeros_like(dbias_ref)
            dalog_ref[...] = jnp.zeros_like(dalog_ref)
            ddcol_ref[...] = jnp.zeros_like(ddcol_ref)

        dtr = dtr_ref[...]
        a, dt, cs, dec, ecs, expand, row, col = _ssd_chunk_terms(dtr, bias_ref[...], alog_ref[...])
        cst = cs.T
        dt_x = _split_dot(dt, expand, exact=False)
        dec_x = _split_dot(dec, expand, exact=False)
        ecs_x = _split_dot(ecs, expand)
        xs = xs_ref[...].astype(F32)
        dyv = dy_ref[...].astype(F32)
        xdt = xs * dt_x
        lane_lo = col < SSM_HEAD_DIM
        causal = row >= col
        ddcol_ref[...] += jnp.sum(dyv * xs, axis=0, keepdims=True)
        dcs_col = jnp.zeros((SSM_CHUNK, LANES), F32)
        dcs_row = jnp.zeros((LANES, SSM_CHUNK), F32)
        for g in range(SSM_GROUPS):
            bg = b_ref[:, g * SSM_STATE:(g + 1) * SSM_STATE].astype(BF16)
            cg = c_ref[:, g * SSM_STATE:(g + 1) * SSM_STATE].astype(BF16)
            cb = _bdot(cg, bg, NT)
            dgm = jnp.zeros((SSM_CHUNK, SSM_CHUNK), F32)
            dbg = jnp.zeros((SSM_CHUNK, SSM_STATE), F32)
            dcg = jnp.zeros((SSM_CHUNK, SSM_STATE), F32)
            for q in range(SSM_PAIRS // SSM_GROUPS):
                pq = g * (SSM_PAIRS // SSM_GROUPS) + q
                sl = slice(pq * LANES, (pq + 1) * LANES)
                dyp = dyv[:, sl]
                xp = xdt[:, sl]
                dxh = []
                for hh in range(2):
                    h = 2 * pq + hh
                    lmat = jnp.exp(jnp.where(causal, cs[:, h:h + 1] - cst[h:h + 1, :], -jnp.inf))
                    mmat = cb * lmat
                    dyh = jnp.where(lane_lo if hh == 0 else jnp.logical_not(lane_lo), dyp, 0.0)
                    dmm = _bdot(dyh, xp, NT)
                    pm = dmm * mmat
                    dcs_col = jnp.where(col == h, jnp.sum(pm, axis=1, keepdims=True), dcs_col)
                    dcs_row = jnp.where(row == h, jnp.sum(pm, axis=0, keepdims=True), dcs_row)
                    dgm = dgm + dmm * lmat
                    dxh.append(_bdot(mmat, dyp, TN))
                s_in = sin_ref[0, pq]
                ecs_p = ecs_x[:, sl]
                dec_p = dec_x[:, sl]
                etot_p = ecs_x[SSM_CHUNK - 1:SSM_CHUNK, sl]
                yoff_s[:, sl] = dyp * (_bdot(cg, s_in) * ecs_p)
                dq = dyp * ecs_p
                dcg = dcg + _bdot(dq, s_in, NT)
                ds = dstate[pq]
                r = _bdot(bg, ds)
                rx_s[:, sl] = r * xp
                dxdt_s[:, sl] = jnp.where(lane_lo, dxh[0], dxh[1]) + dec_p * r
                dbg = dbg + _bdot(xp * dec_p, ds, NT)
                trow_s[0:1, sl] = jnp.sum(ds * s_in, axis=0, keepdims=True) * etot_p
                dstate[pq] = etot_p * ds + _bdot(cg, dq, TN)
            dcg = dcg + _bdot(dgm, bg)
            dbg = dbg + _bdot(dgm, cg, TN)
            dxbc_ref[:, SSM_INNER + g * SSM_STATE:SSM_INNER + (g + 1) * SSM_STATE] = dbg.astype(BF16)
            dxbc_ref[:, SSM_INNER + SSM_BC + g * SSM_STATE:SSM_INNER + SSM_BC + (g + 1) * SSM_STATE] = dcg.astype(BF16)
        ddec = _split_dot(rx_s[...], expand, NT, exact=False) * dec
        dtot = _split_dot(trow_s[...], expand, NT, exact=False)[0:1, :]
        dcs = dcs_col - dcs_row.T + _split_dot(yoff_s[...], expand, NT, exact=False) - ddec
        dcs = dcs + jnp.where(row == SSM_CHUNK - 1, jnp.sum(ddec, axis=0, keepdims=True) + dtot, 0.0)
        da = _fdot((row <= col).astype(F32), dcs)
        dxdt = dxdt_s[...]
        ddt = da * a + _split_dot(dxdt * xs, expand, NT, exact=False)
        dalog_ref[...] += jnp.sum(da * dt, axis=0, keepdims=True) * a
        ddtr = ddt * _sigmoid(dtr + bias_ref[...])
        ddtr_ref[...] = ddtr.astype(BF16)
        dbias_ref[...] += jnp.sum(ddtr, axis=0, keepdims=True)
        dxbc_ref[:, 0:SSM_INNER] = (dxdt * dt_x + dyv * dexp_ref[...]).astype(BF16)

        @pl.when(pl.program_id(0) == last)
        def _():
            _exchange_phase("finish", *comm)

    out = pl.pallas_call(
        body, name="ssd_bwd", grid=(nc,),
        in_specs=[xs_s, bm_s, cm_s, dtr_s, sin_s, pl.BlockSpec((SSM_CHUNK, SSM_INNER), lambda c: (last - c, 0)),
                  vec, vec, wide, HBM] + [HBM] * n_ex,
        out_specs=[pl.BlockSpec((SSM_CHUNK, SSM_CONV_DIM), lambda c: (last - c, 0)), dtr_s, vec, vec, wide]
        + [HBM] * n_ex,
        out_shape=[jax.ShapeDtypeStruct((t, SSM_CONV_DIM), BF16), jax.ShapeDtypeStruct(dproj.shape, dproj.dtype),
                   jax.ShapeDtypeStruct((1, LANES), F32), jax.ShapeDtypeStruct((1, LANES), F32),
                   jax.ShapeDtypeStruct((1, SSM_INNER), F32)]
        + [jax.ShapeDtypeStruct(p.shape, p.dtype) for p in exchange],
        input_output_aliases={9: 1},
        scratch_shapes=[pltpu.VMEM((SSM_PAIRS, SSM_STATE, LANES), F32),
                        pltpu.VMEM((SSM_CHUNK, SSM_INNER), F32), pltpu.VMEM((SSM_CHUNK, SSM_INNER), F32),
                        pltpu.VMEM((SSM_CHUNK, SSM_INNER), F32), pltpu.VMEM((SUBLANES, SSM_INNER), F32)]
        + _comm_scratch(n_ex),
        compiler_params=_params(("arbitrary",)),
    )(xbc_act, xbc_act, xbc_act, proj, s_in_all, dy, bias, alog, dexp, dproj, *exchange)
    return out[:5], out[5:]


def _group_rstd(y):
    n = SSM_INNER // SSM_GROUPS
    parts = []
    for g in range(SSM_GROUPS):
        yg = y[:, g * n:(g + 1) * n]
        r = lax.rsqrt(jnp.mean(yg * yg, axis=-1, keepdims=True) + RMS_EPS)
        parts.append(jnp.broadcast_to(r, yg.shape))
    return jnp.concatenate(parts, axis=1)


def _group_mean(v):
    n = SSM_INNER // SSM_GROUPS
    parts = []
    for g in range(SSM_GROUPS):
        vg = v[:, g * n:(g + 1) * n]
        parts.append(jnp.broadcast_to(jnp.mean(vg, axis=-1, keepdims=True), vg.shape))
    return jnp.concatenate(parts, axis=1)


def _ssm_post_fwd(y_ssd, proj, nw):
    t = proj.shape[0]
    n = SSM_INNER
    tr = _rows(t)

    def body(y_ref, z_ref, nw_ref, o_ref):
        z = z_ref[...]
        y = y_ref[...] * (z * _sigmoid(z))
        o_ref[...] = (y * _group_rstd(y) * nw_ref[...]).astype(BF16)

    row = pl.BlockSpec((tr, n), lambda i: (i, 0))
    return pl.pallas_call(
        body, name="ssm_post_fwd", grid=(t // tr,),
        in_specs=[row, pl.BlockSpec((tr, n), lambda i: (i, COL_Z // n)), pl.BlockSpec((1, n), lambda i: (0, 0))],
        out_specs=row, out_shape=jax.ShapeDtypeStruct((t, n), BF16),
        compiler_params=_params(("parallel",)),
    )(y_ssd, proj, nw)


def _ssm_post_bwd(y_ssd, proj, nw, dout, dproj):
    t = proj.shape[0]
    n = SSM_INNER
    tr = _rows(t)

    def body(y_ref, z_ref, nw_ref, do_ref, dproj_in, dy_ref, dz_ref, dnw_ref):
        i = pl.program_id(0)
        z = z_ref[...]
        sg = _sigmoid(z)
        sz = z * sg
        ys = y_ref[...]
        y = ys * sz
        rstd = _group_rstd(y)
        yn = y * rstd
        dov = do_ref[...]
        dyn = dov * nw_ref[...]
        dyg = rstd * (dyn - yn * _group_mean(dyn * yn))
        dy_ref[...] = (dyg * sz).astype(BF16)
        dz_ref[...] = (dyg * ys * (sg * (1.0 + z * (1.0 - sg)))).astype(BF16)
        part = jnp.sum(dov * yn, axis=0, keepdims=True)

        @pl.when(i == 0)
        def _():
            dnw_ref[...] = part

        @pl.when(i > 0)
        def _():
            dnw_ref[...] += part

    row = pl.BlockSpec((tr, n), lambda i: (i, 0))
    vec = pl.BlockSpec((1, n), lambda i: (0, 0))
    return pl.pallas_call(
        body, name="ssm_post_bwd", grid=(t // tr,),
        in_specs=[row, pl.BlockSpec((tr, n), lambda i: (i, COL_Z // n)), vec, row, HBM],
        out_specs=[row, pl.BlockSpec((tr, n), lambda i: (i, COL_Z // n)), vec],
        out_shape=[jax.ShapeDtypeStruct((t, n), BF16), jax.ShapeDtypeStruct(dproj.shape, dproj.dtype),
                   jax.ShapeDtypeStruct((1, n), F32)],
        input_output_aliases={4: 1},
        compiler_params=_params(("arbitrary",)),
    )(y_ssd, proj, nw, dout, dproj)


SCAN_UNROLL = 8
GELU_C = math.sqrt(2.0 / math.pi)
GELU_K = 0.044715


def _gelu_parts(y):
    th = jnp.tanh(GELU_C * (y + GELU_K * y * y * y))
    val = 0.5 * y * (1.0 + th)
    grad = 0.5 * (1.0 + th) + 0.5 * y * (1.0 - th * th) * GELU_C * (1.0 + 3.0 * GELU_K * y * y)
    return val, grad


def _scan_tiles(a_ref, b_ref, h_ref, n_rows, reverse):
    n_tiles = n_rows // SUBLANES
    shape = (SUBLANES, a_ref.shape[1])
    row = _iota(shape, 0)

    def in_tile(av, bv):
        for s in (1, 2, 4):
            if reverse:
                keep = row < SUBLANES - s
                a_sh = jnp.where(keep, pltpu.roll(av, SUBLANES - s, 0), 1.0)
                b_sh = jnp.where(keep, pltpu.roll(bv, SUBLANES - s, 0), 0.0)
            else:
                keep = row >= s
                a_sh = jnp.where(keep, pltpu.roll(av, s, 0), 1.0)
                b_sh = jnp.where(keep, pltpu.roll(bv, s, 0), 0.0)
            bv = av * b_sh + bv
            av = av * a_sh
        return av, bv

    def step(k, carry):
        first = (n_tiles // SCAN_UNROLL - 1 - k) if reverse else k
        tiles = [first * SCAN_UNROLL + j for j in range(SCAN_UNROLL)]
        if reverse:
            tiles = tiles[::-1]
        ats = [pl.ds(pl.multiple_of(tile * SUBLANES, SUBLANES), SUBLANES) for tile in tiles]
        scanned = [in_tile(a_ref[at, :], b_ref[at, :]) for at in ats]
        for at, (av, bv) in zip(ats, scanned):
            hv = bv + av * carry
            h_ref[at, :] = hv
            carry = hv[0:1, :] if reverse else hv[SUBLANES - 1:SUBLANES, :]
        return carry

    assert n_tiles % SCAN_UNROLL == 0, n_rows
    lax.fori_loop(0, n_tiles // SCAN_UNROLL, step, jnp.zeros((1, a_ref.shape[1]), F32))


def _lru_gates(xl, cw, cb, wr, br, wi, bi, lam):
    u = cb + cw[CONV_K - 1:CONV_K, :] * xl
    for k in range(CONV_K - 1):
        u = u + cw[k:k + 1, :] * _shift_down(xl, CONV_K - 1 - k)
    r = _sigmoid(_bdot(u, wr) + br)
    i = _sigmoid(_bdot(u, wi) + bi)
    sp = _softplus(-lam)
    la = -LRU_C * r * sp
    a = jnp.exp(la)
    mult = jnp.sqrt(-jnp.tanh(la) * (a * a + 1.0))
    return u, r, i, sp, a, mult


def _lru_specs(t):
    c0 = COL_LRU // LANES
    xl = pl.BlockSpec((t, LANES), lambda j: (0, c0 + 2 * j))
    yl = pl.BlockSpec((t, LANES), lambda j: (0, c0 + 2 * j + 1))
    col = pl.BlockSpec((t, LANES), lambda j: (0, j))
    cw = pl.BlockSpec((CONV_K, LANES), lambda j: (0, j))
    vec = pl.BlockSpec((1, LANES), lambda j: (0, j))
    wblk = pl.BlockSpec((1, LANES, LANES), lambda j: (j, 0, 0))
    return xl, yl, col, cw, vec, wblk


def _lru_fwd(proj, cw, cb, wr, br, wi, bi, lam, gather):
    t = proj.shape[0]
    xl_s, yl_s, col, cw_s, vec, wblk = _lru_specs(t)
    n = len(gather)

    def body(*refs):
        xl_ref, yl_ref, cw_ref, cb_ref, wr_ref, br_ref, wi_ref, bi_ref, lam_ref = refs[:9]
        o_ref, h_ref = refs[9 + n:11 + n]
        a_s, b_s = refs[11 + 2 * n:13 + 2 * n]
        comm = (refs[9:9 + n], refs[11 + n:11 + 2 * n]) + tuple(refs[13 + 2 * n:])
        j = pl.program_id(0)
        for step, phase in ((0, "start"), (LRU_BLOCKS - 2, "forward")):
            @pl.when(j == step)
            def _():
                _gather_phase(phase, *comm)

        u, r, i, sp, a, mult = _lru_gates(xl_ref[...], cw_ref[...], cb_ref[...], wr_ref[0], br_ref[...],
                                          wi_ref[0], bi_ref[...], lam_ref[...])
        a_s[...] = a
        b_s[...] = mult * (i * u)
        _scan_tiles(a_s, b_s, h_ref, t, reverse=False)
        o_ref[...] = (h_ref[...] * _gelu_parts(yl_ref[...])[0]).astype(BF16)

        @pl.when(j == LRU_BLOCKS - 1)
        def _():
            _gather_phase("finish", *comm)

    out = pl.pallas_call(
        body, name="lru_fwd", grid=(LRU_BLOCKS,),
        in_specs=[xl_s, yl_s, cw_s, vec, wblk, vec, wblk, vec, vec] + [HBM] * n,
        out_specs=[col, col] + [HBM] * n,
        out_shape=[jax.ShapeDtypeStruct((t, LRU_WIDTH), BF16), jax.ShapeDtypeStruct((t, LRU_WIDTH), F32)]
        + [jax.ShapeDtypeStruct((N_DEV,) + v.shape, v.dtype) for v in gather],
        scratch_shapes=[pltpu.VMEM((t, LANES), F32)] * 2 + _comm_scratch(n),
        compiler_params=_params(("arbitrary",), big=True),
    )(proj, proj, cw, cb, wr, br, wi, bi, lam, *gather)
    return out[:2], out[2:]


def _lru_bwd(proj, cw, cb, wr, br, wi, bi, lam, h_all, dout, dproj, exchange):
    t = proj.shape[0]
    xl_s, yl_s, col, cw_s, vec, wblk = _lru_specs(t)
    pair = pl.BlockSpec((t, 2 * LANES), lambda j: (0, COL_LRU // (2 * LANES) + j))
    n_ex = len(exchange)

    def body(*refs):
        xl_ref, yl_ref, cw_ref, cb_ref, wr_ref, br_ref, wi_ref, bi_ref, lam_ref, h_ref, do_ref = refs[:11]
        dxy_ref, dcw_ref, dcb_ref, dwr_ref, dbr_ref, dwi_ref, dbi_ref, dlam_ref = refs[12 + n_ex:20 + n_ex]
        a_s, b_s, g_s = refs[20 + 2 * n_ex:23 + 2 * n_ex]
        comm = (refs[12:12 + n_ex], refs[20 + n_ex:20 + 2 * n_ex]) + tuple(refs[23 + 2 * n_ex:])

        @pl.when(pl.program_id(0) == 0)
        def _():
            _exchange_phase("start", *comm)

        xl = xl_ref[...]
        cwv = cw_ref[...]
        lam = lam_ref[...]
        u, r, i, sp, a, mult = _lru_gates(xl, cwv, cb_ref[...], wr_ref[0], br_ref[...], wi_ref[0], bi_ref[...], lam)
        v = i * u
        gl, dgl = _gelu_parts(yl_ref[...])
        dov = do_ref[...]
        h = h_ref[...]
        dxy_ref[:, LANES:2 * LANES] = (dov * h * dgl).astype(BF16)
        b_s[...] = dov * gl
        a_s[...] = _shift_up(a, 1)
        _scan_tiles(a_s, b_s, g_s, t, reverse=True)
        g = g_s[...]
        da = g * _shift_down(h, 1)
        dmult = g * v
        dv = g * mult
        dla = da * a - dmult * (a * a) / mult
        dr = dla * (-LRU_C * sp)
        dsp = jnp.sum(dla * (-LRU_C * r), axis=0, keepdims=True)
        dlam_ref[...] = -dsp * _sigmoid(-lam)
        dpr = dr * r * (1.0 - r)
        dpi = dv * u * i * (1.0 - i)
        dbr_ref[...] = jnp.sum(dpr, axis=0, keepdims=True)
        dbi_ref[...] = jnp.sum(dpi, axis=0, keepdims=True)
        dwr_ref[0] = _bdot(u, dpr, TN)
        dwi_ref[0] = _bdot(u, dpi, TN)
        du = dv * i + _bdot(dpr, wr_ref[0], NT) + _bdot(dpi, wi_ref[0], NT)
        dxl = cwv[CONV_K - 1:CONV_K, :] * du
        for k in range(CONV_K - 1):
            dxl = dxl + cwv[k:k + 1, :] * _shift_up(du, CONV_K - 1 - k)
        dxy_ref[:, 0:LANES] = dxl.astype(BF16)
        for k in range(CONV_K):
            dcw_ref[k:k + 1, :] = jnp.sum(du * _shift_down(xl, CONV_K - 1 - k), axis=0, keepdims=True)
        dcb_ref[...] = jnp.sum(du, axis=0, keepdims=True)

        @pl.when(pl.program_id(0) == LRU_BLOCKS - 1)
        def _():
            _exchange_phase("finish", *comm)

    out = pl.pallas_call(
        body, name="lru_bwd", grid=(LRU_BLOCKS,),
        in_specs=[xl_s, yl_s, cw_s, vec, wblk, vec, wblk, vec, vec, col, col, HBM] + [HBM] * n_ex,
        out_specs=[pair, cw_s, vec, wblk, vec, wblk, vec, vec] + [HBM] * n_ex,
        input_output_aliases={11: 0},
        out_shape=[jax.ShapeDtypeStruct(dproj.shape, dproj.dtype),
                   jax.ShapeDtypeStruct((CONV_K, LRU_WIDTH), F32), jax.ShapeDtypeStruct((1, LRU_WIDTH), F32),
                   jax.ShapeDtypeStruct((LRU_BLOCKS, LANES, LANES), F32), jax.ShapeDtypeStruct((1, LRU_WIDTH), F32),
                   jax.ShapeDtypeStruct((LRU_BLOCKS, LANES, LANES), F32), jax.ShapeDtypeStruct((1, LRU_WIDTH), F32),
                   jax.ShapeDtypeStruct((1, LRU_WIDTH), F32)]
        + [jax.ShapeDtypeStruct(p.shape, p.dtype) for p in exchange],
        scratch_shapes=[pltpu.VMEM((t, LANES), F32)] * 3 + _comm_scratch(n_ex),
        compiler_params=_params(("arbitrary",), big=True),
    )(proj, proj, cw, cb, wr, br, wi, bi, lam, h_all, dout, dproj, *exchange)
    return out[:8], out[8:]


def _mesh_pos():
    return lax.axis_index("x"), lax.axis_index("y"), lax.axis_index("c")


HBM = pl.BlockSpec(memory_space=pl.ANY)


def _comm_scratch(n):
    return [pltpu.SemaphoreType.DMA((n, 7)), pltpu.SemaphoreType.DMA((n, 7)), pltpu.SemaphoreType.DMA((n,))]


def _gather_phase(phase, v_refs, out_refs, send_sems, recv_sems, local_sems):
    n = len(v_refs)
    x, y, c = _mesh_pos()
    me, sibling = (x, y, c), (x, y, 1 - c)
    chips = [(1 - x, y), (x, 1 - y), (1 - x, 1 - y)]

    def block(a, px, py, pc):
        return out_refs[a].at[4 * px + 2 * py + pc]

    def copy(a, k, blk, to, src=None):
        return pltpu.make_async_remote_copy(
            src_ref=block(a, *blk) if src is None else src, dst_ref=block(a, *blk),
            send_sem=send_sems.at[a, k], recv_sem=recv_sems.at[a, k], device_id=to, device_id_type=MESH)

    def own(a):
        return pltpu.make_async_copy(v_refs[a], block(a, *me), local_sems.at[a])

    def first(a):
        return ([copy(a, 0, me, sibling, src=v_refs[a])]
                + [copy(a, 1 + j, me, (*chip, c), src=v_refs[a]) for j, chip in enumerate(chips)])

    def forward(a, j):
        return copy(a, 4 + j, (*chips[j], c), sibling)

    if phase == "start":
        for a in range(n):
            own(a).start()
        for a in range(n):
            for cp in first(a):
                cp.start()
    elif phase == "forward":
        for j in range(3):
            for a in range(n):
                copy(a, 1 + j, (*chips[j], c), me).wait_recv()
                forward(a, j).start()
    else:
        for a in range(n):
            copy(a, 0, sibling, me).wait_recv()
            for j in range(3):
                copy(a, 4 + j, (*chips[j], 1 - c), me).wait_recv()
        for a in range(n):
            for cp in first(a) + [forward(a, j) for j in range(3)]:
                cp.wait_send()
            own(a).wait()


def _gather_hosted(phase, *comm):
    _gather_phase("forward" if phase == "late" else phase, *comm)


def _all_gather(vs, name):
    n = len(vs)

    def body(*refs):
        comm = (refs[:n], refs[n:2 * n]) + tuple(refs[2 * n:])
        for phase in ("start", "forward", "finish"):
            _gather_phase(phase, *comm)

    return pl.pallas_call(
        body, name=name,
        out_shape=[jax.ShapeDtypeStruct((N_DEV,) + v.shape, v.dtype) for v in vs],
        in_specs=[HBM] * n, out_specs=[HBM] * n, scratch_shapes=_comm_scratch(n),
    )(*vs)


def _run_copies(phase, local, remote):
    if phase == "start":
        for cp in local + remote:
            cp.start()
    else:
        for cp in remote:
            cp.wait()
        for cp in local:
            cp.wait()


def _exchange_phase(phase, p_refs, out_refs, send_sems, recv_sems, local_sems):
    if phase == "late":
        return
    n = len(p_refs)
    x, y, c = _mesh_pos()
    me = 4 * x + 2 * y + c
    local = [pltpu.make_async_copy(p_refs[a].at[me], out_refs[a].at[me], local_sems.at[a]) for a in range(n)]
    remote = []
    for k in range(1, N_DEV):
        px = (1 - x) if k & 4 else x
        py = (1 - y) if k & 2 else y
        pc = (1 - c) if k & 1 else c
        for a in range(n):
            remote.append(pltpu.make_async_remote_copy(
                src_ref=p_refs[a].at[4 * px + 2 * py + pc], dst_ref=out_refs[a].at[me],
                send_sem=send_sems.at[a, k - 1], recv_sem=recv_sems.at[a, k - 1],
                device_id=(px, py, pc), device_id_type=MESH))
    _run_copies(phase, local, remote)


def _chip_exchange_phase(phase, p_refs, out_refs, send_sems, recv_sems, local_sems):
    if phase == "late":
        return
    n = len(p_refs)
    x, y, c = _mesh_pos()
    me = 2 * x + y
    local = [pltpu.make_async_copy(p_refs[a].at[me], out_refs[a].at[me], local_sems.at[a]) for a in range(n)]
    remote = []
    for k in range(1, 4):
        px = (1 - x) if k & 2 else x
        py = (1 - y) if k & 1 else y
        for a in range(n):
            remote.append(pltpu.make_async_remote_copy(
                src_ref=p_refs[a].at[2 * px + py], dst_ref=out_refs[a].at[me],
                send_sem=send_sems.at[a, k - 1], recv_sem=recv_sems.at[a, k - 1],
                device_id=(px, py, c), device_id_type=MESH))
    _run_copies(phase, local, remote)


def _sibling_exchange(parts, name):
    chips = N_DEV // 2

    def body(p_ref, out_ref, send_sems, recv_sems):
        x, y, c = _mesh_pos()
        copies = [pltpu.make_async_remote_copy(
            src_ref=p_ref.at[2 * q + 1 - c], dst_ref=out_ref.at[q], send_sem=send_sems.at[q], recv_sem=recv_sems.at[q],
            device_id=(x, y, 1 - c), device_id_type=MESH) for q in range(chips)]
        _run_copies("start", [], copies)
        _run_copies("finish", [], copies)

    return pl.pallas_call(
        body, name=name, out_shape=jax.ShapeDtypeStruct((chips,) + parts.shape[1:], parts.dtype),
        in_specs=[HBM], out_specs=HBM,
        scratch_shapes=[pltpu.SemaphoreType.DMA((chips,)), pltpu.SemaphoreType.DMA((chips,))],
    )(parts)


def _pair_sum(parts, theirs, name):
    slots, rows, cols = theirs.shape
    tc = 256
    core = lax.axis_index("c").astype(jnp.int32).reshape(1)

    def body(c_ref, a_ref, b_ref, o_ref):
        o_ref[...] = (a_ref[...].astype(F32) + b_ref[...].astype(F32)).astype(o_ref.dtype)

    spec = pl.BlockSpec((1, rows, tc), lambda q, j, c: (q, 0, j))
    return pl.pallas_call(
        body, name=name,
        grid_spec=pltpu.PrefetchScalarGridSpec(
            num_scalar_prefetch=1, grid=(slots, cols // tc),
            in_specs=[pl.BlockSpec((1, rows, tc), lambda q, j, c: (2 * q + c[0], 0, j)), spec], out_specs=spec),
        out_shape=jax.ShapeDtypeStruct(theirs.shape, theirs.dtype),
        compiler_params=_params(("parallel", "parallel")),
    )(core, parts, theirs)


def _sum_sources(recvs, name):
    k = len(recvs)

    def body(*refs):
        for r_ref, o_ref in zip(refs[:k], refs[k:]):
            acc = r_ref[0].astype(F32)
            for s in range(1, r_ref.shape[0]):
                acc = acc + r_ref[s].astype(F32)
            o_ref[...] = acc

    return pl.pallas_call(
        body, name=name, out_shape=[jax.ShapeDtypeStruct(r.shape[1:], F32) for r in recvs],
        compiler_params=_params(),
    )(*recvs)


def _row_tile(rows):
    for tile in range(512, 15, -16):
        if rows % tile == 0:
            return tile
    return rows


def _adam_update(w, g, m, v):
    nm = ADAM_B1 * m + (1.0 - ADAM_B1) * g
    nv = ADAM_B2 * v + (1.0 - ADAM_B2) * (g * g)
    m_hat = nm / (1.0 - ADAM_B1 ** ADAM_STEP)
    v_hat = nv / (1.0 - ADAM_B2 ** ADAM_STEP)
    return -ADAM_LR * (m_hat / (jnp.sqrt(v_hat) + ADAM_EPS) + ADAM_WD * w), nm, nv


def _vector_offsets(widths):
    offsets, end = [], 0
    for c in widths:
        offsets.append(end)
        end += c + (-c) % LANES
    return offsets, end


def _adamw_small(vec_parts, vec_state, mat_grads, mat_state):
    widths = [w.shape[1] for w, _, _ in vec_state]
    offsets, total = _vector_offsets(widths)
    assert vec_parts.shape == (N_DEV, total), (vec_parts.shape, total)
    n_vec, n_mat = len(vec_state), len(mat_state)

    def body(*refs):
        r_ref = refs[0]
        vec_in = refs[1:1 + 3 * n_vec]
        mat_in = refs[1 + 3 * n_vec:1 + 3 * n_vec + 4 * n_mat]
        outs = refs[1 + 3 * n_vec + 4 * n_mat:]
        for i, (off, c) in enumerate(zip(offsets, widths)):
            g = r_ref[0:1, off:off + c]
            for s in range(1, N_DEV):
                g = g + r_ref[s:s + 1, off:off + c]
            w_ref, m_ref, v_ref = vec_in[3 * i:3 * i + 3]
            g_out, d_out, m_out, v_out = outs[4 * i:4 * i + 4]
            g_out[...] = g
            d_out[...], m_out[...], v_out[...] = _adam_update(w_ref[...], g, m_ref[...], v_ref[...])
        for j in range(n_mat):
            g_ref, w_ref, m_ref, v_ref = mat_in[4 * j:4 * j + 4]
            d_out, m_out, v_out = outs[4 * n_vec + 3 * j:4 * n_vec + 3 * j + 3]
            d_out[...], m_out[...], v_out[...] = _adam_update(w_ref[...], g_ref[...], m_ref[...], v_ref[...])

    args = [vec_parts] + [a for state in vec_state for a in state]
    for g, state in zip(mat_grads, mat_state):
        args += [g, *state]
    out_shape = [jax.ShapeDtypeStruct(w.shape, F32) for w, _, _ in vec_state for _ in range(4)]
    out_shape += [jax.ShapeDtypeStruct(w.shape, F32) for w, _, _ in mat_state for _ in range(3)]
    out = pl.pallas_call(body, name="adamw_replicated", out_shape=out_shape, compiler_params=_params())(*args)
    vec_out = [tuple(out[4 * i:4 * i + 4]) for i in range(n_vec)]
    mat_out = [tuple(out[4 * n_vec + 3 * j:4 * n_vec + 3 * j + 3]) for j in range(n_mat)]
    return vec_out, mat_out


def _adamw(w, recv, m, v, name):
    rows, width = w.shape
    n = recv.shape[0]
    if rows % 16 == 0 or width % 256:
        tr, tc = _row_tile(rows), width
    else:
        tr, tc = rows, 256

    def body(w_ref, r_ref, m_ref, v_ref, g_ref, d_ref, nm_ref, nv_ref):
        gv = r_ref[0].astype(F32)
        for s in range(1, n):
            gv = gv + r_ref[s].astype(F32)
        g_ref[...] = gv
        d_ref[...], nm_ref[...], nv_ref[...] = _adam_update(w_ref[...], gv, m_ref[...], v_ref[...])

    spec = pl.BlockSpec((tr, tc), lambda i, j: (i, j))
    shape = jax.ShapeDtypeStruct((rows, width), F32)
    return pl.pallas_call(
        body, name=name, grid=(rows // tr, width // tc),
        in_specs=[spec, pl.BlockSpec((n, tr, tc), lambda i, j: (0, i, j)), spec, spec],
        out_specs=[spec] * 4, out_shape=[shape] * 4,
        compiler_params=_params(("parallel", "parallel")),
    )(w, recv, m, v)


BIG_NAMES = ("w_in", "w_out_ssm", "w_out_lru", "w_out", "w_ffn_in", "w_ffn_out", "ssm_conv_w", "lru_conv_w")
TRANSPOSED = ("w_in", "w_ffn_in")
CONV_NAMES = ("ssm_conv_w", "lru_conv_w")
MATMUL_NAMES = BIG_NAMES[:6]
NEEDED_FIRST = ("w_in", "ssm_conv_w", "lru_conv_w")
GATHERED_IN_PROJ = ("w_ffn_in", "w_ffn_out")
GATHERED_IN_SSD = ("w_out_ssm",)
GATHERED_IN_LRU = ("w_out_lru", "w_out")
EXCHANGED_IN_SSD = ("w_ffn_in", "w_ffn_out")
EXCHANGED_IN_LRU = ("w_out_ssm", "w_out_lru", "w_out")
SMALL_VECTORS = ("norm1_w", "b_branch_gate", "ssm_conv_b", "ssm_dt_bias", "ssm_a_log", "ssm_d", "ssm_norm_w",
                 "lru_conv_b", "lru_b_r", "lru_b_i", "lru_lambda", "norm2_w", "norm_f_w")
SMALL_MATRICES = ("lru_w_r", "lru_w_i")


def _col_shards(full):
    rows, cols = full.shape
    return full.reshape(rows, N_DEV, cols // N_DEV).transpose(1, 0, 2)


def _from_col_shards(g):
    n, rows, w = g.shape
    return g.transpose(1, 0, 2).reshape(rows, n * w)


def kernel(x, norm1_w, w_in, b_branch_gate, ssm_conv_w, ssm_conv_b, ssm_dt_bias, ssm_a_log, ssm_d, ssm_norm_w, w_out_ssm, lru_conv_w, lru_conv_b, lru_w_r, lru_b_r, lru_w_i, lru_b_i, lru_lambda, w_out_lru, w_out, norm2_w, w_ffn_in, w_ffn_out, norm_f_w, loss_target, m_norm1_w, m_w_in, m_b_branch_gate, m_ssm_conv_w, m_ssm_conv_b, m_ssm_dt_bias, m_ssm_a_log, m_ssm_d, m_ssm_norm_w, m_w_out_ssm, m_lru_conv_w, m_lru_conv_b, m_lru_w_r, m_lru_b_r, m_lru_w_i, m_lru_b_i, m_lru_lambda, m_w_out_lru, m_w_out, m_norm2_w, m_w_ffn_in, m_w_ffn_out, m_norm_f_w, v_norm1_w, v_w_in, v_b_branch_gate, v_ssm_conv_w, v_ssm_conv_b, v_ssm_dt_bias, v_ssm_a_log, v_ssm_d, v_ssm_norm_w, v_w_out_ssm, v_lru_conv_w, v_lru_conv_b, v_lru_w_r, v_lru_b_r, v_lru_w_i, v_lru_b_i, v_lru_lambda, v_w_out_lru, v_w_out, v_norm2_w, v_w_ffn_in, v_w_ffn_out, v_norm_f_w):
    given = dict(locals())
    weights = {n: given[n] for n in BIG_NAMES + SMALL_VECTORS + SMALL_MATRICES}
    t = x.shape[1]
    xt = x[0]
    tgt = loss_target[0]

    def local(n, a):
        return a[0].T if n in TRANSPOSED else a[0]

    def as_output(n, a):
        return a.T[None] if n in TRANSPOSED else a[None]

    def shard(n):
        s = local(n, weights[n])
        return s.astype(BF16) if n in MATMUL_NAMES else s

    def unshard(n, g):
        return _from_col_shards(g) if n in CONV_NAMES else g.reshape(-1, g.shape[-1])

    def grad_slices(n):
        g = grads[n]
        return (_col_shards(g) if n in CONV_NAMES else g.reshape(N_DEV, -1, g.shape[-1])).astype(BF16)

    gathered = _all_gather([shard(n) for n in NEEDED_FIRST], "gather_in_weights")
    full = {n: unshard(n, g) for n, g in zip(NEEDED_FIRST, gathered)}
    ssm_cw, lru_cw = full["ssm_conv_w"], full["lru_conv_w"]
    wi_t = full["w_in"]
    lru_rows = wi_t[ORIG_LRU_X:].reshape(2, LRU_BLOCKS, LANES, D_MODEL).transpose(1, 0, 2, 3)
    w_tail = jnp.concatenate([wi_t[W_IN_MAIN:ORIG_DT], lru_rows.reshape(2 * LRU_WIDTH, D_MODEL),
                              wi_t[ORIG_DT:ORIG_LRU_X], jnp.zeros((PROJ_W - IN_PROJ, D_MODEL), BF16)], axis=0)

    def pad_heads(a):
        return jnp.pad(a.reshape(1, SSM_HEADS), ((0, 0), (0, LANES - SSM_HEADS)))

    dt_bias_p = pad_heads(ssm_dt_bias)
    a_log_p = pad_heads(ssm_a_log)
    d_exp = jnp.repeat(ssm_d.reshape(SSM_HEADS), SSM_HEAD_DIM).reshape(1, SSM_INNER)
    lru_wr, lru_wi = lru_w_r[0], lru_w_i[0]

    hn1 = _rmsnorm_fwd(xt, norm1_w, "norm1_fwd")
    later = [shard(n) for n in GATHERED_IN_PROJ]
    proj, (gathered,) = _mm(
        hn1, wi_t, tb=True, b_tail=w_tail, b_main=W_IN_MAIN, name="in_proj",
        hosted=[(_gather_hosted, later, [jax.ShapeDtypeStruct((N_DEV,) + v.shape, v.dtype) for v in later])])
    full.update({n: unshard(n, g) for n, g in zip(GATHERED_IN_PROJ, gathered)})
    xbc_act = _ssm_conv_fwd(proj, ssm_cw, ssm_conv_b)
    (y_ssd, s_in_all), gathered = _ssd_fwd(xbc_act, proj, dt_bias_p, a_log_p, d_exp,
                                           gather=[shard(n) for n in GATHERED_IN_SSD])
    full.update({n: unshard(n, g) for n, g in zip(GATHERED_IN_SSD, gathered)})
    (l_out, h_lru), gathered = _lru_fwd(proj, lru_cw, lru_conv_b, lru_wr, lru_b_r, lru_wi, lru_b_i, lru_lambda,
                                        gather=[shard(n) for n in GATHERED_IN_LRU])
    full.update({n: unshard(n, g) for n, g in zip(GATHERED_IN_LRU, gathered)})
    y_pre = _ssm_post_fwd(y_ssd, proj, ssm_norm_w)
    y_ssm = _mm(y_pre, full["w_out_ssm"], out_dtype=BF16, name="out_ssm")
    y_lru = _mm(l_out, full["w_out_lru"], out_dtype=BF16, name="out_lru")
    merged = _merge_fwd(proj, b_branch_gate, y_ssm, y_lru)
    h1, hn2 = _mm(merged, full["w_out"], epilogue=_residual_norm_epilogue(xt, norm2_w), name="out_proj")
    gate, up, act = _ffn_in_swiglu(hn2, full["w_ffn_in"])

    grads = {}
    dh2, grads["norm_f_w"], loss_cols = _mm(
        act, full["w_ffn_out"], epilogue=_loss_epilogue(h1, norm_f_w.reshape(1, D_MODEL), tgt), name="ffn_out")
    loss = lax.psum(0.5 * jnp.sum(loss_cols) / D_MODEL, AXES)
    dact = _mm(dh2, full["w_ffn_out"], tb=True, out_dtype=BF16, name="d_act")
    grads["w_ffn_out"] = _mm(act, dh2, ta=True, out_dtype=BF16, name="dw_ffn_out")
    dgu = _swiglu_bwd(gate, up, dact)
    dh1, grads["norm2_w"] = _mm(dgu, full["w_ffn_in"], epilogue=_norm_bwd_epilogue(h1, norm2_w, dh2), name="d_hn2")
    grads["w_ffn_in"] = _mm(dgu, hn2, ta=True, out_dtype=BF16, name="dw_ffn_in")
    dmerged = _mm(dh1, full["w_out"], tb=True, out_dtype=BF16, name="d_merged")
    grads["w_out"] = _mm(merged, dh1, ta=True, out_dtype=BF16, name="dw_out")
    dy_ssm, dy_lru, dproj, grads["b_branch_gate"] = _merge_bwd(proj, b_branch_gate, y_ssm, y_lru, dmerged)
    dy_pre = _mm(dy_ssm, full["w_out_ssm"], tb=True, out_dtype=BF16, name="d_y_pre")
    grads["w_out_ssm"] = _mm(y_pre, dy_ssm, ta=True, out_dtype=BF16, name="dw_out_ssm")
    dl_out = _mm(dy_lru, full["w_out_lru"], tb=True, out_dtype=BF16, name="d_l_out")
    grads["w_out_lru"] = _mm(l_out, dy_lru, ta=True, out_dtype=BF16, name="dw_out_lru")
    dy_ssd, dproj, grads["ssm_norm_w"] = _ssm_post_bwd(y_ssd, proj, ssm_norm_w, dy_pre, dproj)
    (dxbc_act, dproj, dbias, dalog, ddcol), recv_in_ssd = _ssd_bwd(
        xbc_act, proj, s_in_all, dy_ssd, dt_bias_p, a_log_p, d_exp, dproj,
        exchange=[grad_slices(n) for n in EXCHANGED_IN_SSD])
    grads["ssm_dt_bias"] = dbias[:, :SSM_HEADS]
    grads["ssm_a_log"] = dalog[:, :SSM_HEADS]
    grads["ssm_d"] = ddcol.reshape(SSM_HEADS, SSM_HEAD_DIM).sum(axis=1).reshape(1, SSM_HEADS)
    dproj, grads["ssm_conv_w"], grads["ssm_conv_b"] = _ssm_conv_bwd(proj, ssm_cw, ssm_conv_b, dxbc_act, dproj)
    ((dproj, grads["lru_conv_w"], grads["lru_conv_b"], dwr, grads["lru_b_r"], dwi, grads["lru_b_i"],
      grads["lru_lambda"]), recv_in_lru) = _lru_bwd(
        proj, lru_cw, lru_conv_b, lru_wr, lru_b_r, lru_wi, lru_b_i, lru_lambda, h_lru, dl_out, dproj,
        exchange=[grad_slices(n) for n in EXCHANGED_IN_LRU])
    grads["lru_w_r"], grads["lru_w_i"] = dwr[None], dwi[None]
    dwpt = _mm(dproj, hn1, ta=True, out_dtype=BF16, name="dw_in")
    lru_rows = dwpt[COL_LRU:COL_DT].reshape(LRU_BLOCKS, 2, LANES, D_MODEL).transpose(1, 0, 2, 3)
    grads["w_in"] = jnp.concatenate([dwpt[:ORIG_DT], dwpt[COL_DT:COL_DT + SSM_HEADS],
                                     lru_rows.reshape(2 * LRU_WIDTH, D_MODEL)], axis=0)
    w_in_parts = grad_slices("w_in")
    from_sibling = _sibling_exchange(w_in_parts, "sibling_exchange_dw_in")
    chip_parts = [_pair_sum(w_in_parts, from_sibling, "pair_sum_dw_in")]
    direct = [grad_slices(n) for n in CONV_NAMES] + [grads[n].reshape(N_DEV, -1, LANES) for n in SMALL_MATRICES]
    (grad_x, grads["norm1_w"]), (recv_w_in, recv_direct) = _mm(
        dproj, wi_t, b_tail=w_tail, b_main=W_IN_MAIN, epilogue=_norm_bwd_epilogue(xt, norm1_w, dh1), name="d_hn1",
        hosted=[(_chip_exchange_phase, chip_parts, chip_parts), (_exchange_phase, direct, direct)])
    recv_first = list(recv_w_in) + list(recv_direct[:len(CONV_NAMES)])
    recv_mats = recv_direct[len(CONV_NAMES):]

    def as_rows(n, a):
        return a.reshape(1, -1) if n in SMALL_VECTORS else a.reshape(-1, LANES)

    vec_g = jnp.concatenate([jnp.pad(as_rows(n, grads[n]), ((0, 0), (0, (-grads[n].size) % LANES)))
                             for n in SMALL_VECTORS], axis=1)
    gathered = _all_gather([vec_g] + list(_sum_sources(recv_mats, "sum_gate_matrix_grads")), "gather_small_grads")
    vec_parts = gathered[0].reshape(N_DEV, -1)
    mat_g = [g.reshape(-1, LANES) for g in gathered[1:]]

    recv = dict(zip(EXCHANGED_IN_SSD + EXCHANGED_IN_LRU + NEEDED_FIRST,
                    list(recv_in_ssd) + list(recv_in_lru) + list(recv_first)))
    big_out = {n: _adamw(local(n, weights[n]), recv[n], local(n, given["m_" + n]), local(n, given["v_" + n]),
                         "adamw_" + n) for n in BIG_NAMES}

    def state(n):
        return tuple(as_rows(n, given[p + n]) for p in ("", "m_", "v_"))

    vec_out, mat_out = _adamw_small(vec_parts, [state(n) for n in SMALL_VECTORS],
                                    mat_g, [state(n) for n in SMALL_MATRICES])
    small_out = dict(zip(SMALL_VECTORS, vec_out))
    small_out.update({n: (g,) + out for n, g, out in zip(SMALL_MATRICES, mat_g, mat_out)})

    order = list(given)[1:24]
    results = []
    for q in range(4):
        vals = {n: as_output(n, big_out[n][q]) for n in BIG_NAMES}
        vals.update({n: out[q].reshape(weights[n].shape) for n, out in small_out.items()})
        results.extend(vals[n] for n in order)
    return (loss, grad_x[None], *results)
```
